```python
import math
import numpy as np
import jax
import jax.numpy as jnp
from jax import lax

D_MODEL = 2048
BATCH = 8
SEQ = 8192
DEPTH = 2

GRID_W = 64
CTX_LEN = 256
HEAD_DIM = 128
NA_HEADS = 8
NB_Q_HEADS = 8
NB_KV_HEADS = 2
NA_ROWS = 8
NA_COLS = 16
NA_QCOLS = 16
NA_KCOLS = NA_QCOLS + NA_COLS
SW_RADIUS = 128
SW_BLOCK = 128
ROPE_BASE = 10000.0
SSM_GROUP = 16
SSM_GROUPS = D_MODEL // SSM_GROUP
SSM_STATE = 64
D_FF = -(-8 * D_MODEL // (3 * 256)) * 256
A_WIDTH = NA_HEADS * HEAD_DIM
B_Q_WIDTH = NB_Q_HEADS * HEAD_DIM
B_KV_WIDTH = NB_KV_HEADS * HEAD_DIM
IN_WIDTH = 3 * A_WIDTH + B_Q_WIDTH + 2 * B_KV_WIDTH
MIX_WIDTH = A_WIDTH + B_Q_WIDTH
IN_SPLITS = (A_WIDTH, 2 * A_WIDTH, 3 * A_WIDTH, 3 * A_WIDTH + B_Q_WIDTH, 3 * A_WIDTH + B_Q_WIDTH + B_KV_WIDTH)
EPS = 1e-6
NEG_INF = -1e30

kernel_name = 'hybrid_natten_swa_s5_prefix_dit'


def rms_norm(x, g):
    xf = x.astype(jnp.float32)
    y = xf * lax.rsqrt(jnp.mean(xf * xf, axis=-1, keepdims=True) + EPS)
    return (y * g.astype(jnp.float32)).astype(x.dtype)


def ada_mod(cvec, w, b):
    return jnp.split(jax.nn.silu(cvec) @ w + b, 6, axis=-1)


def swiglu(h, w1, w3, w2):
    return (jax.nn.silu(h @ w1) * (h @ w3)) @ w2


def axial_rope(x, pos_r, pos_c):
    half = x.shape[-1] // 2
    quarter = half // 2
    inv_freq = ROPE_BASE ** (-jnp.arange(quarter, dtype=jnp.float32) / quarter)
    xf = x.astype(jnp.float32)

    def rotate(xa, pos):
        ang = pos[:, None] * inv_freq[None, :]
        cos = jnp.cos(ang)[None, :, None, :]
        sin = jnp.sin(ang)[None, :, None, :]
        x1, x2 = xa[..., :quarter], xa[..., quarter:]
        return jnp.concatenate([x1 * cos - x2 * sin, x2 * cos + x1 * sin], axis=-1)

    return jnp.concatenate([rotate(xf[..., :half], pos_r), rotate(xf[..., half:], pos_c)], axis=-1).astype(x.dtype)


def context_attention(q, k, v, sink):
    b, m, hq, d = q.shape
    hkv = k.shape[2]
    grp = hq // hkv
    qg = q.reshape(b, m, hkv, grp, d)
    s = jnp.einsum('bqhgd,bkhd->bhgqk', qg, k, preferred_element_type=jnp.float32) * (d ** -0.5)
    if sink is not None:
        s_sink = jnp.broadcast_to(sink.astype(jnp.float32).reshape(1, hkv, grp, 1, 1), s.shape[:-1] + (1,))
        s = jnp.concatenate([s, s_sink], axis=-1)
    p = jax.nn.softmax(s, axis=-1)[..., :m].astype(v.dtype)
    return jnp.einsum('bhgqk,bkhd->bqhgd', p, v).reshape(b, m, hq, d)


def neighbourhood_attention(q, k, v, k_ctx, v_ctx, rpb):
    b, seq, h, d = q.shape
    rows = seq // GRID_W
    kh = min(NA_ROWS, rows)
    ncb = GRID_W // NA_QCOLS
    r = jnp.arange(rows)
    key_rows = jnp.clip(r - kh // 2, 0, rows - kh)[:, None] + jnp.arange(kh)[None, :]
    blk = jnp.arange(ncb)
    key_cols = (jnp.clip(blk * NA_QCOLS - NA_COLS // 2, 0, GRID_W - NA_KCOLS)[:, None]
                + jnp.arange(NA_KCOLS)[None, :])
    q_cols = blk[:, None] * NA_QCOLS + jnp.arange(NA_QCOLS)[None, :]
    win_start = jnp.clip(q_cols - NA_COLS // 2, 0, GRID_W - NA_COLS)[..., None]
    kc_b = key_cols[:, None, :]
    col_valid = (kc_b >= win_start) & (kc_b < win_start + NA_COLS)
    row_idx = key_rows - r[:, None] + NA_ROWS - 1
    col_idx = jnp.clip(kc_b - q_cols[..., None] + NA_COLS - 1, 0, 2 * NA_COLS - 2)
    bias = rpb.astype(jnp.float32)[:, row_idx[:, None, None, :, None], col_idx[None, :, :, None, :]]

    gather_r = key_rows[:, None, :, None]
    gather_c = key_cols[None, :, None, :]
    kg = k.reshape(b, rows, GRID_W, h, d)[:, gather_r, gather_c]
    vg = v.reshape(b, rows, GRID_W, h, d)[:, gather_r, gather_c]
    qg = q.reshape(b, rows, ncb, NA_QCOLS, h, d)
    scale = d ** -0.5
    n_loc = kh * NA_KCOLS
    s_loc = jnp.einsum('brnqhd,brnikhd->bhrnqik', qg, kg, preferred_element_type=jnp.float32) * scale + bias[None]
    s_loc = jnp.where(col_valid[:, :, None, :], s_loc, NEG_INF).reshape(b, h, rows, ncb, NA_QCOLS, n_loc)
    s_ctx = jnp.einsum('brnqhd,bmhd->bhrnqm', qg, k_ctx, preferred_element_type=jnp.float32) * scale
    p = jax.nn.softmax(jnp.concatenate([s_loc, s_ctx], axis=-1), axis=-1).astype(v.dtype)
    p_loc = p[..., :n_loc].reshape(b, h, rows, ncb, NA_QCOLS, kh, NA_KCOLS)
    o = (jnp.einsum('bhrnqik,brnikhd->brnqhd', p_loc, vg)
         + jnp.einsum('bhrnqm,bmhd->brnqhd', p[..., n_loc:], v_ctx))
    return o.reshape(b, seq, h, d)


def window_attention(q, k, v, k_ctx, v_ctx, sink):
    b, seq, hq, d = q.shape
    hkv = k.shape[2]
    grp = hq // hkv
    nb = seq // SW_BLOCK
    qb = q.reshape(b, nb, SW_BLOCK, hkv, grp, d)

    def band(t):
        tb = jnp.pad(t, ((0, 0), (SW_BLOCK, SW_BLOCK), (0, 0), (0, 0))).reshape(b, nb + 2, SW_BLOCK, hkv, d)
        return jnp.concatenate([tb[:, :-2], tb[:, 1:-1], tb[:, 2:]], axis=2)

    kb, vb = band(k), band(v)
    blocks = jnp.arange(nb)[:, None]
    q_pos = blocks * SW_BLOCK + jnp.arange(SW_BLOCK)[None, :]
    k_pos = (blocks - 1) * SW_BLOCK + jnp.arange(3 * SW_BLOCK)[None, :]
    k_pos_b = k_pos[:, None, :]
    valid = (jnp.abs(k_pos_b - q_pos[:, :, None]) <= SW_RADIUS) & (k_pos_b >= 0) & (k_pos_b < seq)
    scale = d ** -0.5
    s_loc = jnp.einsum('bnqhgd,bnkhd->bhgnqk', qb, kb, preferred_element_type=jnp.float32) * scale
    s_loc = jnp.where(valid, s_loc, NEG_INF)
    s_ctx = jnp.einsum('bnqhgd,bmhd->bhgnqm', qb, k_ctx, preferred_element_type=jnp.float32) * scale
    s_sink = jnp.broadcast_to(sink.astype(jnp.float32).reshape(1, hkv, grp, 1, 1, 1), s_loc.shape[:-1] + (1,))
    p = jax.nn.softmax(jnp.concatenate([s_loc, s_ctx, s_sink], axis=-1), axis=-1).astype(v.dtype)
    n_loc = 3 * SW_BLOCK
    m = k_ctx.shape[1]
    o = (jnp.einsum('bhgnqk,bnkhd->bnqhgd', p[..., :n_loc], vb)
         + jnp.einsum('bhgnqm,bmhd->bnqhgd', p[..., n_loc:n_loc + m], v_ctx))
    return o.reshape(b, seq, hq, d)


def hybrid_attention(hx, hc, w_in, w_out, rpb, sink, pos_r, pos_c, need_ctx):
    def project(h):
        b, t, _ = h.shape
        parts = jnp.split(h @ w_in, IN_SPLITS, axis=-1)
        return [z.reshape(b, t, -1, HEAD_DIM) for z in parts]

    b, seq, _ = hx.shape
    qa, ka, va, qb, kb, vb = project(hx)
    qa_c, ka_c, va_c, qb_c, kb_c, vb_c = project(hc)
    oa = neighbourhood_attention(qa, ka, va, ka_c, va_c, rpb)
    ob = window_attention(axial_rope(qb, pos_r, pos_c), axial_rope(kb, pos_r, pos_c), vb, kb_c, vb_c, sink)
    yx = jnp.concatenate([oa.reshape(b, seq, A_WIDTH), ob.reshape(b, seq, B_Q_WIDTH)], axis=-1) @ w_out
    if not need_ctx:
        return yx, None
    m = hc.shape[1]
    oa_c = context_attention(qa_c, ka_c, va_c, None)
    ob_c = context_attention(qb_c, kb_c, vb_c, sink)
    yc = jnp.concatenate([oa_c.reshape(b, m, A_WIDTH), ob_c.reshape(b, m, B_Q_WIDTH)], axis=-1) @ w_out
    return yx, yc


def s5_discretise(a_re, a_im, log_dt, b_re, b_im):
    ar = a_re.astype(jnp.float32)
    ai = a_im.astype(jnp.float32)
    dt = jnp.exp(log_dt.astype(jnp.float32))[:, None]
    mag = jnp.exp(ar * dt)
    lam_r, lam_i = mag * jnp.cos(ai * dt), mag * jnp.sin(ai * dt)
    den = ar * ar + ai * ai
    nr = lam_r - 1.0
    coef_r = (nr * ar + lam_i * ai) / den
    coef_i = (lam_i * ar - nr * ai) / den
    br, bi = b_re.astype(jnp.float32), b_im.astype(jnp.float32)
    bbar_r = coef_r[..., None] * br - coef_i[..., None] * bi
    bbar_i = coef_r[..., None] * bi + coef_i[..., None] * br
    return lam_r, lam_i, bbar_r, bbar_i


def complex_diag_scan(lam_r, lam_i, u_r, u_i, reverse):
    a_r = jnp.broadcast_to(lam_r, u_r.shape)
    a_i = jnp.broadcast_to(lam_i, u_i.shape)

    def combine(e1, e2):
        a1r, a1i, b1r, b1i = e1
        a2r, a2i, b2r, b2i = e2
        return (a2r * a1r - a2i * a1i, a2r * a1i + a2i * a1r,
                a2r * b1r - a2i * b1i + b2r, a2r * b1i + a2i * b1r + b2i)

    return lax.associative_scan(combine, (a_r, a_i, u_r, u_i), reverse=reverse, axis=1)


def s5_direction(ux, uc, a_re, a_im, log_dt, b_re, b_im, c_re, c_im, reverse, need_ctx):
    lam_r, lam_i, bbar_r, bbar_i = s5_discretise(a_re, a_im, log_dt, b_re, b_im)
    cr, ci = c_re.astype(jnp.float32), c_im.astype(jnp.float32)

    def drive(u):
        ug = u.astype(jnp.float32).reshape(u.shape[0], u.shape[1], SSM_GROUPS, SSM_GROUP)
        return jnp.einsum('btgh,gph->btgp', ug, bbar_r), jnp.einsum('btgh,gph->btgp', ug, bbar_i)

    def readout(s_r, s_i):
        y = jnp.einsum('btgp,ghp->btgh', s_r, cr) - jnp.einsum('btgp,ghp->btgh', s_i, ci)
        return y.reshape(y.shape[0], y.shape[1], -1)

    uc_r, uc_i = drive(uc)
    _, _, sc_r, sc_i = complex_diag_scan(lam_r, lam_i, uc_r, uc_i, reverse)
    end = 0 if reverse else -1
    s0_r, s0_i = sc_r[:, end][:, None], sc_i[:, end][:, None]
    ux_r, ux_i = drive(ux)
    pw_r, pw_i, sx_r, sx_i = complex_diag_scan(lam_r, lam_i, ux_r, ux_i, reverse)
    sx_r = sx_r + pw_r * s0_r - pw_i * s0_i
    sx_i = sx_i + pw_r * s0_i + pw_i * s0_r
    y_ctx = readout(sc_r, sc_i) if need_ctx else None
    return readout(sx_r, sx_i), y_ctx


def s5_glu_mixer(ux, uc, a_re, a_im, log_dt, b_re, b_im, c_re, c_im, d_skip, w_glu, b_glu, need_ctx):
    d = d_skip.astype(jnp.float32)
    y_x = d * ux.astype(jnp.float32)
    y_c = d * uc.astype(jnp.float32) if need_ctx else None
    for direction in range(2):
        yx_d, yc_d = s5_direction(ux, uc, a_re[direction], a_im[direction], log_dt[direction],
                                  b_re[direction], b_im[direction], c_re[direction], c_im[direction],
                                  direction == 1, need_ctx)
        y_x = y_x + yx_d
        if need_ctx:
            y_c = y_c + yc_d

    def glu(y, dtype):
        z = jax.nn.gelu(y).astype(dtype) @ w_glu + b_glu
        val, gate = jnp.split(z, 2, axis=-1)
        return val * jax.nn.sigmoid(gate)

    return glu(y_x, ux.dtype), (glu(y_c, uc.dtype) if need_ctx else None)


def _fwd_setup_inputs(seed: int = 0) -> dict:
    key = jax.random.key(seed)
    keys = iter(jax.random.split(key, 32))
    f32 = jnp.float32

    def normal(shape, scale):
        return jax.random.normal(next(keys), shape, f32) * scale

    d, g, h, p = D_MODEL, SSM_GROUPS, SSM_GROUP, SSM_STATE
    n_attn, n_ssm = (DEPTH + 1) // 2, DEPTH // 2
    a_im0 = math.pi * jnp.arange(p, dtype=f32)
    return {
        'x': normal((BATCH, SEQ, d), 1.0),
        'c': normal((BATCH, d), 1.0),
        'ctx': normal((BATCH, CTX_LEN, d), 1.0),
        'c_ctx': normal((d,), 1.0),
        'ada_w': normal((DEPTH, d, 6 * d), 0.5 * d ** -0.5),
        'ada_b': normal((DEPTH, 6 * d), 0.01),
        'norm_mix': 1.0 + normal((DEPTH, d), 0.02),
        'norm_ffn': 1.0 + normal((DEPTH, d), 0.02),
        'ffn_w1': normal((DEPTH, d, D_FF), d ** -0.5),
        'ffn_w3': normal((DEPTH, d, D_FF), d ** -0.5),
        'ffn_w2': normal((DEPTH, D_FF, d), D_FF ** -0.5),
        'attn_w_in': normal((n_attn, d, IN_WIDTH), d ** -0.5),
        'attn_w_out': normal((n_attn, MIX_WIDTH, d), MIX_WIDTH ** -0.5),
        'attn_rpb': normal((n_attn, NA_HEADS, 2 * NA_ROWS - 1, 2 * NA_COLS - 1), 0.02),
        'attn_sink': normal((n_attn, NB_Q_HEADS), 0.5),
        'ssm_a_re': -0.5 + normal((n_ssm, 2, g, p), 0.01),
        'ssm_a_im': a_im0 + normal((n_ssm, 2, g, p), 0.01),
        'ssm_log_dt': jax.random.uniform(next(keys), (n_ssm, 2, g), f32, math.log(1e-3), math.log(1e-1)),
        'ssm_b_re': normal((n_ssm, 2, g, p, h), (2 * h) ** -0.5),
        'ssm_b_im': normal((n_ssm, 2, g, p, h), (2 * h) ** -0.5),
        'ssm_c_re': normal((n_ssm, 2, g, h, p), p ** -0.5),
        'ssm_c_im': normal((n_ssm, 2, g, h, p), p ** -0.5),
        'ssm_d': normal((n_ssm, d), 0.5),
        'ssm_w_glu': normal((n_ssm, d, 2 * d), d ** -0.5),
        'ssm_b_glu': normal((n_ssm, 2 * d), 0.01),
        'norm_final': 1.0 + normal((d,), 0.02),
    }


def _fwd_reference(x, c, ctx, c_ctx, ada_w, ada_b, norm_mix, norm_ffn, ffn_w1, ffn_w3, ffn_w2,
              attn_w_in, attn_w_out, attn_rpb, attn_sink,
              ssm_a_re, ssm_a_im, ssm_log_dt, ssm_b_re, ssm_b_im, ssm_c_re, ssm_c_im,
              ssm_d, ssm_w_glu, ssm_b_glu, norm_final):
    seq = x.shape[1]
    t = jnp.arange(seq)
    pos_r = (t // GRID_W).astype(jnp.float32)
    pos_c = (t % GRID_W).astype(jnp.float32)
    for layer in range(DEPTH):
        need_ctx = layer < DEPTH - 1
        i = layer // 2
        sh1, sc1, g1, sh2, sc2, g2 = [m[:, None, :] for m in ada_mod(c, ada_w[layer], ada_b[layer])]
        csh1, csc1, cg1, csh2, csc2, cg2 = ada_mod(c_ctx, ada_w[layer], ada_b[layer])
        hx = rms_norm(x, norm_mix[layer]) * (1.0 + sc1) + sh1
        hc = rms_norm(ctx, norm_mix[layer]) * (1.0 + csc1) + csh1
        if layer % 2 == 0:
            yx, yc = hybrid_attention(hx, hc, attn_w_in[i], attn_w_out[i], attn_rpb[i], attn_sink[i],
                                      pos_r, pos_c, need_ctx)
        else:
            yx, yc = s5_glu_mixer(hx, hc, ssm_a_re[i], ssm_a_im[i], ssm_log_dt[i], ssm_b_re[i], ssm_b_im[i],
                                  ssm_c_re[i], ssm_c_im[i], ssm_d[i], ssm_w_glu[i], ssm_b_glu[i], need_ctx)
        x = x + g1 * yx
        x = x + g2 * swiglu(rms_norm(x, norm_ffn[layer]) * (1.0 + sc2) + sh2,
                            ffn_w1[layer], ffn_w3[layer], ffn_w2[layer])
        if need_ctx:
            ctx = ctx + cg1 * yc
            ctx = ctx + cg2 * swiglu(rms_norm(ctx, norm_ffn[layer]) * (1.0 + csc2) + csh2,
                                     ffn_w1[layer], ffn_w3[layer], ffn_w2[layer])
    return rms_norm(x, norm_final)


import jax as _jax
import jax.numpy as _jnp

TWIN_FORMAT = 'train_step'
FWD_PARAMS = ['x', 'c', 'ctx', 'c_ctx', 'ada_w', 'ada_b', 'norm_mix', 'norm_ffn', 'ffn_w1', 'ffn_w3', 'ffn_w2', 'attn_w_in', 'attn_w_out', 'attn_rpb', 'attn_sink', 'ssm_a_re', 'ssm_a_im', 'ssm_log_dt', 'ssm_b_re', 'ssm_b_im', 'ssm_c_re', 'ssm_c_im', 'ssm_d', 'ssm_w_glu', 'ssm_b_glu', 'norm_final']
TWIN_WEIGHTS = ['c_ctx', 'ada_w', 'ada_b', 'norm_mix', 'norm_ffn', 'ffn_w1', 'ffn_w3', 'ffn_w2', 'attn_w_in', 'attn_w_out', 'attn_rpb', 'attn_sink', 'ssm_a_re', 'ssm_a_im', 'ssm_log_dt', 'ssm_b_re', 'ssm_b_im', 'ssm_c_re', 'ssm_c_im', 'ssm_d', 'ssm_w_glu', 'ssm_b_glu', 'norm_final']
TWIN_DIFF_INPUT = 'x'
TWIN_INPUTS = ['x', 'c', 'ctx', 'c_ctx', 'ada_w', 'ada_b', 'norm_mix', 'norm_ffn', 'ffn_w1', 'ffn_w3', 'ffn_w2', 'attn_w_in', 'attn_w_out', 'attn_rpb', 'attn_sink', 'ssm_a_re', 'ssm_a_im', 'ssm_log_dt', 'ssm_b_re', 'ssm_b_im', 'ssm_c_re', 'ssm_c_im', 'ssm_d', 'ssm_w_glu', 'ssm_b_glu', 'norm_final', 'loss_target', 'm_c_ctx', 'm_ada_w', 'm_ada_b', 'm_norm_mix', 'm_norm_ffn', 'm_ffn_w1', 'm_ffn_w3', 'm_ffn_w2', 'm_attn_w_in', 'm_attn_w_out', 'm_attn_rpb', 'm_attn_sink', 'm_ssm_a_re', 'm_ssm_a_im', 'm_ssm_log_dt', 'm_ssm_b_re', 'm_ssm_b_im', 'm_ssm_c_re', 'm_ssm_c_im', 'm_ssm_d', 'm_ssm_w_glu', 'm_ssm_b_glu', 'm_norm_final', 'v_c_ctx', 'v_ada_w', 'v_ada_b', 'v_norm_mix', 'v_norm_ffn', 'v_ffn_w1', 'v_ffn_w3', 'v_ffn_w2', 'v_attn_w_in', 'v_attn_w_out', 'v_attn_rpb', 'v_attn_sink', 'v_ssm_a_re', 'v_ssm_a_im', 'v_ssm_log_dt', 'v_ssm_b_re', 'v_ssm_b_im', 'v_ssm_c_re', 'v_ssm_c_im', 'v_ssm_d', 'v_ssm_w_glu', 'v_ssm_b_glu', 'v_norm_final']
TWIN_OUTPUTS = ['loss', 'grad_x', 'grad_c_ctx', 'grad_ada_w', 'grad_ada_b', 'grad_norm_mix', 'grad_norm_ffn', 'grad_ffn_w1', 'grad_ffn_w3', 'grad_ffn_w2', 'grad_attn_w_in', 'grad_attn_w_out', 'grad_attn_rpb', 'grad_attn_sink', 'grad_ssm_a_re', 'grad_ssm_a_im', 'grad_ssm_log_dt', 'grad_ssm_b_re', 'grad_ssm_b_im', 'grad_ssm_c_re', 'grad_ssm_c_im', 'grad_ssm_d', 'grad_ssm_w_glu', 'grad_ssm_b_glu', 'grad_norm_final', 'delta_c_ctx', 'delta_ada_w', 'delta_ada_b', 'delta_norm_mix', 'delta_norm_ffn', 'delta_ffn_w1', 'delta_ffn_w3', 'delta_ffn_w2', 'delta_attn_w_in', 'delta_attn_w_out', 'delta_attn_rpb', 'delta_attn_sink', 'delta_ssm_a_re', 'delta_ssm_a_im', 'delta_ssm_log_dt', 'delta_ssm_b_re', 'delta_ssm_b_im', 'delta_ssm_c_re', 'delta_ssm_c_im', 'delta_ssm_d', 'delta_ssm_w_glu', 'delta_ssm_b_glu', 'delta_norm_final', 'new_m_c_ctx', 'new_m_ada_w', 'new_m_ada_b', 'new_m_norm_mix', 'new_m_norm_ffn', 'new_m_ffn_w1', 'new_m_ffn_w3', 'new_m_ffn_w2', 'new_m_attn_w_in', 'new_m_attn_w_out', 'new_m_attn_rpb', 'new_m_attn_sink', 'new_m_ssm_a_re', 'new_m_ssm_a_im', 'new_m_ssm_log_dt', 'new_m_ssm_b_re', 'new_m_ssm_b_im', 'new_m_ssm_c_re', 'new_m_ssm_c_im', 'new_m_ssm_d', 'new_m_ssm_w_glu', 'new_m_ssm_b_glu', 'new_m_norm_final', 'new_v_c_ctx', 'new_v_ada_w', 'new_v_ada_b', 'new_v_norm_mix', 'new_v_norm_ffn', 'new_v_ffn_w1', 'new_v_ffn_w3', 'new_v_ffn_w2', 'new_v_attn_w_in', 'new_v_attn_w_out', 'new_v_attn_rpb', 'new_v_attn_sink', 'new_v_ssm_a_re', 'new_v_ssm_a_im', 'new_v_ssm_log_dt', 'new_v_ssm_b_re', 'new_v_ssm_b_im', 'new_v_ssm_c_re', 'new_v_ssm_c_im', 'new_v_ssm_d', 'new_v_ssm_w_glu', 'new_v_ssm_b_glu', 'new_v_norm_final']
TWIN_LEAF_KINDS = {'loss': 'loss', 'grad_x': 'grad_x', 'grad_c_ctx': 'grad_w', 'grad_ada_w': 'grad_w', 'grad_ada_b': 'grad_w', 'grad_norm_mix': 'grad_w', 'grad_norm_ffn': 'grad_w', 'grad_ffn_w1': 'grad_w', 'grad_ffn_w3': 'grad_w', 'grad_ffn_w2': 'grad_w', 'grad_attn_w_in': 'grad_w', 'grad_attn_w_out': 'grad_w', 'grad_attn_rpb': 'grad_w', 'grad_attn_sink': 'grad_w', 'grad_ssm_a_re': 'grad_w', 'grad_ssm_a_im': 'grad_w', 'grad_ssm_log_dt': 'grad_w', 'grad_ssm_b_re': 'grad_w', 'grad_ssm_b_im': 'grad_w', 'grad_ssm_c_re': 'grad_w', 'grad_ssm_c_im': 'grad_w', 'grad_ssm_d': 'grad_w', 'grad_ssm_w_glu': 'grad_w', 'grad_ssm_b_glu': 'grad_w', 'grad_norm_final': 'grad_w', 'delta_c_ctx': 'delta_w', 'delta_ada_w': 'delta_w', 'delta_ada_b': 'delta_w', 'delta_norm_mix': 'delta_w', 'delta_norm_ffn': 'delta_w', 'delta_ffn_w1': 'delta_w', 'delta_ffn_w3': 'delta_w', 'delta_ffn_w2': 'delta_w', 'delta_attn_w_in': 'delta_w', 'delta_attn_w_out': 'delta_w', 'delta_attn_rpb': 'delta_w', 'delta_attn_sink': 'delta_w', 'delta_ssm_a_re': 'delta_w', 'delta_ssm_a_im': 'delta_w', 'delta_ssm_log_dt': 'delta_w', 'delta_ssm_b_re': 'delta_w', 'delta_ssm_b_im': 'delta_w', 'delta_ssm_c_re': 'delta_w', 'delta_ssm_c_im': 'delta_w', 'delta_ssm_d': 'delta_w', 'delta_ssm_w_glu': 'delta_w', 'delta_ssm_b_glu': 'delta_w', 'delta_norm_final': 'delta_w', 'new_m_c_ctx': 'new_m', 'new_m_ada_w': 'new_m', 'new_m_ada_b': 'new_m', 'new_m_norm_mix': 'new_m', 'new_m_norm_ffn': 'new_m', 'new_m_ffn_w1': 'new_m', 'new_m_ffn_w3': 'new_m', 'new_m_ffn_w2': 'new_m', 'new_m_attn_w_in': 'new_m', 'new_m_attn_w_out': 'new_m', 'new_m_attn_rpb': 'new_m', 'new_m_attn_sink': 'new_m', 'new_m_ssm_a_re': 'new_m', 'new_m_ssm_a_im': 'new_m', 'new_m_ssm_log_dt': 'new_m', 'new_m_ssm_b_re': 'new_m', 'new_m_ssm_b_im': 'new_m', 'new_m_ssm_c_re': 'new_m', 'new_m_ssm_c_im': 'new_m', 'new_m_ssm_d': 'new_m', 'new_m_ssm_w_glu': 'new_m', 'new_m_ssm_b_glu': 'new_m', 'new_m_norm_final': 'new_m', 'new_v_c_ctx': 'new_v', 'new_v_ada_w': 'new_v', 'new_v_ada_b': 'new_v', 'new_v_norm_mix': 'new_v', 'new_v_norm_ffn': 'new_v', 'new_v_ffn_w1': 'new_v', 'new_v_ffn_w3': 'new_v', 'new_v_ffn_w2': 'new_v', 'new_v_attn_w_in': 'new_v', 'new_v_attn_w_out': 'new_v', 'new_v_attn_rpb': 'new_v', 'new_v_attn_sink': 'new_v', 'new_v_ssm_a_re': 'new_v', 'new_v_ssm_a_im': 'new_v', 'new_v_ssm_log_dt': 'new_v', 'new_v_ssm_b_re': 'new_v', 'new_v_ssm_b_im': 'new_v', 'new_v_ssm_c_re': 'new_v', 'new_v_ssm_c_im': 'new_v', 'new_v_ssm_d': 'new_v', 'new_v_ssm_w_glu': 'new_v', 'new_v_ssm_b_glu': 'new_v', 'new_v_norm_final': 'new_v'}


def _forward(args):
    return _fwd_reference(*[args[k] for k in FWD_PARAMS])


def _output_shape():
    def fwd():
        inp = _fwd_setup_inputs(0)
        return _fwd_reference(*[inp[k] for k in FWD_PARAMS])
    out = _jax.eval_shape(fwd)
    return out.shape, out.dtype

N_MICROBATCH = 1
ADAM_LR = 0.001
ADAM_B1 = 0.9
ADAM_B2 = 0.999
ADAM_EPS = 1e-08
ADAM_WD = 0.01
ADAM_STEP = 10
PER_EXAMPLE_BATCH_AXIS = {'x': 0, 'c': 0, 'ctx': 0, 'loss_target': 0}
SHARED_INPUTS = []
_WEIGHT_DTYPES = {'c_ctx': _jnp.float32, 'ada_w': _jnp.float32, 'ada_b': _jnp.float32, 'norm_mix': _jnp.float32, 'norm_ffn': _jnp.float32, 'ffn_w1': _jnp.float32, 'ffn_w3': _jnp.float32, 'ffn_w2': _jnp.float32, 'attn_w_in': _jnp.float32, 'attn_w_out': _jnp.float32, 'attn_rpb': _jnp.float32, 'attn_sink': _jnp.float32, 'ssm_a_re': _jnp.float32, 'ssm_a_im': _jnp.float32, 'ssm_log_dt': _jnp.float32, 'ssm_b_re': _jnp.float32, 'ssm_b_im': _jnp.float32, 'ssm_c_re': _jnp.float32, 'ssm_c_im': _jnp.float32, 'ssm_d': _jnp.float32, 'ssm_w_glu': _jnp.float32, 'ssm_b_glu': _jnp.float32, 'norm_final': _jnp.float32}
MOMENT_SCALE = {'c_ctx': 8.833936e-03, 'ada_w': 2.696707e-02, 'ada_b': 4.773012e-02, 'norm_mix': 7.494379e-03, 'norm_ffn': 3.652818e-02, 'ffn_w1': 1.621886e-02, 'ffn_w3': 1.568720e-02, 'ffn_w2': 2.604014e-02, 'attn_w_in': 7.901402e-03, 'attn_w_out': 9.690250e-03, 'attn_rpb': 1.511484e-03, 'attn_sink': 3.487495e-04, 'ssm_a_re': 1.288984e-03, 'ssm_a_im': 9.932164e-04, 'ssm_log_dt': 5.194216e-01, 'ssm_b_re': 6.954891e-04, 'ssm_b_im': 7.146346e-04, 'ssm_c_re': 9.933807e-04, 'ssm_c_im': 1.018065e-03, 'ssm_d': 1.416077e-02, 'ssm_w_glu': 4.882031e-03, 'ssm_b_glu': 1.348509e-02, 'norm_final': 3.196059e+01}


def _to_microbatches(a, axis):
    t = _jnp.moveaxis(a, axis, 0)
    t = t.reshape((N_MICROBATCH, t.shape[0] // N_MICROBATCH) + t.shape[1:])
    return _jnp.moveaxis(t, 1, axis + 1)


def setup_inputs(seed: int = 0) -> dict:
    inp = _fwd_setup_inputs(seed)
    key = _jax.random.fold_in(_jax.random.key(seed), 7919)
    shape, _ = _output_shape()
    out = dict(inp)
    out["loss_target"] = _jax.random.normal(_jax.random.fold_in(key, 0), shape, _jnp.float32)
    for i, name in enumerate(TWIN_WEIGHTS):
        w = inp[name].astype(_jnp.float32)
        if MOMENT_SCALE is None:
            s = _jnp.sqrt(_jnp.mean(_jnp.square(w)) + 1e-30)
        else:
            s = MOMENT_SCALE[name]
        km, kv = _jax.random.split(_jax.random.fold_in(key, i + 1))
        out[name] = w
        out["m_" + name] = s * _jax.random.normal(km, w.shape, _jnp.float32)
        out["v_" + name] = (s * s) * _jax.random.uniform(kv, w.shape, _jnp.float32, 0.5, 1.5)
    if N_MICROBATCH > 1:
        for name, axis in PER_EXAMPLE_BATCH_AXIS.items():
            out[name] = _to_microbatches(out[name], axis)
    return {'x': out['x'], 'c': out['c'], 'ctx': out['ctx'], 'c_ctx': out['c_ctx'], 'ada_w': out['ada_w'], 'ada_b': out['ada_b'], 'norm_mix': out['norm_mix'], 'norm_ffn': out['norm_ffn'], 'ffn_w1': out['ffn_w1'], 'ffn_w3': out['ffn_w3'], 'ffn_w2': out['ffn_w2'], 'attn_w_in': out['attn_w_in'], 'attn_w_out': out['attn_w_out'], 'attn_rpb': out['attn_rpb'], 'attn_sink': out['attn_sink'], 'ssm_a_re': out['ssm_a_re'], 'ssm_a_im': out['ssm_a_im'], 'ssm_log_dt': out['ssm_log_dt'], 'ssm_b_re': out['ssm_b_re'], 'ssm_b_im': out['ssm_b_im'], 'ssm_c_re': out['ssm_c_re'], 'ssm_c_im': out['ssm_c_im'], 'ssm_d': out['ssm_d'], 'ssm_w_glu': out['ssm_w_glu'], 'ssm_b_glu': out['ssm_b_glu'], 'norm_final': out['norm_final'], 'loss_target': out['loss_target'], 'm_c_ctx': out['m_c_ctx'], 'm_ada_w': out['m_ada_w'], 'm_ada_b': out['m_ada_b'], 'm_norm_mix': out['m_norm_mix'], 'm_norm_ffn': out['m_norm_ffn'], 'm_ffn_w1': out['m_ffn_w1'], 'm_ffn_w3': out['m_ffn_w3'], 'm_ffn_w2': out['m_ffn_w2'], 'm_attn_w_in': out['m_attn_w_in'], 'm_attn_w_out': out['m_attn_w_out'], 'm_attn_rpb': out['m_attn_rpb'], 'm_attn_sink': out['m_attn_sink'], 'm_ssm_a_re': out['m_ssm_a_re'], 'm_ssm_a_im': out['m_ssm_a_im'], 'm_ssm_log_dt': out['m_ssm_log_dt'], 'm_ssm_b_re': out['m_ssm_b_re'], 'm_ssm_b_im': out['m_ssm_b_im'], 'm_ssm_c_re': out['m_ssm_c_re'], 'm_ssm_c_im': out['m_ssm_c_im'], 'm_ssm_d': out['m_ssm_d'], 'm_ssm_w_glu': out['m_ssm_w_glu'], 'm_ssm_b_glu': out['m_ssm_b_glu'], 'm_norm_final': out['m_norm_final'], 'v_c_ctx': out['v_c_ctx'], 'v_ada_w': out['v_ada_w'], 'v_ada_b': out['v_ada_b'], 'v_norm_mix': out['v_norm_mix'], 'v_norm_ffn': out['v_norm_ffn'], 'v_ffn_w1': out['v_ffn_w1'], 'v_ffn_w3': out['v_ffn_w3'], 'v_ffn_w2': out['v_ffn_w2'], 'v_attn_w_in': out['v_attn_w_in'], 'v_attn_w_out': out['v_attn_w_out'], 'v_attn_rpb': out['v_attn_rpb'], 'v_attn_sink': out['v_attn_sink'], 'v_ssm_a_re': out['v_ssm_a_re'], 'v_ssm_a_im': out['v_ssm_a_im'], 'v_ssm_log_dt': out['v_ssm_log_dt'], 'v_ssm_b_re': out['v_ssm_b_re'], 'v_ssm_b_im': out['v_ssm_b_im'], 'v_ssm_c_re': out['v_ssm_c_re'], 'v_ssm_c_im': out['v_ssm_c_im'], 'v_ssm_d': out['v_ssm_d'], 'v_ssm_w_glu': out['v_ssm_w_glu'], 'v_ssm_b_glu': out['v_ssm_b_glu'], 'v_norm_final': out['v_norm_final']}


def _loss(weights, diff, rest, loss_target):
    with _jax.named_scope("forward"):
        args = {**rest, TWIN_DIFF_INPUT: diff, **{k: w.astype(_WEIGHT_DTYPES[k]) for k, w in weights.items()}}
        y = _forward(args)
    with _jax.named_scope("loss_head"):
        err = _jnp.square(y.astype(_jnp.float32) - loss_target)
        return 0.5 * _jnp.sum(_jnp.mean(err, axis=-1)) if err.ndim else 0.5 * err


def _adamw(w, g, m, v):
    m = ADAM_B1 * m + (1.0 - ADAM_B1) * g
    v = ADAM_B2 * v + (1.0 - ADAM_B2) * _jnp.square(g)
    m_hat = m / (1.0 - ADAM_B1 ** ADAM_STEP)
    v_hat = v / (1.0 - ADAM_B2 ** ADAM_STEP)
    delta = -ADAM_LR * (m_hat / (_jnp.sqrt(v_hat) + ADAM_EPS) + ADAM_WD * w)
    return delta, m, v


def reference(x, c, ctx, c_ctx, ada_w, ada_b, norm_mix, norm_ffn, ffn_w1, ffn_w3, ffn_w2, attn_w_in, attn_w_out, attn_rpb, attn_sink, ssm_a_re, ssm_a_im, ssm_log_dt, ssm_b_re, ssm_b_im, ssm_c_re, ssm_c_im, ssm_d, ssm_w_glu, ssm_b_glu, norm_final, loss_target, m_c_ctx, m_ada_w, m_ada_b, m_norm_mix, m_norm_ffn, m_ffn_w1, m_ffn_w3, m_ffn_w2, m_attn_w_in, m_attn_w_out, m_attn_rpb, m_attn_sink, m_ssm_a_re, m_ssm_a_im, m_ssm_log_dt, m_ssm_b_re, m_ssm_b_im, m_ssm_c_re, m_ssm_c_im, m_ssm_d, m_ssm_w_glu, m_ssm_b_glu, m_norm_final, v_c_ctx, v_ada_w, v_ada_b, v_norm_mix, v_norm_ffn, v_ffn_w1, v_ffn_w3, v_ffn_w2, v_attn_w_in, v_attn_w_out, v_attn_rpb, v_attn_sink, v_ssm_a_re, v_ssm_a_im, v_ssm_log_dt, v_ssm_b_re, v_ssm_b_im, v_ssm_c_re, v_ssm_c_im, v_ssm_d, v_ssm_w_glu, v_ssm_b_glu, v_norm_final):
    given = dict(x=x, c=c, ctx=ctx, c_ctx=c_ctx, ada_w=ada_w, ada_b=ada_b, norm_mix=norm_mix, norm_ffn=norm_ffn, ffn_w1=ffn_w1, ffn_w3=ffn_w3, ffn_w2=ffn_w2, attn_w_in=attn_w_in, attn_w_out=attn_w_out, attn_rpb=attn_rpb, attn_sink=attn_sink, ssm_a_re=ssm_a_re, ssm_a_im=ssm_a_im, ssm_log_dt=ssm_log_dt, ssm_b_re=ssm_b_re, ssm_b_im=ssm_b_im, ssm_c_re=ssm_c_re, ssm_c_im=ssm_c_im, ssm_d=ssm_d, ssm_w_glu=ssm_w_glu, ssm_b_glu=ssm_b_glu, norm_final=norm_final, loss_target=loss_target, m_c_ctx=m_c_ctx, m_ada_w=m_ada_w, m_ada_b=m_ada_b, m_norm_mix=m_norm_mix, m_norm_ffn=m_norm_ffn, m_ffn_w1=m_ffn_w1, m_ffn_w3=m_ffn_w3, m_ffn_w2=m_ffn_w2, m_attn_w_in=m_attn_w_in, m_attn_w_out=m_attn_w_out, m_attn_rpb=m_attn_rpb, m_attn_sink=m_attn_sink, m_ssm_a_re=m_ssm_a_re, m_ssm_a_im=m_ssm_a_im, m_ssm_log_dt=m_ssm_log_dt, m_ssm_b_re=m_ssm_b_re, m_ssm_b_im=m_ssm_b_im, m_ssm_c_re=m_ssm_c_re, m_ssm_c_im=m_ssm_c_im, m_ssm_d=m_ssm_d, m_ssm_w_glu=m_ssm_w_glu, m_ssm_b_glu=m_ssm_b_glu, m_norm_final=m_norm_final, v_c_ctx=v_c_ctx, v_ada_w=v_ada_w, v_ada_b=v_ada_b, v_norm_mix=v_norm_mix, v_norm_ffn=v_norm_ffn, v_ffn_w1=v_ffn_w1, v_ffn_w3=v_ffn_w3, v_ffn_w2=v_ffn_w2, v_attn_w_in=v_attn_w_in, v_attn_w_out=v_attn_w_out, v_attn_rpb=v_attn_rpb, v_attn_sink=v_attn_sink, v_ssm_a_re=v_ssm_a_re, v_ssm_a_im=v_ssm_a_im, v_ssm_log_dt=v_ssm_log_dt, v_ssm_b_re=v_ssm_b_re, v_ssm_b_im=v_ssm_b_im, v_ssm_c_re=v_ssm_c_re, v_ssm_c_im=v_ssm_c_im, v_ssm_d=v_ssm_d, v_ssm_w_glu=v_ssm_w_glu, v_ssm_b_glu=v_ssm_b_glu, v_norm_final=v_norm_final)
    weights = {n: given[n] for n in TWIN_WEIGHTS}
    shared = {n: given[n] for n in SHARED_INPUTS}
    per_example = {n: given[n] for n in ['x', 'c', 'ctx']}
    grad_fn = _jax.value_and_grad(_loss, argnums=(0, 1))

    def one_microbatch(ex, loss_target):
        ex = dict(ex)
        diff = ex.pop(TWIN_DIFF_INPUT)
        return grad_fn(weights, diff, {**shared, **ex}, loss_target)

    if N_MICROBATCH == 1:
        loss, (grad_w, grad_x) = one_microbatch(per_example, given["loss_target"])
    else:
        def body(carry, xs):
            loss_sum, grad_sum = carry
            l_k, (gw_k, gx_k) = one_microbatch(xs[0], xs[1])
            with _jax.named_scope("update"):
                return (loss_sum + l_k, _jax.tree.map(_jnp.add, grad_sum, gw_k)), gx_k

        init = (_jnp.zeros((), _jnp.float32), _jax.tree.map(_jnp.zeros_like, weights))
        (loss, grad_w), grad_x = _jax.lax.scan(body, init, (per_example, given["loss_target"]))
    with _jax.named_scope("update"):
        delta_w, new_m, new_v = {}, {}, {}
        for n in TWIN_WEIGHTS:
            delta_w[n], new_m[n], new_v[n] = _adamw(weights[n], grad_w[n], given["m_" + n], given["v_" + n])
    return (loss, grad_x, *[grad_w[n] for n in TWIN_WEIGHTS], *[delta_w[n] for n in TWIN_WEIGHTS],
            *[new_m[n] for n in TWIN_WEIGHTS], *[new_v[n] for n in TWIN_WEIGHTS])
```

```python
import functools
import math

import numpy as np
import jax
import jax.numpy as jnp
from jax import lax
from jax.experimental import pallas as pl
from jax.experimental.pallas import tpu as pltpu

F32 = jnp.float32
BF16 = jnp.bfloat16
MESH = pl.DeviceIdType.MESH

HEAD_DIM = 128
GRID_W = 64
NA_ROWS = 8
NA_COLS = 16
SW_RADIUS = 128
ATTN_BLOCK = 128
ROPE_BASE = 10000.0
SSM_GROUP = 16
SSM_STATE = 64
SSM_TILE_GROUPS = 8
SCAN_BLOCKS = 8
EPS = 1e-6
NEG_INF = -1e30
ADAM_LR, ADAM_B1, ADAM_B2, ADAM_EPS, ADAM_WD, ADAM_STEP = 0.001, 0.9, 0.999, 1e-08, 0.01, 10
VMEM_LIMIT_BYTES = 56 * 1024 * 1024
N_DEV = 8


def _cparams(*sem):
    return pltpu.CompilerParams(dimension_semantics=tuple(sem) if sem else None, vmem_limit_bytes=VMEM_LIMIT_BYTES)


def _pick(n, cands):
    for c in cands:
        if n % c == 0:
            return c
    return n


def _mm_call(name, grid, a, b, a_spec, b_spec, o_spec, out_sds, acc_shape, nt):
    nk = grid[2]

    def body(a_ref, b_ref, o_ref, acc_ref):
        kk = pl.program_id(2)
        if nt:
            p = lax.dot_general(a_ref[...], b_ref[...], (((1,), (1,)), ((), ())), preferred_element_type=F32)
        else:
            p = jnp.dot(a_ref[...], b_ref[...], preferred_element_type=F32)
        if nk == 1:
            o_ref[...] = p.astype(o_ref.dtype)
        else:
            @pl.when(kk == 0)
            def _():
                acc_ref[...] = p

            @pl.when(kk > 0)
            def _():
                acc_ref[...] += p

            @pl.when(kk == nk - 1)
            def _():
                o_ref[...] = acc_ref[...].astype(o_ref.dtype)

    return pl.pallas_call(
        body, grid=grid, in_specs=[a_spec, b_spec], out_specs=o_spec, out_shape=out_sds,
        scratch_shapes=[pltpu.VMEM(acc_shape, F32)], name=name,
        compiler_params=_cparams("parallel", "parallel", "arbitrary"))(a, b)


_ROW_TILES = (768, 512, 256, 128, 64, 32, 16, 8)
_K_TILES = (2048, 1408, 1024, 512, 256, 128)


def mm_nn(a, w3, kind, out_dtype, name):
    r = a.shape[0]
    s, d1, d2 = w3.shape
    tm = _pick(r, _ROW_TILES)
    if kind == "col":
        tk = d1 if d1 <= 2048 else _pick(d1, _K_TILES)
        grid = (s, r // tm, d1 // tk)
        a_spec = pl.BlockSpec((tm, tk), lambda j, i, k: (i, k))
        b_spec = pl.BlockSpec((None, tk, d2), lambda j, i, k: (j, k, 0))
        o_spec = pl.BlockSpec((tm, d2), lambda j, i, k: (i, j))
        n = s * d2
    else:
        grid = (1, r // tm, s)
        a_spec = pl.BlockSpec((tm, d1), lambda j, i, k: (i, k))
        b_spec = pl.BlockSpec((None, d1, d2), lambda j, i, k: (k, 0, 0))
        o_spec = pl.BlockSpec((tm, d2), lambda j, i, k: (i, 0))
        n = d2
    return _mm_call(name, grid, a, w3, a_spec, b_spec, o_spec, jax.ShapeDtypeStruct((r, n), out_dtype), (tm, d2), False)


def mm_nt(dy, w3, kind, out_dtype, name):
    r = dy.shape[0]
    s, d1, d2 = w3.shape
    tm = _pick(r, _ROW_TILES)
    if kind == "col":
        tko = d1 if d1 <= 2048 else _pick(d1, _K_TILES)
        grid = (d1 // tko, r // tm, s)
        a_spec = pl.BlockSpec((tm, d2), lambda j, i, k: (i, k))
        b_spec = pl.BlockSpec((None, tko, d2), lambda j, i, k: (k, j, 0))
        o_spec = pl.BlockSpec((tm, tko), lambda j, i, k: (i, j))
        kdim, acc = d1, (tm, tko)
    else:
        grid = (s, r // tm, 1)
        a_spec = pl.BlockSpec((tm, d2), lambda j, i, k: (i, 0))
        b_spec = pl.BlockSpec((None, d1, d2), lambda j, i, k: (j, 0, 0))
        o_spec = pl.BlockSpec((tm, d1), lambda j, i, k: (i, j))
        kdim, acc = s * d1, (tm, d1)
    return _mm_call(name, grid, dy, w3, a_spec, b_spec, o_spec, jax.ShapeDtypeStruct((r, kdim), out_dtype), acc, True)


def mm_tn(at, dy, w_shape, kind, out_dtype, name):
    s, d1, d2 = w_shape
    r = dy.shape[0]
    tr = _pick(r, (1024, 768, 512, 256, 128))
    if kind == "col":
        tkk = d1 if d1 <= 2048 else _pick(d1, _K_TILES)
        grid = (s * (d1 // tkk), 1, r // tr)
        nkk = d1 // tkk
        a_spec = pl.BlockSpec((tkk, tr), lambda j, i, k: (j % nkk, k))
        b_spec = pl.BlockSpec((tr, d2), lambda j, i, k: (k, j // nkk))
        o_spec = pl.BlockSpec((None, tkk, d2), lambda j, i, k: (j // nkk, j % nkk, 0))
        acc = (tkk, d2)
    else:
        grid = (s, 1, r // tr)
        a_spec = pl.BlockSpec((d1, tr), lambda j, i, k: (j, k))
        b_spec = pl.BlockSpec((tr, d2), lambda j, i, k: (k, 0))
        o_spec = pl.BlockSpec((None, d1, d2), lambda j, i, k: (j, 0, 0))
        acc = (d1, d2)
    return _mm_call(name, grid, at, dy, a_spec, b_spec, o_spec, jax.ShapeDtypeStruct(w_shape, out_dtype), acc, False)


def make_linear(kind, out_dtype, name):
    @jax.custom_vjp
    def linear(a, w3):
        return mm_nn(a, w3, kind, out_dtype, name + "_fwd")

    def fwd(a, w3):
        return mm_nn(a, w3, kind, out_dtype, name + "_fwd"), (a, w3)

    def bwd(res, dy):
        a, w3 = res
        dyb = dy.astype(BF16)
        da = mm_nt(dyb, w3, kind, a.dtype, name + "_dx")
        dw = mm_tn(a.T, dyb, w3.shape, kind, w3.dtype, name + "_dw")
        return da, dw

    linear.defvjp(fwd, bwd)
    return linear


def _row_tile(r, t0, d):
    cap = max(8, (2 * 1024 * 1024) // (4 * d))
    cands = [t for t in (1024, 512, 256, 128, 64, 32, 16, 8) if t <= cap]
    for t in cands:
        if r % t == 0 and t0 % t == 0:
            return t
    raise ValueError("no row tile")


def _grp_spec(d, nb0):
    return pl.BlockSpec((None, 1, d), lambda i: (i // nb0, 0, 0))


def _norm_mod_fwd(z, g, scale, shift, t0, name):
    r, d = z.shape
    tr = _row_tile(r, t0, d)
    nb0 = t0 // tr

    def body(z_ref, g_ref, sc_ref, sh_ref, o_ref):
        zz = z_ref[...]
        rstd = lax.rsqrt(jnp.mean(zz * zz, axis=-1, keepdims=True) + EPS)
        y = zz * rstd * g_ref[...]
        o_ref[...] = (y * (1.0 + sc_ref[...]) + sh_ref[...]).astype(o_ref.dtype)

    return pl.pallas_call(
        body, grid=(r // tr,),
        in_specs=[pl.BlockSpec((tr, d), lambda i: (i, 0)), pl.BlockSpec((1, d), lambda i: (0, 0)),
                  _grp_spec(d, nb0), _grp_spec(d, nb0)],
        out_specs=pl.BlockSpec((tr, d), lambda i: (i, 0)),
        out_shape=jax.ShapeDtypeStruct((r, d), BF16), name=name, compiler_params=_cparams("parallel"))(z, g, scale, shift)


def _norm_mod_bwd(z, g, scale, dh, t0, name):
    r, d = z.shape
    ng = scale.shape[0]
    tr = _row_tile(r, t0, d)
    nb0 = t0 // tr

    def body(z_ref, g_ref, sc_ref, dh_ref, dz_ref, dg_ref, dsc_ref, dsh_ref):
        i = pl.program_id(0)
        zz = z_ref[...]
        gg = g_ref[...]
        rstd = lax.rsqrt(jnp.mean(zz * zz, axis=-1, keepdims=True) + EPS)
        zhat = zz * rstd
        dhh = dh_ref[...].astype(F32)
        dy = dhh * (1.0 + sc_ref[...])
        dyg = dy * gg
        dz_ref[...] = rstd * (dyg - zhat * jnp.mean(dyg * zhat, axis=-1, keepdims=True))

        @pl.when(i == 0)
        def _():
            dg_ref[...] = jnp.zeros_like(dg_ref)

        @pl.when((i == 0) | (i == nb0))
        def _():
            dsc_ref[...] = jnp.zeros_like(dsc_ref)
            dsh_ref[...] = jnp.zeros_like(dsh_ref)

        dg_ref[...] += jnp.sum(dy * zhat, axis=0, keepdims=True)
        dsc_ref[...] += jnp.sum(dhh * (zhat * gg), axis=0, keepdims=True)
        dsh_ref[...] += jnp.sum(dhh, axis=0, keepdims=True)

    return pl.pallas_call(
        body, grid=(r // tr,),
        in_specs=[pl.BlockSpec((tr, d), lambda i: (i, 0)), pl.BlockSpec((1, d), lambda i: (0, 0)),
                  _grp_spec(d, nb0), pl.BlockSpec((tr, d), lambda i: (i, 0))],
        out_specs=[pl.BlockSpec((tr, d), lambda i: (i, 0)), pl.BlockSpec((1, d), lambda i: (0, 0)),
                   _grp_spec(d, nb0), _grp_spec(d, nb0)],
        out_shape=[jax.ShapeDtypeStruct((r, d), F32), jax.ShapeDtypeStruct((1, d), F32),
                   jax.ShapeDtypeStruct((ng, 1, d), F32), jax.ShapeDtypeStruct((ng, 1, d), F32)],
        name=name, compiler_params=_cparams("arbitrary"))(z, g, scale, dh)


def make_norm_mod(t0, name):
    @jax.custom_vjp
    def f(z, g, scale, shift):
        return _norm_mod_fwd(z, g, scale, shift, t0, name + "_fwd")

    def fwd(z, g, scale, shift):
        return _norm_mod_fwd(z, g, scale, shift, t0, name + "_fwd"), (z, g, scale)

    def bwd(res, dh):
        z, g, scale = res
        dz, dg, dsc, dsh = _norm_mod_bwd(z, g, scale, dh, t0, name + "_bwd")
        return dz, dg, dsc, dsh

    f.defvjp(fwd, bwd)
    return f


def _gated_fwd(z, y, gate, t0, name):
    r, d = z.shape
    tr = _row_tile(r, t0, d)
    nb0 = t0 // tr

    def body(z_ref, y_ref, g_ref, o_ref):
        o_ref[...] = z_ref[...] + g_ref[...] * y_ref[...].astype(F32)

    return pl.pallas_call(
        body, grid=(r // tr,),
        in_specs=[pl.BlockSpec((tr, d), lambda i: (i, 0)), pl.BlockSpec((tr, d), lambda i: (i, 0)), _grp_spec(d, nb0)],
        out_specs=pl.BlockSpec((tr, d), lambda i: (i, 0)),
        out_shape=jax.ShapeDtypeStruct((r, d), F32), name=name, compiler_params=_cparams("parallel"))(z, y, gate)


def _gated_bwd(y, gate, dzn, t0, name):
    r, d = y.shape
    ng = gate.shape[0]
    tr = _row_tile(r, t0, d)
    nb0 = t0 // tr

    def body(y_ref, g_ref, dz_ref, dy_ref, dg_ref):
        i = pl.program_id(0)
        dzz = dz_ref[...]
        dy_ref[...] = (g_ref[...] * dzz).astype(dy_ref.dtype)

        @pl.when((i == 0) | (i == nb0))
        def _():
            dg_ref[...] = jnp.zeros_like(dg_ref)

        dg_ref[...] += jnp.sum(dzz * y_ref[...].astype(F32), axis=0, keepdims=True)

    return pl.pallas_call(
        body, grid=(r // tr,),
        in_specs=[pl.BlockSpec((tr, d), lambda i: (i, 0)), _grp_spec(d, nb0), pl.BlockSpec((tr, d), lambda i: (i, 0))],
        out_specs=[pl.BlockSpec((tr, d), lambda i: (i, 0)), _grp_spec(d, nb0)],
        out_shape=[jax.ShapeDtypeStruct((r, d), y.dtype), jax.ShapeDtypeStruct((ng, 1, d), F32)],
        name=name, compiler_params=_cparams("arbitrary"))(y, gate, dzn)


def make_gated_residual(t0, name):
    @jax.custom_vjp
    def f(z, y, gate):
        return _gated_fwd(z, y, gate, t0, name + "_fwd")

    def fwd(z, y, gate):
        return _gated_fwd(z, y, gate, t0, name + "_fwd"), (y, gate)

    def bwd(res, dzn):
        y, gate = res
        dy, dgate = _gated_bwd(y, gate, dzn, t0, name + "_bwd")
        return dzn, dy, dgate

    f.defvjp(fwd, bwd)
    return f


def _ew_call(name, body, ins, outs_sds, r, widths_in, widths_out, tr, extra_in=(), extra_specs=(), sem="parallel"):
    in_specs = [pl.BlockSpec((tr, w), lambda i: (i, 0)) for w in widths_in] + list(extra_specs)
    out_specs = [pl.BlockSpec((tr, w), lambda i: (i, 0)) if w is not None else pl.BlockSpec(s.shape, lambda i: (0,) * len(s.shape))
                 for w, s in zip(widths_out, outs_sds)]
    return pl.pallas_call(body, grid=(r // tr,), in_specs=in_specs, out_specs=out_specs, out_shape=outs_sds,
                          name=name, compiler_params=_cparams(sem))(*ins, *extra_in)


def _silu(x):
    return x * jax.nn.sigmoid(x)


def make_swiglu_act(name):
    def fwd_call(h1, h3):
        r, f = h1.shape
        tr = _row_tile(r, r, f)

        def body(a_ref, b_ref, o_ref):
            o_ref[...] = (_silu(a_ref[...].astype(F32)) * b_ref[...].astype(F32)).astype(o_ref.dtype)

        return _ew_call(name + "_fwd", body, (h1, h3), [jax.ShapeDtypeStruct((r, f), BF16)], r, (f, f), (f,), tr)[0]

    @jax.custom_vjp
    def act(h1, h3):
        return fwd_call(h1, h3)

    def fwd(h1, h3):
        return fwd_call(h1, h3), (h1, h3)

    def bwd(res, da):
        h1, h3 = res
        r, f = h1.shape
        tr = _row_tile(r, r, f)

        def body(a_ref, b_ref, da_ref, d1_ref, d3_ref):
            a = a_ref[...].astype(F32)
            b = b_ref[...].astype(F32)
            g = da_ref[...].astype(F32)
            sg = jax.nn.sigmoid(a)
            d1_ref[...] = (g * b * (sg * (1.0 + a * (1.0 - sg)))).astype(d1_ref.dtype)
            d3_ref[...] = (g * a * sg).astype(d3_ref.dtype)

        sds = jax.ShapeDtypeStruct((r, f), BF16)
        return tuple(_ew_call(name + "_bwd", body, (h1, h3, da), [sds, sds], r, (f, f, f), (f, f), tr))

    act.defvjp(fwd, bwd)
    return act


_GELU_C = math.sqrt(2.0 / math.pi)


def _gelu_and_grad(y):
    inner = _GELU_C * (y + 0.044715 * y * y * y)
    t = jnp.tanh(inner)
    val = 0.5 * y * (1.0 + t)
    grad = 0.5 * (1.0 + t) + 0.5 * y * (1.0 - t * t) * _GELU_C * (1.0 + 3 * 0.044715 * y * y)
    return val, grad


def make_gelu_in(name):
    def fwd_call(u, ys, dsk):
        r, d = u.shape
        tr = _row_tile(r, r, d)

        def body(u_ref, y_ref, d_ref, o_ref):
            y = d_ref[...] * u_ref[...].astype(F32) + y_ref[...]
            o_ref[...] = _gelu_and_grad(y)[0].astype(o_ref.dtype)

        return _ew_call(name + "_fwd", body, (u, ys), [jax.ShapeDtypeStruct((r, d), BF16)], r, (d, d), (d,), tr,
                        extra_in=(dsk,), extra_specs=(pl.BlockSpec((1, d), lambda i: (0, 0)),))[0]

    @jax.custom_vjp
    def f(u, ys, dsk):
        return fwd_call(u, ys, dsk)

    def fwd(u, ys, dsk):
        return fwd_call(u, ys, dsk), (u, ys, dsk)

    def bwd(res, dg):
        u, ys, dsk = res
        r, d = u.shape
        tr = _row_tile(r, r, d)

        def body(u_ref, y_ref, dg_ref, d_ref, du_ref, dy_ref, dd_ref):
            i = pl.program_id(0)
            uu = u_ref[...].astype(F32)
            y = d_ref[...] * uu + y_ref[...]
            dy = dg_ref[...].astype(F32) * _gelu_and_grad(y)[1]
            dy_ref[...] = dy
            du_ref[...] = (d_ref[...] * dy).astype(du_ref.dtype)

            @pl.when(i == 0)
            def _():
                dd_ref[...] = jnp.zeros_like(dd_ref)

            dd_ref[...] += jnp.sum(dy * uu, axis=0, keepdims=True)

        outs = [jax.ShapeDtypeStruct((r, d), u.dtype), jax.ShapeDtypeStruct((r, d), F32), jax.ShapeDtypeStruct((1, d), F32)]
        du, dy, dd = _ew_call(name + "_bwd", body, (u, ys, dg), outs, r, (d, d, d), (d, d, None), tr,
                              extra_in=(dsk,), extra_specs=(pl.BlockSpec((1, d), lambda i: (0, 0)),), sem="arbitrary")
        return du, dy, dd

    f.defvjp(fwd, bwd)
    return f


def make_glu(name):
    def fwd_call(z, b):
        r, d2 = z.shape
        d = d2 // 2
        tr = _row_tile(r, r, d2)

        def body(z_ref, b_ref, o_ref):
            zz = z_ref[...].astype(F32) + b_ref[...]
            o_ref[...] = zz[:, :d] * jax.nn.sigmoid(zz[:, d:])

        return _ew_call(name + "_fwd", body, (z,), [jax.ShapeDtypeStruct((r, d), F32)], r, (d2,), (d,), tr,
                        extra_in=(b,), extra_specs=(pl.BlockSpec((1, d2), lambda i: (0, 0)),))[0]

    @jax.custom_vjp
    def f(z, b):
        return fwd_call(z, b)

    def fwd(z, b):
        return fwd_call(z, b), (z, b)

    def bwd(res, do):
        z, b = res
        r, d2 = z.shape
        d = d2 // 2
        tr = _row_tile(r, r, d2)

        def body(z_ref, do_ref, b_ref, dz_ref, db_ref):
            i = pl.program_id(0)
            zz = z_ref[...].astype(F32) + b_ref[...]
            sg = jax.nn.sigmoid(zz[:, d:])
            g = do_ref[...]
            dza = g * sg
            dzb = g * zz[:, :d] * sg * (1.0 - sg)
            dz_ref[:, :d] = dza.astype(dz_ref.dtype)
            dz_ref[:, d:] = dzb.astype(dz_ref.dtype)

            @pl.when(i == 0)
            def _():
                db_ref[...] = jnp.zeros_like(db_ref)

            db_ref[:, :d] += jnp.sum(dza, axis=0, keepdims=True)
            db_ref[:, d:] += jnp.sum(dzb, axis=0, keepdims=True)

        outs = [jax.ShapeDtypeStruct((r, d2), z.dtype), jax.ShapeDtypeStruct((1, d2), F32)]
        dz, db = _ew_call(name + "_bwd", body, (z, do), outs, r, (d2, d), (d2, None), tr,
                          extra_in=(b,), extra_specs=(pl.BlockSpec((1, d2), lambda i: (0, 0)),), sem="arbitrary")
        return dz, db

    f.defvjp(fwd, bwd)
    return f


def make_final_loss(name):
    def call(z, g, target):
        r, d = z.shape
        tr = _row_tile(r, r, d)

        def body(z_ref, t_ref, g_ref, dz_ref, dg_ref, l_ref):
            i = pl.program_id(0)
            zz = z_ref[...]
            gg = g_ref[...]
            rstd = lax.rsqrt(jnp.mean(zz * zz, axis=-1, keepdims=True) + EPS)
            zhat = zz * rstd
            e = zhat * gg - t_ref[...]
            dy = e * (1.0 / d)
            dyg = dy * gg
            dz_ref[...] = rstd * (dyg - zhat * jnp.mean(dyg * zhat, axis=-1, keepdims=True))

            @pl.when(i == 0)
            def _():
                dg_ref[...] = jnp.zeros_like(dg_ref)
                l_ref[...] = jnp.zeros_like(l_ref)

            dg_ref[...] += jnp.sum(dy * zhat, axis=0, keepdims=True)
            l_ref[...] += jnp.sum(jnp.sum(e * e, axis=1, keepdims=True), axis=0, keepdims=True) * (0.5 / d)

        outs = [jax.ShapeDtypeStruct((r, d), F32), jax.ShapeDtypeStruct((1, d), F32), jax.ShapeDtypeStruct((1, 1), F32)]
        return _ew_call(name, body, (z, target), outs, r, (d, d), (d, None, None), tr,
                        extra_in=(g,), extra_specs=(pl.BlockSpec((1, d), lambda i: (0, 0)),), sem="arbitrary")

    @jax.custom_vjp
    def f(z, g, target):
        return call(z, g, target)[2]

    def fwd(z, g, target):
        dz, dg, loss = call(z, g, target)
        return loss, (dz, dg)

    def bwd(res, dl):
        dz, dg = res
        s = dl[0, 0]
        return dz * s, dg * s, None

    f.defvjp(fwd, bwd)
    return f


def _rope_tables(t):
    quarter = HEAD_DIM // 4
    inv_freq = ROPE_BASE ** (-np.arange(quarter, dtype=np.float64) / quarter)
    pos = np.arange(t)
    ang_r = (pos // GRID_W)[:, None] * inv_freq[None, :]
    ang_c = (pos % GRID_W)[:, None] * inv_freq[None, :]
    cos = np.concatenate([np.cos(ang_r), np.cos(ang_r), np.cos(ang_c), np.cos(ang_c)], axis=1)
    sin = np.concatenate([-np.sin(ang_r), np.sin(ang_r), -np.sin(ang_c), np.sin(ang_c)], axis=1)
    return jnp.asarray(cos, F32), jnp.asarray(sin, F32)


def _rope_call(x, cos, sin, name):
    t, w = x.shape
    tr = _pick(t, (512, 256, 128, 64))
    quarter = HEAD_DIM // 4

    def body(x_ref, c_ref, s_ref, o_ref):
        xx = x_ref[...].astype(F32)
        lane = lax.broadcasted_iota(jnp.int32, xx.shape, 1)
        first = (lane % (2 * quarter)) < quarter
        partner = jnp.where(first, pltpu.roll(xx, HEAD_DIM - quarter, 1), pltpu.roll(xx, quarter, 1))
        o_ref[...] = (xx * c_ref[...] + partner * s_ref[...]).astype(o_ref.dtype)

    return pl.pallas_call(
        body, grid=(t // tr, w // HEAD_DIM),
        in_specs=[pl.BlockSpec((tr, HEAD_DIM), lambda i, j: (i, j)), pl.BlockSpec((tr, HEAD_DIM), lambda i, j: (i, 0)),
                  pl.BlockSpec((tr, HEAD_DIM), lambda i, j: (i, 0))],
        out_specs=pl.BlockSpec((tr, HEAD_DIM), lambda i, j: (i, j)),
        out_shape=jax.ShapeDtypeStruct((t, w), x.dtype), name=name, compiler_params=_cparams("parallel", "parallel"))(x, cos, sin)


def make_rope(t, name):
    cos, sin = _rope_tables(t)

    @jax.custom_vjp
    def f(x):
        return _rope_call(x, cos, sin, name + "_fwd")

    def fwd(x):
        return _rope_call(x, cos, sin, name + "_fwd"), None

    def bwd(_, dy):
        return (_rope_call(dy, cos, -sin, name + "_bwd"),)

    f.defvjp(fwd, bwd)
    return f


def _dot_nt(a, b):
    return lax.dot_general(a, b, (((1,), (1,)), ((), ())), preferred_element_type=F32)


def _dot_tn(a, b):
    return lax.dot_general(a, b, (((0,), (0,)), ((), ())), preferred_element_type=F32)


def _attn_specs(g, span, tk, m, nbh, has_ctx):
    hd = HEAD_DIM
    q_spec = pl.BlockSpec((ATTN_BLOCK, g * hd), lambda h, i, meta: (i, h))
    kv_spec = pl.BlockSpec((tk, hd), lambda h, i, meta: (0, h))
    c_spec = pl.BlockSpec((m, hd), lambda h, i, meta: (0, h))
    if nbh > 1:
        b_spec = pl.BlockSpec((None, None, ATTN_BLOCK, span), lambda h, i, meta: (meta[1, i], h, 0, 0))
    else:
        b_spec = pl.BlockSpec((None, None, ATTN_BLOCK, span), lambda h, i, meta: (meta[1, i], 0, 0, 0))
    sink_spec = pl.BlockSpec(memory_space=pltpu.SMEM)
    return q_spec, kv_spec, c_spec, b_spec, sink_spec


def _attn_probs(qh, ks, kc, bias, sink_val, scale, has_ctx, has_sink):
    s = _dot_nt(qh, ks) * scale + bias
    mx = jnp.max(s, axis=-1, keepdims=True)
    sc = None
    if has_ctx:
        sc = _dot_nt(qh, kc) * scale
        mx = jnp.maximum(mx, jnp.max(sc, axis=-1, keepdims=True))
    if has_sink:
        mx = jnp.maximum(mx, sink_val)
    p = jnp.exp(s - mx)
    l = jnp.sum(p, axis=-1, keepdims=True)
    pc = None
    if has_ctx:
        pc = jnp.exp(sc - mx)
        l = l + jnp.sum(pc, axis=-1, keepdims=True)
    ps = None
    if has_sink:
        ps = jnp.exp(sink_val - mx)
        l = l + ps
    return p, pc, ps, l


def _attn_fwd(q, k, v, kc, vc, bias, sink, meta, g, span, has_ctx, has_sink, name):
    rq, wq = q.shape
    tk, wk = k.shape
    hkv = wk // HEAD_DIM
    m = kc.shape[0]
    nbh = bias.shape[1]
    scale = HEAD_DIM ** -0.5
    nqb = rq // ATTN_BLOCK
    q_spec, kv_spec, c_spec, b_spec, sink_spec = _attn_specs(g, span, tk, m, nbh, has_ctx)

    def body(meta_ref, sink_ref, q_ref, k_ref, v_ref, kc_ref, vc_ref, b_ref, o_ref):
        h = pl.program_id(0)
        i = pl.program_id(1)
        ks0 = pl.multiple_of(meta_ref[0, i], 64)
        ks = k_ref[pl.ds(ks0, span), :]
        vs = v_ref[pl.ds(ks0, span), :]
        bias_t = b_ref[...]
        for hh in range(g):
            qh = q_ref[:, hh * HEAD_DIM:(hh + 1) * HEAD_DIM]
            sink_val = sink_ref[h * g + hh] if has_sink else None
            p, pc, _, l = _attn_probs(qh, ks, kc_ref[...], bias_t, sink_val, scale, has_ctx, has_sink)
            acc = jnp.dot(p.astype(BF16), vs, preferred_element_type=F32)
            if has_ctx:
                acc = acc + jnp.dot(pc.astype(BF16), vc_ref[...], preferred_element_type=F32)
            o_ref[:, hh * HEAD_DIM:(hh + 1) * HEAD_DIM] = (acc / l).astype(o_ref.dtype)

    gs = pltpu.PrefetchScalarGridSpec(
        num_scalar_prefetch=1, grid=(hkv, nqb),
        in_specs=[sink_spec, q_spec, kv_spec, kv_spec, c_spec, c_spec, b_spec], out_specs=q_spec)
    return pl.pallas_call(body, grid_spec=gs, out_shape=jax.ShapeDtypeStruct((rq, wq), BF16), name=name,
                          compiler_params=_cparams("parallel", "arbitrary"))(meta, sink, q, k, v, kc, vc, bias)


def _attn_bwd(q, k, v, kc, vc, bias, sink, meta, o, do, g, span, has_ctx, has_sink, want_dbias, name):
    rq, wq = q.shape
    tk, wk = k.shape
    hkv = wk // HEAD_DIM
    m = kc.shape[0]
    ncase, nbh = bias.shape[:2]
    scale = HEAD_DIM ** -0.5
    nqb = rq // ATTN_BLOCK
    q_spec, kv_spec, c_spec, b_spec, sink_spec = _attn_specs(g, span, tk, m, nbh, has_ctx)
    dsink_spec = pl.BlockSpec((None, 8, HEAD_DIM), lambda h, i, meta: (h, 0, 0))

    def body(meta_ref, sink_ref, q_ref, k_ref, v_ref, kc_ref, vc_ref, b_ref, o_ref, do_ref,
             dq_ref, dk_ref, dv_ref, dkc_ref, dvc_ref, db_ref, dsk_ref):
        h = pl.program_id(0)
        i = pl.program_id(1)

        @pl.when(i == 0)
        def _():
            dk_ref[...] = jnp.zeros_like(dk_ref)
            dv_ref[...] = jnp.zeros_like(dv_ref)
            dkc_ref[...] = jnp.zeros_like(dkc_ref)
            dvc_ref[...] = jnp.zeros_like(dvc_ref)
            dsk_ref[...] = jnp.zeros_like(dsk_ref)

        if want_dbias:
            @pl.when(meta_ref[2, i] == 1)
            def _():
                db_ref[...] = jnp.zeros_like(db_ref)
        else:
            @pl.when(i == 0)
            def _():
                db_ref[...] = jnp.zeros_like(db_ref)

        ks0 = pl.multiple_of(meta_ref[0, i], 64)
        ks = k_ref[pl.ds(ks0, span), :]
        vs = v_ref[pl.ds(ks0, span), :]
        bias_t = b_ref[...]
        dk_acc = jnp.zeros((span, HEAD_DIM), F32)
        dv_acc = jnp.zeros((span, HEAD_DIM), F32)
        for hh in range(g):
            cols = slice(hh * HEAD_DIM, (hh + 1) * HEAD_DIM)
            qh = q_ref[:, cols]
            doh = do_ref[:, cols]
            sink_val = sink_ref[h * g + hh] if has_sink else None
            p, pc, ps, l = _attn_probs(qh, ks, kc_ref[...], bias_t, sink_val, scale, has_ctx, has_sink)
            inv_l = 1.0 / l
            delta = jnp.sum(doh.astype(F32) * o_ref[:, cols].astype(F32), axis=-1, keepdims=True)
            pn = p * inv_l
            ds = pn * (_dot_nt(doh, vs) - delta)
            dsb = ds.astype(BF16)
            dq = jnp.dot(dsb, ks, preferred_element_type=F32)
            dk_acc = dk_acc + _dot_tn(dsb, qh)
            dv_acc = dv_acc + _dot_tn(pn.astype(BF16), doh)
            if want_dbias:
                db_ref[...] += ds
            if has_ctx:
                pcn = pc * inv_l
                dsc = (pcn * (_dot_nt(doh, vc_ref[...]) - delta)).astype(BF16)
                dq = dq + jnp.dot(dsc, kc_ref[...], preferred_element_type=F32)
                dkc_ref[...] += _dot_tn(dsc, qh) * scale
                dvc_ref[...] += _dot_tn(pcn.astype(BF16), doh)
            if has_sink:
                dsv = -jnp.sum(ps * inv_l * delta, axis=0, keepdims=True)
                dsk_ref[hh:hh + 1, :] += jnp.broadcast_to(dsv, (1, HEAD_DIM))
            dq_ref[:, cols] = (dq * scale).astype(dq_ref.dtype)
        dk_ref[pl.ds(ks0, span), :] += dk_acc * scale
        dv_ref[pl.ds(ks0, span), :] += dv_acc

    gs = pltpu.PrefetchScalarGridSpec(
        num_scalar_prefetch=1, grid=(hkv, nqb),
        in_specs=[sink_spec, q_spec, kv_spec, kv_spec, c_spec, c_spec, b_spec, q_spec, q_spec],
        out_specs=[q_spec, kv_spec, kv_spec, c_spec, c_spec, b_spec if want_dbias else dsink_spec, dsink_spec])
    db_sds = jax.ShapeDtypeStruct((ncase, nbh, ATTN_BLOCK, span) if want_dbias else (hkv, 8, HEAD_DIM), F32)
    out_shape = [jax.ShapeDtypeStruct((rq, wq), BF16), jax.ShapeDtypeStruct((tk, wk), F32), jax.ShapeDtypeStruct((tk, wk), F32),
                 jax.ShapeDtypeStruct((m, wk), F32), jax.ShapeDtypeStruct((m, wk), F32), db_sds,
                 jax.ShapeDtypeStruct((hkv, 8, HEAD_DIM), F32)]
    return pl.pallas_call(body, grid_spec=gs, out_shape=out_shape, name=name,
                          compiler_params=_cparams("parallel", "arbitrary"))(meta, sink, q, k, v, kc, vc, bias, o, do)


def make_attention(meta_np, g, span, has_ctx, has_sink, want_dbias, name):
    meta = jnp.asarray(meta_np, jnp.int32)

    @jax.custom_vjp
    def f(q, k, v, kc, vc, bias, sink):
        return _attn_fwd(q, k, v, kc, vc, bias, sink, meta, g, span, has_ctx, has_sink, name + "_fwd")

    def fwd(q, k, v, kc, vc, bias, sink):
        o = _attn_fwd(q, k, v, kc, vc, bias, sink, meta, g, span, has_ctx, has_sink, name + "_fwd")
        return o, (q, k, v, kc, vc, bias, sink, o)

    def bwd(res, do):
        q, k, v, kc, vc, bias, sink, o = res
        dq, dk, dv, dkc, dvc, db, dsk = _attn_bwd(q, k, v, kc, vc, bias, sink, meta, o, do.astype(BF16), g, span,
                                                   has_ctx, has_sink, want_dbias, name + "_bwd")
        dsink = dsk[:, :g, 0].reshape(sink.shape)
        if not want_dbias:
            db = jnp.zeros_like(bias)
        return dq, dk.astype(k.dtype), dv.astype(v.dtype), dkc.astype(kc.dtype), dvc.astype(vc.dtype), db, dsink

    f.defvjp(fwd, bwd)
    return f


def _dedupe_cases(tables):
    cases, idx, first = [], [], []
    for tbl in tables:
        if cases and np.array_equal(cases[-1], tbl):
            idx.append(len(cases) - 1)
            first.append(0)
        else:
            cases.append(tbl)
            idx.append(len(cases) - 1)
            first.append(1)
    return cases, idx, first


def _na_plan(t):
    rows = t // GRID_W
    qr = ATTN_BLOCK // GRID_W
    kr = qr + NA_ROWS - 1
    assert rows >= kr and rows % qr == 0
    span = kr * GRID_W
    kstart, tables = [], []
    qcol = np.tile(np.arange(GRID_W), qr)
    kcol = np.tile(np.arange(GRID_W), kr)
    win_c = np.clip(qcol - NA_COLS // 2, 0, GRID_W - NA_COLS)
    col_ok = (kcol[None, :] >= win_c[:, None]) & (kcol[None, :] < win_c[:, None] + NA_COLS)
    dcol = np.clip(kcol[None, :] - qcol[:, None] + NA_COLS - 1, 0, 2 * NA_COLS - 2)
    for r0 in range(0, rows, qr):
        kb = int(np.clip(r0 - NA_ROWS // 2, 0, rows - kr))
        qrow = r0 + np.repeat(np.arange(qr), GRID_W)
        krow = kb + np.repeat(np.arange(kr), GRID_W)
        win_r = np.clip(qrow - NA_ROWS // 2, 0, rows - NA_ROWS)
        row_ok = (krow[None, :] >= win_r[:, None]) & (krow[None, :] < win_r[:, None] + NA_ROWS)
        drow = np.clip(krow[None, :] - qrow[:, None] + NA_ROWS - 1, 0, 2 * NA_ROWS - 2)
        tables.append(np.stack([row_ok & col_ok, drow, dcol]).astype(np.int32))
        kstart.append(kb * GRID_W)
    cases, idx, first = _dedupe_cases(tables)
    meta = np.array([kstart, idx, first], np.int32)
    return meta, span, np.stack(cases)


def _na_bias(rpb, cases):
    valid, drow, dcol = cases[:, 0], cases[:, 1], cases[:, 2]
    oh_r = jnp.asarray(np.eye(2 * NA_ROWS - 1, dtype=np.float32)[drow])
    oh_c = jnp.asarray(np.eye(2 * NA_COLS - 1, dtype=np.float32)[dcol[0]])
    tmp = jnp.einsum("hrc,qkc->hrqk", rpb, oh_c, precision=lax.Precision.HIGHEST)
    b = jnp.einsum("nqkr,hrqk->nhqk", oh_r, tmp, precision=lax.Precision.HIGHEST)
    return jnp.where(jnp.asarray(valid[:, None] > 0), b, NEG_INF)


def _sw_plan(t):
    span = 3 * ATTN_BLOCK
    assert t >= span
    kstart, tables = [], []
    for b in range(t // ATTN_BLOCK):
        ks = int(np.clip((b - 1) * ATTN_BLOCK, 0, t - span))
        qpos = b * ATTN_BLOCK + np.arange(ATTN_BLOCK)
        kpos = ks + np.arange(span)
        ok = np.abs(kpos[None, :] - qpos[:, None]) <= SW_RADIUS
        tables.append(np.where(ok, 0.0, NEG_INF).astype(np.float32))
        kstart.append(ks)
    cases, idx, first = _dedupe_cases(tables)
    return np.array([kstart, idx, first], np.int32), span, np.stack(cases)[:, None]


def _cmul(ar, ai, br, bi):
    return ar * br - ai * bi, ar * bi + ai * br


def _s5_scan_call(x2, win, lam, cin, wout, reverse, n_chunks, name):
    _, ll, d = x2.shape
    nt = d // HEAD_DIM
    sw = 2 * SSM_TILE_GROUPS * SSM_STATE
    hw = sw // 2
    rows = ll // n_chunks
    ic = rows // SCAN_BLOCKS
    full = cin is not None

    def chunk_idx(k):
        return (n_chunks - 1 - k) if reverse else k

    def body(*refs):
        if full:
            x_ref, win_ref, lam_ref, cin_ref, wout_ref, s_out, y_out, ub_ref, st_ref = refs
        else:
            x_ref, win_ref, lam_ref, f_out, ub_ref, st_ref = refs
        k = pl.program_id(2)

        @pl.when(k == 0)
        def _():
            st_ref[...] = cin_ref[...] if full else jnp.zeros_like(st_ref)

        ub_ref[...] = jnp.dot(x_ref[...].astype(BF16), win_ref[...], preferred_element_type=F32)
        lr = lam_ref[:, :hw]
        li = lam_ref[:, hw:]

        def step(ii, carry):
            sr, si = carry
            i = (ic - 1 - ii) if reverse else ii
            r0 = pl.multiple_of(i * SCAN_BLOCKS, SCAN_BLOCKS)
            ur = ub_ref[pl.ds(r0, SCAN_BLOCKS), :hw]
            ui = ub_ref[pl.ds(r0, SCAN_BLOCKS), hw:]
            nr = lr * sr - li * si + ur
            ni = lr * si + li * sr + ui
            if full:
                ub_ref[pl.ds(r0, SCAN_BLOCKS), :hw] = nr
                ub_ref[pl.ds(r0, SCAN_BLOCKS), hw:] = ni
            return nr, ni

        sr, si = lax.fori_loop(0, ic, step, (st_ref[:, :hw], st_ref[:, hw:]), unroll=4 if ic % 4 == 0 else 1)
        st_ref[:, :hw] = sr
        st_ref[:, hw:] = si
        if full:
            s_out[...] = ub_ref[...]
            y_out[...] = jnp.dot(ub_ref[...].astype(BF16), wout_ref[...], preferred_element_type=F32)
        else:
            @pl.when(k == n_chunks - 1)
            def _():
                f_out[...] = st_ref[...]

    x_spec = pl.BlockSpec((None, rows, HEAD_DIM), lambda dd, t, k: (dd, chunk_idx(k), t))
    win_spec = pl.BlockSpec((None, None, HEAD_DIM, sw), lambda dd, t, k: (dd, t, 0, 0))
    vec_spec = pl.BlockSpec((None, None, SCAN_BLOCKS, sw), lambda dd, t, k: (dd, t, 0, 0))
    scratch = [pltpu.VMEM((rows, sw), F32), pltpu.VMEM((SCAN_BLOCKS, sw), F32)]
    if full:
        in_specs = [x_spec, win_spec, vec_spec, vec_spec, pl.BlockSpec((None, None, sw, HEAD_DIM), lambda dd, t, k: (dd, t, 0, 0))]
        out_specs = [pl.BlockSpec((None, None, rows, sw), lambda dd, t, k: (dd, t, chunk_idx(k), 0)), x_spec]
        out_shape = [jax.ShapeDtypeStruct((2, nt, ll, sw), F32), jax.ShapeDtypeStruct((2, ll, d), F32)]
        args = (x2, win, lam, cin, wout)
    else:
        in_specs = [x_spec, win_spec, vec_spec]
        out_specs = vec_spec
        out_shape = jax.ShapeDtypeStruct((2, nt, SCAN_BLOCKS, sw), F32)
        args = (x2, win, lam)
    return pl.pallas_call(body, grid=(2, nt, n_chunks), in_specs=in_specs, out_specs=out_specs, out_shape=out_shape,
                          scratch_shapes=scratch, name=name,
                          compiler_params=_cparams("parallel", "parallel", "arbitrary"))(*args)


def _s5_bwd_call(dy2, wrt, lamc, cin, st, u2, wdt, n_chunks, name):
    _, ll, d = dy2.shape
    nt = d // HEAD_DIM
    sw = 2 * SSM_TILE_GROUPS * SSM_STATE
    hw = sw // 2
    rows = ll // n_chunks
    ic = rows // SCAN_BLOCKS

    def chunk_idx(k):
        return n_chunks - 1 - k

    def body(dy_ref, wrt_ref, lam_ref, cin_ref, st_ref, u_ref, wdt_ref, du_out, dwd_out, dwr_out, dlam_out,
             ds_ref, a_ref):
        k = pl.program_id(2)

        @pl.when(k == 0)
        def _():
            a_ref[...] = cin_ref[...]
            dwd_out[...] = jnp.zeros_like(dwd_out)
            dwr_out[...] = jnp.zeros_like(dwr_out)
            dlam_out[...] = jnp.zeros_like(dlam_out)

        dyb = dy_ref[...].astype(BF16)
        ds_ref[...] = jnp.dot(dyb, wrt_ref[...], preferred_element_type=F32)
        dwr_out[...] += _dot_tn(st_ref[...].astype(BF16), dyb)
        lr = lam_ref[:, :hw]
        li = lam_ref[:, hw:]

        def step(ii, carry):
            ar, ai, gr, gi = carry
            i = ic - 1 - ii
            r0 = pl.multiple_of(i * SCAN_BLOCKS, SCAN_BLOCKS)
            sr = st_ref[pl.ds(r0, SCAN_BLOCKS), :hw]
            si = st_ref[pl.ds(r0, SCAN_BLOCKS), hw:]
            gr = gr + ar * sr + ai * si
            gi = gi + ai * sr - ar * si
            nr = lr * ar - li * ai + ds_ref[pl.ds(r0, SCAN_BLOCKS), :hw]
            ni = lr * ai + li * ar + ds_ref[pl.ds(r0, SCAN_BLOCKS), hw:]
            ds_ref[pl.ds(r0, SCAN_BLOCKS), :hw] = nr
            ds_ref[pl.ds(r0, SCAN_BLOCKS), hw:] = ni
            return nr, ni, gr, gi

        init = (a_ref[:, :hw], a_ref[:, hw:], dlam_out[:, :hw], dlam_out[:, hw:])
        ar, ai, gr, gi = lax.fori_loop(0, ic, step, init, unroll=4 if ic % 4 == 0 else 1)
        a_ref[:, :hw] = ar
        a_ref[:, hw:] = ai
        dlam_out[:, :hw] = gr
        dlam_out[:, hw:] = gi
        ab = ds_ref[...].astype(BF16)
        du_out[...] = jnp.dot(ab, wdt_ref[...], preferred_element_type=F32)
        dwd_out[...] += _dot_tn(u_ref[...].astype(BF16), ab)

    x_spec = pl.BlockSpec((None, rows, HEAD_DIM), lambda dd, t, k: (dd, chunk_idx(k), t))
    w_in = pl.BlockSpec((None, None, HEAD_DIM, sw), lambda dd, t, k: (dd, t, 0, 0))
    w_out = pl.BlockSpec((None, None, sw, HEAD_DIM), lambda dd, t, k: (dd, t, 0, 0))
    vec_spec = pl.BlockSpec((None, None, SCAN_BLOCKS, sw), lambda dd, t, k: (dd, t, 0, 0))
    st_spec = pl.BlockSpec((None, None, rows, sw), lambda dd, t, k: (dd, t, chunk_idx(k), 0))
    out_shape = [jax.ShapeDtypeStruct((2, ll, d), F32), jax.ShapeDtypeStruct((2, nt, HEAD_DIM, sw), F32),
                 jax.ShapeDtypeStruct((2, nt, sw, HEAD_DIM), F32), jax.ShapeDtypeStruct((2, nt, SCAN_BLOCKS, sw), F32)]
    return pl.pallas_call(
        body, grid=(2, nt, n_chunks),
        in_specs=[x_spec, w_in, vec_spec, vec_spec, st_spec, x_spec, w_out],
        out_specs=[x_spec, w_in, w_out, vec_spec], out_shape=out_shape,
        scratch_shapes=[pltpu.VMEM((rows, sw), F32), pltpu.VMEM((SCAN_BLOCKS, sw), F32)], name=name,
        compiler_params=_cparams("parallel", "parallel", "arbitrary"))(dy2, wrt, lamc, cin, st, u2, wdt)


def _cpow(lr, li, n):
    rr, ri = jnp.ones_like(lr), jnp.zeros_like(li)
    br, bi = lr, li
    while n:
        if n & 1:
            rr, ri = _cmul(rr, ri, br, bi)
        br, bi = _cmul(br, bi, br, bi)
        n >>= 1
    return rr, ri


def _resolve_carries(finals, lam, block_len, reverse):
    hw = finals.shape[-1] // 2
    pr, pi = _cpow(lam[:, :, 0, :hw], lam[:, :, 0, hw:], block_len)
    fr, fi = finals[..., :hw], finals[..., hw:]
    cr, ci = jnp.zeros_like(pr), jnp.zeros_like(pi)
    out = [None] * SCAN_BLOCKS
    order = range(SCAN_BLOCKS - 1, -1, -1) if reverse else range(SCAN_BLOCKS)
    for j in order:
        out[j] = jnp.concatenate([cr, ci], axis=-1)
        mr, mi = _cmul(pr, pi, cr, ci)
        cr, ci = mr + fr[:, :, j], mi + fi[:, :, j]
    return jnp.stack(out, axis=2)


def _scan_chunks(ll):
    block_len = ll // SCAN_BLOCKS
    for ic in (132, 128, 96, 64, 48, 36, 32, 24, 16, 8):
        if block_len % ic == 0:
            return block_len // ic
    return 1


def make_s5_core(name):
    def run_fwd(u2, lam, wd, wr):
        ll = u2.shape[1]
        nc = _scan_chunks(ll)
        lam8 = jnp.broadcast_to(lam[:, :, None, :], lam.shape[:2] + (SCAN_BLOCKS, lam.shape[-1]))
        wdb = wd.astype(BF16)
        finals = _s5_scan_call(u2, wdb, lam8, None, None, False, nc, name + "_carry")
        cin = _resolve_carries(finals, lam8, ll // SCAN_BLOCKS, False)
        st, y2 = _s5_scan_call(u2, wdb, lam8, cin, wr.astype(BF16), False, nc, name + "_scan")
        return y2, st, lam8

    @jax.custom_vjp
    def f(u2, lam, wd, wr):
        return run_fwd(u2, lam, wd, wr)[0]

    def fwd(u2, lam, wd, wr):
        y2, st, lam8 = run_fwd(u2, lam, wd, wr)
        return y2, (u2, lam8, wd, wr, st)

    def bwd(res, dy2):
        u2, lam8, wd, wr, st = res
        ll = u2.shape[1]
        nc = _scan_chunks(ll)
        hw = lam8.shape[-1] // 2
        lamc = jnp.concatenate([lam8[..., :hw], -lam8[..., hw:]], axis=-1)
        wrt = jnp.swapaxes(wr, 2, 3).astype(BF16)
        wdt = jnp.swapaxes(wd, 2, 3).astype(BF16)
        finals = _s5_scan_call(dy2, wrt, lamc, None, None, True, nc, name + "_bcarry")
        cin = _resolve_carries(finals, lamc, ll // SCAN_BLOCKS, True)
        du2, dwd, dwr, dlam8 = _s5_bwd_call(dy2, wrt, lamc, cin, st, u2, wdt, nc, name + "_bscan")
        return du2, jnp.sum(dlam8, axis=2), dwd, dwr

    f.defvjp(fwd, bwd)
    return f


def _s5_params(a_re, a_im, log_dt, b_re, b_im, c_re, c_im):
    dt = jnp.exp(log_dt)[..., None]
    mag = jnp.exp(a_re * dt)
    lam_r, lam_i = mag * jnp.cos(a_im * dt), mag * jnp.sin(a_im * dt)
    den = a_re * a_re + a_im * a_im
    nr = lam_r - 1.0
    coef_r = (nr * a_re + lam_i * a_im) / den
    coef_i = (lam_i * a_re - nr * a_im) / den
    bbar_r = coef_r[..., None] * b_re - coef_i[..., None] * b_im
    bbar_i = coef_r[..., None] * b_im + coef_i[..., None] * b_re
    ndir, g, p = lam_r.shape
    tg = SSM_TILE_GROUPS
    nt = g // tg
    eye = jnp.eye(tg, dtype=F32)

    def tile_vec(v):
        return v.reshape(ndir, nt, tg * p)

    lam = jnp.concatenate([tile_vec(lam_r), tile_vec(lam_i)], axis=-1)

    def drive(b):
        bt = b.reshape(ndir, nt, tg, p, SSM_GROUP)
        return (jnp.swapaxes(bt, 3, 4)[:, :, :, :, None, :] * eye[None, None, :, None, :, None]).reshape(ndir, nt, tg * SSM_GROUP, tg * p)

    wd = jnp.concatenate([drive(bbar_r), drive(bbar_i)], axis=-1)

    def readout(c):
        ct = c.reshape(ndir, nt, tg, SSM_GROUP, p)
        return (jnp.swapaxes(ct, 3, 4)[:, :, :, :, None, :] * eye[None, None, :, None, :, None]).reshape(ndir, nt, tg * p, tg * SSM_GROUP)

    wr = jnp.concatenate([readout(c_re), -readout(c_im)], axis=2)
    return lam, wd, wr


def _to_scan_order(seq):
    ll, d = seq.shape
    return seq.reshape(SCAN_BLOCKS, ll // SCAN_BLOCKS, d).swapaxes(0, 1).reshape(ll, d)


def _from_scan_order(y2):
    ll, d = y2.shape
    return y2.reshape(ll // SCAN_BLOCKS, SCAN_BLOCKS, d).swapaxes(0, 1).reshape(ll, d)


def adamw(w, g, m, v, name):
    shape = w.shape
    cols = shape[-1] if len(shape) > 1 else shape[0]
    w2, g2, m2, v2 = (a.reshape(-1, cols) for a in (w, g, m, v))
    r = w2.shape[0]
    cap = max(1, (1024 * 1024) // (4 * cols))
    tr = r
    for t in (512, 256, 128, 64, 32, 16, 8):
        if t <= cap and r % t == 0:
            tr = t
            break
    c1 = 1.0 / (1.0 - ADAM_B1 ** ADAM_STEP)
    c2 = 1.0 / (1.0 - ADAM_B2 ** ADAM_STEP)

    def body(w_ref, g_ref, m_ref, v_ref, d_ref, mo_ref, vo_ref):
        gg = g_ref[...]
        mn = ADAM_B1 * m_ref[...] + (1.0 - ADAM_B1) * gg
        vn = ADAM_B2 * v_ref[...] + (1.0 - ADAM_B2) * (gg * gg)
        d_ref[...] = -ADAM_LR * ((mn * c1) / (jnp.sqrt(vn * c2) + ADAM_EPS) + ADAM_WD * w_ref[...])
        mo_ref[...] = mn
        vo_ref[...] = vn

    spec = pl.BlockSpec((tr, cols), lambda i: (i, 0))
    sds = jax.ShapeDtypeStruct((r, cols), F32)
    d, mn, vn = pl.pallas_call(body, grid=(r // tr,), in_specs=[spec] * 4, out_specs=[spec] * 3, out_shape=[sds] * 3,
                               name=name, compiler_params=_cparams("parallel"))(w2, g2, m2, v2)
    return d.reshape(shape), mn.reshape(shape), vn.reshape(shape)


def _my_pos():
    return lax.axis_index("x"), lax.axis_index("y"), lax.axis_index("c")


def _flip(pos, f):
    return tuple((1 - p) if b else p for p, b in zip(pos, f))


def _lin(pos):
    return 4 * pos[0] + 2 * pos[1] + pos[2]


def xchg(src, n_out, plan, name):
    piece = src.shape[1:]

    def body(src_ref, out_ref, send_sems, recv_sems):
        me = _my_pos()
        copies = []
        for k, (f, sfn, dfn) in enumerate(plan):
            peer = _flip(me, f)
            s_ref = src_ref.at[sfn(me, peer)]
            d_ref = out_ref.at[dfn(me, peer)]
            if any(f):
                cp = pltpu.make_async_remote_copy(src_ref=s_ref, dst_ref=d_ref, send_sem=send_sems.at[k],
                                                  recv_sem=recv_sems.at[k], device_id=peer, device_id_type=MESH)
            else:
                cp = pltpu.make_async_copy(s_ref, d_ref, recv_sems.at[k])
            cp.start()
            copies.append((cp, any(f)))
        for cp, remote in copies:
            if remote:
                cp.wait_recv()
            else:
                cp.wait()
        for cp, remote in copies:
            if remote:
                cp.wait_send()

    return pl.pallas_call(
        body, in_specs=[pl.BlockSpec(memory_space=pl.ANY)], out_specs=pl.BlockSpec(memory_space=pl.ANY),
        out_shape=jax.ShapeDtypeStruct((n_out,) + piece, src.dtype),
        scratch_shapes=[pltpu.SemaphoreType.DMA((len(plan),)), pltpu.SemaphoreType.DMA((len(plan),))], name=name)(src)


_CHIP_FLIPS = ((1, 0, 0), (0, 1, 0), (1, 1, 0))
_ALL_FLIPS = tuple((a, b, c) for a in (0, 1) for b in (0, 1) for c in (0, 1))[1:]


def all_to_all8(src, name):
    plan = [((0, 0, 0), lambda me, peer: _lin(me), lambda me, peer: _lin(me))]
    plan += [(f, lambda me, peer: _lin(peer), lambda me, peer: _lin(me)) for f in _ALL_FLIPS]
    return xchg(src, N_DEV, plan, name)


def all_gather8(piece, name):
    plan = [((0, 0, 0), lambda me, peer: 0, lambda me, peer: _lin(me))]
    plan += [(f, lambda me, peer: 0, lambda me, peer: _lin(me)) for f in _ALL_FLIPS]
    return xchg(piece[None], N_DEV, plan, name)


def gather_weight(shard, name):
    k, ns = shard.shape
    half = shard.astype(BF16).reshape(2, k // 2, ns)

    def slot(pos):
        return 4 * pos[0] + 2 * pos[1] + pos[2]

    plan = [((0, 0, 0), lambda me, peer: me[2], lambda me, peer: slot(me))]
    plan += [(f, lambda me, peer: me[2], lambda me, peer: slot(me)) for f in _CHIP_FLIPS]
    part = xchg(half, 8, plan, name + "_ici")
    plan2 = [((0, 0, 0), lambda me, peer, s=s: 2 * s + me[2], lambda me, peer, s=s: 2 * s + me[2]) for s in range(4)]
    plan2 += [((0, 0, 1), lambda me, peer, s=s: 2 * s + me[2], lambda me, peer, s=s: 2 * s + me[2]) for s in range(4)]
    full = xchg(part, 8, plan2, name + "_d2d")
    return full.reshape(4, k, ns)


def _sum_halves(g8, l1, c_idx, name):
    _, _, r, cc = g8.shape
    tr = _row_tile(r, r, cc)

    def body(c_ref, a_ref, b_ref, o_ref):
        o_ref[...] = (a_ref[...].astype(F32) + b_ref[...].astype(F32)).astype(o_ref.dtype)

    gs = pltpu.PrefetchScalarGridSpec(
        num_scalar_prefetch=1, grid=(4, r // tr),
        in_specs=[pl.BlockSpec((None, None, tr, cc), lambda s, i, c: (s, c[0], i, 0)),
                  pl.BlockSpec((None, tr, cc), lambda s, i, c: (s, i, 0))],
        out_specs=pl.BlockSpec((None, tr, cc), lambda s, i, c: (s, i, 0)))
    return pl.pallas_call(body, grid_spec=gs, out_shape=jax.ShapeDtypeStruct((4, r, cc), BF16), name=name,
                          compiler_params=_cparams("parallel", "parallel"))(c_idx, g8, l1)


def _sum_chips(p4, l2, s_idx, name):
    _, r, cc = p4.shape
    tr = _row_tile(r, r, cc)

    def body(s_ref, a_ref, b0_ref, b1_ref, b2_ref, o_ref):
        o_ref[...] = ((a_ref[...].astype(F32) + b0_ref[...].astype(F32)) + b1_ref[...].astype(F32)) + b2_ref[...].astype(F32)

    gs = pltpu.PrefetchScalarGridSpec(
        num_scalar_prefetch=1, grid=(r // tr,),
        in_specs=[pl.BlockSpec((None, tr, cc), lambda i, s: (s[0], i, 0))]
        + [pl.BlockSpec((None, tr, cc), lambda i, s, j=j: (j, i, 0)) for j in range(3)],
        out_specs=pl.BlockSpec((tr, cc), lambda i, s: (i, 0)))
    return pl.pallas_call(body, grid_spec=gs, out_shape=jax.ShapeDtypeStruct((r, cc), F32), name=name,
                          compiler_params=_cparams("parallel"))(s_idx, p4, l2, l2, l2)


def reduce_scatter_weight(g4, name):
    _, k, ns = g4.shape
    x, y, c = _my_pos()
    c_idx = jnp.reshape(c, (1,)).astype(jnp.int32)
    s_idx = jnp.reshape(2 * x + y, (1,)).astype(jnp.int32)
    g8 = g4.reshape(8, k // 2, ns)
    plan1 = [((0, 0, 1), lambda me, peer, s=s: 2 * s + peer[2], lambda me, peer, s=s: s) for s in range(4)]
    l1 = xchg(g8, 4, plan1, name + "_d2d")
    p4 = _sum_halves(g4.reshape(4, 2, k // 2, ns), l1, c_idx, name + "_sum2")
    plan2 = [(f, lambda me, peer: 2 * peer[0] + peer[1], lambda me, peer, j=j: j) for j, f in enumerate(_CHIP_FLIPS)]
    l2 = xchg(p4, 3, plan2, name + "_ici")
    rh = _sum_chips(p4, l2, s_idx, name + "_sum4")
    plan3 = [((0, 0, 0), lambda me, peer: 0, lambda me, peer: me[2]), ((0, 0, 1), lambda me, peer: 0, lambda me, peer: me[2])]
    out = xchg(rh[None], 2, plan3, name + "_swap")
    return out.reshape(k, ns)


def _sum8(a8, name):
    _, r, cc = a8.shape
    tr = _row_tile(r, r, cc)

    def body(a_ref, o_ref):
        acc = a_ref[0]
        for j in range(1, N_DEV):
            acc = acc + a_ref[j]
        o_ref[...] = acc

    return pl.pallas_call(body, grid=(r // tr,), in_specs=[pl.BlockSpec((N_DEV, tr, cc), lambda i: (0, i, 0))],
                          out_specs=pl.BlockSpec((tr, cc), lambda i: (i, 0)), out_shape=jax.ShapeDtypeStruct((r, cc), F32),
                          name=name, compiler_params=_cparams("parallel"))(a8)


def all_reduce8(flat, name):
    n = flat.shape[0]
    unit = N_DEV * 8 * 128
    npad = -(-n // unit) * unit
    a = jnp.pad(flat, (0, npad - n)).reshape(N_DEV, npad // (N_DEV * 128), 128)
    mine = _sum8(all_to_all8(a, name + "_rs"), name + "_sum")
    return all_gather8(mine, name + "_ag").reshape(npad)[:n]


def _local_loss(x, ctx, target, mods, small, big, dims):
    t, m, d = dims["t"], dims["m"], dims["d"]
    a_w, bq_w, bkv_w = dims["a_w"], dims["bq_w"], dims["bkv_w"]
    r = t + m
    z = jnp.concatenate([x, ctx], axis=0)

    def grp(layer, j, n_groups=2):
        return mods[layer, :n_groups, j][:, None, :]

    h = make_norm_mod(t, "norm_mix0")(z, small["norm_mix"][0][None], grp(0, 1), grp(0, 0))
    qkv = make_linear("col", BF16, "attn_in")(h, big["attn_w_in"])
    o1, o2, o3, o4, o5 = a_w, 2 * a_w, 3 * a_w, 3 * a_w + bq_w, 3 * a_w + bq_w + bkv_w
    qa, ka, va = qkv[:t, :o1], qkv[:t, o1:o2], qkv[:t, o2:o3]
    qa_c, ka_c, va_c = qkv[t:, :o1], qkv[t:, o1:o2], qkv[t:, o2:o3]
    qkb = make_rope(t, "rope")(qkv[:t, o3:o5])
    qb, kb, vb = qkb[:, :bq_w], qkb[:, bq_w:], qkv[:t, o5:]
    qb_c, kb_c, vb_c = qkv[t:, o3:o4], qkv[t:, o4:o5], qkv[t:, o5:]
    sink = small["attn_sink"][0]
    no_sink = jnp.zeros((a_w // HEAD_DIM,), F32)
    na_meta, na_span, na_cases = _na_plan(t)
    oa = make_attention(na_meta, 1, na_span, True, False, True, "na")(
        qa, ka, va, ka_c, va_c, _na_bias(small["attn_rpb"][0], na_cases), no_sink)
    sw_meta, sw_span, sw_bias = _sw_plan(t)
    grp_b = bq_w // bkv_w
    ob = make_attention(sw_meta, grp_b, sw_span, True, True, False, "swa")(qb, kb, vb, kb_c, vb_c, jnp.asarray(sw_bias), sink)
    c_meta = np.array([[0] * (m // ATTN_BLOCK), [0] * (m // ATTN_BLOCK), [1] + [0] * (m // ATTN_BLOCK - 1)], np.int32)
    zero_bias = jnp.zeros((1, 1, ATTN_BLOCK, m), F32)
    oa_c = make_attention(c_meta, 1, m, False, False, False, "ctx_na")(qa_c, ka_c, va_c, ka_c, va_c, zero_bias, no_sink)
    ob_c = make_attention(c_meta, grp_b, m, False, True, False, "ctx_swa")(qb_c, kb_c, vb_c, kb_c, vb_c, zero_bias, sink)
    o = jnp.concatenate([jnp.concatenate([oa, ob], axis=1), jnp.concatenate([oa_c, ob_c], axis=1)], axis=0)
    y = make_linear("row", F32, "attn_out")(o, big["attn_w_out"])
    z = make_gated_residual(t, "res_mix0")(z, y, grp(0, 2))
    h = make_norm_mod(t, "norm_ffn0")(z, small["norm_ffn"][0][None], grp(0, 4), grp(0, 3))
    a = make_swiglu_act("act0")(make_linear("col", BF16, "ffn0_w1")(h, big["ffn_w1_0"]),
                                make_linear("col", BF16, "ffn0_w3")(h, big["ffn_w3_0"]))
    z = make_gated_residual(t, "res_ffn0")(z, make_linear("row", F32, "ffn0_w2")(a, big["ffn_w2_0"]), grp(0, 5))

    h = make_norm_mod(t, "norm_mix1")(z, small["norm_mix"][1][None], grp(1, 1), grp(1, 0))
    hx, hc = h[:t], h[t:]
    lam, wd, wr = _s5_params(small["ssm_a_re"][0], small["ssm_a_im"][0], small["ssm_log_dt"][0], small["ssm_b_re"][0],
                             small["ssm_b_im"][0], small["ssm_c_re"][0], small["ssm_c_im"][0])
    hf = h.astype(F32)
    seq_f = jnp.concatenate([hf[t:], hf[:t]], axis=0)
    seq_r = jnp.concatenate([hf[t:][::-1], hf[:t][::-1]], axis=0)
    u2 = jnp.stack([_to_scan_order(seq_f), _to_scan_order(seq_r)])
    y2 = make_s5_core("s5")(u2, lam, wd, wr)
    ys = _from_scan_order(y2[0])[m:] + _from_scan_order(y2[1])[m:][::-1]
    gl = make_gelu_in("gelu")(hx, ys, small["ssm_d_full"][None])
    zz = make_linear("col", F32, "glu_w")(gl, big["ssm_w_glu"])
    yx = make_glu("glu")(zz, small["ssm_b_glu_full"][None])
    xs = make_gated_residual(t, "res_mix1")(z[:t], yx, grp(1, 2, 1))
    h = make_norm_mod(t, "norm_ffn1")(xs, small["norm_ffn"][1][None], grp(1, 4, 1), grp(1, 3, 1))
    a = make_swiglu_act("act1")(make_linear("col", BF16, "ffn1_w1")(h, big["ffn_w1_1"]),
                                make_linear("col", BF16, "ffn1_w3")(h, big["ffn_w3_1"]))
    xs = make_gated_residual(t, "res_ffn1")(xs, make_linear("row", F32, "ffn1_w2")(a, big["ffn_w2_1"]), grp(1, 5, 1))
    return make_final_loss("loss_head")(xs, small["norm_final"][None], target)[0, 0]


_WEIGHTS = ['c_ctx', 'ada_w', 'ada_b', 'norm_mix', 'norm_ffn', 'ffn_w1', 'ffn_w3', 'ffn_w2', 'attn_w_in', 'attn_w_out',
            'attn_rpb', 'attn_sink', 'ssm_a_re', 'ssm_a_im', 'ssm_log_dt', 'ssm_b_re', 'ssm_b_im', 'ssm_c_re', 'ssm_c_im',
            'ssm_d', 'ssm_w_glu', 'ssm_b_glu', 'norm_final']
_LOCAL_SMALL = ['norm_mix', 'norm_ffn', 'attn_rpb', 'attn_sink', 'ssm_a_re', 'ssm_a_im', 'ssm_log_dt', 'ssm_b_re',
                'ssm_b_im', 'ssm_c_re', 'ssm_c_im', 'norm_final']
_MOD_ROWS = 16


def _gather_chip_vector(v, name):
    g = all_gather8(v[None], name)
    return g[0::2, 0, :].reshape(-1)


def kernel(x, c, ctx, c_ctx, ada_w, ada_b, norm_mix, norm_ffn, ffn_w1, ffn_w3, ffn_w2, attn_w_in, attn_w_out, attn_rpb, attn_sink, ssm_a_re, ssm_a_im, ssm_log_dt, ssm_b_re, ssm_b_im, ssm_c_re, ssm_c_im, ssm_d, ssm_w_glu, ssm_b_glu, norm_final, loss_target, m_c_ctx, m_ada_w, m_ada_b, m_norm_mix, m_norm_ffn, m_ffn_w1, m_ffn_w3, m_ffn_w2, m_attn_w_in, m_attn_w_out, m_attn_rpb, m_attn_sink, m_ssm_a_re, m_ssm_a_im, m_ssm_log_dt, m_ssm_b_re, m_ssm_b_im, m_ssm_c_re, m_ssm_c_im, m_ssm_d, m_ssm_w_glu, m_ssm_b_glu, m_norm_final, v_c_ctx, v_ada_w, v_ada_b, v_norm_mix, v_norm_ffn, v_ffn_w1, v_ffn_w3, v_ffn_w2, v_attn_w_in, v_attn_w_out, v_attn_rpb, v_attn_sink, v_ssm_a_re, v_ssm_a_im, v_ssm_log_dt, v_ssm_b_re, v_ssm_b_im, v_ssm_c_re, v_ssm_c_im, v_ssm_d, v_ssm_w_glu, v_ssm_b_glu, v_norm_final):
    env = dict(locals())
    w = {n: env[n] for n in _WEIGHTS}
    mom = {n: env["m_" + n] for n in _WEIGHTS}
    var = {n: env["v_" + n] for n in _WEIGHTS}
    _, t, d = x.shape
    m = ctx.shape[1]
    px, py, pc = _my_pos()
    s_me = 2 * px + py
    a_w =attn_rpb.shape[1] * HEAD_DIM
    bq_w = attn_sink.shape[1] * HEAD_DIM
    bkv_w = (4 * attn_w_in.shape[2] - 3 * a_w - bq_w) // 2
    dims = dict(t=t, m=m, d=d, a_w=a_w, bq_w=bq_w, bkv_w=bkv_w)
    n_layers = ada_w.shape[0]
    ada_cols = ada_w.shape[2]

    big = {"attn_w_in": gather_weight(attn_w_in[0], "ag_attn_in"), "attn_w_out": gather_weight(attn_w_out[0], "ag_attn_out"),
           "ssm_w_glu": gather_weight(ssm_w_glu[0], "ag_glu")}
    for l in range(n_layers):
        big[f"ffn_w1_{l}"] = gather_weight(ffn_w1[l], f"ag_w1_{l}")
        big[f"ffn_w3_{l}"] = gather_weight(ffn_w3[l], f"ag_w3_{l}")
        big[f"ffn_w2_{l}"] = gather_weight(ffn_w2[l], f"ag_w2_{l}")
    small = {n: w[n] for n in _LOCAL_SMALL}
    small["ssm_d_full"] = _gather_chip_vector(ssm_d[0], "ag_ssm_d")
    small["ssm_b_glu_full"] = _gather_chip_vector(ssm_b_glu[0], "ag_b_glu")

    c_all = all_gather8(c, "ag_c")[:, 0, :]
    cond = jnp.concatenate([c_all, c_ctx[None], jnp.zeros((_MOD_ROWS - N_DEV - 1, d), F32)], axis=0)
    sig = jax.nn.sigmoid(cond)
    silu_c = (cond * sig).astype(BF16)
    ada_wb = ada_w.astype(BF16)
    mods_shard = jnp.stack([mm_nn(silu_c, ada_wb[l][None], "col", F32, f"ada_fwd{l}") for l in range(n_layers)])
    send = jnp.stack([mods_shard[:, jnp.array([tgt, N_DEV]), :].reshape(2 * n_layers, ada_cols) for tgt in range(N_DEV)])
    plan = [((0, 0, 0), lambda me, peer: _lin(me), lambda me, peer: 2 * me[0] + me[1])]
    plan += [(f, lambda me, peer: _lin(peer), lambda me, peer: 2 * me[0] + me[1]) for f in _CHIP_FLIPS]
    got = xchg(send, 4, plan, "mods_xchg")
    mods = got.reshape(4, n_layers, 2, ada_cols).transpose(1, 2, 0, 3).reshape(n_layers, 2, 4 * ada_cols)
    mods = (mods + ada_b[:, None, :]).reshape(n_layers, 2, 6, d)

    def local(xx, mods_, small_, big_):
        return _local_loss(xx, ctx[0], loss_target[0], mods_, small_, big_, dims)

    loss_local, vjp = jax.vjp(local, x[0], mods, small, big)
    g_x, g_mods, g_small, g_big = vjp(jnp.ones((), F32))
    loss = lax.psum(loss_local, ("x", "y", "c"))

    grads = {}
    grads["attn_w_in"] = reduce_scatter_weight(g_big["attn_w_in"], "rs_attn_in")[None]
    grads["attn_w_out"] = reduce_scatter_weight(g_big["attn_w_out"], "rs_attn_out")[None]
    grads["ssm_w_glu"] = reduce_scatter_weight(g_big["ssm_w_glu"], "rs_glu")[None]
    for n in ("ffn_w1", "ffn_w3", "ffn_w2"):
        grads[n] = jnp.stack([reduce_scatter_weight(g_big[f"{n}_{l}"], f"rs_{n}_{l}") for l in range(n_layers)])

    gm = all_gather8(g_mods.reshape(2 * n_layers, 6 * d), "ag_dmods").reshape(N_DEV, n_layers, 2, 6 * d)
    ctx_row = gm[0, :, 1]
    for j in range(1, N_DEV):
        ctx_row = ctx_row + gm[j, :, 1]
    dm16 = jnp.concatenate([gm[:, :, 0].transpose(1, 0, 2), ctx_row[:, None], jnp.zeros((n_layers, _MOD_ROWS - N_DEV - 1, 6 * d), F32)], axis=1)
    grads["ada_b"] = jnp.sum(dm16, axis=1)
    dm_mine = lax.dynamic_slice_in_dim(dm16, s_me * ada_cols, ada_cols, axis=2).astype(BF16)
    grads["ada_w"] = jnp.stack([mm_tn(silu_c.T, dm_mine[l], (1, d, ada_cols), "col", F32, f"ada_dw{l}")[0] for l in range(n_layers)])
    dsilu = mm_nt(dm_mine[0], ada_wb[0][None], "col", F32, "ada_dc0")
    for l in range(1, n_layers):
        dsilu = dsilu + mm_nt(dm_mine[l], ada_wb[l][None], "col", F32, f"ada_dc{l}")
    dsilu_ctx = 0.5 * dsilu[N_DEV]

    packed = [(n, g_small[n]) for n in _LOCAL_SMALL] + [("ssm_d", g_small["ssm_d_full"]), ("ssm_b_glu", g_small["ssm_b_glu_full"]),
                                                        ("c_ctx", dsilu_ctx)]
    flat = all_reduce8(jnp.concatenate([a.reshape(-1) for _, a in packed]), "ar_small")
    off = 0
    for n, a in packed:
        grads[n] = flat[off:off + a.size].reshape(a.shape)
        off += a.size
    sig_ctx = jax.nn.sigmoid(c_ctx)
    grads["c_ctx"] = grads["c_ctx"] * (sig_ctx * (1.0 + c_ctx * (1.0 - sig_ctx)))
    grads["ssm_d"] = lax.dynamic_slice_in_dim(grads["ssm_d"], s_me * ssm_d.shape[1], ssm_d.shape[1])[None]
    grads["ssm_b_glu"] = lax.dynamic_slice_in_dim(grads["ssm_b_glu"], s_me * ssm_b_glu.shape[1], ssm_b_glu.shape[1])[None]

    delta, new_m, new_v = {}, {}, {}
    for n in _WEIGHTS:
        delta[n], new_m[n], new_v[n] = adamw(w[n], grads[n], mom[n], var[n], "adamw_" + n)
    return (loss, g_x[None], *[grads[n] for n in _WEIGHTS], *[delta[n] for n in _WEIGHTS],
            *[new_m[n] for n in _WEIGHTS], *[new_v[n] for n in _WEIGHTS])
```

```python
import functools
import math

import numpy as np
import jax
import jax.numpy as jnp
from jax import lax
from jax.experimental import pallas as pl
from jax.experimental.pallas import tpu as pltpu

F32 = jnp.float32
BF16 = jnp.bfloat16
MESH = pl.DeviceIdType.MESH

HEAD_DIM = 128
GRID_W = 64
NA_ROWS = 8
NA_COLS = 16
SW_RADIUS = 128
ATTN_BLOCK = 128
ROPE_BASE = 10000.0
SSM_GROUP = 16
SSM_STATE = 64
SSM_TILE_GROUPS = 8
SCAN_BLOCKS = 8
EPS = 1e-6
NEG_INF = -1e30
ADAM_LR, ADAM_B1, ADAM_B2, ADAM_EPS, ADAM_WD, ADAM_STEP = 0.001, 0.9, 0.999, 1e-08, 0.01, 10
VMEM_LIMIT_BYTES = 56 * 1024 * 1024
N_DEV = 8


def _cparams(*sem):
    return pltpu.CompilerParams(dimension_semantics=tuple(sem) if sem else None, vmem_limit_bytes=VMEM_LIMIT_BYTES)


def _pick(n, cands):
    for c in cands:
        if n % c == 0:
            return c
    return n


def _mm_call(name, grid, a, b, a_spec, b_spec, o_spec, out_sds, acc_shape, nt):
    nk = grid[2]

    def body(a_ref, b_ref, o_ref, acc_ref):
        kk = pl.program_id(2)
        if nt:
            p = lax.dot_general(a_ref[...], b_ref[...], (((1,), (1,)), ((), ())), preferred_element_type=F32)
        else:
            p = jnp.dot(a_ref[...], b_ref[...], preferred_element_type=F32)
        if nk == 1:
            o_ref[...] = p.astype(o_ref.dtype)
        else:
            @pl.when(kk == 0)
            def _():
                acc_ref[...] = p

            @pl.when(kk > 0)
            def _():
                acc_ref[...] += p

            @pl.when(kk == nk - 1)
            def _():
                o_ref[...] = acc_ref[...].astype(o_ref.dtype)

    return pl.pallas_call(
        body, grid=grid, in_specs=[a_spec, b_spec], out_specs=o_spec, out_shape=out_sds,
        scratch_shapes=[pltpu.VMEM(acc_shape, F32)], name=name,
        compiler_params=_cparams("parallel", "parallel", "arbitrary"))(a, b)


_ROW_TILES = (768, 512, 256, 128, 64, 32, 16, 8)
_K_TILES = (2048, 1408, 1024, 512, 256, 128)


def mm_nn(a, w3, kind, out_dtype, name):
    r = a.shape[0]
    s, d1, d2 = w3.shape
    tm = _pick(r, _ROW_TILES)
    if kind == "col":
        tk = d1 if d1 <= 2048 else _pick(d1, _K_TILES)
        grid = (s, r // tm, d1 // tk)
        a_spec = pl.BlockSpec((tm, tk), lambda j, i, k: (i, k))
        b_spec = pl.BlockSpec((None, tk, d2), lambda j, i, k: (j, k, 0))
        o_spec = pl.BlockSpec((tm, d2), lambda j, i, k: (i, j))
        n = s * d2
    else:
        grid = (1, r // tm, s)
        a_spec = pl.BlockSpec((tm, d1), lambda j, i, k: (i, k))
        b_spec = pl.BlockSpec((None, d1, d2), lambda j, i, k: (k, 0, 0))
        o_spec = pl.BlockSpec((tm, d2), lambda j, i, k: (i, 0))
        n = d2
    return _mm_call(name, grid, a, w3, a_spec, b_spec, o_spec, jax.ShapeDtypeStruct((r, n), out_dtype), (tm, d2), False)


def mm_nt(dy, w3, kind, out_dtype, name):
    r = dy.shape[0]
    s, d1, d2 = w3.shape
    tm = _pick(r, _ROW_TILES)
    if kind == "col":
        tko = d1 if d1 <= 2048 else _pick(d1, _K_TILES)
        grid = (d1 // tko, r // tm, s)
        a_spec = pl.BlockSpec((tm, d2), lambda j, i, k: (i, k))
        b_spec = pl.BlockSpec((None, tko, d2), lambda j, i, k: (k, j, 0))
        o_spec = pl.BlockSpec((tm, tko), lambda j, i, k: (i, j))
        kdim, acc = d1, (tm, tko)
    else:
        grid = (s, r // tm, 1)
        a_spec = pl.BlockSpec((tm, d2), lambda j, i, k: (i, 0))
        b_spec = pl.BlockSpec((None, d1, d2), lambda j, i, k: (j, 0, 0))
        o_spec = pl.BlockSpec((tm, d1), lambda j, i, k: (i, j))
        kdim, acc = s * d1, (tm, d1)
    return _mm_call(name, grid, dy, w3, a_spec, b_spec, o_spec, jax.ShapeDtypeStruct((r, kdim), out_dtype), acc, True)


def mm_tn(at, dy, w_shape, kind, out_dtype, name):
    s, d1, d2 = w_shape
    r = dy.shape[0]
    tr = _pick(r, (1024, 768, 512, 256, 128))
    if kind == "col":
        tkk = d1 if d1 <= 2048 else _pick(d1, _K_TILES)
        grid = (s * (d1 // tkk), 1, r // tr)
        nkk = d1 // tkk
        a_spec = pl.BlockSpec((tkk, tr), lambda j, i, k: (j % nkk, k))
        b_spec = pl.BlockSpec((tr, d2), lambda j, i, k: (k, j // nkk))
        o_spec = pl.BlockSpec((None, tkk, d2), lambda j, i, k: (j // nkk, j % nkk, 0))
        acc = (tkk, d2)
    else:
        grid = (s, 1, r // tr)
        a_spec = pl.BlockSpec((d1, tr), lambda j, i, k: (j, k))
        b_spec = pl.BlockSpec((tr, d2), lambda j, i, k: (k, 0))
        o_spec = pl.BlockSpec((None, d1, d2), lambda j, i, k: (j, 0, 0))
        acc = (d1, d2)
    return _mm_call(name, grid, at, dy, a_spec, b_spec, o_spec, jax.ShapeDtypeStruct(w_shape, out_dtype), acc, False)


def make_linear(kind, out_dtype, name):
    @jax.custom_vjp
    def linear(a, w3):
        return mm_nn(a, w3, kind, out_dtype, name + "_fwd")

    def fwd(a, w3):
        return mm_nn(a, w3, kind, out_dtype, name + "_fwd"), (a, w3)

    def bwd(res, dy):
        a, w3 = res
        dyb = dy.astype(BF16)
        da = mm_nt(dyb, w3, kind, a.dtype, name + "_dx")
        dw = mm_tn(a.T, dyb, w3.shape, kind, w3.dtype, name + "_dw")
        return da, dw

    linear.defvjp(fwd, bwd)
    return linear


def _row_tile(r, t0, d):
    cap = max(8, (2 * 1024 * 1024) // (4 * d))
    cands = [t for t in (1024, 512, 256, 128, 64, 32, 16, 8) if t <= cap]
    for t in cands:
        if r % t == 0 and t0 % t == 0:
            return t
    raise ValueError("no row tile")


def _grp_spec(d, nb0):
    return pl.BlockSpec((None, 1, d), lambda i: (i // nb0, 0, 0))


def _norm_mod_fwd(z, g, scale, shift, t0, name):
    r, d = z.shape
    tr = _row_tile(r, t0, d)
    nb0 = t0 // tr

    def body(z_ref, g_ref, sc_ref, sh_ref, o_ref):
        zz = z_ref[...]
        rstd = lax.rsqrt(jnp.mean(zz * zz, axis=-1, keepdims=True) + EPS)
        y = zz * rstd * g_ref[...]
        o_ref[...] = (y * (1.0 + sc_ref[...]) + sh_ref[...]).astype(o_ref.dtype)

    return pl.pallas_call(
        body, grid=(r // tr,),
        in_specs=[pl.BlockSpec((tr, d), lambda i: (i, 0)), pl.BlockSpec((1, d), lambda i: (0, 0)),
                  _grp_spec(d, nb0), _grp_spec(d, nb0)],
        out_specs=pl.BlockSpec((tr, d), lambda i: (i, 0)),
        out_shape=jax.ShapeDtypeStruct((r, d), BF16), name=name, compiler_params=_cparams("parallel"))(z, g, scale, shift)


def _norm_mod_bwd(z, g, scale, dh, t0, name):
    r, d = z.shape
    ng = scale.shape[0]
    tr = _row_tile(r, t0, d)
    nb0 = t0 // tr

    def body(z_ref, g_ref, sc_ref, dh_ref, dz_ref, dg_ref, dsc_ref, dsh_ref):
        i = pl.program_id(0)
        zz = z_ref[...]
        gg = g_ref[...]
        rstd = lax.rsqrt(jnp.mean(zz * zz, axis=-1, keepdims=True) + EPS)
        zhat = zz * rstd
        dhh = dh_ref[...].astype(F32)
        dy = dhh * (1.0 + sc_ref[...])
        dyg = dy * gg
        dz_ref[...] = rstd * (dyg - zhat * jnp.mean(dyg * zhat, axis=-1, keepdims=True))

        @pl.when(i == 0)
        def _():
            dg_ref[...] = jnp.zeros_like(dg_ref)

        @pl.when((i == 0) | (i == nb0))
        def _():
            dsc_ref[...] = jnp.zeros_like(dsc_ref)
            dsh_ref[...] = jnp.zeros_like(dsh_ref)

        dg_ref[...] += jnp.sum(dy * zhat, axis=0, keepdims=True)
        dsc_ref[...] += jnp.sum(dhh * (zhat * gg), axis=0, keepdims=True)
        dsh_ref[...] += jnp.sum(dhh, axis=0, keepdims=True)

    return pl.pallas_call(
        body, grid=(r // tr,),
        in_specs=[pl.BlockSpec((tr, d), lambda i: (i, 0)), pl.BlockSpec((1, d), lambda i: (0, 0)),
                  _grp_spec(d, nb0), pl.BlockSpec((tr, d), lambda i: (i, 0))],
        out_specs=[pl.BlockSpec((tr, d), lambda i: (i, 0)), pl.BlockSpec((1, d), lambda i: (0, 0)),
                   _grp_spec(d, nb0), _grp_spec(d, nb0)],
        out_shape=[jax.ShapeDtypeStruct((r, d), F32), jax.ShapeDtypeStruct((1, d), F32),
                   jax.ShapeDtypeStruct((ng, 1, d), F32), jax.ShapeDtypeStruct((ng, 1, d), F32)],
        name=name, compiler_params=_cparams("arbitrary"))(z, g, scale, dh)


def make_norm_mod(t0, name):
    @jax.custom_vjp
    def f(z, g, scale, shift):
        return _norm_mod_fwd(z, g, scale, shift, t0, name + "_fwd")

    def fwd(z, g, scale, shift):
        return _norm_mod_fwd(z, g, scale, shift, t0, name + "_fwd"), (z, g, scale)

    def bwd(res, dh):
        z, g, scale = res
        dz, dg, dsc, dsh = _norm_mod_bwd(z, g, scale, dh, t0, name + "_bwd")
        return dz, dg, dsc, dsh

    f.defvjp(fwd, bwd)
    return f


def _gated_fwd(z, y, gate, t0, name):
    r, d = z.shape
    tr = _row_tile(r, t0, d)
    nb0 = t0 // tr

    def body(z_ref, y_ref, g_ref, o_ref):
        o_ref[...] = z_ref[...] + g_ref[...] * y_ref[...].astype(F32)

    return pl.pallas_call(
        body, grid=(r // tr,),
        in_specs=[pl.BlockSpec((tr, d), lambda i: (i, 0)), pl.BlockSpec((tr, d), lambda i: (i, 0)), _grp_spec(d, nb0)],
        out_specs=pl.BlockSpec((tr, d), lambda i: (i, 0)),
        out_shape=jax.ShapeDtypeStruct((r, d), F32), name=name, compiler_params=_cparams("parallel"))(z, y, gate)


def _gated_bwd(y, gate, dzn, t0, name):
    r, d = y.shape
    ng = gate.shape[0]
    tr = _row_tile(r, t0, d)
    nb0 = t0 // tr

    def body(y_ref, g_ref, dz_ref, dy_ref, dg_ref):
        i = pl.program_id(0)
        dzz = dz_ref[...]
        dy_ref[...] = (g_ref[...] * dzz).astype(dy_ref.dtype)

        @pl.when((i == 0) | (i == nb0))
        def _():
            dg_ref[...] = jnp.zeros_like(dg_ref)

        dg_ref[...] += jnp.sum(dzz * y_ref[...].astype(F32), axis=0, keepdims=True)

    return pl.pallas_call(
        body, grid=(r // tr,),
        in_specs=[pl.BlockSpec((tr, d), lambda i: (i, 0)), _grp_spec(d, nb0), pl.BlockSpec((tr, d), lambda i: (i, 0))],
        out_specs=[pl.BlockSpec((tr, d), lambda i: (i, 0)), _grp_spec(d, nb0)],
        out_shape=[jax.ShapeDtypeStruct((r, d), y.dtype), jax.ShapeDtypeStruct((ng, 1, d), F32)],
        name=name, compiler_params=_cparams("arbitrary"))(y, gate, dzn)


def make_gated_residual(t0, name):
    @jax.custom_vjp
    def f(z, y, gate):
        return _gated_fwd(z, y, gate, t0, name + "_fwd")

    def fwd(z, y, gate):
        return _gated_fwd(z, y, gate, t0, name + "_fwd"), (y, gate)

    def bwd(res, dzn):
        y, gate = res
        dy, dgate = _gated_bwd(y, gate, dzn, t0, name + "_bwd")
        return dzn, dy, dgate

    f.defvjp(fwd, bwd)
    return f


def _ew_call(name, body, ins, outs_sds, r, widths_in, widths_out, tr, extra_in=(), extra_specs=(), sem="parallel"):
    in_specs = [pl.BlockSpec((tr, w), lambda i: (i, 0)) for w in widths_in] + list(extra_specs)
    out_specs = [pl.BlockSpec((tr, w), lambda i: (i, 0)) if w is not None else pl.BlockSpec(s.shape, lambda i: (0,) * len(s.shape))
                 for w, s in zip(widths_out, outs_sds)]
    return pl.pallas_call(body, grid=(r // tr,), in_specs=in_specs, out_specs=out_specs, out_shape=outs_sds,
                          name=name, compiler_params=_cparams(sem))(*ins, *extra_in)


def _silu(x):
    return x * jax.nn.sigmoid(x)


def make_swiglu_act(name):
    def fwd_call(h1, h3):
        r, f = h1.shape
        tr = _row_tile(r, r, f)

        def body(a_ref, b_ref, o_ref):
            o_ref[...] = (_silu(a_ref[...].astype(F32)) * b_ref[...].astype(F32)).astype(o_ref.dtype)

        return _ew_call(name + "_fwd", body, (h1, h3), [jax.ShapeDtypeStruct((r, f), BF16)], r, (f, f), (f,), tr)[0]

    @jax.custom_vjp
    def act(h1, h3):
        return fwd_call(h1, h3)

    def fwd(h1, h3):
        return fwd_call(h1, h3), (h1, h3)

    def bwd(res, da):
        h1, h3 = res
        r, f = h1.shape
        tr = _row_tile(r, r, f)

        def body(a_ref, b_ref, da_ref, d1_ref, d3_ref):
            a = a_ref[...].astype(F32)
            b = b_ref[...].astype(F32)
            g = da_ref[...].astype(F32)
            sg = jax.nn.sigmoid(a)
            d1_ref[...] = (g * b * (sg * (1.0 + a * (1.0 - sg)))).astype(d1_ref.dtype)
            d3_ref[...] = (g * a * sg).astype(d3_ref.dtype)

        sds = jax.ShapeDtypeStruct((r, f), BF16)
        return tuple(_ew_call(name + "_bwd", body, (h1, h3, da), [sds, sds], r, (f, f, f), (f, f), tr))

    act.defvjp(fwd, bwd)
    return act


_GELU_C = math.sqrt(2.0 / math.pi)


def _gelu_and_grad(y):
    inner = _GELU_C * (y + 0.044715 * y * y * y)
    t = jnp.tanh(inner)
    val = 0.5 * y * (1.0 + t)
    grad = 0.5 * (1.0 + t) + 0.5 * y * (1.0 - t * t) * _GELU_C * (1.0 + 3 * 0.044715 * y * y)
    return val, grad


def make_gelu_in(name):
    def fwd_call(u, ys, dsk):
        r, d = u.shape
        tr = _row_tile(r, r, d)

        def body(u_ref, y_ref, d_ref, o_ref):
            y = d_ref[...] * u_ref[...].astype(F32) + y_ref[...]
            o_ref[...] = _gelu_and_grad(y)[0].astype(o_ref.dtype)

        return _ew_call(name + "_fwd", body, (u, ys), [jax.ShapeDtypeStruct((r, d), BF16)], r, (d, d), (d,), tr,
                        extra_in=(dsk,), extra_specs=(pl.BlockSpec((1, d), lambda i: (0, 0)),))[0]

    @jax.custom_vjp
    def f(u, ys, dsk):
        return fwd_call(u, ys, dsk)

    def fwd(u, ys, dsk):
        return fwd_call(u, ys, dsk), (u, ys, dsk)

    def bwd(res, dg):
        u, ys, dsk = res
        r, d = u.shape
        tr = _row_tile(r, r, d)

        def body(u_ref, y_ref, dg_ref, d_ref, du_ref, dy_ref, dd_ref):
            i = pl.program_id(0)
            uu = u_ref[...].astype(F32)
            y = d_ref[...] * uu + y_ref[...]
            dy = dg_ref[...].astype(F32) * _gelu_and_grad(y)[1]
            dy_ref[...] = dy
            du_ref[...] = (d_ref[...] * dy).astype(du_ref.dtype)

            @pl.when(i == 0)
            def _():
                dd_ref[...] = jnp.zeros_like(dd_ref)

            dd_ref[...] += jnp.sum(dy * uu, axis=0, keepdims=True)

        outs = [jax.ShapeDtypeStruct((r, d), u.dtype), jax.ShapeDtypeStruct((r, d), F32), jax.ShapeDtypeStruct((1, d), F32)]
        du, dy, dd = _ew_call(name + "_bwd", body, (u, ys, dg), outs, r, (d, d, d), (d, d, None), tr,
                              extra_in=(dsk,), extra_specs=(pl.BlockSpec((1, d), lambda i: (0, 0)),), sem="arbitrary")
        return du, dy, dd

    f.defvjp(fwd, bwd)
    return f


def make_glu(name):
    def fwd_call(z, b):
        r, d2 = z.shape
        d = d2 // 2
        tr = _row_tile(r, r, d2)

        def body(z_ref, b_ref, o_ref):
            zz = z_ref[...].astype(F32) + b_ref[...]
            o_ref[...] = zz[:, :d] * jax.nn.sigmoid(zz[:, d:])

        return _ew_call(name + "_fwd", body, (z,), [jax.ShapeDtypeStruct((r, d), F32)], r, (d2,), (d,), tr,
                        extra_in=(b,), extra_specs=(pl.BlockSpec((1, d2), lambda i: (0, 0)),))[0]

    @jax.custom_vjp
    def f(z, b):
        return fwd_call(z, b)

    def fwd(z, b):
        return fwd_call(z, b), (z, b)

    def bwd(res, do):
        z, b = res
        r, d2 = z.shape
        d = d2 // 2
        tr = _row_tile(r, r, d2)

        def body(z_ref, do_ref, b_ref, dz_ref, db_ref):
            i = pl.program_id(0)
            zz = z_ref[...].astype(F32) + b_ref[...]
            sg = jax.nn.sigmoid(zz[:, d:])
            g = do_ref[...]
            dza = g * sg
            dzb = g * zz[:, :d] * sg * (1.0 - sg)
            dz_ref[:, :d] = dza.astype(dz_ref.dtype)
            dz_ref[:, d:] = dzb.astype(dz_ref.dtype)

            @pl.when(i == 0)
            def _():
                db_ref[...] = jnp.zeros_like(db_ref)

            db_ref[:, :d] += jnp.sum(dza, axis=0, keepdims=True)
            db_ref[:, d:] += jnp.sum(dzb, axis=0, keepdims=True)

        outs = [jax.ShapeDtypeStruct((r, d2), z.dtype), jax.ShapeDtypeStruct((1, d2), F32)]
        dz, db = _ew_call(name + "_bwd", body, (z, do), outs, r, (d2, d), (d2, None), tr,
                          extra_in=(b,), extra_specs=(pl.BlockSpec((1, d2), lambda i: (0, 0)),), sem="arbitrary")
        return dz, db

    f.defvjp(fwd, bwd)
    return f


def make_final_loss(name):
    def call(z, g, target):
        r, d = z.shape
        tr = _row_tile(r, r, d)

        def body(z_ref, t_ref, g_ref, dz_ref, dg_ref, l_ref):
            i = pl.program_id(0)
            zz = z_ref[...]
            gg = g_ref[...]
            rstd = lax.rsqrt(jnp.mean(zz * zz, axis=-1, keepdims=True) + EPS)
            zhat = zz * rstd
            e = zhat * gg - t_ref[...]
            dy = e * (1.0 / d)
            dyg = dy * gg
            dz_ref[...] = rstd * (dyg - zhat * jnp.mean(dyg * zhat, axis=-1, keepdims=True))

            @pl.when(i == 0)
            def _():
                dg_ref[...] = jnp.zeros_like(dg_ref)
                l_ref[...] = jnp.zeros_like(l_ref)

            dg_ref[...] += jnp.sum(dy * zhat, axis=0, keepdims=True)
            l_ref[...] += jnp.sum(jnp.sum(e * e, axis=1, keepdims=True), axis=0, keepdims=True) * (0.5 / d)

        outs = [jax.ShapeDtypeStruct((r, d), F32), jax.ShapeDtypeStruct((1, d), F32), jax.ShapeDtypeStruct((1, 1), F32)]
        return _ew_call(name, body, (z, target), outs, r, (d, d), (d, None, None), tr,
                        extra_in=(g,), extra_specs=(pl.BlockSpec((1, d), lambda i: (0, 0)),), sem="arbitrary")

    @jax.custom_vjp
    def f(z, g, target):
        return call(z, g, target)[2]

    def fwd(z, g, target):
        dz, dg, loss = call(z, g, target)
        return loss, (dz, dg)

    def bwd(res, dl):
        dz, dg = res
        s = dl[0, 0]
        return dz * s, dg * s, None

    f.defvjp(fwd, bwd)
    return f


def _rope_tables(t):
    quarter = HEAD_DIM // 4
    inv_freq = ROPE_BASE ** (-np.arange(quarter, dtype=np.float64) / quarter)
    pos = np.arange(t)
    ang_r = (pos // GRID_W)[:, None] * inv_freq[None, :]
    ang_c = (pos % GRID_W)[:, None] * inv_freq[None, :]
    cos = np.concatenate([np.cos(ang_r), np.cos(ang_r), np.cos(ang_c), np.cos(ang_c)], axis=1)
    sin = np.concatenate([-np.sin(ang_r), np.sin(ang_r), -np.sin(ang_c), np.sin(ang_c)], axis=1)
    return jnp.asarray(cos, F32), jnp.asarray(sin, F32)


def _rope_call(x, cos, sin, name):
    t, w = x.shape
    tr = _pick(t, (512, 256, 128, 64))
    quarter = HEAD_DIM // 4

    def body(x_ref, c_ref, s_ref, o_ref):
        xx = x_ref[...].astype(F32)
        lane = lax.broadcasted_iota(jnp.int32, xx.shape, 1)
        first = (lane % (2 * quarter)) < quarter
        partner = jnp.where(first, pltpu.roll(xx, HEAD_DIM - quarter, 1), pltpu.roll(xx, quarter, 1))
        o_ref[...] = (xx * c_ref[...] + partner * s_ref[...]).astype(o_ref.dtype)

    return pl.pallas_call(
        body, grid=(t // tr, w // HEAD_DIM),
        in_specs=[pl.BlockSpec((tr, HEAD_DIM), lambda i, j: (i, j)), pl.BlockSpec((tr, HEAD_DIM), lambda i, j: (i, 0)),
                  pl.BlockSpec((tr, HEAD_DIM), lambda i, j: (i, 0))],
        out_specs=pl.BlockSpec((tr, HEAD_DIM), lambda i, j: (i, j)),
        out_shape=jax.ShapeDtypeStruct((t, w), x.dtype), name=name, compiler_params=_cparams("parallel", "parallel"))(x, cos, sin)


def make_rope(t, name):
    cos, sin = _rope_tables(t)

    @jax.custom_vjp
    def f(x):
        return _rope_call(x, cos, sin, name + "_fwd")

    def fwd(x):
        return _rope_call(x, cos, sin, name + "_fwd"), None

    def bwd(_, dy):
        return (_rope_call(dy, cos, -sin, name + "_bwd"),)

    f.defvjp(fwd, bwd)
    return f


def _dot_nt(a, b):
    return lax.dot_general(a, b, (((1,), (1,)), ((), ())), preferred_element_type=F32)


def _dot_tn(a, b):
    return lax.dot_general(a, b, (((0,), (0,)), ((), ())), preferred_element_type=F32)


def _attn_specs(g, span, tk, m, nbh, has_ctx):
    hd = HEAD_DIM
    q_spec = pl.BlockSpec((ATTN_BLOCK, g * hd), lambda h, i, meta: (i, h))
    kv_spec = pl.BlockSpec((tk, hd), lambda h, i, meta: (0, h))
    c_spec = pl.BlockSpec((m, hd), lambda h, i, meta: (0, h))
    if nbh > 1:
        b_spec = pl.BlockSpec((None, None, ATTN_BLOCK, span), lambda h, i, meta: (meta[1, i], h, 0, 0))
    else:
        b_spec = pl.BlockSpec((None, None, ATTN_BLOCK, span), lambda h, i, meta: (meta[1, i], 0, 0, 0))
    sink_spec = pl.BlockSpec(memory_space=pltpu.SMEM)
    return q_spec, kv_spec, c_spec, b_spec, sink_spec


def _attn_probs(qh, ks, kc, bias, sink_val, scale, has_ctx, has_sink):
    s = _dot_nt(qh, ks) * scale + bias
    mx = jnp.max(s, axis=-1, keepdims=True)
    sc = None
    if has_ctx:
        sc = _dot_nt(qh, kc) * scale
        mx = jnp.maximum(mx, jnp.max(sc, axis=-1, keepdims=True))
    if has_sink:
        mx = jnp.maximum(mx, sink_val)
    p = jnp.exp(s - mx)
    l = jnp.sum(p, axis=-1, keepdims=True)
    pc = None
    if has_ctx:
        pc = jnp.exp(sc - mx)
        l = l + jnp.sum(pc, axis=-1, keepdims=True)
    ps = None
    if has_sink:
        ps = jnp.exp(sink_val - mx)
        l = l + ps
    return p, pc, ps, l


def _attn_fwd(q, k, v, kc, vc, bias, sink, meta, g, span, has_ctx, has_sink, name):
    rq, wq = q.shape
    tk, wk = k.shape
    hkv = wk // HEAD_DIM
    m = kc.shape[0]
    nbh = bias.shape[1]
    scale = HEAD_DIM ** -0.5
    nqb = rq // ATTN_BLOCK
    q_spec, kv_spec, c_spec, b_spec, sink_spec = _attn_specs(g, span, tk, m, nbh, has_ctx)

    def body(meta_ref, sink_ref, q_ref, k_ref, v_ref, kc_ref, vc_ref, b_ref, o_ref):
        h = pl.program_id(0)
        i = pl.program_id(1)
        ks0 = pl.multiple_of(meta_ref[0, i], 64)
        ks = k_ref[pl.ds(ks0, span), :]
        vs = v_ref[pl.ds(ks0, span), :]
        bias_t = b_ref[...]
        for hh in range(g):
            qh = q_ref[:, hh * HEAD_DIM:(hh + 1) * HEAD_DIM]
            sink_val = sink_ref[h * g + hh] if has_sink else None
            p, pc, _, l = _attn_probs(qh, ks, kc_ref[...], bias_t, sink_val, scale, has_ctx, has_sink)
            acc = jnp.dot(p.astype(BF16), vs, preferred_element_type=F32)
            if has_ctx:
                acc = acc + jnp.dot(pc.astype(BF16), vc_ref[...], preferred_element_type=F32)
            o_ref[:, hh * HEAD_DIM:(hh + 1) * HEAD_DIM] = (acc / l).astype(o_ref.dtype)

    gs = pltpu.PrefetchScalarGridSpec(
        num_scalar_prefetch=1, grid=(hkv, nqb),
        in_specs=[sink_spec, q_spec, kv_spec, kv_spec, c_spec, c_spec, b_spec], out_specs=q_spec)
    return pl.pallas_call(body, grid_spec=gs, out_shape=jax.ShapeDtypeStruct((rq, wq), BF16), name=name,
                          compiler_params=_cparams("parallel", "arbitrary"))(meta, sink, q, k, v, kc, vc, bias)


def _attn_bwd(q, k, v, kc, vc, bias, sink, meta, o, do, g, span, has_ctx, has_sink, want_dbias, name):
    rq, wq = q.shape
    tk, wk = k.shape
    hkv = wk // HEAD_DIM
    m = kc.shape[0]
    ncase, nbh = bias.shape[:2]
    scale = HEAD_DIM ** -0.5
    nqb = rq // ATTN_BLOCK
    q_spec, kv_spec, c_spec, b_spec, sink_spec = _attn_specs(g, span, tk, m, nbh, has_ctx)
    dsink_spec = pl.BlockSpec((None, 8, HEAD_DIM), lambda h, i, meta: (h, 0, 0))

    def body(meta_ref, sink_ref, q_ref, k_ref, v_ref, kc_ref, vc_ref, b_ref, o_ref, do_ref,
             dq_ref, dk_ref, dv_ref, dkc_ref, dvc_ref, db_ref, dsk_ref):
        h = pl.program_id(0)
        i = pl.program_id(1)

        @pl.when(i == 0)
        def _():
            dk_ref[...] = jnp.zeros_like(dk_ref)
            dv_ref[...] = jnp.zeros_like(dv_ref)
            dkc_ref[...] = jnp.zeros_like(dkc_ref)
            dvc_ref[...] = jnp.zeros_like(dvc_ref)
            dsk_ref[...] = jnp.zeros_like(dsk_ref)

        if want_dbias:
            @pl.when(meta_ref[2, i] == 1)
            def _():
                db_ref[...] = jnp.zeros_like(db_ref)
        else:
            @pl.when(i == 0)
            def _():
                db_ref[...] = jnp.zeros_like(db_ref)

        ks0 = pl.multiple_of(meta_ref[0, i], 64)
        ks = k_ref[pl.ds(ks0, span), :]
        vs = v_ref[pl.ds(ks0, span), :]
        bias_t = b_ref[...]
        dk_acc = jnp.zeros((span, HEAD_DIM), F32)
        dv_acc = jnp.zeros((span, HEAD_DIM), F32)
        for hh in range(g):
            cols = slice(hh * HEAD_DIM, (hh + 1) * HEAD_DIM)
            qh = q_ref[:, cols]
            doh = do_ref[:, cols]
            sink_val = sink_ref[h * g + hh] if has_sink else None
            p, pc, ps, l = _attn_probs(qh, ks, kc_ref[...], bias_t, sink_val, scale, has_ctx, has_sink)
            inv_l = 1.0 / l
            delta = jnp.sum(doh.astype(F32) * o_ref[:, cols].astype(F32), axis=-1, keepdims=True)
            pn = p * inv_l
            ds = pn * (_dot_nt(doh, vs) - delta)
            dsb = ds.astype(BF16)
            dq = jnp.dot(dsb, ks, preferred_element_type=F32)
            dk_acc = dk_acc + _dot_tn(dsb, qh)
            dv_acc = dv_acc + _dot_tn(pn.astype(BF16), doh)
            if want_dbias:
                db_ref[...] += ds
            if has_ctx:
                pcn = pc * inv_l
                dsc = (pcn * (_dot_nt(doh, vc_ref[...]) - delta)).astype(BF16)
                dq = dq + jnp.dot(dsc, kc_ref[...], preferred_element_type=F32)
                dkc_ref[...] += _dot_tn(dsc, qh) * scale
                dvc_ref[...] += _dot_tn(pcn.astype(BF16), doh)
            if has_sink:
                dsv = -jnp.sum(ps * inv_l * delta, axis=0, keepdims=True)
                dsk_ref[hh:hh + 1, :] += jnp.broadcast_to(dsv, (1, HEAD_DIM))
            dq_ref[:, cols] = (dq * scale).astype(dq_ref.dtype)
        dk_ref[pl.ds(ks0, span), :] += dk_acc * scale
        dv_ref[pl.ds(ks0, span), :] += dv_acc

    gs = pltpu.PrefetchScalarGridSpec(
        num_scalar_prefetch=1, grid=(hkv, nqb),
        in_specs=[sink_spec, q_spec, kv_spec, kv_spec, c_spec, c_spec, b_spec, q_spec, q_spec],
        out_specs=[q_spec, kv_spec, kv_spec, c_spec, c_spec, b_spec if want_dbias else dsink_spec, dsink_spec])
    db_sds = jax.ShapeDtypeStruct((ncase, nbh, ATTN_BLOCK, span) if want_dbias else (hkv, 8, HEAD_DIM), F32)
    out_shape = [jax.ShapeDtypeStruct((rq, wq), BF16), jax.ShapeDtypeStruct((tk, wk), F32), jax.ShapeDtypeStruct((tk, wk), F32),
                 jax.ShapeDtypeStruct((m, wk), F32), jax.ShapeDtypeStruct((m, wk), F32), db_sds,
                 jax.ShapeDtypeStruct((hkv, 8, HEAD_DIM), F32)]
    return pl.pallas_call(body, grid_spec=gs, out_shape=out_shape, name=name,
                          compiler_params=_cparams("parallel", "arbitrary"))(meta, sink, q, k, v, kc, vc, bias, o, do)


def make_attention(meta_np, g, span, has_ctx, has_sink, want_dbias, name):
    meta = jnp.asarray(meta_np, jnp.int32)

    @jax.custom_vjp
    def f(q, k, v, kc, vc, bias, sink):
        return _attn_fwd(q, k, v, kc, vc, bias, sink, meta, g, span, has_ctx, has_sink, name + "_fwd")

    def fwd(q, k, v, kc, vc, bias, sink):
        o = _attn_fwd(q, k, v, kc, vc, bias, sink, meta, g, span, has_ctx, has_sink, name + "_fwd")
        return o, (q, k, v, kc, vc, bias, sink, o)

    def bwd(res, do):
        q, k, v, kc, vc, bias, sink, o = res
        dq, dk, dv, dkc, dvc, db, dsk = _attn_bwd(q, k, v, kc, vc, bias, sink, meta, o, do.astype(BF16), g, span,
                                                   has_ctx, has_sink, want_dbias, name + "_bwd")
        dsink = dsk[:, :g, 0].reshape(sink.shape)
        if not want_dbias:
            db = jnp.zeros_like(bias)
        return dq, dk.astype(k.dtype), dv.astype(v.dtype), dkc.astype(kc.dtype), dvc.astype(vc.dtype), db, dsink

    f.defvjp(fwd, bwd)
    return f


def _dedupe_cases(tables):
    cases, idx, first = [], [], []
    for tbl in tables:
        if cases and np.array_equal(cases[-1], tbl):
            idx.append(len(cases) - 1)
            first.append(0)
        else:
            cases.append(tbl)
            idx.append(len(cases) - 1)
            first.append(1)
    return cases, idx, first


def _na_plan(t):
    rows = t // GRID_W
    qr = ATTN_BLOCK // GRID_W
    kr = qr + NA_ROWS - 1
    assert rows >= kr and rows % qr == 0
    span = kr * GRID_W
    kstart, tables = [], []
    qcol = np.tile(np.arange(GRID_W), qr)
    kcol = np.tile(np.arange(GRID_W), kr)
    win_c = np.clip(qcol - NA_COLS // 2, 0, GRID_W - NA_COLS)
    col_ok = (kcol[None, :] >= win_c[:, None]) & (kcol[None, :] < win_c[:, None] + NA_COLS)
    dcol = np.clip(kcol[None, :] - qcol[:, None] + NA_COLS - 1, 0, 2 * NA_COLS - 2)
    for r0 in range(0, rows, qr):
        kb = int(np.clip(r0 - NA_ROWS // 2, 0, rows - kr))
        qrow = r0 + np.repeat(np.arange(qr), GRID_W)
        krow = kb + np.repeat(np.arange(kr), GRID_W)
        win_r = np.clip(qrow - NA_ROWS // 2, 0, rows - NA_ROWS)
        row_ok = (krow[None, :] >= win_r[:, None]) & (krow[None, :] < win_r[:, None] + NA_ROWS)
        drow = np.clip(krow[None, :] - qrow[:, None] + NA_ROWS - 1, 0, 2 * NA_ROWS - 2)
        tables.append(np.stack([row_ok & col_ok, drow, dcol]).astype(np.int32))
        kstart.append(kb * GRID_W)
    cases, idx, first = _dedupe_cases(tables)
    meta = np.array([kstart, idx, first], np.int32)
    return meta, span, np.stack(cases)


def _na_bias(rpb, cases):
    valid, drow, dcol = cases[:, 0], cases[:, 1], cases[:, 2]
    ncase, qn, span = valid.shape
    qr, kr = qn // GRID_W, span // GRID_W
    drow_s = drow.reshape(ncase, qr, GRID_W, kr, GRID_W)[:, :, 0, :, 0]
    dcol_s = dcol[0].reshape(qr, GRID_W, kr, GRID_W)[0, :, 0, :]
    oh_r = jnp.asarray(np.eye(2 * NA_ROWS - 1, dtype=np.float32)[drow_s])
    oh_c = jnp.asarray(np.eye(2 * NA_COLS - 1, dtype=np.float32)[dcol_s])
    tmp = jnp.einsum("hrc,xyc->hrxy", rpb, oh_c, precision=lax.Precision.HIGHEST)
    b = jnp.einsum("nakr,hrxy->nhaxky", oh_r, tmp, precision=lax.Precision.HIGHEST).reshape(ncase, -1, qn, span)
    return jnp.where(jnp.asarray(valid[:, None] > 0), b, NEG_INF)


def _sw_plan(t):
    span = 3 * ATTN_BLOCK
    assert t >= span
    kstart, tables = [], []
    for b in range(t // ATTN_BLOCK):
        ks = int(np.clip((b - 1) * ATTN_BLOCK, 0, t - span))
        qpos = b * ATTN_BLOCK + np.arange(ATTN_BLOCK)
        kpos = ks + np.arange(span)
        ok = np.abs(kpos[None, :] - qpos[:, None]) <= SW_RADIUS
        tables.append(np.where(ok, 0.0, NEG_INF).astype(np.float32))
        kstart.append(ks)
    cases, idx, first = _dedupe_cases(tables)
    return np.array([kstart, idx, first], np.int32), span, np.stack(cases)[:, None]


def _cmul(ar, ai, br, bi):
    return ar * br - ai * bi, ar * bi + ai * br


def _s5_scan_call(x2, win, lam, cin, wout, reverse, n_chunks, name):
    _, ll, d = x2.shape
    nt = d // HEAD_DIM
    sw = 2 * SSM_TILE_GROUPS * SSM_STATE
    hw = sw // 2
    rows = ll // n_chunks
    ic = rows // SCAN_BLOCKS
    full = cin is not None

    down_dir = 0 if reverse else 1

    def chunk_idx(k, dd):
        return jnp.where(dd == down_dir, n_chunks - 1 - k, k)

    def body(*refs):
        if full:
            x_ref, win_ref, lam_ref, cin_ref, wout_ref, s_out, y_out, ub_ref, st_ref = refs
        else:
            x_ref, win_ref, lam_ref, f_out, ub_ref, st_ref = refs
        k = pl.program_id(2)
        down = pl.program_id(0) == down_dir

        @pl.when(k == 0)
        def _():
            st_ref[...] = cin_ref[...] if full else jnp.zeros_like(st_ref)

        ub_ref[...] = jnp.dot(x_ref[...].astype(BF16), win_ref[...], preferred_element_type=F32)
        lr = lam_ref[:, :hw]
        li = lam_ref[:, hw:]

        def step(ii, carry):
            sr, si = carry
            i = jnp.where(down, ic - 1 - ii, ii)
            r0 = pl.multiple_of(i * SCAN_BLOCKS, SCAN_BLOCKS)
            ur = ub_ref[pl.ds(r0, SCAN_BLOCKS), :hw]
            ui = ub_ref[pl.ds(r0, SCAN_BLOCKS), hw:]
            nr = lr * sr - li * si + ur
            ni = lr * si + li * sr + ui
            if full:
                ub_ref[pl.ds(r0, SCAN_BLOCKS), :hw] = nr
                ub_ref[pl.ds(r0, SCAN_BLOCKS), hw:] = ni
            return nr, ni

        sr, si = lax.fori_loop(0, ic, step, (st_ref[:, :hw], st_ref[:, hw:]), unroll=4 if ic % 4 == 0 else 1)
        st_ref[:, :hw] = sr
        st_ref[:, hw:] = si
        if full:
            s_out[...] = ub_ref[...]
            y_out[...] = jnp.dot(ub_ref[...].astype(BF16), wout_ref[...], preferred_element_type=F32)
        else:
            @pl.when(k == n_chunks - 1)
            def _():
                f_out[...] = st_ref[...]

    x_spec = pl.BlockSpec((None, rows, HEAD_DIM), lambda dd, t, k: (dd, chunk_idx(k, dd), t))
    win_spec = pl.BlockSpec((None, None, HEAD_DIM, sw), lambda dd, t, k: (dd, t, 0, 0))
    vec_spec = pl.BlockSpec((None, None, SCAN_BLOCKS, sw), lambda dd, t, k: (dd, t, 0, 0))
    scratch = [pltpu.VMEM((rows, sw), F32), pltpu.VMEM((SCAN_BLOCKS, sw), F32)]
    if full:
        in_specs = [x_spec, win_spec, vec_spec, vec_spec, pl.BlockSpec((None, None, sw, HEAD_DIM), lambda dd, t, k: (dd, t, 0, 0))]
        out_specs = [pl.BlockSpec((None, None, rows, sw), lambda dd, t, k: (dd, t, chunk_idx(k, dd), 0)), x_spec]
        out_shape = [jax.ShapeDtypeStruct((2, nt, ll, sw), F32), jax.ShapeDtypeStruct((2, ll, d), F32)]
        args = (x2, win, lam, cin, wout)
    else:
        in_specs = [x_spec, win_spec, vec_spec]
        out_specs = vec_spec
        out_shape = jax.ShapeDtypeStruct((2, nt, SCAN_BLOCKS, sw), F32)
        args = (x2, win, lam)
    return pl.pallas_call(body, grid=(2, nt, n_chunks), in_specs=in_specs, out_specs=out_specs, out_shape=out_shape,
                          scratch_shapes=scratch, name=name,
                          compiler_params=_cparams("parallel", "parallel", "arbitrary"))(*args)


def _s5_bwd_call(dy2, wrt, lamc, cin, st, u2, wdt, n_chunks, name):
    _, ll, d = dy2.shape
    nt = d // HEAD_DIM
    sw = 2 * SSM_TILE_GROUPS * SSM_STATE
    hw = sw // 2
    rows = ll // n_chunks
    ic = rows // SCAN_BLOCKS

    def chunk_idx(k, dd):
        return jnp.where(dd == 0, n_chunks - 1 - k, k)

    def body(dy_ref, wrt_ref, lam_ref, cin_ref, st_ref, u_ref, wdt_ref, du_out, dwd_out, dwr_out, dlam_out,
             ds_ref, a_ref):
        k = pl.program_id(2)
        down = pl.program_id(0) == 0

        @pl.when(k == 0)
        def _():
            a_ref[...] = cin_ref[...]
            dwd_out[...] = jnp.zeros_like(dwd_out)
            dwr_out[...] = jnp.zeros_like(dwr_out)
            dlam_out[...] = jnp.zeros_like(dlam_out)

        dyb = dy_ref[...].astype(BF16)
        ds_ref[...] = jnp.dot(dyb, wrt_ref[...], preferred_element_type=F32)
        dwr_out[...] += _dot_tn(st_ref[...].astype(BF16), dyb)
        lr = lam_ref[:, :hw]
        li = lam_ref[:, hw:]

        def step(ii, carry):
            ar, ai, gr, gi = carry
            i = jnp.where(down, ic - 1 - ii, ii)
            r0 =pl.multiple_of(i * SCAN_BLOCKS, SCAN_BLOCKS)
            sr = st_ref[pl.ds(r0, SCAN_BLOCKS), :hw]
            si = st_ref[pl.ds(r0, SCAN_BLOCKS), hw:]
            gr = gr + ar * sr + ai * si
            gi = gi + ai * sr - ar * si
            nr = lr * ar - li * ai + ds_ref[pl.ds(r0, SCAN_BLOCKS), :hw]
            ni = lr * ai + li * ar + ds_ref[pl.ds(r0, SCAN_BLOCKS), hw:]
            ds_ref[pl.ds(r0, SCAN_BLOCKS), :hw] = nr
            ds_ref[pl.ds(r0, SCAN_BLOCKS), hw:] = ni
            return nr, ni, gr, gi

        init = (a_ref[:, :hw], a_ref[:, hw:], dlam_out[:, :hw], dlam_out[:, hw:])
        ar, ai, gr, gi = lax.fori_loop(0, ic, step, init, unroll=4 if ic % 4 == 0 else 1)
        a_ref[:, :hw] = ar
        a_ref[:, hw:] = ai
        dlam_out[:, :hw] = gr
        dlam_out[:, hw:] = gi
        ab = ds_ref[...].astype(BF16)
        du_out[...] = jnp.dot(ab, wdt_ref[...], preferred_element_type=F32).astype(du_out.dtype)
        dwd_out[...] += _dot_tn(u_ref[...].astype(BF16), ab)

    x_spec = pl.BlockSpec((None, rows, HEAD_DIM), lambda dd, t, k: (dd, chunk_idx(k, dd), t))
    w_in = pl.BlockSpec((None, None, HEAD_DIM, sw), lambda dd, t, k: (dd, t, 0, 0))
    w_out = pl.BlockSpec((None, None, sw, HEAD_DIM), lambda dd, t, k: (dd, t, 0, 0))
    vec_spec = pl.BlockSpec((None, None, SCAN_BLOCKS, sw), lambda dd, t, k: (dd, t, 0, 0))
    st_spec = pl.BlockSpec((None, None, rows, sw), lambda dd, t, k: (dd, t, chunk_idx(k, dd), 0))
    out_shape = [jax.ShapeDtypeStruct((2, ll, d), u2.dtype), jax.ShapeDtypeStruct((2, nt, HEAD_DIM, sw), F32),
                 jax.ShapeDtypeStruct((2, nt, sw, HEAD_DIM), F32), jax.ShapeDtypeStruct((2, nt, SCAN_BLOCKS, sw), F32)]
    return pl.pallas_call(
        body, grid=(2, nt, n_chunks),
        in_specs=[x_spec, w_in, vec_spec, vec_spec, st_spec, x_spec, w_out],
        out_specs=[x_spec, w_in, w_out, vec_spec], out_shape=out_shape,
        scratch_shapes=[pltpu.VMEM((rows, sw), F32), pltpu.VMEM((SCAN_BLOCKS, sw), F32)], name=name,
        compiler_params=_cparams("parallel", "parallel", "arbitrary"))(dy2, wrt, lamc, cin, st, u2, wdt)


def _cpow(lr, li, n):
    rr, ri = jnp.ones_like(lr), jnp.zeros_like(li)
    br, bi = lr, li
    while n:
        if n & 1:
            rr, ri = _cmul(rr, ri, br, bi)
        br, bi = _cmul(br, bi, br, bi)
        n >>= 1
    return rr, ri


def _resolve_carries(finals, lam, block_len, down_dir):
    hw = finals.shape[-1] // 2
    pr, pi = _cpow(lam[:, :, 0, :hw], lam[:, :, 0, hw:], block_len)
    fr, fi = finals[..., :hw], finals[..., hw:]

    def walk(order):
        cr, ci = jnp.zeros_like(pr), jnp.zeros_like(pi)
        out = [None] * SCAN_BLOCKS
        for j in order:
            out[j] = jnp.concatenate([cr, ci], axis=-1)
            mr, mi = _cmul(pr, pi, cr, ci)
            cr, ci = mr + fr[:, :, j], mi + fi[:, :, j]
        return jnp.stack(out, axis=2)

    up, down = walk(range(SCAN_BLOCKS)), walk(range(SCAN_BLOCKS - 1, -1, -1))
    return jnp.stack([down[0], up[1]] if down_dir == 0 else [up[0], down[1]])


def _scan_chunks(ll):
    block_len = ll // SCAN_BLOCKS
    for ic in (132, 128, 96, 64, 48, 36, 32, 24, 16, 8):
        if block_len % ic == 0:
            return block_len // ic
    return 1


def make_s5_core(name):
    def run_fwd(u2, lam, wd, wr):
        ll = u2.shape[1]
        nc = _scan_chunks(ll)
        lam8 = jnp.broadcast_to(lam[:, :, None, :], lam.shape[:2] + (SCAN_BLOCKS, lam.shape[-1]))
        wdb = wd.astype(BF16)
        finals = _s5_scan_call(u2, wdb, lam8, None, None, False, nc, name + "_carry")
        cin = _resolve_carries(finals, lam8, ll // SCAN_BLOCKS, 1)
        st, y2 = _s5_scan_call(u2, wdb, lam8, cin, wr.astype(BF16), False, nc, name + "_scan")
        return y2, st, lam8

    @jax.custom_vjp
    def f(u2, lam, wd, wr):
        return run_fwd(u2, lam, wd, wr)[0]

    def fwd(u2, lam, wd, wr):
        y2, st, lam8 = run_fwd(u2, lam, wd, wr)
        return y2, (u2, lam8, wd, wr, st)

    def bwd(res, dy2):
        u2, lam8, wd, wr, st = res
        ll = u2.shape[1]
        nc = _scan_chunks(ll)
        hw = lam8.shape[-1] // 2
        lamc = jnp.concatenate([lam8[..., :hw], -lam8[..., hw:]], axis=-1)
        wrt = jnp.swapaxes(wr, 2, 3).astype(BF16)
        wdt = jnp.swapaxes(wd, 2, 3).astype(BF16)
        finals = _s5_scan_call(dy2, wrt, lamc, None, None, True, nc, name + "_bcarry")
        cin = _resolve_carries(finals, lamc, ll // SCAN_BLOCKS, 0)
        du2, dwd, dwr, dlam8 = _s5_bwd_call(dy2, wrt, lamc, cin, st, u2, wdt, nc, name + "_bscan")
        return du2, jnp.sum(dlam8, axis=2), dwd, dwr

    f.defvjp(fwd, bwd)
    return f


def _s5_params(a_re, a_im, log_dt, b_re, b_im, c_re, c_im):
    dt = jnp.exp(log_dt)[..., None]
    mag = jnp.exp(a_re * dt)
    lam_r, lam_i = mag * jnp.cos(a_im * dt), mag * jnp.sin(a_im * dt)
    den = a_re * a_re + a_im * a_im
    nr = lam_r - 1.0
    coef_r = (nr * a_re + lam_i * a_im) / den
    coef_i = (lam_i * a_re - nr * a_im) / den
    bbar_r = coef_r[..., None] * b_re - coef_i[..., None] * b_im
    bbar_i = coef_r[..., None] * b_im + coef_i[..., None] * b_re
    ndir, g, p = lam_r.shape
    tg = SSM_TILE_GROUPS
    nt = g // tg
    eye = jnp.eye(tg, dtype=F32)

    def tile_vec(v):
        return v.reshape(ndir, nt, tg * p)

    lam = jnp.concatenate([tile_vec(lam_r), tile_vec(lam_i)], axis=-1)

    def drive(b):
        bt = b.reshape(ndir, nt, tg, p, SSM_GROUP)
        return (jnp.swapaxes(bt, 3, 4)[:, :, :, :, None, :] * eye[None, None, :, None, :, None]).reshape(ndir, nt, tg * SSM_GROUP, tg * p)

    wd = jnp.concatenate([drive(bbar_r), drive(bbar_i)], axis=-1)

    def readout(c):
        ct = c.reshape(ndir, nt, tg, SSM_GROUP, p)
        return (jnp.swapaxes(ct, 3, 4)[:, :, :, :, None, :] * eye[None, None, :, None, :, None]).reshape(ndir, nt, tg * p, tg * SSM_GROUP)

    wr = jnp.concatenate([readout(c_re), -readout(c_im)], axis=2)
    return lam, wd, wr


def _to_scan_order(seq):
    ll, d = seq.shape
    return seq.reshape(SCAN_BLOCKS, ll // SCAN_BLOCKS, d).swapaxes(0, 1).reshape(ll, d)


def _from_scan_order(y2):
    ll, d = y2.shape
    return y2.reshape(ll // SCAN_BLOCKS, SCAN_BLOCKS, d).swapaxes(0, 1).reshape(ll, d)


def adamw(w, g, m, v, name):
    shape = w.shape
    cols = shape[-1] if len(shape) > 1 else shape[0]
    w2, g2, m2, v2 = (a.reshape(-1, cols) for a in (w, g, m, v))
    r = w2.shape[0]
    cap = max(1, (1024 * 1024) // (4 * cols))
    tr = r
    for t in (512, 256, 128, 64, 32, 16, 8):
        if t <= cap and r % t == 0:
            tr = t
            break
    c1 = 1.0 / (1.0 - ADAM_B1 ** ADAM_STEP)
    c2 = 1.0 / (1.0 - ADAM_B2 ** ADAM_STEP)

    def body(w_ref, g_ref, m_ref, v_ref, d_ref, mo_ref, vo_ref):
        gg = g_ref[...]
        mn = ADAM_B1 * m_ref[...] + (1.0 - ADAM_B1) * gg
        vn = ADAM_B2 * v_ref[...] + (1.0 - ADAM_B2) * (gg * gg)
        d_ref[...] = -ADAM_LR * ((mn * c1) / (jnp.sqrt(vn * c2) + ADAM_EPS) + ADAM_WD * w_ref[...])
        mo_ref[...] = mn
        vo_ref[...] = vn

    spec = pl.BlockSpec((tr, cols), lambda i: (i, 0))
    sds = jax.ShapeDtypeStruct((r, cols), F32)
    d, mn, vn = pl.pallas_call(body, grid=(r // tr,), in_specs=[spec] * 4, out_specs=[spec] * 3, out_shape=[sds] * 3,
                               name=name, compiler_params=_cparams("parallel"))(w2, g2, m2, v2)
    return d.reshape(shape), mn.reshape(shape), vn.reshape(shape)


def _my_pos():
    return lax.axis_index("x"), lax.axis_index("y"), lax.axis_index("c")


def _flip(pos, f):
    return tuple((1 - p) if b else p for p, b in zip(pos, f))


def _lin(pos):
    return 4 * pos[0] + 2 * pos[1] + pos[2]


def xchg(src, n_out, plan, name, inplace=False):
    piece = src.shape[1:]

    def body(src_ref, out_ref, send_sems, recv_sems):
        me = _my_pos()
        copies = []
        for k, (f, sfn, dfn) in enumerate(plan):
            peer = _flip(me, f)
            s_ref = (out_ref if inplace else src_ref).at[sfn(me, peer)]
            d_ref = out_ref.at[dfn(me, peer)]
            if any(f):
                cp = pltpu.make_async_remote_copy(src_ref=s_ref, dst_ref=d_ref, send_sem=send_sems.at[k],
                                                  recv_sem=recv_sems.at[k], device_id=peer, device_id_type=MESH)
            else:
                cp = pltpu.make_async_copy(s_ref, d_ref, recv_sems.at[k])
            cp.start()
            copies.append((cp, any(f)))
        for cp, remote in copies:
            if remote:
                cp.wait_recv()
            else:
                cp.wait()
        for cp, remote in copies:
            if remote:
                cp.wait_send()

    return pl.pallas_call(
        body, in_specs=[pl.BlockSpec(memory_space=pl.ANY)], out_specs=pl.BlockSpec(memory_space=pl.ANY),
        out_shape=jax.ShapeDtypeStruct((n_out,) + piece, src.dtype),
        scratch_shapes=[pltpu.SemaphoreType.DMA((len(plan),)), pltpu.SemaphoreType.DMA((len(plan),))],
        input_output_aliases={0: 0} if inplace else {}, name=name)(src)


_CHIP_FLIPS = ((1, 0, 0), (0, 1, 0), (1, 1, 0))
_ALL_FLIPS = tuple((a, b, c) for a in (0, 1) for b in (0, 1) for c in (0, 1))[1:]


def all_to_all8(src, name):
    plan = [((0, 0, 0), lambda me, peer: _lin(me), lambda me, peer: _lin(me))]
    plan += [(f, lambda me, peer: _lin(peer), lambda me, peer: _lin(me)) for f in _ALL_FLIPS]
    return xchg(src, N_DEV, plan, name)


def all_gather8(piece, name):
    plan = [((0, 0, 0), lambda me, peer: 0, lambda me, peer: _lin(me))]
    plan += [(f, lambda me, peer: 0, lambda me, peer: _lin(me)) for f in _ALL_FLIPS]
    return xchg(piece[None], N_DEV, plan, name)


def gather_weight(shard, name):
    k, ns = shard.shape
    px, py, _ = _my_pos()
    own = shard.astype(BF16)[None]
    buf = lax.dynamic_update_slice(jnp.zeros((4, k, ns), BF16), own, (2 * px + py, 0, 0)).reshape(8, k // 2, ns)

    def body(in_ref, out_ref, send_sems, recv_sems):
        me = _my_pos()
        sibling = _flip(me, (0, 0, 1))
        chips = [_flip(me, f) for f in _CHIP_FLIPS]

        def copy(sem, holder, to):
            rows = out_ref.at[4 * holder[0] + 2 * holder[1] + me[2]]
            return pltpu.make_async_remote_copy(src_ref=rows, dst_ref=rows, send_sem=send_sems.at[sem],
                                                recv_sem=recv_sems.at[sem], device_id=to, device_id_type=MESH)

        first = [copy(j, me, chip) for j, chip in enumerate(chips)]
        for cp in first:
            cp.start()
        passed = [copy(3 + j, chip, sibling) for j, chip in enumerate(chips)]
        for j, chip in enumerate(chips):
            copy(j, chip, me).wait_recv()
            passed[j].start()
        for j in range(3):
            passed[j].wait_recv()
        for cp in first + passed:
            cp.wait_send()

    full = pl.pallas_call(
        body, in_specs=[pl.BlockSpec(memory_space=pl.ANY)], out_specs=pl.BlockSpec(memory_space=pl.ANY),
        out_shape=jax.ShapeDtypeStruct(buf.shape, BF16),
        scratch_shapes=[pltpu.SemaphoreType.DMA((6,)), pltpu.SemaphoreType.DMA((6,))],
        input_output_aliases={0: 0}, name=name)(buf)
    return full.reshape(4, k, ns)


def _sum_halves(g8, l1, c_idx, name):
    _, _, r, cc = g8.shape
    tr = _row_tile(r, r, cc)

    def body(c_ref, a_ref, b_ref, o_ref):
        o_ref[...] = (a_ref[...].astype(F32) + b_ref[...].astype(F32)).astype(o_ref.dtype)

    gs = pltpu.PrefetchScalarGridSpec(
        num_scalar_prefetch=1, grid=(4, r // tr),
        in_specs=[pl.BlockSpec((None, None, tr, cc), lambda s, i, c: (s, c[0], i, 0)),
                  pl.BlockSpec((None, tr, cc), lambda s, i, c: (s, i, 0))],
        out_specs=pl.BlockSpec((None, tr, cc), lambda s, i, c: (s, i, 0)))
    return pl.pallas_call(body, grid_spec=gs, out_shape=jax.ShapeDtypeStruct((4, r, cc), BF16), name=name,
                          compiler_params=_cparams("parallel", "parallel"))(c_idx, g8, l1)


def _sum_chips(p4, l2, sc_idx, name):
    _, r, cc = p4.shape
    tr = _row_tile(r, r, cc)

    def body(s_ref, a_ref, b0_ref, b1_ref, b2_ref, o_ref):
        o_ref[...] = ((a_ref[...].astype(F32) + b0_ref[...].astype(F32)) + b1_ref[...].astype(F32)) + b2_ref[...].astype(F32)

    gs = pltpu.PrefetchScalarGridSpec(
        num_scalar_prefetch=1, grid=(r // tr,),
        in_specs=[pl.BlockSpec((None, tr, cc), lambda i, s: (s[0], i, 0))]
        + [pl.BlockSpec((None, tr, cc), lambda i, s, j=j: (j, i, 0)) for j in range(3)],
        out_specs=pl.BlockSpec((None, tr, cc), lambda i, s: (s[1], i, 0)))
    return pl.pallas_call(body, grid_spec=gs, out_shape=jax.ShapeDtypeStruct((2, r, cc), F32), name=name,
                          compiler_params=_cparams("parallel"))(sc_idx, p4, l2, l2, l2)


def reduce_scatter_weight(g4, name):
    _, k, ns = g4.shape
    x, y, c = _my_pos()
    c_idx = jnp.reshape(c, (1,)).astype(jnp.int32)
    sc_idx = jnp.stack([2 * x + y, c]).astype(jnp.int32)
    g8 = g4.reshape(8, k // 2, ns)
    plan1 = [((0, 0, 1), lambda me, peer, s=s: 2 * s + peer[2], lambda me, peer, s=s: s) for s in range(4)]
    l1 = xchg(g8, 4, plan1, name + "_d2d")
    p4 = _sum_halves(g4.reshape(4, 2, k // 2, ns), l1, c_idx, name + "_sum2")
    plan2 = [(f, lambda me, peer: 2 * peer[0] + peer[1], lambda me, peer, j=j: j) for j, f in enumerate(_CHIP_FLIPS)]
    l2 = xchg(p4, 3, plan2, name + "_ici")
    halves = _sum_chips(p4, l2, sc_idx, name + "_sum4")
    plan3 = [((0, 0, 1), lambda me, peer: me[2], lambda me, peer: me[2])]
    return xchg(halves, 2, plan3, name + "_swap", inplace=True).reshape(k, ns)


def _sum8(a8, name):
    _, r, cc = a8.shape
    tr = _row_tile(r, r, cc)

    def body(a_ref, o_ref):
        acc = a_ref[0]
        for j in range(1, N_DEV):
            acc = acc + a_ref[j]
        o_ref[...] = acc

    return pl.pallas_call(body, grid=(r // tr,), in_specs=[pl.BlockSpec((N_DEV, tr, cc), lambda i: (0, i, 0))],
                          out_specs=pl.BlockSpec((tr, cc), lambda i: (i, 0)), out_shape=jax.ShapeDtypeStruct((r, cc), F32),
                          name=name, compiler_params=_cparams("parallel"))(a8)


def all_reduce8(flat, name):
    n = flat.shape[0]
    unit = N_DEV * 8 * 128
    npad = -(-n // unit) * unit
    a = jnp.pad(flat, (0, npad - n)).reshape(N_DEV, npad // (N_DEV * 128), 128)
    mine = _sum8(all_to_all8(a, name + "_rs"), name + "_sum")
    return all_gather8(mine, name + "_ag").reshape(npad)[:n]


def _local_loss(x, ctx, target, mods, small, big, dims):
    t, m, d = dims["t"], dims["m"], dims["d"]
    a_w, bq_w, bkv_w = dims["a_w"], dims["bq_w"], dims["bkv_w"]
    r = t + m
    z = jnp.concatenate([x, ctx], axis=0)

    def grp(layer, j, n_groups=2):
        return mods[layer, :n_groups, j][:, None, :]

    h = make_norm_mod(t, "norm_mix0")(z, small["norm_mix"][0][None], grp(0, 1), grp(0, 0))
    qkv = make_linear("col", BF16, "attn_in")(h, big["attn_w_in"])
    o1, o2, o3, o4, o5 = a_w, 2 * a_w, 3 * a_w, 3 * a_w + bq_w, 3 * a_w + bq_w + bkv_w
    qa, ka, va = qkv[:t, :o1], qkv[:t, o1:o2], qkv[:t, o2:o3]
    qa_c, ka_c, va_c = qkv[t:, :o1], qkv[t:, o1:o2], qkv[t:, o2:o3]
    qkb = make_rope(t, "rope")(qkv[:t, o3:o5])
    qb, kb, vb = qkb[:, :bq_w], qkb[:, bq_w:], qkv[:t, o5:]
    qb_c, kb_c, vb_c = qkv[t:, o3:o4], qkv[t:, o4:o5], qkv[t:, o5:]
    sink = small["attn_sink"][0]
    no_sink = jnp.zeros((a_w // HEAD_DIM,), F32)
    na_meta, na_span, na_cases = _na_plan(t)
    oa = make_attention(na_meta, 1, na_span, True, False, True, "na")(
        qa, ka, va, ka_c, va_c, _na_bias(small["attn_rpb"][0], na_cases), no_sink)
    sw_meta, sw_span, sw_bias = _sw_plan(t)
    grp_b = bq_w // bkv_w
    ob = make_attention(sw_meta, grp_b, sw_span, True, True, False, "swa")(qb, kb, vb, kb_c, vb_c, jnp.asarray(sw_bias), sink)
    c_meta = np.array([[0] * (m // ATTN_BLOCK), [0] * (m // ATTN_BLOCK), [1] + [0] * (m // ATTN_BLOCK - 1)], np.int32)
    zero_bias = jnp.zeros((1, 1, ATTN_BLOCK, m), F32)
    oa_c = make_attention(c_meta, 1, m, False, False, False, "ctx_na")(qa_c, ka_c, va_c, ka_c, va_c, zero_bias, no_sink)
    ob_c = make_attention(c_meta, grp_b, m, False, True, False, "ctx_swa")(qb_c, kb_c, vb_c, kb_c, vb_c, zero_bias, sink)
    o = jnp.concatenate([jnp.concatenate([oa, ob], axis=1), jnp.concatenate([oa_c, ob_c], axis=1)], axis=0)
    y = make_linear("row", F32, "attn_out")(o, big["attn_w_out"])
    z = make_gated_residual(t, "res_mix0")(z, y, grp(0, 2))
    h = make_norm_mod(t, "norm_ffn0")(z, small["norm_ffn"][0][None], grp(0, 4), grp(0, 3))
    a = make_swiglu_act("act0")(make_linear("col", BF16, "ffn0_w1")(h, big["ffn_w1_0"]),
                                make_linear("col", BF16, "ffn0_w3")(h, big["ffn_w3_0"]))
    z = make_gated_residual(t, "res_ffn0")(z, make_linear("row", F32, "ffn0_w2")(a, big["ffn_w2_0"]), grp(0, 5))

    h = make_norm_mod(t, "norm_mix1")(z, small["norm_mix"][1][None], grp(1, 1), grp(1, 0))
    hx, hc = h[:t], h[t:]
    lam, wd, wr = _s5_params(small["ssm_a_re"][0], small["ssm_a_im"][0], small["ssm_log_dt"][0], small["ssm_b_re"][0],
                             small["ssm_b_im"][0], small["ssm_c_re"][0], small["ssm_c_im"][0])
    u2 = jnp.stack([_to_scan_order(jnp.concatenate([hc, hx], axis=0)), _to_scan_order(h)])
    y2 = make_s5_core("s5")(u2, lam, wd, wr)
    ys = _from_scan_order(y2[0])[m:] + _from_scan_order(y2[1])[:t]
    gl = make_gelu_in("gelu")(hx, ys, small["ssm_d_full"][None])
    zz = make_linear("col", F32, "glu_w")(gl, big["ssm_w_glu"])
    yx = make_glu("glu")(zz, small["ssm_b_glu_full"][None])
    xs = make_gated_residual(t, "res_mix1")(z[:t], yx, grp(1, 2, 1))
    h = make_norm_mod(t, "norm_ffn1")(xs, small["norm_ffn"][1][None], grp(1, 4, 1), grp(1, 3, 1))
    a = make_swiglu_act("act1")(make_linear("col", BF16, "ffn1_w1")(h, big["ffn_w1_1"]),
                                make_linear("col", BF16, "ffn1_w3")(h, big["ffn_w3_1"]))
    xs = make_gated_residual(t, "res_ffn1")(xs, make_linear("row", F32, "ffn1_w2")(a, big["ffn_w2_1"]), grp(1, 5, 1))
    return make_final_loss("loss_head")(xs, small["norm_final"][None], target)[0, 0]


_WEIGHTS = ['c_ctx', 'ada_w', 'ada_b', 'norm_mix', 'norm_ffn', 'ffn_w1', 'ffn_w3', 'ffn_w2', 'attn_w_in', 'attn_w_out',
            'attn_rpb', 'attn_sink', 'ssm_a_re', 'ssm_a_im', 'ssm_log_dt', 'ssm_b_re', 'ssm_b_im', 'ssm_c_re', 'ssm_c_im',
            'ssm_d', 'ssm_w_glu', 'ssm_b_glu', 'norm_final']
_LOCAL_SMALL = ['norm_mix', 'norm_ffn', 'attn_rpb', 'attn_sink', 'ssm_a_re', 'ssm_a_im', 'ssm_log_dt', 'ssm_b_re',
                'ssm_b_im', 'ssm_c_re', 'ssm_c_im', 'norm_final']
_MOD_ROWS = 16


def _gather_chip_vector(v, name):
    g = all_gather8(v[None], name)
    return g[0::2, 0, :].reshape(-1)


def kernel(x, c, ctx, c_ctx, ada_w, ada_b, norm_mix, norm_ffn, ffn_w1, ffn_w3, ffn_w2, attn_w_in, attn_w_out, attn_rpb, attn_sink, ssm_a_re, ssm_a_im, ssm_log_dt, ssm_b_re, ssm_b_im, ssm_c_re, ssm_c_im, ssm_d, ssm_w_glu, ssm_b_glu, norm_final, loss_target, m_c_ctx, m_ada_w, m_ada_b, m_norm_mix, m_norm_ffn, m_ffn_w1, m_ffn_w3, m_ffn_w2, m_attn_w_in, m_attn_w_out, m_attn_rpb, m_attn_sink, m_ssm_a_re, m_ssm_a_im, m_ssm_log_dt, m_ssm_b_re, m_ssm_b_im, m_ssm_c_re, m_ssm_c_im, m_ssm_d, m_ssm_w_glu, m_ssm_b_glu, m_norm_final, v_c_ctx, v_ada_w, v_ada_b, v_norm_mix, v_norm_ffn, v_ffn_w1, v_ffn_w3, v_ffn_w2, v_attn_w_in, v_attn_w_out, v_attn_rpb, v_attn_sink, v_ssm_a_re, v_ssm_a_im, v_ssm_log_dt, v_ssm_b_re, v_ssm_b_im, v_ssm_c_re, v_ssm_c_im, v_ssm_d, v_ssm_w_glu, v_ssm_b_glu, v_norm_final):
    env = dict(locals())
    w = {n: env[n] for n in _WEIGHTS}
    mom = {n: env["m_" + n] for n in _WEIGHTS}
    var = {n: env["v_" + n] for n in _WEIGHTS}
    _, t, d = x.shape
    m = ctx.shape[1]
    px, py, pc = _my_pos()
    s_me = 2 * px + py
    a_w =attn_rpb.shape[1] * HEAD_DIM
    bq_w = attn_sink.shape[1] * HEAD_DIM
    bkv_w = (4 * attn_w_in.shape[2] - 3 * a_w - bq_w) // 2
    dims = dict(t=t, m=m, d=d, a_w=a_w, bq_w=bq_w, bkv_w=bkv_w)
    n_layers = ada_w.shape[0]
    ada_cols = ada_w.shape[2]

    big = {"attn_w_in": gather_weight(attn_w_in[0], "ag_attn_in"), "attn_w_out": gather_weight(attn_w_out[0], "ag_attn_out"),
           "ssm_w_glu": gather_weight(ssm_w_glu[0], "ag_glu")}
    for l in range(n_layers):
        big[f"ffn_w1_{l}"] = gather_weight(ffn_w1[l], f"ag_w1_{l}")
        big[f"ffn_w3_{l}"] = gather_weight(ffn_w3[l], f"ag_w3_{l}")
        big[f"ffn_w2_{l}"] = gather_weight(ffn_w2[l], f"ag_w2_{l}")
    small = {n: w[n] for n in _LOCAL_SMALL}
    small["ssm_d_full"] = _gather_chip_vector(ssm_d[0], "ag_ssm_d")
    small["ssm_b_glu_full"] = _gather_chip_vector(ssm_b_glu[0], "ag_b_glu")

    c_all = all_gather8(c, "ag_c")[:, 0, :]
    cond = jnp.concatenate([c_all, c_ctx[None], jnp.zeros((_MOD_ROWS - N_DEV - 1, d), F32)], axis=0)
    sig = jax.nn.sigmoid(cond)
    silu_c = (cond * sig).astype(BF16)
    ada_wb = ada_w.astype(BF16)
    mods_shard = jnp.stack([mm_nn(silu_c, ada_wb[l][None], "col", F32, f"ada_fwd{l}") for l in range(n_layers)])
    send = jnp.stack([mods_shard[:, jnp.array([tgt, N_DEV]), :].reshape(2 * n_layers, ada_cols) for tgt in range(N_DEV)])
    plan = [((0, 0, 0), lambda me, peer: _lin(me), lambda me, peer: 2 * me[0] + me[1])]
    plan += [(f, lambda me, peer: _lin(peer), lambda me, peer: 2 * me[0] + me[1]) for f in _CHIP_FLIPS]
    got = xchg(send, 4, plan, "mods_xchg")
    mods = got.reshape(4, n_layers, 2, ada_cols).transpose(1, 2, 0, 3).reshape(n_layers, 2, 4 * ada_cols)
    mods = (mods + ada_b[:, None, :]).reshape(n_layers, 2, 6, d)

    def local(xx, mods_, small_, big_):
        return _local_loss(xx, ctx[0], loss_target[0], mods_, small_, big_, dims)

    loss_local, vjp = jax.vjp(local, x[0], mods, small, big)
    g_x, g_mods, g_small, g_big = vjp(jnp.ones((), F32))
    loss = lax.psum(loss_local, ("x", "y", "c"))

    grads = {}
    grads["attn_w_in"] = reduce_scatter_weight(g_big["attn_w_in"], "rs_attn_in")[None]
    grads["attn_w_out"] = reduce_scatter_weight(g_big["attn_w_out"], "rs_attn_out")[None]
    grads["ssm_w_glu"] = reduce_scatter_weight(g_big["ssm_w_glu"], "rs_glu")[None]
    for n in ("ffn_w1", "ffn_w3", "ffn_w2"):
        grads[n] = jnp.stack([reduce_scatter_weight(g_big[f"{n}_{l}"], f"rs_{n}_{l}") for l in range(n_layers)])

    gm = all_gather8(g_mods.reshape(2 * n_layers, 6 * d), "ag_dmods").reshape(N_DEV, n_layers, 2, 6 * d)
    ctx_row = gm[0, :, 1]
    for j in range(1, N_DEV):
        ctx_row = ctx_row + gm[j, :, 1]
    dm16 = jnp.concatenate([gm[:, :, 0].transpose(1, 0, 2), ctx_row[:, None], jnp.zeros((n_layers, _MOD_ROWS - N_DEV - 1, 6 * d), F32)], axis=1)
    grads["ada_b"] = jnp.sum(dm16, axis=1)
    dm_mine = lax.dynamic_slice_in_dim(dm16, s_me * ada_cols, ada_cols, axis=2).astype(BF16)
    grads["ada_w"] = jnp.stack([mm_tn(silu_c.T, dm_mine[l], (1, d, ada_cols), "col", F32, f"ada_dw{l}")[0] for l in range(n_layers)])
    dsilu = mm_nt(dm_mine[0], ada_wb[0][None], "col", F32, "ada_dc0")
    for l in range(1, n_layers):
        dsilu = dsilu + mm_nt(dm_mine[l], ada_wb[l][None], "col", F32, f"ada_dc{l}")
    dsilu_ctx = 0.5 * dsilu[N_DEV]

    packed = [(n, g_small[n]) for n in _LOCAL_SMALL] + [("ssm_d", g_small["ssm_d_full"]), ("ssm_b_glu", g_small["ssm_b_glu_full"]),
                                                        ("c_ctx", dsilu_ctx)]
    flat = all_reduce8(jnp.concatenate([a.reshape(-1) for _, a in packed]), "ar_small")
    off = 0
    for n, a in packed:
        grads[n] = flat[off:off + a.size].reshape(a.shape)
        off += a.size
    sig_ctx = jax.nn.sigmoid(c_ctx)
    grads["c_ctx"] = grads["c_ctx"] * (sig_ctx * (1.0 + c_ctx * (1.0 - sig_ctx)))
    grads["ssm_d"] = lax.dynamic_slice_in_dim(grads["ssm_d"], s_me * ssm_d.shape[1], ssm_d.shape[1])[None]
    grads["ssm_b_glu"] = lax.dynamic_slice_in_dim(grads["ssm_b_glu"], s_me * ssm_b_glu.shape[1], ssm_b_glu.shape[1])[None]

    delta, new_m, new_v = {}, {}, {}
    for n in _WEIGHTS:
        delta[n], new_m[n], new_v[n] = adamw(w[n], grads[n], mom[n], var[n], "adamw_" + n)
    return (loss, g_x[None], *[grads[n] for n in _WEIGHTS], *[delta[n] for n in _WEIGHTS],
            *[new_m[n] for n in _WEIGHTS], *[new_v[n] for n in _WEIGHTS])
```

```python
import functools
import math

import numpy as np
import jax
import jax.numpy as jnp
from jax import lax
from jax.experimental import pallas as pl
from jax.experimental.pallas import tpu as pltpu

F32 = jnp.float32
BF16 = jnp.bfloat16
MESH = pl.DeviceIdType.MESH

HEAD_DIM = 128
GRID_W = 64
NA_ROWS = 8
NA_COLS = 16
SW_RADIUS = 128
ATTN_BLOCK = 128
ROPE_BASE = 10000.0
SSM_GROUP = 16
SSM_STATE = 64
SSM_TILE_GROUPS = 8
SCAN_BLOCKS = 8
EPS = 1e-6
NEG_INF = -1e30
ADAM_LR, ADAM_B1, ADAM_B2, ADAM_EPS, ADAM_WD, ADAM_STEP = 0.001, 0.9, 0.999, 1e-08, 0.01, 10
VMEM_LIMIT_BYTES = 56 * 1024 * 1024
N_DEV = 8


def _cparams(*sem):
    return pltpu.CompilerParams(dimension_semantics=tuple(sem) if sem else None, vmem_limit_bytes=VMEM_LIMIT_BYTES)


def _pick(n, cands):
    for c in cands:
        if n % c == 0:
            return c
    return n


def _dot_nn(a, b):
    return jnp.dot(a, b, preferred_element_type=F32)


def _dot_nt(a, b):
    return lax.dot_general(a, b, (((1,), (1,)), ((), ())), preferred_element_type=F32)


def _dot_tn(a, b):
    return lax.dot_general(a, b, (((0,), (0,)), ((), ())), preferred_element_type=F32)


def _mm_call(name, grid, a, b, a_spec, b_spec, o_spec, out_sds, acc_shape, step, carry=None):
    nk = grid[2]

    def body(*refs):
        if carry is None:
            a_ref, b_ref, o_ref = refs[:3]
            acc_ref = refs[3] if nk > 1 else None
        else:
            a_ref, b_ref, src_ref, o_ref, land_ref = refs[:5]
            acc_ref = refs[5] if nk > 1 else None
            send_sems, recv_sems = refs[-2:]
            ids = [pl.program_id(ax) for ax in range(3)]
            first = (ids[0] == 0) & (ids[1] == 0) & (ids[2] == 0)
            last = (ids[0] == grid[0] - 1) & (ids[1] == grid[1] - 1) & (ids[2] == grid[2] - 1)
            copies = _remote_copies(src_ref, land_ref, send_sems, recv_sems, carry[2])

            @pl.when(first)
            def _():
                for cp in copies:
                    cp.start()

        kk = pl.program_id(2)
        if nk == 1:
            o_ref[...] = step(a_ref, b_ref).astype(o_ref.dtype)
        else:
            @pl.when(kk == 0)
            def _():
                acc_ref[...] = step(a_ref, b_ref)

            @pl.when(kk > 0)
            def _():
                acc_ref[...] = step(a_ref, b_ref) + acc_ref[...]

            @pl.when(kk == nk - 1)
            def _():
                o_ref[...] = acc_ref[...].astype(o_ref.dtype)

        if carry is not None:
            @pl.when(last)
            def _():
                for cp in copies:
                    cp.wait_recv()
                for cp in copies:
                    cp.wait_send()

    scratch = [pltpu.VMEM(acc_shape, F32)] if nk > 1 else []
    if carry is None:
        return pl.pallas_call(
            body, grid=grid, in_specs=[a_spec, b_spec], out_specs=o_spec, out_shape=out_sds, scratch_shapes=scratch,
            name=name, compiler_params=_cparams("parallel", "parallel", "arbitrary"))(a, b)
    src, n_out, plan = carry
    any_spec = pl.BlockSpec(memory_space=pl.ANY)
    scratch += [pltpu.SemaphoreType.DMA((len(plan),)), pltpu.SemaphoreType.DMA((len(plan),))]
    return pl.pallas_call(
        body, grid=grid, in_specs=[a_spec, b_spec, any_spec], out_specs=[o_spec, any_spec],
        out_shape=[out_sds, jax.ShapeDtypeStruct((n_out,) + src.shape[1:], src.dtype)], scratch_shapes=scratch,
        name=name, compiler_params=_cparams("arbitrary", "arbitrary", "arbitrary"))(a, b, src)


_ROW_TILES = (768, 512, 256, 128, 64, 32, 16, 8)
_K_TILES = (2048, 1408, 1024, 512, 256, 128)
_CONTRACT_TILES = (1408, 1024, 768, 512, 256, 128)


def mm_nn(a, w3, kind, out_dtype, name):
    r = a.shape[0]
    s, d1, d2 = w3.shape
    tm = _pick(r, _ROW_TILES)
    if kind == "col":
        grid = (s, r // tm, 1)
        a_spec = pl.BlockSpec((tm, d1), lambda j, i, k: (i, 0))
        b_spec = pl.BlockSpec((None, d1, d2), lambda j, i, k: (j, 0, 0))
        o_spec = pl.BlockSpec((tm, d2), lambda j, i, k: (i, j))
        n = s * d2

        def step(a_ref, b_ref):
            return _dot_nn(a_ref[...], b_ref[...])
    else:
        tn = d2 if d2 <= 1024 else _pick(d2, (1024, 512, 256, 128))
        grid = (d2 // tn, r // tm, 1)
        a_spec = pl.BlockSpec((tm, s * d1), lambda j, i, k: (i, 0))
        b_spec = pl.BlockSpec((s, d1, tn), lambda j, i, k: (0, 0, j))
        o_spec = pl.BlockSpec((tm, tn), lambda j, i, k: (i, j))
        n = d2

        def step(a_ref, b_ref):
            p = _dot_nn(a_ref[:, :d1], b_ref[0])
            for q in range(1, s):
                p = _dot_nn(a_ref[:, q * d1:(q + 1) * d1], b_ref[q]) + p
            return p
    return _mm_call(name, grid, a, w3, a_spec, b_spec, o_spec, jax.ShapeDtypeStruct((r, n), out_dtype), None, step)


def mm_nt(dy, w3, kind, out_dtype, name, carry=None):
    r = dy.shape[0]
    s, d1, d2 = w3.shape
    tm = _pick(r, _ROW_TILES)
    if kind == "col":
        tko = d1 if d1 <= 1024 else _pick(d1, (1024, 512, 256, 128))
        grid = (d1 // tko, r // tm, 1)
        a_spec = pl.BlockSpec((tm, s * d2), lambda j, i, k: (i, 0))
        b_spec = pl.BlockSpec((s, tko, d2), lambda j, i, k: (0, j, 0))
        o_spec = pl.BlockSpec((tm, tko), lambda j, i, k: (i, j))
        kdim = d1

        def step(a_ref, b_ref):
            p = _dot_nt(a_ref[:, :d2], b_ref[0])
            for q in range(1, s):
                p = _dot_nt(a_ref[:, q * d2:(q + 1) * d2], b_ref[q]) + p
            return p
    else:
        grid = (s, r // tm, 1)
        a_spec = pl.BlockSpec((tm, d2), lambda j, i, k: (i, 0))
        b_spec = pl.BlockSpec((None, d1, d2), lambda j, i, k: (j, 0, 0))
        o_spec = pl.BlockSpec((tm, d1), lambda j, i, k: (i, j))
        kdim = s * d1

        def step(a_ref, b_ref):
            return _dot_nt(a_ref[...], b_ref[...])
    return _mm_call(name, grid, dy, w3, a_spec, b_spec, o_spec, jax.ShapeDtypeStruct((r, kdim), out_dtype), None, step, carry)


def mm_tn(at, dy, w_shape, kind, out_dtype, name):
    s, d1, d2 = w_shape
    r = dy.shape[0]
    tr = _pick(r, _CONTRACT_TILES)
    if kind == "col":
        tkk = d1 if d1 <= 1024 else _pick(d1, (1024, 512, 256, 128))
        grid = (s * (d1 // tkk), 1, r // tr)
        nkk = d1 // tkk
        a_spec = pl.BlockSpec((tkk, tr), lambda j, i, k: (j % nkk, k))
        b_spec = pl.BlockSpec((tr, d2), lambda j, i, k: (k, j // nkk))
        o_spec = pl.BlockSpec((None, tkk, d2), lambda j, i, k: (j // nkk, j % nkk, 0))
        acc = (tkk, d2)
    else:
        tn = d2 if d2 <= 1024 else _pick(d2, (1024, 512, 256, 128))
        nn = d2 // tn
        grid = (s * nn, 1, r // tr)
        a_spec = pl.BlockSpec((d1, tr), lambda j, i, k: (j // nn, k))
        b_spec = pl.BlockSpec((tr, tn), lambda j, i, k: (k, j % nn))
        o_spec = pl.BlockSpec((None, d1, tn), lambda j, i, k: (j // nn, 0, j % nn))
        acc = (d1, tn)

    def step(a_ref, b_ref):
        return _dot_nn(a_ref[...], b_ref[...])
    return _mm_call(name, grid, at, dy, a_spec, b_spec, o_spec, jax.ShapeDtypeStruct(w_shape, out_dtype), acc, step)


def make_linear(kind, out_dtype, name):
    @jax.custom_vjp
    def linear(a, w3, w_shard):
        return mm_nn(a, w3, kind, out_dtype, name + "_fwd")

    def fwd(a, w3, w_shard):
        return mm_nn(a, w3, kind, out_dtype, name + "_fwd"), (a, w3, w_shard.shape)

    def bwd(res, dy):
        a, w3, shard_shape = res
        dyb = dy.astype(BF16)
        dw = mm_tn(a.T, dyb, w3.shape, kind, BF16, name + "_dw")
        chip_sums, plan = _rs_begin(dw, name)
        da, landed = mm_nt(dyb, w3, kind, a.dtype, name + "_dx", carry=(chip_sums, 3, plan))
        return da, None, _rs_finish(chip_sums, landed, name).reshape(shard_shape)

    linear.defvjp(fwd, bwd)
    return linear


def _row_tile(r, t0, d):
    cap = max(8, (2 * 1024 * 1024) // (4 * d))
    cands = [t for t in (1024, 512, 256, 128, 64, 32, 16, 8) if t <= cap]
    for t in cands:
        if r % t == 0 and t0 % t == 0:
            return t
    raise ValueError("no row tile")


def _grp_spec(d, nb0):
    return pl.BlockSpec((None, 1, d), lambda i: (i // nb0, 0, 0))


def _norm_mod_fwd(z, g, scale, shift, t0, name):
    r, d = z.shape
    tr = _row_tile(r, t0, d)
    nb0 = t0 // tr

    def body(z_ref, g_ref, sc_ref, sh_ref, o_ref):
        zz = z_ref[...]
        rstd = lax.rsqrt(jnp.mean(zz * zz, axis=-1, keepdims=True) + EPS)
        y = zz * rstd * g_ref[...]
        o_ref[...] = (y * (1.0 + sc_ref[...]) + sh_ref[...]).astype(o_ref.dtype)

    return pl.pallas_call(
        body, grid=(r // tr,),
        in_specs=[pl.BlockSpec((tr, d), lambda i: (i, 0)), pl.BlockSpec((1, d), lambda i: (0, 0)),
                  _grp_spec(d, nb0), _grp_spec(d, nb0)],
        out_specs=pl.BlockSpec((tr, d), lambda i: (i, 0)),
        out_shape=jax.ShapeDtypeStruct((r, d), BF16), name=name, compiler_params=_cparams("parallel"))(z, g, scale, shift)


def _norm_mod_bwd(z, g, scale, dh, t0, name):
    r, d = z.shape
    ng = scale.shape[0]
    tr = _row_tile(r, t0, d)
    nb0 = t0 // tr

    def body(z_ref, g_ref, sc_ref, dh_ref, dz_ref, dg_ref, dsc_ref, dsh_ref):
        i = pl.program_id(0)
        zz = z_ref[...]
        gg = g_ref[...]
        rstd = lax.rsqrt(jnp.mean(zz * zz, axis=-1, keepdims=True) + EPS)
        zhat = zz * rstd
        dhh = dh_ref[...].astype(F32)
        dy = dhh * (1.0 + sc_ref[...])
        dyg = dy * gg
        dz_ref[...] = rstd * (dyg - zhat * jnp.mean(dyg * zhat, axis=-1, keepdims=True))

        @pl.when(i == 0)
        def _():
            dg_ref[...] = jnp.zeros_like(dg_ref)

        @pl.when((i == 0) | (i == nb0))
        def _():
            dsc_ref[...] = jnp.zeros_like(dsc_ref)
            dsh_ref[...] = jnp.zeros_like(dsh_ref)

        dg_ref[...] += jnp.sum(dy * zhat, axis=0, keepdims=True)
        dsc_ref[...] += jnp.sum(dhh * (zhat * gg), axis=0, keepdims=True)
        dsh_ref[...] += jnp.sum(dhh, axis=0, keepdims=True)

    return pl.pallas_call(
        body, grid=(r // tr,),
        in_specs=[pl.BlockSpec((tr, d), lambda i: (i, 0)), pl.BlockSpec((1, d), lambda i: (0, 0)),
                  _grp_spec(d, nb0), pl.BlockSpec((tr, d), lambda i: (i, 0))],
        out_specs=[pl.BlockSpec((tr, d), lambda i: (i, 0)), pl.BlockSpec((1, d), lambda i: (0, 0)),
                   _grp_spec(d, nb0), _grp_spec(d, nb0)],
        out_shape=[jax.ShapeDtypeStruct((r, d), F32), jax.ShapeDtypeStruct((1, d), F32),
                   jax.ShapeDtypeStruct((ng, 1, d), F32), jax.ShapeDtypeStruct((ng, 1, d), F32)],
        name=name, compiler_params=_cparams("arbitrary"))(z, g, scale, dh)


def make_norm_mod(t0, name):
    @jax.custom_vjp
    def f(z, g, scale, shift):
        return _norm_mod_fwd(z, g, scale, shift, t0, name + "_fwd")

    def fwd(z, g, scale, shift):
        return _norm_mod_fwd(z, g, scale, shift, t0, name + "_fwd"), (z, g, scale)

    def bwd(res, dh):
        z, g, scale = res
        dz, dg, dsc, dsh = _norm_mod_bwd(z, g, scale, dh, t0, name + "_bwd")
        return dz, dg, dsc, dsh

    f.defvjp(fwd, bwd)
    return f


def _gated_fwd(z, y, gate, t0, name):
    r, d = z.shape
    tr = _row_tile(r, t0, d)
    nb0 = t0 // tr

    def body(z_ref, y_ref, g_ref, o_ref):
        o_ref[...] = z_ref[...] + g_ref[...] * y_ref[...].astype(F32)

    return pl.pallas_call(
        body, grid=(r // tr,),
        in_specs=[pl.BlockSpec((tr, d), lambda i: (i, 0)), pl.BlockSpec((tr, d), lambda i: (i, 0)), _grp_spec(d, nb0)],
        out_specs=pl.BlockSpec((tr, d), lambda i: (i, 0)),
        out_shape=jax.ShapeDtypeStruct((r, d), F32), name=name, compiler_params=_cparams("parallel"))(z, y, gate)


def _gated_bwd(y, gate, dzn, t0, name):
    r, d = y.shape
    ng = gate.shape[0]
    tr = _row_tile(r, t0, d)
    nb0 = t0 // tr

    def body(y_ref, g_ref, dz_ref, dy_ref, dg_ref):
        i = pl.program_id(0)
        dzz = dz_ref[...]
        dy_ref[...] = (g_ref[...] * dzz).astype(dy_ref.dtype)

        @pl.when((i == 0) | (i == nb0))
        def _():
            dg_ref[...] = jnp.zeros_like(dg_ref)

        dg_ref[...] += jnp.sum(dzz * y_ref[...].astype(F32), axis=0, keepdims=True)

    return pl.pallas_call(
        body, grid=(r // tr,),
        in_specs=[pl.BlockSpec((tr, d), lambda i: (i, 0)), _grp_spec(d, nb0), pl.BlockSpec((tr, d), lambda i: (i, 0))],
        out_specs=[pl.BlockSpec((tr, d), lambda i: (i, 0)), _grp_spec(d, nb0)],
        out_shape=[jax.ShapeDtypeStruct((r, d), y.dtype), jax.ShapeDtypeStruct((ng, 1, d), F32)],
        name=name, compiler_params=_cparams("arbitrary"))(y, gate, dzn)


def make_gated_residual(t0, name):
    @jax.custom_vjp
    def f(z, y, gate):
        return _gated_fwd(z, y, gate, t0, name + "_fwd")

    def fwd(z, y, gate):
        return _gated_fwd(z, y, gate, t0, name + "_fwd"), (y, gate)

    def bwd(res, dzn):
        y, gate = res
        dy, dgate = _gated_bwd(y, gate, dzn, t0, name + "_bwd")
        return dzn, dy, dgate

    f.defvjp(fwd, bwd)
    return f


def _ew_call(name, body, ins, outs_sds, r, widths_in, widths_out, tr, extra_in=(), extra_specs=(), sem="parallel"):
    in_specs = [pl.BlockSpec((tr, w), lambda i: (i, 0)) for w in widths_in] + list(extra_specs)
    out_specs = [pl.BlockSpec((tr, w), lambda i: (i, 0)) if w is not None else pl.BlockSpec(s.shape, lambda i: (0,) * len(s.shape))
                 for w, s in zip(widths_out, outs_sds)]
    return pl.pallas_call(body, grid=(r // tr,), in_specs=in_specs, out_specs=out_specs, out_shape=outs_sds,
                          name=name, compiler_params=_cparams(sem))(*ins, *extra_in)


def _silu(x):
    return x * jax.nn.sigmoid(x)


def make_swiglu_act(name):
    def fwd_call(h1, h3):
        r, f = h1.shape
        tr = _row_tile(r, r, f)

        def body(a_ref, b_ref, o_ref):
            o_ref[...] = (_silu(a_ref[...].astype(F32)) * b_ref[...].astype(F32)).astype(o_ref.dtype)

        return _ew_call(name + "_fwd", body, (h1, h3), [jax.ShapeDtypeStruct((r, f), BF16)], r, (f, f), (f,), tr)[0]

    @jax.custom_vjp
    def act(h1, h3):
        return fwd_call(h1, h3)

    def fwd(h1, h3):
        return fwd_call(h1, h3), (h1, h3)

    def bwd(res, da):
        h1, h3 = res
        r, f = h1.shape
        tr = _row_tile(r, r, f)

        def body(a_ref, b_ref, da_ref, d1_ref, d3_ref):
            a = a_ref[...].astype(F32)
            b = b_ref[...].astype(F32)
            g = da_ref[...].astype(F32)
            sg = jax.nn.sigmoid(a)
            d1_ref[...] = (g * b * (sg * (1.0 + a * (1.0 - sg)))).astype(d1_ref.dtype)
            d3_ref[...] = (g * a * sg).astype(d3_ref.dtype)

        sds = jax.ShapeDtypeStruct((r, f), BF16)
        return tuple(_ew_call(name + "_bwd", body, (h1, h3, da), [sds, sds], r, (f, f, f), (f, f), tr))

    act.defvjp(fwd, bwd)
    return act


_GELU_C = math.sqrt(2.0 / math.pi)


def _gelu_and_grad(y):
    inner = _GELU_C * (y + 0.044715 * y * y * y)
    t = jnp.tanh(inner)
    val = 0.5 * y * (1.0 + t)
    grad = 0.5 * (1.0 + t) + 0.5 * y * (1.0 - t * t) * _GELU_C * (1.0 + 3 * 0.044715 * y * y)
    return val, grad


def make_gelu_in(name):
    def fwd_call(u, ys, dsk):
        r, d = u.shape
        tr = _row_tile(r, r, d)

        def body(u_ref, y_ref, d_ref, o_ref):
            y = d_ref[...] * u_ref[...].astype(F32) + y_ref[...]
            o_ref[...] = _gelu_and_grad(y)[0].astype(o_ref.dtype)

        return _ew_call(name + "_fwd", body, (u, ys), [jax.ShapeDtypeStruct((r, d), BF16)], r, (d, d), (d,), tr,
                        extra_in=(dsk,), extra_specs=(pl.BlockSpec((1, d), lambda i: (0, 0)),))[0]

    @jax.custom_vjp
    def f(u, ys, dsk):
        return fwd_call(u, ys, dsk)

    def fwd(u, ys, dsk):
        return fwd_call(u, ys, dsk), (u, ys, dsk)

    def bwd(res, dg):
        u, ys, dsk = res
        r, d = u.shape
        tr = _row_tile(r, r, d)

        def body(u_ref, y_ref, dg_ref, d_ref, du_ref, dy_ref, dd_ref):
            i = pl.program_id(0)
            uu = u_ref[...].astype(F32)
            y = d_ref[...] * uu + y_ref[...]
            dy = dg_ref[...].astype(F32) * _gelu_and_grad(y)[1]
            dy_ref[...] = dy
            du_ref[...] = (d_ref[...] * dy).astype(du_ref.dtype)

            @pl.when(i == 0)
            def _():
                dd_ref[...] = jnp.zeros_like(dd_ref)

            dd_ref[...] += jnp.sum(dy * uu, axis=0, keepdims=True)

        outs = [jax.ShapeDtypeStruct((r, d), u.dtype), jax.ShapeDtypeStruct((r, d), F32), jax.ShapeDtypeStruct((1, d), F32)]
        du, dy, dd = _ew_call(name + "_bwd", body, (u, ys, dg), outs, r, (d, d, d), (d, d, None), tr,
                              extra_in=(dsk,), extra_specs=(pl.BlockSpec((1, d), lambda i: (0, 0)),), sem="arbitrary")
        return du, dy, dd

    f.defvjp(fwd, bwd)
    return f


def make_glu(name):
    def fwd_call(z, b):
        r, d2 = z.shape
        d = d2 // 2
        tr = _row_tile(r, r, d2)

        def body(z_ref, b_ref, o_ref):
            zz = z_ref[...].astype(F32) + b_ref[...]
            o_ref[...] = zz[:, :d] * jax.nn.sigmoid(zz[:, d:])

        return _ew_call(name + "_fwd", body, (z,), [jax.ShapeDtypeStruct((r, d), F32)], r, (d2,), (d,), tr,
                        extra_in=(b,), extra_specs=(pl.BlockSpec((1, d2), lambda i: (0, 0)),))[0]

    @jax.custom_vjp
    def f(z, b):
        return fwd_call(z, b)

    def fwd(z, b):
        return fwd_call(z, b), (z, b)

    def bwd(res, do):
        z, b = res
        r, d2 = z.shape
        d = d2 // 2
        tr = _row_tile(r, r, d2)

        def body(z_ref, do_ref, b_ref, dz_ref, db_ref):
            i = pl.program_id(0)
            zz = z_ref[...].astype(F32) + b_ref[...]
            sg = jax.nn.sigmoid(zz[:, d:])
            g = do_ref[...]
            dza = g * sg
            dzb = g * zz[:, :d] * sg * (1.0 - sg)
            dz_ref[:, :d] = dza.astype(dz_ref.dtype)
            dz_ref[:, d:] = dzb.astype(dz_ref.dtype)

            @pl.when(i == 0)
            def _():
                db_ref[...] = jnp.zeros_like(db_ref)

            db_ref[:, :d] += jnp.sum(dza, axis=0, keepdims=True)
            db_ref[:, d:] += jnp.sum(dzb, axis=0, keepdims=True)

        outs = [jax.ShapeDtypeStruct((r, d2), z.dtype), jax.ShapeDtypeStruct((1, d2), F32)]
        dz, db = _ew_call(name + "_bwd", body, (z, do), outs, r, (d2, d), (d2, None), tr,
                          extra_in=(b,), extra_specs=(pl.BlockSpec((1, d2), lambda i: (0, 0)),), sem="arbitrary")
        return dz, db

    f.defvjp(fwd, bwd)
    return f


def make_final_loss(name):
    def call(z, g, target):
        r, d = z.shape
        tr = _row_tile(r, r, d)

        def body(z_ref, t_ref, g_ref, dz_ref, dg_ref, l_ref):
            i = pl.program_id(0)
            zz = z_ref[...]
            gg = g_ref[...]
            rstd = lax.rsqrt(jnp.mean(zz * zz, axis=-1, keepdims=True) + EPS)
            zhat = zz * rstd
            e = zhat * gg - t_ref[...]
            dy = e * (1.0 / d)
            dyg = dy * gg
            dz_ref[...] = rstd * (dyg - zhat * jnp.mean(dyg * zhat, axis=-1, keepdims=True))

            @pl.when(i == 0)
            def _():
                dg_ref[...] = jnp.zeros_like(dg_ref)
                l_ref[...] = jnp.zeros_like(l_ref)

            dg_ref[...] += jnp.sum(dy * zhat, axis=0, keepdims=True)
            l_ref[...] += jnp.sum(jnp.sum(e * e, axis=1, keepdims=True), axis=0, keepdims=True) * (0.5 / d)

        outs = [jax.ShapeDtypeStruct((r, d), F32), jax.ShapeDtypeStruct((1, d), F32), jax.ShapeDtypeStruct((1, 1), F32)]
        return _ew_call(name, body, (z, target), outs, r, (d, d), (d, None, None), tr,
                        extra_in=(g,), extra_specs=(pl.BlockSpec((1, d), lambda i: (0, 0)),), sem="arbitrary")

    @jax.custom_vjp
    def f(z, g, target):
        return call(z, g, target)[2]

    def fwd(z, g, target):
        dz, dg, loss = call(z, g, target)
        return loss, (dz, dg)

    def bwd(res, dl):
        dz, dg = res
        s = dl[0, 0]
        return dz * s, dg * s, None

    f.defvjp(fwd, bwd)
    return f


def _rope_tables(t):
    quarter = HEAD_DIM // 4
    inv_freq = ROPE_BASE ** (-np.arange(quarter, dtype=np.float64) / quarter)
    pos = np.arange(t)
    ang_r = (pos // GRID_W)[:, None] * inv_freq[None, :]
    ang_c = (pos % GRID_W)[:, None] * inv_freq[None, :]
    cos = np.concatenate([np.cos(ang_r), np.cos(ang_r), np.cos(ang_c), np.cos(ang_c)], axis=1)
    sin = np.concatenate([-np.sin(ang_r), np.sin(ang_r), -np.sin(ang_c), np.sin(ang_c)], axis=1)
    return jnp.asarray(cos, F32), jnp.asarray(sin, F32)


def _rope_call(x, cos, sin, name):
    t, w = x.shape
    tr = _pick(t, (512, 256, 128, 64))
    quarter = HEAD_DIM // 4

    def body(x_ref, c_ref, s_ref, o_ref):
        xx = x_ref[...].astype(F32)
        lane = lax.broadcasted_iota(jnp.int32, xx.shape, 1)
        first = (lane % (2 * quarter)) < quarter
        partner = jnp.where(first, pltpu.roll(xx, HEAD_DIM - quarter, 1), pltpu.roll(xx, quarter, 1))
        o_ref[...] = (xx * c_ref[...] + partner * s_ref[...]).astype(o_ref.dtype)

    return pl.pallas_call(
        body, grid=(t // tr, w // HEAD_DIM),
        in_specs=[pl.BlockSpec((tr, HEAD_DIM), lambda i, j: (i, j)), pl.BlockSpec((tr, HEAD_DIM), lambda i, j: (i, 0)),
                  pl.BlockSpec((tr, HEAD_DIM), lambda i, j: (i, 0))],
        out_specs=pl.BlockSpec((tr, HEAD_DIM), lambda i, j: (i, j)),
        out_shape=jax.ShapeDtypeStruct((t, w), x.dtype), name=name, compiler_params=_cparams("parallel", "parallel"))(x, cos, sin)


def make_rope(t, name):
    cos, sin = _rope_tables(t)

    @jax.custom_vjp
    def f(x):
        return _rope_call(x, cos, sin, name + "_fwd")

    def fwd(x):
        return _rope_call(x, cos, sin, name + "_fwd"), None

    def bwd(_, dy):
        return (_rope_call(dy, cos, -sin, name + "_bwd"),)

    f.defvjp(fwd, bwd)
    return f


def _attn_specs(g, span, tk, m, nbh, has_ctx):
    hd = HEAD_DIM
    q_spec = pl.BlockSpec((ATTN_BLOCK, g * hd), lambda h, i, meta: (i, h))
    kv_spec = pl.BlockSpec((tk, hd), lambda h, i, meta: (0, h))
    c_spec = pl.BlockSpec((m, hd), lambda h, i, meta: (0, h))
    if nbh > 1:
        b_spec = pl.BlockSpec((None, None, ATTN_BLOCK, span), lambda h, i, meta: (meta[1, i], h, 0, 0))
    else:
        b_spec = pl.BlockSpec((None, None, ATTN_BLOCK, span), lambda h, i, meta: (meta[1, i], 0, 0, 0))
    sink_spec = pl.BlockSpec(memory_space=pltpu.SMEM)
    return q_spec, kv_spec, c_spec, b_spec, sink_spec


def _attn_probs(qh, ks, kc, bias, sink_val, scale, has_ctx, has_sink):
    s = _dot_nt(qh, ks) * scale + bias
    mx = jnp.max(s, axis=-1, keepdims=True)
    sc = None
    if has_ctx:
        sc = _dot_nt(qh, kc) * scale
        mx = jnp.maximum(mx, jnp.max(sc, axis=-1, keepdims=True))
    if has_sink:
        mx = jnp.maximum(mx, sink_val)
    p = jnp.exp(s - mx)
    l = jnp.sum(p, axis=-1, keepdims=True)
    pc = None
    if has_ctx:
        pc = jnp.exp(sc - mx)
        l = l + jnp.sum(pc, axis=-1, keepdims=True)
    ps = None
    if has_sink:
        ps = jnp.exp(sink_val - mx)
        l = l + ps
    return p, pc, ps, l


def _attn_fwd(q, k, v, kc, vc, bias, sink, meta, g, span, has_ctx, has_sink, name):
    rq, wq = q.shape
    tk, wk = k.shape
    hkv = wk // HEAD_DIM
    m = kc.shape[0]
    nbh = bias.shape[1]
    scale = HEAD_DIM ** -0.5
    nqb = rq // ATTN_BLOCK
    q_spec, kv_spec, c_spec, b_spec, sink_spec = _attn_specs(g, span, tk, m, nbh, has_ctx)

    def body(meta_ref, sink_ref, q_ref, k_ref, v_ref, kc_ref, vc_ref, b_ref, o_ref):
        h = pl.program_id(0)
        i = pl.program_id(1)
        ks0 = pl.multiple_of(meta_ref[0, i], 64)
        ks = k_ref[pl.ds(ks0, span), :]
        vs = v_ref[pl.ds(ks0, span), :]
        bias_t = b_ref[...]
        for hh in range(g):
            qh = q_ref[:, hh * HEAD_DIM:(hh + 1) * HEAD_DIM]
            sink_val = sink_ref[h * g + hh] if has_sink else None
            p, pc, _, l = _attn_probs(qh, ks, kc_ref[...], bias_t, sink_val, scale, has_ctx, has_sink)
            acc = jnp.dot(p.astype(BF16), vs, preferred_element_type=F32)
            if has_ctx:
                acc = acc + jnp.dot(pc.astype(BF16), vc_ref[...], preferred_element_type=F32)
            o_ref[:, hh * HEAD_DIM:(hh + 1) * HEAD_DIM] = (acc / l).astype(o_ref.dtype)

    gs = pltpu.PrefetchScalarGridSpec(
        num_scalar_prefetch=1, grid=(hkv, nqb),
        in_specs=[sink_spec, q_spec, kv_spec, kv_spec, c_spec, c_spec, b_spec], out_specs=q_spec)
    return pl.pallas_call(body, grid_spec=gs, out_shape=jax.ShapeDtypeStruct((rq, wq), BF16), name=name,
                          compiler_params=_cparams("parallel", "arbitrary"))(meta, sink, q, k, v, kc, vc, bias)


def _attn_bwd(q, k, v, kc, vc, bias, sink, meta, o, do, g, span, has_ctx, has_sink, want_dbias, name):
    rq, wq = q.shape
    tk, wk = k.shape
    hkv = wk // HEAD_DIM
    m = kc.shape[0]
    ncase, nbh = bias.shape[:2]
    scale = HEAD_DIM ** -0.5
    nqb = rq // ATTN_BLOCK
    q_spec, kv_spec, c_spec, b_spec, sink_spec = _attn_specs(g, span, tk, m, nbh, has_ctx)
    dsink_spec = pl.BlockSpec((None, 8, HEAD_DIM), lambda h, i, meta: (h, 0, 0))

    def body(meta_ref, sink_ref, q_ref, k_ref, v_ref, kc_ref, vc_ref, b_ref, o_ref, do_ref,
             dq_ref, dk_ref, dv_ref, dkc_ref, dvc_ref, db_ref, dsk_ref):
        h = pl.program_id(0)
        i = pl.program_id(1)

        @pl.when(i == 0)
        def _():
            dk_ref[...] = jnp.zeros_like(dk_ref)
            dv_ref[...] = jnp.zeros_like(dv_ref)
            dkc_ref[...] = jnp.zeros_like(dkc_ref)
            dvc_ref[...] = jnp.zeros_like(dvc_ref)
            dsk_ref[...] = jnp.zeros_like(dsk_ref)

        if want_dbias:
            @pl.when(meta_ref[2, i] == 1)
            def _():
                db_ref[...] = jnp.zeros_like(db_ref)
        else:
            @pl.when(i == 0)
            def _():
                db_ref[...] = jnp.zeros_like(db_ref)

        ks0 = pl.multiple_of(meta_ref[0, i], 64)
        ks = k_ref[pl.ds(ks0, span), :]
        vs = v_ref[pl.ds(ks0, span), :]
        bias_t = b_ref[...]
        dk_acc = jnp.zeros((span, HEAD_DIM), F32)
        dv_acc = jnp.zeros((span, HEAD_DIM), F32)
        for hh in range(g):
            cols = slice(hh * HEAD_DIM, (hh + 1) * HEAD_DIM)
            qh = q_ref[:, cols]
            doh = do_ref[:, cols]
            sink_val = sink_ref[h * g + hh] if has_sink else None
            p, pc, ps, l = _attn_probs(qh, ks, kc_ref[...], bias_t, sink_val, scale, has_ctx, has_sink)
            inv_l = 1.0 / l
            delta = jnp.sum(doh.astype(F32) * o_ref[:, cols].astype(F32), axis=-1, keepdims=True)
            pn = p * inv_l
            ds = pn * (_dot_nt(doh, vs) - delta)
            dsb = ds.astype(BF16)
            dq = jnp.dot(dsb, ks, preferred_element_type=F32)
            dk_acc = dk_acc + _dot_tn(dsb, qh)
            dv_acc = dv_acc + _dot_tn(pn.astype(BF16), doh)
            if want_dbias:
                db_ref[...] += ds
            if has_ctx:
                pcn = pc * inv_l
                dsc = (pcn * (_dot_nt(doh, vc_ref[...]) - delta)).astype(BF16)
                dq = dq + jnp.dot(dsc, kc_ref[...], preferred_element_type=F32)
                dkc_ref[...] += _dot_tn(dsc, qh) * scale
                dvc_ref[...] += _dot_tn(pcn.astype(BF16), doh)
            if has_sink:
                dsv = -jnp.sum(ps * inv_l * delta, axis=0, keepdims=True)
                dsk_ref[hh:hh + 1, :] += jnp.broadcast_to(dsv, (1, HEAD_DIM))
            dq_ref[:, cols] = (dq * scale).astype(dq_ref.dtype)
        dk_ref[pl.ds(ks0, span), :] += dk_acc * scale
        dv_ref[pl.ds(ks0, span), :] += dv_acc

    gs = pltpu.PrefetchScalarGridSpec(
        num_scalar_prefetch=1, grid=(hkv, nqb),
        in_specs=[sink_spec, q_spec, kv_spec, kv_spec, c_spec, c_spec, b_spec, q_spec, q_spec],
        out_specs=[q_spec, kv_spec, kv_spec, c_spec, c_spec, b_spec if want_dbias else dsink_spec, dsink_spec])
    db_sds = jax.ShapeDtypeStruct((ncase, nbh, ATTN_BLOCK, span) if want_dbias else (hkv, 8, HEAD_DIM), F32)
    out_shape = [jax.ShapeDtypeStruct((rq, wq), BF16), jax.ShapeDtypeStruct((tk, wk), F32), jax.ShapeDtypeStruct((tk, wk), F32),
                 jax.ShapeDtypeStruct((m, wk), F32), jax.ShapeDtypeStruct((m, wk), F32), db_sds,
                 jax.ShapeDtypeStruct((hkv, 8, HEAD_DIM), F32)]
    return pl.pallas_call(body, grid_spec=gs, out_shape=out_shape, name=name,
                          compiler_params=_cparams("parallel", "arbitrary"))(meta, sink, q, k, v, kc, vc, bias, o, do)


def make_attention(meta_np, g, span, has_ctx, has_sink, want_dbias, name):
    meta = jnp.asarray(meta_np, jnp.int32)

    @jax.custom_vjp
    def f(q, k, v, kc, vc, bias, sink):
        return _attn_fwd(q, k, v, kc, vc, bias, sink, meta, g, span, has_ctx, has_sink, name + "_fwd")

    def fwd(q, k, v, kc, vc, bias, sink):
        o = _attn_fwd(q, k, v, kc, vc, bias, sink, meta, g, span, has_ctx, has_sink, name + "_fwd")
        return o, (q, k, v, kc, vc, bias, sink, o)

    def bwd(res, do):
        q, k, v, kc, vc, bias, sink, o = res
        dq, dk, dv, dkc, dvc, db, dsk = _attn_bwd(q, k, v, kc, vc, bias, sink, meta, o, do.astype(BF16), g, span,
                                                   has_ctx, has_sink, want_dbias, name + "_bwd")
        dsink = dsk[:, :g, 0].reshape(sink.shape)
        if not want_dbias:
            db = jnp.zeros_like(bias)
        return dq, dk.astype(k.dtype), dv.astype(v.dtype), dkc.astype(kc.dtype), dvc.astype(vc.dtype), db, dsink

    f.defvjp(fwd, bwd)
    return f


def _dedupe_cases(tables):
    cases, idx, first = [], [], []
    for tbl in tables:
        if cases and np.array_equal(cases[-1], tbl):
            idx.append(len(cases) - 1)
            first.append(0)
        else:
            cases.append(tbl)
            idx.append(len(cases) - 1)
            first.append(1)
    return cases, idx, first


def _na_plan(t):
    rows = t // GRID_W
    qr = ATTN_BLOCK // GRID_W
    kr = qr + NA_ROWS - 1
    assert rows >= kr and rows % qr == 0
    span = kr * GRID_W
    kstart, tables = [], []
    qcol = np.tile(np.arange(GRID_W), qr)
    kcol = np.tile(np.arange(GRID_W), kr)
    win_c = np.clip(qcol - NA_COLS // 2, 0, GRID_W - NA_COLS)
    col_ok = (kcol[None, :] >= win_c[:, None]) & (kcol[None, :] < win_c[:, None] + NA_COLS)
    dcol = np.clip(kcol[None, :] - qcol[:, None] + NA_COLS - 1, 0, 2 * NA_COLS - 2)
    for r0 in range(0, rows, qr):
        kb = int(np.clip(r0 - NA_ROWS // 2, 0, rows - kr))
        qrow = r0 + np.repeat(np.arange(qr), GRID_W)
        krow = kb + np.repeat(np.arange(kr), GRID_W)
        win_r = np.clip(qrow - NA_ROWS // 2, 0, rows - NA_ROWS)
        row_ok = (krow[None, :] >= win_r[:, None]) & (krow[None, :] < win_r[:, None] + NA_ROWS)
        drow = np.clip(krow[None, :] - qrow[:, None] + NA_ROWS - 1, 0, 2 * NA_ROWS - 2)
        tables.append(np.stack([row_ok & col_ok, drow, dcol]).astype(np.int32))
        kstart.append(kb * GRID_W)
    cases, idx, first = _dedupe_cases(tables)
    meta = np.array([kstart, idx, first], np.int32)
    return meta, span, np.stack(cases)


def _na_bias(rpb, cases):
    valid, drow, dcol = cases[:, 0], cases[:, 1], cases[:, 2]
    ncase, qn, span = valid.shape
    qr, kr = qn // GRID_W, span // GRID_W
    drow_s = drow.reshape(ncase, qr, GRID_W, kr, GRID_W)[:, :, 0, :, 0]
    dcol_s = dcol[0].reshape(qr, GRID_W, kr, GRID_W)[0, :, 0, :]
    oh_r = jnp.asarray(np.eye(2 * NA_ROWS - 1, dtype=np.float32)[drow_s])
    oh_c = jnp.asarray(np.eye(2 * NA_COLS - 1, dtype=np.float32)[dcol_s])
    tmp = jnp.einsum("hrc,xyc->hrxy", rpb, oh_c, precision=lax.Precision.HIGHEST)
    b = jnp.einsum("nakr,hrxy->nhaxky", oh_r, tmp, precision=lax.Precision.HIGHEST).reshape(ncase, -1, qn, span)
    return jnp.where(jnp.asarray(valid[:, None] > 0), b, NEG_INF)


def _sw_plan(t):
    span = 3 * ATTN_BLOCK
    assert t >= span
    kstart, tables = [], []
    for b in range(t // ATTN_BLOCK):
        ks = int(np.clip((b - 1) * ATTN_BLOCK, 0, t - span))
        qpos = b * ATTN_BLOCK + np.arange(ATTN_BLOCK)
        kpos = ks + np.arange(span)
        ok = np.abs(kpos[None, :] - qpos[:, None]) <= SW_RADIUS
        tables.append(np.where(ok, 0.0, NEG_INF).astype(np.float32))
        kstart.append(ks)
    cases, idx, first = _dedupe_cases(tables)
    return np.array([kstart, idx, first], np.int32), span, np.stack(cases)[:, None]


def _cmul(ar, ai, br, bi):
    return ar * br - ai * bi, ar * bi + ai * br


def _s5_scan_call(x2, win, lam, cin, wout, reverse, n_chunks, name):
    _, ll, d = x2.shape
    nt = d // HEAD_DIM
    sw = 2 * SSM_TILE_GROUPS * SSM_STATE
    hw = sw // 2
    rows = ll // n_chunks
    ic = rows // SCAN_BLOCKS
    full = cin is not None

    down_dir = 0 if reverse else 1

    def chunk_idx(k, dd):
        return jnp.where(dd == down_dir, n_chunks - 1 - k, k)

    def body(*refs):
        if full:
            x_ref, win_ref, lam_ref, cin_ref, wout_ref, s_out, y_out, ub_ref, st_ref = refs
        else:
            x_ref, win_ref, lam_ref, f_out, ub_ref, st_ref = refs
        k = pl.program_id(2)
        down = pl.program_id(0) == down_dir

        @pl.when(k == 0)
        def _():
            st_ref[...] = cin_ref[...] if full else jnp.zeros_like(st_ref)

        ub_ref[...] = jnp.dot(x_ref[...].astype(BF16), win_ref[...], preferred_element_type=F32)
        lr = lam_ref[:, :hw]
        li = lam_ref[:, hw:]

        def step(ii, carry):
            sr, si = carry
            i = jnp.where(down, ic - 1 - ii, ii)
            r0 = pl.multiple_of(i * SCAN_BLOCKS, SCAN_BLOCKS)
            ur = ub_ref[pl.ds(r0, SCAN_BLOCKS), :hw]
            ui = ub_ref[pl.ds(r0, SCAN_BLOCKS), hw:]
            nr = lr * sr - li * si + ur
            ni = lr * si + li * sr + ui
            if full:
                ub_ref[pl.ds(r0, SCAN_BLOCKS), :hw] = nr
                ub_ref[pl.ds(r0, SCAN_BLOCKS), hw:] = ni
            return nr, ni

        sr, si = lax.fori_loop(0, ic, step, (st_ref[:, :hw], st_ref[:, hw:]), unroll=4 if ic % 4 == 0 else 1)
        st_ref[:, :hw] = sr
        st_ref[:, hw:] = si
        if full:
            sb = ub_ref[...].astype(BF16)
            s_out[...] = sb
            y_out[...] = jnp.dot(sb, wout_ref[...], preferred_element_type=F32)
        else:
            @pl.when(k == n_chunks - 1)
            def _():
                f_out[...] = st_ref[...]

    x_spec = pl.BlockSpec((None, rows, HEAD_DIM), lambda dd, t, k: (dd, chunk_idx(k, dd), t))
    win_spec = pl.BlockSpec((None, None, HEAD_DIM, sw), lambda dd, t, k: (dd, t, 0, 0))
    vec_spec = pl.BlockSpec((None, None, SCAN_BLOCKS, sw), lambda dd, t, k: (dd, t, 0, 0))
    scratch = [pltpu.VMEM((rows, sw), F32), pltpu.VMEM((SCAN_BLOCKS, sw), F32)]
    if full:
        in_specs = [x_spec, win_spec, vec_spec, vec_spec, pl.BlockSpec((None, None, sw, HEAD_DIM), lambda dd, t, k: (dd, t, 0, 0))]
        out_specs = [pl.BlockSpec((None, None, rows, sw), lambda dd, t, k: (dd, t, chunk_idx(k, dd), 0)), x_spec]
        out_shape = [jax.ShapeDtypeStruct((2, nt, ll, sw), BF16), jax.ShapeDtypeStruct((2, ll, d), F32)]
        args = (x2, win, lam, cin, wout)
    else:
        in_specs = [x_spec, win_spec, vec_spec]
        out_specs = vec_spec
        out_shape = jax.ShapeDtypeStruct((2, nt, SCAN_BLOCKS, sw), F32)
        args = (x2, win, lam)
    return pl.pallas_call(body, grid=(2, nt, n_chunks), in_specs=in_specs, out_specs=out_specs, out_shape=out_shape,
                          scratch_shapes=scratch, name=name,
                          compiler_params=_cparams("parallel", "parallel", "arbitrary"))(*args)


def _s5_bwd_call(dy2, wrt, lamc, cin, st, u2, wdt, n_chunks, name):
    _, ll, d = dy2.shape
    nt = d // HEAD_DIM
    sw = 2 * SSM_TILE_GROUPS * SSM_STATE
    hw = sw // 2
    rows = ll // n_chunks
    ic = rows // SCAN_BLOCKS

    def chunk_idx(k, dd):
        return jnp.where(dd == 0, n_chunks - 1 - k, k)

    def body(dy_ref, wrt_ref, lam_ref, cin_ref, stb_ref, u_ref, wdt_ref, du_out, dwd_out, dwr_out, dlam_out,
             ds_ref, a_ref, st_ref):
        k = pl.program_id(2)
        down = pl.program_id(0) == 0
        st_ref[...] = stb_ref[...].astype(F32)

        @pl.when(k == 0)
        def _():
            a_ref[...] = cin_ref[...]
            dwd_out[...] = jnp.zeros_like(dwd_out)
            dwr_out[...] = jnp.zeros_like(dwr_out)
            dlam_out[...] = jnp.zeros_like(dlam_out)

        dyb = dy_ref[...].astype(BF16)
        ds_ref[...] = jnp.dot(dyb, wrt_ref[...], preferred_element_type=F32)
        dwr_out[...] += _dot_tn(stb_ref[...], dyb)
        lr = lam_ref[:, :hw]
        li = lam_ref[:, hw:]

        def step(ii, carry):
            ar, ai, gr, gi = carry
            i = jnp.where(down, ic - 1 - ii, ii)
            r0 =pl.multiple_of(i * SCAN_BLOCKS, SCAN_BLOCKS)
            sr = st_ref[pl.ds(r0, SCAN_BLOCKS), :hw]
            si = st_ref[pl.ds(r0, SCAN_BLOCKS), hw:]
            gr = gr + ar * sr + ai * si
            gi = gi + ai * sr - ar * si
            nr = lr * ar - li * ai + ds_ref[pl.ds(r0, SCAN_BLOCKS), :hw]
            ni = lr * ai + li * ar + ds_ref[pl.ds(r0, SCAN_BLOCKS), hw:]
            ds_ref[pl.ds(r0, SCAN_BLOCKS), :hw] = nr
            ds_ref[pl.ds(r0, SCAN_BLOCKS), hw:] = ni
            return nr, ni, gr, gi

        init = (a_ref[:, :hw], a_ref[:, hw:], dlam_out[:, :hw], dlam_out[:, hw:])
        ar, ai, gr, gi = lax.fori_loop(0, ic, step, init, unroll=4 if ic % 4 == 0 else 1)
        a_ref[:, :hw] = ar
        a_ref[:, hw:] = ai
        dlam_out[:, :hw] = gr
        dlam_out[:, hw:] = gi
        ab = ds_ref[...].astype(BF16)
        du_out[...] = jnp.dot(ab, wdt_ref[...], preferred_element_type=F32).astype(du_out.dtype)
        dwd_out[...] += _dot_tn(u_ref[...].astype(BF16), ab)

    x_spec = pl.BlockSpec((None, rows, HEAD_DIM), lambda dd, t, k: (dd, chunk_idx(k, dd), t))
    w_in = pl.BlockSpec((None, None, HEAD_DIM, sw), lambda dd, t, k: (dd, t, 0, 0))
    w_out = pl.BlockSpec((None, None, sw, HEAD_DIM), lambda dd, t, k: (dd, t, 0, 0))
    vec_spec = pl.BlockSpec((None, None, SCAN_BLOCKS, sw), lambda dd, t, k: (dd, t, 0, 0))
    st_spec = pl.BlockSpec((None, None, rows, sw), lambda dd, t, k: (dd, t, chunk_idx(k, dd), 0))
    out_shape = [jax.ShapeDtypeStruct((2, ll, d), u2.dtype), jax.ShapeDtypeStruct((2, nt, HEAD_DIM, sw), F32),
                 jax.ShapeDtypeStruct((2, nt, sw, HEAD_DIM), F32), jax.ShapeDtypeStruct((2, nt, SCAN_BLOCKS, sw), F32)]
    return pl.pallas_call(
        body, grid=(2, nt, n_chunks),
        in_specs=[x_spec, w_in, vec_spec, vec_spec, st_spec, x_spec, w_out],
        out_specs=[x_spec, w_in, w_out, vec_spec], out_shape=out_shape,
        scratch_shapes=[pltpu.VMEM((rows, sw), F32), pltpu.VMEM((SCAN_BLOCKS, sw), F32), pltpu.VMEM((rows, sw), F32)],
        name=name, compiler_params=_cparams("parallel", "parallel", "arbitrary"))(dy2, wrt, lamc, cin, st, u2, wdt)


def _cpow(lr, li, n):
    rr, ri = jnp.ones_like(lr), jnp.zeros_like(li)
    br, bi = lr, li
    while n:
        if n & 1:
            rr, ri = _cmul(rr, ri, br, bi)
        br, bi = _cmul(br, bi, br, bi)
        n >>= 1
    return rr, ri


def _resolve_carries(finals, lam, block_len, down_dir):
    hw = finals.shape[-1] // 2
    pr, pi = _cpow(lam[:, :, 0, :hw], lam[:, :, 0, hw:], block_len)
    fr, fi = finals[..., :hw], finals[..., hw:]

    def walk(order):
        cr, ci = jnp.zeros_like(pr), jnp.zeros_like(pi)
        out = [None] * SCAN_BLOCKS
        for j in order:
            out[j] = jnp.concatenate([cr, ci], axis=-1)
            mr, mi = _cmul(pr, pi, cr, ci)
            cr, ci = mr + fr[:, :, j], mi + fi[:, :, j]
        return jnp.stack(out, axis=2)

    up, down = walk(range(SCAN_BLOCKS)), walk(range(SCAN_BLOCKS - 1, -1, -1))
    return jnp.stack([down[0], up[1]] if down_dir == 0 else [up[0], down[1]])


def _scan_chunks(ll):
    block_len = ll // SCAN_BLOCKS
    for ic in (132, 128, 96, 64, 48, 36, 32, 24, 16, 8):
        if block_len % ic == 0:
            return block_len // ic
    return 1


def make_s5_core(name):
    def run_fwd(u2, lam, wd, wr):
        ll = u2.shape[1]
        nc = _scan_chunks(ll)
        lam8 = jnp.broadcast_to(lam[:, :, None, :], lam.shape[:2] + (SCAN_BLOCKS, lam.shape[-1]))
        wdb = wd.astype(BF16)
        finals = _s5_scan_call(u2, wdb, lam8, None, None, False, nc, name + "_carry")
        cin = _resolve_carries(finals, lam8, ll // SCAN_BLOCKS, 1)
        st, y2 = _s5_scan_call(u2, wdb, lam8, cin, wr.astype(BF16), False, nc, name + "_scan")
        return y2, st, lam8

    @jax.custom_vjp
    def f(u2, lam, wd, wr):
        return run_fwd(u2, lam, wd, wr)[0]

    def fwd(u2, lam, wd, wr):
        y2, st, lam8 = run_fwd(u2, lam, wd, wr)
        return y2, (u2, lam8, wd, wr, st)

    def bwd(res, dy2):
        u2, lam8, wd, wr, st = res
        ll = u2.shape[1]
        nc = _scan_chunks(ll)
        hw = lam8.shape[-1] // 2
        lamc = jnp.concatenate([lam8[..., :hw], -lam8[..., hw:]], axis=-1)
        wrt = jnp.swapaxes(wr, 2, 3).astype(BF16)
        wdt = jnp.swapaxes(wd, 2, 3).astype(BF16)
        finals = _s5_scan_call(dy2, wrt, lamc, None, None, True, nc, name + "_bcarry")
        cin = _resolve_carries(finals, lamc, ll // SCAN_BLOCKS, 0)
        du2, dwd, dwr, dlam8 = _s5_bwd_call(dy2, wrt, lamc, cin, st, u2, wdt, nc, name + "_bscan")
        return du2, jnp.sum(dlam8, axis=2), dwd, dwr

    f.defvjp(fwd, bwd)
    return f


def _s5_params(a_re, a_im, log_dt, b_re, b_im, c_re, c_im):
    dt = jnp.exp(log_dt)[..., None]
    mag = jnp.exp(a_re * dt)
    lam_r, lam_i = mag * jnp.cos(a_im * dt), mag * jnp.sin(a_im * dt)
    den = a_re * a_re + a_im * a_im
    nr = lam_r - 1.0
    coef_r = (nr * a_re + lam_i * a_im) / den
    coef_i = (lam_i * a_re - nr * a_im) / den
    bbar_r = coef_r[..., None] * b_re - coef_i[..., None] * b_im
    bbar_i = coef_r[..., None] * b_im + coef_i[..., None] * b_re
    ndir, g, p = lam_r.shape
    tg = SSM_TILE_GROUPS
    nt = g // tg
    eye = jnp.eye(tg, dtype=F32)

    def tile_vec(v):
        return v.reshape(ndir, nt, tg * p)

    lam = jnp.concatenate([tile_vec(lam_r), tile_vec(lam_i)], axis=-1)

    def drive(b):
        bt = b.reshape(ndir, nt, tg, p, SSM_GROUP)
        return (jnp.swapaxes(bt, 3, 4)[:, :, :, :, None, :] * eye[None, None, :, None, :, None]).reshape(ndir, nt, tg * SSM_GROUP, tg * p)

    wd = jnp.concatenate([drive(bbar_r), drive(bbar_i)], axis=-1)

    def readout(c):
        ct = c.reshape(ndir, nt, tg, SSM_GROUP, p)
        return (jnp.swapaxes(ct, 3, 4)[:, :, :, :, None, :] * eye[None, None, :, None, :, None]).reshape(ndir, nt, tg * p, tg * SSM_GROUP)

    wr = jnp.concatenate([readout(c_re), -readout(c_im)], axis=2)
    return lam, wd, wr


def _to_scan_order(seq):
    ll, d = seq.shape
    return seq.reshape(SCAN_BLOCKS, ll // SCAN_BLOCKS, d).swapaxes(0, 1).reshape(ll, d)


def _from_scan_order(y2):
    ll, d = y2.shape
    return y2.reshape(ll // SCAN_BLOCKS, SCAN_BLOCKS, d).swapaxes(0, 1).reshape(ll, d)


def adamw(w, g, m, v, name):
    shape = w.shape
    cols = shape[-1] if len(shape) > 1 else shape[0]
    w2, g2, m2, v2 = (a.reshape(-1, cols) for a in (w, g, m, v))
    r = w2.shape[0]
    cap = max(1, (1024 * 1024) // (4 * cols))
    tr = r
    for t in (512, 256, 128, 64, 32, 16, 8):
        if t <= cap and r % t == 0:
            tr = t
            break
    c1 = 1.0 / (1.0 - ADAM_B1 ** ADAM_STEP)
    c2 = 1.0 / (1.0 - ADAM_B2 ** ADAM_STEP)

    def body(w_ref, g_ref, m_ref, v_ref, d_ref, mo_ref, vo_ref):
        gg = g_ref[...]
        mn = ADAM_B1 * m_ref[...] + (1.0 - ADAM_B1) * gg
        vn = ADAM_B2 * v_ref[...] + (1.0 - ADAM_B2) * (gg * gg)
        d_ref[...] = -ADAM_LR * ((mn * c1) / (jnp.sqrt(vn * c2) + ADAM_EPS) + ADAM_WD * w_ref[...])
        mo_ref[...] = mn
        vo_ref[...] = vn

    spec = pl.BlockSpec((tr, cols), lambda i: (i, 0))
    sds = jax.ShapeDtypeStruct((r, cols), F32)
    d, mn, vn = pl.pallas_call(body, grid=(r // tr,), in_specs=[spec] * 4, out_specs=[spec] * 3, out_shape=[sds] * 3,
                               name=name, compiler_params=_cparams("parallel"))(w2, g2, m2, v2)
    return d.reshape(shape), mn.reshape(shape), vn.reshape(shape)


def _my_pos():
    return lax.axis_index("x"), lax.axis_index("y"), lax.axis_index("c")


def _flip(pos, f):
    return tuple((1 - p) if b else p for p, b in zip(pos, f))


def _lin(pos):
    return 4 * pos[0] + 2 * pos[1] + pos[2]


def _remote_copies(src_ref, out_ref, send_sems, recv_sems, plan):
    me = _my_pos()
    copies = []
    for k, (f, sfn, dfn) in enumerate(plan):
        peer = _flip(me, f)
        copies.append(pltpu.make_async_remote_copy(
            src_ref=src_ref.at[sfn(me, peer)], dst_ref=out_ref.at[dfn(me, peer)], send_sem=send_sems.at[k],
            recv_sem=recv_sems.at[k], device_id=peer, device_id_type=MESH))
    return copies


def xchg(src, n_out, plan, name, inplace=False):
    piece = src.shape[1:]

    def body(src_ref, out_ref, send_sems, recv_sems):
        me = _my_pos()
        copies = []
        for k, (f, sfn, dfn) in enumerate(plan):
            peer = _flip(me, f)
            s_ref = (out_ref if inplace else src_ref).at[sfn(me, peer)]
            d_ref = out_ref.at[dfn(me, peer)]
            if any(f):
                cp = pltpu.make_async_remote_copy(src_ref=s_ref, dst_ref=d_ref, send_sem=send_sems.at[k],
                                                  recv_sem=recv_sems.at[k], device_id=peer, device_id_type=MESH)
            else:
                cp = pltpu.make_async_copy(s_ref, d_ref, recv_sems.at[k])
            cp.start()
            copies.append((cp, any(f)))
        for cp, remote in copies:
            if remote:
                cp.wait_recv()
            else:
                cp.wait()
        for cp, remote in copies:
            if remote:
                cp.wait_send()

    return pl.pallas_call(
        body, in_specs=[pl.BlockSpec(memory_space=pl.ANY)], out_specs=pl.BlockSpec(memory_space=pl.ANY),
        out_shape=jax.ShapeDtypeStruct((n_out,) + piece, src.dtype),
        scratch_shapes=[pltpu.SemaphoreType.DMA((len(plan),)), pltpu.SemaphoreType.DMA((len(plan),))],
        input_output_aliases={0: 0} if inplace else {}, name=name)(src)


_CHIP_FLIPS = ((1, 0, 0), (0, 1, 0), (1, 1, 0))
_ALL_FLIPS = tuple((a, b, c) for a in (0, 1) for b in (0, 1) for c in (0, 1))[1:]


def all_to_all8(src, name):
    plan = [((0, 0, 0), lambda me, peer: _lin(me), lambda me, peer: _lin(me))]
    plan += [(f, lambda me, peer: _lin(peer), lambda me, peer: _lin(me)) for f in _ALL_FLIPS]
    return xchg(src, N_DEV, plan, name)


def all_gather8(piece, name):
    plan = [((0, 0, 0), lambda me, peer: 0, lambda me, peer: _lin(me))]
    plan += [(f, lambda me, peer: 0, lambda me, peer: _lin(me)) for f in _ALL_FLIPS]
    return xchg(piece[None], N_DEV, plan, name)


def gather_weight(shard, name):
    k, ns = shard.shape
    px, py, _ = _my_pos()
    own = shard.astype(BF16)[None]
    buf = lax.dynamic_update_slice(jnp.zeros((4, k, ns), BF16), own, (2 * px + py, 0, 0)).reshape(8, k // 2, ns)

    def body(in_ref, out_ref, send_sems, recv_sems):
        me = _my_pos()
        sibling = _flip(me, (0, 0, 1))
        chips = [_flip(me, f) for f in _CHIP_FLIPS]

        def copy(sem, holder, to):
            rows = out_ref.at[4 * holder[0] + 2 * holder[1] + me[2]]
            return pltpu.make_async_remote_copy(src_ref=rows, dst_ref=rows, send_sem=send_sems.at[sem],
                                                recv_sem=recv_sems.at[sem], device_id=to, device_id_type=MESH)

        first = [copy(j, me, chip) for j, chip in enumerate(chips)]
        for cp in first:
            cp.start()
        passed = [copy(3 + j, chip, sibling) for j, chip in enumerate(chips)]
        for j, chip in enumerate(chips):
            copy(j, chip, me).wait_recv()
            passed[j].start()
        for j in range(3):
            passed[j].wait_recv()
        for cp in first + passed:
            cp.wait_send()

    full = pl.pallas_call(
        body, in_specs=[pl.BlockSpec(memory_space=pl.ANY)], out_specs=pl.BlockSpec(memory_space=pl.ANY),
        out_shape=jax.ShapeDtypeStruct(buf.shape, BF16),
        scratch_shapes=[pltpu.SemaphoreType.DMA((6,)), pltpu.SemaphoreType.DMA((6,))],
        input_output_aliases={0: 0}, name=name)(buf)
    return full.reshape(4, k, ns)


def _sum_halves(g8, l1, c_idx, name):
    _, _, r, cc = g8.shape
    tr = _row_tile(r, r, cc)

    def body(c_ref, a_ref, b_ref, o_ref):
        o_ref[...] = (a_ref[...].astype(F32) + b_ref[...].astype(F32)).astype(o_ref.dtype)

    gs = pltpu.PrefetchScalarGridSpec(
        num_scalar_prefetch=1, grid=(4, r // tr),
        in_specs=[pl.BlockSpec((None, None, tr, cc), lambda s, i, c: (s, c[0], i, 0)),
                  pl.BlockSpec((None, tr, cc), lambda s, i, c: (s, i, 0))],
        out_specs=pl.BlockSpec((None, tr, cc), lambda s, i, c: (s, i, 0)))
    return pl.pallas_call(body, grid_spec=gs, out_shape=jax.ShapeDtypeStruct((4, r, cc), BF16), name=name,
                          compiler_params=_cparams("parallel", "parallel"))(c_idx, g8, l1)


def _sum_chips(p4, l2, sc_idx, name):
    _, r, cc = p4.shape
    tr = _row_tile(r, r, cc)

    def body(s_ref, a_ref, b0_ref, b1_ref, b2_ref, o_ref):
        o_ref[...] = ((a_ref[...].astype(F32) + b0_ref[...].astype(F32)) + b1_ref[...].astype(F32)) + b2_ref[...].astype(F32)

    gs = pltpu.PrefetchScalarGridSpec(
        num_scalar_prefetch=1, grid=(r // tr,),
        in_specs=[pl.BlockSpec((None, tr, cc), lambda i, s: (s[0], i, 0))]
        + [pl.BlockSpec((None, tr, cc), lambda i, s, j=j: (j, i, 0)) for j in range(3)],
        out_specs=pl.BlockSpec((None, tr, cc), lambda i, s: (s[1], i, 0)))
    return pl.pallas_call(body, grid_spec=gs, out_shape=jax.ShapeDtypeStruct((2, r, cc), F32), name=name,
                          compiler_params=_cparams("parallel"))(sc_idx, p4, l2, l2, l2)


def _rs_begin(g4, name):
    _, k, ns = g4.shape
    c_idx = jnp.reshape(_my_pos()[2], (1,)).astype(jnp.int32)
    plan1 = [((0, 0, 1), lambda me, peer, s=s: 2 * s + peer[2], lambda me, peer, s=s: s) for s in range(4)]
    l1 = xchg(g4.reshape(8, k // 2, ns), 4, plan1, name + "_rs_d2d")
    p4 = _sum_halves(g4.reshape(4, 2, k // 2, ns), l1, c_idx, name + "_rs_sum2")
    plan2 = [(f, lambda me, peer: 2 * peer[0] + peer[1], lambda me, peer, j=j: j) for j, f in enumerate(_CHIP_FLIPS)]
    return p4, plan2


def _rs_finish(p4, landed, name):
    _, kh, ns = p4.shape
    x, y, c = _my_pos()
    sc_idx = jnp.stack([2 * x + y, c]).astype(jnp.int32)
    halves = _sum_chips(p4, landed, sc_idx, name + "_rs_sum4")
    plan3 = [((0, 0, 1), lambda me, peer: me[2], lambda me, peer: me[2])]
    return xchg(halves, 2, plan3, name + "_rs_swap", inplace=True).reshape(2 * kh, ns)


def reduce_scatter_weight(g4, name):
    p4, plan = _rs_begin(g4, name)
    return _rs_finish(p4, xchg(p4, 3, plan, name + "_rs_ici"), name)


def _sum8(a8, name):
    _, r, cc = a8.shape
    tr = _row_tile(r, r, cc)

    def body(a_ref, o_ref):
        acc = a_ref[0]
        for j in range(1, N_DEV):
            acc = acc + a_ref[j]
        o_ref[...] = acc

    return pl.pallas_call(body, grid=(r // tr,), in_specs=[pl.BlockSpec((N_DEV, tr, cc), lambda i: (0, i, 0))],
                          out_specs=pl.BlockSpec((tr, cc), lambda i: (i, 0)), out_shape=jax.ShapeDtypeStruct((r, cc), F32),
                          name=name, compiler_params=_cparams("parallel"))(a8)


def all_reduce8(flat, name):
    n = flat.shape[0]
    unit = N_DEV * 8 * 128
    npad = -(-n // unit) * unit
    a = jnp.pad(flat, (0, npad - n)).reshape(N_DEV, npad // (N_DEV * 128), 128)
    mine = _sum8(all_to_all8(a, name + "_rs"), name + "_sum")
    return all_gather8(mine, name + "_ag").reshape(npad)[:n]


def _make_split(t, cuts):
    def pieces(qkv):
        out = []
        for rows in (slice(None, t), slice(t, None)):
            out += [qkv[rows, a:b] for a, b in zip(cuts[:-1], cuts[1:])]
        return tuple(out)

    @jax.custom_vjp
    def split(qkv):
        return pieces(qkv)

    def fwd(qkv):
        return pieces(qkv), None

    def bwd(_, g):
        n = len(cuts) - 1
        return (jnp.concatenate([jnp.concatenate(g[:n], axis=1), jnp.concatenate(g[n:], axis=1)], axis=0),)

    split.defvjp(fwd, bwd)
    return split


def _local_loss(x, ctx, target, mods, small, big, shards, dims):
    t, m, d = dims["t"], dims["m"], dims["d"]
    a_w, bq_w, bkv_w = dims["a_w"], dims["bq_w"], dims["bkv_w"]
    z = jnp.concatenate([x, ctx], axis=0)

    def grp(layer, j, n_groups=2):
        return mods[layer, :n_groups, j][:, None, :]

    def lin(kind, out_dtype, name, a, wname):
        return make_linear(kind, out_dtype, name)(a, big[wname], shards[wname])

    h = make_norm_mod(t, "norm_mix0")(z, small["norm_mix"][0][None], grp(0, 1), grp(0, 0))
    qkv = lin("col", BF16, "attn_in", h, "attn_w_in")
    o3, o5 = 3 * a_w, 3 * a_w + bq_w + bkv_w
    cuts = (0, a_w, 2 * a_w, o3, o3 + bq_w, o5, o5 + bkv_w)
    qa, ka, va, qb_u, kb_u, vb, qa_c, ka_c, va_c, qb_c, kb_c, vb_c = _make_split(t, cuts)(qkv)
    qb, kb = make_rope(t, "rope_q")(qb_u), make_rope(t, "rope_k")(kb_u)
    sink = small["attn_sink"][0]
    no_sink = jnp.zeros((a_w // HEAD_DIM,), F32)
    na_meta, na_span, na_cases = _na_plan(t)
    oa = make_attention(na_meta, 1, na_span, True, False, True, "na")(
        qa, ka, va, ka_c, va_c, _na_bias(small["attn_rpb"][0], na_cases), no_sink)
    sw_meta, sw_span, sw_bias = _sw_plan(t)
    grp_b = bq_w // bkv_w
    ob = make_attention(sw_meta, grp_b, sw_span, True, True, False, "swa")(qb, kb, vb, kb_c, vb_c, jnp.asarray(sw_bias), sink)
    c_meta = np.array([[0] * (m // ATTN_BLOCK), [0] * (m // ATTN_BLOCK), [1] + [0] * (m // ATTN_BLOCK - 1)], np.int32)
    zero_bias = jnp.zeros((1, 1, ATTN_BLOCK, m), F32)
    oa_c = make_attention(c_meta, 1, m, False, False, False, "ctx_na")(qa_c, ka_c, va_c, ka_c, va_c, zero_bias, no_sink)
    ob_c = make_attention(c_meta, grp_b, m, False, True, False, "ctx_swa")(qb_c, kb_c, vb_c, kb_c, vb_c, zero_bias, sink)
    o = jnp.concatenate([jnp.concatenate([oa, ob], axis=1), jnp.concatenate([oa_c, ob_c], axis=1)], axis=0)
    y = lin("row", F32, "attn_out", o, "attn_w_out")
    z = make_gated_residual(t, "res_mix0")(z, y, grp(0, 2))
    h = make_norm_mod(t, "norm_ffn0")(z, small["norm_ffn"][0][None], grp(0, 4), grp(0, 3))
    a = make_swiglu_act("act0")(lin("col", BF16, "ffn0_w1", h, "ffn_w1_0"), lin("col", BF16, "ffn0_w3", h, "ffn_w3_0"))
    z = make_gated_residual(t, "res_ffn0")(z, lin("row", F32, "ffn0_w2", a, "ffn_w2_0"), grp(0, 5))

    h = make_norm_mod(t, "norm_mix1")(z, small["norm_mix"][1][None], grp(1, 1), grp(1, 0))
    hx, hc = h[:t], h[t:]
    lam, wd, wr = _s5_params(small["ssm_a_re"][0], small["ssm_a_im"][0], small["ssm_log_dt"][0], small["ssm_b_re"][0],
                             small["ssm_b_im"][0], small["ssm_c_re"][0], small["ssm_c_im"][0])
    u2 = jnp.stack([_to_scan_order(jnp.concatenate([hc, hx], axis=0)), _to_scan_order(h)])
    y2 = make_s5_core("s5")(u2, lam, wd, wr)
    ys = _from_scan_order(y2[0])[m:] + _from_scan_order(y2[1])[:t]
    gl = make_gelu_in("gelu")(hx, ys, small["ssm_d_full"][None])
    zz = lin("col", F32, "glu_w", gl, "ssm_w_glu")
    yx = make_glu("glu")(zz, small["ssm_b_glu_full"][None])
    xs = make_gated_residual(t, "res_mix1")(z[:t], yx, grp(1, 2, 1))
    h = make_norm_mod(t, "norm_ffn1")(xs, small["norm_ffn"][1][None], grp(1, 4, 1), grp(1, 3, 1))
    a = make_swiglu_act("act1")(lin("col", BF16, "ffn1_w1", h, "ffn_w1_1"), lin("col", BF16, "ffn1_w3", h, "ffn_w3_1"))
    xs = make_gated_residual(t, "res_ffn1")(xs, lin("row", F32, "ffn1_w2", a, "ffn_w2_1"), grp(1, 5, 1))
    return make_final_loss("loss_head")(xs, small["norm_final"][None], target)[0, 0]


_WEIGHTS = ['c_ctx', 'ada_w', 'ada_b', 'norm_mix', 'norm_ffn', 'ffn_w1', 'ffn_w3', 'ffn_w2', 'attn_w_in', 'attn_w_out',
            'attn_rpb', 'attn_sink', 'ssm_a_re', 'ssm_a_im', 'ssm_log_dt', 'ssm_b_re', 'ssm_b_im', 'ssm_c_re', 'ssm_c_im',
            'ssm_d', 'ssm_w_glu', 'ssm_b_glu', 'norm_final']
_LOCAL_SMALL = ['norm_mix', 'norm_ffn', 'attn_rpb', 'attn_sink', 'ssm_a_re', 'ssm_a_im', 'ssm_log_dt', 'ssm_b_re',
                'ssm_b_im', 'ssm_c_re', 'ssm_c_im', 'norm_final']
_MOD_ROWS = 16


def _gather_chip_vector(v, name):
    g = all_gather8(v[None], name)
    return g[0::2, 0, :].reshape(-1)


def kernel(x, c, ctx, c_ctx, ada_w, ada_b, norm_mix, norm_ffn, ffn_w1, ffn_w3, ffn_w2, attn_w_in, attn_w_out, attn_rpb, attn_sink, ssm_a_re, ssm_a_im, ssm_log_dt, ssm_b_re, ssm_b_im, ssm_c_re, ssm_c_im, ssm_d, ssm_w_glu, ssm_b_glu, norm_final, loss_target, m_c_ctx, m_ada_w, m_ada_b, m_norm_mix, m_norm_ffn, m_ffn_w1, m_ffn_w3, m_ffn_w2, m_attn_w_in, m_attn_w_out, m_attn_rpb, m_attn_sink, m_ssm_a_re, m_ssm_a_im, m_ssm_log_dt, m_ssm_b_re, m_ssm_b_im, m_ssm_c_re, m_ssm_c_im, m_ssm_d, m_ssm_w_glu, m_ssm_b_glu, m_norm_final, v_c_ctx, v_ada_w, v_ada_b, v_norm_mix, v_norm_ffn, v_ffn_w1, v_ffn_w3, v_ffn_w2, v_attn_w_in, v_attn_w_out, v_attn_rpb, v_attn_sink, v_ssm_a_re, v_ssm_a_im, v_ssm_log_dt, v_ssm_b_re, v_ssm_b_im, v_ssm_c_re, v_ssm_c_im, v_ssm_d, v_ssm_w_glu, v_ssm_b_glu, v_norm_final):
    env = dict(locals())
    w = {n: env[n] for n in _WEIGHTS}
    mom = {n: env["m_" + n] for n in _WEIGHTS}
    var = {n: env["v_" + n] for n in _WEIGHTS}
    _, t, d = x.shape
    m = ctx.shape[1]
    px, py, pc = _my_pos()
    s_me = 2 * px + py
    a_w =attn_rpb.shape[1] * HEAD_DIM
    bq_w = attn_sink.shape[1] * HEAD_DIM
    bkv_w = (4 * attn_w_in.shape[2] - 3 * a_w - bq_w) // 2
    dims = dict(t=t, m=m, d=d, a_w=a_w, bq_w=bq_w, bkv_w=bkv_w)
    n_layers = ada_w.shape[0]
    ada_cols = ada_w.shape[2]

    big = {"attn_w_in": gather_weight(attn_w_in[0], "ag_attn_in"), "attn_w_out": gather_weight(attn_w_out[0], "ag_attn_out"),
           "ssm_w_glu": gather_weight(ssm_w_glu[0], "ag_glu")}
    for l in range(n_layers):
        big[f"ffn_w1_{l}"] = gather_weight(ffn_w1[l], f"ag_w1_{l}")
        big[f"ffn_w3_{l}"] = gather_weight(ffn_w3[l], f"ag_w3_{l}")
        big[f"ffn_w2_{l}"] = gather_weight(ffn_w2[l], f"ag_w2_{l}")
    small = {n: w[n] for n in _LOCAL_SMALL}
    small["ssm_d_full"] = _gather_chip_vector(ssm_d[0], "ag_ssm_d")
    small["ssm_b_glu_full"] = _gather_chip_vector(ssm_b_glu[0], "ag_b_glu")

    c_all = all_gather8(c, "ag_c")[:, 0, :]
    cond = jnp.concatenate([c_all, c_ctx[None], jnp.zeros((_MOD_ROWS - N_DEV - 1, d), F32)], axis=0)
    sig = jax.nn.sigmoid(cond)
    silu_c = (cond * sig).astype(BF16)
    ada_wb = ada_w.astype(BF16)
    mods_shard = jnp.stack([mm_nn(silu_c, ada_wb[l][None], "col", F32, f"ada_fwd{l}") for l in range(n_layers)])
    send = jnp.stack([jnp.stack([mods_shard[:, tgt], mods_shard[:, N_DEV]], axis=1).reshape(2 * n_layers, ada_cols)
                      for tgt in range(N_DEV)])
    plan = [((0, 0, 0), lambda me, peer: _lin(me), lambda me, peer: 2 * me[0] + me[1])]
    plan += [(f, lambda me, peer: _lin(peer), lambda me, peer: 2 * me[0] + me[1]) for f in _CHIP_FLIPS]
    got = xchg(send, 4, plan, "mods_xchg")
    mods = got.reshape(4, n_layers, 2, ada_cols).transpose(1, 2, 0, 3).reshape(n_layers, 2, 4 * ada_cols)
    mods = (mods + ada_b[:, None, :]).reshape(n_layers, 2, 6, d)

    shards = {"attn_w_in": attn_w_in[0], "attn_w_out": attn_w_out[0], "ssm_w_glu": ssm_w_glu[0]}
    for l in range(n_layers):
        shards.update({f"ffn_w1_{l}": ffn_w1[l], f"ffn_w3_{l}": ffn_w3[l], f"ffn_w2_{l}": ffn_w2[l]})

    def local(xx, mods_, small_, shards_):
        return _local_loss(xx, ctx[0], loss_target[0], mods_, small_, big, shards_, dims)

    loss_local, vjp = jax.vjp(local, x[0], mods, small, shards)
    g_x, g_mods, g_small, g_shards = vjp(jnp.ones((), F32))
    loss = lax.psum(loss_local, ("x", "y", "c"))
    grads = {"attn_w_in": g_shards["attn_w_in"][None], "attn_w_out": g_shards["attn_w_out"][None],
             "ssm_w_glu": g_shards["ssm_w_glu"][None]}
    for n in ("ffn_w1", "ffn_w3", "ffn_w2"):
        grads[n] = jnp.stack([g_shards[f"{n}_{l}"] for l in range(n_layers)])

    gm = all_gather8(g_mods.reshape(2 * n_layers, 6 * d), "ag_dmods").reshape(N_DEV, n_layers, 2, 6 * d)
    ctx_row = gm[0, :, 1]
    for j in range(1, N_DEV):
        ctx_row = ctx_row + gm[j, :, 1]
    dm16 = jnp.concatenate([gm[:, :, 0].transpose(1, 0, 2), ctx_row[:, None], jnp.zeros((n_layers, _MOD_ROWS - N_DEV - 1, 6 * d), F32)], axis=1)
    grads["ada_b"] = jnp.sum(dm16, axis=1)
    dm_mine = lax.dynamic_slice_in_dim(dm16, s_me * ada_cols, ada_cols, axis=2).astype(BF16)
    grads["ada_w"] = jnp.stack([mm_tn(silu_c.T, dm_mine[l], (1, d, ada_cols), "col", F32, f"ada_dw{l}")[0] for l in range(n_layers)])
    dsilu = mm_nt(dm_mine[0], ada_wb[0][None], "col", F32, "ada_dc0")
    for l in range(1, n_layers):
        dsilu = dsilu + mm_nt(dm_mine[l], ada_wb[l][None], "col", F32, f"ada_dc{l}")
    dsilu_ctx = 0.5 * dsilu[N_DEV]

    packed = [(n, g_small[n]) for n in _LOCAL_SMALL] + [("ssm_d", g_small["ssm_d_full"]), ("ssm_b_glu", g_small["ssm_b_glu_full"]),
                                                        ("c_ctx", dsilu_ctx)]
    flat = all_reduce8(jnp.concatenate([a.reshape(-1) for _, a in packed]), "ar_small")
    off = 0
    for n, a in packed:
        grads[n] = flat[off:off + a.size].reshape(a.shape)
        off += a.size
    sig_ctx = jax.nn.sigmoid(c_ctx)
    grads["c_ctx"] = grads["c_ctx"] * (sig_ctx * (1.0 + c_ctx * (1.0 - sig_ctx)))
    grads["ssm_d"] = lax.dynamic_slice_in_dim(grads["ssm_d"], s_me * ssm_d.shape[1], ssm_d.shape[1])[None]
    grads["ssm_b_glu"] = lax.dynamic_slice_in_dim(grads["ssm_b_glu"], s_me * ssm_b_glu.shape[1], ssm_b_glu.shape[1])[None]

    delta, new_m, new_v = {}, {}, {}
    for n in _WEIGHTS:
        delta[n], new_m[n], new_v[n] = adamw(w[n], grads[n], mom[n], var[n], "adamw_" + n)
    return (loss, g_x[None], *[grads[n] for n in _WEIGHTS], *[delta[n] for n in _WEIGHTS],
            *[new_m[n] for n in _WEIGHTS], *[new_v[n] for n in _WEIGHTS])
```

```python
import functools
import math

import numpy as np
import jax
import jax.numpy as jnp
from jax import lax
from jax.experimental import pallas as pl
from jax.experimental.pallas import tpu as pltpu

F32 = jnp.float32
BF16 = jnp.bfloat16
MESH = pl.DeviceIdType.MESH

HEAD_DIM = 128
GRID_W = 64
NA_ROWS = 8
NA_COLS = 16
SW_RADIUS = 128
ATTN_BLOCK = 128
ROPE_BASE = 10000.0
SSM_GROUP = 16
SSM_STATE = 64
SSM_TILE_GROUPS = 8
SCAN_BLOCKS = 8
EPS = 1e-6
NEG_INF = -1e30
ADAM_LR, ADAM_B1, ADAM_B2, ADAM_EPS, ADAM_WD, ADAM_STEP = 0.001, 0.9, 0.999, 1e-08, 0.01, 10
VMEM_LIMIT_BYTES = 56 * 1024 * 1024
N_DEV = 8


def _cparams(*sem):
    return pltpu.CompilerParams(dimension_semantics=tuple(sem) if sem else None, vmem_limit_bytes=VMEM_LIMIT_BYTES)


def _pick(n, cands):
    for c in cands:
        if n % c == 0:
            return c
    return n


def _dot_nn(a, b):
    return jnp.dot(a, b, preferred_element_type=F32)


def _dot_nt(a, b):
    return lax.dot_general(a, b, (((1,), (1,)), ((), ())), preferred_element_type=F32)


def _dot_tn(a, b):
    return lax.dot_general(a, b, (((0,), (0,)), ((), ())), preferred_element_type=F32)


def _mm_call(name, grid, a, b, a_spec, b_spec, o_spec, out_sds, acc_shape, step, carry=None):
    nk = grid[2]
    nb = 0 if carry is None else len(carry)

    def body(*refs):
        if carry is None:
            a_ref, b_ref, o_ref = refs[:3]
            acc_ref = refs[3] if nk > 1 else None
        else:
            a_ref, b_ref = refs[:2]
            o_ref = refs[2 + nb]
            bufs = refs[3 + nb:3 + 2 * nb]
            rest = refs[3 + 2 * nb:]
            acc_ref = rest[0] if nk > 1 else None
            sems = rest[-2 * nb:]
            ids = [pl.program_id(ax) for ax in range(3)]
            first = (ids[0] == 0) & (ids[1] == 0) & (ids[2] == 0)
            last = (ids[0] == grid[0] - 1) & (ids[1] == grid[1] - 1) & (ids[2] == grid[2] - 1)
            copies = []
            for q, (_, plan) in enumerate(carry):
                copies += _remote_copies(bufs[q], bufs[q], sems[2 * q], sems[2 * q + 1], plan)

            @pl.when(first)
            def _():
                for cp in copies:
                    cp.start()

        kk = pl.program_id(2)
        if nk == 1:
            o_ref[...] = step(a_ref, b_ref).astype(o_ref.dtype)
        else:
            @pl.when(kk == 0)
            def _():
                acc_ref[...] = step(a_ref, b_ref)

            @pl.when(kk > 0)
            def _():
                acc_ref[...] = step(a_ref, b_ref) + acc_ref[...]

            @pl.when(kk == nk - 1)
            def _():
                o_ref[...] = acc_ref[...].astype(o_ref.dtype)

        if carry is not None:
            @pl.when(last)
            def _():
                for cp in copies:
                    cp.wait_recv()
                for cp in copies:
                    cp.wait_send()

    scratch = [pltpu.VMEM(acc_shape, F32)] if nk > 1 else []
    if carry is None:
        return pl.pallas_call(
            body, grid=grid, in_specs=[a_spec, b_spec], out_specs=o_spec, out_shape=out_sds, scratch_shapes=scratch,
            name=name, compiler_params=_cparams("parallel", "parallel", "arbitrary"))(a, b)
    any_spec = pl.BlockSpec(memory_space=pl.ANY)
    for _, plan in carry:
        scratch += [pltpu.SemaphoreType.DMA((len(plan),)), pltpu.SemaphoreType.DMA((len(plan),))]
    return pl.pallas_call(
        body, grid=grid, in_specs=[a_spec, b_spec] + [any_spec] * nb, out_specs=[o_spec] + [any_spec] * nb,
        out_shape=[out_sds] + [jax.ShapeDtypeStruct(buf.shape, buf.dtype) for buf, _ in carry], scratch_shapes=scratch,
        input_output_aliases={2 + q: 1 + q for q in range(nb)}, name=name,
        compiler_params=_cparams("arbitrary", "arbitrary", "arbitrary"))(a, b, *[buf for buf, _ in carry])


_ROW_TILES = (768, 512, 256, 128, 64, 32, 16, 8)
_K_TILES = (2048, 1408, 1024, 512, 256, 128)
_CONTRACT_TILES = (1408, 1024, 768, 512, 256, 128)


def mm_nn(a, w3, kind, out_dtype, name, carry=None):
    r = a.shape[0]
    s, d1, d2 = w3.shape
    tm = _pick(r, _ROW_TILES)
    if kind == "col":
        grid = (s, r // tm, 1)
        a_spec = pl.BlockSpec((tm, d1), lambda j, i, k: (i, 0))
        b_spec = pl.BlockSpec((None, d1, d2), lambda j, i, k: (j, 0, 0))
        o_spec = pl.BlockSpec((tm, d2), lambda j, i, k: (i, j))
        n = s * d2

        def step(a_ref, b_ref):
            return _dot_nn(a_ref[...], b_ref[...])
    else:
        tn = d2 if d2 <= 1024 else _pick(d2, (1024, 512, 256, 128))
        grid = (d2 // tn, r // tm, 1)
        a_spec = pl.BlockSpec((tm, s * d1), lambda j, i, k: (i, 0))
        b_spec = pl.BlockSpec((s, d1, tn), lambda j, i, k: (0, 0, j))
        o_spec = pl.BlockSpec((tm, tn), lambda j, i, k: (i, j))
        n = d2

        def step(a_ref, b_ref):
            p = _dot_nn(a_ref[:, :d1], b_ref[0])
            for q in range(1, s):
                p = _dot_nn(a_ref[:, q * d1:(q + 1) * d1], b_ref[q]) + p
            return p
    return _mm_call(name, grid, a, w3, a_spec, b_spec, o_spec, jax.ShapeDtypeStruct((r, n), out_dtype), None, step, carry)


def mm_nt(dy, w3, kind, out_dtype, name, carry=None):
    r = dy.shape[0]
    s, d1, d2 = w3.shape
    tm = _pick(r, _ROW_TILES)
    if kind == "col":
        tko = d1 if d1 <= 1024 else _pick(d1, (1024, 512, 256, 128))
        grid = (d1 // tko, r // tm, 1)
        a_spec = pl.BlockSpec((tm, s * d2), lambda j, i, k: (i, 0))
        b_spec = pl.BlockSpec((s, tko, d2), lambda j, i, k: (0, j, 0))
        o_spec = pl.BlockSpec((tm, tko), lambda j, i, k: (i, j))
        kdim = d1

        def step(a_ref, b_ref):
            p = _dot_nt(a_ref[:, :d2], b_ref[0])
            for q in range(1, s):
                p = _dot_nt(a_ref[:, q * d2:(q + 1) * d2], b_ref[q]) + p
            return p
    else:
        grid = (s, r // tm, 1)
        a_spec = pl.BlockSpec((tm, d2), lambda j, i, k: (i, 0))
        b_spec = pl.BlockSpec((None, d1, d2), lambda j, i, k: (j, 0, 0))
        o_spec = pl.BlockSpec((tm, d1), lambda j, i, k: (i, j))
        kdim = s * d1

        def step(a_ref, b_ref):
            return _dot_nt(a_ref[...], b_ref[...])
    return _mm_call(name, grid, dy, w3, a_spec, b_spec, o_spec, jax.ShapeDtypeStruct((r, kdim), out_dtype), None, step, carry)


def mm_tn(a, dy, w_shape, kind, out_dtype, name):
    s, d1, d2 = w_shape
    r = dy.shape[0]
    tr = _pick(r, _CONTRACT_TILES)
    if kind == "col":
        tkk = d1 if d1 <= 1024 else _pick(d1, (1024, 512, 256, 128))
        grid = (s * (d1 // tkk), 1, r // tr)
        nkk = d1 // tkk
        a_spec = pl.BlockSpec((tr, tkk), lambda j, i, k: (k, j % nkk))
        b_spec = pl.BlockSpec((tr, d2), lambda j, i, k: (k, j // nkk))
        o_spec = pl.BlockSpec((None, tkk, d2), lambda j, i, k: (j // nkk, j % nkk, 0))
        acc = (tkk, d2)
    else:
        tn = d2 if d2 <= 1024 else _pick(d2, (1024, 512, 256, 128))
        nn = d2 // tn
        grid = (s * nn, 1, r // tr)
        a_spec = pl.BlockSpec((tr, d1), lambda j, i, k: (k, j // nn))
        b_spec = pl.BlockSpec((tr, tn), lambda j, i, k: (k, j % nn))
        o_spec = pl.BlockSpec((None, d1, tn), lambda j, i, k: (j // nn, 0, j % nn))
        acc = (d1, tn)

    def step(a_ref, b_ref):
        return _dot_tn(a_ref[...], b_ref[...])
    return _mm_call(name, grid, a, dy, a_spec, b_spec, o_spec, jax.ShapeDtypeStruct(w_shape, out_dtype), acc, step)


def make_linear(kind, out_dtype, name):
    def run(a, w3, gathering):
        carry = [(buf, _ag_plan()) for buf in gathering] or None
        out = mm_nn(a, w3, kind, out_dtype, name + "_fwd", carry)
        if not carry:
            return out, ()
        return out[0], tuple(_ag_finish(buf, f"{name}_gathered{q}") for q, buf in enumerate(out[1:]))

    @jax.custom_vjp
    def linear(a, w3, w_shard, gathering):
        return run(a, w3, gathering)

    def fwd(a, w3, w_shard, gathering):
        return run(a, w3, gathering), (a, w3, w_shard.shape, len(gathering))

    def bwd(res, cts):
        a, w3, shard_shape, n_gathering = res
        dyb = cts[0].astype(BF16)
        dw = mm_tn(a, dyb, w3.shape, kind, BF16, name + "_dw")
        buf, plan = _rs_begin(dw, name)
        da, buf = mm_nt(dyb, w3, kind, a.dtype, name + "_dx", carry=[(buf, plan)])
        return da, None, _rs_finish(buf, name).reshape(shard_shape), (None,) * n_gathering

    linear.defvjp(fwd, bwd)
    return linear


def _row_tile(r, t0, d):
    cap = max(8, (2 * 1024 * 1024) // (4 * d))
    cands = [t for t in (1024, 512, 256, 128, 64, 32, 16, 8) if t <= cap]
    for t in cands:
        if r % t == 0 and t0 % t == 0:
            return t
    raise ValueError("no row tile")


def _grp_spec(d, nb0):
    return pl.BlockSpec((None, 1, d), lambda i: (i // nb0, 0, 0))


def _norm_mod_fwd(z, g, scale, shift, t0, name):
    r, d = z.shape
    tr = _row_tile(r, t0, d)
    nb0 = t0 // tr

    def body(z_ref, g_ref, sc_ref, sh_ref, o_ref):
        zz = z_ref[...]
        rstd = lax.rsqrt(jnp.mean(zz * zz, axis=-1, keepdims=True) + EPS)
        y = zz * rstd * g_ref[...]
        o_ref[...] = (y * (1.0 + sc_ref[...]) + sh_ref[...]).astype(o_ref.dtype)

    return pl.pallas_call(
        body, grid=(r // tr,),
        in_specs=[pl.BlockSpec((tr, d), lambda i: (i, 0)), pl.BlockSpec((1, d), lambda i: (0, 0)),
                  _grp_spec(d, nb0), _grp_spec(d, nb0)],
        out_specs=pl.BlockSpec((tr, d), lambda i: (i, 0)),
        out_shape=jax.ShapeDtypeStruct((r, d), BF16), name=name, compiler_params=_cparams("parallel"))(z, g, scale, shift)


def _norm_mod_bwd(z, g, scale, dh, t0, name):
    r, d = z.shape
    ng = scale.shape[0]
    tr = _row_tile(r, t0, d)
    nb0 = t0 // tr

    def body(z_ref, g_ref, sc_ref, dh_ref, dz_ref, dg_ref, dsc_ref, dsh_ref):
        i = pl.program_id(0)
        zz = z_ref[...]
        gg = g_ref[...]
        rstd = lax.rsqrt(jnp.mean(zz * zz, axis=-1, keepdims=True) + EPS)
        zhat = zz * rstd
        dhh = dh_ref[...].astype(F32)
        dy = dhh * (1.0 + sc_ref[...])
        dyg = dy * gg
        dz_ref[...] = rstd * (dyg - zhat * jnp.mean(dyg * zhat, axis=-1, keepdims=True))

        @pl.when(i == 0)
        def _():
            dg_ref[...] = jnp.zeros_like(dg_ref)

        @pl.when((i == 0) | (i == nb0))
        def _():
            dsc_ref[...] = jnp.zeros_like(dsc_ref)
            dsh_ref[...] = jnp.zeros_like(dsh_ref)

        dg_ref[...] += jnp.sum(dy * zhat, axis=0, keepdims=True)
        dsc_ref[...] += jnp.sum(dhh * (zhat * gg), axis=0, keepdims=True)
        dsh_ref[...] += jnp.sum(dhh, axis=0, keepdims=True)

    return pl.pallas_call(
        body, grid=(r // tr,),
        in_specs=[pl.BlockSpec((tr, d), lambda i: (i, 0)), pl.BlockSpec((1, d), lambda i: (0, 0)),
                  _grp_spec(d, nb0), pl.BlockSpec((tr, d), lambda i: (i, 0))],
        out_specs=[pl.BlockSpec((tr, d), lambda i: (i, 0)), pl.BlockSpec((1, d), lambda i: (0, 0)),
                   _grp_spec(d, nb0), _grp_spec(d, nb0)],
        out_shape=[jax.ShapeDtypeStruct((r, d), F32), jax.ShapeDtypeStruct((1, d), F32),
                   jax.ShapeDtypeStruct((ng, 1, d), F32), jax.ShapeDtypeStruct((ng, 1, d), F32)],
        name=name, compiler_params=_cparams("arbitrary"))(z, g, scale, dh)


def make_norm_mod(t0, name):
    @jax.custom_vjp
    def f(z, g, scale, shift):
        return _norm_mod_fwd(z, g, scale, shift, t0, name + "_fwd")

    def fwd(z, g, scale, shift):
        return _norm_mod_fwd(z, g, scale, shift, t0, name + "_fwd"), (z, g, scale)

    def bwd(res, dh):
        z, g, scale = res
        dz, dg, dsc, dsh = _norm_mod_bwd(z, g, scale, dh, t0, name + "_bwd")
        return dz, dg, dsc, dsh

    f.defvjp(fwd, bwd)
    return f


def _gated_fwd(z, y, gate, t0, name):
    r, d = z.shape
    tr = _row_tile(r, t0, d)
    nb0 = t0 // tr

    def body(z_ref, y_ref, g_ref, o_ref):
        o_ref[...] = z_ref[...] + g_ref[...] * y_ref[...].astype(F32)

    return pl.pallas_call(
        body, grid=(r // tr,),
        in_specs=[pl.BlockSpec((tr, d), lambda i: (i, 0)), pl.BlockSpec((tr, d), lambda i: (i, 0)), _grp_spec(d, nb0)],
        out_specs=pl.BlockSpec((tr, d), lambda i: (i, 0)),
        out_shape=jax.ShapeDtypeStruct((r, d), F32), name=name, compiler_params=_cparams("parallel"))(z, y, gate)


def _gated_bwd(y, gate, dzn, t0, name):
    r, d = y.shape
    ng = gate.shape[0]
    tr = _row_tile(r, t0, d)
    nb0 = t0 // tr

    def body(y_ref, g_ref, dz_ref, dy_ref, dg_ref):
        i = pl.program_id(0)
        dzz = dz_ref[...]
        dy_ref[...] = (g_ref[...] * dzz).astype(dy_ref.dtype)

        @pl.when((i == 0) | (i == nb0))
        def _():
            dg_ref[...] = jnp.zeros_like(dg_ref)

        dg_ref[...] += jnp.sum(dzz * y_ref[...].astype(F32), axis=0, keepdims=True)

    return pl.pallas_call(
        body, grid=(r // tr,),
        in_specs=[pl.BlockSpec((tr, d), lambda i: (i, 0)), _grp_spec(d, nb0), pl.BlockSpec((tr, d), lambda i: (i, 0))],
        out_specs=[pl.BlockSpec((tr, d), lambda i: (i, 0)), _grp_spec(d, nb0)],
        out_shape=[jax.ShapeDtypeStruct((r, d), y.dtype), jax.ShapeDtypeStruct((ng, 1, d), F32)],
        name=name, compiler_params=_cparams("arbitrary"))(y, gate, dzn)


def make_gated_residual(t0, name):
    @jax.custom_vjp
    def f(z, y, gate):
        return _gated_fwd(z, y, gate, t0, name + "_fwd")

    def fwd(z, y, gate):
        return _gated_fwd(z, y, gate, t0, name + "_fwd"), (y, gate)

    def bwd(res, dzn):
        y, gate = res
        dy, dgate = _gated_bwd(y, gate, dzn, t0, name + "_bwd")
        return dzn, dy, dgate

    f.defvjp(fwd, bwd)
    return f


def _ew_call(name, body, ins, outs_sds, r, widths_in, widths_out, tr, extra_in=(), extra_specs=(), sem="parallel"):
    in_specs = [pl.BlockSpec((tr, w), lambda i: (i, 0)) for w in widths_in] + list(extra_specs)
    out_specs = [pl.BlockSpec((tr, w), lambda i: (i, 0)) if w is not None else pl.BlockSpec(s.shape, lambda i: (0,) * len(s.shape))
                 for w, s in zip(widths_out, outs_sds)]
    return pl.pallas_call(body, grid=(r // tr,), in_specs=in_specs, out_specs=out_specs, out_shape=outs_sds,
                          name=name, compiler_params=_cparams(sem))(*ins, *extra_in)


def _silu(x):
    return x * jax.nn.sigmoid(x)


def make_swiglu_act(name):
    def fwd_call(h1, h3):
        r, f = h1.shape
        tr = _row_tile(r, r, f)

        def body(a_ref, b_ref, o_ref):
            o_ref[...] = (_silu(a_ref[...].astype(F32)) * b_ref[...].astype(F32)).astype(o_ref.dtype)

        return _ew_call(name + "_fwd", body, (h1, h3), [jax.ShapeDtypeStruct((r, f), BF16)], r, (f, f), (f,), tr)[0]

    @jax.custom_vjp
    def act(h1, h3):
        return fwd_call(h1, h3)

    def fwd(h1, h3):
        return fwd_call(h1, h3), (h1, h3)

    def bwd(res, da):
        h1, h3 = res
        r, f = h1.shape
        tr = _row_tile(r, r, f)

        def body(a_ref, b_ref, da_ref, d1_ref, d3_ref):
            a = a_ref[...].astype(F32)
            b = b_ref[...].astype(F32)
            g = da_ref[...].astype(F32)
            sg = jax.nn.sigmoid(a)
            d1_ref[...] = (g * b * (sg * (1.0 + a * (1.0 - sg)))).astype(d1_ref.dtype)
            d3_ref[...] = (g * a * sg).astype(d3_ref.dtype)

        sds = jax.ShapeDtypeStruct((r, f), BF16)
        return tuple(_ew_call(name + "_bwd", body, (h1, h3, da), [sds, sds], r, (f, f, f), (f, f), tr))

    act.defvjp(fwd, bwd)
    return act


_GELU_C = math.sqrt(2.0 / math.pi)


def _gelu_and_grad(y):
    inner = _GELU_C * (y + 0.044715 * y * y * y)
    t = jnp.tanh(inner)
    val = 0.5 * y * (1.0 + t)
    grad = 0.5 * (1.0 + t) + 0.5 * y * (1.0 - t * t) * _GELU_C * (1.0 + 3 * 0.044715 * y * y)
    return val, grad


def make_gelu_in(name):
    def fwd_call(u, ys, dsk):
        r, d = u.shape
        tr = _row_tile(r, r, d)

        def body(u_ref, y_ref, d_ref, o_ref):
            y = d_ref[...] * u_ref[...].astype(F32) + y_ref[...]
            o_ref[...] = _gelu_and_grad(y)[0].astype(o_ref.dtype)

        return _ew_call(name + "_fwd", body, (u, ys), [jax.ShapeDtypeStruct((r, d), BF16)], r, (d, d), (d,), tr,
                        extra_in=(dsk,), extra_specs=(pl.BlockSpec((1, d), lambda i: (0, 0)),))[0]

    @jax.custom_vjp
    def f(u, ys, dsk):
        return fwd_call(u, ys, dsk)

    def fwd(u, ys, dsk):
        return fwd_call(u, ys, dsk), (u, ys, dsk)

    def bwd(res, dg):
        u, ys, dsk = res
        r, d = u.shape
        tr = _row_tile(r, r, d)

        def body(u_ref, y_ref, dg_ref, d_ref, du_ref, dy_ref, dd_ref):
            i = pl.program_id(0)
            uu = u_ref[...].astype(F32)
            y = d_ref[...] * uu + y_ref[...]
            dy = dg_ref[...].astype(F32) * _gelu_and_grad(y)[1]
            dy_ref[...] = dy
            du_ref[...] = (d_ref[...] * dy).astype(du_ref.dtype)

            @pl.when(i == 0)
            def _():
                dd_ref[...] = jnp.zeros_like(dd_ref)

            dd_ref[...] += jnp.sum(dy * uu, axis=0, keepdims=True)

        outs = [jax.ShapeDtypeStruct((r, d), u.dtype), jax.ShapeDtypeStruct((r, d), F32), jax.ShapeDtypeStruct((1, d), F32)]
        du, dy, dd = _ew_call(name + "_bwd", body, (u, ys, dg), outs, r, (d, d, d), (d, d, None), tr,
                              extra_in=(dsk,), extra_specs=(pl.BlockSpec((1, d), lambda i: (0, 0)),), sem="arbitrary")
        return du, dy, dd

    f.defvjp(fwd, bwd)
    return f


def make_glu(name):
    def fwd_call(z, b):
        r, d2 = z.shape
        d = d2 // 2
        tr = _row_tile(r, r, d2)

        def body(z_ref, b_ref, o_ref):
            zz = z_ref[...].astype(F32) + b_ref[...]
            o_ref[...] = zz[:, :d] * jax.nn.sigmoid(zz[:, d:])

        return _ew_call(name + "_fwd", body, (z,), [jax.ShapeDtypeStruct((r, d), F32)], r, (d2,), (d,), tr,
                        extra_in=(b,), extra_specs=(pl.BlockSpec((1, d2), lambda i: (0, 0)),))[0]

    @jax.custom_vjp
    def f(z, b):
        return fwd_call(z, b)

    def fwd(z, b):
        return fwd_call(z, b), (z, b)

    def bwd(res, do):
        z, b = res
        r, d2 = z.shape
        d = d2 // 2
        tr = _row_tile(r, r, d2)

        def body(z_ref, do_ref, b_ref, dz_ref, db_ref):
            i = pl.program_id(0)
            zz = z_ref[...].astype(F32) + b_ref[...]
            sg = jax.nn.sigmoid(zz[:, d:])
            g = do_ref[...]
            dza = g * sg
            dzb = g * zz[:, :d] * sg * (1.0 - sg)
            dz_ref[:, :d] = dza.astype(dz_ref.dtype)
            dz_ref[:, d:] = dzb.astype(dz_ref.dtype)

            @pl.when(i == 0)
            def _():
                db_ref[...] = jnp.zeros_like(db_ref)

            db_ref[:, :d] += jnp.sum(dza, axis=0, keepdims=True)
            db_ref[:, d:] += jnp.sum(dzb, axis=0, keepdims=True)

        outs = [jax.ShapeDtypeStruct((r, d2), z.dtype), jax.ShapeDtypeStruct((1, d2), F32)]
        dz, db = _ew_call(name + "_bwd", body, (z, do), outs, r, (d2, d), (d2, None), tr,
                          extra_in=(b,), extra_specs=(pl.BlockSpec((1, d2), lambda i: (0, 0)),), sem="arbitrary")
        return dz, db

    f.defvjp(fwd, bwd)
    return f


def make_final_loss(name):
    def call(z, g, target):
        r, d = z.shape
        tr = _row_tile(r, r, d)

        def body(z_ref, t_ref, g_ref, dz_ref, dg_ref, l_ref):
            i = pl.program_id(0)
            zz = z_ref[...]
            gg = g_ref[...]
            rstd = lax.rsqrt(jnp.mean(zz * zz, axis=-1, keepdims=True) + EPS)
            zhat = zz * rstd
            e = zhat * gg - t_ref[...]
            dy = e * (1.0 / d)
            dyg = dy * gg
            dz_ref[...] = rstd * (dyg - zhat * jnp.mean(dyg * zhat, axis=-1, keepdims=True))

            @pl.when(i == 0)
            def _():
                dg_ref[...] = jnp.zeros_like(dg_ref)
                l_ref[...] = jnp.zeros_like(l_ref)

            dg_ref[...] += jnp.sum(dy * zhat, axis=0, keepdims=True)
            l_ref[...] += jnp.sum(jnp.sum(e * e, axis=1, keepdims=True), axis=0, keepdims=True) * (0.5 / d)

        outs = [jax.ShapeDtypeStruct((r, d), F32), jax.ShapeDtypeStruct((1, d), F32), jax.ShapeDtypeStruct((1, 1), F32)]
        return _ew_call(name, body, (z, target), outs, r, (d, d), (d, None, None), tr,
                        extra_in=(g,), extra_specs=(pl.BlockSpec((1, d), lambda i: (0, 0)),), sem="arbitrary")

    @jax.custom_vjp
    def f(z, g, target):
        return call(z, g, target)[2]

    def fwd(z, g, target):
        dz, dg, loss = call(z, g, target)
        return loss, (dz, dg)

    def bwd(res, dl):
        dz, dg = res
        s = dl[0, 0]
        return dz * s, dg * s, None

    f.defvjp(fwd, bwd)
    return f


def _rope_tables(t):
    quarter = HEAD_DIM // 4
    inv_freq = ROPE_BASE ** (-np.arange(quarter, dtype=np.float64) / quarter)
    pos = np.arange(t)
    ang_r = (pos // GRID_W)[:, None] * inv_freq[None, :]
    ang_c = (pos % GRID_W)[:, None] * inv_freq[None, :]
    cos = np.concatenate([np.cos(ang_r), np.cos(ang_r), np.cos(ang_c), np.cos(ang_c)], axis=1)
    sin = np.concatenate([-np.sin(ang_r), np.sin(ang_r), -np.sin(ang_c), np.sin(ang_c)], axis=1)
    return jnp.asarray(cos, F32), jnp.asarray(sin, F32)


def _rope_call(x, cos, sin, name):
    t, w = x.shape
    tr = _pick(t, (512, 256, 128, 64))
    quarter = HEAD_DIM // 4

    def body(x_ref, c_ref, s_ref, o_ref):
        xx = x_ref[...].astype(F32)
        lane = lax.broadcasted_iota(jnp.int32, xx.shape, 1)
        first = (lane % (2 * quarter)) < quarter
        partner = jnp.where(first, pltpu.roll(xx, HEAD_DIM - quarter, 1), pltpu.roll(xx, quarter, 1))
        o_ref[...] = (xx * c_ref[...] + partner * s_ref[...]).astype(o_ref.dtype)

    return pl.pallas_call(
        body, grid=(t // tr, w // HEAD_DIM),
        in_specs=[pl.BlockSpec((tr, HEAD_DIM), lambda i, j: (i, j)), pl.BlockSpec((tr, HEAD_DIM), lambda i, j: (i, 0)),
                  pl.BlockSpec((tr, HEAD_DIM), lambda i, j: (i, 0))],
        out_specs=pl.BlockSpec((tr, HEAD_DIM), lambda i, j: (i, j)),
        out_shape=jax.ShapeDtypeStruct((t, w), x.dtype), name=name, compiler_params=_cparams("parallel", "parallel"))(x, cos, sin)


def make_rope(t, name):
    cos, sin = _rope_tables(t)

    @jax.custom_vjp
    def f(x):
        return _rope_call(x, cos, sin, name + "_fwd")

    def fwd(x):
        return _rope_call(x, cos, sin, name + "_fwd"), None

    def bwd(_, dy):
        return (_rope_call(dy, cos, -sin, name + "_bwd"),)

    f.defvjp(fwd, bwd)
    return f


def _attn_specs(g, span, tk, m, nbh, has_ctx):
    hd = HEAD_DIM
    q_spec = pl.BlockSpec((ATTN_BLOCK, g * hd), lambda h, i, meta: (i, h))
    kv_spec = pl.BlockSpec((tk, hd), lambda h, i, meta: (0, h))
    c_spec = pl.BlockSpec((m, hd), lambda h, i, meta: (0, h))
    if nbh > 1:
        b_spec = pl.BlockSpec((None, None, ATTN_BLOCK, span), lambda h, i, meta: (meta[1, i], h, 0, 0))
    else:
        b_spec = pl.BlockSpec((None, None, ATTN_BLOCK, span), lambda h, i, meta: (meta[1, i], 0, 0, 0))
    sink_spec = pl.BlockSpec(memory_space=pltpu.SMEM)
    return q_spec, kv_spec, c_spec, b_spec, sink_spec


def _attn_probs(qh, ks, kc, bias, sink_val, scale, has_ctx, has_sink):
    s = _dot_nt(qh, ks) * scale + bias
    mx = jnp.max(s, axis=-1, keepdims=True)
    sc = None
    if has_ctx:
        sc = _dot_nt(qh, kc) * scale
        mx = jnp.maximum(mx, jnp.max(sc, axis=-1, keepdims=True))
    if has_sink:
        mx = jnp.maximum(mx, sink_val)
    p = jnp.exp(s - mx)
    l = jnp.sum(p, axis=-1, keepdims=True)
    pc = None
    if has_ctx:
        pc = jnp.exp(sc - mx)
        l = l + jnp.sum(pc, axis=-1, keepdims=True)
    ps = None
    if has_sink:
        ps = jnp.exp(sink_val - mx)
        l = l + ps
    return p, pc, ps, l


def _attn_fwd(q, k, v, kc, vc, bias, sink, meta, g, span, has_ctx, has_sink, name):
    rq, wq = q.shape
    tk, wk = k.shape
    hkv = wk // HEAD_DIM
    m = kc.shape[0]
    nbh = bias.shape[1]
    scale = HEAD_DIM ** -0.5
    nqb = rq // ATTN_BLOCK
    q_spec, kv_spec, c_spec, b_spec, sink_spec = _attn_specs(g, span, tk, m, nbh, has_ctx)

    def body(meta_ref, sink_ref, q_ref, k_ref, v_ref, kc_ref, vc_ref, b_ref, o_ref):
        h = pl.program_id(0)
        i = pl.program_id(1)
        ks0 = pl.multiple_of(meta_ref[0, i], 64)
        ks = k_ref[pl.ds(ks0, span), :]
        vs = v_ref[pl.ds(ks0, span), :]
        bias_t = b_ref[...]
        for hh in range(g):
            qh = q_ref[:, hh * HEAD_DIM:(hh + 1) * HEAD_DIM]
            sink_val = sink_ref[h * g + hh] if has_sink else None
            p, pc, _, l = _attn_probs(qh, ks, kc_ref[...], bias_t, sink_val, scale, has_ctx, has_sink)
            acc = jnp.dot(p.astype(BF16), vs, preferred_element_type=F32)
            if has_ctx:
                acc = acc + jnp.dot(pc.astype(BF16), vc_ref[...], preferred_element_type=F32)
            o_ref[:, hh * HEAD_DIM:(hh + 1) * HEAD_DIM] = (acc / l).astype(o_ref.dtype)

    gs = pltpu.PrefetchScalarGridSpec(
        num_scalar_prefetch=1, grid=(hkv, nqb),
        in_specs=[sink_spec, q_spec, kv_spec, kv_spec, c_spec, c_spec, b_spec], out_specs=q_spec)
    return pl.pallas_call(body, grid_spec=gs, out_shape=jax.ShapeDtypeStruct((rq, wq), BF16), name=name,
                          compiler_params=_cparams("parallel", "arbitrary"))(meta, sink, q, k, v, kc, vc, bias)


def _attn_bwd(q, k, v, kc, vc, bias, sink, meta, o, do, g, span, has_ctx, has_sink, want_dbias, name):
    rq, wq = q.shape
    tk, wk = k.shape
    hkv = wk // HEAD_DIM
    m = kc.shape[0]
    ncase, nbh = bias.shape[:2]
    scale = HEAD_DIM ** -0.5
    nqb = rq // ATTN_BLOCK
    q_spec, kv_spec, c_spec, b_spec, sink_spec = _attn_specs(g, span, tk, m, nbh, has_ctx)
    dsink_spec = pl.BlockSpec((None, 8, HEAD_DIM), lambda h, i, meta: (h, 0, 0))

    def body(meta_ref, sink_ref, q_ref, k_ref, v_ref, kc_ref, vc_ref, b_ref, o_ref, do_ref,
             dq_ref, dk_ref, dv_ref, dkc_ref, dvc_ref, db_ref, dsk_ref):
        h = pl.program_id(0)
        i = pl.program_id(1)

        @pl.when(i == 0)
        def _():
            dk_ref[...] = jnp.zeros_like(dk_ref)
            dv_ref[...] = jnp.zeros_like(dv_ref)
            dkc_ref[...] = jnp.zeros_like(dkc_ref)
            dvc_ref[...] = jnp.zeros_like(dvc_ref)
            dsk_ref[...] = jnp.zeros_like(dsk_ref)

        if want_dbias:
            @pl.when(meta_ref[2, i] == 1)
            def _():
                db_ref[...] = jnp.zeros_like(db_ref)
        else:
            @pl.when(i == 0)
            def _():
                db_ref[...] = jnp.zeros_like(db_ref)

        ks0 = pl.multiple_of(meta_ref[0, i], 64)
        ks = k_ref[pl.ds(ks0, span), :]
        vs = v_ref[pl.ds(ks0, span), :]
        bias_t = b_ref[...]
        dk_acc = jnp.zeros((span, HEAD_DIM), F32)
        dv_acc = jnp.zeros((span, HEAD_DIM), F32)
        for hh in range(g):
            cols = slice(hh * HEAD_DIM, (hh + 1) * HEAD_DIM)
            qh = q_ref[:, cols]
            doh = do_ref[:, cols]
            sink_val = sink_ref[h * g + hh] if has_sink else None
            p, pc, ps, l = _attn_probs(qh, ks, kc_ref[...], bias_t, sink_val, scale, has_ctx, has_sink)
            inv_l = 1.0 / l
            delta = jnp.sum(doh.astype(F32) * o_ref[:, cols].astype(F32), axis=-1, keepdims=True)
            pn = p * inv_l
            ds = pn * (_dot_nt(doh, vs) - delta)
            dsb = ds.astype(BF16)
            dq = jnp.dot(dsb, ks, preferred_element_type=F32)
            dk_acc = dk_acc + _dot_tn(dsb, qh)
            dv_acc = dv_acc + _dot_tn(pn.astype(BF16), doh)
            if want_dbias:
                db_ref[...] += ds
            if has_ctx:
                pcn = pc * inv_l
                dsc = (pcn * (_dot_nt(doh, vc_ref[...]) - delta)).astype(BF16)
                dq = dq + jnp.dot(dsc, kc_ref[...], preferred_element_type=F32)
                dkc_ref[...] += _dot_tn(dsc, qh) * scale
                dvc_ref[...] += _dot_tn(pcn.astype(BF16), doh)
            if has_sink:
                dsv = -jnp.sum(ps * inv_l * delta, axis=0, keepdims=True)
                dsk_ref[hh:hh + 1, :] += jnp.broadcast_to(dsv, (1, HEAD_DIM))
            dq_ref[:, cols] = (dq * scale).astype(dq_ref.dtype)
        dk_ref[pl.ds(ks0, span), :] += dk_acc * scale
        dv_ref[pl.ds(ks0, span), :] += dv_acc

    gs = pltpu.PrefetchScalarGridSpec(
        num_scalar_prefetch=1, grid=(hkv, nqb),
        in_specs=[sink_spec, q_spec, kv_spec, kv_spec, c_spec, c_spec, b_spec, q_spec, q_spec],
        out_specs=[q_spec, kv_spec, kv_spec, c_spec, c_spec, b_spec if want_dbias else dsink_spec, dsink_spec])
    db_sds = jax.ShapeDtypeStruct((ncase, nbh, ATTN_BLOCK, span) if want_dbias else (hkv, 8, HEAD_DIM), F32)
    out_shape = [jax.ShapeDtypeStruct((rq, wq), BF16), jax.ShapeDtypeStruct((tk, wk), F32), jax.ShapeDtypeStruct((tk, wk), F32),
                 jax.ShapeDtypeStruct((m, wk), F32), jax.ShapeDtypeStruct((m, wk), F32), db_sds,
                 jax.ShapeDtypeStruct((hkv, 8, HEAD_DIM), F32)]
    return pl.pallas_call(body, grid_spec=gs, out_shape=out_shape, name=name,
                          compiler_params=_cparams("parallel", "arbitrary"))(meta, sink, q, k, v, kc, vc, bias, o, do)


def make_attention(meta_np, g, span, has_ctx, has_sink, want_dbias, name):
    meta = jnp.asarray(meta_np, jnp.int32)

    @jax.custom_vjp
    def f(q, k, v, kc, vc, bias, sink):
        return _attn_fwd(q, k, v, kc, vc, bias, sink, meta, g, span, has_ctx, has_sink, name + "_fwd")

    def fwd(q, k, v, kc, vc, bias, sink):
        o = _attn_fwd(q, k, v, kc, vc, bias, sink, meta, g, span, has_ctx, has_sink, name + "_fwd")
        return o, (q, k, v, kc, vc, bias, sink, o)

    def bwd(res, do):
        q, k, v, kc, vc, bias, sink, o = res
        dq, dk, dv, dkc, dvc, db, dsk = _attn_bwd(q, k, v, kc, vc, bias, sink, meta, o, do.astype(BF16), g, span,
                                                   has_ctx, has_sink, want_dbias, name + "_bwd")
        dsink = dsk[:, :g, 0].reshape(sink.shape)
        if not want_dbias:
            db = jnp.zeros_like(bias)
        return dq, dk.astype(k.dtype), dv.astype(v.dtype), dkc.astype(kc.dtype), dvc.astype(vc.dtype), db, dsink

    f.defvjp(fwd, bwd)
    return f


def _dedupe_cases(tables):
    cases, idx, first = [], [], []
    for tbl in tables:
        if cases and np.array_equal(cases[-1], tbl):
            idx.append(len(cases) - 1)
            first.append(0)
        else:
            cases.append(tbl)
            idx.append(len(cases) - 1)
            first.append(1)
    return cases, idx, first


def _na_plan(t):
    rows = t // GRID_W
    qr = ATTN_BLOCK // GRID_W
    kr = qr + NA_ROWS - 1
    assert rows >= kr and rows % qr == 0
    span = kr * GRID_W
    kstart, tables = [], []
    qcol = np.tile(np.arange(GRID_W), qr)
    kcol = np.tile(np.arange(GRID_W), kr)
    win_c = np.clip(qcol - NA_COLS // 2, 0, GRID_W - NA_COLS)
    col_ok = (kcol[None, :] >= win_c[:, None]) & (kcol[None, :] < win_c[:, None] + NA_COLS)
    dcol = np.clip(kcol[None, :] - qcol[:, None] + NA_COLS - 1, 0, 2 * NA_COLS - 2)
    for r0 in range(0, rows, qr):
        kb = int(np.clip(r0 - NA_ROWS // 2, 0, rows - kr))
        qrow = r0 + np.repeat(np.arange(qr), GRID_W)
        krow = kb + np.repeat(np.arange(kr), GRID_W)
        win_r = np.clip(qrow - NA_ROWS // 2, 0, rows - NA_ROWS)
        row_ok = (krow[None, :] >= win_r[:, None]) & (krow[None, :] < win_r[:, None] + NA_ROWS)
        drow = np.clip(krow[None, :] - qrow[:, None] + NA_ROWS - 1, 0, 2 * NA_ROWS - 2)
        tables.append(np.stack([row_ok & col_ok, drow, dcol]).astype(np.int32))
        kstart.append(kb * GRID_W)
    cases, idx, first = _dedupe_cases(tables)
    meta = np.array([kstart, idx, first], np.int32)
    return meta, span, np.stack(cases)


def _na_bias(rpb, cases):
    valid, drow, dcol = cases[:, 0], cases[:, 1], cases[:, 2]
    ncase, qn, span = valid.shape
    qr, kr = qn // GRID_W, span // GRID_W
    drow_s = drow.reshape(ncase, qr, GRID_W, kr, GRID_W)[:, :, 0, :, 0]
    dcol_s = dcol[0].reshape(qr, GRID_W, kr, GRID_W)[0, :, 0, :]
    oh_r = jnp.asarray(np.eye(2 * NA_ROWS - 1, dtype=np.float32)[drow_s])
    oh_c = jnp.asarray(np.eye(2 * NA_COLS - 1, dtype=np.float32)[dcol_s])
    tmp = jnp.einsum("hrc,xyc->hrxy", rpb, oh_c, precision=lax.Precision.HIGHEST)
    b = jnp.einsum("nakr,hrxy->nhaxky", oh_r, tmp, precision=lax.Precision.HIGHEST).reshape(ncase, -1, qn, span)
    return jnp.where(jnp.asarray(valid[:, None] > 0), b, NEG_INF)


def _sw_plan(t):
    span = 3 * ATTN_BLOCK
    assert t >= span
    kstart, tables = [], []
    for b in range(t // ATTN_BLOCK):
        ks = int(np.clip((b - 1) * ATTN_BLOCK, 0, t - span))
        qpos = b * ATTN_BLOCK + np.arange(ATTN_BLOCK)
        kpos = ks + np.arange(span)
        ok = np.abs(kpos[None, :] - qpos[:, None]) <= SW_RADIUS
        tables.append(np.where(ok, 0.0, NEG_INF).astype(np.float32))
        kstart.append(ks)
    cases, idx, first = _dedupe_cases(tables)
    return np.array([kstart, idx, first], np.int32), span, np.stack(cases)[:, None]


def _cmul(ar, ai, br, bi):
    return ar * br - ai * bi, ar * bi + ai * br


def _s5_scan_call(x2, win, lam, cin, wout, reverse, n_chunks, name):
    _, ll, d = x2.shape
    nt = d // HEAD_DIM
    sw = 2 * SSM_TILE_GROUPS * SSM_STATE
    hw = sw // 2
    rows = ll // n_chunks
    ic = rows // SCAN_BLOCKS
    full = cin is not None

    down_dir = 0 if reverse else 1

    def chunk_idx(k, dd):
        return jnp.where(dd == down_dir, n_chunks - 1 - k, k)

    def body(*refs):
        if full:
            x_ref, win_ref, lam_ref, cin_ref, wout_ref, s_out, y_out, ub_ref, st_ref = refs
        else:
            x_ref, win_ref, lam_ref, f_out, ub_ref, st_ref = refs
        k = pl.program_id(2)
        down = pl.program_id(0) == down_dir

        @pl.when(k == 0)
        def _():
            st_ref[...] = cin_ref[...] if full else jnp.zeros_like(st_ref)

        ub_ref[...] = jnp.dot(x_ref[...].astype(BF16), win_ref[...], preferred_element_type=F32)
        lr = lam_ref[:, :hw]
        li = lam_ref[:, hw:]

        def step(ii, carry):
            sr, si = carry
            i = jnp.where(down, ic - 1 - ii, ii)
            r0 = pl.multiple_of(i * SCAN_BLOCKS, SCAN_BLOCKS)
            ur = ub_ref[pl.ds(r0, SCAN_BLOCKS), :hw]
            ui = ub_ref[pl.ds(r0, SCAN_BLOCKS), hw:]
            nr = lr * sr - li * si + ur
            ni = lr * si + li * sr + ui
            if full:
                ub_ref[pl.ds(r0, SCAN_BLOCKS), :hw] = nr
                ub_ref[pl.ds(r0, SCAN_BLOCKS), hw:] = ni
            return nr, ni

        sr, si = lax.fori_loop(0, ic, step, (st_ref[:, :hw], st_ref[:, hw:]), unroll=4 if ic % 4 == 0 else 1)
        st_ref[:, :hw] = sr
        st_ref[:, hw:] = si
        if full:
            sb = ub_ref[...].astype(BF16)
            s_out[...] = sb
            y_out[...] = jnp.dot(sb, wout_ref[...], preferred_element_type=F32)
        else:
            @pl.when(k == n_chunks - 1)
            def _():
                f_out[...] = st_ref[...]

    x_spec = pl.BlockSpec((None, rows, HEAD_DIM), lambda dd, t, k: (dd, chunk_idx(k, dd), t))
    win_spec = pl.BlockSpec((None, None, HEAD_DIM, sw), lambda dd, t, k: (dd, t, 0, 0))
    vec_spec = pl.BlockSpec((None, None, SCAN_BLOCKS, sw), lambda dd, t, k: (dd, t, 0, 0))
    scratch = [pltpu.VMEM((rows, sw), F32), pltpu.VMEM((SCAN_BLOCKS, sw), F32)]
    if full:
        in_specs = [x_spec, win_spec, vec_spec, vec_spec, pl.BlockSpec((None, None, sw, HEAD_DIM), lambda dd, t, k: (dd, t, 0, 0))]
        out_specs = [pl.BlockSpec((None, None, rows, sw), lambda dd, t, k: (dd, t, chunk_idx(k, dd), 0)), x_spec]
        out_shape = [jax.ShapeDtypeStruct((2, nt, ll, sw), BF16), jax.ShapeDtypeStruct((2, ll, d), F32)]
        args = (x2, win, lam, cin, wout)
    else:
        in_specs = [x_spec, win_spec, vec_spec]
        out_specs = vec_spec
        out_shape = jax.ShapeDtypeStruct((2, nt, SCAN_BLOCKS, sw), F32)
        args = (x2, win, lam)
    return pl.pallas_call(body, grid=(2, nt, n_chunks), in_specs=in_specs, out_specs=out_specs, out_shape=out_shape,
                          scratch_shapes=scratch, name=name,
                          compiler_params=_cparams("parallel", "parallel", "arbitrary"))(*args)


def _s5_bwd_call(dy2, wrt, lamc, cin, st, u2, wdt, n_chunks, name):
    _, ll, d = dy2.shape
    nt = d // HEAD_DIM
    sw = 2 * SSM_TILE_GROUPS * SSM_STATE
    hw = sw // 2
    rows = ll // n_chunks
    ic = rows // SCAN_BLOCKS

    def chunk_idx(k, dd):
        return jnp.where(dd == 0, n_chunks - 1 - k, k)

    def body(dy_ref, wrt_ref, lam_ref, cin_ref, stb_ref, u_ref, wdt_ref, du_out, dwd_out, dwr_out, dlam_out,
             ds_ref, a_ref, st_ref):
        k = pl.program_id(2)
        down = pl.program_id(0) == 0
        st_ref[...] = stb_ref[...].astype(F32)

        @pl.when(k == 0)
        def _():
            a_ref[...] = cin_ref[...]
            dwd_out[...] = jnp.zeros_like(dwd_out)
            dwr_out[...] = jnp.zeros_like(dwr_out)
            dlam_out[...] = jnp.zeros_like(dlam_out)

        dyb = dy_ref[...].astype(BF16)
        ds_ref[...] = jnp.dot(dyb, wrt_ref[...], preferred_element_type=F32)
        dwr_out[...] += _dot_tn(stb_ref[...], dyb)
        lr = lam_ref[:, :hw]
        li = lam_ref[:, hw:]

        def step(ii, carry):
            ar, ai, gr, gi = carry
            i = jnp.where(down, ic - 1 - ii, ii)
            r0 =pl.multiple_of(i * SCAN_BLOCKS, SCAN_BLOCKS)
            sr = st_ref[pl.ds(r0, SCAN_BLOCKS), :hw]
            si = st_ref[pl.ds(r0, SCAN_BLOCKS), hw:]
            gr = gr + ar * sr + ai * si
            gi = gi + ai * sr - ar * si
            nr = lr * ar - li * ai + ds_ref[pl.ds(r0, SCAN_BLOCKS), :hw]
            ni = lr * ai + li * ar + ds_ref[pl.ds(r0, SCAN_BLOCKS), hw:]
            ds_ref[pl.ds(r0, SCAN_BLOCKS), :hw] = nr
            ds_ref[pl.ds(r0, SCAN_BLOCKS), hw:] = ni
            return nr, ni, gr, gi

        init = (a_ref[:, :hw], a_ref[:, hw:], dlam_out[:, :hw], dlam_out[:, hw:])
        ar, ai, gr, gi = lax.fori_loop(0, ic, step, init, unroll=4 if ic % 4 == 0 else 1)
        a_ref[:, :hw] = ar
        a_ref[:, hw:] = ai
        dlam_out[:, :hw] = gr
        dlam_out[:, hw:] = gi
        ab = ds_ref[...].astype(BF16)
        du_out[...] = jnp.dot(ab, wdt_ref[...], preferred_element_type=F32).astype(du_out.dtype)
        dwd_out[...] += _dot_tn(u_ref[...].astype(BF16), ab)

    x_spec = pl.BlockSpec((None, rows, HEAD_DIM), lambda dd, t, k: (dd, chunk_idx(k, dd), t))
    w_in = pl.BlockSpec((None, None, HEAD_DIM, sw), lambda dd, t, k: (dd, t, 0, 0))
    w_out = pl.BlockSpec((None, None, sw, HEAD_DIM), lambda dd, t, k: (dd, t, 0, 0))
    vec_spec = pl.BlockSpec((None, None, SCAN_BLOCKS, sw), lambda dd, t, k: (dd, t, 0, 0))
    st_spec = pl.BlockSpec((None, None, rows, sw), lambda dd, t, k: (dd, t, chunk_idx(k, dd), 0))
    out_shape = [jax.ShapeDtypeStruct((2, ll, d), u2.dtype), jax.ShapeDtypeStruct((2, nt, HEAD_DIM, sw), F32),
                 jax.ShapeDtypeStruct((2, nt, sw, HEAD_DIM), F32), jax.ShapeDtypeStruct((2, nt, SCAN_BLOCKS, sw), F32)]
    return pl.pallas_call(
        body, grid=(2, nt, n_chunks),
        in_specs=[x_spec, w_in, vec_spec, vec_spec, st_spec, x_spec, w_out],
        out_specs=[x_spec, w_in, w_out, vec_spec], out_shape=out_shape,
        scratch_shapes=[pltpu.VMEM((rows, sw), F32), pltpu.VMEM((SCAN_BLOCKS, sw), F32), pltpu.VMEM((rows, sw), F32)],
        name=name, compiler_params=_cparams("parallel", "parallel", "arbitrary"))(dy2, wrt, lamc, cin, st, u2, wdt)


def _cpow(lr, li, n):
    rr, ri = jnp.ones_like(lr), jnp.zeros_like(li)
    br, bi = lr, li
    while n:
        if n & 1:
            rr, ri = _cmul(rr, ri, br, bi)
        br, bi = _cmul(br, bi, br, bi)
        n >>= 1
    return rr, ri


def _resolve_carries(finals, lam, block_len, down_dir):
    hw = finals.shape[-1] // 2
    pr, pi = _cpow(lam[:, :, 0, :hw], lam[:, :, 0, hw:], block_len)
    fr, fi = finals[..., :hw], finals[..., hw:]

    def walk(order):
        cr, ci = jnp.zeros_like(pr), jnp.zeros_like(pi)
        out = [None] * SCAN_BLOCKS
        for j in order:
            out[j] = jnp.concatenate([cr, ci], axis=-1)
            mr, mi = _cmul(pr, pi, cr, ci)
            cr, ci = mr + fr[:, :, j], mi + fi[:, :, j]
        return jnp.stack(out, axis=2)

    up, down = walk(range(SCAN_BLOCKS)), walk(range(SCAN_BLOCKS - 1, -1, -1))
    return jnp.stack([down[0], up[1]] if down_dir == 0 else [up[0], down[1]])


def _scan_chunks(ll):
    block_len = ll // SCAN_BLOCKS
    for ic in (132, 128, 96, 64, 48, 36, 32, 24, 16, 8):
        if block_len % ic == 0:
            return block_len // ic
    return 1


def make_s5_core(name):
    def run_fwd(u2, lam, wd, wr):
        ll = u2.shape[1]
        nc = _scan_chunks(ll)
        lam8 = jnp.broadcast_to(lam[:, :, None, :], lam.shape[:2] + (SCAN_BLOCKS, lam.shape[-1]))
        wdb = wd.astype(BF16)
        finals = _s5_scan_call(u2, wdb, lam8, None, None, False, nc, name + "_carry")
        cin = _resolve_carries(finals, lam8, ll // SCAN_BLOCKS, 1)
        st, y2 = _s5_scan_call(u2, wdb, lam8, cin, wr.astype(BF16), False, nc, name + "_scan")
        return y2, st, lam8

    @jax.custom_vjp
    def f(u2, lam, wd, wr):
        return run_fwd(u2, lam, wd, wr)[0]

    def fwd(u2, lam, wd, wr):
        y2, st, lam8 = run_fwd(u2, lam, wd, wr)
        return y2, (u2, lam8, wd, wr, st)

    def bwd(res, dy2):
        u2, lam8, wd, wr, st = res
        ll = u2.shape[1]
        nc = _scan_chunks(ll)
        hw = lam8.shape[-1] // 2
        lamc = jnp.concatenate([lam8[..., :hw], -lam8[..., hw:]], axis=-1)
        wrt = jnp.swapaxes(wr, 2, 3).astype(BF16)
        wdt = jnp.swapaxes(wd, 2, 3).astype(BF16)
        finals = _s5_scan_call(dy2, wrt, lamc, None, None, True, nc, name + "_bcarry")
        cin = _resolve_carries(finals, lamc, ll // SCAN_BLOCKS, 0)
        du2, dwd, dwr, dlam8 = _s5_bwd_call(dy2, wrt, lamc, cin, st, u2, wdt, nc, name + "_bscan")
        return du2, jnp.sum(dlam8, axis=2), dwd, dwr

    f.defvjp(fwd, bwd)
    return f


def _s5_params(a_re, a_im, log_dt, b_re, b_im, c_re, c_im):
    dt = jnp.exp(log_dt)[..., None]
    mag = jnp.exp(a_re * dt)
    lam_r, lam_i = mag * jnp.cos(a_im * dt), mag * jnp.sin(a_im * dt)
    den = a_re * a_re + a_im * a_im
    nr = lam_r - 1.0
    coef_r = (nr * a_re + lam_i * a_im) / den
    coef_i = (lam_i * a_re - nr * a_im) / den
    bbar_r = coef_r[..., None] * b_re - coef_i[..., None] * b_im
    bbar_i = coef_r[..., None] * b_im + coef_i[..., None] * b_re
    ndir, g, p = lam_r.shape
    tg = SSM_TILE_GROUPS
    nt = g // tg
    eye = jnp.eye(tg, dtype=F32)

    def tile_vec(v):
        return v.reshape(ndir, nt, tg * p)

    lam = jnp.concatenate([tile_vec(lam_r), tile_vec(lam_i)], axis=-1)

    def drive(b):
        bt = b.reshape(ndir, nt, tg, p, SSM_GROUP)
        return (jnp.swapaxes(bt, 3, 4)[:, :, :, :, None, :] * eye[None, None, :, None, :, None]).reshape(ndir, nt, tg * SSM_GROUP, tg * p)

    wd = jnp.concatenate([drive(bbar_r), drive(bbar_i)], axis=-1)

    def readout(c):
        ct = c.reshape(ndir, nt, tg, SSM_GROUP, p)
        return (jnp.swapaxes(ct, 3, 4)[:, :, :, :, None, :] * eye[None, None, :, None, :, None]).reshape(ndir, nt, tg * p, tg * SSM_GROUP)

    wr = jnp.concatenate([readout(c_re), -readout(c_im)], axis=2)
    return lam, wd, wr


def _to_scan_order(seq):
    ll, d = seq.shape
    return seq.reshape(SCAN_BLOCKS, ll // SCAN_BLOCKS, d).swapaxes(0, 1).reshape(ll, d)


def _from_scan_order(y2):
    ll, d = y2.shape
    return y2.reshape(ll // SCAN_BLOCKS, SCAN_BLOCKS, d).swapaxes(0, 1).reshape(ll, d)


def adamw(w, g, m, v, name):
    shape = w.shape
    cols = shape[-1] if len(shape) > 1 else shape[0]
    w2, g2, m2, v2 = (a.reshape(-1, cols) for a in (w, g, m, v))
    r = w2.shape[0]
    cap = max(1, (1024 * 1024) // (4 * cols))
    tr = r
    for t in (512, 256, 128, 64, 32, 16, 8):
        if t <= cap and r % t == 0:
            tr = t
            break
    c1 = 1.0 / (1.0 - ADAM_B1 ** ADAM_STEP)
    c2 = 1.0 / (1.0 - ADAM_B2 ** ADAM_STEP)

    def body(w_ref, g_ref, m_ref, v_ref, d_ref, mo_ref, vo_ref):
        gg = g_ref[...]
        mn = ADAM_B1 * m_ref[...] + (1.0 - ADAM_B1) * gg
        vn = ADAM_B2 * v_ref[...] + (1.0 - ADAM_B2) * (gg * gg)
        d_ref[...] = -ADAM_LR * ((mn * c1) / (jnp.sqrt(vn * c2) + ADAM_EPS) + ADAM_WD * w_ref[...])
        mo_ref[...] = mn
        vo_ref[...] = vn

    spec = pl.BlockSpec((tr, cols), lambda i: (i, 0))
    sds = jax.ShapeDtypeStruct((r, cols), F32)
    d, mn, vn = pl.pallas_call(body, grid=(r // tr,), in_specs=[spec] * 4, out_specs=[spec] * 3, out_shape=[sds] * 3,
                               name=name, compiler_params=_cparams("parallel"))(w2, g2, m2, v2)
    return d.reshape(shape), mn.reshape(shape), vn.reshape(shape)


def _my_pos():
    return lax.axis_index("x"), lax.axis_index("y"), lax.axis_index("c")


def _flip(pos, f):
    return tuple((1 - p) if b else p for p, b in zip(pos, f))


def _lin(pos):
    return 4 * pos[0] + 2 * pos[1] + pos[2]


def _remote_copies(src_ref, out_ref, send_sems, recv_sems, plan):
    me = _my_pos()
    copies = []
    for k, (f, sfn, dfn) in enumerate(plan):
        peer = _flip(me, f)
        copies.append(pltpu.make_async_remote_copy(
            src_ref=src_ref.at[sfn(me, peer)], dst_ref=out_ref.at[dfn(me, peer)], send_sem=send_sems.at[k],
            recv_sem=recv_sems.at[k], device_id=peer, device_id_type=MESH))
    return copies


def xchg(src, n_out, plan, name, inplace=False):
    piece = src.shape[1:]

    def body(src_ref, out_ref, send_sems, recv_sems):
        me = _my_pos()
        copies = []
        for k, (f, sfn, dfn) in enumerate(plan):
            peer = _flip(me, f)
            s_ref = (out_ref if inplace else src_ref).at[sfn(me, peer)]
            d_ref = out_ref.at[dfn(me, peer)]
            if any(f):
                cp = pltpu.make_async_remote_copy(src_ref=s_ref, dst_ref=d_ref, send_sem=send_sems.at[k],
                                                  recv_sem=recv_sems.at[k], device_id=peer, device_id_type=MESH)
            else:
                cp = pltpu.make_async_copy(s_ref, d_ref, recv_sems.at[k])
            cp.start()
            copies.append((cp, any(f)))
        for cp, remote in copies:
            if remote:
                cp.wait_recv()
            else:
                cp.wait()
        for cp, remote in copies:
            if remote:
                cp.wait_send()

    return pl.pallas_call(
        body, in_specs=[pl.BlockSpec(memory_space=pl.ANY)], out_specs=pl.BlockSpec(memory_space=pl.ANY),
        out_shape=jax.ShapeDtypeStruct((n_out,) + piece, src.dtype),
        scratch_shapes=[pltpu.SemaphoreType.DMA((len(plan),)), pltpu.SemaphoreType.DMA((len(plan),))],
        input_output_aliases={0: 0} if inplace else {}, name=name)(src)


_CHIP_FLIPS = ((1, 0, 0), (0, 1, 0), (1, 1, 0))
_ALL_FLIPS = tuple((a, b, c) for a in (0, 1) for b in (0, 1) for c in (0, 1))[1:]


def all_to_all8(src, name):
    plan = [((0, 0, 0), lambda me, peer: _lin(me), lambda me, peer: _lin(me))]
    plan += [(f, lambda me, peer: _lin(peer), lambda me, peer: _lin(me)) for f in _ALL_FLIPS]
    return xchg(src, N_DEV, plan, name)


def all_gather8(piece, name):
    plan = [((0, 0, 0), lambda me, peer: 0, lambda me, peer: _lin(me))]
    plan += [(f, lambda me, peer: 0, lambda me, peer: _lin(me)) for f in _ALL_FLIPS]
    return xchg(piece[None], N_DEV, plan, name)


def _ag_prepare(shard):
    k, ns = shard.shape
    px, py, _ = _my_pos()
    own = shard.astype(BF16)[None]
    return lax.dynamic_update_slice(jnp.zeros((4, k, ns), BF16), own, (2 * px + py, 0, 0)).reshape(8, k // 2, ns)


def _ag_plan():
    return [(f, lambda me, peer: _lin(me), lambda me, peer: _lin(me)) for f in _CHIP_FLIPS]


def _ag_finish(buf, name):
    plan = [((0, 0, 1), lambda me, peer, f=f: _lin(_flip(me, f)), lambda me, peer, f=f: _lin(_flip(me, f)))
            for f in _CHIP_FLIPS]
    _, kh, ns = buf.shape
    return xchg(buf, 8, plan, name, inplace=True).reshape(4, 2 * kh, ns)


def gather_weight(shard, name):
    k, ns = shard.shape
    buf = _ag_prepare(shard)

    def body(in_ref, out_ref, send_sems, recv_sems):
        me = _my_pos()
        sibling = _flip(me, (0, 0, 1))
        chips = [_flip(me, f) for f in _CHIP_FLIPS]

        def copy(sem, holder, to):
            rows = out_ref.at[4 * holder[0] + 2 * holder[1] + me[2]]
            return pltpu.make_async_remote_copy(src_ref=rows, dst_ref=rows, send_sem=send_sems.at[sem],
                                                recv_sem=recv_sems.at[sem], device_id=to, device_id_type=MESH)

        first = [copy(j, me, chip) for j, chip in enumerate(chips)]
        for cp in first:
            cp.start()
        passed = [copy(3 + j, chip, sibling) for j, chip in enumerate(chips)]
        for j, chip in enumerate(chips):
            copy(j, chip, me).wait_recv()
            passed[j].start()
        for j in range(3):
            passed[j].wait_recv()
        for cp in first + passed:
            cp.wait_send()

    full = pl.pallas_call(
        body, in_specs=[pl.BlockSpec(memory_space=pl.ANY)], out_specs=pl.BlockSpec(memory_space=pl.ANY),
        out_shape=jax.ShapeDtypeStruct(buf.shape, BF16),
        scratch_shapes=[pltpu.SemaphoreType.DMA((6,)), pltpu.SemaphoreType.DMA((6,))],
        input_output_aliases={0: 0}, name=name)(buf)
    return full.reshape(4, k, ns)


_RS_SLOTS = 7


def _sum_halves(g8, l1, c_idx, name):
    _, _, r, cc = g8.shape
    tr = _row_tile(r, r, cc)

    def body(c_ref, a_ref, b_ref, o_ref):
        o_ref[...] = (a_ref[...].astype(F32) + b_ref[...].astype(F32)).astype(o_ref.dtype)

    gs = pltpu.PrefetchScalarGridSpec(
        num_scalar_prefetch=1, grid=(4, r // tr),
        in_specs=[pl.BlockSpec((None, None, tr, cc), lambda s, i, c: (s, c[0], i, 0)),
                  pl.BlockSpec((None, tr, cc), lambda s, i, c: (s, i, 0))],
        out_specs=pl.BlockSpec((None, tr, cc), lambda s, i, c: (s, i, 0)))
    return pl.pallas_call(body, grid_spec=gs, out_shape=jax.ShapeDtypeStruct((_RS_SLOTS, r, cc), BF16), name=name,
                          compiler_params=_cparams("parallel", "parallel"))(c_idx, g8, l1)


def _sum_chips(buf, sc_idx, name):
    _, r, cc = buf.shape
    tr = _row_tile(r, r, cc)

    def body(s_ref, a_ref, b0_ref, b1_ref, b2_ref, o_ref):
        o_ref[...] = ((a_ref[...].astype(F32) + b0_ref[...].astype(F32)) + b1_ref[...].astype(F32)) + b2_ref[...].astype(F32)

    gs = pltpu.PrefetchScalarGridSpec(
        num_scalar_prefetch=1, grid=(r // tr,),
        in_specs=[pl.BlockSpec((None, tr, cc), lambda i, s: (s[0], i, 0))]
        + [pl.BlockSpec((None, tr, cc), lambda i, s, j=j: (4 + j, i, 0)) for j in range(3)],
        out_specs=pl.BlockSpec((None, tr, cc), lambda i, s: (s[1], i, 0)))
    return pl.pallas_call(body, grid_spec=gs, out_shape=jax.ShapeDtypeStruct((2, r, cc), F32), name=name,
                          compiler_params=_cparams("parallel"))(sc_idx, buf, buf, buf, buf)


def _rs_begin(g4, name):
    _, k, ns = g4.shape
    c_idx = jnp.reshape(_my_pos()[2], (1,)).astype(jnp.int32)
    plan1 = [((0, 0, 1), lambda me, peer, s=s: 2 * s + peer[2], lambda me, peer, s=s: s) for s in range(4)]
    l1 = xchg(g4.reshape(8, k // 2, ns), 4, plan1, name + "_rs_d2d")
    buf = _sum_halves(g4.reshape(4, 2, k // 2, ns), l1, c_idx, name + "_rs_sum2")
    plan2 = [(f, lambda me, peer: 2 * peer[0] + peer[1], lambda me, peer, j=j: 4 + j) for j, f in enumerate(_CHIP_FLIPS)]
    return buf, plan2


def _rs_finish(buf, name):
    _, kh, ns = buf.shape
    x, y, c = _my_pos()
    sc_idx = jnp.stack([2 * x + y, c]).astype(jnp.int32)
    halves = _sum_chips(buf, sc_idx, name + "_rs_sum4")
    plan3 = [((0, 0, 1), lambda me, peer: me[2], lambda me, peer: me[2])]
    return xchg(halves, 2, plan3, name + "_rs_swap", inplace=True).reshape(2 * kh, ns)


def reduce_scatter_weight(g4, name):
    buf, plan = _rs_begin(g4, name)
    return _rs_finish(xchg(buf, _RS_SLOTS, plan, name + "_rs_ici", inplace=True), name)


def _sum8(a8, name):
    _, r, cc = a8.shape
    tr = _row_tile(r, r, cc)

    def body(a_ref, o_ref):
        acc = a_ref[0]
        for j in range(1, N_DEV):
            acc = acc + a_ref[j]
        o_ref[...] = acc

    return pl.pallas_call(body, grid=(r // tr,), in_specs=[pl.BlockSpec((N_DEV, tr, cc), lambda i: (0, i, 0))],
                          out_specs=pl.BlockSpec((tr, cc), lambda i: (i, 0)), out_shape=jax.ShapeDtypeStruct((r, cc), F32),
                          name=name, compiler_params=_cparams("parallel"))(a8)


def all_reduce8(flat, name):
    n = flat.shape[0]
    unit = N_DEV * 8 * 128
    npad = -(-n // unit) * unit
    a = jnp.pad(flat, (0, npad - n)).reshape(N_DEV, npad // (N_DEV * 128), 128)
    mine = _sum8(all_to_all8(a, name + "_rs"), name + "_sum")
    return all_gather8(mine, name + "_ag").reshape(npad)[:n]


def _make_split(t, cuts):
    def pieces(qkv):
        out = []
        for rows in (slice(None, t), slice(t, None)):
            out += [qkv[rows, a:b] for a, b in zip(cuts[:-1], cuts[1:])]
        return tuple(out)

    @jax.custom_vjp
    def split(qkv):
        return pieces(qkv)

    def fwd(qkv):
        return pieces(qkv), None

    def bwd(_, g):
        n = len(cuts) - 1
        return (jnp.concatenate([jnp.concatenate(g[:n], axis=1), jnp.concatenate(g[n:], axis=1)], axis=0),)

    split.defvjp(fwd, bwd)
    return split


def _local_loss(x, ctx, target, mods, small, big, pending, shards, dims):
    t, m, d = dims["t"], dims["m"], dims["d"]
    a_w, bq_w, bkv_w = dims["a_w"], dims["bq_w"], dims["bkv_w"]
    z = jnp.concatenate([x, ctx], axis=0)
    big = dict(big)

    def grp(layer, j, n_groups=2):
        return mods[layer, :n_groups, j][:, None, :]

    def lin(kind, out_dtype, name, a, wname, gather=()):
        y, gathered = make_linear(kind, out_dtype, name)(a, big[wname], shards[wname], tuple(pending[n] for n in gather))
        big.update(zip(gather, gathered))
        return y

    h = make_norm_mod(t, "norm_mix0")(z, small["norm_mix"][0][None], grp(0, 1), grp(0, 0))
    qkv = lin("col", BF16, "attn_in", h, "attn_w_in", ("attn_w_out", "ffn_w1_0"))
    o3, o5 = 3 * a_w, 3 * a_w + bq_w + bkv_w
    cuts = (0, a_w, 2 * a_w, o3, o3 + bq_w, o5, o5 + bkv_w)
    qa, ka, va, qb_u, kb_u, vb, qa_c, ka_c, va_c, qb_c, kb_c, vb_c = _make_split(t, cuts)(qkv)
    qb, kb = make_rope(t, "rope_q")(qb_u), make_rope(t, "rope_k")(kb_u)
    sink = small["attn_sink"][0]
    no_sink = jnp.zeros((a_w // HEAD_DIM,), F32)
    na_meta, na_span, na_cases = _na_plan(t)
    oa = make_attention(na_meta, 1, na_span, True, False, True, "na")(
        qa, ka, va, ka_c, va_c, _na_bias(small["attn_rpb"][0], na_cases), no_sink)
    sw_meta, sw_span, sw_bias = _sw_plan(t)
    grp_b = bq_w // bkv_w
    ob = make_attention(sw_meta, grp_b, sw_span, True, True, False, "swa")(qb, kb, vb, kb_c, vb_c, jnp.asarray(sw_bias), sink)
    c_meta = np.array([[0] * (m // ATTN_BLOCK), [0] * (m // ATTN_BLOCK), [1] + [0] * (m // ATTN_BLOCK - 1)], np.int32)
    zero_bias = jnp.zeros((1, 1, ATTN_BLOCK, m), F32)
    oa_c = make_attention(c_meta, 1, m, False, False, False, "ctx_na")(qa_c, ka_c, va_c, ka_c, va_c, zero_bias, no_sink)
    ob_c = make_attention(c_meta, grp_b, m, False, True, False, "ctx_swa")(qb_c, kb_c, vb_c, kb_c, vb_c, zero_bias, sink)
    o = jnp.concatenate([jnp.concatenate([oa, ob], axis=1), jnp.concatenate([oa_c, ob_c], axis=1)], axis=0)
    y = lin("row", F32, "attn_out", o, "attn_w_out", ("ffn_w3_0",))
    z = make_gated_residual(t, "res_mix0")(z, y, grp(0, 2))
    h = make_norm_mod(t, "norm_ffn0")(z, small["norm_ffn"][0][None], grp(0, 4), grp(0, 3))
    h1 = lin("col", BF16, "ffn0_w1", h, "ffn_w1_0", ("ffn_w2_0", "ssm_w_glu"))
    a = make_swiglu_act("act0")(h1, lin("col", BF16, "ffn0_w3", h, "ffn_w3_0", ("ffn_w1_1",)))
    z = make_gated_residual(t, "res_ffn0")(z, lin("row", F32, "ffn0_w2", a, "ffn_w2_0", ("ffn_w3_1",)), grp(0, 5))

    h = make_norm_mod(t, "norm_mix1")(z, small["norm_mix"][1][None], grp(1, 1), grp(1, 0))
    hx, hc = h[:t], h[t:]
    lam, wd, wr = _s5_params(small["ssm_a_re"][0], small["ssm_a_im"][0], small["ssm_log_dt"][0], small["ssm_b_re"][0],
                             small["ssm_b_im"][0], small["ssm_c_re"][0], small["ssm_c_im"][0])
    u2 = jnp.stack([_to_scan_order(jnp.concatenate([hc, hx], axis=0)), _to_scan_order(h)])
    y2 = make_s5_core("s5")(u2, lam, wd, wr)
    ys = _from_scan_order(y2[0])[m:] + _from_scan_order(y2[1])[:t]
    gl = make_gelu_in("gelu")(hx, ys, small["ssm_d_full"][None])
    zz = lin("col", F32, "glu_w", gl, "ssm_w_glu", ("ffn_w2_1",))
    yx = make_glu("glu")(zz, small["ssm_b_glu_full"][None])
    xs = make_gated_residual(t, "res_mix1")(z[:t], yx, grp(1, 2, 1))
    h = make_norm_mod(t, "norm_ffn1")(xs, small["norm_ffn"][1][None], grp(1, 4, 1), grp(1, 3, 1))
    a = make_swiglu_act("act1")(lin("col", BF16, "ffn1_w1", h, "ffn_w1_1"), lin("col", BF16, "ffn1_w3", h, "ffn_w3_1"))
    xs = make_gated_residual(t, "res_ffn1")(xs, lin("row", F32, "ffn1_w2", a, "ffn_w2_1"), grp(1, 5, 1))
    return make_final_loss("loss_head")(xs, small["norm_final"][None], target)[0, 0]


_WEIGHTS = ['c_ctx', 'ada_w', 'ada_b', 'norm_mix', 'norm_ffn', 'ffn_w1', 'ffn_w3', 'ffn_w2', 'attn_w_in', 'attn_w_out',
            'attn_rpb', 'attn_sink', 'ssm_a_re', 'ssm_a_im', 'ssm_log_dt', 'ssm_b_re', 'ssm_b_im', 'ssm_c_re', 'ssm_c_im',
            'ssm_d', 'ssm_w_glu', 'ssm_b_glu', 'norm_final']
_LOCAL_SMALL = ['norm_mix', 'norm_ffn', 'attn_rpb', 'attn_sink', 'ssm_a_re', 'ssm_a_im', 'ssm_log_dt', 'ssm_b_re',
                'ssm_b_im', 'ssm_c_re', 'ssm_c_im', 'norm_final']
_MOD_ROWS = 16


def _gather_chip_vector(v, name):
    g = all_gather8(v[None], name)
    return g[0::2, 0, :].reshape(-1)


def kernel(x, c, ctx, c_ctx, ada_w, ada_b, norm_mix, norm_ffn, ffn_w1, ffn_w3, ffn_w2, attn_w_in, attn_w_out, attn_rpb, attn_sink, ssm_a_re, ssm_a_im, ssm_log_dt, ssm_b_re, ssm_b_im, ssm_c_re, ssm_c_im, ssm_d, ssm_w_glu, ssm_b_glu, norm_final, loss_target, m_c_ctx, m_ada_w, m_ada_b, m_norm_mix, m_norm_ffn, m_ffn_w1, m_ffn_w3, m_ffn_w2, m_attn_w_in, m_attn_w_out, m_attn_rpb, m_attn_sink, m_ssm_a_re, m_ssm_a_im, m_ssm_log_dt, m_ssm_b_re, m_ssm_b_im, m_ssm_c_re, m_ssm_c_im, m_ssm_d, m_ssm_w_glu, m_ssm_b_glu, m_norm_final, v_c_ctx, v_ada_w, v_ada_b, v_norm_mix, v_norm_ffn, v_ffn_w1, v_ffn_w3, v_ffn_w2, v_attn_w_in, v_attn_w_out, v_attn_rpb, v_attn_sink, v_ssm_a_re, v_ssm_a_im, v_ssm_log_dt, v_ssm_b_re, v_ssm_b_im, v_ssm_c_re, v_ssm_c_im, v_ssm_d, v_ssm_w_glu, v_ssm_b_glu, v_norm_final):
    env = dict(locals())
    w = {n: env[n] for n in _WEIGHTS}
    mom = {n: env["m_" + n] for n in _WEIGHTS}
    var = {n: env["v_" + n] for n in _WEIGHTS}
    _, t, d = x.shape
    m = ctx.shape[1]
    px, py, pc = _my_pos()
    s_me = 2 * px + py
    a_w =attn_rpb.shape[1] * HEAD_DIM
    bq_w = attn_sink.shape[1] * HEAD_DIM
    bkv_w = (4 * attn_w_in.shape[2] - 3 * a_w - bq_w) // 2
    dims = dict(t=t, m=m, d=d, a_w=a_w, bq_w=bq_w, bkv_w=bkv_w)
    n_layers = ada_w.shape[0]
    ada_cols = ada_w.shape[2]

    big = {"attn_w_in": gather_weight(attn_w_in[0], "ag_attn_in")}
    pending = {"attn_w_out": _ag_prepare(attn_w_out[0]), "ssm_w_glu": _ag_prepare(ssm_w_glu[0])}
    for l in range(n_layers):
        pending.update({f"ffn_w1_{l}": _ag_prepare(ffn_w1[l]), f"ffn_w3_{l}": _ag_prepare(ffn_w3[l]),
                        f"ffn_w2_{l}": _ag_prepare(ffn_w2[l])})
    small ={n: w[n] for n in _LOCAL_SMALL}
    small["ssm_d_full"] = _gather_chip_vector(ssm_d[0], "ag_ssm_d")
    small["ssm_b_glu_full"] = _gather_chip_vector(ssm_b_glu[0], "ag_b_glu")

    c_all = all_gather8(c, "ag_c")[:, 0, :]
    cond = jnp.concatenate([c_all, c_ctx[None], jnp.zeros((_MOD_ROWS - N_DEV - 1, d), F32)], axis=0)
    sig = jax.nn.sigmoid(cond)
    silu_c = (cond * sig).astype(BF16)
    ada_wb = ada_w.astype(BF16)
    mods_shard = jnp.stack([mm_nn(silu_c, ada_wb[l][None], "col", F32, f"ada_fwd{l}") for l in range(n_layers)])
    send = jnp.stack([jnp.stack([mods_shard[:, tgt], mods_shard[:, N_DEV]], axis=1).reshape(2 * n_layers, ada_cols)
                      for tgt in range(N_DEV)])
    plan = [((0, 0, 0), lambda me, peer: _lin(me), lambda me, peer: 2 * me[0] + me[1])]
    plan += [(f, lambda me, peer: _lin(peer), lambda me, peer: 2 * me[0] + me[1]) for f in _CHIP_FLIPS]
    got = xchg(send, 4, plan, "mods_xchg")
    mods = got.reshape(4, n_layers, 2, ada_cols).transpose(1, 2, 0, 3).reshape(n_layers, 2, 4 * ada_cols)
    mods = (mods + ada_b[:, None, :]).reshape(n_layers, 2, 6, d)

    shards = {"attn_w_in": attn_w_in[0], "attn_w_out": attn_w_out[0], "ssm_w_glu": ssm_w_glu[0]}
    for l in range(n_layers):
        shards.update({f"ffn_w1_{l}": ffn_w1[l], f"ffn_w3_{l}": ffn_w3[l], f"ffn_w2_{l}": ffn_w2[l]})

    def local(xx, mods_, small_, shards_):
        return _local_loss(xx, ctx[0], loss_target[0], mods_, small_, big, pending, shards_, dims)

    loss_local, vjp = jax.vjp(local, x[0], mods, small, shards)
    g_x, g_mods, g_small, g_shards = vjp(jnp.ones((), F32))
    loss = lax.psum(loss_local, ("x", "y", "c"))
    grads = {"attn_w_in": g_shards["attn_w_in"][None], "attn_w_out": g_shards["attn_w_out"][None],
             "ssm_w_glu": g_shards["ssm_w_glu"][None]}
    for n in ("ffn_w1", "ffn_w3", "ffn_w2"):
        grads[n] = jnp.stack([g_shards[f"{n}_{l}"] for l in range(n_layers)])

    gm = all_gather8(g_mods.reshape(2 * n_layers, 6 * d), "ag_dmods").reshape(N_DEV, n_layers, 2, 6 * d)
    ctx_row = gm[0, :, 1]
    for j in range(1, N_DEV):
        ctx_row = ctx_row + gm[j, :, 1]
    dm16 = jnp.concatenate([gm[:, :, 0].transpose(1, 0, 2), ctx_row[:, None], jnp.zeros((n_layers, _MOD_ROWS - N_DEV - 1, 6 * d), F32)], axis=1)
    grads["ada_b"] = jnp.sum(dm16, axis=1)
    dm_mine = lax.dynamic_slice_in_dim(dm16, s_me * ada_cols, ada_cols, axis=2).astype(BF16)
    grads["ada_w"] = jnp.stack([mm_tn(silu_c, dm_mine[l], (1, d, ada_cols), "col", F32, f"ada_dw{l}")[0] for l in range(n_layers)])
    dsilu = mm_nt(dm_mine[0], ada_wb[0][None], "col", F32, "ada_dc0")
    for l in range(1, n_layers):
        dsilu = dsilu + mm_nt(dm_mine[l], ada_wb[l][None], "col", F32, f"ada_dc{l}")
    dsilu_ctx = 0.5 * dsilu[N_DEV]

    packed = [(n, g_small[n]) for n in _LOCAL_SMALL] + [("ssm_d", g_small["ssm_d_full"]), ("ssm_b_glu", g_small["ssm_b_glu_full"]),
                                                        ("c_ctx", dsilu_ctx)]
    flat = all_reduce8(jnp.concatenate([a.reshape(-1) for _, a in packed]), "ar_small")
    off = 0
    for n, a in packed:
        grads[n] = flat[off:off + a.size].reshape(a.shape)
        off += a.size
    sig_ctx = jax.nn.sigmoid(c_ctx)
    grads["c_ctx"] = grads["c_ctx"] * (sig_ctx * (1.0 + c_ctx * (1.0 - sig_ctx)))
    grads["ssm_d"] = lax.dynamic_slice_in_dim(grads["ssm_d"], s_me * ssm_d.shape[1], ssm_d.shape[1])[None]
    grads["ssm_b_glu"] = lax.dynamic_slice_in_dim(grads["ssm_b_glu"], s_me * ssm_b_glu.shape[1], ssm_b_glu.shape[1])[None]

    delta, new_m, new_v = {}, {}, {}
    for n in _WEIGHTS:
        delta[n], new_m[n], new_v[n] = adamw(w[n], grads[n], mom[n], var[n], "adamw_" + n)
    return (loss, g_x[None], *[grads[n] for n in _WEIGHTS], *[delta[n] for n in _WEIGHTS],
            *[new_m[n] for n in _WEIGHTS], *[new_v[n] for n in _WEIGHTS])
```

```python
import functools
import math

import numpy as np
import jax
import jax.numpy as jnp
from jax import lax
from jax.experimental import pallas as pl
from jax.experimental.pallas import tpu as pltpu

F32 = jnp.float32
BF16 = jnp.bfloat16
MESH = pl.DeviceIdType.MESH

HEAD_DIM = 128
GRID_W = 64
NA_ROWS = 8
NA_COLS = 16
SW_RADIUS = 128
ATTN_BLOCK = 128
ROPE_BASE = 10000.0
SSM_GROUP = 16
SSM_STATE = 64
SSM_TILE_GROUPS = 8
SCAN_BLOCKS = 8
EPS = 1e-6
NEG_INF = -1e30
ADAM_LR, ADAM_B1, ADAM_B2, ADAM_EPS, ADAM_WD, ADAM_STEP = 0.001, 0.9, 0.999, 1e-08, 0.01, 10
VMEM_LIMIT_BYTES = 56 * 1024 * 1024
N_DEV = 8


def _cparams(*sem):
    return pltpu.CompilerParams(dimension_semantics=tuple(sem) if sem else None, vmem_limit_bytes=VMEM_LIMIT_BYTES)


def _pick(n, cands):
    for c in cands:
        if n % c == 0:
            return c
    return n


def _dot_nn(a, b):
    return jnp.dot(a, b, preferred_element_type=F32)


def _dot_nt(a, b):
    return lax.dot_general(a, b, (((1,), (1,)), ((), ())), preferred_element_type=F32)


def _dot_tn(a, b):
    return lax.dot_general(a, b, (((0,), (0,)), ((), ())), preferred_element_type=F32)


def _mm_call(name, grid, ins, in_specs, o_specs, out_sds, acc_shape, step, carry=None):
    nk = grid[2]
    nb = 0 if carry is None else len(carry)
    ni, no = len(ins), len(out_sds)
    assert nk == 1 or no == 1

    def body(*refs):
        in_refs = refs[:ni]
        o_refs = refs[ni + nb:ni + nb + no]
        rest = refs[ni + nb + no + nb:]
        acc_ref = rest[0] if nk > 1 else None
        o_ref = o_refs[0]
        if carry is not None:
            bufs = refs[ni + nb + no:ni + nb + no + nb]
            sems = rest[-2 * nb:]
            ids = [pl.program_id(ax) for ax in range(3)]
            first = (ids[0] == 0) & (ids[1] == 0) & (ids[2] == 0)
            last = (ids[0] == grid[0] - 1) & (ids[1] == grid[1] - 1) & (ids[2] == grid[2] - 1)
            copies = []
            for q, (_, plan) in enumerate(carry):
                copies += _remote_copies(bufs[q], bufs[q], sems[2 * q], sems[2 * q + 1], plan)

            @pl.when(first)
            def _():
                for cp in copies:
                    cp.start()

        kk = pl.program_id(2)
        if nk == 1:
            for ref, val in zip(o_refs, step(*in_refs)):
                ref[...] = val.astype(ref.dtype)
        else:
            @pl.when(kk == 0)
            def _():
                acc_ref[...] = step(*in_refs)[0]

            @pl.when(kk > 0)
            def _():
                acc_ref[...] = step(*in_refs)[0] + acc_ref[...]

            @pl.when(kk == nk - 1)
            def _():
                o_ref[...] = acc_ref[...].astype(o_ref.dtype)

        if carry is not None:
            @pl.when(last)
            def _():
                for cp in copies:
                    cp.wait_recv()
                for cp in copies:
                    cp.wait_send()

    scratch = [pltpu.VMEM(acc_shape, F32)] if nk > 1 else []
    if carry is None:
        return pl.pallas_call(
            body, grid=grid, in_specs=list(in_specs), out_specs=list(o_specs), out_shape=list(out_sds),
            scratch_shapes=scratch, name=name, compiler_params=_cparams("parallel", "parallel", "arbitrary"))(*ins)
    any_spec = pl.BlockSpec(memory_space=pl.ANY)
    for _, plan in carry:
        scratch += [pltpu.SemaphoreType.DMA((len(plan),)), pltpu.SemaphoreType.DMA((len(plan),))]
    return pl.pallas_call(
        body, grid=grid, in_specs=list(in_specs) + [any_spec] * nb, out_specs=list(o_specs) + [any_spec] * nb,
        out_shape=list(out_sds) + [jax.ShapeDtypeStruct(buf.shape, buf.dtype) for buf, _ in carry],
        scratch_shapes=scratch, input_output_aliases={ni + q: no + q for q in range(nb)}, name=name,
        compiler_params=_cparams("arbitrary", "arbitrary", "arbitrary"))(*ins, *[buf for buf, _ in carry])


_ROW_TILES = (768, 512, 256, 128, 64, 32, 16, 8)
_K_TILES = (2048, 1408, 1024, 512, 256, 128)
_CONTRACT_TILES = (1408, 1024, 768, 512, 256, 128)
_FUSED_ROW_TILES = (528, 512, 384, 256, 128, 64, 32, 16, 8)


def mm_nn(a, w3, kind, out_dtype, name, carry=None):
    r = a.shape[0]
    s, d1, d2 = w3.shape
    tm = _pick(r, _ROW_TILES)
    if kind == "col":
        grid = (s, r // tm, 1)
        a_spec = pl.BlockSpec((tm, d1), lambda j, i, k: (i, 0))
        b_spec = pl.BlockSpec((None, d1, d2), lambda j, i, k: (j, 0, 0))
        o_spec = pl.BlockSpec((tm, d2), lambda j, i, k: (i, j))
        n = s * d2

        def step(a_ref, b_ref):
            return (_dot_nn(a_ref[...], b_ref[...]),)
    else:
        tn = d2 if d2 <= 1024 else _pick(d2, (1024, 512, 256, 128))
        grid = (d2 // tn, r // tm, 1)
        a_spec = pl.BlockSpec((tm, s * d1), lambda j, i, k: (i, 0))
        b_spec = pl.BlockSpec((s, d1, tn), lambda j, i, k: (0, 0, j))
        o_spec = pl.BlockSpec((tm, tn), lambda j, i, k: (i, j))
        n = d2

        def step(a_ref, b_ref):
            p = _dot_nn(a_ref[:, :d1], b_ref[0])
            for q in range(1, s):
                p = _dot_nn(a_ref[:, q * d1:(q + 1) * d1], b_ref[q]) + p
            return (p,)
    out = _mm_call(name, grid, [a, w3], [a_spec, b_spec], [o_spec], [jax.ShapeDtypeStruct((r, n), out_dtype)], None, step, carry)
    return out if carry else out[0]


def mm_nt(dy, w3, kind, out_dtype, name, carry=None, addend=None):
    r = dy.shape[0]
    s, d1, d2 = w3.shape
    tm = _pick(r, _ROW_TILES)
    if kind == "col":
        tko = d1 if d1 <= 1024 else _pick(d1, (1024, 512, 256, 128))
        grid = (d1 // tko, r // tm, 1)
        a_spec = pl.BlockSpec((tm, s * d2), lambda j, i, k: (i, 0))
        b_spec = pl.BlockSpec((s, tko, d2), lambda j, i, k: (0, j, 0))
        o_spec = pl.BlockSpec((tm, tko), lambda j, i, k: (i, j))
        kdim = d1

        def step(a_ref, b_ref, *more):
            p = _dot_nt(a_ref[:, :d2], b_ref[0])
            for q in range(1, s):
                p = _dot_nt(a_ref[:, q * d2:(q + 1) * d2], b_ref[q]) + p
            return (p + more[0][...].astype(F32),) if more else (p,)
    else:
        grid = (s, r // tm, 1)
        a_spec = pl.BlockSpec((tm, d2), lambda j, i, k: (i, 0))
        b_spec = pl.BlockSpec((None, d1, d2), lambda j, i, k: (j, 0, 0))
        o_spec = pl.BlockSpec((tm, d1), lambda j, i, k: (i, j))
        kdim = s * d1

        def step(a_ref, b_ref, *more):
            p = _dot_nt(a_ref[...], b_ref[...])
            return (p + more[0][...].astype(F32),) if more else (p,)
    ins, specs = [dy, w3], [a_spec, b_spec]
    if addend is not None:
        ins, specs = ins + [addend], specs + [o_spec]
    out = _mm_call(name, grid, ins, specs, [o_spec], [jax.ShapeDtypeStruct((r, kdim), out_dtype)], None, step, carry)
    return out if carry else out[0]


def mm_tn(a, dy, w_shape, kind, out_dtype, name):
    s, d1, d2 = w_shape
    r = dy.shape[0]
    tr = _pick(r, _CONTRACT_TILES)
    if kind == "col":
        tkk = d1 if d1 <= 1024 else _pick(d1, (1024, 512, 256, 128))
        grid = (s * (d1 // tkk), 1, r // tr)
        nkk = d1 // tkk
        a_spec = pl.BlockSpec((tr, tkk), lambda j, i, k: (k, j % nkk))
        b_spec = pl.BlockSpec((tr, d2), lambda j, i, k: (k, j // nkk))
        o_spec = pl.BlockSpec((None, tkk, d2), lambda j, i, k: (j // nkk, j % nkk, 0))
        acc = (tkk, d2)
    else:
        tn = d2 if d2 <= 1024 else _pick(d2, (1024, 512, 256, 128))
        nn = d2 // tn
        grid = (s * nn, 1, r // tr)
        a_spec = pl.BlockSpec((tr, d1), lambda j, i, k: (k, j // nn))
        b_spec = pl.BlockSpec((tr, tn), lambda j, i, k: (k, j % nn))
        o_spec = pl.BlockSpec((None, d1, tn), lambda j, i, k: (j // nn, 0, j % nn))
        acc = (d1, tn)

    def step(a_ref, b_ref):
        return (_dot_tn(a_ref[...], b_ref[...]),)
    return _mm_call(name, grid, [a, dy], [a_spec, b_spec], [o_spec], [jax.ShapeDtypeStruct(w_shape, out_dtype)], acc, step)[0]


def mm_ffn_in(h, w1, w3, name, carry=None):
    r = h.shape[0]
    s, d1, d2 = w1.shape
    tm = _pick(r, _FUSED_ROW_TILES)
    a_spec = pl.BlockSpec((tm, d1), lambda j, i, k: (i, 0))
    b_spec = pl.BlockSpec((None, d1, d2), lambda j, i, k: (j, 0, 0))
    o_spec = pl.BlockSpec((tm, d2), lambda j, i, k: (i, j))
    sds = jax.ShapeDtypeStruct((r, s * d2), BF16)

    def step(a_ref, b1_ref, b3_ref):
        p1 = _dot_nn(a_ref[...], b1_ref[...])
        p3 = _dot_nn(a_ref[...], b3_ref[...])
        return p1, p3, _silu(p1) * p3

    return _mm_call(name, (s, r // tm, 1), [h, w1, w3], [a_spec, b_spec, b_spec], [o_spec] * 3, [sds] * 3, None, step, carry)


def mm_ffn_back(dy, w2, h1, h3, name, carry=None):
    r = dy.shape[0]
    s, d1, d2 = w2.shape
    tm = _pick(r, _FUSED_ROW_TILES)
    a_spec = pl.BlockSpec((tm, d2), lambda j, i, k: (i, 0))
    b_spec = pl.BlockSpec((None, d1, d2), lambda j, i, k: (j, 0, 0))
    o_spec = pl.BlockSpec((tm, d1), lambda j, i, k: (i, j))
    sds = jax.ShapeDtypeStruct((r, s * d1), BF16)

    def step(a_ref, b_ref, h1_ref, h3_ref):
        g = _dot_nt(a_ref[...], b_ref[...])
        a1 = h1_ref[...].astype(F32)
        sg = jax.nn.sigmoid(a1)
        return g * h3_ref[...].astype(F32) * (sg * (1.0 + a1 * (1.0 - sg))), g * a1 * sg

    return _mm_call(name, (s, r // tm, 1), [dy, w2, h1, h3], [a_spec, b_spec, o_spec, o_spec], [o_spec] * 2, [sds] * 2,
                    None, step, carry)


def make_linear(kind, out_dtype, name):
    def run(a, w3, gathering):
        carry = [(buf, _ag_plan()) for buf in gathering] or None
        out = mm_nn(a, w3, kind, out_dtype, name + "_fwd", carry)
        if not carry:
            return out, ()
        return out[0], _ag_finish_all(out[1:], name)

    @jax.custom_vjp
    def linear(a, w3, w_shard, gathering):
        return run(a, w3, gathering)

    def fwd(a, w3, w_shard, gathering):
        return run(a, w3, gathering), (a, w3, w_shard.shape, len(gathering))

    def bwd(res, cts):
        a, w3, shard_shape, n_gathering = res
        dyb = cts[0].astype(BF16)
        dw = mm_tn(a, dyb, w3.shape, kind, BF16, name + "_dw")
        buf, plan = _rs_begin(dw, name)
        da, buf = mm_nt(dyb, w3, kind, a.dtype, name + "_dx", carry=[(buf, plan)])
        return da, None, _rs_finish(buf, name).reshape(shard_shape), (None,) * n_gathering

    linear.defvjp(fwd, bwd)
    return linear


def _ag_finish_all(bufs, name):
    return tuple(_ag_finish(buf, f"{name}_gathered{q}") for q, buf in enumerate(bufs))


def make_ffn(name, w2_pending):
    def run(h, w1, w3, w2, gathering):
        g_in, g_out = gathering
        if w2_pending:
            g_in = (w2,) + tuple(g_in)
        carry = [(buf, _ag_plan()) for buf in g_in] or None
        out = mm_ffn_in(h, w1, w3, name + "_in", carry)
        h1, h3, act = out[:3]
        done_in = _ag_finish_all(out[3:], name + "_in")
        if w2_pending:
            w2, done_in = done_in[0], done_in[1:]
        carry2 = [(buf, _ag_plan()) for buf in g_out] or None
        y = mm_nn(act, w2, "row", F32, name + "_out", carry2)
        done = (done_in, _ag_finish_all(y[1:], name + "_out") if carry2 else ())
        return (y[0] if carry2 else y), done, (h1, h3, act, w2)

    @jax.custom_vjp
    def ffn(h, w1, w3, w2, s1, s3, s2, gathering):
        return run(h, w1, w3, w2, gathering)[:2]

    def fwd(h, w1, w3, w2, s1, s3, s2, gathering):
        y, done, (h1, h3, act, w2) = run(h, w1, w3, w2, gathering)
        return (y, done), (h, w1, w3, w2, h1, h3, act, s1.shape, s3.shape, s2.shape, tuple(len(g) for g in gathering))

    def bwd(res, cts):
        h, w1, w3, w2, h1, h3, act, shape1, shape3, shape2, n_gathering = res
        dyb = cts[0].astype(BF16)
        buf2, plan2 = _rs_begin(mm_tn(act, dyb, w2.shape, "row", BF16, name + "_dw2"), name + "_w2")
        dh1, dh3, buf2 = mm_ffn_back(dyb, w2, h1, h3, name + "_back", carry=[(buf2, plan2)])
        buf1, plan1 = _rs_begin(mm_tn(h, dh1, w1.shape, "col", BF16, name + "_dw1"), name + "_w1")
        buf3, plan3 = _rs_begin(mm_tn(h, dh3, w3.shape, "col", BF16, name + "_dw3"), name + "_w3")
        dh, buf1 = mm_nt(dh1, w1, "col", h.dtype, name + "_dx1", carry=[(buf1, plan1)])
        dh, buf3 = mm_nt(dh3, w3, "col", h.dtype, name + "_dx3", carry=[(buf3, plan3)], addend=dh)
        grads = [_rs_finish(b, name + n).reshape(sh) for b, n, sh in
                 ((buf1, "_w1", shape1), (buf3, "_w3", shape3), (buf2, "_w2", shape2))]
        return (dh, None, None, None, *grads, tuple((None,) * n for n in n_gathering))

    ffn.defvjp(fwd, bwd)
    return ffn


def _row_tile(r, t0, d):
    cap = max(8, (2 * 1024 * 1024) // (4 * d))
    cands = [t for t in (1024, 512, 256, 128, 64, 32, 16, 8) if t <= cap]
    for t in cands:
        if r % t == 0 and t0 % t == 0:
            return t
    raise ValueError("no row tile")


def _grp_spec(d, nb0):
    return pl.BlockSpec((None, 1, d), lambda i: (i // nb0, 0, 0))


def _norm_mod_fwd(z, g, scale, shift, t0, name):
    r, d = z.shape
    tr = _row_tile(r, t0, d)
    nb0 = t0 // tr

    def body(z_ref, g_ref, sc_ref, sh_ref, o_ref):
        zz = z_ref[...]
        rstd = lax.rsqrt(jnp.mean(zz * zz, axis=-1, keepdims=True) + EPS)
        y = zz * rstd * g_ref[...]
        o_ref[...] = (y * (1.0 + sc_ref[...]) + sh_ref[...]).astype(o_ref.dtype)

    return pl.pallas_call(
        body, grid=(r // tr,),
        in_specs=[pl.BlockSpec((tr, d), lambda i: (i, 0)), pl.BlockSpec((1, d), lambda i: (0, 0)),
                  _grp_spec(d, nb0), _grp_spec(d, nb0)],
        out_specs=pl.BlockSpec((tr, d), lambda i: (i, 0)),
        out_shape=jax.ShapeDtypeStruct((r, d), BF16), name=name, compiler_params=_cparams("parallel"))(z, g, scale, shift)


def _norm_mod_bwd(z, g, scale, dh, t0, name):
    r, d = z.shape
    ng = scale.shape[0]
    tr = _row_tile(r, t0, d)
    nb0 = t0 // tr

    def body(z_ref, g_ref, sc_ref, dh_ref, dz_ref, dg_ref, dsc_ref, dsh_ref):
        i = pl.program_id(0)
        zz = z_ref[...]
        gg = g_ref[...]
        rstd = lax.rsqrt(jnp.mean(zz * zz, axis=-1, keepdims=True) + EPS)
        zhat = zz * rstd
        dhh = dh_ref[...].astype(F32)
        dy = dhh * (1.0 + sc_ref[...])
        dyg = dy * gg
        dz_ref[...] = rstd * (dyg - zhat * jnp.mean(dyg * zhat, axis=-1, keepdims=True))

        @pl.when(i == 0)
        def _():
            dg_ref[...] = jnp.zeros_like(dg_ref)

        @pl.when((i == 0) | (i == nb0))
        def _():
            dsc_ref[...] = jnp.zeros_like(dsc_ref)
            dsh_ref[...] = jnp.zeros_like(dsh_ref)

        dg_ref[...] += jnp.sum(dy * zhat, axis=0, keepdims=True)
        dsc_ref[...] += jnp.sum(dhh * (zhat * gg), axis=0, keepdims=True)
        dsh_ref[...] += jnp.sum(dhh, axis=0, keepdims=True)

    return pl.pallas_call(
        body, grid=(r // tr,),
        in_specs=[pl.BlockSpec((tr, d), lambda i: (i, 0)), pl.BlockSpec((1, d), lambda i: (0, 0)),
                  _grp_spec(d, nb0), pl.BlockSpec((tr, d), lambda i: (i, 0))],
        out_specs=[pl.BlockSpec((tr, d), lambda i: (i, 0)), pl.BlockSpec((1, d), lambda i: (0, 0)),
                   _grp_spec(d, nb0), _grp_spec(d, nb0)],
        out_shape=[jax.ShapeDtypeStruct((r, d), F32), jax.ShapeDtypeStruct((1, d), F32),
                   jax.ShapeDtypeStruct((ng, 1, d), F32), jax.ShapeDtypeStruct((ng, 1, d), F32)],
        name=name, compiler_params=_cparams("arbitrary"))(z, g, scale, dh)


def make_norm_mod(t0, name):
    @jax.custom_vjp
    def f(z, g, scale, shift):
        return _norm_mod_fwd(z, g, scale, shift, t0, name + "_fwd")

    def fwd(z, g, scale, shift):
        return _norm_mod_fwd(z, g, scale, shift, t0, name + "_fwd"), (z, g, scale)

    def bwd(res, dh):
        z, g, scale = res
        dz, dg, dsc, dsh = _norm_mod_bwd(z, g, scale, dh, t0, name + "_bwd")
        return dz, dg, dsc, dsh

    f.defvjp(fwd, bwd)
    return f


def _gated_fwd(z, y, gate, t0, name):
    r, d = z.shape
    tr = _row_tile(r, t0, d)
    nb0 = t0 // tr

    def body(z_ref, y_ref, g_ref, o_ref):
        o_ref[...] = z_ref[...] + g_ref[...] * y_ref[...].astype(F32)

    return pl.pallas_call(
        body, grid=(r // tr,),
        in_specs=[pl.BlockSpec((tr, d), lambda i: (i, 0)), pl.BlockSpec((tr, d), lambda i: (i, 0)), _grp_spec(d, nb0)],
        out_specs=pl.BlockSpec((tr, d), lambda i: (i, 0)),
        out_shape=jax.ShapeDtypeStruct((r, d), F32), name=name, compiler_params=_cparams("parallel"))(z, y, gate)


def _gated_bwd(y, gate, dzn, t0, name):
    r, d = y.shape
    ng = gate.shape[0]
    tr = _row_tile(r, t0, d)
    nb0 = t0 // tr

    def body(y_ref, g_ref, dz_ref, dy_ref, dg_ref):
        i = pl.program_id(0)
        dzz = dz_ref[...]
        dy_ref[...] = (g_ref[...] * dzz).astype(dy_ref.dtype)

        @pl.when((i == 0) | (i == nb0))
        def _():
            dg_ref[...] = jnp.zeros_like(dg_ref)

        dg_ref[...] += jnp.sum(dzz * y_ref[...].astype(F32), axis=0, keepdims=True)

    return pl.pallas_call(
        body, grid=(r // tr,),
        in_specs=[pl.BlockSpec((tr, d), lambda i: (i, 0)), _grp_spec(d, nb0), pl.BlockSpec((tr, d), lambda i: (i, 0))],
        out_specs=[pl.BlockSpec((tr, d), lambda i: (i, 0)), _grp_spec(d, nb0)],
        out_shape=[jax.ShapeDtypeStruct((r, d), y.dtype), jax.ShapeDtypeStruct((ng, 1, d), F32)],
        name=name, compiler_params=_cparams("arbitrary"))(y, gate, dzn)


def make_gated_residual(t0, name):
    @jax.custom_vjp
    def f(z, y, gate):
        return _gated_fwd(z, y, gate, t0, name + "_fwd")

    def fwd(z, y, gate):
        return _gated_fwd(z, y, gate, t0, name + "_fwd"), (y, gate)

    def bwd(res, dzn):
        y, gate = res
        dy, dgate = _gated_bwd(y, gate, dzn, t0, name + "_bwd")
        return dzn, dy, dgate

    f.defvjp(fwd, bwd)
    return f


def _ew_call(name, body, ins, outs_sds, r, widths_in, widths_out, tr, extra_in=(), extra_specs=(), sem="parallel"):
    in_specs = [pl.BlockSpec((tr, w), lambda i: (i, 0)) for w in widths_in] + list(extra_specs)
    out_specs = [pl.BlockSpec((tr, w), lambda i: (i, 0)) if w is not None else pl.BlockSpec(s.shape, lambda i: (0,) * len(s.shape))
                 for w, s in zip(widths_out, outs_sds)]
    return pl.pallas_call(body, grid=(r // tr,), in_specs=in_specs, out_specs=out_specs, out_shape=outs_sds,
                          name=name, compiler_params=_cparams(sem))(*ins, *extra_in)


def _silu(x):
    return x * jax.nn.sigmoid(x)


_GELU_C = math.sqrt(2.0 / math.pi)


def _gelu_and_grad(y):
    inner = _GELU_C * (y + 0.044715 * y * y * y)
    t = jnp.tanh(inner)
    val = 0.5 * y * (1.0 + t)
    grad = 0.5 * (1.0 + t) + 0.5 * y * (1.0 - t * t) * _GELU_C * (1.0 + 3 * 0.044715 * y * y)
    return val, grad


def make_gelu_in(name):
    def fwd_call(u, ys, dsk):
        r, d = u.shape
        tr = _row_tile(r, r, d)

        def body(u_ref, y_ref, d_ref, o_ref):
            y = d_ref[...] * u_ref[...].astype(F32) + y_ref[...]
            o_ref[...] = _gelu_and_grad(y)[0].astype(o_ref.dtype)

        return _ew_call(name + "_fwd", body, (u, ys), [jax.ShapeDtypeStruct((r, d), BF16)], r, (d, d), (d,), tr,
                        extra_in=(dsk,), extra_specs=(pl.BlockSpec((1, d), lambda i: (0, 0)),))[0]

    @jax.custom_vjp
    def f(u, ys, dsk):
        return fwd_call(u, ys, dsk)

    def fwd(u, ys, dsk):
        return fwd_call(u, ys, dsk), (u, ys, dsk)

    def bwd(res, dg):
        u, ys, dsk = res
        r, d = u.shape
        tr = _row_tile(r, r, d)

        def body(u_ref, y_ref, dg_ref, d_ref, du_ref, dy_ref, dd_ref):
            i = pl.program_id(0)
            uu = u_ref[...].astype(F32)
            y = d_ref[...] * uu + y_ref[...]
            dy = dg_ref[...].astype(F32) * _gelu_and_grad(y)[1]
            dy_ref[...] = dy
            du_ref[...] = (d_ref[...] * dy).astype(du_ref.dtype)

            @pl.when(i == 0)
            def _():
                dd_ref[...] = jnp.zeros_like(dd_ref)

            dd_ref[...] += jnp.sum(dy * uu, axis=0, keepdims=True)

        outs = [jax.ShapeDtypeStruct((r, d), u.dtype), jax.ShapeDtypeStruct((r, d), F32), jax.ShapeDtypeStruct((1, d), F32)]
        du, dy, dd = _ew_call(name + "_bwd", body, (u, ys, dg), outs, r, (d, d, d), (d, d, None), tr,
                              extra_in=(dsk,), extra_specs=(pl.BlockSpec((1, d), lambda i: (0, 0)),), sem="arbitrary")
        return du, dy, dd

    f.defvjp(fwd, bwd)
    return f


def make_glu(name):
    def fwd_call(z, b):
        r, d2 = z.shape
        d = d2 // 2
        tr = _row_tile(r, r, d2)

        def body(z_ref, b_ref, o_ref):
            zz = z_ref[...].astype(F32) + b_ref[...]
            o_ref[...] = zz[:, :d] * jax.nn.sigmoid(zz[:, d:])

        return _ew_call(name + "_fwd", body, (z,), [jax.ShapeDtypeStruct((r, d), F32)], r, (d2,), (d,), tr,
                        extra_in=(b,), extra_specs=(pl.BlockSpec((1, d2), lambda i: (0, 0)),))[0]

    @jax.custom_vjp
    def f(z, b):
        return fwd_call(z, b)

    def fwd(z, b):
        return fwd_call(z, b), (z, b)

    def bwd(res, do):
        z, b = res
        r, d2 = z.shape
        d = d2 // 2
        tr = _row_tile(r, r, d2)

        def body(z_ref, do_ref, b_ref, dz_ref, db_ref):
            i = pl.program_id(0)
            zz = z_ref[...].astype(F32) + b_ref[...]
            sg = jax.nn.sigmoid(zz[:, d:])
            g = do_ref[...]
            dza = g * sg
            dzb = g * zz[:, :d] * sg * (1.0 - sg)
            dz_ref[:, :d] = dza.astype(dz_ref.dtype)
            dz_ref[:, d:] = dzb.astype(dz_ref.dtype)

            @pl.when(i == 0)
            def _():
                db_ref[...] = jnp.zeros_like(db_ref)

            db_ref[:, :d] += jnp.sum(dza, axis=0, keepdims=True)
            db_ref[:, d:] += jnp.sum(dzb, axis=0, keepdims=True)

        outs = [jax.ShapeDtypeStruct((r, d2), z.dtype), jax.ShapeDtypeStruct((1, d2), F32)]
        dz, db = _ew_call(name + "_bwd", body, (z, do), outs, r, (d2, d), (d2, None), tr,
                          extra_in=(b,), extra_specs=(pl.BlockSpec((1, d2), lambda i: (0, 0)),), sem="arbitrary")
        return dz, db

    f.defvjp(fwd, bwd)
    return f


def make_final_loss(name):
    def call(z, g, target):
        r, d = z.shape
        tr = _row_tile(r, r, d)

        def body(z_ref, t_ref, g_ref, dz_ref, dg_ref, l_ref):
            i = pl.program_id(0)
            zz = z_ref[...]
            gg = g_ref[...]
            rstd = lax.rsqrt(jnp.mean(zz * zz, axis=-1, keepdims=True) + EPS)
            zhat = zz * rstd
            e = zhat * gg - t_ref[...]
            dy = e * (1.0 / d)
            dyg = dy * gg
            dz_ref[...] = rstd * (dyg - zhat * jnp.mean(dyg * zhat, axis=-1, keepdims=True))

            @pl.when(i == 0)
            def _():
                dg_ref[...] = jnp.zeros_like(dg_ref)
                l_ref[...] = jnp.zeros_like(l_ref)

            dg_ref[...] += jnp.sum(dy * zhat, axis=0, keepdims=True)
            l_ref[...] += jnp.sum(jnp.sum(e * e, axis=1, keepdims=True), axis=0, keepdims=True) * (0.5 / d)

        outs = [jax.ShapeDtypeStruct((r, d), F32), jax.ShapeDtypeStruct((1, d), F32), jax.ShapeDtypeStruct((1, 1), F32)]
        return _ew_call(name, body, (z, target), outs, r, (d, d), (d, None, None), tr,
                        extra_in=(g,), extra_specs=(pl.BlockSpec((1, d), lambda i: (0, 0)),), sem="arbitrary")

    @jax.custom_vjp
    def f(z, g, target):
        return call(z, g, target)[2]

    def fwd(z, g, target):
        dz, dg, loss = call(z, g, target)
        return loss, (dz, dg)

    def bwd(res, dl):
        dz, dg = res
        s = dl[0, 0]
        return dz * s, dg * s, None

    f.defvjp(fwd, bwd)
    return f


def _rope_tables(t):
    quarter = HEAD_DIM // 4
    inv_freq = ROPE_BASE ** (-np.arange(quarter, dtype=np.float64) / quarter)
    pos = np.arange(t)
    ang_r = (pos // GRID_W)[:, None] * inv_freq[None, :]
    ang_c = (pos % GRID_W)[:, None] * inv_freq[None, :]
    cos = np.concatenate([np.cos(ang_r), np.cos(ang_r), np.cos(ang_c), np.cos(ang_c)], axis=1)
    sin = np.concatenate([-np.sin(ang_r), np.sin(ang_r), -np.sin(ang_c), np.sin(ang_c)], axis=1)
    return jnp.asarray(cos, F32), jnp.asarray(sin, F32)


def _rope_call(x, cos, sin, name):
    t, w = x.shape
    tr = _pick(t, (512, 256, 128, 64))
    quarter = HEAD_DIM // 4

    def body(x_ref, c_ref, s_ref, o_ref):
        xx = x_ref[...].astype(F32)
        lane = lax.broadcasted_iota(jnp.int32, xx.shape, 1)
        first = (lane % (2 * quarter)) < quarter
        partner = jnp.where(first, pltpu.roll(xx, HEAD_DIM - quarter, 1), pltpu.roll(xx, quarter, 1))
        o_ref[...] = (xx * c_ref[...] + partner * s_ref[...]).astype(o_ref.dtype)

    return pl.pallas_call(
        body, grid=(t // tr, w // HEAD_DIM),
        in_specs=[pl.BlockSpec((tr, HEAD_DIM), lambda i, j: (i, j)), pl.BlockSpec((tr, HEAD_DIM), lambda i, j: (i, 0)),
                  pl.BlockSpec((tr, HEAD_DIM), lambda i, j: (i, 0))],
        out_specs=pl.BlockSpec((tr, HEAD_DIM), lambda i, j: (i, j)),
        out_shape=jax.ShapeDtypeStruct((t, w), x.dtype), name=name, compiler_params=_cparams("parallel", "parallel"))(x, cos, sin)


def make_rope(t, name):
    cos, sin = _rope_tables(t)

    @jax.custom_vjp
    def f(x):
        return _rope_call(x, cos, sin, name + "_fwd")

    def fwd(x):
        return _rope_call(x, cos, sin, name + "_fwd"), None

    def bwd(_, dy):
        return (_rope_call(dy, cos, -sin, name + "_bwd"),)

    f.defvjp(fwd, bwd)
    return f


def _attn_specs(g, span, tk, m, nbh, has_ctx):
    hd = HEAD_DIM
    q_spec = pl.BlockSpec((ATTN_BLOCK, g * hd), lambda h, i, meta: (i, h))
    kv_spec = pl.BlockSpec((tk, hd), lambda h, i, meta: (0, h))
    c_spec = pl.BlockSpec((m, hd), lambda h, i, meta: (0, h))
    if nbh > 1:
        b_spec = pl.BlockSpec((None, None, ATTN_BLOCK, span), lambda h, i, meta: (meta[1, i], h, 0, 0))
    else:
        b_spec = pl.BlockSpec((None, None, ATTN_BLOCK, span), lambda h, i, meta: (meta[1, i], 0, 0, 0))
    sink_spec = pl.BlockSpec(memory_space=pltpu.SMEM)
    return q_spec, kv_spec, c_spec, b_spec, sink_spec


def _attn_probs(qh, ks, kc, bias, sink_val, scale, has_ctx, has_sink):
    s = _dot_nt(qh, ks) * scale + bias
    mx = jnp.max(s, axis=-1, keepdims=True)
    sc = None
    if has_ctx:
        sc = _dot_nt(qh, kc) * scale
        mx = jnp.maximum(mx, jnp.max(sc, axis=-1, keepdims=True))
    if has_sink:
        mx = jnp.maximum(mx, sink_val)
    p = jnp.exp(s - mx)
    l = jnp.sum(p, axis=-1, keepdims=True)
    pc = None
    if has_ctx:
        pc = jnp.exp(sc - mx)
        l = l + jnp.sum(pc, axis=-1, keepdims=True)
    ps = None
    if has_sink:
        ps = jnp.exp(sink_val - mx)
        l = l + ps
    return p, pc, ps, l


def _attn_fwd(q, k, v, kc, vc, bias, sink, meta, g, span, has_ctx, has_sink, name):
    rq, wq = q.shape
    tk, wk = k.shape
    hkv = wk // HEAD_DIM
    m = kc.shape[0]
    nbh = bias.shape[1]
    scale = HEAD_DIM ** -0.5
    nqb = rq // ATTN_BLOCK
    q_spec, kv_spec, c_spec, b_spec, sink_spec = _attn_specs(g, span, tk, m, nbh, has_ctx)

    def body(meta_ref, sink_ref, q_ref, k_ref, v_ref, kc_ref, vc_ref, b_ref, o_ref):
        h = pl.program_id(0)
        i = pl.program_id(1)
        ks0 = pl.multiple_of(meta_ref[0, i], 64)
        ks = k_ref[pl.ds(ks0, span), :]
        vs = v_ref[pl.ds(ks0, span), :]
        bias_t = b_ref[...]
        for hh in range(g):
            qh = q_ref[:, hh * HEAD_DIM:(hh + 1) * HEAD_DIM]
            sink_val = sink_ref[h * g + hh] if has_sink else None
            p, pc, _, l = _attn_probs(qh, ks, kc_ref[...], bias_t, sink_val, scale, has_ctx, has_sink)
            acc = jnp.dot(p.astype(BF16), vs, preferred_element_type=F32)
            if has_ctx:
                acc = acc + jnp.dot(pc.astype(BF16), vc_ref[...], preferred_element_type=F32)
            o_ref[:, hh * HEAD_DIM:(hh + 1) * HEAD_DIM] = (acc / l).astype(o_ref.dtype)

    gs = pltpu.PrefetchScalarGridSpec(
        num_scalar_prefetch=1, grid=(hkv, nqb),
        in_specs=[sink_spec, q_spec, kv_spec, kv_spec, c_spec, c_spec, b_spec], out_specs=q_spec)
    return pl.pallas_call(body, grid_spec=gs, out_shape=jax.ShapeDtypeStruct((rq, wq), BF16), name=name,
                          compiler_params=_cparams("parallel", "arbitrary"))(meta, sink, q, k, v, kc, vc, bias)


def _attn_bwd(q, k, v, kc, vc, bias, sink, meta, o, do, g, span, has_ctx, has_sink, want_dbias, name):
    rq, wq = q.shape
    tk, wk = k.shape
    hkv = wk // HEAD_DIM
    m = kc.shape[0]
    ncase, nbh = bias.shape[:2]
    scale = HEAD_DIM ** -0.5
    nqb = rq // ATTN_BLOCK
    q_spec, kv_spec, c_spec, b_spec, sink_spec = _attn_specs(g, span, tk, m, nbh, has_ctx)
    dsink_spec = pl.BlockSpec((None, 8, HEAD_DIM), lambda h, i, meta: (h, 0, 0))

    def body(meta_ref, sink_ref, q_ref, k_ref, v_ref, kc_ref, vc_ref, b_ref, o_ref, do_ref,
             dq_ref, dk_ref, dv_ref, dkc_ref, dvc_ref, db_ref, dsk_ref):
        h = pl.program_id(0)
        i = pl.program_id(1)

        @pl.when(i == 0)
        def _():
            dk_ref[...] = jnp.zeros_like(dk_ref)
            dv_ref[...] = jnp.zeros_like(dv_ref)
            dkc_ref[...] = jnp.zeros_like(dkc_ref)
            dvc_ref[...] = jnp.zeros_like(dvc_ref)
            dsk_ref[...] = jnp.zeros_like(dsk_ref)

        if want_dbias:
            @pl.when(meta_ref[2, i] == 1)
            def _():
                db_ref[...] = jnp.zeros_like(db_ref)
        else:
            @pl.when(i == 0)
            def _():
                db_ref[...] = jnp.zeros_like(db_ref)

        ks0 = pl.multiple_of(meta_ref[0, i], 64)
        ks = k_ref[pl.ds(ks0, span), :]
        vs = v_ref[pl.ds(ks0, span), :]
        bias_t = b_ref[...]
        dk_acc = jnp.zeros((span, HEAD_DIM), F32)
        dv_acc = jnp.zeros((span, HEAD_DIM), F32)
        for hh in range(g):
            cols = slice(hh * HEAD_DIM, (hh + 1) * HEAD_DIM)
            qh = q_ref[:, cols]
            doh = do_ref[:, cols]
            sink_val = sink_ref[h * g + hh] if has_sink else None
            p, pc, ps, l = _attn_probs(qh, ks, kc_ref[...], bias_t, sink_val, scale, has_ctx, has_sink)
            inv_l = 1.0 / l
            delta = jnp.sum(doh.astype(F32) * o_ref[:, cols].astype(F32), axis=-1, keepdims=True)
            pn = p * inv_l
            ds = pn * (_dot_nt(doh, vs) - delta)
            dsb = ds.astype(BF16)
            dq = jnp.dot(dsb, ks, preferred_element_type=F32)
            dk_acc = dk_acc + _dot_tn(dsb, qh)
            dv_acc = dv_acc + _dot_tn(pn.astype(BF16), doh)
            if want_dbias:
                db_ref[...] += ds
            if has_ctx:
                pcn = pc * inv_l
                dsc = (pcn * (_dot_nt(doh, vc_ref[...]) - delta)).astype(BF16)
                dq = dq + jnp.dot(dsc, kc_ref[...], preferred_element_type=F32)
                dkc_ref[...] += _dot_tn(dsc, qh) * scale
                dvc_ref[...] += _dot_tn(pcn.astype(BF16), doh)
            if has_sink:
                dsv = -jnp.sum(ps * inv_l * delta, axis=0, keepdims=True)
                dsk_ref[hh:hh + 1, :] += jnp.broadcast_to(dsv, (1, HEAD_DIM))
            dq_ref[:, cols] = (dq * scale).astype(dq_ref.dtype)
        dk_ref[pl.ds(ks0, span), :] += dk_acc * scale
        dv_ref[pl.ds(ks0, span), :] += dv_acc

    gs = pltpu.PrefetchScalarGridSpec(
        num_scalar_prefetch=1, grid=(hkv, nqb),
        in_specs=[sink_spec, q_spec, kv_spec, kv_spec, c_spec, c_spec, b_spec, q_spec, q_spec],
        out_specs=[q_spec, kv_spec, kv_spec, c_spec, c_spec, b_spec if want_dbias else dsink_spec, dsink_spec])
    db_sds = jax.ShapeDtypeStruct((ncase, nbh, ATTN_BLOCK, span) if want_dbias else (hkv, 8, HEAD_DIM), F32)
    out_shape = [jax.ShapeDtypeStruct((rq, wq), BF16), jax.ShapeDtypeStruct((tk, wk), F32), jax.ShapeDtypeStruct((tk, wk), F32),
                 jax.ShapeDtypeStruct((m, wk), F32), jax.ShapeDtypeStruct((m, wk), F32), db_sds,
                 jax.ShapeDtypeStruct((hkv, 8, HEAD_DIM), F32)]
    return pl.pallas_call(body, grid_spec=gs, out_shape=out_shape, name=name,
                          compiler_params=_cparams("parallel", "arbitrary"))(meta, sink, q, k, v, kc, vc, bias, o, do)


def make_attention(meta_np, g, span, has_ctx, has_sink, want_dbias, name):
    meta = jnp.asarray(meta_np, jnp.int32)

    @jax.custom_vjp
    def f(q, k, v, kc, vc, bias, sink):
        return _attn_fwd(q, k, v, kc, vc, bias, sink, meta, g, span, has_ctx, has_sink, name + "_fwd")

    def fwd(q, k, v, kc, vc, bias, sink):
        o = _attn_fwd(q, k, v, kc, vc, bias, sink, meta, g, span, has_ctx, has_sink, name + "_fwd")
        return o, (q, k, v, kc, vc, bias, sink, o)

    def bwd(res, do):
        q, k, v, kc, vc, bias, sink, o = res
        dq, dk, dv, dkc, dvc, db, dsk = _attn_bwd(q, k, v, kc, vc, bias, sink, meta, o, do.astype(BF16), g, span,
                                                   has_ctx, has_sink, want_dbias, name + "_bwd")
        dsink = dsk[:, :g, 0].reshape(sink.shape)
        if not want_dbias:
            db = jnp.zeros_like(bias)
        return dq, dk.astype(k.dtype), dv.astype(v.dtype), dkc.astype(kc.dtype), dvc.astype(vc.dtype), db, dsink

    f.defvjp(fwd, bwd)
    return f


def _dedupe_cases(tables):
    cases, idx, first = [], [], []
    for tbl in tables:
        if cases and np.array_equal(cases[-1], tbl):
            idx.append(len(cases) - 1)
            first.append(0)
        else:
            cases.append(tbl)
            idx.append(len(cases) - 1)
            first.append(1)
    return cases, idx, first


def _na_plan(t):
    rows = t // GRID_W
    qr = ATTN_BLOCK // GRID_W
    kr = qr + NA_ROWS - 1
    assert rows >= kr and rows % qr == 0
    span = kr * GRID_W
    kstart, tables = [], []
    qcol = np.tile(np.arange(GRID_W), qr)
    kcol = np.tile(np.arange(GRID_W), kr)
    win_c = np.clip(qcol - NA_COLS // 2, 0, GRID_W - NA_COLS)
    col_ok = (kcol[None, :] >= win_c[:, None]) & (kcol[None, :] < win_c[:, None] + NA_COLS)
    dcol = np.clip(kcol[None, :] - qcol[:, None] + NA_COLS - 1, 0, 2 * NA_COLS - 2)
    for r0 in range(0, rows, qr):
        kb = int(np.clip(r0 - NA_ROWS // 2, 0, rows - kr))
        qrow = r0 + np.repeat(np.arange(qr), GRID_W)
        krow = kb + np.repeat(np.arange(kr), GRID_W)
        win_r = np.clip(qrow - NA_ROWS // 2, 0, rows - NA_ROWS)
        row_ok = (krow[None, :] >= win_r[:, None]) & (krow[None, :] < win_r[:, None] + NA_ROWS)
        drow = np.clip(krow[None, :] - qrow[:, None] + NA_ROWS - 1, 0, 2 * NA_ROWS - 2)
        tables.append(np.stack([row_ok & col_ok, drow, dcol]).astype(np.int32))
        kstart.append(kb * GRID_W)
    cases, idx, first = _dedupe_cases(tables)
    meta = np.array([kstart, idx, first], np.int32)
    return meta, span, np.stack(cases)


def _na_bias(rpb, cases):
    valid, drow, dcol = cases[:, 0], cases[:, 1], cases[:, 2]
    ncase, qn, span = valid.shape
    qr, kr = qn // GRID_W, span // GRID_W
    drow_s = drow.reshape(ncase, qr, GRID_W, kr, GRID_W)[:, :, 0, :, 0]
    dcol_s = dcol[0].reshape(qr, GRID_W, kr, GRID_W)[0, :, 0, :]
    oh_r = jnp.asarray(np.eye(2 * NA_ROWS - 1, dtype=np.float32)[drow_s])
    oh_c = jnp.asarray(np.eye(2 * NA_COLS - 1, dtype=np.float32)[dcol_s])
    tmp = jnp.einsum("hrc,xyc->hrxy", rpb, oh_c, precision=lax.Precision.HIGHEST)
    b = jnp.einsum("nakr,hrxy->nhaxky", oh_r, tmp, precision=lax.Precision.HIGHEST).reshape(ncase, -1, qn, span)
    return jnp.where(jnp.asarray(valid[:, None] > 0), b, NEG_INF)


def _sw_plan(t):
    span = 3 * ATTN_BLOCK
    assert t >= span
    kstart, tables = [], []
    for b in range(t // ATTN_BLOCK):
        ks = int(np.clip((b - 1) * ATTN_BLOCK, 0, t - span))
        qpos = b * ATTN_BLOCK + np.arange(ATTN_BLOCK)
        kpos = ks + np.arange(span)
        ok = np.abs(kpos[None, :] - qpos[:, None]) <= SW_RADIUS
        tables.append(np.where(ok, 0.0, NEG_INF).astype(np.float32))
        kstart.append(ks)
    cases, idx, first = _dedupe_cases(tables)
    return np.array([kstart, idx, first], np.int32), span, np.stack(cases)[:, None]


def _cmul(ar, ai, br, bi):
    return ar * br - ai * bi, ar * bi + ai * br


def _s5_scan_call(x2, win, lam, cin, wout, reverse, n_chunks, name):
    _, ll, d = x2.shape
    nt = d // HEAD_DIM
    sw = 2 * SSM_TILE_GROUPS * SSM_STATE
    hw = sw // 2
    rows = ll // n_chunks
    ic = rows // SCAN_BLOCKS
    full = cin is not None

    down_dir = 0 if reverse else 1

    def chunk_idx(k, dd):
        return jnp.where(dd == down_dir, n_chunks - 1 - k, k)

    def body(*refs):
        if full:
            x_ref, win_ref, lam_ref, cin_ref, wout_ref, s_out, y_out, ub_ref, st_ref = refs
        else:
            x_ref, win_ref, lam_ref, f_out, ub_ref, st_ref = refs
        k = pl.program_id(2)
        down = pl.program_id(0) == down_dir

        @pl.when(k == 0)
        def _():
            st_ref[...] = cin_ref[...] if full else jnp.zeros_like(st_ref)

        ub_ref[...] = jnp.dot(x_ref[...].astype(BF16), win_ref[...], preferred_element_type=F32)
        lr = lam_ref[:, :hw]
        li = lam_ref[:, hw:]

        def step(ii, carry):
            sr, si = carry
            i = jnp.where(down, ic - 1 - ii, ii)
            r0 = pl.multiple_of(i * SCAN_BLOCKS, SCAN_BLOCKS)
            ur = ub_ref[pl.ds(r0, SCAN_BLOCKS), :hw]
            ui = ub_ref[pl.ds(r0, SCAN_BLOCKS), hw:]
            nr = lr * sr - li * si + ur
            ni = lr * si + li * sr + ui
            if full:
                ub_ref[pl.ds(r0, SCAN_BLOCKS), :hw] = nr
                ub_ref[pl.ds(r0, SCAN_BLOCKS), hw:] = ni
            return nr, ni

        sr, si = lax.fori_loop(0, ic, step, (st_ref[:, :hw], st_ref[:, hw:]), unroll=4 if ic % 4 == 0 else 1)
        st_ref[:, :hw] = sr
        st_ref[:, hw:] = si
        if full:
            sb = ub_ref[...].astype(BF16)
            s_out[...] = sb
            y_out[...] = jnp.dot(sb, wout_ref[...], preferred_element_type=F32)
        else:
            @pl.when(k == n_chunks - 1)
            def _():
                f_out[...] = st_ref[...]

    x_spec = pl.BlockSpec((None, rows, HEAD_DIM), lambda dd, t, k: (dd, chunk_idx(k, dd), t))
    win_spec = pl.BlockSpec((None, None, HEAD_DIM, sw), lambda dd, t, k: (dd, t, 0, 0))
    vec_spec = pl.BlockSpec((None, None, SCAN_BLOCKS, sw), lambda dd, t, k: (dd, t, 0, 0))
    scratch = [pltpu.VMEM((rows, sw), F32), pltpu.VMEM((SCAN_BLOCKS, sw), F32)]
    if full:
        in_specs = [x_spec, win_spec, vec_spec, vec_spec, pl.BlockSpec((None, None, sw, HEAD_DIM), lambda dd, t, k: (dd, t, 0, 0))]
        out_specs = [pl.BlockSpec((None, None, rows, sw), lambda dd, t, k: (dd, t, chunk_idx(k, dd), 0)), x_spec]
        out_shape = [jax.ShapeDtypeStruct((2, nt, ll, sw), BF16), jax.ShapeDtypeStruct((2, ll, d), F32)]
        args = (x2, win, lam, cin, wout)
    else:
        in_specs = [x_spec, win_spec, vec_spec]
        out_specs = vec_spec
        out_shape = jax.ShapeDtypeStruct((2, nt, SCAN_BLOCKS, sw), F32)
        args = (x2, win, lam)
    return pl.pallas_call(body, grid=(2, nt, n_chunks), in_specs=in_specs, out_specs=out_specs, out_shape=out_shape,
                          scratch_shapes=scratch, name=name,
                          compiler_params=_cparams("parallel", "parallel", "arbitrary"))(*args)


def _s5_bwd_call(dy2, wrt, lamc, cin, st, u2, wdt, n_chunks, name):
    _, ll, d = dy2.shape
    nt = d // HEAD_DIM
    sw = 2 * SSM_TILE_GROUPS * SSM_STATE
    hw = sw // 2
    rows = ll // n_chunks
    ic = rows // SCAN_BLOCKS

    def chunk_idx(k, dd):
        return jnp.where(dd == 0, n_chunks - 1 - k, k)

    def body(dy_ref, wrt_ref, lam_ref, cin_ref, stb_ref, u_ref, wdt_ref, du_out, dwd_out, dwr_out, dlam_out,
             ds_ref, a_ref, st_ref):
        k = pl.program_id(2)
        down = pl.program_id(0) == 0
        st_ref[...] = stb_ref[...].astype(F32)

        @pl.when(k == 0)
        def _():
            a_ref[...] = cin_ref[...]
            dwd_out[...] = jnp.zeros_like(dwd_out)
            dwr_out[...] = jnp.zeros_like(dwr_out)
            dlam_out[...] = jnp.zeros_like(dlam_out)

        dyb = dy_ref[...].astype(BF16)
        ds_ref[...] = jnp.dot(dyb, wrt_ref[...], preferred_element_type=F32)
        dwr_out[...] += _dot_tn(stb_ref[...], dyb)
        lr = lam_ref[:, :hw]
        li = lam_ref[:, hw:]

        def step(ii, carry):
            ar, ai, gr, gi = carry
            i = jnp.where(down, ic - 1 - ii, ii)
            r0 =pl.multiple_of(i * SCAN_BLOCKS, SCAN_BLOCKS)
            sr = st_ref[pl.ds(r0, SCAN_BLOCKS), :hw]
            si = st_ref[pl.ds(r0, SCAN_BLOCKS), hw:]
            gr = gr + ar * sr + ai * si
            gi = gi + ai * sr - ar * si
            nr = lr * ar - li * ai + ds_ref[pl.ds(r0, SCAN_BLOCKS), :hw]
            ni = lr * ai + li * ar + ds_ref[pl.ds(r0, SCAN_BLOCKS), hw:]
            ds_ref[pl.ds(r0, SCAN_BLOCKS), :hw] = nr
            ds_ref[pl.ds(r0, SCAN_BLOCKS), hw:] = ni
            return nr, ni, gr, gi

        init = (a_ref[:, :hw], a_ref[:, hw:], dlam_out[:, :hw], dlam_out[:, hw:])
        ar, ai, gr, gi = lax.fori_loop(0, ic, step, init, unroll=4 if ic % 4 == 0 else 1)
        a_ref[:, :hw] = ar
        a_ref[:, hw:] = ai
        dlam_out[:, :hw] = gr
        dlam_out[:, hw:] = gi
        ab = ds_ref[...].astype(BF16)
        du_out[...] = jnp.dot(ab, wdt_ref[...], preferred_element_type=F32).astype(du_out.dtype)
        dwd_out[...] += _dot_tn(u_ref[...].astype(BF16), ab)

    x_spec = pl.BlockSpec((None, rows, HEAD_DIM), lambda dd, t, k: (dd, chunk_idx(k, dd), t))
    w_in = pl.BlockSpec((None, None, HEAD_DIM, sw), lambda dd, t, k: (dd, t, 0, 0))
    w_out = pl.BlockSpec((None, None, sw, HEAD_DIM), lambda dd, t, k: (dd, t, 0, 0))
    vec_spec = pl.BlockSpec((None, None, SCAN_BLOCKS, sw), lambda dd, t, k: (dd, t, 0, 0))
    st_spec = pl.BlockSpec((None, None, rows, sw), lambda dd, t, k: (dd, t, chunk_idx(k, dd), 0))
    out_shape = [jax.ShapeDtypeStruct((2, ll, d), u2.dtype), jax.ShapeDtypeStruct((2, nt, HEAD_DIM, sw), F32),
                 jax.ShapeDtypeStruct((2, nt, sw, HEAD_DIM), F32), jax.ShapeDtypeStruct((2, nt, SCAN_BLOCKS, sw), F32)]
    return pl.pallas_call(
        body, grid=(2, nt, n_chunks),
        in_specs=[x_spec, w_in, vec_spec, vec_spec, st_spec, x_spec, w_out],
        out_specs=[x_spec, w_in, w_out, vec_spec], out_shape=out_shape,
        scratch_shapes=[pltpu.VMEM((rows, sw), F32), pltpu.VMEM((SCAN_BLOCKS, sw), F32), pltpu.VMEM((rows, sw), F32)],
        name=name, compiler_params=_cparams("parallel", "parallel", "arbitrary"))(dy2, wrt, lamc, cin, st, u2, wdt)


def _cpow(lr, li, n):
    rr, ri = jnp.ones_like(lr), jnp.zeros_like(li)
    br, bi = lr, li
    while n:
        if n & 1:
            rr, ri = _cmul(rr, ri, br, bi)
        br, bi = _cmul(br, bi, br, bi)
        n >>= 1
    return rr, ri


def _resolve_carries(finals, lam, block_len, down_dir):
    hw = finals.shape[-1] // 2
    pr, pi = _cpow(lam[:, :, 0, :hw], lam[:, :, 0, hw:], block_len)
    fr, fi = finals[..., :hw], finals[..., hw:]

    def walk(order):
        cr, ci = jnp.zeros_like(pr), jnp.zeros_like(pi)
        out = [None] * SCAN_BLOCKS
        for j in order:
            out[j] = jnp.concatenate([cr, ci], axis=-1)
            mr, mi = _cmul(pr, pi, cr, ci)
            cr, ci = mr + fr[:, :, j], mi + fi[:, :, j]
        return jnp.stack(out, axis=2)

    up, down = walk(range(SCAN_BLOCKS)), walk(range(SCAN_BLOCKS - 1, -1, -1))
    return jnp.stack([down[0], up[1]] if down_dir == 0 else [up[0], down[1]])


def _scan_chunks(ll):
    block_len = ll // SCAN_BLOCKS
    for ic in (132, 128, 96, 64, 48, 36, 32, 24, 16, 8):
        if block_len % ic == 0:
            return block_len // ic
    return 1


def make_s5_core(name):
    def run_fwd(u2, lam, wd, wr):
        ll = u2.shape[1]
        nc = _scan_chunks(ll)
        lam8 = jnp.broadcast_to(lam[:, :, None, :], lam.shape[:2] + (SCAN_BLOCKS, lam.shape[-1]))
        wdb = wd.astype(BF16)
        finals = _s5_scan_call(u2, wdb, lam8, None, None, False, nc, name + "_carry")
        cin = _resolve_carries(finals, lam8, ll // SCAN_BLOCKS, 1)
        st, y2 = _s5_scan_call(u2, wdb, lam8, cin, wr.astype(BF16), False, nc, name + "_scan")
        return y2, st, lam8

    @jax.custom_vjp
    def f(u2, lam, wd, wr):
        return run_fwd(u2, lam, wd, wr)[0]

    def fwd(u2, lam, wd, wr):
        y2, st, lam8 = run_fwd(u2, lam, wd, wr)
        return y2, (u2, lam8, wd, wr, st)

    def bwd(res, dy2):
        u2, lam8, wd, wr, st = res
        ll = u2.shape[1]
        nc = _scan_chunks(ll)
        hw = lam8.shape[-1] // 2
        lamc = jnp.concatenate([lam8[..., :hw], -lam8[..., hw:]], axis=-1)
        wrt = jnp.swapaxes(wr, 2, 3).astype(BF16)
        wdt = jnp.swapaxes(wd, 2, 3).astype(BF16)
        finals = _s5_scan_call(dy2, wrt, lamc, None, None, True, nc, name + "_bcarry")
        cin = _resolve_carries(finals, lamc, ll // SCAN_BLOCKS, 0)
        du2, dwd, dwr, dlam8 = _s5_bwd_call(dy2, wrt, lamc, cin, st, u2, wdt, nc, name + "_bscan")
        return du2, jnp.sum(dlam8, axis=2), dwd, dwr

    f.defvjp(fwd, bwd)
    return f


def _s5_params(a_re, a_im, log_dt, b_re, b_im, c_re, c_im):
    dt = jnp.exp(log_dt)[..., None]
    mag = jnp.exp(a_re * dt)
    lam_r, lam_i = mag * jnp.cos(a_im * dt), mag * jnp.sin(a_im * dt)
    den = a_re * a_re + a_im * a_im
    nr = lam_r - 1.0
    coef_r = (nr * a_re + lam_i * a_im) / den
    coef_i = (lam_i * a_re - nr * a_im) / den
    bbar_r = coef_r[..., None] * b_re - coef_i[..., None] * b_im
    bbar_i = coef_r[..., None] * b_im + coef_i[..., None] * b_re
    ndir, g, p = lam_r.shape
    tg = SSM_TILE_GROUPS
    nt = g // tg
    eye = jnp.eye(tg, dtype=F32)

    def tile_vec(v):
        return v.reshape(ndir, nt, tg * p)

    lam = jnp.concatenate([tile_vec(lam_r), tile_vec(lam_i)], axis=-1)

    def drive(b):
        bt = b.reshape(ndir, nt, tg, p, SSM_GROUP)
        return (jnp.swapaxes(bt, 3, 4)[:, :, :, :, None, :] * eye[None, None, :, None, :, None]).reshape(ndir, nt, tg * SSM_GROUP, tg * p)

    wd = jnp.concatenate([drive(bbar_r), drive(bbar_i)], axis=-1)

    def readout(c):
        ct = c.reshape(ndir, nt, tg, SSM_GROUP, p)
        return (jnp.swapaxes(ct, 3, 4)[:, :, :, :, None, :] * eye[None, None, :, None, :, None]).reshape(ndir, nt, tg * p, tg * SSM_GROUP)

    wr = jnp.concatenate([readout(c_re), -readout(c_im)], axis=2)
    return lam, wd, wr


def _to_scan_order(seq):
    ll, d = seq.shape
    return seq.reshape(SCAN_BLOCKS, ll // SCAN_BLOCKS, d).swapaxes(0, 1).reshape(ll, d)


def _from_scan_order(y2):
    ll, d = y2.shape
    return y2.reshape(ll // SCAN_BLOCKS, SCAN_BLOCKS, d).swapaxes(0, 1).reshape(ll, d)


def adamw(w, g, m, v, name):
    shape = w.shape
    cols = shape[-1] if len(shape) > 1 else shape[0]
    w2, g2, m2, v2 = (a.reshape(-1, cols) for a in (w, g, m, v))
    r = w2.shape[0]
    cap = max(1, (1024 * 1024) // (4 * cols))
    tr = r
    for t in (512, 256, 128, 64, 32, 16, 8):
        if t <= cap and r % t == 0:
            tr = t
            break
    c1 = 1.0 / (1.0 - ADAM_B1 ** ADAM_STEP)
    c2 = 1.0 / (1.0 - ADAM_B2 ** ADAM_STEP)

    def body(w_ref, g_ref, m_ref, v_ref, d_ref, mo_ref, vo_ref):
        gg = g_ref[...]
        mn = ADAM_B1 * m_ref[...] + (1.0 - ADAM_B1) * gg
        vn = ADAM_B2 * v_ref[...] + (1.0 - ADAM_B2) * (gg * gg)
        d_ref[...] = -ADAM_LR * ((mn * c1) / (jnp.sqrt(vn * c2) + ADAM_EPS) + ADAM_WD * w_ref[...])
        mo_ref[...] = mn
        vo_ref[...] = vn

    spec = pl.BlockSpec((tr, cols), lambda i: (i, 0))
    sds = jax.ShapeDtypeStruct((r, cols), F32)
    d, mn, vn = pl.pallas_call(body, grid=(r // tr,), in_specs=[spec] * 4, out_specs=[spec] * 3, out_shape=[sds] * 3,
                               name=name, compiler_params=_cparams("parallel"))(w2, g2, m2, v2)
    return d.reshape(shape), mn.reshape(shape), vn.reshape(shape)


def _my_pos():
    return lax.axis_index("x"), lax.axis_index("y"), lax.axis_index("c")


def _flip(pos, f):
    return tuple((1 - p) if b else p for p, b in zip(pos, f))


def _lin(pos):
    return 4 * pos[0] + 2 * pos[1] + pos[2]


def _remote_copies(src_ref, out_ref, send_sems, recv_sems, plan):
    me = _my_pos()
    copies = []
    for k, (f, sfn, dfn) in enumerate(plan):
        peer = _flip(me, f)
        copies.append(pltpu.make_async_remote_copy(
            src_ref=src_ref.at[sfn(me, peer)], dst_ref=out_ref.at[dfn(me, peer)], send_sem=send_sems.at[k],
            recv_sem=recv_sems.at[k], device_id=peer, device_id_type=MESH))
    return copies


def xchg(src, n_out, plan, name, inplace=False):
    piece = src.shape[1:]

    def body(src_ref, out_ref, send_sems, recv_sems):
        me = _my_pos()
        copies = []
        for k, (f, sfn, dfn) in enumerate(plan):
            peer = _flip(me, f)
            s_ref = (out_ref if inplace else src_ref).at[sfn(me, peer)]
            d_ref = out_ref.at[dfn(me, peer)]
            if any(f):
                cp = pltpu.make_async_remote_copy(src_ref=s_ref, dst_ref=d_ref, send_sem=send_sems.at[k],
                                                  recv_sem=recv_sems.at[k], device_id=peer, device_id_type=MESH)
            else:
                cp = pltpu.make_async_copy(s_ref, d_ref, recv_sems.at[k])
            cp.start()
            copies.append((cp, any(f)))
        for cp, remote in copies:
            if remote:
                cp.wait_recv()
            else:
                cp.wait()
        for cp, remote in copies:
            if remote:
                cp.wait_send()

    return pl.pallas_call(
        body, in_specs=[pl.BlockSpec(memory_space=pl.ANY)], out_specs=pl.BlockSpec(memory_space=pl.ANY),
        out_shape=jax.ShapeDtypeStruct((n_out,) + piece, src.dtype),
        scratch_shapes=[pltpu.SemaphoreType.DMA((len(plan),)), pltpu.SemaphoreType.DMA((len(plan),))],
        input_output_aliases={0: 0} if inplace else {}, name=name)(src)


_CHIP_FLIPS = ((1, 0, 0), (0, 1, 0), (1, 1, 0))
_ALL_FLIPS = tuple((a, b, c) for a in (0, 1) for b in (0, 1) for c in (0, 1))[1:]


def all_to_all8(src, name):
    plan = [((0, 0, 0), lambda me, peer: _lin(me), lambda me, peer: _lin(me))]
    plan += [(f, lambda me, peer: _lin(peer), lambda me, peer: _lin(me)) for f in _ALL_FLIPS]
    return xchg(src, N_DEV, plan, name)


def all_gather8(piece, name):
    plan = [((0, 0, 0), lambda me, peer: 0, lambda me, peer: _lin(me))]
    plan += [(f, lambda me, peer: 0, lambda me, peer: _lin(me)) for f in _ALL_FLIPS]
    return xchg(piece[None], N_DEV, plan, name)


def _ag_prepare(shard):
    k, ns = shard.shape
    px, py, _ = _my_pos()
    own = shard.astype(BF16)[None]
    return lax.dynamic_update_slice(jnp.zeros((4, k, ns), BF16), own, (2 * px + py, 0, 0)).reshape(8, k // 2, ns)


def _ag_plan():
    return [(f, lambda me, peer: _lin(me), lambda me, peer: _lin(me)) for f in _CHIP_FLIPS]


def _ag_finish(buf, name):
    plan = [((0, 0, 1), lambda me, peer, f=f: _lin(_flip(me, f)), lambda me, peer, f=f: _lin(_flip(me, f)))
            for f in _CHIP_FLIPS]
    _, kh, ns = buf.shape
    return xchg(buf, 8, plan, name, inplace=True).reshape(4, 2 * kh, ns)


def gather_weight(shard, name):
    k, ns = shard.shape
    buf = _ag_prepare(shard)

    def body(in_ref, out_ref, send_sems, recv_sems):
        me = _my_pos()
        sibling = _flip(me, (0, 0, 1))
        chips = [_flip(me, f) for f in _CHIP_FLIPS]

        def copy(sem, holder, to):
            rows = out_ref.at[4 * holder[0] + 2 * holder[1] + me[2]]
            return pltpu.make_async_remote_copy(src_ref=rows, dst_ref=rows, send_sem=send_sems.at[sem],
                                                recv_sem=recv_sems.at[sem], device_id=to, device_id_type=MESH)

        first = [copy(j, me, chip) for j, chip in enumerate(chips)]
        for cp in first:
            cp.start()
        passed = [copy(3 + j, chip, sibling) for j, chip in enumerate(chips)]
        for j, chip in enumerate(chips):
            copy(j, chip, me).wait_recv()
            passed[j].start()
        for j in range(3):
            passed[j].wait_recv()
        for cp in first + passed:
            cp.wait_send()

    full = pl.pallas_call(
        body, in_specs=[pl.BlockSpec(memory_space=pl.ANY)], out_specs=pl.BlockSpec(memory_space=pl.ANY),
        out_shape=jax.ShapeDtypeStruct(buf.shape, BF16),
        scratch_shapes=[pltpu.SemaphoreType.DMA((6,)), pltpu.SemaphoreType.DMA((6,))],
        input_output_aliases={0: 0}, name=name)(buf)
    return full.reshape(4, k, ns)


_RS_SLOTS = 7


def _sum_halves(g8, l1, c_idx, name):
    _, _, r, cc = g8.shape
    tr = _row_tile(r, r, cc)

    def body(c_ref, a_ref, b_ref, o_ref):
        o_ref[...] = (a_ref[...].astype(F32) + b_ref[...].astype(F32)).astype(o_ref.dtype)

    gs = pltpu.PrefetchScalarGridSpec(
        num_scalar_prefetch=1, grid=(4, r // tr),
        in_specs=[pl.BlockSpec((None, None, tr, cc), lambda s, i, c: (s, c[0], i, 0)),
                  pl.BlockSpec((None, tr, cc), lambda s, i, c: (s, i, 0))],
        out_specs=pl.BlockSpec((None, tr, cc), lambda s, i, c: (s, i, 0)))
    return pl.pallas_call(body, grid_spec=gs, out_shape=jax.ShapeDtypeStruct((_RS_SLOTS, r, cc), BF16), name=name,
                          compiler_params=_cparams("parallel", "parallel"))(c_idx, g8, l1)


def _sum_chips(buf, sc_idx, name):
    _, r, cc = buf.shape
    tr = _row_tile(r, r, cc)

    def body(s_ref, a_ref, b0_ref, b1_ref, b2_ref, o_ref):
        o_ref[...] = ((a_ref[...].astype(F32) + b0_ref[...].astype(F32)) + b1_ref[...].astype(F32)) + b2_ref[...].astype(F32)

    gs = pltpu.PrefetchScalarGridSpec(
        num_scalar_prefetch=1, grid=(r // tr,),
        in_specs=[pl.BlockSpec((None, tr, cc), lambda i, s: (s[0], i, 0))]
        + [pl.BlockSpec((None, tr, cc), lambda i, s, j=j: (4 + j, i, 0)) for j in range(3)],
        out_specs=pl.BlockSpec((None, tr, cc), lambda i, s: (s[1], i, 0)))
    return pl.pallas_call(body, grid_spec=gs, out_shape=jax.ShapeDtypeStruct((2, r, cc), F32), name=name,
                          compiler_params=_cparams("parallel"))(sc_idx, buf, buf, buf, buf)


def _rs_begin(g4, name):
    _, k, ns = g4.shape
    c_idx = jnp.reshape(_my_pos()[2], (1,)).astype(jnp.int32)
    plan1 = [((0, 0, 1), lambda me, peer, s=s: 2 * s + peer[2], lambda me, peer, s=s: s) for s in range(4)]
    l1 = xchg(g4.reshape(8, k // 2, ns), 4, plan1, name + "_rs_d2d")
    buf = _sum_halves(g4.reshape(4, 2, k // 2, ns), l1, c_idx, name + "_rs_sum2")
    plan2 = [(f, lambda me, peer: 2 * peer[0] + peer[1], lambda me, peer, j=j: 4 + j) for j, f in enumerate(_CHIP_FLIPS)]
    return buf, plan2


def _rs_finish(buf, name):
    _, kh, ns = buf.shape
    x, y, c = _my_pos()
    sc_idx = jnp.stack([2 * x + y, c]).astype(jnp.int32)
    halves = _sum_chips(buf, sc_idx, name + "_rs_sum4")
    plan3 = [((0, 0, 1), lambda me, peer: me[2], lambda me, peer: me[2])]
    return xchg(halves, 2, plan3, name + "_rs_swap", inplace=True).reshape(2 * kh, ns)


def reduce_scatter_weight(g4, name):
    buf, plan = _rs_begin(g4, name)
    return _rs_finish(xchg(buf, _RS_SLOTS, plan, name + "_rs_ici", inplace=True), name)


def _sum8(a8, name):
    _, r, cc = a8.shape
    tr = _row_tile(r, r, cc)

    def body(a_ref, o_ref):
        acc = a_ref[0]
        for j in range(1, N_DEV):
            acc = acc + a_ref[j]
        o_ref[...] = acc

    return pl.pallas_call(body, grid=(r // tr,), in_specs=[pl.BlockSpec((N_DEV, tr, cc), lambda i: (0, i, 0))],
                          out_specs=pl.BlockSpec((tr, cc), lambda i: (i, 0)), out_shape=jax.ShapeDtypeStruct((r, cc), F32),
                          name=name, compiler_params=_cparams("parallel"))(a8)


def all_reduce8(flat, name):
    n = flat.shape[0]
    unit = N_DEV * 8 * 128
    npad = -(-n // unit) * unit
    a = jnp.pad(flat, (0, npad - n)).reshape(N_DEV, npad // (N_DEV * 128), 128)
    mine = _sum8(all_to_all8(a, name + "_rs"), name + "_sum")
    return all_gather8(mine, name + "_ag").reshape(npad)[:n]


def _make_split(t, cuts):
    def pieces(qkv):
        out = []
        for rows in (slice(None, t), slice(t, None)):
            out += [qkv[rows, a:b] for a, b in zip(cuts[:-1], cuts[1:])]
        return tuple(out)

    @jax.custom_vjp
    def split(qkv):
        return pieces(qkv)

    def fwd(qkv):
        return pieces(qkv), None

    def bwd(_, g):
        n = len(cuts) - 1
        return (jnp.concatenate([jnp.concatenate(g[:n], axis=1), jnp.concatenate(g[n:], axis=1)], axis=0),)

    split.defvjp(fwd, bwd)
    return split


def _local_loss(x, ctx, target, mods, small, big, pending, shards, dims):
    t, m, d = dims["t"], dims["m"], dims["d"]
    a_w, bq_w, bkv_w = dims["a_w"], dims["bq_w"], dims["bkv_w"]
    z = jnp.concatenate([x, ctx], axis=0)
    big = dict(big)

    def grp(layer, j, n_groups=2):
        return mods[layer, :n_groups, j][:, None, :]

    def lin(kind, out_dtype, name, a, wname, gather=()):
        y, gathered = make_linear(kind, out_dtype, name)(a, big[wname], shards[wname], tuple(pending[n] for n in gather))
        big.update(zip(gather, gathered))
        return y

    def ffn(layer, a, gather_in, gather_out):
        n1, n3, n2 = f"ffn_w1_{layer}", f"ffn_w3_{layer}", f"ffn_w2_{layer}"
        gathering = (tuple(pending[n] for n in gather_in), tuple(pending[n] for n in gather_out))
        w2 = big[n2] if n2 in big else pending[n2]
        y, (got_in, got_out) = make_ffn(f"ffn{layer}", n2 not in big)(a, big[n1], big[n3], w2, shards[n1], shards[n3],
                                                                    shards[n2], gathering)
        big.update(zip(gather_in, got_in))
        big.update(zip(gather_out, got_out))
        return y

    h = make_norm_mod(t, "norm_mix0")(z, small["norm_mix"][0][None], grp(0, 1), grp(0, 0))
    qkv = lin("col", BF16, "attn_in", h, "attn_w_in", ("attn_w_out", "ffn_w1_0"))
    o3, o5 = 3 * a_w, 3 * a_w + bq_w + bkv_w
    cuts = (0, a_w, 2 * a_w, o3, o3 + bq_w, o5, o5 + bkv_w)
    qa, ka, va, qb_u, kb_u, vb, qa_c, ka_c, va_c, qb_c, kb_c, vb_c = _make_split(t, cuts)(qkv)
    qb, kb = make_rope(t, "rope_q")(qb_u), make_rope(t, "rope_k")(kb_u)
    sink = small["attn_sink"][0]
    no_sink = jnp.zeros((a_w // HEAD_DIM,), F32)
    na_meta, na_span, na_cases = _na_plan(t)
    oa = make_attention(na_meta, 1, na_span, True, False, True, "na")(
        qa, ka, va, ka_c, va_c, _na_bias(small["attn_rpb"][0], na_cases), no_sink)
    sw_meta, sw_span, sw_bias = _sw_plan(t)
    grp_b = bq_w // bkv_w
    ob = make_attention(sw_meta, grp_b, sw_span, True, True, False, "swa")(qb, kb, vb, kb_c, vb_c, jnp.asarray(sw_bias), sink)
    c_meta = np.array([[0] * (m // ATTN_BLOCK), [0] * (m // ATTN_BLOCK), [1] + [0] * (m // ATTN_BLOCK - 1)], np.int32)
    zero_bias = jnp.zeros((1, 1, ATTN_BLOCK, m), F32)
    oa_c = make_attention(c_meta, 1, m, False, False, False, "ctx_na")(qa_c, ka_c, va_c, ka_c, va_c, zero_bias, no_sink)
    ob_c = make_attention(c_meta, grp_b, m, False, True, False, "ctx_swa")(qb_c, kb_c, vb_c, kb_c, vb_c, zero_bias, sink)
    o = jnp.concatenate([jnp.concatenate([oa, ob], axis=1), jnp.concatenate([oa_c, ob_c], axis=1)], axis=0)
    y = lin("row", F32, "attn_out", o, "attn_w_out", ("ffn_w3_0",))
    z = make_gated_residual(t, "res_mix0")(z, y, grp(0, 2))
    h = make_norm_mod(t, "norm_ffn0")(z, small["norm_ffn"][0][None], grp(0, 4), grp(0, 3))
    z = make_gated_residual(t, "res_ffn0")(z, ffn(0, h, ("ssm_w_glu", "ffn_w1_1"), ("ffn_w3_1",)), grp(0, 5))

    h = make_norm_mod(t, "norm_mix1")(z, small["norm_mix"][1][None], grp(1, 1), grp(1, 0))
    hx, hc = h[:t], h[t:]
    lam, wd, wr = _s5_params(small["ssm_a_re"][0], small["ssm_a_im"][0], small["ssm_log_dt"][0], small["ssm_b_re"][0],
                             small["ssm_b_im"][0], small["ssm_c_re"][0], small["ssm_c_im"][0])
    u2 = jnp.stack([_to_scan_order(jnp.concatenate([hc, hx], axis=0)), _to_scan_order(h)])
    y2 = make_s5_core("s5")(u2, lam, wd, wr)
    ys = _from_scan_order(y2[0])[m:] + _from_scan_order(y2[1])[:t]
    gl = make_gelu_in("gelu")(hx, ys, small["ssm_d_full"][None])
    zz = lin("col", F32, "glu_w", gl, "ssm_w_glu", ("ffn_w2_1",))
    yx = make_glu("glu")(zz, small["ssm_b_glu_full"][None])
    xs = make_gated_residual(t, "res_mix1")(z[:t], yx, grp(1, 2, 1))
    h = make_norm_mod(t, "norm_ffn1")(xs, small["norm_ffn"][1][None], grp(1, 4, 1), grp(1, 3, 1))
    xs = make_gated_residual(t, "res_ffn1")(xs, ffn(1, h, (), ()), grp(1, 5, 1))
    return make_final_loss("loss_head")(xs, small["norm_final"][None], target)[0, 0]


_WEIGHTS = ['c_ctx', 'ada_w', 'ada_b', 'norm_mix', 'norm_ffn', 'ffn_w1', 'ffn_w3', 'ffn_w2', 'attn_w_in', 'attn_w_out',
            'attn_rpb', 'attn_sink', 'ssm_a_re', 'ssm_a_im', 'ssm_log_dt', 'ssm_b_re', 'ssm_b_im', 'ssm_c_re', 'ssm_c_im',
            'ssm_d', 'ssm_w_glu', 'ssm_b_glu', 'norm_final']
_LOCAL_SMALL = ['norm_mix', 'norm_ffn', 'attn_rpb', 'attn_sink', 'ssm_a_re', 'ssm_a_im', 'ssm_log_dt', 'ssm_b_re',
                'ssm_b_im', 'ssm_c_re', 'ssm_c_im', 'norm_final']
_MOD_ROWS = 16


def _gather_chip_vector(v, name):
    g = all_gather8(v[None], name)
    return g[0::2, 0, :].reshape(-1)


def kernel(x, c, ctx, c_ctx, ada_w, ada_b, norm_mix, norm_ffn, ffn_w1, ffn_w3, ffn_w2, attn_w_in, attn_w_out, attn_rpb, attn_sink, ssm_a_re, ssm_a_im, ssm_log_dt, ssm_b_re, ssm_b_im, ssm_c_re, ssm_c_im, ssm_d, ssm_w_glu, ssm_b_glu, norm_final, loss_target, m_c_ctx, m_ada_w, m_ada_b, m_norm_mix, m_norm_ffn, m_ffn_w1, m_ffn_w3, m_ffn_w2, m_attn_w_in, m_attn_w_out, m_attn_rpb, m_attn_sink, m_ssm_a_re, m_ssm_a_im, m_ssm_log_dt, m_ssm_b_re, m_ssm_b_im, m_ssm_c_re, m_ssm_c_im, m_ssm_d, m_ssm_w_glu, m_ssm_b_glu, m_norm_final, v_c_ctx, v_ada_w, v_ada_b, v_norm_mix, v_norm_ffn, v_ffn_w1, v_ffn_w3, v_ffn_w2, v_attn_w_in, v_attn_w_out, v_attn_rpb, v_attn_sink, v_ssm_a_re, v_ssm_a_im, v_ssm_log_dt, v_ssm_b_re, v_ssm_b_im, v_ssm_c_re, v_ssm_c_im, v_ssm_d, v_ssm_w_glu, v_ssm_b_glu, v_norm_final):
    env = dict(locals())
    w = {n: env[n] for n in _WEIGHTS}
    mom = {n: env["m_" + n] for n in _WEIGHTS}
    var = {n: env["v_" + n] for n in _WEIGHTS}
    _, t, d = x.shape
    m = ctx.shape[1]
    px, py, pc = _my_pos()
    s_me = 2 * px + py
    a_w =attn_rpb.shape[1] * HEAD_DIM
    bq_w = attn_sink.shape[1] * HEAD_DIM
    bkv_w = (4 * attn_w_in.shape[2] - 3 * a_w - bq_w) // 2
    dims = dict(t=t, m=m, d=d, a_w=a_w, bq_w=bq_w, bkv_w=bkv_w)
    n_layers = ada_w.shape[0]
    ada_cols = ada_w.shape[2]

    big = {"attn_w_in": gather_weight(attn_w_in[0], "ag_attn_in")}
    pending = {"attn_w_out": _ag_prepare(attn_w_out[0]), "ssm_w_glu": _ag_prepare(ssm_w_glu[0])}
    for l in range(n_layers):
        pending.update({f"ffn_w1_{l}": _ag_prepare(ffn_w1[l]), f"ffn_w3_{l}": _ag_prepare(ffn_w3[l]),
                        f"ffn_w2_{l}": _ag_prepare(ffn_w2[l])})
    small ={n: w[n] for n in _LOCAL_SMALL}
    small["ssm_d_full"] = _gather_chip_vector(ssm_d[0], "ag_ssm_d")
    small["ssm_b_glu_full"] = _gather_chip_vector(ssm_b_glu[0], "ag_b_glu")

    c_all = all_gather8(c, "ag_c")[:, 0, :]
    cond = jnp.concatenate([c_all, c_ctx[None], jnp.zeros((_MOD_ROWS - N_DEV - 1, d), F32)], axis=0)
    sig = jax.nn.sigmoid(cond)
    silu_c = (cond * sig).astype(BF16)
    ada_wb = ada_w.astype(BF16)
    mods_shard = jnp.stack([mm_nn(silu_c, ada_wb[l][None], "col", F32, f"ada_fwd{l}") for l in range(n_layers)])
    send = jnp.stack([jnp.stack([mods_shard[:, tgt], mods_shard[:, N_DEV]], axis=1).reshape(2 * n_layers, ada_cols)
                      for tgt in range(N_DEV)])
    plan = [((0, 0, 0), lambda me, peer: _lin(me), lambda me, peer: 2 * me[0] + me[1])]
    plan += [(f, lambda me, peer: _lin(peer), lambda me, peer: 2 * me[0] + me[1]) for f in _CHIP_FLIPS]
    got = xchg(send, 4, plan, "mods_xchg")
    mods = got.reshape(4, n_layers, 2, ada_cols).transpose(1, 2, 0, 3).reshape(n_layers, 2, 4 * ada_cols)
    mods = (mods + ada_b[:, None, :]).reshape(n_layers, 2, 6, d)

    shards = {"attn_w_in": attn_w_in[0], "attn_w_out": attn_w_out[0], "ssm_w_glu": ssm_w_glu[0]}
    for l in range(n_layers):
        shards.update({f"ffn_w1_{l}": ffn_w1[l], f"ffn_w3_{l}": ffn_w3[l], f"ffn_w2_{l}": ffn_w2[l]})

    def local(xx, mods_, small_, shards_):
        return _local_loss(xx, ctx[0], loss_target[0], mods_, small_, big, pending, shards_, dims)

    loss_local, vjp = jax.vjp(local, x[0], mods, small, shards)
    g_x, g_mods, g_small, g_shards = vjp(jnp.ones((), F32))
    loss = lax.psum(loss_local, ("x", "y", "c"))
    grads = {"attn_w_in": g_shards["attn_w_in"][None], "attn_w_out": g_shards["attn_w_out"][None],
             "ssm_w_glu": g_shards["ssm_w_glu"][None]}
    for n in ("ffn_w1", "ffn_w3", "ffn_w2"):
        grads[n] = jnp.stack([g_shards[f"{n}_{l}"] for l in range(n_layers)])

    gm = all_gather8(g_mods.reshape(2 * n_layers, 6 * d), "ag_dmods").reshape(N_DEV, n_layers, 2, 6 * d)
    ctx_row = gm[0, :, 1]
    for j in range(1, N_DEV):
        ctx_row = ctx_row + gm[j, :, 1]
    dm16 = jnp.concatenate([gm[:, :, 0].transpose(1, 0, 2), ctx_row[:, None], jnp.zeros((n_layers, _MOD_ROWS - N_DEV - 1, 6 * d), F32)], axis=1)
    grads["ada_b"] = jnp.sum(dm16, axis=1)
    dm_mine = lax.dynamic_slice_in_dim(dm16, s_me * ada_cols, ada_cols, axis=2).astype(BF16)
    grads["ada_w"] = jnp.stack([mm_tn(silu_c, dm_mine[l], (1, d, ada_cols), "col", F32, f"ada_dw{l}")[0] for l in range(n_layers)])
    dsilu = mm_nt(dm_mine[0], ada_wb[0][None], "col", F32, "ada_dc0")
    for l in range(1, n_layers):
        dsilu = dsilu + mm_nt(dm_mine[l], ada_wb[l][None], "col", F32, f"ada_dc{l}")
    dsilu_ctx = 0.5 * dsilu[N_DEV]

    packed = [(n, g_small[n]) for n in _LOCAL_SMALL] + [("ssm_d", g_small["ssm_d_full"]), ("ssm_b_glu", g_small["ssm_b_glu_full"]),
                                                        ("c_ctx", dsilu_ctx)]
    flat = all_reduce8(jnp.concatenate([a.reshape(-1) for _, a in packed]), "ar_small")
    off = 0
    for n, a in packed:
        grads[n] = flat[off:off + a.size].reshape(a.shape)
        off += a.size
    sig_ctx = jax.nn.sigmoid(c_ctx)
    grads["c_ctx"] = grads["c_ctx"] * (sig_ctx * (1.0 + c_ctx * (1.0 - sig_ctx)))
    grads["ssm_d"] = lax.dynamic_slice_in_dim(grads["ssm_d"], s_me * ssm_d.shape[1], ssm_d.shape[1])[None]
    grads["ssm_b_glu"] = lax.dynamic_slice_in_dim(grads["ssm_b_glu"], s_me * ssm_b_glu.shape[1], ssm_b_glu.shape[1])[None]

    delta, new_m, new_v = {}, {}, {}
    for n in _WEIGHTS:
        delta[n], new_m[n], new_v[n] = adamw(w[n], grads[n], mom[n], var[n], "adamw_" + n)
    return (loss, g_x[None], *[grads[n] for n in _WEIGHTS], *[delta[n] for n in _WEIGHTS],
            *[new_m[n] for n in _WEIGHTS], *[new_v[n] for n in _WEIGHTS])
```

```python
import functools
import math

import numpy as np
import jax
import jax.numpy as jnp
from jax import lax
from jax.experimental import pallas as pl
from jax.experimental.pallas import tpu as pltpu

F32 = jnp.float32
BF16 = jnp.bfloat16
MESH = pl.DeviceIdType.MESH

HEAD_DIM = 128
GRID_W = 64
NA_ROWS = 8
NA_COLS = 16
SW_RADIUS = 128
ATTN_BLOCK = 128
ROPE_BASE = 10000.0
SSM_GROUP = 16
SSM_STATE = 64
SSM_TILE_GROUPS = 8
SCAN_BLOCKS = 8
EPS = 1e-6
NEG_INF = -1e30
ADAM_LR, ADAM_B1, ADAM_B2, ADAM_EPS, ADAM_WD, ADAM_STEP = 0.001, 0.9, 0.999, 1e-08, 0.01, 10
VMEM_LIMIT_BYTES = 56 * 1024 * 1024
N_DEV = 8


def _cparams(*sem):
    return pltpu.CompilerParams(dimension_semantics=tuple(sem) if sem else None, vmem_limit_bytes=VMEM_LIMIT_BYTES)


def _pick(n, cands):
    for c in cands:
        if n % c == 0:
            return c
    return n


def _dot_nn(a, b):
    return jnp.dot(a, b, preferred_element_type=F32)


def _dot_nt(a, b):
    return lax.dot_general(a, b, (((1,), (1,)), ((), ())), preferred_element_type=F32)


def _dot_tn(a, b):
    return lax.dot_general(a, b, (((0,), (0,)), ((), ())), preferred_element_type=F32)


def _mm_call(name, grid, ins, in_specs, o_specs, out_sds, acc_shape, step, carry=None):
    nk = grid[2]
    nb = 0 if carry is None else len(carry)
    ni, no = len(ins), len(out_sds)
    assert nk == 1 or no == 1

    def body(*refs):
        in_refs = refs[:ni]
        o_refs = refs[ni + nb:ni + nb + no]
        rest = refs[ni + nb + no + nb:]
        acc_ref = rest[0] if nk > 1 else None
        o_ref = o_refs[0]
        if carry is not None:
            bufs = refs[ni + nb + no:ni + nb + no + nb]
            sems = rest[-2 * nb:]
            ids = [pl.program_id(ax) for ax in range(3)]
            first = (ids[0] == 0) & (ids[1] == 0) & (ids[2] == 0)
            last = (ids[0] == grid[0] - 1) & (ids[1] == grid[1] - 1) & (ids[2] == grid[2] - 1)
            copies = []
            for q, (_, plan) in enumerate(carry):
                copies += _remote_copies(bufs[q], bufs[q], sems[2 * q], sems[2 * q + 1], plan)

            @pl.when(first)
            def _():
                for cp in copies:
                    cp.start()

        kk = pl.program_id(2)
        if nk == 1:
            for ref, val in zip(o_refs, step(*in_refs)):
                ref[...] = val.astype(ref.dtype)
        else:
            @pl.when(kk == 0)
            def _():
                acc_ref[...] = step(*in_refs)[0]

            @pl.when(kk > 0)
            def _():
                acc_ref[...] = step(*in_refs)[0] + acc_ref[...]

            @pl.when(kk == nk - 1)
            def _():
                o_ref[...] = acc_ref[...].astype(o_ref.dtype)

        if carry is not None:
            @pl.when(last)
            def _():
                for cp in copies:
                    cp.wait_recv()
                for cp in copies:
                    cp.wait_send()

    scratch = [pltpu.VMEM(acc_shape, F32)] if nk > 1 else []
    if carry is None:
        return pl.pallas_call(
            body, grid=grid, in_specs=list(in_specs), out_specs=list(o_specs), out_shape=list(out_sds),
            scratch_shapes=scratch, name=name, compiler_params=_cparams("parallel", "parallel", "arbitrary"))(*ins)
    any_spec = pl.BlockSpec(memory_space=pl.ANY)
    for _, plan in carry:
        scratch += [pltpu.SemaphoreType.DMA((len(plan),)), pltpu.SemaphoreType.DMA((len(plan),))]
    return pl.pallas_call(
        body, grid=grid, in_specs=list(in_specs) + [any_spec] * nb, out_specs=list(o_specs) + [any_spec] * nb,
        out_shape=list(out_sds) + [jax.ShapeDtypeStruct(buf.shape, buf.dtype) for buf, _ in carry],
        scratch_shapes=scratch, input_output_aliases={ni + q: no + q for q in range(nb)}, name=name,
        compiler_params=_cparams("arbitrary", "arbitrary", "arbitrary"))(*ins, *[buf for buf, _ in carry])


_ROW_TILES = (768, 512, 256, 128, 64, 32, 16, 8)
_K_TILES = (2048, 1408, 1024, 512, 256, 128)
_CONTRACT_TILES = (1408, 1024, 768, 512, 256, 128)
_FUSED_ROW_TILES = (528, 512, 384, 256, 128, 64, 32, 16, 8)


def mm_nn(a, w3, kind, out_dtype, name, carry=None):
    r = a.shape[0]
    s, d1, d2 = w3.shape
    tm = _pick(r, _ROW_TILES)
    if kind == "col":
        grid = (s, r // tm, 1)
        a_spec = pl.BlockSpec((tm, d1), lambda j, i, k: (i, 0))
        b_spec = pl.BlockSpec((None, d1, d2), lambda j, i, k: (j, 0, 0))
        o_spec = pl.BlockSpec((tm, d2), lambda j, i, k: (i, j))
        n = s * d2

        def step(a_ref, b_ref):
            return (_dot_nn(a_ref[...], b_ref[...]),)
    else:
        tn = d2 if d2 <= 1024 else _pick(d2, (1024, 512, 256, 128))
        grid = (d2 // tn, r // tm, 1)
        a_spec = pl.BlockSpec((tm, s * d1), lambda j, i, k: (i, 0))
        b_spec = pl.BlockSpec((s, d1, tn), lambda j, i, k: (0, 0, j))
        o_spec = pl.BlockSpec((tm, tn), lambda j, i, k: (i, j))
        n = d2

        def step(a_ref, b_ref):
            p = _dot_nn(a_ref[:, :d1], b_ref[0])
            for q in range(1, s):
                p = _dot_nn(a_ref[:, q * d1:(q + 1) * d1], b_ref[q]) + p
            return (p,)
    out = _mm_call(name, grid, [a, w3], [a_spec, b_spec], [o_spec], [jax.ShapeDtypeStruct((r, n), out_dtype)], None, step, carry)
    return out if carry else out[0]


def mm_nt(dy, w3, kind, out_dtype, name, carry=None, addend=None):
    r = dy.shape[0]
    s, d1, d2 = w3.shape
    tm = _pick(r, _ROW_TILES)
    if kind == "col":
        tko = d1 if d1 <= 1024 else _pick(d1, (1024, 512, 256, 128))
        grid = (d1 // tko, r // tm, 1)
        a_spec = pl.BlockSpec((tm, s * d2), lambda j, i, k: (i, 0))
        b_spec = pl.BlockSpec((s, tko, d2), lambda j, i, k: (0, j, 0))
        o_spec = pl.BlockSpec((tm, tko), lambda j, i, k: (i, j))
        kdim = d1

        def step(a_ref, b_ref, *more):
            p = _dot_nt(a_ref[:, :d2], b_ref[0])
            for q in range(1, s):
                p = _dot_nt(a_ref[:, q * d2:(q + 1) * d2], b_ref[q]) + p
            return (p + more[0][...].astype(F32),) if more else (p,)
    else:
        grid = (s, r // tm, 1)
        a_spec = pl.BlockSpec((tm, d2), lambda j, i, k: (i, 0))
        b_spec = pl.BlockSpec((None, d1, d2), lambda j, i, k: (j, 0, 0))
        o_spec = pl.BlockSpec((tm, d1), lambda j, i, k: (i, j))
        kdim = s * d1

        def step(a_ref, b_ref, *more):
            p = _dot_nt(a_ref[...], b_ref[...])
            return (p + more[0][...].astype(F32),) if more else (p,)
    ins, specs = [dy, w3], [a_spec, b_spec]
    if addend is not None:
        ins, specs = ins + [addend], specs + [o_spec]
    out = _mm_call(name, grid, ins, specs, [o_spec], [jax.ShapeDtypeStruct((r, kdim), out_dtype)], None, step, carry)
    return out if carry else out[0]


def mm_tn(a, dy, w_shape, kind, out_dtype, name):
    s, d1, d2 = w_shape
    r = dy.shape[0]
    tr = _pick(r, _CONTRACT_TILES)
    if kind == "col":
        tkk = d1 if d1 <= 1024 else _pick(d1, (1024, 512, 256, 128))
        grid = (s * (d1 // tkk), 1, r // tr)
        nkk = d1 // tkk
        a_spec = pl.BlockSpec((tr, tkk), lambda j, i, k: (k, j % nkk))
        b_spec = pl.BlockSpec((tr, d2), lambda j, i, k: (k, j // nkk))
        o_spec = pl.BlockSpec((None, tkk, d2), lambda j, i, k: (j // nkk, j % nkk, 0))
        acc = (tkk, d2)
    else:
        tn = d2 if d2 <= 1024 else _pick(d2, (1024, 512, 256, 128))
        nn = d2 // tn
        grid = (s * nn, 1, r // tr)
        a_spec = pl.BlockSpec((tr, d1), lambda j, i, k: (k, j // nn))
        b_spec = pl.BlockSpec((tr, tn), lambda j, i, k: (k, j % nn))
        o_spec = pl.BlockSpec((None, d1, tn), lambda j, i, k: (j // nn, 0, j % nn))
        acc = (d1, tn)

    def step(a_ref, b_ref):
        return (_dot_tn(a_ref[...], b_ref[...]),)
    return _mm_call(name, grid, [a, dy], [a_spec, b_spec], [o_spec], [jax.ShapeDtypeStruct(w_shape, out_dtype)], acc, step)[0]


def mm_ffn_in(h, w1, w3, name, carry=None):
    r = h.shape[0]
    s, d1, d2 = w1.shape
    tm = _pick(r, _FUSED_ROW_TILES)
    a_spec = pl.BlockSpec((tm, d1), lambda j, i, k: (i, 0))
    b_spec = pl.BlockSpec((None, d1, d2), lambda j, i, k: (j, 0, 0))
    o_spec = pl.BlockSpec((tm, d2), lambda j, i, k: (i, j))
    sds = jax.ShapeDtypeStruct((r, s * d2), BF16)

    def step(a_ref, b1_ref, b3_ref):
        p1 = _dot_nn(a_ref[...], b1_ref[...])
        p3 = _dot_nn(a_ref[...], b3_ref[...])
        return p1, p3, _silu(p1) * p3

    return _mm_call(name, (s, r // tm, 1), [h, w1, w3], [a_spec, b_spec, b_spec], [o_spec] * 3, [sds] * 3, None, step, carry)


def mm_ffn_back(dy, w2, h1, h3, name, carry=None):
    r = dy.shape[0]
    s, d1, d2 = w2.shape
    tm = _pick(r, _FUSED_ROW_TILES)
    a_spec = pl.BlockSpec((tm, d2), lambda j, i, k: (i, 0))
    b_spec = pl.BlockSpec((None, d1, d2), lambda j, i, k: (j, 0, 0))
    o_spec = pl.BlockSpec((tm, d1), lambda j, i, k: (i, j))
    sds = jax.ShapeDtypeStruct((r, s * d1), BF16)

    def step(a_ref, b_ref, h1_ref, h3_ref):
        g = _dot_nt(a_ref[...], b_ref[...])
        a1 = h1_ref[...].astype(F32)
        sg = jax.nn.sigmoid(a1)
        return g * h3_ref[...].astype(F32) * (sg * (1.0 + a1 * (1.0 - sg))), g * a1 * sg

    return _mm_call(name, (s, r // tm, 1), [dy, w2, h1, h3], [a_spec, b_spec, o_spec, o_spec], [o_spec] * 2, [sds] * 2,
                    None, step, carry)


def make_linear(kind, out_dtype, name):
    def run(a, w3, gathering):
        carry = [(buf, _ag_plan()) for buf in gathering] or None
        out = mm_nn(a, w3, kind, out_dtype, name + "_fwd", carry)
        if not carry:
            return out, ()
        return out[0], _ag_finish_all(out[1:], name)

    @jax.custom_vjp
    def linear(a, w3, w_shard, gathering):
        return run(a, w3, gathering)

    def fwd(a, w3, w_shard, gathering):
        return run(a, w3, gathering), (a, w3, w_shard.shape, len(gathering))

    def bwd(res, cts):
        a, w3, shard_shape, n_gathering = res
        dyb = cts[0].astype(BF16)
        dw = mm_tn(a, dyb, w3.shape, kind, BF16, name + "_dw")
        buf, plan = _rs_begin(dw, name)
        da, buf = mm_nt(dyb, w3, kind, a.dtype, name + "_dx", carry=[(buf, plan)])
        return da, None, _rs_finish(buf, name).reshape(shard_shape), (None,) * n_gathering

    linear.defvjp(fwd, bwd)
    return linear


def _ag_finish_all(bufs, name):
    return tuple(_ag_finish(buf, f"{name}_gathered{q}") for q, buf in enumerate(bufs))


def make_ffn(name, w2_pending):
    def run(h, w1, w3, w2, gathering):
        g_in, g_out = gathering
        if w2_pending:
            g_in = (w2,) + tuple(g_in)
        carry = [(buf, _ag_plan()) for buf in g_in] or None
        out = mm_ffn_in(h, w1, w3, name + "_in", carry)
        h1, h3, act = out[:3]
        done_in = _ag_finish_all(out[3:], name + "_in")
        if w2_pending:
            w2, done_in = done_in[0], done_in[1:]
        carry2 = [(buf, _ag_plan()) for buf in g_out] or None
        y = mm_nn(act, w2, "row", F32, name + "_out", carry2)
        done = (done_in, _ag_finish_all(y[1:], name + "_out") if carry2 else ())
        return (y[0] if carry2 else y), done, (h1, h3, act, w2)

    @jax.custom_vjp
    def ffn(h, w1, w3, w2, s1, s3, s2, gathering):
        return run(h, w1, w3, w2, gathering)[:2]

    def fwd(h, w1, w3, w2, s1, s3, s2, gathering):
        y, done, (h1, h3, act, w2) = run(h, w1, w3, w2, gathering)
        return (y, done), (h, w1, w3, w2, h1, h3, act, s1.shape, s3.shape, s2.shape, tuple(len(g) for g in gathering))

    def bwd(res, cts):
        h, w1, w3, w2, h1, h3, act, shape1, shape3, shape2, n_gathering = res
        dyb = cts[0].astype(BF16)
        buf2, plan2 = _rs_begin(mm_tn(act, dyb, w2.shape, "row", BF16, name + "_dw2"), name + "_w2")
        dh1, dh3, buf2 = mm_ffn_back(dyb, w2, h1, h3, name + "_back", carry=[(buf2, plan2)])
        buf1, plan1 = _rs_begin(mm_tn(h, dh1, w1.shape, "col", BF16, name + "_dw1"), name + "_w1")
        buf3, plan3 = _rs_begin(mm_tn(h, dh3, w3.shape, "col", BF16, name + "_dw3"), name + "_w3")
        dh, buf1 = mm_nt(dh1, w1, "col", h.dtype, name + "_dx1", carry=[(buf1, plan1)])
        dh, buf3 = mm_nt(dh3, w3, "col", h.dtype, name + "_dx3", carry=[(buf3, plan3)], addend=dh)
        grads = [_rs_finish(b, name + n).reshape(sh) for b, n, sh in
                 ((buf1, "_w1", shape1), (buf3, "_w3", shape3), (buf2, "_w2", shape2))]
        return (dh, None, None, None, *grads, tuple((None,) * n for n in n_gathering))

    ffn.defvjp(fwd, bwd)
    return ffn


def _row_tile(r, t0, d):
    cap = max(8, (2 * 1024 * 1024) // (4 * d))
    cands = [t for t in (1024, 512, 256, 128, 64, 32, 16, 8) if t <= cap]
    for t in cands:
        if r % t == 0 and t0 % t == 0:
            return t
    raise ValueError("no row tile")


def _grp_spec(d, nb0):
    return pl.BlockSpec((None, 1, d), lambda i: (i // nb0, 0, 0))


def _norm_mod_fwd(z, g, scale, shift, t0, name):
    r, d = z.shape
    tr = _row_tile(r, t0, d)
    nb0 = t0 // tr

    def body(z_ref, g_ref, sc_ref, sh_ref, o_ref):
        zz = z_ref[...]
        rstd = lax.rsqrt(jnp.mean(zz * zz, axis=-1, keepdims=True) + EPS)
        y = zz * rstd * g_ref[...]
        o_ref[...] = (y * (1.0 + sc_ref[...]) + sh_ref[...]).astype(o_ref.dtype)

    return pl.pallas_call(
        body, grid=(r // tr,),
        in_specs=[pl.BlockSpec((tr, d), lambda i: (i, 0)), pl.BlockSpec((1, d), lambda i: (0, 0)),
                  _grp_spec(d, nb0), _grp_spec(d, nb0)],
        out_specs=pl.BlockSpec((tr, d), lambda i: (i, 0)),
        out_shape=jax.ShapeDtypeStruct((r, d), BF16), name=name, compiler_params=_cparams("parallel"))(z, g, scale, shift)


def _norm_mod_bwd(z, g, scale, dh, t0, name):
    r, d = z.shape
    ng = scale.shape[0]
    tr = _row_tile(r, t0, d)
    nb0 = t0 // tr

    def body(z_ref, g_ref, sc_ref, dh_ref, dz_ref, dg_ref, dsc_ref, dsh_ref):
        i = pl.program_id(0)
        zz = z_ref[...]
        gg = g_ref[...]
        rstd = lax.rsqrt(jnp.mean(zz * zz, axis=-1, keepdims=True) + EPS)
        zhat = zz * rstd
        dhh = dh_ref[...].astype(F32)
        dy = dhh * (1.0 + sc_ref[...])
        dyg = dy * gg
        dz_ref[...] = rstd * (dyg - zhat * jnp.mean(dyg * zhat, axis=-1, keepdims=True))

        @pl.when(i == 0)
        def _():
            dg_ref[...] = jnp.zeros_like(dg_ref)

        @pl.when((i == 0) | (i == nb0))
        def _():
            dsc_ref[...] = jnp.zeros_like(dsc_ref)
            dsh_ref[...] = jnp.zeros_like(dsh_ref)

        dg_ref[...] += jnp.sum(dy * zhat, axis=0, keepdims=True)
        dsc_ref[...] += jnp.sum(dhh * (zhat * gg), axis=0, keepdims=True)
        dsh_ref[...] += jnp.sum(dhh, axis=0, keepdims=True)

    return pl.pallas_call(
        body, grid=(r // tr,),
        in_specs=[pl.BlockSpec((tr, d), lambda i: (i, 0)), pl.BlockSpec((1, d), lambda i: (0, 0)),
                  _grp_spec(d, nb0), pl.BlockSpec((tr, d), lambda i: (i, 0))],
        out_specs=[pl.BlockSpec((tr, d), lambda i: (i, 0)), pl.BlockSpec((1, d), lambda i: (0, 0)),
                   _grp_spec(d, nb0), _grp_spec(d, nb0)],
        out_shape=[jax.ShapeDtypeStruct((r, d), F32), jax.ShapeDtypeStruct((1, d), F32),
                   jax.ShapeDtypeStruct((ng, 1, d), F32), jax.ShapeDtypeStruct((ng, 1, d), F32)],
        name=name, compiler_params=_cparams("arbitrary"))(z, g, scale, dh)


def make_norm_mod(t0, name):
    @jax.custom_vjp
    def f(z, g, scale, shift):
        return _norm_mod_fwd(z, g, scale, shift, t0, name + "_fwd")

    def fwd(z, g, scale, shift):
        return _norm_mod_fwd(z, g, scale, shift, t0, name + "_fwd"), (z, g, scale)

    def bwd(res, dh):
        z, g, scale = res
        dz, dg, dsc, dsh = _norm_mod_bwd(z, g, scale, dh, t0, name + "_bwd")
        return dz, dg, dsc, dsh

    f.defvjp(fwd, bwd)
    return f


def _gated_fwd(z, y, gate, t0, name):
    r, d = z.shape
    tr = _row_tile(r, t0, d)
    nb0 = t0 // tr

    def body(z_ref, y_ref, g_ref, o_ref):
        o_ref[...] = z_ref[...] + g_ref[...] * y_ref[...].astype(F32)

    return pl.pallas_call(
        body, grid=(r // tr,),
        in_specs=[pl.BlockSpec((tr, d), lambda i: (i, 0)), pl.BlockSpec((tr, d), lambda i: (i, 0)), _grp_spec(d, nb0)],
        out_specs=pl.BlockSpec((tr, d), lambda i: (i, 0)),
        out_shape=jax.ShapeDtypeStruct((r, d), F32), name=name, compiler_params=_cparams("parallel"))(z, y, gate)


def _gated_bwd(y, gate, dzn, t0, name):
    r, d = y.shape
    ng = gate.shape[0]
    tr = _row_tile(r, t0, d)
    nb0 = t0 // tr

    def body(y_ref, g_ref, dz_ref, dy_ref, dg_ref):
        i = pl.program_id(0)
        dzz = dz_ref[...]
        dy_ref[...] = (g_ref[...] * dzz).astype(dy_ref.dtype)

        @pl.when((i == 0) | (i == nb0))
        def _():
            dg_ref[...] = jnp.zeros_like(dg_ref)

        dg_ref[...] += jnp.sum(dzz * y_ref[...].astype(F32), axis=0, keepdims=True)

    return pl.pallas_call(
        body, grid=(r // tr,),
        in_specs=[pl.BlockSpec((tr, d), lambda i: (i, 0)), _grp_spec(d, nb0), pl.BlockSpec((tr, d), lambda i: (i, 0))],
        out_specs=[pl.BlockSpec((tr, d), lambda i: (i, 0)), _grp_spec(d, nb0)],
        out_shape=[jax.ShapeDtypeStruct((r, d), y.dtype), jax.ShapeDtypeStruct((ng, 1, d), F32)],
        name=name, compiler_params=_cparams("arbitrary"))(y, gate, dzn)


def make_gated_residual(t0, name):
    @jax.custom_vjp
    def f(z, y, gate):
        return _gated_fwd(z, y, gate, t0, name + "_fwd")

    def fwd(z, y, gate):
        return _gated_fwd(z, y, gate, t0, name + "_fwd"), (y, gate)

    def bwd(res, dzn):
        y, gate = res
        dy, dgate = _gated_bwd(y, gate, dzn, t0, name + "_bwd")
        return dzn, dy, dgate

    f.defvjp(fwd, bwd)
    return f


def make_residual_norm(t0, name):
    def row_spec(tr, d):
        return pl.BlockSpec((tr, d), lambda i: (i, 0))

    def fwd_call(z, y, gate, g, scale, shift):
        r, d = z.shape
        tr = _row_tile(r, t0, d)
        nb0 = t0 // tr

        def body(z_ref, y_ref, gt_ref, g_ref, sc_ref, sh_ref, zn_ref, h_ref):
            zz = z_ref[...] + gt_ref[...] * y_ref[...].astype(F32)
            zn_ref[...] = zz
            rstd = lax.rsqrt(jnp.mean(zz * zz, axis=-1, keepdims=True) + EPS)
            h_ref[...] = (zz * rstd * g_ref[...] * (1.0 + sc_ref[...]) + sh_ref[...]).astype(h_ref.dtype)

        grp = _grp_spec(d, nb0)
        return pl.pallas_call(
            body, grid=(r // tr,),
            in_specs=[row_spec(tr, d), row_spec(tr, d), grp, pl.BlockSpec((1, d), lambda i: (0, 0)), grp, grp],
            out_specs=[row_spec(tr, d), row_spec(tr, d)],
            out_shape=[jax.ShapeDtypeStruct((r, d), F32), jax.ShapeDtypeStruct((r, d), BF16)],
            name=name + "_fwd", compiler_params=_cparams("parallel"))(z, y, gate, g, scale, shift)

    def bwd_call(zn, y, gate, g, scale, dzn, dh):
        r, d = zn.shape
        ng = scale.shape[0]
        tr = _row_tile(r, t0, d)
        nb0 = t0 // tr

        def body(zn_ref, y_ref, gt_ref, g_ref, sc_ref, dzn_ref, dh_ref, dz_ref, dy_ref, dgt_ref, dg_ref, dsc_ref, dsh_ref):
            i = pl.program_id(0)
            zz = zn_ref[...]
            gg = g_ref[...]
            rstd = lax.rsqrt(jnp.mean(zz * zz, axis=-1, keepdims=True) + EPS)
            zhat = zz * rstd
            dhh = dh_ref[...].astype(F32)
            dyn = dhh * (1.0 + sc_ref[...])
            dyg = dyn * gg
            dz = dzn_ref[...] + rstd * (dyg - zhat * jnp.mean(dyg * zhat, axis=-1, keepdims=True))
            dz_ref[...] = dz
            dy_ref[...] = (gt_ref[...] * dz).astype(dy_ref.dtype)

            @pl.when(i == 0)
            def _():
                dg_ref[...] = jnp.zeros_like(dg_ref)

            @pl.when((i == 0) | (i == nb0))
            def _():
                dgt_ref[...] = jnp.zeros_like(dgt_ref)
                dsc_ref[...] = jnp.zeros_like(dsc_ref)
                dsh_ref[...] = jnp.zeros_like(dsh_ref)

            dgt_ref[...] += jnp.sum(dz * y_ref[...].astype(F32), axis=0, keepdims=True)
            dg_ref[...] += jnp.sum(dyn * zhat, axis=0, keepdims=True)
            dsc_ref[...] += jnp.sum(dhh * (zhat * gg), axis=0, keepdims=True)
            dsh_ref[...] += jnp.sum(dhh, axis=0, keepdims=True)

        grp = _grp_spec(d, nb0)
        vec = pl.BlockSpec((1, d), lambda i: (0, 0))
        gsds = jax.ShapeDtypeStruct((ng, 1, d), F32)
        return pl.pallas_call(
            body, grid=(r // tr,),
            in_specs=[row_spec(tr, d), row_spec(tr, d), grp, vec, grp, row_spec(tr, d), row_spec(tr, d)],
            out_specs=[row_spec(tr, d), row_spec(tr, d), grp, vec, grp, grp],
            out_shape=[jax.ShapeDtypeStruct((r, d), F32), jax.ShapeDtypeStruct((r, d), y.dtype), gsds,
                       jax.ShapeDtypeStruct((1, d), F32), gsds, gsds],
            name=name + "_bwd", compiler_params=_cparams("arbitrary"))(zn, y, gate, g, scale, dzn, dh)

    @jax.custom_vjp
    def f(z, y, gate, g, scale, shift):
        return tuple(fwd_call(z, y, gate, g, scale, shift))

    def fwd(z, y, gate, g, scale, shift):
        zn, h = fwd_call(z, y, gate, g, scale, shift)
        return (zn, h), (zn, y, gate, g, scale)

    def bwd(res, cts):
        zn, y, gate, g, scale = res
        dz, dy, dgate, dg, dsc, dsh = bwd_call(zn, y, gate, g, scale, cts[0], cts[1])
        return dz, dy, dgate, dg, dsc, dsh

    f.defvjp(fwd, bwd)
    return f


def _ew_call(name, body, ins, outs_sds, r, widths_in, widths_out, tr, extra_in=(), extra_specs=(), sem="parallel"):
    in_specs = [pl.BlockSpec((tr, w), lambda i: (i, 0)) for w in widths_in] + list(extra_specs)
    out_specs = [pl.BlockSpec((tr, w), lambda i: (i, 0)) if w is not None else pl.BlockSpec(s.shape, lambda i: (0,) * len(s.shape))
                 for w, s in zip(widths_out, outs_sds)]
    return pl.pallas_call(body, grid=(r // tr,), in_specs=in_specs, out_specs=out_specs, out_shape=outs_sds,
                          name=name, compiler_params=_cparams(sem))(*ins, *extra_in)


def _silu(x):
    return x * jax.nn.sigmoid(x)


_GELU_C = math.sqrt(2.0 / math.pi)


def _gelu_and_grad(y):
    inner = _GELU_C * (y + 0.044715 * y * y * y)
    t = jnp.tanh(inner)
    val = 0.5 * y * (1.0 + t)
    grad = 0.5 * (1.0 + t) + 0.5 * y * (1.0 - t * t) * _GELU_C * (1.0 + 3 * 0.044715 * y * y)
    return val, grad


def make_gelu_in(name):
    def fwd_call(u, ys, dsk):
        r, d = u.shape
        tr = _row_tile(r, r, d)

        def body(u_ref, y_ref, d_ref, o_ref):
            y = d_ref[...] * u_ref[...].astype(F32) + y_ref[...]
            o_ref[...] = _gelu_and_grad(y)[0].astype(o_ref.dtype)

        return _ew_call(name + "_fwd", body, (u, ys), [jax.ShapeDtypeStruct((r, d), BF16)], r, (d, d), (d,), tr,
                        extra_in=(dsk,), extra_specs=(pl.BlockSpec((1, d), lambda i: (0, 0)),))[0]

    @jax.custom_vjp
    def f(u, ys, dsk):
        return fwd_call(u, ys, dsk)

    def fwd(u, ys, dsk):
        return fwd_call(u, ys, dsk), (u, ys, dsk)

    def bwd(res, dg):
        u, ys, dsk = res
        r, d = u.shape
        tr = _row_tile(r, r, d)

        def body(u_ref, y_ref, dg_ref, d_ref, du_ref, dy_ref, dd_ref):
            i = pl.program_id(0)
            uu = u_ref[...].astype(F32)
            y = d_ref[...] * uu + y_ref[...]
            dy = dg_ref[...].astype(F32) * _gelu_and_grad(y)[1]
            dy_ref[...] = dy
            du_ref[...] = (d_ref[...] * dy).astype(du_ref.dtype)

            @pl.when(i == 0)
            def _():
                dd_ref[...] = jnp.zeros_like(dd_ref)

            dd_ref[...] += jnp.sum(dy * uu, axis=0, keepdims=True)

        outs = [jax.ShapeDtypeStruct((r, d), u.dtype), jax.ShapeDtypeStruct((r, d), F32), jax.ShapeDtypeStruct((1, d), F32)]
        du, dy, dd = _ew_call(name + "_bwd", body, (u, ys, dg), outs, r, (d, d, d), (d, d, None), tr,
                              extra_in=(dsk,), extra_specs=(pl.BlockSpec((1, d), lambda i: (0, 0)),), sem="arbitrary")
        return du, dy, dd

    f.defvjp(fwd, bwd)
    return f


def make_glu(name):
    def fwd_call(z, b):
        r, d2 = z.shape
        d = d2 // 2
        tr = _row_tile(r, r, d2)

        def body(z_ref, b_ref, o_ref):
            zz = z_ref[...].astype(F32) + b_ref[...]
            o_ref[...] = zz[:, :d] * jax.nn.sigmoid(zz[:, d:])

        return _ew_call(name + "_fwd", body, (z,), [jax.ShapeDtypeStruct((r, d), F32)], r, (d2,), (d,), tr,
                        extra_in=(b,), extra_specs=(pl.BlockSpec((1, d2), lambda i: (0, 0)),))[0]

    @jax.custom_vjp
    def f(z, b):
        return fwd_call(z, b)

    def fwd(z, b):
        return fwd_call(z, b), (z, b)

    def bwd(res, do):
        z, b = res
        r, d2 = z.shape
        d = d2 // 2
        tr = _row_tile(r, r, d2)

        def body(z_ref, do_ref, b_ref, dz_ref, db_ref):
            i = pl.program_id(0)
            zz = z_ref[...].astype(F32) + b_ref[...]
            sg = jax.nn.sigmoid(zz[:, d:])
            g = do_ref[...]
            dza = g * sg
            dzb = g * zz[:, :d] * sg * (1.0 - sg)
            dz_ref[:, :d] = dza.astype(dz_ref.dtype)
            dz_ref[:, d:] = dzb.astype(dz_ref.dtype)

            @pl.when(i == 0)
            def _():
                db_ref[...] = jnp.zeros_like(db_ref)

            db_ref[:, :d] += jnp.sum(dza, axis=0, keepdims=True)
            db_ref[:, d:] += jnp.sum(dzb, axis=0, keepdims=True)

        outs = [jax.ShapeDtypeStruct((r, d2), z.dtype), jax.ShapeDtypeStruct((1, d2), F32)]
        dz, db = _ew_call(name + "_bwd", body, (z, do), outs, r, (d2, d), (d2, None), tr,
                          extra_in=(b,), extra_specs=(pl.BlockSpec((1, d2), lambda i: (0, 0)),), sem="arbitrary")
        return dz, db

    f.defvjp(fwd, bwd)
    return f


def make_final_loss(name):
    def call(z, g, target):
        r, d = z.shape
        tr = _row_tile(r, r, d)

        def body(z_ref, t_ref, g_ref, dz_ref, dg_ref, l_ref):
            i = pl.program_id(0)
            zz = z_ref[...]
            gg = g_ref[...]
            rstd = lax.rsqrt(jnp.mean(zz * zz, axis=-1, keepdims=True) + EPS)
            zhat = zz * rstd
            e = zhat * gg - t_ref[...]
            dy = e * (1.0 / d)
            dyg = dy * gg
            dz_ref[...] = rstd * (dyg - zhat * jnp.mean(dyg * zhat, axis=-1, keepdims=True))

            @pl.when(i == 0)
            def _():
                dg_ref[...] = jnp.zeros_like(dg_ref)
                l_ref[...] = jnp.zeros_like(l_ref)

            dg_ref[...] += jnp.sum(dy * zhat, axis=0, keepdims=True)
            l_ref[...] += jnp.sum(jnp.sum(e * e, axis=1, keepdims=True), axis=0, keepdims=True) * (0.5 / d)

        outs = [jax.ShapeDtypeStruct((r, d), F32), jax.ShapeDtypeStruct((1, d), F32), jax.ShapeDtypeStruct((1, 1), F32)]
        return _ew_call(name, body, (z, target), outs, r, (d, d), (d, None, None), tr,
                        extra_in=(g,), extra_specs=(pl.BlockSpec((1, d), lambda i: (0, 0)),), sem="arbitrary")

    @jax.custom_vjp
    def f(z, g, target):
        return call(z, g, target)[2]

    def fwd(z, g, target):
        dz, dg, loss = call(z, g, target)
        return loss, (dz, dg)

    def bwd(res, dl):
        dz, dg = res
        s = dl[0, 0]
        return dz * s, dg * s, None

    f.defvjp(fwd, bwd)
    return f


def _rope_tables(t):
    quarter = HEAD_DIM // 4
    inv_freq = ROPE_BASE ** (-np.arange(quarter, dtype=np.float64) / quarter)
    pos = np.arange(t)
    ang_r = (pos // GRID_W)[:, None] * inv_freq[None, :]
    ang_c = (pos % GRID_W)[:, None] * inv_freq[None, :]
    cos = np.concatenate([np.cos(ang_r), np.cos(ang_r), np.cos(ang_c), np.cos(ang_c)], axis=1)
    sin = np.concatenate([-np.sin(ang_r), np.sin(ang_r), -np.sin(ang_c), np.sin(ang_c)], axis=1)
    return jnp.asarray(cos, F32), jnp.asarray(sin, F32)


def _rope_call(x, cos, sin, name):
    t, w = x.shape
    tr = _pick(t, (512, 256, 128, 64))
    quarter = HEAD_DIM // 4

    def body(x_ref, c_ref, s_ref, o_ref):
        xx = x_ref[...].astype(F32)
        lane = lax.broadcasted_iota(jnp.int32, xx.shape, 1)
        first = (lane % (2 * quarter)) < quarter
        partner = jnp.where(first, pltpu.roll(xx, HEAD_DIM - quarter, 1), pltpu.roll(xx, quarter, 1))
        o_ref[...] = (xx * c_ref[...] + partner * s_ref[...]).astype(o_ref.dtype)

    return pl.pallas_call(
        body, grid=(t // tr, w // HEAD_DIM),
        in_specs=[pl.BlockSpec((tr, HEAD_DIM), lambda i, j: (i, j)), pl.BlockSpec((tr, HEAD_DIM), lambda i, j: (i, 0)),
                  pl.BlockSpec((tr, HEAD_DIM), lambda i, j: (i, 0))],
        out_specs=pl.BlockSpec((tr, HEAD_DIM), lambda i, j: (i, j)),
        out_shape=jax.ShapeDtypeStruct((t, w), x.dtype), name=name, compiler_params=_cparams("parallel", "parallel"))(x, cos, sin)


def make_rope(t, name):
    cos, sin = _rope_tables(t)

    @jax.custom_vjp
    def f(x):
        return _rope_call(x, cos, sin, name + "_fwd")

    def fwd(x):
        return _rope_call(x, cos, sin, name + "_fwd"), None

    def bwd(_, dy):
        return (_rope_call(dy, cos, -sin, name + "_bwd"),)

    f.defvjp(fwd, bwd)
    return f


def _attn_specs(g, span, tk, m, nbh, has_ctx, hb=1):
    hd = HEAD_DIM
    q_spec = pl.BlockSpec((ATTN_BLOCK, hb * g * hd), lambda h, i, meta: (i, h))
    kv_spec = pl.BlockSpec((tk, hb * hd), lambda h, i, meta: (0, h))
    c_spec = pl.BlockSpec((m, hb * hd), lambda h, i, meta: (0, h))
    if nbh > 1 and hb > 1:
        b_spec = pl.BlockSpec((None, hb, ATTN_BLOCK, span), lambda h, i, meta: (meta[1, i], h, 0, 0))
    elif nbh > 1:
        b_spec = pl.BlockSpec((None, None, ATTN_BLOCK, span), lambda h, i, meta: (meta[1, i], h, 0, 0))
    else:
        b_spec = pl.BlockSpec((None, None, ATTN_BLOCK, span), lambda h, i, meta: (meta[1, i], 0, 0, 0))
    sink_spec = pl.BlockSpec(memory_space=pltpu.SMEM)
    return q_spec, kv_spec, c_spec, b_spec, sink_spec


def _attn_probs(qh, ks, kc, bias, sink_val, scale, has_ctx, has_sink):
    s = _dot_nt(qh, ks) * scale + bias
    mx = jnp.max(s, axis=-1, keepdims=True)
    sc = None
    if has_ctx:
        sc = _dot_nt(qh, kc) * scale
        mx = jnp.maximum(mx, jnp.max(sc, axis=-1, keepdims=True))
    if has_sink:
        mx = jnp.maximum(mx, sink_val)
    p = jnp.exp(s - mx)
    l = jnp.sum(p, axis=-1, keepdims=True)
    pc = None
    if has_ctx:
        pc = jnp.exp(sc - mx)
        l = l + jnp.sum(pc, axis=-1, keepdims=True)
    ps = None
    if has_sink:
        ps = jnp.exp(sink_val - mx)
        l = l + ps
    return p, pc, ps, l


def _attn_fwd(q, k, v, kc, vc, bias, sink, meta, g, span, has_ctx, has_sink, name):
    rq, wq = q.shape
    tk, wk = k.shape
    hkv = wk // HEAD_DIM
    m = kc.shape[0]
    nbh = bias.shape[1]
    scale = HEAD_DIM ** -0.5
    nqb = rq // ATTN_BLOCK
    hb = _pick(hkv, (4, 2, 1)) if g == 1 else 1
    q_spec, kv_spec, c_spec, b_spec, sink_spec = _attn_specs(g, span, tk, m, nbh, has_ctx, hb)
    hd = HEAD_DIM

    def body(meta_ref, sink_ref, q_ref, k_ref, v_ref, kc_ref, vc_ref, b_ref, o_ref):
        h = pl.program_id(0)
        i = pl.program_id(1)
        ks0 = pl.multiple_of(meta_ref[0, i], 64)
        for kl in range(hb):
            kcols = slice(kl * hd, (kl + 1) * hd)
            ks = k_ref[pl.ds(ks0, span), kcols]
            vs = v_ref[pl.ds(ks0, span), kcols]
            bias_t = b_ref[kl] if (hb > 1 and nbh > 1) else b_ref[...]
            for hh in range(g):
                cols = slice((kl * g + hh) * hd, (kl * g + hh + 1) * hd)
                sink_val = sink_ref[(h * hb + kl) * g + hh] if has_sink else None
                p, pc, _, l = _attn_probs(q_ref[:, cols], ks, kc_ref[:, kcols], bias_t, sink_val, scale, has_ctx, has_sink)
                acc = jnp.dot(p.astype(BF16), vs, preferred_element_type=F32)
                if has_ctx:
                    acc = acc + jnp.dot(pc.astype(BF16), vc_ref[:, kcols], preferred_element_type=F32)
                o_ref[:, cols] = (acc / l).astype(o_ref.dtype)

    gs = pltpu.PrefetchScalarGridSpec(
        num_scalar_prefetch=1, grid=(hkv // hb, nqb),
        in_specs=[sink_spec, q_spec, kv_spec, kv_spec, c_spec, c_spec, b_spec], out_specs=q_spec)
    return pl.pallas_call(body, grid_spec=gs, out_shape=jax.ShapeDtypeStruct((rq, wq), BF16), name=name,
                          compiler_params=_cparams("parallel", "arbitrary"))(meta, sink, q, k, v, kc, vc, bias)


def _attn_bwd(q, k, v, kc, vc, bias, sink, meta, o, do, g, span, has_ctx, has_sink, want_dbias, name):
    rq, wq = q.shape
    tk, wk = k.shape
    hkv = wk // HEAD_DIM
    m = kc.shape[0]
    ncase, nbh = bias.shape[:2]
    scale = HEAD_DIM ** -0.5
    nqb = rq // ATTN_BLOCK
    q_spec, kv_spec, c_spec, b_spec, sink_spec = _attn_specs(g, span, tk, m, nbh, has_ctx)
    dsink_spec = pl.BlockSpec((None, 8, HEAD_DIM), lambda h, i, meta: (h, 0, 0))

    def body(meta_ref, sink_ref, q_ref, k_ref, v_ref, kc_ref, vc_ref, b_ref, o_ref, do_ref,
             dq_ref, dk_ref, dv_ref, dkc_ref, dvc_ref, db_ref, dsk_ref):
        h = pl.program_id(0)
        i = pl.program_id(1)

        @pl.when(i == 0)
        def _():
            dk_ref[...] = jnp.zeros_like(dk_ref)
            dv_ref[...] = jnp.zeros_like(dv_ref)
            dkc_ref[...] = jnp.zeros_like(dkc_ref)
            dvc_ref[...] = jnp.zeros_like(dvc_ref)
            dsk_ref[...] = jnp.zeros_like(dsk_ref)

        if want_dbias:
            @pl.when(meta_ref[2, i] == 1)
            def _():
                db_ref[...] = jnp.zeros_like(db_ref)
        else:
            @pl.when(i == 0)
            def _():
                db_ref[...] = jnp.zeros_like(db_ref)

        ks0 = pl.multiple_of(meta_ref[0, i], 64)
        ks = k_ref[pl.ds(ks0, span), :]
        vs = v_ref[pl.ds(ks0, span), :]
        bias_t = b_ref[...]
        dk_acc = jnp.zeros((span, HEAD_DIM), F32)
        dv_acc = jnp.zeros((span, HEAD_DIM), F32)
        for hh in range(g):
            cols = slice(hh * HEAD_DIM, (hh + 1) * HEAD_DIM)
            qh = q_ref[:, cols]
            doh = do_ref[:, cols]
            sink_val = sink_ref[h * g + hh] if has_sink else None
            p, pc, ps, l = _attn_probs(qh, ks, kc_ref[...], bias_t, sink_val, scale, has_ctx, has_sink)
            inv_l = 1.0 / l
            delta = jnp.sum(doh.astype(F32) * o_ref[:, cols].astype(F32), axis=-1, keepdims=True)
            pn = p * inv_l
            ds = pn * (_dot_nt(doh, vs) - delta)
            dsb = ds.astype(BF16)
            dq = jnp.dot(dsb, ks, preferred_element_type=F32)
            dk_acc = dk_acc + _dot_tn(dsb, qh)
            dv_acc = dv_acc + _dot_tn(pn.astype(BF16), doh)
            if want_dbias:
                db_ref[...] += ds
            if has_ctx:
                pcn = pc * inv_l
                dsc = (pcn * (_dot_nt(doh, vc_ref[...]) - delta)).astype(BF16)
                dq = dq + jnp.dot(dsc, kc_ref[...], preferred_element_type=F32)
                dkc_ref[...] += _dot_tn(dsc, qh) * scale
                dvc_ref[...] += _dot_tn(pcn.astype(BF16), doh)
            if has_sink:
                dsv = -jnp.sum(ps * inv_l * delta, axis=0, keepdims=True)
                dsk_ref[hh:hh + 1, :] += jnp.broadcast_to(dsv, (1, HEAD_DIM))
            dq_ref[:, cols] = (dq * scale).astype(dq_ref.dtype)
        dk_ref[pl.ds(ks0, span), :] += dk_acc * scale
        dv_ref[pl.ds(ks0, span), :] += dv_acc

    gs = pltpu.PrefetchScalarGridSpec(
        num_scalar_prefetch=1, grid=(hkv, nqb),
        in_specs=[sink_spec, q_spec, kv_spec, kv_spec, c_spec, c_spec, b_spec, q_spec, q_spec],
        out_specs=[q_spec, kv_spec, kv_spec, c_spec, c_spec, b_spec if want_dbias else dsink_spec, dsink_spec])
    db_sds = jax.ShapeDtypeStruct((ncase, nbh, ATTN_BLOCK, span) if want_dbias else (hkv, 8, HEAD_DIM), F32)
    out_shape = [jax.ShapeDtypeStruct((rq, wq), BF16), jax.ShapeDtypeStruct((tk, wk), F32), jax.ShapeDtypeStruct((tk, wk), F32),
                 jax.ShapeDtypeStruct((m, wk), F32), jax.ShapeDtypeStruct((m, wk), F32), db_sds,
                 jax.ShapeDtypeStruct((hkv, 8, HEAD_DIM), F32)]
    return pl.pallas_call(body, grid_spec=gs, out_shape=out_shape, name=name,
                          compiler_params=_cparams("parallel", "arbitrary"))(meta, sink, q, k, v, kc, vc, bias, o, do)


def make_attention(meta_np, g, span, has_ctx, has_sink, want_dbias, name):
    meta = jnp.asarray(meta_np, jnp.int32)

    @jax.custom_vjp
    def f(q, k, v, kc, vc, bias, sink):
        return _attn_fwd(q, k, v, kc, vc, bias, sink, meta, g, span, has_ctx, has_sink, name + "_fwd")

    def fwd(q, k, v, kc, vc, bias, sink):
        o = _attn_fwd(q, k, v, kc, vc, bias, sink, meta, g, span, has_ctx, has_sink, name + "_fwd")
        return o, (q, k, v, kc, vc, bias, sink, o)

    def bwd(res, do):
        q, k, v, kc, vc, bias, sink, o = res
        dq, dk, dv, dkc, dvc, db, dsk = _attn_bwd(q, k, v, kc, vc, bias, sink, meta, o, do.astype(BF16), g, span,
                                                   has_ctx, has_sink, want_dbias, name + "_bwd")
        dsink = dsk[:, :g, 0].reshape(sink.shape)
        if not want_dbias:
            db = jnp.zeros_like(bias)
        return dq, dk.astype(k.dtype), dv.astype(v.dtype), dkc.astype(kc.dtype), dvc.astype(vc.dtype), db, dsink

    f.defvjp(fwd, bwd)
    return f


def _dedupe_cases(tables):
    cases, idx, first = [], [], []
    for tbl in tables:
        if cases and np.array_equal(cases[-1], tbl):
            idx.append(len(cases) - 1)
            first.append(0)
        else:
            cases.append(tbl)
            idx.append(len(cases) - 1)
            first.append(1)
    return cases, idx, first


def _na_plan(t):
    rows = t // GRID_W
    qr = ATTN_BLOCK // GRID_W
    kr = qr + NA_ROWS - 1
    assert rows >= kr and rows % qr == 0
    span = kr * GRID_W
    kstart, tables = [], []
    qcol = np.tile(np.arange(GRID_W), qr)
    kcol = np.tile(np.arange(GRID_W), kr)
    win_c = np.clip(qcol - NA_COLS // 2, 0, GRID_W - NA_COLS)
    col_ok = (kcol[None, :] >= win_c[:, None]) & (kcol[None, :] < win_c[:, None] + NA_COLS)
    dcol = np.clip(kcol[None, :] - qcol[:, None] + NA_COLS - 1, 0, 2 * NA_COLS - 2)
    for r0 in range(0, rows, qr):
        kb = int(np.clip(r0 - NA_ROWS // 2, 0, rows - kr))
        qrow = r0 + np.repeat(np.arange(qr), GRID_W)
        krow = kb + np.repeat(np.arange(kr), GRID_W)
        win_r = np.clip(qrow - NA_ROWS // 2, 0, rows - NA_ROWS)
        row_ok = (krow[None, :] >= win_r[:, None]) & (krow[None, :] < win_r[:, None] + NA_ROWS)
        drow = np.clip(krow[None, :] - qrow[:, None] + NA_ROWS - 1, 0, 2 * NA_ROWS - 2)
        tables.append(np.stack([row_ok & col_ok, drow, dcol]).astype(np.int32))
        kstart.append(kb * GRID_W)
    cases, idx, first = _dedupe_cases(tables)
    meta = np.array([kstart, idx, first], np.int32)
    return meta, span, np.stack(cases)


def _na_bias(rpb, cases):
    valid, drow, dcol = cases[:, 0], cases[:, 1], cases[:, 2]
    ncase, qn, span = valid.shape
    qr, kr = qn // GRID_W, span // GRID_W
    drow_s = drow.reshape(ncase, qr, GRID_W, kr, GRID_W)[:, :, 0, :, 0]
    dcol_s = dcol[0].reshape(qr, GRID_W, kr, GRID_W)[0, :, 0, :]
    oh_r = jnp.asarray(np.eye(2 * NA_ROWS - 1, dtype=np.float32)[drow_s])
    oh_c = jnp.asarray(np.eye(2 * NA_COLS - 1, dtype=np.float32)[dcol_s])
    tmp = jnp.einsum("hrc,xyc->hrxy", rpb, oh_c, precision=lax.Precision.HIGHEST)
    b = jnp.einsum("nakr,hrxy->nhaxky", oh_r, tmp, precision=lax.Precision.HIGHEST).reshape(ncase, -1, qn, span)
    return jnp.where(jnp.asarray(valid[:, None] > 0), b, NEG_INF)


def _sw_plan(t):
    span = 3 * ATTN_BLOCK
    assert t >= span
    kstart, tables = [], []
    for b in range(t // ATTN_BLOCK):
        ks = int(np.clip((b - 1) * ATTN_BLOCK, 0, t - span))
        qpos = b * ATTN_BLOCK + np.arange(ATTN_BLOCK)
        kpos = ks + np.arange(span)
        ok = np.abs(kpos[None, :] - qpos[:, None]) <= SW_RADIUS
        tables.append(np.where(ok, 0.0, NEG_INF).astype(np.float32))
        kstart.append(ks)
    cases, idx, first = _dedupe_cases(tables)
    return np.array([kstart, idx, first], np.int32), span, np.stack(cases)[:, None]


def _cmul(ar, ai, br, bi):
    return ar * br - ai * bi, ar * bi + ai * br


def _s5_scan_call(x2, win, lam, cin, wout, reverse, n_chunks, name):
    _, ll, d = x2.shape
    nt = d // HEAD_DIM
    sw = 2 * SSM_TILE_GROUPS * SSM_STATE
    hw = sw // 2
    rows = ll // n_chunks
    ic = rows // SCAN_BLOCKS
    full = cin is not None

    down_dir = 0 if reverse else 1

    def chunk_idx(k, dd):
        return jnp.where(dd == down_dir, n_chunks - 1 - k, k)

    def body(*refs):
        if full:
            x_ref, win_ref, lam_ref, cin_ref, wout_ref, s_out, y_out, ub_ref, st_ref = refs
        else:
            x_ref, win_ref, lam_ref, f_out, ub_ref, st_ref = refs
        k = pl.program_id(2)
        down = pl.program_id(0) == down_dir

        @pl.when(k == 0)
        def _():
            st_ref[...] = cin_ref[...] if full else jnp.zeros_like(st_ref)

        ub_ref[...] = jnp.dot(x_ref[...].astype(BF16), win_ref[...], preferred_element_type=F32)
        lr = lam_ref[:, :hw]
        li = lam_ref[:, hw:]

        def step(ii, carry):
            sr, si = carry
            i = jnp.where(down, ic - 1 - ii, ii)
            r0 = pl.multiple_of(i * SCAN_BLOCKS, SCAN_BLOCKS)
            ur = ub_ref[pl.ds(r0, SCAN_BLOCKS), :hw]
            ui = ub_ref[pl.ds(r0, SCAN_BLOCKS), hw:]
            nr = lr * sr - li * si + ur
            ni = lr * si + li * sr + ui
            if full:
                ub_ref[pl.ds(r0, SCAN_BLOCKS), :hw] = nr
                ub_ref[pl.ds(r0, SCAN_BLOCKS), hw:] = ni
            return nr, ni

        sr, si = lax.fori_loop(0, ic, step, (st_ref[:, :hw], st_ref[:, hw:]), unroll=4 if ic % 4 == 0 else 1)
        st_ref[:, :hw] = sr
        st_ref[:, hw:] = si
        if full:
            sb = ub_ref[...].astype(BF16)
            s_out[...] = sb
            y_out[...] = jnp.dot(sb, wout_ref[...], preferred_element_type=F32)
        else:
            @pl.when(k == n_chunks - 1)
            def _():
                f_out[...] = st_ref[...]

    x_spec = pl.BlockSpec((None, rows, HEAD_DIM), lambda dd, t, k: (dd, chunk_idx(k, dd), t))
    win_spec = pl.BlockSpec((None, None, HEAD_DIM, sw), lambda dd, t, k: (dd, t, 0, 0))
    vec_spec = pl.BlockSpec((None, None, SCAN_BLOCKS, sw), lambda dd, t, k: (dd, t, 0, 0))
    scratch = [pltpu.VMEM((rows, sw), F32), pltpu.VMEM((SCAN_BLOCKS, sw), F32)]
    if full:
        in_specs = [x_spec, win_spec, vec_spec, vec_spec, pl.BlockSpec((None, None, sw, HEAD_DIM), lambda dd, t, k: (dd, t, 0, 0))]
        out_specs = [pl.BlockSpec((None, None, rows, sw), lambda dd, t, k: (dd, t, chunk_idx(k, dd), 0)), x_spec]
        out_shape = [jax.ShapeDtypeStruct((2, nt, ll, sw), BF16), jax.ShapeDtypeStruct((2, ll, d), F32)]
        args = (x2, win, lam, cin, wout)
    else:
        in_specs = [x_spec, win_spec, vec_spec]
        out_specs = vec_spec
        out_shape = jax.ShapeDtypeStruct((2, nt, SCAN_BLOCKS, sw), F32)
        args = (x2, win, lam)
    return pl.pallas_call(body, grid=(2, nt, n_chunks), in_specs=in_specs, out_specs=out_specs, out_shape=out_shape,
                          scratch_shapes=scratch, name=name,
                          compiler_params=_cparams("parallel", "parallel", "arbitrary"))(*args)


def _s5_bwd_call(dy2, wrt, lamc, cin, st, u2, wdt, n_chunks, name):
    _, ll, d = dy2.shape
    nt = d // HEAD_DIM
    sw = 2 * SSM_TILE_GROUPS * SSM_STATE
    hw = sw // 2
    rows = ll // n_chunks
    ic = rows // SCAN_BLOCKS

    def chunk_idx(k, dd):
        return jnp.where(dd == 0, n_chunks - 1 - k, k)

    def body(dy_ref, wrt_ref, lam_ref, cin_ref, stb_ref, u_ref, wdt_ref, du_out, dwd_out, dwr_out, dlam_out,
             ds_ref, a_ref, st_ref):
        k = pl.program_id(2)
        down = pl.program_id(0) == 0
        st_ref[...] = stb_ref[...].astype(F32)

        @pl.when(k == 0)
        def _():
            a_ref[...] = cin_ref[...]
            dwd_out[...] = jnp.zeros_like(dwd_out)
            dwr_out[...] = jnp.zeros_like(dwr_out)
            dlam_out[...] = jnp.zeros_like(dlam_out)

        dyb = dy_ref[...].astype(BF16)
        ds_ref[...] = jnp.dot(dyb, wrt_ref[...], preferred_element_type=F32)
        dwr_out[...] += _dot_tn(stb_ref[...], dyb)
        lr = lam_ref[:, :hw]
        li = lam_ref[:, hw:]

        def step(ii, carry):
            ar, ai, gr, gi = carry
            i = jnp.where(down, ic - 1 - ii, ii)
            r0 =pl.multiple_of(i * SCAN_BLOCKS, SCAN_BLOCKS)
            sr = st_ref[pl.ds(r0, SCAN_BLOCKS), :hw]
            si = st_ref[pl.ds(r0, SCAN_BLOCKS), hw:]
            gr = gr + ar * sr + ai * si
            gi = gi + ai * sr - ar * si
            nr = lr * ar - li * ai + ds_ref[pl.ds(r0, SCAN_BLOCKS), :hw]
            ni = lr * ai + li * ar + ds_ref[pl.ds(r0, SCAN_BLOCKS), hw:]
            ds_ref[pl.ds(r0, SCAN_BLOCKS), :hw] = nr
            ds_ref[pl.ds(r0, SCAN_BLOCKS), hw:] = ni
            return nr, ni, gr, gi

        init = (a_ref[:, :hw], a_ref[:, hw:], dlam_out[:, :hw], dlam_out[:, hw:])
        ar, ai, gr, gi = lax.fori_loop(0, ic, step, init, unroll=4 if ic % 4 == 0 else 1)
        a_ref[:, :hw] = ar
        a_ref[:, hw:] = ai
        dlam_out[:, :hw] = gr
        dlam_out[:, hw:] = gi
        ab = ds_ref[...].astype(BF16)
        du_out[...] = jnp.dot(ab, wdt_ref[...], preferred_element_type=F32).astype(du_out.dtype)
        dwd_out[...] += _dot_tn(u_ref[...].astype(BF16), ab)

    x_spec = pl.BlockSpec((None, rows, HEAD_DIM), lambda dd, t, k: (dd, chunk_idx(k, dd), t))
    w_in = pl.BlockSpec((None, None, HEAD_DIM, sw), lambda dd, t, k: (dd, t, 0, 0))
    w_out = pl.BlockSpec((None, None, sw, HEAD_DIM), lambda dd, t, k: (dd, t, 0, 0))
    vec_spec = pl.BlockSpec((None, None, SCAN_BLOCKS, sw), lambda dd, t, k: (dd, t, 0, 0))
    st_spec = pl.BlockSpec((None, None, rows, sw), lambda dd, t, k: (dd, t, chunk_idx(k, dd), 0))
    out_shape = [jax.ShapeDtypeStruct((2, ll, d), u2.dtype), jax.ShapeDtypeStruct((2, nt, HEAD_DIM, sw), F32),
                 jax.ShapeDtypeStruct((2, nt, sw, HEAD_DIM), F32), jax.ShapeDtypeStruct((2, nt, SCAN_BLOCKS, sw), F32)]
    return pl.pallas_call(
        body, grid=(2, nt, n_chunks),
        in_specs=[x_spec, w_in, vec_spec, vec_spec, st_spec, x_spec, w_out],
        out_specs=[x_spec, w_in, w_out, vec_spec], out_shape=out_shape,
        scratch_shapes=[pltpu.VMEM((rows, sw), F32), pltpu.VMEM((SCAN_BLOCKS, sw), F32), pltpu.VMEM((rows, sw), F32)],
        name=name, compiler_params=_cparams("parallel", "parallel", "arbitrary"))(dy2, wrt, lamc, cin, st, u2, wdt)


def _cpow(lr, li, n):
    rr, ri = jnp.ones_like(lr), jnp.zeros_like(li)
    br, bi = lr, li
    while n:
        if n & 1:
            rr, ri = _cmul(rr, ri, br, bi)
        br, bi = _cmul(br, bi, br, bi)
        n >>= 1
    return rr, ri


def _resolve_carries(finals, lam, block_len, down_dir):
    hw = finals.shape[-1] // 2
    pr, pi = _cpow(lam[:, :, 0, :hw], lam[:, :, 0, hw:], block_len)
    fr, fi = finals[..., :hw], finals[..., hw:]

    def walk(order):
        cr, ci = jnp.zeros_like(pr), jnp.zeros_like(pi)
        out = [None] * SCAN_BLOCKS
        for j in order:
            out[j] = jnp.concatenate([cr, ci], axis=-1)
            mr, mi = _cmul(pr, pi, cr, ci)
            cr, ci = mr + fr[:, :, j], mi + fi[:, :, j]
        return jnp.stack(out, axis=2)

    up, down = walk(range(SCAN_BLOCKS)), walk(range(SCAN_BLOCKS - 1, -1, -1))
    return jnp.stack([down[0], up[1]] if down_dir == 0 else [up[0], down[1]])


def _scan_chunks(ll):
    block_len = ll // SCAN_BLOCKS
    for ic in (132, 128, 96, 64, 48, 36, 32, 24, 16, 8):
        if block_len % ic == 0:
            return block_len // ic
    return 1


def make_s5_core(name):
    def run_fwd(u2, lam, wd, wr):
        ll = u2.shape[1]
        nc = _scan_chunks(ll)
        lam8 = jnp.broadcast_to(lam[:, :, None, :], lam.shape[:2] + (SCAN_BLOCKS, lam.shape[-1]))
        wdb = wd.astype(BF16)
        finals = _s5_scan_call(u2, wdb, lam8, None, None, False, nc, name + "_carry")
        cin = _resolve_carries(finals, lam8, ll // SCAN_BLOCKS, 1)
        st, y2 = _s5_scan_call(u2, wdb, lam8, cin, wr.astype(BF16), False, nc, name + "_scan")
        return y2, st, lam8

    @jax.custom_vjp
    def f(u2, lam, wd, wr):
        return run_fwd(u2, lam, wd, wr)[0]

    def fwd(u2, lam, wd, wr):
        y2, st, lam8 = run_fwd(u2, lam, wd, wr)
        return y2, (u2, lam8, wd, wr, st)

    def bwd(res, dy2):
        u2, lam8, wd, wr, st = res
        ll = u2.shape[1]
        nc = _scan_chunks(ll)
        hw = lam8.shape[-1] // 2
        lamc = jnp.concatenate([lam8[..., :hw], -lam8[..., hw:]], axis=-1)
        wrt = jnp.swapaxes(wr, 2, 3).astype(BF16)
        wdt = jnp.swapaxes(wd, 2, 3).astype(BF16)
        finals = _s5_scan_call(dy2, wrt, lamc, None, None, True, nc, name + "_bcarry")
        cin = _resolve_carries(finals, lamc, ll // SCAN_BLOCKS, 0)
        du2, dwd, dwr, dlam8 = _s5_bwd_call(dy2, wrt, lamc, cin, st, u2, wdt, nc, name + "_bscan")
        return du2, jnp.sum(dlam8, axis=2), dwd, dwr

    f.defvjp(fwd, bwd)
    return f


def _s5_params(a_re, a_im, log_dt, b_re, b_im, c_re, c_im):
    dt = jnp.exp(log_dt)[..., None]
    mag = jnp.exp(a_re * dt)
    lam_r, lam_i = mag * jnp.cos(a_im * dt), mag * jnp.sin(a_im * dt)
    den = a_re * a_re + a_im * a_im
    nr = lam_r - 1.0
    coef_r = (nr * a_re + lam_i * a_im) / den
    coef_i = (lam_i * a_re - nr * a_im) / den
    bbar_r = coef_r[..., None] * b_re - coef_i[..., None] * b_im
    bbar_i = coef_r[..., None] * b_im + coef_i[..., None] * b_re
    ndir, g, p = lam_r.shape
    tg = SSM_TILE_GROUPS
    nt = g // tg
    eye = jnp.eye(tg, dtype=F32)

    def tile_vec(v):
        return v.reshape(ndir, nt, tg * p)

    lam = jnp.concatenate([tile_vec(lam_r), tile_vec(lam_i)], axis=-1)

    def drive(b):
        bt = b.reshape(ndir, nt, tg, p, SSM_GROUP)
        return (jnp.swapaxes(bt, 3, 4)[:, :, :, :, None, :] * eye[None, None, :, None, :, None]).reshape(ndir, nt, tg * SSM_GROUP, tg * p)

    wd = jnp.concatenate([drive(bbar_r), drive(bbar_i)], axis=-1)

    def readout(c):
        ct = c.reshape(ndir, nt, tg, SSM_GROUP, p)
        return (jnp.swapaxes(ct, 3, 4)[:, :, :, :, None, :] * eye[None, None, :, None, :, None]).reshape(ndir, nt, tg * p, tg * SSM_GROUP)

    wr = jnp.concatenate([readout(c_re), -readout(c_im)], axis=2)
    return lam, wd, wr


def _to_scan_order(seq):
    ll, d = seq.shape
    return seq.reshape(SCAN_BLOCKS, ll // SCAN_BLOCKS, d).swapaxes(0, 1).reshape(ll, d)


def _from_scan_order(y2):
    ll, d = y2.shape
    return y2.reshape(ll // SCAN_BLOCKS, SCAN_BLOCKS, d).swapaxes(0, 1).reshape(ll, d)


def adamw(w, g, m, v, name):
    shape = w.shape
    cols = shape[-1] if len(shape) > 1 else shape[0]
    w2, g2, m2, v2 = (a.reshape(-1, cols) for a in (w, g, m, v))
    r = w2.shape[0]
    cap = max(1, (1024 * 1024) // (4 * cols))
    tr = r
    for t in (512, 256, 128, 64, 32, 16, 8):
        if t <= cap and r % t == 0:
            tr = t
            break
    c1 = 1.0 / (1.0 - ADAM_B1 ** ADAM_STEP)
    c2 = 1.0 / (1.0 - ADAM_B2 ** ADAM_STEP)

    def body(w_ref, g_ref, m_ref, v_ref, d_ref, mo_ref, vo_ref):
        gg = g_ref[...]
        mn = ADAM_B1 * m_ref[...] + (1.0 - ADAM_B1) * gg
        vn = ADAM_B2 * v_ref[...] + (1.0 - ADAM_B2) * (gg * gg)
        d_ref[...] = -ADAM_LR * ((mn * c1) / (jnp.sqrt(vn * c2) + ADAM_EPS) + ADAM_WD * w_ref[...])
        mo_ref[...] = mn
        vo_ref[...] = vn

    spec = pl.BlockSpec((tr, cols), lambda i: (i, 0))
    sds = jax.ShapeDtypeStruct((r, cols), F32)
    d, mn, vn = pl.pallas_call(body, grid=(r // tr,), in_specs=[spec] * 4, out_specs=[spec] * 3, out_shape=[sds] * 3,
                               name=name, compiler_params=_cparams("parallel"))(w2, g2, m2, v2)
    return d.reshape(shape), mn.reshape(shape), vn.reshape(shape)


def _my_pos():
    return lax.axis_index("x"), lax.axis_index("y"), lax.axis_index("c")


def _flip(pos, f):
    return tuple((1 - p) if b else p for p, b in zip(pos, f))


def _lin(pos):
    return 4 * pos[0] + 2 * pos[1] + pos[2]


def _remote_copies(src_ref, out_ref, send_sems, recv_sems, plan):
    me = _my_pos()
    copies = []
    for k, (f, sfn, dfn) in enumerate(plan):
        peer = _flip(me, f)
        copies.append(pltpu.make_async_remote_copy(
            src_ref=src_ref.at[sfn(me, peer)], dst_ref=out_ref.at[dfn(me, peer)], send_sem=send_sems.at[k],
            recv_sem=recv_sems.at[k], device_id=peer, device_id_type=MESH))
    return copies


def xchg(src, n_out, plan, name, inplace=False):
    piece = src.shape[1:]

    def body(src_ref, out_ref, send_sems, recv_sems):
        me = _my_pos()
        copies = []
        for k, (f, sfn, dfn) in enumerate(plan):
            peer = _flip(me, f)
            s_ref = (out_ref if inplace else src_ref).at[sfn(me, peer)]
            d_ref = out_ref.at[dfn(me, peer)]
            if any(f):
                cp = pltpu.make_async_remote_copy(src_ref=s_ref, dst_ref=d_ref, send_sem=send_sems.at[k],
                                                  recv_sem=recv_sems.at[k], device_id=peer, device_id_type=MESH)
            else:
                cp = pltpu.make_async_copy(s_ref, d_ref, recv_sems.at[k])
            cp.start()
            copies.append((cp, any(f)))
        for cp, remote in copies:
            if remote:
                cp.wait_recv()
            else:
                cp.wait()
        for cp, remote in copies:
            if remote:
                cp.wait_send()

    return pl.pallas_call(
        body, in_specs=[pl.BlockSpec(memory_space=pl.ANY)], out_specs=pl.BlockSpec(memory_space=pl.ANY),
        out_shape=jax.ShapeDtypeStruct((n_out,) + piece, src.dtype),
        scratch_shapes=[pltpu.SemaphoreType.DMA((len(plan),)), pltpu.SemaphoreType.DMA((len(plan),))],
        input_output_aliases={0: 0} if inplace else {}, name=name)(src)


_CHIP_FLIPS = ((1, 0, 0), (0, 1, 0), (1, 1, 0))
_ALL_FLIPS = tuple((a, b, c) for a in (0, 1) for b in (0, 1) for c in (0, 1))[1:]


def all_to_all8(src, name):
    plan = [((0, 0, 0), lambda me, peer: _lin(me), lambda me, peer: _lin(me))]
    plan += [(f, lambda me, peer: _lin(peer), lambda me, peer: _lin(me)) for f in _ALL_FLIPS]
    return xchg(src, N_DEV, plan, name)


def all_gather8(piece, name):
    plan = [((0, 0, 0), lambda me, peer: 0, lambda me, peer: _lin(me))]
    plan += [(f, lambda me, peer: 0, lambda me, peer: _lin(me)) for f in _ALL_FLIPS]
    return xchg(piece[None], N_DEV, plan, name)


def _ag_prepare(shard):
    k, ns = shard.shape
    px, py, _ = _my_pos()
    own = shard.astype(BF16)[None]
    return lax.dynamic_update_slice(jnp.zeros((4, k, ns), BF16), own, (2 * px + py, 0, 0)).reshape(8, k // 2, ns)


def _ag_plan():
    return [(f, lambda me, peer: _lin(me), lambda me, peer: _lin(me)) for f in _CHIP_FLIPS]


def _ag_finish(buf, name):
    plan = [((0, 0, 1), lambda me, peer, f=f: _lin(_flip(me, f)), lambda me, peer, f=f: _lin(_flip(me, f)))
            for f in _CHIP_FLIPS]
    _, kh, ns = buf.shape
    return xchg(buf, 8, plan, name, inplace=True).reshape(4, 2 * kh, ns)


def gather_weight(shard, name):
    k, ns = shard.shape
    buf = _ag_prepare(shard)

    def body(in_ref, out_ref, send_sems, recv_sems):
        me = _my_pos()
        sibling = _flip(me, (0, 0, 1))
        chips = [_flip(me, f) for f in _CHIP_FLIPS]

        def copy(sem, holder, to):
            rows = out_ref.at[4 * holder[0] + 2 * holder[1] + me[2]]
            return pltpu.make_async_remote_copy(src_ref=rows, dst_ref=rows, send_sem=send_sems.at[sem],
                                                recv_sem=recv_sems.at[sem], device_id=to, device_id_type=MESH)

        first = [copy(j, me, chip) for j, chip in enumerate(chips)]
        for cp in first:
            cp.start()
        passed = [copy(3 + j, chip, sibling) for j, chip in enumerate(chips)]
        for j, chip in enumerate(chips):
            copy(j, chip, me).wait_recv()
            passed[j].start()
        for j in range(3):
            passed[j].wait_recv()
        for cp in first + passed:
            cp.wait_send()

    full = pl.pallas_call(
        body, in_specs=[pl.BlockSpec(memory_space=pl.ANY)], out_specs=pl.BlockSpec(memory_space=pl.ANY),
        out_shape=jax.ShapeDtypeStruct(buf.shape, BF16),
        scratch_shapes=[pltpu.SemaphoreType.DMA((6,)), pltpu.SemaphoreType.DMA((6,))],
        input_output_aliases={0: 0}, name=name)(buf)
    return full.reshape(4, k, ns)


_RS_SLOTS = 7


def _sum_halves(g8, l1, c_idx, name):
    _, _, r, cc = g8.shape
    tr = _row_tile(r, r, cc)

    def body(c_ref, a_ref, b_ref, o_ref):
        o_ref[...] = (a_ref[...].astype(F32) + b_ref[...].astype(F32)).astype(o_ref.dtype)

    gs = pltpu.PrefetchScalarGridSpec(
        num_scalar_prefetch=1, grid=(4, r // tr),
        in_specs=[pl.BlockSpec((None, None, tr, cc), lambda s, i, c: (s, c[0], i, 0)),
                  pl.BlockSpec((None, tr, cc), lambda s, i, c: (s, i, 0))],
        out_specs=pl.BlockSpec((None, tr, cc), lambda s, i, c: (s, i, 0)))
    return pl.pallas_call(body, grid_spec=gs, out_shape=jax.ShapeDtypeStruct((_RS_SLOTS, r, cc), BF16), name=name,
                          compiler_params=_cparams("parallel", "parallel"))(c_idx, g8, l1)


def _sum_chips(buf, sc_idx, name):
    _, r, cc = buf.shape
    tr = _row_tile(r, r, cc)

    def body(s_ref, a_ref, b0_ref, b1_ref, b2_ref, o_ref):
        o_ref[...] = ((a_ref[...].astype(F32) + b0_ref[...].astype(F32)) + b1_ref[...].astype(F32)) + b2_ref[...].astype(F32)

    gs = pltpu.PrefetchScalarGridSpec(
        num_scalar_prefetch=1, grid=(r // tr,),
        in_specs=[pl.BlockSpec((None, tr, cc), lambda i, s: (s[0], i, 0))]
        + [pl.BlockSpec((None, tr, cc), lambda i, s, j=j: (4 + j, i, 0)) for j in range(3)],
        out_specs=pl.BlockSpec((None, tr, cc), lambda i, s: (s[1], i, 0)))
    return pl.pallas_call(body, grid_spec=gs, out_shape=jax.ShapeDtypeStruct((2, r, cc), F32), name=name,
                          compiler_params=_cparams("parallel"))(sc_idx, buf, buf, buf, buf)


def _rs_begin(g4, name):
    _, k, ns = g4.shape
    c_idx = jnp.reshape(_my_pos()[2], (1,)).astype(jnp.int32)
    plan1 = [((0, 0, 1), lambda me, peer, s=s: 2 * s + peer[2], lambda me, peer, s=s: s) for s in range(4)]
    l1 = xchg(g4.reshape(8, k // 2, ns), 4, plan1, name + "_rs_d2d")
    buf = _sum_halves(g4.reshape(4, 2, k // 2, ns), l1, c_idx, name + "_rs_sum2")
    plan2 = [(f, lambda me, peer: 2 * peer[0] + peer[1], lambda me, peer, j=j: 4 + j) for j, f in enumerate(_CHIP_FLIPS)]
    return buf, plan2


def _rs_finish(buf, name):
    _, kh, ns = buf.shape
    x, y, c = _my_pos()
    sc_idx = jnp.stack([2 * x + y, c]).astype(jnp.int32)
    halves = _sum_chips(buf, sc_idx, name + "_rs_sum4")
    plan3 = [((0, 0, 1), lambda me, peer: me[2], lambda me, peer: me[2])]
    return xchg(halves, 2, plan3, name + "_rs_swap", inplace=True).reshape(2 * kh, ns)


def reduce_scatter_weight(g4, name):
    buf, plan = _rs_begin(g4, name)
    return _rs_finish(xchg(buf, _RS_SLOTS, plan, name + "_rs_ici", inplace=True), name)


def _sum8(a8, name):
    _, r, cc = a8.shape
    tr = _row_tile(r, r, cc)

    def body(a_ref, o_ref):
        acc = a_ref[0]
        for j in range(1, N_DEV):
            acc = acc + a_ref[j]
        o_ref[...] = acc

    return pl.pallas_call(body, grid=(r // tr,), in_specs=[pl.BlockSpec((N_DEV, tr, cc), lambda i: (0, i, 0))],
                          out_specs=pl.BlockSpec((tr, cc), lambda i: (i, 0)), out_shape=jax.ShapeDtypeStruct((r, cc), F32),
                          name=name, compiler_params=_cparams("parallel"))(a8)


def all_reduce8(flat, name):
    n = flat.shape[0]
    unit = N_DEV * 8 * 128
    npad = -(-n // unit) * unit
    a = jnp.pad(flat, (0, npad - n)).reshape(N_DEV, npad // (N_DEV * 128), 128)
    mine = _sum8(all_to_all8(a, name + "_rs"), name + "_sum")
    return all_gather8(mine, name + "_ag").reshape(npad)[:n]


def _make_split(t, cuts):
    def pieces(qkv):
        out = []
        for rows in (slice(None, t), slice(t, None)):
            out += [qkv[rows, a:b] for a, b in zip(cuts[:-1], cuts[1:])]
        return tuple(out)

    @jax.custom_vjp
    def split(qkv):
        return pieces(qkv)

    def fwd(qkv):
        return pieces(qkv), None

    def bwd(_, g):
        n = len(cuts) - 1
        return (jnp.concatenate([jnp.concatenate(g[:n], axis=1), jnp.concatenate(g[n:], axis=1)], axis=0),)

    split.defvjp(fwd, bwd)
    return split


def _local_loss(x, ctx, target, mods, small, big, pending, shards, dims):
    t, m, d = dims["t"], dims["m"], dims["d"]
    a_w, bq_w, bkv_w = dims["a_w"], dims["bq_w"], dims["bkv_w"]
    z = jnp.concatenate([x, ctx], axis=0)
    big = dict(big)

    def grp(layer, j, n_groups=2):
        return mods[layer, :n_groups, j][:, None, :]

    def lin(kind, out_dtype, name, a, wname, gather=()):
        y, gathered = make_linear(kind, out_dtype, name)(a, big[wname], shards[wname], tuple(pending[n] for n in gather))
        big.update(zip(gather, gathered))
        return y

    def ffn(layer, a, gather_in, gather_out):
        n1, n3, n2 = f"ffn_w1_{layer}", f"ffn_w3_{layer}", f"ffn_w2_{layer}"
        gathering = (tuple(pending[n] for n in gather_in), tuple(pending[n] for n in gather_out))
        w2 = big[n2] if n2 in big else pending[n2]
        y, (got_in, got_out) = make_ffn(f"ffn{layer}", n2 not in big)(a, big[n1], big[n3], w2, shards[n1], shards[n3],
                                                                    shards[n2], gathering)
        big.update(zip(gather_in, got_in))
        big.update(zip(gather_out, got_out))
        return y

    h = make_norm_mod(t, "norm_mix0")(z, small["norm_mix"][0][None], grp(0, 1), grp(0, 0))
    qkv = lin("col", BF16, "attn_in", h, "attn_w_in", ("attn_w_out", "ffn_w1_0"))
    o3, o5 = 3 * a_w, 3 * a_w + bq_w + bkv_w
    cuts = (0, a_w, 2 * a_w, o3, o3 + bq_w, o5, o5 + bkv_w)
    qa, ka, va, qb_u, kb_u, vb, qa_c, ka_c, va_c, qb_c, kb_c, vb_c = _make_split(t, cuts)(qkv)
    qb, kb = make_rope(t, "rope_q")(qb_u), make_rope(t, "rope_k")(kb_u)
    sink = small["attn_sink"][0]
    no_sink = jnp.zeros((a_w // HEAD_DIM,), F32)
    na_meta, na_span, na_cases = _na_plan(t)
    oa = make_attention(na_meta, 1, na_span, True, False, True, "na")(
        qa, ka, va, ka_c, va_c, _na_bias(small["attn_rpb"][0], na_cases), no_sink)
    sw_meta, sw_span, sw_bias = _sw_plan(t)
    grp_b = bq_w // bkv_w
    ob = make_attention(sw_meta, grp_b, sw_span, True, True, False, "swa")(qb, kb, vb, kb_c, vb_c, jnp.asarray(sw_bias), sink)
    c_meta = np.array([[0] * (m // ATTN_BLOCK), [0] * (m // ATTN_BLOCK), [1] + [0] * (m // ATTN_BLOCK - 1)], np.int32)
    zero_bias = jnp.zeros((1, 1, ATTN_BLOCK, m), F32)
    oa_c = make_attention(c_meta, 1, m, False, False, False, "ctx_na")(qa_c, ka_c, va_c, ka_c, va_c, zero_bias, no_sink)
    ob_c = make_attention(c_meta, grp_b, m, False, True, False, "ctx_swa")(qb_c, kb_c, vb_c, kb_c, vb_c, zero_bias, sink)
    o = jnp.concatenate([jnp.concatenate([oa, ob], axis=1), jnp.concatenate([oa_c, ob_c], axis=1)], axis=0)
    y = lin("row", F32, "attn_out", o, "attn_w_out", ("ffn_w3_0",))
    z, h = make_residual_norm(t, "mix0_to_ffn0")(z, y, grp(0, 2), small["norm_ffn"][0][None], grp(0, 4), grp(0, 3))
    y = ffn(0, h, ("ssm_w_glu", "ffn_w1_1"), ("ffn_w3_1",))

    z, h = make_residual_norm(t, "ffn0_to_mix1")(z, y, grp(0, 5), small["norm_mix"][1][None], grp(1, 1), grp(1, 0))
    hx, hc = h[:t], h[t:]
    lam, wd, wr = _s5_params(small["ssm_a_re"][0], small["ssm_a_im"][0], small["ssm_log_dt"][0], small["ssm_b_re"][0],
                             small["ssm_b_im"][0], small["ssm_c_re"][0], small["ssm_c_im"][0])
    u2 = jnp.stack([_to_scan_order(jnp.concatenate([hc, hx], axis=0)), _to_scan_order(h)])
    y2 = make_s5_core("s5")(u2, lam, wd, wr)
    ys = _from_scan_order(y2[0])[m:] + _from_scan_order(y2[1])[:t]
    gl = make_gelu_in("gelu")(hx, ys, small["ssm_d_full"][None])
    zz = lin("col", F32, "glu_w", gl, "ssm_w_glu", ("ffn_w2_1",))
    yx = make_glu("glu")(zz, small["ssm_b_glu_full"][None])
    xs, h = make_residual_norm(t, "mix1_to_ffn1")(z[:t], yx, grp(1, 2, 1), small["norm_ffn"][1][None], grp(1, 4, 1), grp(1, 3, 1))
    xs = make_gated_residual(t, "res_ffn1")(xs, ffn(1, h, (), ()), grp(1, 5, 1))
    return make_final_loss("loss_head")(xs, small["norm_final"][None], target)[0, 0]


_WEIGHTS = ['c_ctx', 'ada_w', 'ada_b', 'norm_mix', 'norm_ffn', 'ffn_w1', 'ffn_w3', 'ffn_w2', 'attn_w_in', 'attn_w_out',
            'attn_rpb', 'attn_sink', 'ssm_a_re', 'ssm_a_im', 'ssm_log_dt', 'ssm_b_re', 'ssm_b_im', 'ssm_c_re', 'ssm_c_im',
            'ssm_d', 'ssm_w_glu', 'ssm_b_glu', 'norm_final']
_LOCAL_SMALL = ['norm_mix', 'norm_ffn', 'attn_rpb', 'attn_sink', 'ssm_a_re', 'ssm_a_im', 'ssm_log_dt', 'ssm_b_re',
                'ssm_b_im', 'ssm_c_re', 'ssm_c_im', 'norm_final']
_MOD_ROWS = 16


def _gather_chip_vector(v, name):
    g = all_gather8(v[None], name)
    return g[0::2, 0, :].reshape(-1)


def kernel(x, c, ctx, c_ctx, ada_w, ada_b, norm_mix, norm_ffn, ffn_w1, ffn_w3, ffn_w2, attn_w_in, attn_w_out, attn_rpb, attn_sink, ssm_a_re, ssm_a_im, ssm_log_dt, ssm_b_re, ssm_b_im, ssm_c_re, ssm_c_im, ssm_d, ssm_w_glu, ssm_b_glu, norm_final, loss_target, m_c_ctx, m_ada_w, m_ada_b, m_norm_mix, m_norm_ffn, m_ffn_w1, m_ffn_w3, m_ffn_w2, m_attn_w_in, m_attn_w_out, m_attn_rpb, m_attn_sink, m_ssm_a_re, m_ssm_a_im, m_ssm_log_dt, m_ssm_b_re, m_ssm_b_im, m_ssm_c_re, m_ssm_c_im, m_ssm_d, m_ssm_w_glu, m_ssm_b_glu, m_norm_final, v_c_ctx, v_ada_w, v_ada_b, v_norm_mix, v_norm_ffn, v_ffn_w1, v_ffn_w3, v_ffn_w2, v_attn_w_in, v_attn_w_out, v_attn_rpb, v_attn_sink, v_ssm_a_re, v_ssm_a_im, v_ssm_log_dt, v_ssm_b_re, v_ssm_b_im, v_ssm_c_re, v_ssm_c_im, v_ssm_d, v_ssm_w_glu, v_ssm_b_glu, v_norm_final):
    env = dict(locals())
    w = {n: env[n] for n in _WEIGHTS}
    mom = {n: env["m_" + n] for n in _WEIGHTS}
    var = {n: env["v_" + n] for n in _WEIGHTS}
    _, t, d = x.shape
    m = ctx.shape[1]
    px, py, pc = _my_pos()
    s_me = 2 * px + py
    a_w =attn_rpb.shape[1] * HEAD_DIM
    bq_w = attn_sink.shape[1] * HEAD_DIM
    bkv_w = (4 * attn_w_in.shape[2] - 3 * a_w - bq_w) // 2
    dims = dict(t=t, m=m, d=d, a_w=a_w, bq_w=bq_w, bkv_w=bkv_w)
    n_layers = ada_w.shape[0]
    ada_cols = ada_w.shape[2]

    big = {"attn_w_in": gather_weight(attn_w_in[0], "ag_attn_in")}
    pending = {"attn_w_out": _ag_prepare(attn_w_out[0]), "ssm_w_glu": _ag_prepare(ssm_w_glu[0])}
    for l in range(n_layers):
        pending.update({f"ffn_w1_{l}": _ag_prepare(ffn_w1[l]), f"ffn_w3_{l}": _ag_prepare(ffn_w3[l]),
                        f"ffn_w2_{l}": _ag_prepare(ffn_w2[l])})
    small ={n: w[n] for n in _LOCAL_SMALL}
    small["ssm_d_full"] = _gather_chip_vector(ssm_d[0], "ag_ssm_d")
    small["ssm_b_glu_full"] = _gather_chip_vector(ssm_b_glu[0], "ag_b_glu")

    c_all = all_gather8(c, "ag_c")[:, 0, :]
    cond = jnp.concatenate([c_all, c_ctx[None], jnp.zeros((_MOD_ROWS - N_DEV - 1, d), F32)], axis=0)
    sig = jax.nn.sigmoid(cond)
    silu_c = (cond * sig).astype(BF16)
    ada_wb = ada_w.astype(BF16)
    mods_shard = jnp.stack([mm_nn(silu_c, ada_wb[l][None], "col", F32, f"ada_fwd{l}") for l in range(n_layers)])
    send = jnp.stack([jnp.stack([mods_shard[:, tgt], mods_shard[:, N_DEV]], axis=1).reshape(2 * n_layers, ada_cols)
                      for tgt in range(N_DEV)])
    plan = [((0, 0, 0), lambda me, peer: _lin(me), lambda me, peer: 2 * me[0] + me[1])]
    plan += [(f, lambda me, peer: _lin(peer), lambda me, peer: 2 * me[0] + me[1]) for f in _CHIP_FLIPS]
    got = xchg(send, 4, plan, "mods_xchg")
    mods = got.reshape(4, n_layers, 2, ada_cols).transpose(1, 2, 0, 3).reshape(n_layers, 2, 4 * ada_cols)
    mods = (mods + ada_b[:, None, :]).reshape(n_layers, 2, 6, d)

    shards = {"attn_w_in": attn_w_in[0], "attn_w_out": attn_w_out[0], "ssm_w_glu": ssm_w_glu[0]}
    for l in range(n_layers):
        shards.update({f"ffn_w1_{l}": ffn_w1[l], f"ffn_w3_{l}": ffn_w3[l], f"ffn_w2_{l}": ffn_w2[l]})

    def local(xx, mods_, small_, shards_):
        return _local_loss(xx, ctx[0], loss_target[0], mods_, small_, big, pending, shards_, dims)

    loss_local, vjp = jax.vjp(local, x[0], mods, small, shards)
    g_x, g_mods, g_small, g_shards = vjp(jnp.ones((), F32))
    loss = lax.psum(loss_local, ("x", "y", "c"))
    grads = {"attn_w_in": g_shards["attn_w_in"][None], "attn_w_out": g_shards["attn_w_out"][None],
             "ssm_w_glu": g_shards["ssm_w_glu"][None]}
    for n in ("ffn_w1", "ffn_w3", "ffn_w2"):
        grads[n] = jnp.stack([g_shards[f"{n}_{l}"] for l in range(n_layers)])

    gm = all_gather8(g_mods.reshape(2 * n_layers, 6 * d), "ag_dmods").reshape(N_DEV, n_layers, 2, 6 * d)
    ctx_row = gm[0, :, 1]
    for j in range(1, N_DEV):
        ctx_row = ctx_row + gm[j, :, 1]
    dm16 = jnp.concatenate([gm[:, :, 0].transpose(1, 0, 2), ctx_row[:, None], jnp.zeros((n_layers, _MOD_ROWS - N_DEV - 1, 6 * d), F32)], axis=1)
    grads["ada_b"] = jnp.sum(dm16, axis=1)
    dm_mine = lax.dynamic_slice_in_dim(dm16, s_me * ada_cols, ada_cols, axis=2).astype(BF16)
    grads["ada_w"] = jnp.stack([mm_tn(silu_c, dm_mine[l], (1, d, ada_cols), "col", F32, f"ada_dw{l}")[0] for l in range(n_layers)])
    dsilu = mm_nt(dm_mine[0], ada_wb[0][None], "col", F32, "ada_dc0")
    for l in range(1, n_layers):
        dsilu = dsilu + mm_nt(dm_mine[l], ada_wb[l][None], "col", F32, f"ada_dc{l}")
    dsilu_ctx = 0.5 * dsilu[N_DEV]

    packed = [(n, g_small[n]) for n in _LOCAL_SMALL] + [("ssm_d", g_small["ssm_d_full"]), ("ssm_b_glu", g_small["ssm_b_glu_full"]),
                                                        ("c_ctx", dsilu_ctx)]
    flat = all_reduce8(jnp.concatenate([a.reshape(-1) for _, a in packed]), "ar_small")
    off = 0
    for n, a in packed:
        grads[n] = flat[off:off + a.size].reshape(a.shape)
        off += a.size
    sig_ctx = jax.nn.sigmoid(c_ctx)
    grads["c_ctx"] = grads["c_ctx"] * (sig_ctx * (1.0 + c_ctx * (1.0 - sig_ctx)))
    grads["ssm_d"] = lax.dynamic_slice_in_dim(grads["ssm_d"], s_me * ssm_d.shape[1], ssm_d.shape[1])[None]
    grads["ssm_b_glu"] = lax.dynamic_slice_in_dim(grads["ssm_b_glu"], s_me * ssm_b_glu.shape[1], ssm_b_glu.shape[1])[None]

    delta, new_m, new_v = {}, {}, {}
    for n in _WEIGHTS:
        delta[n], new_m[n], new_v[n] = adamw(w[n], grads[n], mom[n], var[n], "adamw_" + n)
    return (loss, g_x[None], *[grads[n] for n in _WEIGHTS], *[delta[n] for n in _WEIGHTS],
            *[new_m[n] for n in _WEIGHTS], *[new_v[n] for n in _WEIGHTS])
```

```python
import functools
import math

import numpy as np
import jax
import jax.numpy as jnp
from jax import lax
from jax.experimental import pallas as pl
from jax.experimental.pallas import tpu as pltpu

F32 = jnp.float32
BF16 = jnp.bfloat16
MESH = pl.DeviceIdType.MESH

HEAD_DIM = 128
GRID_W = 64
NA_ROWS = 8
NA_COLS = 16
SW_RADIUS = 128
ATTN_BLOCK = 128
ROPE_BASE = 10000.0
SSM_GROUP = 16
SSM_STATE = 64
SSM_TILE_GROUPS = 8
SCAN_BLOCKS = 8
EPS = 1e-6
NEG_INF = -1e30
ADAM_LR, ADAM_B1, ADAM_B2, ADAM_EPS, ADAM_WD, ADAM_STEP = 0.001, 0.9, 0.999, 1e-08, 0.01, 10
VMEM_LIMIT_BYTES = 56 * 1024 * 1024
N_DEV = 8


def _cparams(*sem):
    return pltpu.CompilerParams(dimension_semantics=tuple(sem) if sem else None, vmem_limit_bytes=VMEM_LIMIT_BYTES)


def _pick(n, cands):
    for c in cands:
        if n % c == 0:
            return c
    return n


def _dot_nn(a, b):
    return jnp.dot(a, b, preferred_element_type=F32)


def _dot_nt(a, b):
    return lax.dot_general(a, b, (((1,), (1,)), ((), ())), preferred_element_type=F32)


def _dot_tn(a, b):
    return lax.dot_general(a, b, (((0,), (0,)), ((), ())), preferred_element_type=F32)


def _mm_call(name, grid, ins, in_specs, o_specs, out_sds, acc_shape, step, carry=None):
    nk = grid[2]
    nb = 0 if carry is None else len(carry)
    ni, no = len(ins), len(out_sds)
    assert nk == 1 or no == 1

    def body(*refs):
        in_refs = refs[:ni]
        o_refs = refs[ni + nb:ni + nb + no]
        rest = refs[ni + nb + no + nb:]
        acc_ref = rest[0] if nk > 1 else None
        o_ref = o_refs[0]
        if carry is not None:
            bufs = refs[ni + nb + no:ni + nb + no + nb]
            sems = rest[-2 * nb:]
            ids = [pl.program_id(ax) for ax in range(3)]
            first = (ids[0] == 0) & (ids[1] == 0) & (ids[2] == 0)
            last = (ids[0] == grid[0] - 1) & (ids[1] == grid[1] - 1) & (ids[2] == grid[2] - 1)
            copies = []
            for q, (_, plan) in enumerate(carry):
                copies += _remote_copies(bufs[q], bufs[q], sems[2 * q], sems[2 * q + 1], plan)

            @pl.when(first)
            def _():
                for cp in copies:
                    cp.start()

        kk = pl.program_id(2)
        if nk == 1:
            for ref, val in zip(o_refs, step(*in_refs)):
                ref[...] = val.astype(ref.dtype)
        else:
            @pl.when(kk == 0)
            def _():
                acc_ref[...] = step(*in_refs)[0]

            @pl.when(kk > 0)
            def _():
                acc_ref[...] = step(*in_refs)[0] + acc_ref[...]

            @pl.when(kk == nk - 1)
            def _():
                o_ref[...] = acc_ref[...].astype(o_ref.dtype)

        if carry is not None:
            @pl.when(last)
            def _():
                for cp in copies:
                    cp.wait_recv()
                for cp in copies:
                    cp.wait_send()

    scratch = [pltpu.VMEM(acc_shape, F32)] if nk > 1 else []
    if carry is None:
        return pl.pallas_call(
            body, grid=grid, in_specs=list(in_specs), out_specs=list(o_specs), out_shape=list(out_sds),
            scratch_shapes=scratch, name=name, compiler_params=_cparams("parallel", "parallel", "arbitrary"))(*ins)
    any_spec = pl.BlockSpec(memory_space=pl.ANY)
    for _, plan in carry:
        scratch += [pltpu.SemaphoreType.DMA((len(plan),)), pltpu.SemaphoreType.DMA((len(plan),))]
    return pl.pallas_call(
        body, grid=grid, in_specs=list(in_specs) + [any_spec] * nb, out_specs=list(o_specs) + [any_spec] * nb,
        out_shape=list(out_sds) + [jax.ShapeDtypeStruct(buf.shape, buf.dtype) for buf, _ in carry],
        scratch_shapes=scratch, input_output_aliases={ni + q: no + q for q in range(nb)}, name=name,
        compiler_params=_cparams("arbitrary", "arbitrary", "arbitrary"))(*ins, *[buf for buf, _ in carry])


_ROW_TILES = (768, 512, 256, 128, 64, 32, 16, 8)
_K_TILES = (2048, 1408, 1024, 512, 256, 128)
_CONTRACT_TILES = (1408, 1024, 768, 512, 256, 128)
_WEIGHT_BLOCK_BYTES = 12 * 1024 * 1024
_FUSED_ROW_TILES =(528, 512, 384, 256, 128, 64, 32, 16, 8)


def mm_nn(a, w3, kind, out_dtype, name, carry=None):
    r = a.shape[0]
    s, d1, d2 = w3.shape
    tm = _pick(r, _ROW_TILES)
    if kind == "col":
        tn = d2 if d1 * d2 * w3.dtype.itemsize <= _WEIGHT_BLOCK_BYTES else _pick(d2, (1024, 512, 256, 128))
        nn = d2 // tn
        grid = (s * nn, r // tm, 1)
        a_spec = pl.BlockSpec((tm, d1), lambda j, i, k: (i, 0))
        b_spec = pl.BlockSpec((None, d1, tn), lambda j, i, k: (j // nn, 0, j % nn))
        o_spec = pl.BlockSpec((tm, tn), lambda j, i, k: (i, j))
        n = s * d2

        def step(a_ref, b_ref):
            return (_dot_nn(a_ref[...], b_ref[...].astype(a_ref.dtype)),)
    else:
        tn = d2 if d2 <= 1024 else _pick(d2, (1024, 512, 256, 128))
        grid = (d2 // tn, r // tm, 1)
        a_spec = pl.BlockSpec((tm, s * d1), lambda j, i, k: (i, 0))
        b_spec = pl.BlockSpec((s, d1, tn), lambda j, i, k: (0, 0, j))
        o_spec = pl.BlockSpec((tm, tn), lambda j, i, k: (i, j))
        n = d2

        def step(a_ref, b_ref):
            p = _dot_nn(a_ref[:, :d1], b_ref[0])
            for q in range(1, s):
                p = _dot_nn(a_ref[:, q * d1:(q + 1) * d1], b_ref[q]) + p
            return (p,)
    out = _mm_call(name, grid, [a, w3], [a_spec, b_spec], [o_spec], [jax.ShapeDtypeStruct((r, n), out_dtype)], None, step, carry)
    return out if carry else out[0]


def mm_nt(dy, w3, kind, out_dtype, name, carry=None, addend=None):
    r = dy.shape[0]
    s, d1, d2 = w3.shape
    tm = _pick(r, _ROW_TILES)
    if kind == "col":
        fits = [c for c in (1024, 512, 256, 128) if d1 % c == 0 and s * c * d2 * w3.dtype.itemsize <= _WEIGHT_BLOCK_BYTES]
        tko = d1 if d1 <= 1024 and s * d1 * d2 * w3.dtype.itemsize <= _WEIGHT_BLOCK_BYTES else fits[0]
        grid = (d1 // tko, r // tm, 1)
        a_spec = pl.BlockSpec((tm, s * d2), lambda j, i, k: (i, 0))
        b_spec = pl.BlockSpec((s, tko, d2), lambda j, i, k: (0, j, 0))
        o_spec = pl.BlockSpec((tm, tko), lambda j, i, k: (i, j))
        kdim = d1

        def step(a_ref, b_ref, *more):
            p = _dot_nt(a_ref[:, :d2], b_ref[0].astype(a_ref.dtype))
            for q in range(1, s):
                p = _dot_nt(a_ref[:, q * d2:(q + 1) * d2], b_ref[q].astype(a_ref.dtype)) + p
            return (p + more[0][...].astype(F32),) if more else (p,)
    else:
        grid = (s, r // tm, 1)
        a_spec = pl.BlockSpec((tm, d2), lambda j, i, k: (i, 0))
        b_spec = pl.BlockSpec((None, d1, d2), lambda j, i, k: (j, 0, 0))
        o_spec = pl.BlockSpec((tm, d1), lambda j, i, k: (i, j))
        kdim = s * d1

        def step(a_ref, b_ref, *more):
            p = _dot_nt(a_ref[...], b_ref[...])
            return (p + more[0][...].astype(F32),) if more else (p,)
    ins, specs = [dy, w3], [a_spec, b_spec]
    if addend is not None:
        ins, specs = ins + [addend], specs + [o_spec]
    out = _mm_call(name, grid, ins, specs, [o_spec], [jax.ShapeDtypeStruct((r, kdim), out_dtype)], None, step, carry)
    return out if carry else out[0]


def mm_tn(a, dy, w_shape, kind, out_dtype, name):
    s, d1, d2 = w_shape
    r = dy.shape[0]
    tr = _pick(r, _CONTRACT_TILES)
    if kind == "col":
        tkk = d1 if d1 <= 1024 else _pick(d1, (1024, 512, 256, 128))
        grid = (s * (d1 // tkk), 1, r // tr)
        nkk = d1 // tkk
        a_spec = pl.BlockSpec((tr, tkk), lambda j, i, k: (k, j % nkk))
        b_spec = pl.BlockSpec((tr, d2), lambda j, i, k: (k, j // nkk))
        o_spec = pl.BlockSpec((None, tkk, d2), lambda j, i, k: (j // nkk, j % nkk, 0))
        acc = (tkk, d2)
    else:
        tn = d2 if d2 <= 1024 else _pick(d2, (1024, 512, 256, 128))
        nn = d2 // tn
        grid = (s * nn, 1, r // tr)
        a_spec = pl.BlockSpec((tr, d1), lambda j, i, k: (k, j // nn))
        b_spec = pl.BlockSpec((tr, tn), lambda j, i, k: (k, j % nn))
        o_spec = pl.BlockSpec((None, d1, tn), lambda j, i, k: (j // nn, 0, j % nn))
        acc = (d1, tn)

    def step(a_ref, b_ref):
        return (_dot_tn(a_ref[...], b_ref[...]),)
    return _mm_call(name, grid, [a, dy], [a_spec, b_spec], [o_spec], [jax.ShapeDtypeStruct(w_shape, out_dtype)], acc, step)[0]


def mm_ffn_in(h, w1, w3, name, carry=None):
    r = h.shape[0]
    s, d1, d2 = w1.shape
    tm = _pick(r, _FUSED_ROW_TILES)
    a_spec = pl.BlockSpec((tm, d1), lambda j, i, k: (i, 0))
    b_spec = pl.BlockSpec((None, d1, d2), lambda j, i, k: (j, 0, 0))
    o_spec = pl.BlockSpec((tm, d2), lambda j, i, k: (i, j))
    sds = jax.ShapeDtypeStruct((r, s * d2), BF16)

    def step(a_ref, b1_ref, b3_ref):
        p1 = _dot_nn(a_ref[...], b1_ref[...])
        p3 = _dot_nn(a_ref[...], b3_ref[...])
        return p1, p3, _silu(p1) * p3

    return _mm_call(name, (s, r // tm, 1), [h, w1, w3], [a_spec, b_spec, b_spec], [o_spec] * 3, [sds] * 3, None, step, carry)


def mm_ffn_back(dy, w2, h1, h3, name, carry=None):
    r = dy.shape[0]
    s, d1, d2 = w2.shape
    tm = _pick(r, _FUSED_ROW_TILES)
    a_spec = pl.BlockSpec((tm, d2), lambda j, i, k: (i, 0))
    b_spec = pl.BlockSpec((None, d1, d2), lambda j, i, k: (j, 0, 0))
    o_spec = pl.BlockSpec((tm, d1), lambda j, i, k: (i, j))
    sds = jax.ShapeDtypeStruct((r, s * d1), BF16)

    def step(a_ref, b_ref, h1_ref, h3_ref):
        g = _dot_nt(a_ref[...], b_ref[...])
        a1 = h1_ref[...].astype(F32)
        sg = jax.nn.sigmoid(a1)
        return g * h3_ref[...].astype(F32) * (sg * (1.0 + a1 * (1.0 - sg))), g * a1 * sg

    return _mm_call(name, (s, r // tm, 1), [dy, w2, h1, h3], [a_spec, b_spec, o_spec, o_spec], [o_spec] * 2, [sds] * 2,
                    None, step, carry)


def make_linear(kind, out_dtype, name):
    def run(a, w3, gathering):
        carry = [(buf, _ag_plan()) for buf in gathering] or None
        out = mm_nn(a, w3, kind, out_dtype, name + "_fwd", carry)
        if not carry:
            return out, ()
        return out[0], _ag_finish_all(out[1:], name)

    @jax.custom_vjp
    def linear(a, w3, w_shard, gathering):
        return run(a, w3, gathering)

    def fwd(a, w3, w_shard, gathering):
        return run(a, w3, gathering), (a, w3, w_shard.shape, len(gathering))

    def bwd(res, cts):
        a, w3, shard_shape, n_gathering = res
        dyb = cts[0].astype(BF16)
        dw = mm_tn(a, dyb, w3.shape, kind, BF16, name + "_dw")
        buf, plan = _rs_begin(dw, name)
        da, buf = mm_nt(dyb, w3, kind, a.dtype, name + "_dx", carry=[(buf, plan)])
        return da, None, _rs_finish(buf, name).reshape(shard_shape), (None,) * n_gathering

    linear.defvjp(fwd, bwd)
    return linear


def _ag_finish_all(bufs, name):
    return tuple(_ag_finish(buf, f"{name}_gathered{q}") for q, buf in enumerate(bufs))


def make_ffn(name, w2_pending):
    def run(h, w1, w3, w2, gathering):
        g_in, g_out = gathering
        if w2_pending:
            g_in = (w2,) + tuple(g_in)
        carry = [(buf, _ag_plan()) for buf in g_in] or None
        out = mm_ffn_in(h, w1, w3, name + "_in", carry)
        h1, h3, act = out[:3]
        done_in = _ag_finish_all(out[3:], name + "_in")
        if w2_pending:
            w2, done_in = done_in[0], done_in[1:]
        carry2 = [(buf, _ag_plan()) for buf in g_out] or None
        y = mm_nn(act, w2, "row", F32, name + "_out", carry2)
        done = (done_in, _ag_finish_all(y[1:], name + "_out") if carry2 else ())
        return (y[0] if carry2 else y), done, (h1, h3, act, w2)

    @jax.custom_vjp
    def ffn(h, w1, w3, w2, s1, s3, s2, gathering):
        return run(h, w1, w3, w2, gathering)[:2]

    def fwd(h, w1, w3, w2, s1, s3, s2, gathering):
        y, done, (h1, h3, act, w2) = run(h, w1, w3, w2, gathering)
        return (y, done), (h, w1, w3, w2, h1, h3, act, s1.shape, s3.shape, s2.shape, tuple(len(g) for g in gathering))

    def bwd(res, cts):
        h, w1, w3, w2, h1, h3, act, shape1, shape3, shape2, n_gathering = res
        dyb = cts[0].astype(BF16)
        buf2, plan2 = _rs_begin(mm_tn(act, dyb, w2.shape, "row", BF16, name + "_dw2"), name + "_w2")
        dh1, dh3, buf2 = mm_ffn_back(dyb, w2, h1, h3, name + "_back", carry=[(buf2, plan2)])
        buf1, plan1 = _rs_begin(mm_tn(h, dh1, w1.shape, "col", BF16, name + "_dw1"), name + "_w1")
        buf3, plan3 = _rs_begin(mm_tn(h, dh3, w3.shape, "col", BF16, name + "_dw3"), name + "_w3")
        dh, buf1 = mm_nt(dh1, w1, "col", h.dtype, name + "_dx1", carry=[(buf1, plan1)])
        dh, buf3 = mm_nt(dh3, w3, "col", h.dtype, name + "_dx3", carry=[(buf3, plan3)], addend=dh)
        grads = [_rs_finish(b, name + n).reshape(sh) for b, n, sh in
                 ((buf1, "_w1", shape1), (buf3, "_w3", shape3), (buf2, "_w2", shape2))]
        return (dh, None, None, None, *grads, tuple((None,) * n for n in n_gathering))

    ffn.defvjp(fwd, bwd)
    return ffn


def _row_tile(r, t0, d):
    cap = max(8, (2 * 1024 * 1024) // (4 * d))
    cands = [t for t in (1024, 512, 256, 128, 64, 32, 16, 8) if t <= cap]
    for t in cands:
        if r % t == 0 and t0 % t == 0:
            return t
    raise ValueError("no row tile")


def _grp_spec(d, nb0):
    return pl.BlockSpec((None, 1, d), lambda i: (i // nb0, 0, 0))


def _norm_mod_fwd(z, g, scale, shift, t0, name):
    r, d = z.shape
    tr = _row_tile(r, t0, d)
    nb0 = t0 // tr

    def body(z_ref, g_ref, sc_ref, sh_ref, o_ref):
        zz = z_ref[...]
        rstd = lax.rsqrt(jnp.mean(zz * zz, axis=-1, keepdims=True) + EPS)
        y = zz * rstd * g_ref[...]
        o_ref[...] = (y * (1.0 + sc_ref[...]) + sh_ref[...]).astype(o_ref.dtype)

    return pl.pallas_call(
        body, grid=(r // tr,),
        in_specs=[pl.BlockSpec((tr, d), lambda i: (i, 0)), pl.BlockSpec((1, d), lambda i: (0, 0)),
                  _grp_spec(d, nb0), _grp_spec(d, nb0)],
        out_specs=pl.BlockSpec((tr, d), lambda i: (i, 0)),
        out_shape=jax.ShapeDtypeStruct((r, d), BF16), name=name, compiler_params=_cparams("parallel"))(z, g, scale, shift)


def _norm_mod_bwd(z, g, scale, dh, t0, name):
    r, d = z.shape
    ng = scale.shape[0]
    tr = _row_tile(r, t0, d)
    nb0 = t0 // tr

    def body(z_ref, g_ref, sc_ref, dh_ref, dz_ref, dg_ref, dsc_ref, dsh_ref):
        i = pl.program_id(0)
        zz = z_ref[...]
        gg = g_ref[...]
        rstd = lax.rsqrt(jnp.mean(zz * zz, axis=-1, keepdims=True) + EPS)
        zhat = zz * rstd
        dhh = dh_ref[...].astype(F32)
        dy = dhh * (1.0 + sc_ref[...])
        dyg = dy * gg
        dz_ref[...] = rstd * (dyg - zhat * jnp.mean(dyg * zhat, axis=-1, keepdims=True))

        @pl.when(i == 0)
        def _():
            dg_ref[...] = jnp.zeros_like(dg_ref)

        @pl.when((i == 0) | (i == nb0))
        def _():
            dsc_ref[...] = jnp.zeros_like(dsc_ref)
            dsh_ref[...] = jnp.zeros_like(dsh_ref)

        dg_ref[...] += jnp.sum(dy * zhat, axis=0, keepdims=True)
        dsc_ref[...] += jnp.sum(dhh * (zhat * gg), axis=0, keepdims=True)
        dsh_ref[...] += jnp.sum(dhh, axis=0, keepdims=True)

    return pl.pallas_call(
        body, grid=(r // tr,),
        in_specs=[pl.BlockSpec((tr, d), lambda i: (i, 0)), pl.BlockSpec((1, d), lambda i: (0, 0)),
                  _grp_spec(d, nb0), pl.BlockSpec((tr, d), lambda i: (i, 0))],
        out_specs=[pl.BlockSpec((tr, d), lambda i: (i, 0)), pl.BlockSpec((1, d), lambda i: (0, 0)),
                   _grp_spec(d, nb0), _grp_spec(d, nb0)],
        out_shape=[jax.ShapeDtypeStruct((r, d), F32), jax.ShapeDtypeStruct((1, d), F32),
                   jax.ShapeDtypeStruct((ng, 1, d), F32), jax.ShapeDtypeStruct((ng, 1, d), F32)],
        name=name, compiler_params=_cparams("arbitrary"))(z, g, scale, dh)


def make_norm_mod(t0, name):
    @jax.custom_vjp
    def f(z, g, scale, shift):
        return _norm_mod_fwd(z, g, scale, shift, t0, name + "_fwd")

    def fwd(z, g, scale, shift):
        return _norm_mod_fwd(z, g, scale, shift, t0, name + "_fwd"), (z, g, scale)

    def bwd(res, dh):
        z, g, scale = res
        dz, dg, dsc, dsh = _norm_mod_bwd(z, g, scale, dh, t0, name + "_bwd")
        return dz, dg, dsc, dsh

    f.defvjp(fwd, bwd)
    return f


def _gated_fwd(z, y, gate, t0, name):
    r, d = z.shape
    tr = _row_tile(r, t0, d)
    nb0 = t0 // tr

    def body(z_ref, y_ref, g_ref, o_ref):
        o_ref[...] = z_ref[...] + g_ref[...] * y_ref[...].astype(F32)

    return pl.pallas_call(
        body, grid=(r // tr,),
        in_specs=[pl.BlockSpec((tr, d), lambda i: (i, 0)), pl.BlockSpec((tr, d), lambda i: (i, 0)), _grp_spec(d, nb0)],
        out_specs=pl.BlockSpec((tr, d), lambda i: (i, 0)),
        out_shape=jax.ShapeDtypeStruct((r, d), F32), name=name, compiler_params=_cparams("parallel"))(z, y, gate)


def _gated_bwd(y, gate, dzn, t0, name):
    r, d = y.shape
    ng = gate.shape[0]
    tr = _row_tile(r, t0, d)
    nb0 = t0 // tr

    def body(y_ref, g_ref, dz_ref, dy_ref, dg_ref):
        i = pl.program_id(0)
        dzz = dz_ref[...]
        dy_ref[...] = (g_ref[...] * dzz).astype(dy_ref.dtype)

        @pl.when((i == 0) | (i == nb0))
        def _():
            dg_ref[...] = jnp.zeros_like(dg_ref)

        dg_ref[...] += jnp.sum(dzz * y_ref[...].astype(F32), axis=0, keepdims=True)

    return pl.pallas_call(
        body, grid=(r // tr,),
        in_specs=[pl.BlockSpec((tr, d), lambda i: (i, 0)), _grp_spec(d, nb0), pl.BlockSpec((tr, d), lambda i: (i, 0))],
        out_specs=[pl.BlockSpec((tr, d), lambda i: (i, 0)), _grp_spec(d, nb0)],
        out_shape=[jax.ShapeDtypeStruct((r, d), y.dtype), jax.ShapeDtypeStruct((ng, 1, d), F32)],
        name=name, compiler_params=_cparams("arbitrary"))(y, gate, dzn)


def make_gated_residual(t0, name):
    @jax.custom_vjp
    def f(z, y, gate):
        return _gated_fwd(z, y, gate, t0, name + "_fwd")

    def fwd(z, y, gate):
        return _gated_fwd(z, y, gate, t0, name + "_fwd"), (y, gate)

    def bwd(res, dzn):
        y, gate = res
        dy, dgate = _gated_bwd(y, gate, dzn, t0, name + "_bwd")
        return dzn, dy, dgate

    f.defvjp(fwd, bwd)
    return f


def make_residual_norm(t0, name):
    def row_spec(tr, d):
        return pl.BlockSpec((tr, d), lambda i: (i, 0))

    def fwd_call(z, y, gate, g, scale, shift):
        r, d = z.shape
        tr = _row_tile(r, t0, d)
        nb0 = t0 // tr

        def body(z_ref, y_ref, gt_ref, g_ref, sc_ref, sh_ref, zn_ref, h_ref):
            zz = z_ref[...] + gt_ref[...] * y_ref[...].astype(F32)
            zn_ref[...] = zz
            rstd = lax.rsqrt(jnp.mean(zz * zz, axis=-1, keepdims=True) + EPS)
            h_ref[...] = (zz * rstd * g_ref[...] * (1.0 + sc_ref[...]) + sh_ref[...]).astype(h_ref.dtype)

        grp = _grp_spec(d, nb0)
        return pl.pallas_call(
            body, grid=(r // tr,),
            in_specs=[row_spec(tr, d), row_spec(tr, d), grp, pl.BlockSpec((1, d), lambda i: (0, 0)), grp, grp],
            out_specs=[row_spec(tr, d), row_spec(tr, d)],
            out_shape=[jax.ShapeDtypeStruct((r, d), F32), jax.ShapeDtypeStruct((r, d), BF16)],
            name=name + "_fwd", compiler_params=_cparams("parallel"))(z, y, gate, g, scale, shift)

    def bwd_call(zn, y, gate, g, scale, dzn, dh):
        r, d = zn.shape
        ng = scale.shape[0]
        tr = _row_tile(r, t0, d)
        nb0 = t0 // tr

        def body(zn_ref, y_ref, gt_ref, g_ref, sc_ref, dzn_ref, dh_ref, dz_ref, dy_ref, dgt_ref, dg_ref, dsc_ref, dsh_ref):
            i = pl.program_id(0)
            zz = zn_ref[...]
            gg = g_ref[...]
            rstd = lax.rsqrt(jnp.mean(zz * zz, axis=-1, keepdims=True) + EPS)
            zhat = zz * rstd
            dhh = dh_ref[...].astype(F32)
            dyn = dhh * (1.0 + sc_ref[...])
            dyg = dyn * gg
            dz = dzn_ref[...] + rstd * (dyg - zhat * jnp.mean(dyg * zhat, axis=-1, keepdims=True))
            dz_ref[...] = dz
            dy_ref[...] = (gt_ref[...] * dz).astype(dy_ref.dtype)

            @pl.when(i == 0)
            def _():
                dg_ref[...] = jnp.zeros_like(dg_ref)

            @pl.when((i == 0) | (i == nb0))
            def _():
                dgt_ref[...] = jnp.zeros_like(dgt_ref)
                dsc_ref[...] = jnp.zeros_like(dsc_ref)
                dsh_ref[...] = jnp.zeros_like(dsh_ref)

            dgt_ref[...] += jnp.sum(dz * y_ref[...].astype(F32), axis=0, keepdims=True)
            dg_ref[...] += jnp.sum(dyn * zhat, axis=0, keepdims=True)
            dsc_ref[...] += jnp.sum(dhh * (zhat * gg), axis=0, keepdims=True)
            dsh_ref[...] += jnp.sum(dhh, axis=0, keepdims=True)

        grp = _grp_spec(d, nb0)
        vec = pl.BlockSpec((1, d), lambda i: (0, 0))
        gsds = jax.ShapeDtypeStruct((ng, 1, d), F32)
        return pl.pallas_call(
            body, grid=(r // tr,),
            in_specs=[row_spec(tr, d), row_spec(tr, d), grp, vec, grp, row_spec(tr, d), row_spec(tr, d)],
            out_specs=[row_spec(tr, d), row_spec(tr, d), grp, vec, grp, grp],
            out_shape=[jax.ShapeDtypeStruct((r, d), F32), jax.ShapeDtypeStruct((r, d), y.dtype), gsds,
                       jax.ShapeDtypeStruct((1, d), F32), gsds, gsds],
            name=name + "_bwd", compiler_params=_cparams("arbitrary"))(zn, y, gate, g, scale, dzn, dh)

    @jax.custom_vjp
    def f(z, y, gate, g, scale, shift):
        return tuple(fwd_call(z, y, gate, g, scale, shift))

    def fwd(z, y, gate, g, scale, shift):
        zn, h = fwd_call(z, y, gate, g, scale, shift)
        return (zn, h), (zn, y, gate, g, scale)

    def bwd(res, cts):
        zn, y, gate, g, scale = res
        dz, dy, dgate, dg, dsc, dsh = bwd_call(zn, y, gate, g, scale, cts[0], cts[1])
        return dz, dy, dgate, dg, dsc, dsh

    f.defvjp(fwd, bwd)
    return f


def _ew_call(name, body, ins, outs_sds, r, widths_in, widths_out, tr, extra_in=(), extra_specs=(), sem="parallel"):
    in_specs = [pl.BlockSpec((tr, w), lambda i: (i, 0)) for w in widths_in] + list(extra_specs)
    out_specs = [pl.BlockSpec((tr, w), lambda i: (i, 0)) if w is not None else pl.BlockSpec(s.shape, lambda i: (0,) * len(s.shape))
                 for w, s in zip(widths_out, outs_sds)]
    return pl.pallas_call(body, grid=(r // tr,), in_specs=in_specs, out_specs=out_specs, out_shape=outs_sds,
                          name=name, compiler_params=_cparams(sem))(*ins, *extra_in)


def _silu(x):
    return x * jax.nn.sigmoid(x)


_GELU_C = math.sqrt(2.0 / math.pi)


def _gelu_and_grad(y):
    inner = _GELU_C * (y + 0.044715 * y * y * y)
    t = jnp.tanh(inner)
    val = 0.5 * y * (1.0 + t)
    grad = 0.5 * (1.0 + t) + 0.5 * y * (1.0 - t * t) * _GELU_C * (1.0 + 3 * 0.044715 * y * y)
    return val, grad


def make_gelu_in(name):
    def fwd_call(u, ys, dsk):
        r, d = u.shape
        tr = _row_tile(r, r, d)

        def body(u_ref, y_ref, d_ref, o_ref):
            y = d_ref[...] * u_ref[...].astype(F32) + y_ref[...]
            o_ref[...] = _gelu_and_grad(y)[0].astype(o_ref.dtype)

        return _ew_call(name + "_fwd", body, (u, ys), [jax.ShapeDtypeStruct((r, d), BF16)], r, (d, d), (d,), tr,
                        extra_in=(dsk,), extra_specs=(pl.BlockSpec((1, d), lambda i: (0, 0)),))[0]

    @jax.custom_vjp
    def f(u, ys, dsk):
        return fwd_call(u, ys, dsk)

    def fwd(u, ys, dsk):
        return fwd_call(u, ys, dsk), (u, ys, dsk)

    def bwd(res, dg):
        u, ys, dsk = res
        r, d = u.shape
        tr = _row_tile(r, r, d)

        def body(u_ref, y_ref, dg_ref, d_ref, du_ref, dy_ref, dd_ref):
            i = pl.program_id(0)
            uu = u_ref[...].astype(F32)
            y = d_ref[...] * uu + y_ref[...]
            dy = dg_ref[...].astype(F32) * _gelu_and_grad(y)[1]
            dy_ref[...] = dy
            du_ref[...] = (d_ref[...] * dy).astype(du_ref.dtype)

            @pl.when(i == 0)
            def _():
                dd_ref[...] = jnp.zeros_like(dd_ref)

            dd_ref[...] += jnp.sum(dy * uu, axis=0, keepdims=True)

        outs = [jax.ShapeDtypeStruct((r, d), u.dtype), jax.ShapeDtypeStruct((r, d), F32), jax.ShapeDtypeStruct((1, d), F32)]
        du, dy, dd = _ew_call(name + "_bwd", body, (u, ys, dg), outs, r, (d, d, d), (d, d, None), tr,
                              extra_in=(dsk,), extra_specs=(pl.BlockSpec((1, d), lambda i: (0, 0)),), sem="arbitrary")
        return du, dy, dd

    f.defvjp(fwd, bwd)
    return f


def make_glu(name):
    def fwd_call(z, b):
        r, d2 = z.shape
        d = d2 // 2
        tr = _row_tile(r, r, d2)

        def body(z_ref, b_ref, o_ref):
            zz = z_ref[...].astype(F32) + b_ref[...]
            o_ref[...] = zz[:, :d] * jax.nn.sigmoid(zz[:, d:])

        return _ew_call(name + "_fwd", body, (z,), [jax.ShapeDtypeStruct((r, d), F32)], r, (d2,), (d,), tr,
                        extra_in=(b,), extra_specs=(pl.BlockSpec((1, d2), lambda i: (0, 0)),))[0]

    @jax.custom_vjp
    def f(z, b):
        return fwd_call(z, b)

    def fwd(z, b):
        return fwd_call(z, b), (z, b)

    def bwd(res, do):
        z, b = res
        r, d2 = z.shape
        d = d2 // 2
        tr = _row_tile(r, r, d2)

        def body(z_ref, do_ref, b_ref, dz_ref, db_ref):
            i = pl.program_id(0)
            zz = z_ref[...].astype(F32) + b_ref[...]
            sg = jax.nn.sigmoid(zz[:, d:])
            g = do_ref[...]
            dza = g * sg
            dzb = g * zz[:, :d] * sg * (1.0 - sg)
            dz_ref[:, :d] = dza.astype(dz_ref.dtype)
            dz_ref[:, d:] = dzb.astype(dz_ref.dtype)

            @pl.when(i == 0)
            def _():
                db_ref[...] = jnp.zeros_like(db_ref)

            db_ref[:, :d] += jnp.sum(dza, axis=0, keepdims=True)
            db_ref[:, d:] += jnp.sum(dzb, axis=0, keepdims=True)

        outs = [jax.ShapeDtypeStruct((r, d2), z.dtype), jax.ShapeDtypeStruct((1, d2), F32)]
        dz, db = _ew_call(name + "_bwd", body, (z, do), outs, r, (d2, d), (d2, None), tr,
                          extra_in=(b,), extra_specs=(pl.BlockSpec((1, d2), lambda i: (0, 0)),), sem="arbitrary")
        return dz, db

    f.defvjp(fwd, bwd)
    return f


def make_final_loss(name):
    def call(z, g, target):
        r, d = z.shape
        tr = _row_tile(r, r, d)

        def body(z_ref, t_ref, g_ref, dz_ref, dg_ref, l_ref):
            i = pl.program_id(0)
            zz = z_ref[...]
            gg = g_ref[...]
            rstd = lax.rsqrt(jnp.mean(zz * zz, axis=-1, keepdims=True) + EPS)
            zhat = zz * rstd
            e = zhat * gg - t_ref[...]
            dy = e * (1.0 / d)
            dyg = dy * gg
            dz_ref[...] = rstd * (dyg - zhat * jnp.mean(dyg * zhat, axis=-1, keepdims=True))

            @pl.when(i == 0)
            def _():
                dg_ref[...] = jnp.zeros_like(dg_ref)
                l_ref[...] = jnp.zeros_like(l_ref)

            dg_ref[...] += jnp.sum(dy * zhat, axis=0, keepdims=True)
            l_ref[...] += jnp.sum(jnp.sum(e * e, axis=1, keepdims=True), axis=0, keepdims=True) * (0.5 / d)

        outs = [jax.ShapeDtypeStruct((r, d), F32), jax.ShapeDtypeStruct((1, d), F32), jax.ShapeDtypeStruct((1, 1), F32)]
        return _ew_call(name, body, (z, target), outs, r, (d, d), (d, None, None), tr,
                        extra_in=(g,), extra_specs=(pl.BlockSpec((1, d), lambda i: (0, 0)),), sem="arbitrary")

    @jax.custom_vjp
    def f(z, g, target):
        return call(z, g, target)[2]

    def fwd(z, g, target):
        dz, dg, loss = call(z, g, target)
        return loss, (dz, dg)

    def bwd(res, dl):
        dz, dg = res
        s = dl[0, 0]
        return dz * s, dg * s, None

    f.defvjp(fwd, bwd)
    return f


def _rope_tables(t):
    quarter = HEAD_DIM // 4
    inv_freq = ROPE_BASE ** (-np.arange(quarter, dtype=np.float64) / quarter)
    pos = np.arange(t)
    ang_r = (pos // GRID_W)[:, None] * inv_freq[None, :]
    ang_c = (pos % GRID_W)[:, None] * inv_freq[None, :]
    cos = np.concatenate([np.cos(ang_r), np.cos(ang_r), np.cos(ang_c), np.cos(ang_c)], axis=1)
    sin = np.concatenate([-np.sin(ang_r), np.sin(ang_r), -np.sin(ang_c), np.sin(ang_c)], axis=1)
    return jnp.asarray(cos, F32), jnp.asarray(sin, F32)


def _rope_call(x, cos, sin, name):
    t, w = x.shape
    tr = _pick(t, (512, 256, 128, 64))
    quarter = HEAD_DIM // 4

    def body(x_ref, c_ref, s_ref, o_ref):
        xx = x_ref[...].astype(F32)
        lane = lax.broadcasted_iota(jnp.int32, xx.shape, 1)
        first = (lane % (2 * quarter)) < quarter
        partner = jnp.where(first, pltpu.roll(xx, HEAD_DIM - quarter, 1), pltpu.roll(xx, quarter, 1))
        o_ref[...] = (xx * c_ref[...] + partner * s_ref[...]).astype(o_ref.dtype)

    return pl.pallas_call(
        body, grid=(t // tr, w // HEAD_DIM),
        in_specs=[pl.BlockSpec((tr, HEAD_DIM), lambda i, j: (i, j)), pl.BlockSpec((tr, HEAD_DIM), lambda i, j: (i, 0)),
                  pl.BlockSpec((tr, HEAD_DIM), lambda i, j: (i, 0))],
        out_specs=pl.BlockSpec((tr, HEAD_DIM), lambda i, j: (i, j)),
        out_shape=jax.ShapeDtypeStruct((t, w), x.dtype), name=name, compiler_params=_cparams("parallel", "parallel"))(x, cos, sin)


def make_rope(t, name):
    cos, sin = _rope_tables(t)

    @jax.custom_vjp
    def f(x):
        return _rope_call(x, cos, sin, name + "_fwd")

    def fwd(x):
        return _rope_call(x, cos, sin, name + "_fwd"), None

    def bwd(_, dy):
        return (_rope_call(dy, cos, -sin, name + "_bwd"),)

    f.defvjp(fwd, bwd)
    return f


def _attn_specs(g, span, tk, m, nbh, has_ctx, hb=1):
    hd = HEAD_DIM
    q_spec = pl.BlockSpec((ATTN_BLOCK, hb * g * hd), lambda h, i, meta: (i, h))
    kv_spec = pl.BlockSpec((tk, hb * hd), lambda h, i, meta: (0, h))
    c_spec = pl.BlockSpec((m, hb * hd), lambda h, i, meta: (0, h))
    if nbh > 1 and hb > 1:
        b_spec = pl.BlockSpec((None, hb, ATTN_BLOCK, span), lambda h, i, meta: (meta[1, i], h, 0, 0))
    elif nbh > 1:
        b_spec = pl.BlockSpec((None, None, ATTN_BLOCK, span), lambda h, i, meta: (meta[1, i], h, 0, 0))
    else:
        b_spec = pl.BlockSpec((None, None, ATTN_BLOCK, span), lambda h, i, meta: (meta[1, i], 0, 0, 0))
    sink_spec = pl.BlockSpec(memory_space=pltpu.SMEM)
    return q_spec, kv_spec, c_spec, b_spec, sink_spec


def _attn_probs(qh, ks, kc, bias, sink_val, scale, has_ctx, has_sink):
    s = _dot_nt(qh, ks) * scale + bias
    mx = jnp.max(s, axis=-1, keepdims=True)
    sc = None
    if has_ctx:
        sc = _dot_nt(qh, kc) * scale
        mx = jnp.maximum(mx, jnp.max(sc, axis=-1, keepdims=True))
    if has_sink:
        mx = jnp.maximum(mx, sink_val)
    p = jnp.exp(s - mx)
    l = jnp.sum(p, axis=-1, keepdims=True)
    pc = None
    if has_ctx:
        pc = jnp.exp(sc - mx)
        l = l + jnp.sum(pc, axis=-1, keepdims=True)
    ps = None
    if has_sink:
        ps = jnp.exp(sink_val - mx)
        l = l + ps
    return p, pc, ps, l


def _attn_fwd(q, k, v, kc, vc, bias, sink, meta, g, span, has_ctx, has_sink, name):
    rq, wq = q.shape
    tk, wk = k.shape
    hkv = wk // HEAD_DIM
    m = kc.shape[0]
    nbh = bias.shape[1]
    scale = HEAD_DIM ** -0.5
    nqb = rq // ATTN_BLOCK
    hb = _pick(hkv, (4, 2, 1)) if g == 1 else 1
    q_spec, kv_spec, c_spec, b_spec, sink_spec = _attn_specs(g, span, tk, m, nbh, has_ctx, hb)
    hd = HEAD_DIM

    def body(meta_ref, sink_ref, q_ref, k_ref, v_ref, kc_ref, vc_ref, b_ref, o_ref):
        h = pl.program_id(0)
        i = pl.program_id(1)
        ks0 = pl.multiple_of(meta_ref[0, i], 64)
        for kl in range(hb):
            kcols = slice(kl * hd, (kl + 1) * hd)
            ks = k_ref[pl.ds(ks0, span), kcols]
            vs = v_ref[pl.ds(ks0, span), kcols]
            bias_t = b_ref[kl] if (hb > 1 and nbh > 1) else b_ref[...]
            for hh in range(g):
                cols = slice((kl * g + hh) * hd, (kl * g + hh + 1) * hd)
                sink_val = sink_ref[(h * hb + kl) * g + hh] if has_sink else None
                p, pc, _, l = _attn_probs(q_ref[:, cols], ks, kc_ref[:, kcols], bias_t, sink_val, scale, has_ctx, has_sink)
                acc = jnp.dot(p.astype(BF16), vs, preferred_element_type=F32)
                if has_ctx:
                    acc = acc + jnp.dot(pc.astype(BF16), vc_ref[:, kcols], preferred_element_type=F32)
                o_ref[:, cols] = (acc / l).astype(o_ref.dtype)

    gs = pltpu.PrefetchScalarGridSpec(
        num_scalar_prefetch=1, grid=(hkv // hb, nqb),
        in_specs=[sink_spec, q_spec, kv_spec, kv_spec, c_spec, c_spec, b_spec], out_specs=q_spec)
    return pl.pallas_call(body, grid_spec=gs, out_shape=jax.ShapeDtypeStruct((rq, wq), BF16), name=name,
                          compiler_params=_cparams("parallel", "arbitrary"))(meta, sink, q, k, v, kc, vc, bias)


def _attn_bwd(q, k, v, kc, vc, bias, sink, meta, o, do, g, span, has_ctx, has_sink, want_dbias, name):
    rq, wq = q.shape
    tk, wk = k.shape
    hkv = wk // HEAD_DIM
    m = kc.shape[0]
    ncase, nbh = bias.shape[:2]
    scale = HEAD_DIM ** -0.5
    nqb = rq // ATTN_BLOCK
    hb = _pick(hkv, (2, 1)) if g == 1 else 1
    q_spec, kv_spec, c_spec, b_spec, sink_spec = _attn_specs(g, span, tk, m, nbh, has_ctx, hb)
    dsink_spec = pl.BlockSpec((None, 8, HEAD_DIM), lambda h, i, meta: (h, 0, 0))
    hd = HEAD_DIM

    def body(meta_ref, sink_ref, q_ref, k_ref, v_ref, kc_ref, vc_ref, b_ref, o_ref, do_ref,
             dq_ref, dk_ref, dv_ref, dkc_ref, dvc_ref, db_ref, dsk_ref):
        h = pl.program_id(0)
        i = pl.program_id(1)

        @pl.when(i == 0)
        def _():
            dk_ref[...] = jnp.zeros_like(dk_ref)
            dv_ref[...] = jnp.zeros_like(dv_ref)
            dkc_ref[...] = jnp.zeros_like(dkc_ref)
            dvc_ref[...] = jnp.zeros_like(dvc_ref)
            dsk_ref[...] = jnp.zeros_like(dsk_ref)

        if want_dbias:
            @pl.when(meta_ref[2, i] == 1)
            def _():
                db_ref[...] = jnp.zeros_like(db_ref)
        else:
            @pl.when(i == 0)
            def _():
                db_ref[...] = jnp.zeros_like(db_ref)

        ks0 = pl.multiple_of(meta_ref[0, i], 64)
        for kl in range(hb):
            kcols = slice(kl * hd, (kl + 1) * hd)
            ks = k_ref[pl.ds(ks0, span), kcols]
            vs = v_ref[pl.ds(ks0, span), kcols]
            per_head_bias = hb > 1 and nbh > 1
            bias_t = b_ref[kl] if per_head_bias else b_ref[...]
            dk_acc = jnp.zeros((span, hd), F32)
            dv_acc = jnp.zeros((span, hd), F32)
            for hh in range(g):
                cols = slice((kl * g + hh) * hd, (kl * g + hh + 1) * hd)
                qh = q_ref[:, cols]
                doh = do_ref[:, cols]
                sink_val = sink_ref[(h * hb + kl) * g + hh] if has_sink else None
                p, pc, ps, l = _attn_probs(qh, ks, kc_ref[:, kcols], bias_t, sink_val, scale, has_ctx, has_sink)
                inv_l = 1.0 / l
                delta = jnp.sum(doh.astype(F32) * o_ref[:, cols].astype(F32), axis=-1, keepdims=True)
                pn = p * inv_l
                ds = pn * (_dot_nt(doh, vs) - delta)
                dsb = ds.astype(BF16)
                dq = jnp.dot(dsb, ks, preferred_element_type=F32)
                dk_acc = dk_acc + _dot_tn(dsb, qh)
                dv_acc = dv_acc + _dot_tn(pn.astype(BF16), doh)
                if want_dbias and per_head_bias:
                    db_ref[kl] += ds
                elif want_dbias:
                    db_ref[...] += ds
                if has_ctx:
                    pcn = pc * inv_l
                    dsc = (pcn * (_dot_nt(doh, vc_ref[:, kcols]) - delta)).astype(BF16)
                    dq = dq + jnp.dot(dsc, kc_ref[:, kcols], preferred_element_type=F32)
                    dkc_ref[:, kcols] += _dot_tn(dsc, qh) * scale
                    dvc_ref[:, kcols] += _dot_tn(pcn.astype(BF16), doh)
                if has_sink:
                    dsv = -jnp.sum(ps * inv_l * delta, axis=0, keepdims=True)
                    dsk_ref[kl * g + hh:kl * g + hh + 1, :] += jnp.broadcast_to(dsv, (1, hd))
                dq_ref[:, cols] = (dq * scale).astype(dq_ref.dtype)
            dk_ref[pl.ds(ks0, span), kcols] += dk_acc * scale
            dv_ref[pl.ds(ks0, span), kcols] += dv_acc

    gs = pltpu.PrefetchScalarGridSpec(
        num_scalar_prefetch=1, grid=(hkv // hb, nqb),
        in_specs=[sink_spec, q_spec, kv_spec, kv_spec, c_spec, c_spec, b_spec, q_spec, q_spec],
        out_specs=[q_spec, kv_spec, kv_spec, c_spec, c_spec, b_spec if want_dbias else dsink_spec, dsink_spec])
    db_sds = jax.ShapeDtypeStruct((ncase, nbh, ATTN_BLOCK, span) if want_dbias else (hkv, 8, HEAD_DIM), F32)
    out_shape = [jax.ShapeDtypeStruct((rq, wq), BF16), jax.ShapeDtypeStruct((tk, wk), F32), jax.ShapeDtypeStruct((tk, wk), F32),
                 jax.ShapeDtypeStruct((m, wk), F32), jax.ShapeDtypeStruct((m, wk), F32), db_sds,
                 jax.ShapeDtypeStruct((hkv, 8, HEAD_DIM), F32)]
    return pl.pallas_call(body, grid_spec=gs, out_shape=out_shape, name=name,
                          compiler_params=_cparams("parallel", "arbitrary"))(meta, sink, q, k, v, kc, vc, bias, o, do)


def make_attention(meta_np, g, span, has_ctx, has_sink, want_dbias, name):
    meta = jnp.asarray(meta_np, jnp.int32)

    @jax.custom_vjp
    def f(q, k, v, kc, vc, bias, sink):
        return _attn_fwd(q, k, v, kc, vc, bias, sink, meta, g, span, has_ctx, has_sink, name + "_fwd")

    def fwd(q, k, v, kc, vc, bias, sink):
        o = _attn_fwd(q, k, v, kc, vc, bias, sink, meta, g, span, has_ctx, has_sink, name + "_fwd")
        return o, (q, k, v, kc, vc, bias, sink, o)

    def bwd(res, do):
        q, k, v, kc, vc, bias, sink, o = res
        dq, dk, dv, dkc, dvc, db, dsk = _attn_bwd(q, k, v, kc, vc, bias, sink, meta, o, do.astype(BF16), g, span,
                                                   has_ctx, has_sink, want_dbias, name + "_bwd")
        dsink = dsk[:, :g, 0].reshape(sink.shape) if has_sink else jnp.zeros_like(sink)
        if not want_dbias:
            db = jnp.zeros_like(bias)
        return dq, dk.astype(k.dtype), dv.astype(v.dtype), dkc.astype(kc.dtype), dvc.astype(vc.dtype), db, dsink

    f.defvjp(fwd, bwd)
    return f


def _dedupe_cases(tables):
    cases, idx, first = [], [], []
    for tbl in tables:
        if cases and np.array_equal(cases[-1], tbl):
            idx.append(len(cases) - 1)
            first.append(0)
        else:
            cases.append(tbl)
            idx.append(len(cases) - 1)
            first.append(1)
    return cases, idx, first


def _na_plan(t):
    rows = t // GRID_W
    qr = ATTN_BLOCK // GRID_W
    kr = qr + NA_ROWS - 1
    assert rows >= kr and rows % qr == 0
    span = kr * GRID_W
    kstart, tables = [], []
    qcol = np.tile(np.arange(GRID_W), qr)
    kcol = np.tile(np.arange(GRID_W), kr)
    win_c = np.clip(qcol - NA_COLS // 2, 0, GRID_W - NA_COLS)
    col_ok = (kcol[None, :] >= win_c[:, None]) & (kcol[None, :] < win_c[:, None] + NA_COLS)
    dcol = np.clip(kcol[None, :] - qcol[:, None] + NA_COLS - 1, 0, 2 * NA_COLS - 2)
    for r0 in range(0, rows, qr):
        kb = int(np.clip(r0 - NA_ROWS // 2, 0, rows - kr))
        qrow = r0 + np.repeat(np.arange(qr), GRID_W)
        krow = kb + np.repeat(np.arange(kr), GRID_W)
        win_r = np.clip(qrow - NA_ROWS // 2, 0, rows - NA_ROWS)
        row_ok = (krow[None, :] >= win_r[:, None]) & (krow[None, :] < win_r[:, None] + NA_ROWS)
        drow = np.clip(krow[None, :] - qrow[:, None] + NA_ROWS - 1, 0, 2 * NA_ROWS - 2)
        tables.append(np.stack([row_ok & col_ok, drow, dcol]).astype(np.int32))
        kstart.append(kb * GRID_W)
    cases, idx, first = _dedupe_cases(tables)
    meta = np.array([kstart, idx, first], np.int32)
    return meta, span, np.stack(cases)


def _na_bias(rpb, cases):
    valid, drow, dcol = cases[:, 0], cases[:, 1], cases[:, 2]
    ncase, qn, span = valid.shape
    qr, kr = qn // GRID_W, span // GRID_W
    drow_s = drow.reshape(ncase, qr, GRID_W, kr, GRID_W)[:, :, 0, :, 0]
    dcol_s = dcol[0].reshape(qr, GRID_W, kr, GRID_W)[0, :, 0, :]
    oh_r = jnp.asarray(np.eye(2 * NA_ROWS - 1, dtype=np.float32)[drow_s])
    oh_c = jnp.asarray(np.eye(2 * NA_COLS - 1, dtype=np.float32)[dcol_s])
    tmp = jnp.einsum("hrc,xyc->hrxy", rpb, oh_c, precision=lax.Precision.HIGHEST)
    b = jnp.einsum("nakr,hrxy->nhaxky", oh_r, tmp, precision=lax.Precision.HIGHEST).reshape(ncase, -1, qn, span)
    return jnp.where(jnp.asarray(valid[:, None] > 0), b, NEG_INF)


def _sw_plan(t):
    span = 3 * ATTN_BLOCK
    assert t >= span
    kstart, tables = [], []
    for b in range(t // ATTN_BLOCK):
        ks = int(np.clip((b - 1) * ATTN_BLOCK, 0, t - span))
        qpos = b * ATTN_BLOCK + np.arange(ATTN_BLOCK)
        kpos = ks + np.arange(span)
        ok = np.abs(kpos[None, :] - qpos[:, None]) <= SW_RADIUS
        tables.append(np.where(ok, 0.0, NEG_INF).astype(np.float32))
        kstart.append(ks)
    cases, idx, first = _dedupe_cases(tables)
    return np.array([kstart, idx, first], np.int32), span, np.stack(cases)[:, None]


def _cmul(ar, ai, br, bi):
    return ar * br - ai * bi, ar * bi + ai * br


def _s5_scan_call(x2, win, lam, cin, wout, reverse, n_chunks, name):
    _, ll, d = x2.shape
    nt = d // HEAD_DIM
    sw = 2 * SSM_TILE_GROUPS * SSM_STATE
    hw = sw // 2
    rows = ll // n_chunks
    ic = rows // SCAN_BLOCKS
    full = cin is not None

    down_dir = 0 if reverse else 1

    def chunk_idx(k, dd):
        return jnp.where(dd == down_dir, n_chunks - 1 - k, k)

    def body(*refs):
        if full:
            x_ref, win_ref, lam_ref, cin_ref, wout_ref, s_out, y_out, ub_ref, st_ref = refs
        else:
            x_ref, win_ref, lam_ref, f_out, ub_ref, st_ref = refs
        k = pl.program_id(2)
        down = pl.program_id(0) == down_dir

        @pl.when(k == 0)
        def _():
            st_ref[...] = cin_ref[...] if full else jnp.zeros_like(st_ref)

        ub_ref[...] = jnp.dot(x_ref[...].astype(BF16), win_ref[...], preferred_element_type=F32)
        lr = lam_ref[:, :hw]
        li = lam_ref[:, hw:]

        def step(ii, carry):
            sr, si = carry
            i = jnp.where(down, ic - 1 - ii, ii)
            r0 = pl.multiple_of(i * SCAN_BLOCKS, SCAN_BLOCKS)
            ur = ub_ref[pl.ds(r0, SCAN_BLOCKS), :hw]
            ui = ub_ref[pl.ds(r0, SCAN_BLOCKS), hw:]
            nr = lr * sr - li * si + ur
            ni = lr * si + li * sr + ui
            if full:
                ub_ref[pl.ds(r0, SCAN_BLOCKS), :hw] = nr
                ub_ref[pl.ds(r0, SCAN_BLOCKS), hw:] = ni
            return nr, ni

        sr, si = lax.fori_loop(0, ic, step, (st_ref[:, :hw], st_ref[:, hw:]), unroll=4 if ic % 4 == 0 else 1)
        st_ref[:, :hw] = sr
        st_ref[:, hw:] = si
        if full:
            sb = ub_ref[...].astype(BF16)
            s_out[...] = sb
            y_out[...] = jnp.dot(sb, wout_ref[...], preferred_element_type=F32)
        else:
            @pl.when(k == n_chunks - 1)
            def _():
                f_out[...] = st_ref[...]

    x_spec = pl.BlockSpec((None, rows, HEAD_DIM), lambda dd, t, k: (dd, chunk_idx(k, dd), t))
    win_spec = pl.BlockSpec((None, None, HEAD_DIM, sw), lambda dd, t, k: (dd, t, 0, 0))
    vec_spec = pl.BlockSpec((None, None, SCAN_BLOCKS, sw), lambda dd, t, k: (dd, t, 0, 0))
    scratch = [pltpu.VMEM((rows, sw), F32), pltpu.VMEM((SCAN_BLOCKS, sw), F32)]
    if full:
        in_specs = [x_spec, win_spec, vec_spec, vec_spec, pl.BlockSpec((None, None, sw, HEAD_DIM), lambda dd, t, k: (dd, t, 0, 0))]
        out_specs = [pl.BlockSpec((None, None, rows, sw), lambda dd, t, k: (dd, t, chunk_idx(k, dd), 0)), x_spec]
        out_shape = [jax.ShapeDtypeStruct((2, nt, ll, sw), BF16), jax.ShapeDtypeStruct((2, ll, d), F32)]
        args = (x2, win, lam, cin, wout)
    else:
        in_specs = [x_spec, win_spec, vec_spec]
        out_specs = vec_spec
        out_shape = jax.ShapeDtypeStruct((2, nt, SCAN_BLOCKS, sw), F32)
        args = (x2, win, lam)
    return pl.pallas_call(body, grid=(2, nt, n_chunks), in_specs=in_specs, out_specs=out_specs, out_shape=out_shape,
                          scratch_shapes=scratch, name=name,
                          compiler_params=_cparams("parallel", "parallel", "arbitrary"))(*args)


def _s5_bwd_call(dy2, wrt, lamc, cin, st, u2, wdt, n_chunks, name):
    _, ll, d = dy2.shape
    nt = d // HEAD_DIM
    sw = 2 * SSM_TILE_GROUPS * SSM_STATE
    hw = sw // 2
    rows = ll // n_chunks
    ic = rows // SCAN_BLOCKS

    def chunk_idx(k, dd):
        return jnp.where(dd == 0, n_chunks - 1 - k, k)

    def body(dy_ref, wrt_ref, lam_ref, cin_ref, stb_ref, u_ref, wdt_ref, du_out, dwd_out, dwr_out, dlam_out,
             ds_ref, a_ref, st_ref):
        k = pl.program_id(2)
        down = pl.program_id(0) == 0
        st_ref[...] = stb_ref[...].astype(F32)

        @pl.when(k == 0)
        def _():
            a_ref[...] = cin_ref[...]
            dwd_out[...] = jnp.zeros_like(dwd_out)
            dwr_out[...] = jnp.zeros_like(dwr_out)
            dlam_out[...] = jnp.zeros_like(dlam_out)

        dyb = dy_ref[...].astype(BF16)
        ds_ref[...] = jnp.dot(dyb, wrt_ref[...], preferred_element_type=F32)
        dwr_out[...] += _dot_tn(stb_ref[...], dyb)
        lr = lam_ref[:, :hw]
        li = lam_ref[:, hw:]

        def step(ii, carry):
            ar, ai, gr, gi = carry
            i = jnp.where(down, ic - 1 - ii, ii)
            r0 =pl.multiple_of(i * SCAN_BLOCKS, SCAN_BLOCKS)
            sr = st_ref[pl.ds(r0, SCAN_BLOCKS), :hw]
            si = st_ref[pl.ds(r0, SCAN_BLOCKS), hw:]
            gr = gr + ar * sr + ai * si
            gi = gi + ai * sr - ar * si
            nr = lr * ar - li * ai + ds_ref[pl.ds(r0, SCAN_BLOCKS), :hw]
            ni = lr * ai + li * ar + ds_ref[pl.ds(r0, SCAN_BLOCKS), hw:]
            ds_ref[pl.ds(r0, SCAN_BLOCKS), :hw] = nr
            ds_ref[pl.ds(r0, SCAN_BLOCKS), hw:] = ni
            return nr, ni, gr, gi

        init = (a_ref[:, :hw], a_ref[:, hw:], dlam_out[:, :hw], dlam_out[:, hw:])
        ar, ai, gr, gi = lax.fori_loop(0, ic, step, init, unroll=4 if ic % 4 == 0 else 1)
        a_ref[:, :hw] = ar
        a_ref[:, hw:] = ai
        dlam_out[:, :hw] = gr
        dlam_out[:, hw:] = gi
        ab = ds_ref[...].astype(BF16)
        du_out[...] = jnp.dot(ab, wdt_ref[...], preferred_element_type=F32).astype(du_out.dtype)
        dwd_out[...] += _dot_tn(u_ref[...].astype(BF16), ab)

    x_spec = pl.BlockSpec((None, rows, HEAD_DIM), lambda dd, t, k: (dd, chunk_idx(k, dd), t))
    w_in = pl.BlockSpec((None, None, HEAD_DIM, sw), lambda dd, t, k: (dd, t, 0, 0))
    w_out = pl.BlockSpec((None, None, sw, HEAD_DIM), lambda dd, t, k: (dd, t, 0, 0))
    vec_spec = pl.BlockSpec((None, None, SCAN_BLOCKS, sw), lambda dd, t, k: (dd, t, 0, 0))
    st_spec = pl.BlockSpec((None, None, rows, sw), lambda dd, t, k: (dd, t, chunk_idx(k, dd), 0))
    out_shape = [jax.ShapeDtypeStruct((2, ll, d), u2.dtype), jax.ShapeDtypeStruct((2, nt, HEAD_DIM, sw), F32),
                 jax.ShapeDtypeStruct((2, nt, sw, HEAD_DIM), F32), jax.ShapeDtypeStruct((2, nt, SCAN_BLOCKS, sw), F32)]
    return pl.pallas_call(
        body, grid=(2, nt, n_chunks),
        in_specs=[x_spec, w_in, vec_spec, vec_spec, st_spec, x_spec, w_out],
        out_specs=[x_spec, w_in, w_out, vec_spec], out_shape=out_shape,
        scratch_shapes=[pltpu.VMEM((rows, sw), F32), pltpu.VMEM((SCAN_BLOCKS, sw), F32), pltpu.VMEM((rows, sw), F32)],
        name=name, compiler_params=_cparams("parallel", "parallel", "arbitrary"))(dy2, wrt, lamc, cin, st, u2, wdt)


def _cpow(lr, li, n):
    rr, ri = jnp.ones_like(lr), jnp.zeros_like(li)
    br, bi = lr, li
    while n:
        if n & 1:
            rr, ri = _cmul(rr, ri, br, bi)
        br, bi = _cmul(br, bi, br, bi)
        n >>= 1
    return rr, ri


def _resolve_carries(finals, lam, block_len, down_dir):
    hw = finals.shape[-1] // 2
    pr, pi = _cpow(lam[:, :, 0, :hw], lam[:, :, 0, hw:], block_len)
    fr, fi = finals[..., :hw], finals[..., hw:]

    def walk(order):
        cr, ci = jnp.zeros_like(pr), jnp.zeros_like(pi)
        out = [None] * SCAN_BLOCKS
        for j in order:
            out[j] = jnp.concatenate([cr, ci], axis=-1)
            mr, mi = _cmul(pr, pi, cr, ci)
            cr, ci = mr + fr[:, :, j], mi + fi[:, :, j]
        return jnp.stack(out, axis=2)

    up, down = walk(range(SCAN_BLOCKS)), walk(range(SCAN_BLOCKS - 1, -1, -1))
    return jnp.stack([down[0], up[1]] if down_dir == 0 else [up[0], down[1]])


def _scan_chunks(ll):
    block_len = ll // SCAN_BLOCKS
    for ic in (132, 128, 96, 64, 48, 36, 32, 24, 16, 8):
        if block_len % ic == 0:
            return block_len // ic
    return 1


def make_s5_core(name):
    def run_fwd(u2, lam, wd, wr):
        ll = u2.shape[1]
        nc = _scan_chunks(ll)
        lam8 = jnp.broadcast_to(lam[:, :, None, :], lam.shape[:2] + (SCAN_BLOCKS, lam.shape[-1]))
        wdb = wd.astype(BF16)
        finals = _s5_scan_call(u2, wdb, lam8, None, None, False, nc, name + "_carry")
        cin = _resolve_carries(finals, lam8, ll // SCAN_BLOCKS, 1)
        st, y2 = _s5_scan_call(u2, wdb, lam8, cin, wr.astype(BF16), False, nc, name + "_scan")
        return y2, st, lam8

    @jax.custom_vjp
    def f(u2, lam, wd, wr):
        return run_fwd(u2, lam, wd, wr)[0]

    def fwd(u2, lam, wd, wr):
        y2, st, lam8 = run_fwd(u2, lam, wd, wr)
        return y2, (u2, lam8, wd, wr, st)

    def bwd(res, dy2):
        u2, lam8, wd, wr, st = res
        ll = u2.shape[1]
        nc = _scan_chunks(ll)
        hw = lam8.shape[-1] // 2
        lamc = jnp.concatenate([lam8[..., :hw], -lam8[..., hw:]], axis=-1)
        wrt = jnp.swapaxes(wr, 2, 3).astype(BF16)
        wdt = jnp.swapaxes(wd, 2, 3).astype(BF16)
        finals = _s5_scan_call(dy2, wrt, lamc, None, None, True, nc, name + "_bcarry")
        cin = _resolve_carries(finals, lamc, ll // SCAN_BLOCKS, 0)
        du2, dwd, dwr, dlam8 = _s5_bwd_call(dy2, wrt, lamc, cin, st, u2, wdt, nc, name + "_bscan")
        return du2, jnp.sum(dlam8, axis=2), dwd, dwr

    f.defvjp(fwd, bwd)
    return f


def _s5_params(a_re, a_im, log_dt, b_re, b_im, c_re, c_im):
    dt = jnp.exp(log_dt)[..., None]
    mag = jnp.exp(a_re * dt)
    lam_r, lam_i = mag * jnp.cos(a_im * dt), mag * jnp.sin(a_im * dt)
    den = a_re * a_re + a_im * a_im
    nr = lam_r - 1.0
    coef_r = (nr * a_re + lam_i * a_im) / den
    coef_i = (lam_i * a_re - nr * a_im) / den
    bbar_r = coef_r[..., None] * b_re - coef_i[..., None] * b_im
    bbar_i = coef_r[..., None] * b_im + coef_i[..., None] * b_re
    ndir, g, p = lam_r.shape
    tg = SSM_TILE_GROUPS
    nt = g // tg
    eye = jnp.eye(tg, dtype=F32)

    def tile_vec(v):
        return v.reshape(ndir, nt, tg * p)

    lam = jnp.concatenate([tile_vec(lam_r), tile_vec(lam_i)], axis=-1)

    def drive(b):
        bt = b.reshape(ndir, nt, tg, p, SSM_GROUP)
        return (jnp.swapaxes(bt, 3, 4)[:, :, :, :, None, :] * eye[None, None, :, None, :, None]).reshape(ndir, nt, tg * SSM_GROUP, tg * p)

    wd = jnp.concatenate([drive(bbar_r), drive(bbar_i)], axis=-1)

    def readout(c):
        ct = c.reshape(ndir, nt, tg, SSM_GROUP, p)
        return (jnp.swapaxes(ct, 3, 4)[:, :, :, :, None, :] * eye[None, None, :, None, :, None]).reshape(ndir, nt, tg * p, tg * SSM_GROUP)

    wr = jnp.concatenate([readout(c_re), -readout(c_im)], axis=2)
    return lam, wd, wr


def _to_scan_order(seq):
    ll, d = seq.shape
    return seq.reshape(SCAN_BLOCKS, ll // SCAN_BLOCKS, d).swapaxes(0, 1).reshape(ll, d)


def _from_scan_order(y2):
    ll, d = y2.shape
    return y2.reshape(ll // SCAN_BLOCKS, SCAN_BLOCKS, d).swapaxes(0, 1).reshape(ll, d)


def adamw(w, g, m, v, name):
    shape = w.shape
    cols = shape[-1] if len(shape) > 1 else shape[0]
    w2, g2, m2, v2 = (a.reshape(-1, cols) for a in (w, g, m, v))
    r = w2.shape[0]
    cap = max(1, (1024 * 1024) // (4 * cols))
    tr = r
    for t in (512, 256, 128, 64, 32, 16, 8):
        if t <= cap and r % t == 0:
            tr = t
            break
    c1 = 1.0 / (1.0 - ADAM_B1 ** ADAM_STEP)
    c2 = 1.0 / (1.0 - ADAM_B2 ** ADAM_STEP)

    def body(w_ref, g_ref, m_ref, v_ref, d_ref, mo_ref, vo_ref):
        gg = g_ref[...]
        mn = ADAM_B1 * m_ref[...] + (1.0 - ADAM_B1) * gg
        vn = ADAM_B2 * v_ref[...] + (1.0 - ADAM_B2) * (gg * gg)
        d_ref[...] = -ADAM_LR * ((mn * c1) / (jnp.sqrt(vn * c2) + ADAM_EPS) + ADAM_WD * w_ref[...])
        mo_ref[...] = mn
        vo_ref[...] = vn

    spec = pl.BlockSpec((tr, cols), lambda i: (i, 0))
    sds = jax.ShapeDtypeStruct((r, cols), F32)
    d, mn, vn = pl.pallas_call(body, grid=(r // tr,), in_specs=[spec] * 4, out_specs=[spec] * 3, out_shape=[sds] * 3,
                               name=name, compiler_params=_cparams("parallel"))(w2, g2, m2, v2)
    return d.reshape(shape), mn.reshape(shape), vn.reshape(shape)


def _my_pos():
    return lax.axis_index("x"), lax.axis_index("y"), lax.axis_index("c")


def _flip(pos, f):
    return tuple((1 - p) if b else p for p, b in zip(pos, f))


def _lin(pos):
    return 4 * pos[0] + 2 * pos[1] + pos[2]


def _remote_copies(src_ref, out_ref, send_sems, recv_sems, plan):
    me = _my_pos()
    copies = []
    for k, (f, sfn, dfn) in enumerate(plan):
        peer = _flip(me, f)
        copies.append(pltpu.make_async_remote_copy(
            src_ref=src_ref.at[sfn(me, peer)], dst_ref=out_ref.at[dfn(me, peer)], send_sem=send_sems.at[k],
            recv_sem=recv_sems.at[k], device_id=peer, device_id_type=MESH))
    return copies


def xchg(src, n_out, plan, name, inplace=False):
    piece = src.shape[1:]

    def body(src_ref, out_ref, send_sems, recv_sems):
        me = _my_pos()
        copies = []
        for k, (f, sfn, dfn) in enumerate(plan):
            peer = _flip(me, f)
            s_ref = (out_ref if inplace else src_ref).at[sfn(me, peer)]
            d_ref = out_ref.at[dfn(me, peer)]
            if any(f):
                cp = pltpu.make_async_remote_copy(src_ref=s_ref, dst_ref=d_ref, send_sem=send_sems.at[k],
                                                  recv_sem=recv_sems.at[k], device_id=peer, device_id_type=MESH)
            else:
                cp = pltpu.make_async_copy(s_ref, d_ref, recv_sems.at[k])
            cp.start()
            copies.append((cp, any(f)))
        for cp, remote in copies:
            if remote:
                cp.wait_recv()
            else:
                cp.wait()
        for cp, remote in copies:
            if remote:
                cp.wait_send()

    return pl.pallas_call(
        body, in_specs=[pl.BlockSpec(memory_space=pl.ANY)], out_specs=pl.BlockSpec(memory_space=pl.ANY),
        out_shape=jax.ShapeDtypeStruct((n_out,) + piece, src.dtype),
        scratch_shapes=[pltpu.SemaphoreType.DMA((len(plan),)), pltpu.SemaphoreType.DMA((len(plan),))],
        input_output_aliases={0: 0} if inplace else {}, name=name)(src)


_CHIP_FLIPS = ((1, 0, 0), (0, 1, 0), (1, 1, 0))
_ALL_FLIPS = tuple((a, b, c) for a in (0, 1) for b in (0, 1) for c in (0, 1))[1:]


def all_to_all8(src, name):
    plan = [((0, 0, 0), lambda me, peer: _lin(me), lambda me, peer: _lin(me))]
    plan += [(f, lambda me, peer: _lin(peer), lambda me, peer: _lin(me)) for f in _ALL_FLIPS]
    return xchg(src, N_DEV, plan, name)


def all_gather8(piece, name):
    plan = [((0, 0, 0), lambda me, peer: 0, lambda me, peer: _lin(me))]
    plan += [(f, lambda me, peer: 0, lambda me, peer: _lin(me)) for f in _ALL_FLIPS]
    return xchg(piece[None], N_DEV, plan, name)


def _ag_prepare(shard):
    k, ns = shard.shape
    px, py, _ = _my_pos()
    own = shard.astype(BF16)[None]
    return lax.dynamic_update_slice(jnp.zeros((4, k, ns), BF16), own, (2 * px + py, 0, 0)).reshape(8, k // 2, ns)


def _ag_plan():
    return [(f, lambda me, peer: _lin(me), lambda me, peer: _lin(me)) for f in _CHIP_FLIPS]


def _ag_finish(buf, name):
    plan = [((0, 0, 1), lambda me, peer, f=f: _lin(_flip(me, f)), lambda me, peer, f=f: _lin(_flip(me, f)))
            for f in _CHIP_FLIPS]
    _, kh, ns = buf.shape
    return xchg(buf, 8, plan, name, inplace=True).reshape(4, 2 * kh, ns)


def gather_weight(shard, name):
    k, ns = shard.shape
    buf = _ag_prepare(shard)

    def body(in_ref, out_ref, send_sems, recv_sems):
        me = _my_pos()
        sibling = _flip(me, (0, 0, 1))
        chips = [_flip(me, f) for f in _CHIP_FLIPS]

        def copy(sem, holder, to):
            rows = out_ref.at[4 * holder[0] + 2 * holder[1] + me[2]]
            return pltpu.make_async_remote_copy(src_ref=rows, dst_ref=rows, send_sem=send_sems.at[sem],
                                                recv_sem=recv_sems.at[sem], device_id=to, device_id_type=MESH)

        first = [copy(j, me, chip) for j, chip in enumerate(chips)]
        for cp in first:
            cp.start()
        passed = [copy(3 + j, chip, sibling) for j, chip in enumerate(chips)]
        for j, chip in enumerate(chips):
            copy(j, chip, me).wait_recv()
            passed[j].start()
        for j in range(3):
            passed[j].wait_recv()
        for cp in first + passed:
            cp.wait_send()

    full = pl.pallas_call(
        body, in_specs=[pl.BlockSpec(memory_space=pl.ANY)], out_specs=pl.BlockSpec(memory_space=pl.ANY),
        out_shape=jax.ShapeDtypeStruct(buf.shape, BF16),
        scratch_shapes=[pltpu.SemaphoreType.DMA((6,)), pltpu.SemaphoreType.DMA((6,))],
        input_output_aliases={0: 0}, name=name)(buf)
    return full.reshape(4, k, ns)


_RS_SLOTS = 7


def _sum_halves(g8, l1, c_idx, name):
    _, _, r, cc = g8.shape
    tr = _row_tile(r, r, cc)

    def body(c_ref, a_ref, b_ref, o_ref):
        o_ref[...] = (a_ref[...].astype(F32) + b_ref[...].astype(F32)).astype(o_ref.dtype)

    gs = pltpu.PrefetchScalarGridSpec(
        num_scalar_prefetch=1, grid=(4, r // tr),
        in_specs=[pl.BlockSpec((None, None, tr, cc), lambda s, i, c: (s, c[0], i, 0)),
                  pl.BlockSpec((None, tr, cc), lambda s, i, c: (s, i, 0))],
        out_specs=pl.BlockSpec((None, tr, cc), lambda s, i, c: (s, i, 0)))
    return pl.pallas_call(body, grid_spec=gs, out_shape=jax.ShapeDtypeStruct((_RS_SLOTS, r, cc), BF16), name=name,
                          compiler_params=_cparams("parallel", "parallel"))(c_idx, g8, l1)


def _sum_chips(buf, sc_idx, name):
    _, r, cc = buf.shape
    tr = _row_tile(r, r, cc)

    def body(s_ref, a_ref, b0_ref, b1_ref, b2_ref, o_ref):
        o_ref[...] = ((a_ref[...].astype(F32) + b0_ref[...].astype(F32)) + b1_ref[...].astype(F32)) + b2_ref[...].astype(F32)

    gs = pltpu.PrefetchScalarGridSpec(
        num_scalar_prefetch=1, grid=(r // tr,),
        in_specs=[pl.BlockSpec((None, tr, cc), lambda i, s: (s[0], i, 0))]
        + [pl.BlockSpec((None, tr, cc), lambda i, s, j=j: (4 + j, i, 0)) for j in range(3)],
        out_specs=pl.BlockSpec((None, tr, cc), lambda i, s: (s[1], i, 0)))
    return pl.pallas_call(body, grid_spec=gs, out_shape=jax.ShapeDtypeStruct((2, r, cc), F32), name=name,
                          compiler_params=_cparams("parallel"))(sc_idx, buf, buf, buf, buf)


def _rs_begin(g4, name):
    _, k, ns = g4.shape
    c_idx = jnp.reshape(_my_pos()[2], (1,)).astype(jnp.int32)
    plan1 = [((0, 0, 1), lambda me, peer, s=s: 2 * s + peer[2], lambda me, peer, s=s: s) for s in range(4)]
    l1 = xchg(g4.reshape(8, k // 2, ns), 4, plan1, name + "_rs_d2d")
    buf = _sum_halves(g4.reshape(4, 2, k // 2, ns), l1, c_idx, name + "_rs_sum2")
    plan2 = [(f, lambda me, peer: 2 * peer[0] + peer[1], lambda me, peer, j=j: 4 + j) for j, f in enumerate(_CHIP_FLIPS)]
    return buf, plan2


def _rs_finish(buf, name):
    _, kh, ns = buf.shape
    x, y, c = _my_pos()
    sc_idx = jnp.stack([2 * x + y, c]).astype(jnp.int32)
    halves = _sum_chips(buf, sc_idx, name + "_rs_sum4")
    plan3 = [((0, 0, 1), lambda me, peer: me[2], lambda me, peer: me[2])]
    return xchg(halves, 2, plan3, name + "_rs_swap", inplace=True).reshape(2 * kh, ns)


def reduce_scatter_weight(g4, name):
    buf, plan = _rs_begin(g4, name)
    return _rs_finish(xchg(buf, _RS_SLOTS, plan, name + "_rs_ici", inplace=True), name)


def _sum8(a8, name):
    _, r, cc = a8.shape
    tr = _row_tile(r, r, cc)

    def body(a_ref, o_ref):
        acc = a_ref[0]
        for j in range(1, N_DEV):
            acc = acc + a_ref[j]
        o_ref[...] = acc

    return pl.pallas_call(body, grid=(r // tr,), in_specs=[pl.BlockSpec((N_DEV, tr, cc), lambda i: (0, i, 0))],
                          out_specs=pl.BlockSpec((tr, cc), lambda i: (i, 0)), out_shape=jax.ShapeDtypeStruct((r, cc), F32),
                          name=name, compiler_params=_cparams("parallel"))(a8)


def all_reduce8(flat, name):
    n = flat.shape[0]
    unit = N_DEV * 8 * 128
    npad = -(-n // unit) * unit
    a = jnp.pad(flat, (0, npad - n)).reshape(N_DEV, npad // (N_DEV * 128), 128)
    mine = _sum8(all_to_all8(a, name + "_rs"), name + "_sum")
    return all_gather8(mine, name + "_ag").reshape(npad)[:n]


def _make_split(t, cuts):
    def pieces(qkv):
        out = []
        for rows in (slice(None, t), slice(t, None)):
            out += [qkv[rows, a:b] for a, b in zip(cuts[:-1], cuts[1:])]
        return tuple(out)

    @jax.custom_vjp
    def split(qkv):
        return pieces(qkv)

    def fwd(qkv):
        return pieces(qkv), None

    def bwd(_, g):
        n = len(cuts) - 1
        return (jnp.concatenate([jnp.concatenate(g[:n], axis=1), jnp.concatenate(g[n:], axis=1)], axis=0),)

    split.defvjp(fwd, bwd)
    return split


def _local_loss(x, ctx, target, mods, small, big, pending, shards, dims):
    t, m, d = dims["t"], dims["m"], dims["d"]
    a_w, bq_w, bkv_w = dims["a_w"], dims["bq_w"], dims["bkv_w"]
    z = jnp.concatenate([x, ctx], axis=0)
    big = dict(big)

    def grp(layer, j, n_groups=2):
        return mods[layer, :n_groups, j][:, None, :]

    def lin(kind, out_dtype, name, a, wname, gather=()):
        y, gathered = make_linear(kind, out_dtype, name)(a, big[wname], shards[wname], tuple(pending[n] for n in gather))
        big.update(zip(gather, gathered))
        return y

    def ffn(layer, a, gather_in, gather_out):
        n1, n3, n2 = f"ffn_w1_{layer}", f"ffn_w3_{layer}", f"ffn_w2_{layer}"
        gathering = (tuple(pending[n] for n in gather_in), tuple(pending[n] for n in gather_out))
        w2 = big[n2] if n2 in big else pending[n2]
        y, (got_in, got_out) = make_ffn(f"ffn{layer}", n2 not in big)(a, big[n1], big[n3], w2, shards[n1], shards[n3],
                                                                    shards[n2], gathering)
        big.update(zip(gather_in, got_in))
        big.update(zip(gather_out, got_out))
        return y

    h = make_norm_mod(t, "norm_mix0")(z, small["norm_mix"][0][None], grp(0, 1), grp(0, 0))
    qkv = lin("col", BF16, "attn_in", h, "attn_w_in", ("attn_w_out", "ffn_w1_0"))
    o3, o5 = 3 * a_w, 3 * a_w + bq_w + bkv_w
    cuts = (0, a_w, 2 * a_w, o3, o3 + bq_w, o5, o5 + bkv_w)
    qa, ka, va, qb_u, kb_u, vb, qa_c, ka_c, va_c, qb_c, kb_c, vb_c = _make_split(t, cuts)(qkv)
    qb, kb = make_rope(t, "rope_q")(qb_u), make_rope(t, "rope_k")(kb_u)
    sink = small["attn_sink"][0]
    no_sink = jnp.zeros((a_w // HEAD_DIM,), F32)
    na_meta, na_span, na_cases = _na_plan(t)
    oa = make_attention(na_meta, 1, na_span, True, False, True, "na")(
        qa, ka, va, ka_c, va_c, _na_bias(small["attn_rpb"][0], na_cases), no_sink)
    sw_meta, sw_span, sw_bias = _sw_plan(t)
    grp_b = bq_w // bkv_w
    ob = make_attention(sw_meta, grp_b, sw_span, True, True, False, "swa")(qb, kb, vb, kb_c, vb_c, jnp.asarray(sw_bias), sink)
    c_meta = np.array([[0] * (m // ATTN_BLOCK), [0] * (m // ATTN_BLOCK), [1] + [0] * (m // ATTN_BLOCK - 1)], np.int32)
    zero_bias = jnp.zeros((1, 1, ATTN_BLOCK, m), F32)
    oa_c = make_attention(c_meta, 1, m, False, False, False, "ctx_na")(qa_c, ka_c, va_c, ka_c, va_c, zero_bias, no_sink)
    ob_c = make_attention(c_meta, grp_b, m, False, True, False, "ctx_swa")(qb_c, kb_c, vb_c, kb_c, vb_c, zero_bias, sink)
    o = jnp.concatenate([jnp.concatenate([oa, ob], axis=1), jnp.concatenate([oa_c, ob_c], axis=1)], axis=0)
    y = lin("row", F32, "attn_out", o, "attn_w_out", ("ffn_w3_0",))
    z, h = make_residual_norm(t, "mix0_to_ffn0")(z, y, grp(0, 2), small["norm_ffn"][0][None], grp(0, 4), grp(0, 3))
    y = ffn(0, h, ("ssm_w_glu", "ffn_w1_1"), ("ffn_w3_1",))

    z, h = make_residual_norm(t, "ffn0_to_mix1")(z, y, grp(0, 5), small["norm_mix"][1][None], grp(1, 1), grp(1, 0))
    hx, hc = h[:t], h[t:]
    lam, wd, wr = _s5_params(small["ssm_a_re"][0], small["ssm_a_im"][0], small["ssm_log_dt"][0], small["ssm_b_re"][0],
                             small["ssm_b_im"][0], small["ssm_c_re"][0], small["ssm_c_im"][0])
    u2 = jnp.stack([_to_scan_order(jnp.concatenate([hc, hx], axis=0)), _to_scan_order(h)])
    y2 = make_s5_core("s5")(u2, lam, wd, wr)
    ys = _from_scan_order(y2[0])[m:] + _from_scan_order(y2[1])[:t]
    gl = make_gelu_in("gelu")(hx, ys, small["ssm_d_full"][None])
    zz = lin("col", F32, "glu_w", gl, "ssm_w_glu", ("ffn_w2_1",))
    yx = make_glu("glu")(zz, small["ssm_b_glu_full"][None])
    xs, h = make_residual_norm(t, "mix1_to_ffn1")(z[:t], yx, grp(1, 2, 1), small["norm_ffn"][1][None], grp(1, 4, 1), grp(1, 3, 1))
    xs = make_gated_residual(t, "res_ffn1")(xs, ffn(1, h, (), ()), grp(1, 5, 1))
    return make_final_loss("loss_head")(xs, small["norm_final"][None], target)[0, 0]


_WEIGHTS = ['c_ctx', 'ada_w', 'ada_b', 'norm_mix', 'norm_ffn', 'ffn_w1', 'ffn_w3', 'ffn_w2', 'attn_w_in', 'attn_w_out',
            'attn_rpb', 'attn_sink', 'ssm_a_re', 'ssm_a_im', 'ssm_log_dt', 'ssm_b_re', 'ssm_b_im', 'ssm_c_re', 'ssm_c_im',
            'ssm_d', 'ssm_w_glu', 'ssm_b_glu', 'norm_final']
_LOCAL_SMALL = ['norm_mix', 'norm_ffn', 'attn_rpb', 'attn_sink', 'ssm_a_re', 'ssm_a_im', 'ssm_log_dt', 'ssm_b_re',
                'ssm_b_im', 'ssm_c_re', 'ssm_c_im', 'norm_final']
_MOD_ROWS = 16


def _gather_chip_vector(v, name):
    g = all_gather8(v[None], name)
    return g[0::2, 0, :].reshape(-1)


def kernel(x, c, ctx, c_ctx, ada_w, ada_b, norm_mix, norm_ffn, ffn_w1, ffn_w3, ffn_w2, attn_w_in, attn_w_out, attn_rpb, attn_sink, ssm_a_re, ssm_a_im, ssm_log_dt, ssm_b_re, ssm_b_im, ssm_c_re, ssm_c_im, ssm_d, ssm_w_glu, ssm_b_glu, norm_final, loss_target, m_c_ctx, m_ada_w, m_ada_b, m_norm_mix, m_norm_ffn, m_ffn_w1, m_ffn_w3, m_ffn_w2, m_attn_w_in, m_attn_w_out, m_attn_rpb, m_attn_sink, m_ssm_a_re, m_ssm_a_im, m_ssm_log_dt, m_ssm_b_re, m_ssm_b_im, m_ssm_c_re, m_ssm_c_im, m_ssm_d, m_ssm_w_glu, m_ssm_b_glu, m_norm_final, v_c_ctx, v_ada_w, v_ada_b, v_norm_mix, v_norm_ffn, v_ffn_w1, v_ffn_w3, v_ffn_w2, v_attn_w_in, v_attn_w_out, v_attn_rpb, v_attn_sink, v_ssm_a_re, v_ssm_a_im, v_ssm_log_dt, v_ssm_b_re, v_ssm_b_im, v_ssm_c_re, v_ssm_c_im, v_ssm_d, v_ssm_w_glu, v_ssm_b_glu, v_norm_final):
    env = dict(locals())
    w = {n: env[n] for n in _WEIGHTS}
    mom = {n: env["m_" + n] for n in _WEIGHTS}
    var = {n: env["v_" + n] for n in _WEIGHTS}
    _, t, d = x.shape
    m = ctx.shape[1]
    px, py, pc = _my_pos()
    s_me = 2 * px + py
    a_w =attn_rpb.shape[1] * HEAD_DIM
    bq_w = attn_sink.shape[1] * HEAD_DIM
    bkv_w = (4 * attn_w_in.shape[2] - 3 * a_w - bq_w) // 2
    dims = dict(t=t, m=m, d=d, a_w=a_w, bq_w=bq_w, bkv_w=bkv_w)
    n_layers = ada_w.shape[0]
    ada_cols = ada_w.shape[2]

    big = {"attn_w_in": gather_weight(attn_w_in[0], "ag_attn_in")}
    pending = {"attn_w_out": _ag_prepare(attn_w_out[0]), "ssm_w_glu": _ag_prepare(ssm_w_glu[0])}
    for l in range(n_layers):
        pending.update({f"ffn_w1_{l}": _ag_prepare(ffn_w1[l]), f"ffn_w3_{l}": _ag_prepare(ffn_w3[l]),
                        f"ffn_w2_{l}": _ag_prepare(ffn_w2[l])})
    small ={n: w[n] for n in _LOCAL_SMALL}
    small["ssm_d_full"] = _gather_chip_vector(ssm_d[0], "ag_ssm_d")
    small["ssm_b_glu_full"] = _gather_chip_vector(ssm_b_glu[0], "ag_b_glu")

    c_all = all_gather8(c, "ag_c")[:, 0, :]
    cond = jnp.concatenate([c_all, c_ctx[None], jnp.zeros((_MOD_ROWS - N_DEV - 1, d), F32)], axis=0)
    sig = jax.nn.sigmoid(cond)
    silu_c = (cond * sig).astype(BF16)
    mods_shard = mm_nn(silu_c, ada_w, "col", F32, "ada_fwd").reshape(_MOD_ROWS, n_layers, ada_cols).transpose(1, 0, 2)
    send = jnp.stack([jnp.stack([mods_shard[:, tgt], mods_shard[:, N_DEV]], axis=1).reshape(2 * n_layers, ada_cols)
                      for tgt in range(N_DEV)])
    plan = [((0, 0, 0), lambda me, peer: _lin(me), lambda me, peer: 2 * me[0] + me[1])]
    plan += [(f, lambda me, peer: _lin(peer), lambda me, peer: 2 * me[0] + me[1]) for f in _CHIP_FLIPS]
    got = xchg(send, 4, plan, "mods_xchg")
    mods = got.reshape(4, n_layers, 2, ada_cols).transpose(1, 2, 0, 3).reshape(n_layers, 2, 4 * ada_cols)
    mods = (mods + ada_b[:, None, :]).reshape(n_layers, 2, 6, d)

    shards = {"attn_w_in": attn_w_in[0], "attn_w_out": attn_w_out[0], "ssm_w_glu": ssm_w_glu[0]}
    for l in range(n_layers):
        shards.update({f"ffn_w1_{l}": ffn_w1[l], f"ffn_w3_{l}": ffn_w3[l], f"ffn_w2_{l}": ffn_w2[l]})

    def local(xx, mods_, small_, shards_):
        return _local_loss(xx, ctx[0], loss_target[0], mods_, small_, big, pending, shards_, dims)

    loss_local, vjp = jax.vjp(local, x[0], mods, small, shards)
    g_x, g_mods, g_small, g_shards = vjp(jnp.ones((), F32))
    loss = lax.psum(loss_local, ("x", "y", "c"))
    grads = {"attn_w_in": g_shards["attn_w_in"][None], "attn_w_out": g_shards["attn_w_out"][None],
             "ssm_w_glu": g_shards["ssm_w_glu"][None]}
    for n in ("ffn_w1", "ffn_w3", "ffn_w2"):
        grads[n] = jnp.stack([g_shards[f"{n}_{l}"] for l in range(n_layers)])

    gm = all_gather8(g_mods.reshape(2 * n_layers, 6 * d), "ag_dmods").reshape(N_DEV, n_layers, 2, 6 * d)
    ctx_row = gm[0, :, 1]
    for j in range(1, N_DEV):
        ctx_row = ctx_row + gm[j, :, 1]
    dm16 = jnp.concatenate([gm[:, :, 0].transpose(1, 0, 2), ctx_row[:, None], jnp.zeros((n_layers, _MOD_ROWS - N_DEV - 1, 6 * d), F32)], axis=1)
    grads["ada_b"] = jnp.sum(dm16, axis=1)
    dm_mine = lax.dynamic_slice_in_dim(dm16, s_me * ada_cols, ada_cols, axis=2).astype(BF16)
    grads["ada_w"] = jnp.stack([mm_tn(silu_c, dm_mine[l], (1, d, ada_cols), "col", F32, f"ada_dw{l}")[0] for l in range(n_layers)])
    dsilu = mm_nt(dm_mine.transpose(1, 0, 2).reshape(_MOD_ROWS, n_layers * ada_cols), ada_w, "col", F32, "ada_dc")
    dsilu_ctx = 0.5 * dsilu[N_DEV]

    packed = [(n, g_small[n]) for n in _LOCAL_SMALL] + [("ssm_d", g_small["ssm_d_full"]), ("ssm_b_glu", g_small["ssm_b_glu_full"]),
                                                        ("c_ctx", dsilu_ctx)]
    flat = all_reduce8(jnp.concatenate([a.reshape(-1) for _, a in packed]), "ar_small")
    off = 0
    for n, a in packed:
        grads[n] = flat[off:off + a.size].reshape(a.shape)
        off += a.size
    sig_ctx = jax.nn.sigmoid(c_ctx)
    grads["c_ctx"] = grads["c_ctx"] * (sig_ctx * (1.0 + c_ctx * (1.0 - sig_ctx)))
    grads["ssm_d"] = lax.dynamic_slice_in_dim(grads["ssm_d"], s_me * ssm_d.shape[1], ssm_d.shape[1])[None]
    grads["ssm_b_glu"] = lax.dynamic_slice_in_dim(grads["ssm_b_glu"], s_me * ssm_b_glu.shape[1], ssm_b_glu.shape[1])[None]

    delta, new_m, new_v = {}, {}, {}
    for n in _WEIGHTS:
        delta[n], new_m[n], new_v[n] = adamw(w[n], grads[n], mom[n], var[n], "adamw_" + n)
    return (loss, g_x[None], *[grads[n] for n in _WEIGHTS], *[delta[n] for n in _WEIGHTS],
            *[new_m[n] for n in _WEIGHTS], *[new_v[n] for n in _WEIGHTS])
```

```python
import functools
import math

import numpy as np
import jax
import jax.numpy as jnp
from jax import lax
from jax.experimental import pallas as pl
from jax.experimental.pallas import tpu as pltpu

F32 = jnp.float32
BF16 = jnp.bfloat16
MESH = pl.DeviceIdType.MESH

HEAD_DIM = 128
GRID_W = 64
NA_ROWS = 8
NA_COLS = 16
SW_RADIUS = 128
ATTN_BLOCK = 128
ROPE_BASE = 10000.0
SSM_GROUP = 16
SSM_STATE = 64
SSM_TILE_GROUPS = 8
SCAN_BLOCKS = 8
EPS = 1e-6
NEG_INF = -1e30
ADAM_LR, ADAM_B1, ADAM_B2, ADAM_EPS, ADAM_WD, ADAM_STEP = 0.001, 0.9, 0.999, 1e-08, 0.01, 10
VMEM_LIMIT_BYTES = 56 * 1024 * 1024
N_DEV = 8


def _cparams(*sem):
    return pltpu.CompilerParams(dimension_semantics=tuple(sem) if sem else None, vmem_limit_bytes=VMEM_LIMIT_BYTES)


def _pick(n, cands):
    for c in cands:
        if n % c == 0:
            return c
    return n


def _dot_nn(a, b):
    return jnp.dot(a, b, preferred_element_type=F32)


def _dot_nt(a, b):
    return lax.dot_general(a, b, (((1,), (1,)), ((), ())), preferred_element_type=F32)


def _dot_tn(a, b):
    return lax.dot_general(a, b, (((0,), (0,)), ((), ())), preferred_element_type=F32)


def _mm_call(name, grid, ins, in_specs, o_specs, out_sds, acc_shape, step, carry=None):
    nk = grid[2]
    nb = 0 if carry is None else len(carry)
    ni, no = len(ins), len(out_sds)
    assert nk == 1 or no == 1

    def body(*refs):
        in_refs = refs[:ni]
        o_refs = refs[ni + nb:ni + nb + no]
        rest = refs[ni + nb + no + nb:]
        acc_ref = rest[0] if nk > 1 else None
        o_ref = o_refs[0]
        if carry is not None:
            bufs = refs[ni + nb + no:ni + nb + no + nb]
            sems = rest[-2 * nb:]
            ids = [pl.program_id(ax) for ax in range(3)]
            first = (ids[0] == 0) & (ids[1] == 0) & (ids[2] == 0)
            last = (ids[0] == grid[0] - 1) & (ids[1] == grid[1] - 1) & (ids[2] == grid[2] - 1)
            copies = []
            for q, (_, plan) in enumerate(carry):
                copies += _remote_copies(bufs[q], bufs[q], sems[2 * q], sems[2 * q + 1], plan)

            @pl.when(first)
            def _():
                for cp in copies:
                    cp.start()

        kk = pl.program_id(2)
        if nk == 1:
            for ref, val in zip(o_refs, step(*in_refs)):
                ref[...] = val.astype(ref.dtype)
        else:
            @pl.when(kk == 0)
            def _():
                acc_ref[...] = step(*in_refs)[0]

            @pl.when(kk > 0)
            def _():
                acc_ref[...] = step(*in_refs)[0] + acc_ref[...]

            @pl.when(kk == nk - 1)
            def _():
                o_ref[...] = acc_ref[...].astype(o_ref.dtype)

        if carry is not None:
            @pl.when(last)
            def _():
                for cp in copies:
                    cp.wait_recv()
                for cp in copies:
                    cp.wait_send()

    scratch = [pltpu.VMEM(acc_shape, F32)] if nk > 1 else []
    if carry is None:
        return pl.pallas_call(
            body, grid=grid, in_specs=list(in_specs), out_specs=list(o_specs), out_shape=list(out_sds),
            scratch_shapes=scratch, name=name, compiler_params=_cparams("parallel", "parallel", "arbitrary"))(*ins)
    any_spec = pl.BlockSpec(memory_space=pl.ANY)
    for _, plan in carry:
        scratch += [pltpu.SemaphoreType.DMA((len(plan),)), pltpu.SemaphoreType.DMA((len(plan),))]
    return pl.pallas_call(
        body, grid=grid, in_specs=list(in_specs) + [any_spec] * nb, out_specs=list(o_specs) + [any_spec] * nb,
        out_shape=list(out_sds) + [jax.ShapeDtypeStruct(buf.shape, buf.dtype) for buf, _ in carry],
        scratch_shapes=scratch, input_output_aliases={ni + q: no + q for q in range(nb)}, name=name,
        compiler_params=_cparams("arbitrary", "arbitrary", "arbitrary"))(*ins, *[buf for buf, _ in carry])


_ROW_TILES = (768, 512, 256, 128, 64, 32, 16, 8)
_K_TILES = (2048, 1408, 1024, 512, 256, 128)
_CONTRACT_TILES = (1408, 1024, 768, 512, 256, 128)
_WEIGHT_BLOCK_BYTES = 12 * 1024 * 1024
_FUSED_ROW_TILES =(528, 512, 384, 256, 128, 64, 32, 16, 8)


def mm_nn(a, w3, kind, out_dtype, name, carry=None):
    r = a.shape[0]
    s, d1, d2 = w3.shape
    tm = _pick(r, _ROW_TILES)
    if kind == "col":
        tn = d2 if d1 * d2 * w3.dtype.itemsize <= _WEIGHT_BLOCK_BYTES else _pick(d2, (1024, 512, 256, 128))
        nn = d2 // tn
        grid = (s * nn, r // tm, 1)
        a_spec = pl.BlockSpec((tm, d1), lambda j, i, k: (i, 0))
        b_spec = pl.BlockSpec((None, d1, tn), lambda j, i, k: (j // nn, 0, j % nn))
        o_spec = pl.BlockSpec((tm, tn), lambda j, i, k: (i, j))
        n = s * d2

        def step(a_ref, b_ref):
            return (_dot_nn(a_ref[...], b_ref[...].astype(a_ref.dtype)),)
    else:
        tn = d2 if d2 <= 1024 else _pick(d2, (1024, 512, 256, 128))
        grid = (d2 // tn, r // tm, 1)
        a_spec = pl.BlockSpec((tm, s * d1), lambda j, i, k: (i, 0))
        b_spec = pl.BlockSpec((s, d1, tn), lambda j, i, k: (0, 0, j))
        o_spec = pl.BlockSpec((tm, tn), lambda j, i, k: (i, j))
        n = d2

        def step(a_ref, b_ref):
            p = _dot_nn(a_ref[:, :d1], b_ref[0])
            for q in range(1, s):
                p = _dot_nn(a_ref[:, q * d1:(q + 1) * d1], b_ref[q]) + p
            return (p,)
    out = _mm_call(name, grid, [a, w3], [a_spec, b_spec], [o_spec], [jax.ShapeDtypeStruct((r, n), out_dtype)], None, step, carry)
    return out if carry else out[0]


def mm_nt(dy, w3, kind, out_dtype, name, carry=None, addend=None):
    r = dy.shape[0]
    s, d1, d2 = w3.shape
    tm = _pick(r, _ROW_TILES)
    if kind == "col":
        fits = [c for c in (1024, 512, 256, 128) if d1 % c == 0 and s * c * d2 * w3.dtype.itemsize <= _WEIGHT_BLOCK_BYTES]
        tko = d1 if d1 <= 1024 and s * d1 * d2 * w3.dtype.itemsize <= _WEIGHT_BLOCK_BYTES else fits[0]
        grid = (d1 // tko, r // tm, 1)
        a_spec = pl.BlockSpec((tm, s * d2), lambda j, i, k: (i, 0))
        b_spec = pl.BlockSpec((s, tko, d2), lambda j, i, k: (0, j, 0))
        o_spec = pl.BlockSpec((tm, tko), lambda j, i, k: (i, j))
        kdim = d1

        def step(a_ref, b_ref, *more):
            p = _dot_nt(a_ref[:, :d2], b_ref[0].astype(a_ref.dtype))
            for q in range(1, s):
                p = _dot_nt(a_ref[:, q * d2:(q + 1) * d2], b_ref[q].astype(a_ref.dtype)) + p
            return (p + more[0][...].astype(F32),) if more else (p,)
    else:
        grid = (s, r // tm, 1)
        a_spec = pl.BlockSpec((tm, d2), lambda j, i, k: (i, 0))
        b_spec = pl.BlockSpec((None, d1, d2), lambda j, i, k: (j, 0, 0))
        o_spec = pl.BlockSpec((tm, d1), lambda j, i, k: (i, j))
        kdim = s * d1

        def step(a_ref, b_ref, *more):
            p = _dot_nt(a_ref[...], b_ref[...])
            return (p + more[0][...].astype(F32),) if more else (p,)
    ins, specs = [dy, w3], [a_spec, b_spec]
    if addend is not None:
        ins, specs = ins + [addend], specs + [o_spec]
    out = _mm_call(name, grid, ins, specs, [o_spec], [jax.ShapeDtypeStruct((r, kdim), out_dtype)], None, step, carry)
    return out if carry else out[0]


def mm_tn(a, dy, w_shape, kind, out_dtype, name):
    s, d1, d2 = w_shape
    r = dy.shape[0]
    tr = _pick(r, _CONTRACT_TILES)
    if kind == "col":
        tkk = d1 if d1 <= 1024 else _pick(d1, (1024, 512, 256, 128))
        grid = (s * (d1 // tkk), 1, r // tr)
        nkk = d1 // tkk
        a_spec = pl.BlockSpec((tr, tkk), lambda j, i, k: (k, j % nkk))
        b_spec = pl.BlockSpec((tr, d2), lambda j, i, k: (k, j // nkk))
        o_spec = pl.BlockSpec((None, tkk, d2), lambda j, i, k: (j // nkk, j % nkk, 0))
        acc = (tkk, d2)
    else:
        tn = d2 if d2 <= 1024 else _pick(d2, (1024, 512, 256, 128))
        nn = d2 // tn
        grid = (s * nn, 1, r // tr)
        a_spec = pl.BlockSpec((tr, d1), lambda j, i, k: (k, j // nn))
        b_spec = pl.BlockSpec((tr, tn), lambda j, i, k: (k, j % nn))
        o_spec = pl.BlockSpec((None, d1, tn), lambda j, i, k: (j // nn, 0, j % nn))
        acc = (d1, tn)

    def step(a_ref, b_ref):
        return (_dot_tn(a_ref[...], b_ref[...]),)
    return _mm_call(name, grid, [a, dy], [a_spec, b_spec], [o_spec], [jax.ShapeDtypeStruct(w_shape, out_dtype)], acc, step)[0]


def mm_ffn_in(h, w1, w3, name, carry=None):
    r = h.shape[0]
    s, d1, d2 = w1.shape
    tm = _pick(r, _FUSED_ROW_TILES)
    a_spec = pl.BlockSpec((tm, d1), lambda j, i, k: (i, 0))
    b_spec = pl.BlockSpec((None, d1, d2), lambda j, i, k: (j, 0, 0))
    o_spec = pl.BlockSpec((tm, d2), lambda j, i, k: (i, j))
    sds = jax.ShapeDtypeStruct((r, s * d2), BF16)

    def step(a_ref, b1_ref, b3_ref):
        p1 = _dot_nn(a_ref[...], b1_ref[...])
        p3 = _dot_nn(a_ref[...], b3_ref[...])
        return p1, p3, _silu(p1) * p3

    return _mm_call(name, (s, r // tm, 1), [h, w1, w3], [a_spec, b_spec, b_spec], [o_spec] * 3, [sds] * 3, None, step, carry)


def mm_ffn_back(dy, w2, h1, h3, name, carry=None):
    r = dy.shape[0]
    s, d1, d2 = w2.shape
    tm = _pick(r, _FUSED_ROW_TILES)
    a_spec = pl.BlockSpec((tm, d2), lambda j, i, k: (i, 0))
    b_spec = pl.BlockSpec((None, d1, d2), lambda j, i, k: (j, 0, 0))
    o_spec = pl.BlockSpec((tm, d1), lambda j, i, k: (i, j))
    sds = jax.ShapeDtypeStruct((r, s * d1), BF16)

    def step(a_ref, b_ref, h1_ref, h3_ref):
        g = _dot_nt(a_ref[...], b_ref[...])
        a1 = h1_ref[...].astype(F32)
        sg = jax.nn.sigmoid(a1)
        return g * h3_ref[...].astype(F32) * (sg * (1.0 + a1 * (1.0 - sg))), g * a1 * sg

    return _mm_call(name, (s, r // tm, 1), [dy, w2, h1, h3], [a_spec, b_spec, o_spec, o_spec], [o_spec] * 2, [sds] * 2,
                    None, step, carry)


def make_linear(kind, out_dtype, name):
    def run(a, w3, gathering):
        carry = [(buf, _ag_plan()) for buf in gathering] or None
        out = mm_nn(a, w3, kind, out_dtype, name + "_fwd", carry)
        if not carry:
            return out, ()
        return out[0], _ag_finish_all(out[1:], name)

    @jax.custom_vjp
    def linear(a, w3, w_shard, gathering):
        return run(a, w3, gathering)

    def fwd(a, w3, w_shard, gathering):
        return run(a, w3, gathering), (a, w3, w_shard.shape, len(gathering))

    def bwd(res, cts):
        a, w3, shard_shape, n_gathering = res
        dyb = cts[0].astype(BF16)
        dw = mm_tn(a, dyb, w3.shape, kind, BF16, name + "_dw")
        buf, plan = _rs_begin(dw, name)
        da, buf = mm_nt(dyb, w3, kind, a.dtype, name + "_dx", carry=[(buf, plan)])
        return da, None, _rs_finish(buf, name).reshape(shard_shape), (None,) * n_gathering

    linear.defvjp(fwd, bwd)
    return linear


def _ag_finish_all(bufs, name):
    return tuple(_ag_finish(buf, f"{name}_gathered{q}") for q, buf in enumerate(bufs))


def make_ffn(name, w2_pending):
    def run(h, w1, w3, w2, gathering):
        g_in, g_out = gathering
        if w2_pending:
            g_in = (w2,) + tuple(g_in)
        carry = [(buf, _ag_plan()) for buf in g_in] or None
        out = mm_ffn_in(h, w1, w3, name + "_in", carry)
        h1, h3, act = out[:3]
        done_in = _ag_finish_all(out[3:], name + "_in")
        if w2_pending:
            w2, done_in = done_in[0], done_in[1:]
        carry2 = [(buf, _ag_plan()) for buf in g_out] or None
        y = mm_nn(act, w2, "row", F32, name + "_out", carry2)
        done = (done_in, _ag_finish_all(y[1:], name + "_out") if carry2 else ())
        return (y[0] if carry2 else y), done, (h1, h3, act, w2)

    @jax.custom_vjp
    def ffn(h, w1, w3, w2, s1, s3, s2, gathering):
        return run(h, w1, w3, w2, gathering)[:2]

    def fwd(h, w1, w3, w2, s1, s3, s2, gathering):
        y, done, (h1, h3, act, w2) = run(h, w1, w3, w2, gathering)
        return (y, done), (h, w1, w3, w2, h1, h3, act, s1.shape, s3.shape, s2.shape, tuple(len(g) for g in gathering))

    def bwd(res, cts):
        h, w1, w3, w2, h1, h3, act, shape1, shape3, shape2, n_gathering = res
        dyb = cts[0].astype(BF16)
        buf2, plan2 = _rs_begin(mm_tn(act, dyb, w2.shape, "row", BF16, name + "_dw2"), name + "_w2")
        dh1, dh3, buf2 = mm_ffn_back(dyb, w2, h1, h3, name + "_back", carry=[(buf2, plan2)])
        buf1, plan1 = _rs_begin(mm_tn(h, dh1, w1.shape, "col", BF16, name + "_dw1"), name + "_w1")
        buf3, plan3 = _rs_begin(mm_tn(h, dh3, w3.shape, "col", BF16, name + "_dw3"), name + "_w3")
        dh, buf1 = mm_nt(dh1, w1, "col", h.dtype, name + "_dx1", carry=[(buf1, plan1)])
        dh, buf3 = mm_nt(dh3, w3, "col", h.dtype, name + "_dx3", carry=[(buf3, plan3)], addend=dh)
        grads = [_rs_finish(b, name + n).reshape(sh) for b, n, sh in
                 ((buf1, "_w1", shape1), (buf3, "_w3", shape3), (buf2, "_w2", shape2))]
        return (dh, None, None, None, *grads, tuple((None,) * n for n in n_gathering))

    ffn.defvjp(fwd, bwd)
    return ffn


def _row_tile(r, t0, d):
    cap = max(8, (2 * 1024 * 1024) // (4 * d))
    cands = [t for t in (1024, 512, 256, 128, 64, 32, 16, 8) if t <= cap]
    for t in cands:
        if r % t == 0 and t0 % t == 0:
            return t
    raise ValueError("no row tile")


def _grp_spec(d, nb0):
    return pl.BlockSpec((None, 1, d), lambda i: (i // nb0, 0, 0))


def _norm_mod_fwd(z, g, scale, shift, t0, name):
    r, d = z.shape
    tr = _row_tile(r, t0, d)
    nb0 = t0 // tr

    def body(z_ref, g_ref, sc_ref, sh_ref, o_ref):
        zz = z_ref[...]
        rstd = lax.rsqrt(jnp.mean(zz * zz, axis=-1, keepdims=True) + EPS)
        y = zz * rstd * g_ref[...]
        o_ref[...] = (y * (1.0 + sc_ref[...]) + sh_ref[...]).astype(o_ref.dtype)

    return pl.pallas_call(
        body, grid=(r // tr,),
        in_specs=[pl.BlockSpec((tr, d), lambda i: (i, 0)), pl.BlockSpec((1, d), lambda i: (0, 0)),
                  _grp_spec(d, nb0), _grp_spec(d, nb0)],
        out_specs=pl.BlockSpec((tr, d), lambda i: (i, 0)),
        out_shape=jax.ShapeDtypeStruct((r, d), BF16), name=name, compiler_params=_cparams("parallel"))(z, g, scale, shift)


def _norm_mod_bwd(z, g, scale, dh, t0, name):
    r, d = z.shape
    ng = scale.shape[0]
    tr = _row_tile(r, t0, d)
    nb0 = t0 // tr

    def body(z_ref, g_ref, sc_ref, dh_ref, dz_ref, dg_ref, dsc_ref, dsh_ref):
        i = pl.program_id(0)
        zz = z_ref[...]
        gg = g_ref[...]
        rstd = lax.rsqrt(jnp.mean(zz * zz, axis=-1, keepdims=True) + EPS)
        zhat = zz * rstd
        dhh = dh_ref[...].astype(F32)
        dy = dhh * (1.0 + sc_ref[...])
        dyg = dy * gg
        dz_ref[...] = rstd * (dyg - zhat * jnp.mean(dyg * zhat, axis=-1, keepdims=True))

        @pl.when(i == 0)
        def _():
            dg_ref[...] = jnp.zeros_like(dg_ref)

        @pl.when((i == 0) | (i == nb0))
        def _():
            dsc_ref[...] = jnp.zeros_like(dsc_ref)
            dsh_ref[...] = jnp.zeros_like(dsh_ref)

        dg_ref[...] += jnp.sum(dy * zhat, axis=0, keepdims=True)
        dsc_ref[...] += jnp.sum(dhh * (zhat * gg), axis=0, keepdims=True)
        dsh_ref[...] += jnp.sum(dhh, axis=0, keepdims=True)

    return pl.pallas_call(
        body, grid=(r // tr,),
        in_specs=[pl.BlockSpec((tr, d), lambda i: (i, 0)), pl.BlockSpec((1, d), lambda i: (0, 0)),
                  _grp_spec(d, nb0), pl.BlockSpec((tr, d), lambda i: (i, 0))],
        out_specs=[pl.BlockSpec((tr, d), lambda i: (i, 0)), pl.BlockSpec((1, d), lambda i: (0, 0)),
                   _grp_spec(d, nb0), _grp_spec(d, nb0)],
        out_shape=[jax.ShapeDtypeStruct((r, d), F32), jax.ShapeDtypeStruct((1, d), F32),
                   jax.ShapeDtypeStruct((ng, 1, d), F32), jax.ShapeDtypeStruct((ng, 1, d), F32)],
        name=name, compiler_params=_cparams("arbitrary"))(z, g, scale, dh)


def make_norm_mod(t0, name):
    @jax.custom_vjp
    def f(z, g, scale, shift):
        return _norm_mod_fwd(z, g, scale, shift, t0, name + "_fwd")

    def fwd(z, g, scale, shift):
        return _norm_mod_fwd(z, g, scale, shift, t0, name + "_fwd"), (z, g, scale)

    def bwd(res, dh):
        z, g, scale = res
        dz, dg, dsc, dsh = _norm_mod_bwd(z, g, scale, dh, t0, name + "_bwd")
        return dz, dg, dsc, dsh

    f.defvjp(fwd, bwd)
    return f


def _gated_fwd(z, y, gate, t0, name):
    r, d = z.shape
    tr = _row_tile(r, t0, d)
    nb0 = t0 // tr

    def body(z_ref, y_ref, g_ref, o_ref):
        o_ref[...] = z_ref[...] + g_ref[...] * y_ref[...].astype(F32)

    return pl.pallas_call(
        body, grid=(r // tr,),
        in_specs=[pl.BlockSpec((tr, d), lambda i: (i, 0)), pl.BlockSpec((tr, d), lambda i: (i, 0)), _grp_spec(d, nb0)],
        out_specs=pl.BlockSpec((tr, d), lambda i: (i, 0)),
        out_shape=jax.ShapeDtypeStruct((r, d), F32), name=name, compiler_params=_cparams("parallel"))(z, y, gate)


def _gated_bwd(y, gate, dzn, t0, name):
    r, d = y.shape
    ng = gate.shape[0]
    tr = _row_tile(r, t0, d)
    nb0 = t0 // tr

    def body(y_ref, g_ref, dz_ref, dy_ref, dg_ref):
        i = pl.program_id(0)
        dzz = dz_ref[...]
        dy_ref[...] = (g_ref[...] * dzz).astype(dy_ref.dtype)

        @pl.when((i == 0) | (i == nb0))
        def _():
            dg_ref[...] = jnp.zeros_like(dg_ref)

        dg_ref[...] += jnp.sum(dzz * y_ref[...].astype(F32), axis=0, keepdims=True)

    return pl.pallas_call(
        body, grid=(r // tr,),
        in_specs=[pl.BlockSpec((tr, d), lambda i: (i, 0)), _grp_spec(d, nb0), pl.BlockSpec((tr, d), lambda i: (i, 0))],
        out_specs=[pl.BlockSpec((tr, d), lambda i: (i, 0)), _grp_spec(d, nb0)],
        out_shape=[jax.ShapeDtypeStruct((r, d), y.dtype), jax.ShapeDtypeStruct((ng, 1, d), F32)],
        name=name, compiler_params=_cparams("arbitrary"))(y, gate, dzn)


def make_gated_residual(t0, name):
    @jax.custom_vjp
    def f(z, y, gate):
        return _gated_fwd(z, y, gate, t0, name + "_fwd")

    def fwd(z, y, gate):
        return _gated_fwd(z, y, gate, t0, name + "_fwd"), (y, gate)

    def bwd(res, dzn):
        y, gate = res
        dy, dgate = _gated_bwd(y, gate, dzn, t0, name + "_bwd")
        return dzn, dy, dgate

    f.defvjp(fwd, bwd)
    return f


def make_residual_norm(t0, name):
    def row_spec(tr, d):
        return pl.BlockSpec((tr, d), lambda i: (i, 0))

    def fwd_call(z, y, gate, g, scale, shift):
        r, d = z.shape
        tr = _row_tile(r, t0, d)
        nb0 = t0 // tr

        def body(z_ref, y_ref, gt_ref, g_ref, sc_ref, sh_ref, zn_ref, h_ref):
            zz = z_ref[...] + gt_ref[...] * y_ref[...].astype(F32)
            zn_ref[...] = zz
            rstd = lax.rsqrt(jnp.mean(zz * zz, axis=-1, keepdims=True) + EPS)
            h_ref[...] = (zz * rstd * g_ref[...] * (1.0 + sc_ref[...]) + sh_ref[...]).astype(h_ref.dtype)

        grp = _grp_spec(d, nb0)
        return pl.pallas_call(
            body, grid=(r // tr,),
            in_specs=[row_spec(tr, d), row_spec(tr, d), grp, pl.BlockSpec((1, d), lambda i: (0, 0)), grp, grp],
            out_specs=[row_spec(tr, d), row_spec(tr, d)],
            out_shape=[jax.ShapeDtypeStruct((r, d), F32), jax.ShapeDtypeStruct((r, d), BF16)],
            name=name + "_fwd", compiler_params=_cparams("parallel"))(z, y, gate, g, scale, shift)

    def bwd_call(zn, y, gate, g, scale, dzn, dh):
        r, d = zn.shape
        ng = scale.shape[0]
        tr = _row_tile(r, t0, d)
        nb0 = t0 // tr

        def body(zn_ref, y_ref, gt_ref, g_ref, sc_ref, dzn_ref, dh_ref, dz_ref, dy_ref, dgt_ref, dg_ref, dsc_ref, dsh_ref):
            i = pl.program_id(0)
            zz = zn_ref[...]
            gg = g_ref[...]
            rstd = lax.rsqrt(jnp.mean(zz * zz, axis=-1, keepdims=True) + EPS)
            zhat = zz * rstd
            dhh = dh_ref[...].astype(F32)
            dyn = dhh * (1.0 + sc_ref[...])
            dyg = dyn * gg
            dz = dzn_ref[...] + rstd * (dyg - zhat * jnp.mean(dyg * zhat, axis=-1, keepdims=True))
            dz_ref[...] = dz
            dy_ref[...] = (gt_ref[...] * dz).astype(dy_ref.dtype)

            @pl.when(i == 0)
            def _():
                dg_ref[...] = jnp.zeros_like(dg_ref)

            @pl.when((i == 0) | (i == nb0))
            def _():
                dgt_ref[...] = jnp.zeros_like(dgt_ref)
                dsc_ref[...] = jnp.zeros_like(dsc_ref)
                dsh_ref[...] = jnp.zeros_like(dsh_ref)

            dgt_ref[...] += jnp.sum(dz * y_ref[...].astype(F32), axis=0, keepdims=True)
            dg_ref[...] += jnp.sum(dyn * zhat, axis=0, keepdims=True)
            dsc_ref[...] += jnp.sum(dhh * (zhat * gg), axis=0, keepdims=True)
            dsh_ref[...] += jnp.sum(dhh, axis=0, keepdims=True)

        grp = _grp_spec(d, nb0)
        vec = pl.BlockSpec((1, d), lambda i: (0, 0))
        gsds = jax.ShapeDtypeStruct((ng, 1, d), F32)
        return pl.pallas_call(
            body, grid=(r // tr,),
            in_specs=[row_spec(tr, d), row_spec(tr, d), grp, vec, grp, row_spec(tr, d), row_spec(tr, d)],
            out_specs=[row_spec(tr, d), row_spec(tr, d), grp, vec, grp, grp],
            out_shape=[jax.ShapeDtypeStruct((r, d), F32), jax.ShapeDtypeStruct((r, d), y.dtype), gsds,
                       jax.ShapeDtypeStruct((1, d), F32), gsds, gsds],
            name=name + "_bwd", compiler_params=_cparams("arbitrary"))(zn, y, gate, g, scale, dzn, dh)

    @jax.custom_vjp
    def f(z, y, gate, g, scale, shift):
        return tuple(fwd_call(z, y, gate, g, scale, shift))

    def fwd(z, y, gate, g, scale, shift):
        zn, h = fwd_call(z, y, gate, g, scale, shift)
        return (zn, h), (zn, y, gate, g, scale)

    def bwd(res, cts):
        zn, y, gate, g, scale = res
        dz, dy, dgate, dg, dsc, dsh = bwd_call(zn, y, gate, g, scale, cts[0], cts[1])
        return dz, dy, dgate, dg, dsc, dsh

    f.defvjp(fwd, bwd)
    return f


def _ew_call(name, body, ins, outs_sds, r, widths_in, widths_out, tr, extra_in=(), extra_specs=(), sem="parallel"):
    in_specs = [pl.BlockSpec((tr, w), lambda i: (i, 0)) for w in widths_in] + list(extra_specs)
    out_specs = [pl.BlockSpec((tr, w), lambda i: (i, 0)) if w is not None else pl.BlockSpec(s.shape, lambda i: (0,) * len(s.shape))
                 for w, s in zip(widths_out, outs_sds)]
    return pl.pallas_call(body, grid=(r // tr,), in_specs=in_specs, out_specs=out_specs, out_shape=outs_sds,
                          name=name, compiler_params=_cparams(sem))(*ins, *extra_in)


def _silu(x):
    return x * jax.nn.sigmoid(x)


_GELU_C = math.sqrt(2.0 / math.pi)


def _gelu_and_grad(y):
    inner = _GELU_C * (y + 0.044715 * y * y * y)
    t = jnp.tanh(inner)
    val = 0.5 * y * (1.0 + t)
    grad = 0.5 * (1.0 + t) + 0.5 * y * (1.0 - t * t) * _GELU_C * (1.0 + 3 * 0.044715 * y * y)
    return val, grad


def make_gelu_in(name):
    def fwd_call(u, ys, dsk):
        r, d = u.shape
        tr = _row_tile(r, r, d)

        def body(u_ref, y_ref, d_ref, o_ref):
            y = d_ref[...] * u_ref[...].astype(F32) + y_ref[...]
            o_ref[...] = _gelu_and_grad(y)[0].astype(o_ref.dtype)

        return _ew_call(name + "_fwd", body, (u, ys), [jax.ShapeDtypeStruct((r, d), BF16)], r, (d, d), (d,), tr,
                        extra_in=(dsk,), extra_specs=(pl.BlockSpec((1, d), lambda i: (0, 0)),))[0]

    @jax.custom_vjp
    def f(u, ys, dsk):
        return fwd_call(u, ys, dsk)

    def fwd(u, ys, dsk):
        return fwd_call(u, ys, dsk), (u, ys, dsk)

    def bwd(res, dg):
        u, ys, dsk = res
        r, d = u.shape
        tr = _row_tile(r, r, d)

        def body(u_ref, y_ref, dg_ref, d_ref, du_ref, dy_ref, dd_ref):
            i = pl.program_id(0)
            uu = u_ref[...].astype(F32)
            y = d_ref[...] * uu + y_ref[...]
            dy = dg_ref[...].astype(F32) * _gelu_and_grad(y)[1]
            dy_ref[...] = dy
            du_ref[...] = (d_ref[...] * dy).astype(du_ref.dtype)

            @pl.when(i == 0)
            def _():
                dd_ref[...] = jnp.zeros_like(dd_ref)

            dd_ref[...] += jnp.sum(dy * uu, axis=0, keepdims=True)

        outs = [jax.ShapeDtypeStruct((r, d), u.dtype), jax.ShapeDtypeStruct((r, d), F32), jax.ShapeDtypeStruct((1, d), F32)]
        du, dy, dd = _ew_call(name + "_bwd", body, (u, ys, dg), outs, r, (d, d, d), (d, d, None), tr,
                              extra_in=(dsk,), extra_specs=(pl.BlockSpec((1, d), lambda i: (0, 0)),), sem="arbitrary")
        return du, dy, dd

    f.defvjp(fwd, bwd)
    return f


def make_glu(name):
    def fwd_call(z, b):
        r, d2 = z.shape
        d = d2 // 2
        tr = _row_tile(r, r, d2)

        def body(z_ref, b_ref, o_ref):
            zz = z_ref[...].astype(F32) + b_ref[...]
            o_ref[...] = zz[:, :d] * jax.nn.sigmoid(zz[:, d:])

        return _ew_call(name + "_fwd", body, (z,), [jax.ShapeDtypeStruct((r, d), F32)], r, (d2,), (d,), tr,
                        extra_in=(b,), extra_specs=(pl.BlockSpec((1, d2), lambda i: (0, 0)),))[0]

    @jax.custom_vjp
    def f(z, b):
        return fwd_call(z, b)

    def fwd(z, b):
        return fwd_call(z, b), (z, b)

    def bwd(res, do):
        z, b = res
        r, d2 = z.shape
        d = d2 // 2
        tr = _row_tile(r, r, d2)

        def body(z_ref, do_ref, b_ref, dz_ref, db_ref):
            i = pl.program_id(0)
            zz = z_ref[...].astype(F32) + b_ref[...]
            sg = jax.nn.sigmoid(zz[:, d:])
            g = do_ref[...]
            dza = g * sg
            dzb = g * zz[:, :d] * sg * (1.0 - sg)
            dz_ref[:, :d] = dza.astype(dz_ref.dtype)
            dz_ref[:, d:] = dzb.astype(dz_ref.dtype)

            @pl.when(i == 0)
            def _():
                db_ref[...] = jnp.zeros_like(db_ref)

            db_ref[:, :d] += jnp.sum(dza, axis=0, keepdims=True)
            db_ref[:, d:] += jnp.sum(dzb, axis=0, keepdims=True)

        outs = [jax.ShapeDtypeStruct((r, d2), z.dtype), jax.ShapeDtypeStruct((1, d2), F32)]
        dz, db = _ew_call(name + "_bwd", body, (z, do), outs, r, (d2, d), (d2, None), tr,
                          extra_in=(b,), extra_specs=(pl.BlockSpec((1, d2), lambda i: (0, 0)),), sem="arbitrary")
        return dz, db

    f.defvjp(fwd, bwd)
    return f


def make_final_loss(name):
    def call(z, g, target):
        r, d = z.shape
        tr = _row_tile(r, r, d)

        def body(z_ref, t_ref, g_ref, dz_ref, dg_ref, l_ref):
            i = pl.program_id(0)
            zz = z_ref[...]
            gg = g_ref[...]
            rstd = lax.rsqrt(jnp.mean(zz * zz, axis=-1, keepdims=True) + EPS)
            zhat = zz * rstd
            e = zhat * gg - t_ref[...]
            dy = e * (1.0 / d)
            dyg = dy * gg
            dz_ref[...] = rstd * (dyg - zhat * jnp.mean(dyg * zhat, axis=-1, keepdims=True))

            @pl.when(i == 0)
            def _():
                dg_ref[...] = jnp.zeros_like(dg_ref)
                l_ref[...] = jnp.zeros_like(l_ref)

            dg_ref[...] += jnp.sum(dy * zhat, axis=0, keepdims=True)
            l_ref[...] += jnp.sum(jnp.sum(e * e, axis=1, keepdims=True), axis=0, keepdims=True) * (0.5 / d)

        outs = [jax.ShapeDtypeStruct((r, d), F32), jax.ShapeDtypeStruct((1, d), F32), jax.ShapeDtypeStruct((1, 1), F32)]
        return _ew_call(name, body, (z, target), outs, r, (d, d), (d, None, None), tr,
                        extra_in=(g,), extra_specs=(pl.BlockSpec((1, d), lambda i: (0, 0)),), sem="arbitrary")

    @jax.custom_vjp
    def f(z, g, target):
        return call(z, g, target)[2]

    def fwd(z, g, target):
        dz, dg, loss = call(z, g, target)
        return loss, (dz, dg)

    def bwd(res, dl):
        dz, dg = res
        s = dl[0, 0]
        return dz * s, dg * s, None

    f.defvjp(fwd, bwd)
    return f


def _rope_tables(t):
    quarter = HEAD_DIM // 4
    inv_freq = ROPE_BASE ** (-np.arange(quarter, dtype=np.float64) / quarter)
    pos = np.arange(t)
    ang_r = (pos // GRID_W)[:, None] * inv_freq[None, :]
    ang_c = (pos % GRID_W)[:, None] * inv_freq[None, :]
    cos = np.concatenate([np.cos(ang_r), np.cos(ang_r), np.cos(ang_c), np.cos(ang_c)], axis=1)
    sin = np.concatenate([-np.sin(ang_r), np.sin(ang_r), -np.sin(ang_c), np.sin(ang_c)], axis=1)
    return jnp.asarray(cos, F32), jnp.asarray(sin, F32)


def _rope_call(x, cos, sin, name):
    t, w = x.shape
    tr = _pick(t, (512, 256, 128, 64))
    quarter = HEAD_DIM // 4

    def body(x_ref, c_ref, s_ref, o_ref):
        xx = x_ref[...].astype(F32)
        lane = lax.broadcasted_iota(jnp.int32, xx.shape, 1)
        first = (lane % (2 * quarter)) < quarter
        partner = jnp.where(first, pltpu.roll(xx, HEAD_DIM - quarter, 1), pltpu.roll(xx, quarter, 1))
        o_ref[...] = (xx * c_ref[...] + partner * s_ref[...]).astype(o_ref.dtype)

    return pl.pallas_call(
        body, grid=(t // tr, w // HEAD_DIM),
        in_specs=[pl.BlockSpec((tr, HEAD_DIM), lambda i, j: (i, j)), pl.BlockSpec((tr, HEAD_DIM), lambda i, j: (i, 0)),
                  pl.BlockSpec((tr, HEAD_DIM), lambda i, j: (i, 0))],
        out_specs=pl.BlockSpec((tr, HEAD_DIM), lambda i, j: (i, j)),
        out_shape=jax.ShapeDtypeStruct((t, w), x.dtype), name=name, compiler_params=_cparams("parallel", "parallel"))(x, cos, sin)


def make_rope(t, name):
    cos, sin = _rope_tables(t)

    @jax.custom_vjp
    def f(x):
        return _rope_call(x, cos, sin, name + "_fwd")

    def fwd(x):
        return _rope_call(x, cos, sin, name + "_fwd"), None

    def bwd(_, dy):
        return (_rope_call(dy, cos, -sin, name + "_bwd"),)

    f.defvjp(fwd, bwd)
    return f


def _attn_specs(g, span, tk, m, nbh, has_ctx, hb=1):
    hd = HEAD_DIM
    q_spec = pl.BlockSpec((ATTN_BLOCK, hb * g * hd), lambda h, i, meta: (i, h))
    kv_spec = pl.BlockSpec((tk, hb * hd), lambda h, i, meta: (0, h))
    c_spec = pl.BlockSpec((m, hb * hd), lambda h, i, meta: (0, h))
    if nbh > 1 and hb > 1:
        b_spec = pl.BlockSpec((None, hb, ATTN_BLOCK, span), lambda h, i, meta: (meta[1, i], h, 0, 0))
    elif nbh > 1:
        b_spec = pl.BlockSpec((None, None, ATTN_BLOCK, span), lambda h, i, meta: (meta[1, i], h, 0, 0))
    else:
        b_spec = pl.BlockSpec((None, None, ATTN_BLOCK, span), lambda h, i, meta: (meta[1, i], 0, 0, 0))
    sink_spec = pl.BlockSpec(memory_space=pltpu.SMEM)
    return q_spec, kv_spec, c_spec, b_spec, sink_spec


def _attn_probs(qh, ks, kc, bias, sink_val, scale, has_ctx, has_sink):
    s = _dot_nt(qh, ks) * scale + bias
    mx = jnp.max(s, axis=-1, keepdims=True)
    sc = None
    if has_ctx:
        sc = _dot_nt(qh, kc) * scale
        mx = jnp.maximum(mx, jnp.max(sc, axis=-1, keepdims=True))
    if has_sink:
        mx = jnp.maximum(mx, sink_val)
    p = jnp.exp(s - mx)
    l = jnp.sum(p, axis=-1, keepdims=True)
    pc = None
    if has_ctx:
        pc = jnp.exp(sc - mx)
        l = l + jnp.sum(pc, axis=-1, keepdims=True)
    ps = None
    if has_sink:
        ps = jnp.exp(sink_val - mx)
        l = l + ps
    return p, pc, ps, l


def _attn_fwd(q, k, v, kc, vc, bias, sink, meta, g, span, has_ctx, has_sink, name):
    rq, wq = q.shape
    tk, wk = k.shape
    hkv = wk // HEAD_DIM
    m = kc.shape[0]
    nbh = bias.shape[1]
    scale = HEAD_DIM ** -0.5
    nqb = rq // ATTN_BLOCK
    hb = _pick(hkv, (4, 2, 1)) if g == 1 else 1
    q_spec, kv_spec, c_spec, b_spec, sink_spec = _attn_specs(g, span, tk, m, nbh, has_ctx, hb)
    hd = HEAD_DIM

    def body(meta_ref, sink_ref, q_ref, k_ref, v_ref, kc_ref, vc_ref, b_ref, o_ref):
        h = pl.program_id(0)
        i = pl.program_id(1)
        ks0 = pl.multiple_of(meta_ref[0, i], 64)
        for kl in range(hb):
            kcols = slice(kl * hd, (kl + 1) * hd)
            ks = k_ref[pl.ds(ks0, span), kcols]
            vs = v_ref[pl.ds(ks0, span), kcols]
            bias_t = b_ref[kl] if (hb > 1 and nbh > 1) else b_ref[...]
            for hh in range(g):
                cols = slice((kl * g + hh) * hd, (kl * g + hh + 1) * hd)
                sink_val = sink_ref[(h * hb + kl) * g + hh] if has_sink else None
                p, pc, _, l = _attn_probs(q_ref[:, cols], ks, kc_ref[:, kcols], bias_t, sink_val, scale, has_ctx, has_sink)
                acc = jnp.dot(p.astype(BF16), vs, preferred_element_type=F32)
                if has_ctx:
                    acc = acc + jnp.dot(pc.astype(BF16), vc_ref[:, kcols], preferred_element_type=F32)
                o_ref[:, cols] = (acc / l).astype(o_ref.dtype)

    gs = pltpu.PrefetchScalarGridSpec(
        num_scalar_prefetch=1, grid=(hkv // hb, nqb),
        in_specs=[sink_spec, q_spec, kv_spec, kv_spec, c_spec, c_spec, b_spec], out_specs=q_spec)
    return pl.pallas_call(body, grid_spec=gs, out_shape=jax.ShapeDtypeStruct((rq, wq), BF16), name=name,
                          compiler_params=_cparams("parallel", "arbitrary"))(meta, sink, q, k, v, kc, vc, bias)


def _attn_bwd(q, k, v, kc, vc, bias, sink, meta, o, do, g, span, has_ctx, has_sink, want_dbias, name):
    rq, wq = q.shape
    tk, wk = k.shape
    hkv = wk // HEAD_DIM
    m = kc.shape[0]
    ncase, nbh = bias.shape[:2]
    scale = HEAD_DIM ** -0.5
    nqb = rq // ATTN_BLOCK
    hb = _pick(hkv, (2, 1)) if g == 1 else 1
    q_spec, kv_spec, c_spec, b_spec, sink_spec = _attn_specs(g, span, tk, m, nbh, has_ctx, hb)
    dsink_spec = pl.BlockSpec((None, 8, HEAD_DIM), lambda h, i, meta: (h, 0, 0))
    hd = HEAD_DIM

    def body(meta_ref, sink_ref, q_ref, k_ref, v_ref, kc_ref, vc_ref, b_ref, o_ref, do_ref,
             dq_ref, dk_ref, dv_ref, dkc_ref, dvc_ref, db_ref, dsk_ref):
        h = pl.program_id(0)
        i = pl.program_id(1)

        @pl.when(i == 0)
        def _():
            dk_ref[...] = jnp.zeros_like(dk_ref)
            dv_ref[...] = jnp.zeros_like(dv_ref)
            dkc_ref[...] = jnp.zeros_like(dkc_ref)
            dvc_ref[...] = jnp.zeros_like(dvc_ref)
            dsk_ref[...] = jnp.zeros_like(dsk_ref)

        if want_dbias:
            @pl.when(meta_ref[2, i] == 1)
            def _():
                db_ref[...] = jnp.zeros_like(db_ref)
        else:
            @pl.when(i == 0)
            def _():
                db_ref[...] = jnp.zeros_like(db_ref)

        ks0 = pl.multiple_of(meta_ref[0, i], 64)
        for kl in range(hb):
            kcols = slice(kl * hd, (kl + 1) * hd)
            ks = k_ref[pl.ds(ks0, span), kcols]
            vs = v_ref[pl.ds(ks0, span), kcols]
            per_head_bias = hb > 1 and nbh > 1
            bias_t = b_ref[kl] if per_head_bias else b_ref[...]
            dk_acc = jnp.zeros((span, hd), F32)
            dv_acc = jnp.zeros((span, hd), F32)
            for hh in range(g):
                cols = slice((kl * g + hh) * hd, (kl * g + hh + 1) * hd)
                qh = q_ref[:, cols]
                doh = do_ref[:, cols]
                sink_val = sink_ref[(h * hb + kl) * g + hh] if has_sink else None
                p, pc, ps, l = _attn_probs(qh, ks, kc_ref[:, kcols], bias_t, sink_val, scale, has_ctx, has_sink)
                inv_l = 1.0 / l
                delta = jnp.sum(doh.astype(F32) * o_ref[:, cols].astype(F32), axis=-1, keepdims=True)
                pn = p * inv_l
                ds = pn * (_dot_nt(doh, vs) - delta)
                dsb = ds.astype(BF16)
                dq = jnp.dot(dsb, ks, preferred_element_type=F32)
                dk_acc = dk_acc + _dot_tn(dsb, qh)
                dv_acc = dv_acc + _dot_tn(pn.astype(BF16), doh)
                if want_dbias and per_head_bias:
                    db_ref[kl] += ds
                elif want_dbias:
                    db_ref[...] += ds
                if has_ctx:
                    pcn = pc * inv_l
                    dsc = (pcn * (_dot_nt(doh, vc_ref[:, kcols]) - delta)).astype(BF16)
                    dq = dq + jnp.dot(dsc, kc_ref[:, kcols], preferred_element_type=F32)
                    dkc_ref[:, kcols] += _dot_tn(dsc, qh) * scale
                    dvc_ref[:, kcols] += _dot_tn(pcn.astype(BF16), doh)
                if has_sink:
                    dsv = -jnp.sum(ps * inv_l * delta, axis=0, keepdims=True)
                    dsk_ref[kl * g + hh:kl * g + hh + 1, :] += jnp.broadcast_to(dsv, (1, hd))
                dq_ref[:, cols] = (dq * scale).astype(dq_ref.dtype)
            dk_ref[pl.ds(ks0, span), kcols] += dk_acc * scale
            dv_ref[pl.ds(ks0, span), kcols] += dv_acc

    gs = pltpu.PrefetchScalarGridSpec(
        num_scalar_prefetch=1, grid=(hkv // hb, nqb),
        in_specs=[sink_spec, q_spec, kv_spec, kv_spec, c_spec, c_spec, b_spec, q_spec, q_spec],
        out_specs=[q_spec, kv_spec, kv_spec, c_spec, c_spec, b_spec if want_dbias else dsink_spec, dsink_spec])
    db_sds = jax.ShapeDtypeStruct((ncase, nbh, ATTN_BLOCK, span) if want_dbias else (hkv, 8, HEAD_DIM), F32)
    out_shape = [jax.ShapeDtypeStruct((rq, wq), BF16), jax.ShapeDtypeStruct((tk, wk), F32), jax.ShapeDtypeStruct((tk, wk), F32),
                 jax.ShapeDtypeStruct((m, wk), F32), jax.ShapeDtypeStruct((m, wk), F32), db_sds,
                 jax.ShapeDtypeStruct((hkv, 8, HEAD_DIM), F32)]
    return pl.pallas_call(body, grid_spec=gs, out_shape=out_shape, name=name,
                          compiler_params=_cparams("parallel", "arbitrary"))(meta, sink, q, k, v, kc, vc, bias, o, do)


def make_attention(meta_np, g, span, has_ctx, has_sink, want_dbias, name):
    meta = jnp.asarray(meta_np, jnp.int32)

    @jax.custom_vjp
    def f(q, k, v, kc, vc, bias, sink):
        return _attn_fwd(q, k, v, kc, vc, bias, sink, meta, g, span, has_ctx, has_sink, name + "_fwd")

    def fwd(q, k, v, kc, vc, bias, sink):
        o = _attn_fwd(q, k, v, kc, vc, bias, sink, meta, g, span, has_ctx, has_sink, name + "_fwd")
        return o, (q, k, v, kc, vc, bias, sink, o)

    def bwd(res, do):
        q, k, v, kc, vc, bias, sink, o = res
        dq, dk, dv, dkc, dvc, db, dsk = _attn_bwd(q, k, v, kc, vc, bias, sink, meta, o, do.astype(BF16), g, span,
                                                   has_ctx, has_sink, want_dbias, name + "_bwd")
        dsink = dsk[:, :g, 0].reshape(sink.shape) if has_sink else jnp.zeros_like(sink)
        if not want_dbias:
            db = jnp.zeros_like(bias)
        return dq, dk.astype(k.dtype), dv.astype(v.dtype), dkc.astype(kc.dtype), dvc.astype(vc.dtype), db, dsink

    f.defvjp(fwd, bwd)
    return f


def _dedupe_cases(tables):
    cases, idx, first = [], [], []
    for tbl in tables:
        if cases and np.array_equal(cases[-1], tbl):
            idx.append(len(cases) - 1)
            first.append(0)
        else:
            cases.append(tbl)
            idx.append(len(cases) - 1)
            first.append(1)
    return cases, idx, first


def _na_plan(t):
    rows = t // GRID_W
    qr = ATTN_BLOCK // GRID_W
    kr = qr + NA_ROWS - 1
    assert rows >= kr and rows % qr == 0
    span = kr * GRID_W
    kstart, tables = [], []
    qcol = np.tile(np.arange(GRID_W), qr)
    kcol = np.tile(np.arange(GRID_W), kr)
    win_c = np.clip(qcol - NA_COLS // 2, 0, GRID_W - NA_COLS)
    col_ok = (kcol[None, :] >= win_c[:, None]) & (kcol[None, :] < win_c[:, None] + NA_COLS)
    dcol = np.clip(kcol[None, :] - qcol[:, None] + NA_COLS - 1, 0, 2 * NA_COLS - 2)
    for r0 in range(0, rows, qr):
        kb = int(np.clip(r0 - NA_ROWS // 2, 0, rows - kr))
        qrow = r0 + np.repeat(np.arange(qr), GRID_W)
        krow = kb + np.repeat(np.arange(kr), GRID_W)
        win_r = np.clip(qrow - NA_ROWS // 2, 0, rows - NA_ROWS)
        row_ok = (krow[None, :] >= win_r[:, None]) & (krow[None, :] < win_r[:, None] + NA_ROWS)
        drow = np.clip(krow[None, :] - qrow[:, None] + NA_ROWS - 1, 0, 2 * NA_ROWS - 2)
        tables.append(np.stack([row_ok & col_ok, drow, dcol]).astype(np.int32))
        kstart.append(kb * GRID_W)
    cases, idx, first = _dedupe_cases(tables)
    meta = np.array([kstart, idx, first], np.int32)
    return meta, span, np.stack(cases)


def _na_bias(rpb, cases):
    valid, drow, dcol = cases[:, 0], cases[:, 1], cases[:, 2]
    ncase, qn, span = valid.shape
    qr, kr = qn // GRID_W, span // GRID_W
    drow_s = drow.reshape(ncase, qr, GRID_W, kr, GRID_W)[:, :, 0, :, 0]
    dcol_s = dcol[0].reshape(qr, GRID_W, kr, GRID_W)[0, :, 0, :]
    oh_r = jnp.asarray(np.eye(2 * NA_ROWS - 1, dtype=np.float32)[drow_s])
    oh_c = jnp.asarray(np.eye(2 * NA_COLS - 1, dtype=np.float32)[dcol_s])
    tmp = jnp.einsum("hrc,xyc->hrxy", rpb, oh_c, precision=lax.Precision.HIGHEST)
    b = jnp.einsum("nakr,hrxy->nhaxky", oh_r, tmp, precision=lax.Precision.HIGHEST).reshape(ncase, -1, qn, span)
    return jnp.where(jnp.asarray(valid[:, None] > 0), b, NEG_INF)


def _sw_plan(t):
    span = 3 * ATTN_BLOCK
    assert t >= span
    kstart, tables = [], []
    for b in range(t // ATTN_BLOCK):
        ks = int(np.clip((b - 1) * ATTN_BLOCK, 0, t - span))
        qpos = b * ATTN_BLOCK + np.arange(ATTN_BLOCK)
        kpos = ks + np.arange(span)
        ok = np.abs(kpos[None, :] - qpos[:, None]) <= SW_RADIUS
        tables.append(np.where(ok, 0.0, NEG_INF).astype(np.float32))
        kstart.append(ks)
    cases, idx, first = _dedupe_cases(tables)
    return np.array([kstart, idx, first], np.int32), span, np.stack(cases)[:, None]


def _cmul(ar, ai, br, bi):
    return ar * br - ai * bi, ar * bi + ai * br


def _s5_scan_call(x2, win, lam, cin, wout, reverse, n_chunks, name):
    _, ll, d = x2.shape
    nt = d // HEAD_DIM
    sw = 2 * SSM_TILE_GROUPS * SSM_STATE
    hw = sw // 2
    rows = ll // n_chunks
    full = cin is not None

    down_dir = 0 if reverse else 1

    def chunk_idx(k, dd):
        return jnp.where(dd == down_dir, n_chunks - 1 - k, k)

    n_sub = 3 if (not full and rows % (3 * 16) == 0) else 1
    sub = rows // n_sub
    ics = sub // SCAN_BLOCKS

    def body(*refs):
        if full:
            x_ref, win_ref, lam_ref, cin_ref, wout_ref, s_out, y_out = refs[:7]
        else:
            x_ref, win_ref, lam_ref, f_out = refs[:4]
        ubs, st_ref = refs[-1 - n_sub:-1], refs[-1]
        k = pl.program_id(2)
        down = pl.program_id(0) == down_dir

        @pl.when(k == 0)
        def _():
            st_ref[...] = cin_ref[...] if full else jnp.zeros_like(st_ref)

        def sub_rows(p):
            return pl.ds(pl.multiple_of(jnp.where(down, n_sub - 1 - p, p) * sub, 16), sub)

        def drive(p):
            ubs[p][...] = jnp.dot(x_ref[sub_rows(p), :].astype(BF16), win_ref[...], preferred_element_type=F32)

        lr = lam_ref[:, :hw]
        li = lam_ref[:, hw:]
        if full:
            ub = ubs[0]
            drive(0)

            def step(ii, carry):
                sr, si = carry
                i = jnp.where(down, ics - 1 - ii, ii)
                r0 = pl.multiple_of(i * SCAN_BLOCKS, SCAN_BLOCKS)
                nr = lr * sr - li * si + ub[pl.ds(r0, SCAN_BLOCKS), :hw]
                ni = lr * si + li * sr + ub[pl.ds(r0, SCAN_BLOCKS), hw:]
                ub[pl.ds(r0, SCAN_BLOCKS), :hw] = nr
                ub[pl.ds(r0, SCAN_BLOCKS), hw:] = ni
                return nr, ni

            sr, si = lax.fori_loop(0, ics, step, (st_ref[:, :hw], st_ref[:, hw:]), unroll=4 if ics % 4 == 0 else 1)
            sb = ub[...].astype(BF16)
            s_out[...] = sb
            y_out[...] = jnp.dot(sb, wout_ref[...], preferred_element_type=F32)
        else:
            sr, si = st_ref[:, :hw], st_ref[:, hw:]
            drive(0)
            for p in range(n_sub):
                if p + 1 < n_sub:
                    drive(p + 1)
                for ii in range(ics):
                    r0 = pl.multiple_of(jnp.where(down, (ics - 1 - ii) * SCAN_BLOCKS, ii * SCAN_BLOCKS), SCAN_BLOCKS)
                    ur = ubs[p][pl.ds(r0, SCAN_BLOCKS), :hw]
                    ui = ubs[p][pl.ds(r0, SCAN_BLOCKS), hw:]
                    sr, si = lr * sr - li * si + ur, lr * si + li * sr + ui
        st_ref[:, :hw] = sr
        st_ref[:, hw:] = si
        if not full:
            @pl.when(k == n_chunks - 1)
            def _():
                f_out[...] = st_ref[...]

    x_spec = pl.BlockSpec((None, rows, HEAD_DIM), lambda dd, t, k: (dd, chunk_idx(k, dd), t))
    win_spec = pl.BlockSpec((None, None, HEAD_DIM, sw), lambda dd, t, k: (dd, t, 0, 0))
    vec_spec = pl.BlockSpec((None, None, SCAN_BLOCKS, sw), lambda dd, t, k: (dd, t, 0, 0))
    scratch = [pltpu.VMEM((sub, sw), F32) for _ in range(n_sub)] + [pltpu.VMEM((SCAN_BLOCKS, sw), F32)]
    if full:
        in_specs = [x_spec, win_spec, vec_spec, vec_spec, pl.BlockSpec((None, None, sw, HEAD_DIM), lambda dd, t, k: (dd, t, 0, 0))]
        out_specs = [pl.BlockSpec((None, None, rows, sw), lambda dd, t, k: (dd, t, chunk_idx(k, dd), 0)), x_spec]
        out_shape = [jax.ShapeDtypeStruct((2, nt, ll, sw), BF16), jax.ShapeDtypeStruct((2, ll, d), F32)]
        args = (x2, win, lam, cin, wout)
    else:
        in_specs = [x_spec, win_spec, vec_spec]
        out_specs = vec_spec
        out_shape = jax.ShapeDtypeStruct((2, nt, SCAN_BLOCKS, sw), F32)
        args = (x2, win, lam)
    return pl.pallas_call(body, grid=(2, nt, n_chunks), in_specs=in_specs, out_specs=out_specs, out_shape=out_shape,
                          scratch_shapes=scratch, name=name,
                          compiler_params=_cparams("parallel", "parallel", "arbitrary"))(*args)


def _s5_bwd_call(dy2, wrt, lamc, cin, st, u2, wdt, n_chunks, name):
    _, ll, d = dy2.shape
    nt = d // HEAD_DIM
    sw = 2 * SSM_TILE_GROUPS * SSM_STATE
    hw = sw // 2
    rows = ll // n_chunks
    ic = rows // SCAN_BLOCKS

    def chunk_idx(k, dd):
        return jnp.where(dd == 0, n_chunks - 1 - k, k)

    def body(dy_ref, wrt_ref, lam_ref, cin_ref, stb_ref, u_ref, wdt_ref, du_out, dwd_out, dwr_out, dlam_out,
             ds_ref, a_ref, st_ref):
        k = pl.program_id(2)
        down = pl.program_id(0) == 0
        st_ref[...] = stb_ref[...].astype(F32)

        @pl.when(k == 0)
        def _():
            a_ref[...] = cin_ref[...]
            dwd_out[...] = jnp.zeros_like(dwd_out)
            dwr_out[...] = jnp.zeros_like(dwr_out)
            dlam_out[...] = jnp.zeros_like(dlam_out)

        dyb = dy_ref[...].astype(BF16)
        ds_ref[...] = jnp.dot(dyb, wrt_ref[...], preferred_element_type=F32)
        dwr_out[...] += _dot_tn(stb_ref[...], dyb)
        lr = lam_ref[:, :hw]
        li = lam_ref[:, hw:]

        def step(ii, carry):
            ar, ai, gr, gi = carry
            i = jnp.where(down, ic - 1 - ii, ii)
            r0 =pl.multiple_of(i * SCAN_BLOCKS, SCAN_BLOCKS)
            sr = st_ref[pl.ds(r0, SCAN_BLOCKS), :hw]
            si = st_ref[pl.ds(r0, SCAN_BLOCKS), hw:]
            gr = gr + ar * sr + ai * si
            gi = gi + ai * sr - ar * si
            nr = lr * ar - li * ai + ds_ref[pl.ds(r0, SCAN_BLOCKS), :hw]
            ni = lr * ai + li * ar + ds_ref[pl.ds(r0, SCAN_BLOCKS), hw:]
            ds_ref[pl.ds(r0, SCAN_BLOCKS), :hw] = nr
            ds_ref[pl.ds(r0, SCAN_BLOCKS), hw:] = ni
            return nr, ni, gr, gi

        init = (a_ref[:, :hw], a_ref[:, hw:], dlam_out[:, :hw], dlam_out[:, hw:])
        ar, ai, gr, gi = lax.fori_loop(0, ic, step, init, unroll=4 if ic % 4 == 0 else 1)
        a_ref[:, :hw] = ar
        a_ref[:, hw:] = ai
        dlam_out[:, :hw] = gr
        dlam_out[:, hw:] = gi
        ab = ds_ref[...].astype(BF16)
        du_out[...] = jnp.dot(ab, wdt_ref[...], preferred_element_type=F32).astype(du_out.dtype)
        dwd_out[...] += _dot_tn(u_ref[...].astype(BF16), ab)

    x_spec = pl.BlockSpec((None, rows, HEAD_DIM), lambda dd, t, k: (dd, chunk_idx(k, dd), t))
    w_in = pl.BlockSpec((None, None, HEAD_DIM, sw), lambda dd, t, k: (dd, t, 0, 0))
    w_out = pl.BlockSpec((None, None, sw, HEAD_DIM), lambda dd, t, k: (dd, t, 0, 0))
    vec_spec = pl.BlockSpec((None, None, SCAN_BLOCKS, sw), lambda dd, t, k: (dd, t, 0, 0))
    st_spec = pl.BlockSpec((None, None, rows, sw), lambda dd, t, k: (dd, t, chunk_idx(k, dd), 0))
    out_shape = [jax.ShapeDtypeStruct((2, ll, d), u2.dtype), jax.ShapeDtypeStruct((2, nt, HEAD_DIM, sw), F32),
                 jax.ShapeDtypeStruct((2, nt, sw, HEAD_DIM), F32), jax.ShapeDtypeStruct((2, nt, SCAN_BLOCKS, sw), F32)]
    return pl.pallas_call(
        body, grid=(2, nt, n_chunks),
        in_specs=[x_spec, w_in, vec_spec, vec_spec, st_spec, x_spec, w_out],
        out_specs=[x_spec, w_in, w_out, vec_spec], out_shape=out_shape,
        scratch_shapes=[pltpu.VMEM((rows, sw), F32), pltpu.VMEM((SCAN_BLOCKS, sw), F32), pltpu.VMEM((rows, sw), F32)],
        name=name, compiler_params=_cparams("parallel", "parallel", "arbitrary"))(dy2, wrt, lamc, cin, st, u2, wdt)


def _cpow(lr, li, n):
    rr, ri = jnp.ones_like(lr), jnp.zeros_like(li)
    br, bi = lr, li
    while n:
        if n & 1:
            rr, ri = _cmul(rr, ri, br, bi)
        br, bi = _cmul(br, bi, br, bi)
        n >>= 1
    return rr, ri


def _resolve_carries(finals, lam, block_len, down_dir):
    hw = finals.shape[-1] // 2
    pr, pi = _cpow(lam[:, :, 0, :hw], lam[:, :, 0, hw:], block_len)
    fr, fi = finals[..., :hw], finals[..., hw:]

    def walk(order):
        cr, ci = jnp.zeros_like(pr), jnp.zeros_like(pi)
        out = [None] * SCAN_BLOCKS
        for j in order:
            out[j] = jnp.concatenate([cr, ci], axis=-1)
            mr, mi = _cmul(pr, pi, cr, ci)
            cr, ci = mr + fr[:, :, j], mi + fi[:, :, j]
        return jnp.stack(out, axis=2)

    up, down = walk(range(SCAN_BLOCKS)), walk(range(SCAN_BLOCKS - 1, -1, -1))
    return jnp.stack([down[0], up[1]] if down_dir == 0 else [up[0], down[1]])


def _scan_chunks(ll):
    block_len = ll // SCAN_BLOCKS
    for ic in (132, 128, 96, 64, 48, 36, 32, 24, 16, 8):
        if block_len % ic == 0:
            return block_len // ic
    return 1


def make_s5_core(name):
    def run_fwd(u2, lam, wd, wr):
        ll = u2.shape[1]
        nc = _scan_chunks(ll)
        lam8 = jnp.broadcast_to(lam[:, :, None, :], lam.shape[:2] + (SCAN_BLOCKS, lam.shape[-1]))
        wdb = wd.astype(BF16)
        finals = _s5_scan_call(u2, wdb, lam8, None, None, False, nc, name + "_carry")
        cin = _resolve_carries(finals, lam8, ll // SCAN_BLOCKS, 1)
        st, y2 = _s5_scan_call(u2, wdb, lam8, cin, wr.astype(BF16), False, nc, name + "_scan")
        return y2, st, lam8

    @jax.custom_vjp
    def f(u2, lam, wd, wr):
        return run_fwd(u2, lam, wd, wr)[0]

    def fwd(u2, lam, wd, wr):
        y2, st, lam8 = run_fwd(u2, lam, wd, wr)
        return y2, (u2, lam8, wd, wr, st)

    def bwd(res, dy2):
        u2, lam8, wd, wr, st = res
        ll = u2.shape[1]
        nc = _scan_chunks(ll)
        hw = lam8.shape[-1] // 2
        lamc = jnp.concatenate([lam8[..., :hw], -lam8[..., hw:]], axis=-1)
        wrt = jnp.swapaxes(wr, 2, 3).astype(BF16)
        wdt = jnp.swapaxes(wd, 2, 3).astype(BF16)
        finals = _s5_scan_call(dy2, wrt, lamc, None, None, True, nc, name + "_bcarry")
        cin = _resolve_carries(finals, lamc, ll // SCAN_BLOCKS, 0)
        du2, dwd, dwr, dlam8 = _s5_bwd_call(dy2, wrt, lamc, cin, st, u2, wdt, nc, name + "_bscan")
        return du2, jnp.sum(dlam8, axis=2), dwd, dwr

    f.defvjp(fwd, bwd)
    return f


def _s5_params(a_re, a_im, log_dt, b_re, b_im, c_re, c_im):
    dt = jnp.exp(log_dt)[..., None]
    mag = jnp.exp(a_re * dt)
    lam_r, lam_i = mag * jnp.cos(a_im * dt), mag * jnp.sin(a_im * dt)
    den = a_re * a_re + a_im * a_im
    nr = lam_r - 1.0
    coef_r = (nr * a_re + lam_i * a_im) / den
    coef_i = (lam_i * a_re - nr * a_im) / den
    bbar_r = coef_r[..., None] * b_re - coef_i[..., None] * b_im
    bbar_i = coef_r[..., None] * b_im + coef_i[..., None] * b_re
    ndir, g, p = lam_r.shape
    tg = SSM_TILE_GROUPS
    nt = g // tg
    eye = jnp.eye(tg, dtype=F32)

    def tile_vec(v):
        return v.reshape(ndir, nt, tg * p)

    lam = jnp.concatenate([tile_vec(lam_r), tile_vec(lam_i)], axis=-1)

    def drive(b):
        bt = b.reshape(ndir, nt, tg, p, SSM_GROUP)
        return (jnp.swapaxes(bt, 3, 4)[:, :, :, :, None, :] * eye[None, None, :, None, :, None]).reshape(ndir, nt, tg * SSM_GROUP, tg * p)

    wd = jnp.concatenate([drive(bbar_r), drive(bbar_i)], axis=-1)

    def readout(c):
        ct = c.reshape(ndir, nt, tg, SSM_GROUP, p)
        return (jnp.swapaxes(ct, 3, 4)[:, :, :, :, None, :] * eye[None, None, :, None, :, None]).reshape(ndir, nt, tg * p, tg * SSM_GROUP)

    wr = jnp.concatenate([readout(c_re), -readout(c_im)], axis=2)
    return lam, wd, wr


def _to_scan_order(seq):
    ll, d = seq.shape
    return seq.reshape(SCAN_BLOCKS, ll // SCAN_BLOCKS, d).swapaxes(0, 1).reshape(ll, d)


def _from_scan_order(y2):
    ll, d = y2.shape
    return y2.reshape(ll // SCAN_BLOCKS, SCAN_BLOCKS, d).swapaxes(0, 1).reshape(ll, d)


def adamw(w, g, m, v, name):
    shape = w.shape
    cols = shape[-1] if len(shape) > 1 else shape[0]
    w2, g2, m2, v2 = (a.reshape(-1, cols) for a in (w, g, m, v))
    r = w2.shape[0]
    cap = max(1, (1024 * 1024) // (4 * cols))
    tr = r
    for t in (512, 256, 128, 64, 32, 16, 8):
        if t <= cap and r % t == 0:
            tr = t
            break
    c1 = 1.0 / (1.0 - ADAM_B1 ** ADAM_STEP)
    c2 = 1.0 / (1.0 - ADAM_B2 ** ADAM_STEP)

    def body(w_ref, g_ref, m_ref, v_ref, d_ref, mo_ref, vo_ref):
        gg = g_ref[...]
        mn = ADAM_B1 * m_ref[...] + (1.0 - ADAM_B1) * gg
        vn = ADAM_B2 * v_ref[...] + (1.0 - ADAM_B2) * (gg * gg)
        d_ref[...] = -ADAM_LR * ((mn * c1) / (jnp.sqrt(vn * c2) + ADAM_EPS) + ADAM_WD * w_ref[...])
        mo_ref[...] = mn
        vo_ref[...] = vn

    spec = pl.BlockSpec((tr, cols), lambda i: (i, 0))
    sds = jax.ShapeDtypeStruct((r, cols), F32)
    d, mn, vn = pl.pallas_call(body, grid=(r // tr,), in_specs=[spec] * 4, out_specs=[spec] * 3, out_shape=[sds] * 3,
                               name=name, compiler_params=_cparams("parallel"))(w2, g2, m2, v2)
    return d.reshape(shape), mn.reshape(shape), vn.reshape(shape)


def _my_pos():
    return lax.axis_index("x"), lax.axis_index("y"), lax.axis_index("c")


def _flip(pos, f):
    return tuple((1 - p) if b else p for p, b in zip(pos, f))


def _lin(pos):
    return 4 * pos[0] + 2 * pos[1] + pos[2]


def _remote_copies(src_ref, out_ref, send_sems, recv_sems, plan):
    me = _my_pos()
    copies = []
    for k, (f, sfn, dfn) in enumerate(plan):
        peer = _flip(me, f)
        copies.append(pltpu.make_async_remote_copy(
            src_ref=src_ref.at[sfn(me, peer)], dst_ref=out_ref.at[dfn(me, peer)], send_sem=send_sems.at[k],
            recv_sem=recv_sems.at[k], device_id=peer, device_id_type=MESH))
    return copies


def xchg(src, n_out, plan, name, inplace=False):
    piece = src.shape[1:]

    def body(src_ref, out_ref, send_sems, recv_sems):
        me = _my_pos()
        copies = []
        for k, (f, sfn, dfn) in enumerate(plan):
            peer = _flip(me, f)
            s_ref = (out_ref if inplace else src_ref).at[sfn(me, peer)]
            d_ref = out_ref.at[dfn(me, peer)]
            if any(f):
                cp = pltpu.make_async_remote_copy(src_ref=s_ref, dst_ref=d_ref, send_sem=send_sems.at[k],
                                                  recv_sem=recv_sems.at[k], device_id=peer, device_id_type=MESH)
            else:
                cp = pltpu.make_async_copy(s_ref, d_ref, recv_sems.at[k])
            cp.start()
            copies.append((cp, any(f)))
        for cp, remote in copies:
            if remote:
                cp.wait_recv()
            else:
                cp.wait()
        for cp, remote in copies:
            if remote:
                cp.wait_send()

    return pl.pallas_call(
        body, in_specs=[pl.BlockSpec(memory_space=pl.ANY)], out_specs=pl.BlockSpec(memory_space=pl.ANY),
        out_shape=jax.ShapeDtypeStruct((n_out,) + piece, src.dtype),
        scratch_shapes=[pltpu.SemaphoreType.DMA((len(plan),)), pltpu.SemaphoreType.DMA((len(plan),))],
        input_output_aliases={0: 0} if inplace else {}, name=name)(src)


_CHIP_FLIPS = ((1, 0, 0), (0, 1, 0), (1, 1, 0))
_ALL_FLIPS = tuple((a, b, c) for a in (0, 1) for b in (0, 1) for c in (0, 1))[1:]


def all_to_all8(src, name):
    plan = [((0, 0, 0), lambda me, peer: _lin(me), lambda me, peer: _lin(me))]
    plan += [(f, lambda me, peer: _lin(peer), lambda me, peer: _lin(me)) for f in _ALL_FLIPS]
    return xchg(src, N_DEV, plan, name)


def all_gather8(piece, name):
    plan = [((0, 0, 0), lambda me, peer: 0, lambda me, peer: _lin(me))]
    plan += [(f, lambda me, peer: 0, lambda me, peer: _lin(me)) for f in _ALL_FLIPS]
    return xchg(piece[None], N_DEV, plan, name)


def _ag_prepare(shard):
    k, ns = shard.shape
    px, py, _ = _my_pos()
    own = shard.astype(BF16)[None]
    return lax.dynamic_update_slice(jnp.zeros((4, k, ns), BF16), own, (2 * px + py, 0, 0)).reshape(8, k // 2, ns)


def _ag_plan():
    return [(f, lambda me, peer: _lin(me), lambda me, peer: _lin(me)) for f in _CHIP_FLIPS]


def _ag_finish(buf, name):
    plan = [((0, 0, 1), lambda me, peer, f=f: _lin(_flip(me, f)), lambda me, peer, f=f: _lin(_flip(me, f)))
            for f in _CHIP_FLIPS]
    _, kh, ns = buf.shape
    return xchg(buf, 8, plan, name, inplace=True).reshape(4, 2 * kh, ns)


def gather_weight(shard, name):
    k, ns = shard.shape
    buf = _ag_prepare(shard)

    def body(in_ref, out_ref, send_sems, recv_sems):
        me = _my_pos()
        sibling = _flip(me, (0, 0, 1))
        chips = [_flip(me, f) for f in _CHIP_FLIPS]

        def copy(sem, holder, to):
            rows = out_ref.at[4 * holder[0] + 2 * holder[1] + me[2]]
            return pltpu.make_async_remote_copy(src_ref=rows, dst_ref=rows, send_sem=send_sems.at[sem],
                                                recv_sem=recv_sems.at[sem], device_id=to, device_id_type=MESH)

        first = [copy(j, me, chip) for j, chip in enumerate(chips)]
        for cp in first:
            cp.start()
        passed = [copy(3 + j, chip, sibling) for j, chip in enumerate(chips)]
        for j, chip in enumerate(chips):
            copy(j, chip, me).wait_recv()
            passed[j].start()
        for j in range(3):
            passed[j].wait_recv()
        for cp in first + passed:
            cp.wait_send()

    full = pl.pallas_call(
        body, in_specs=[pl.BlockSpec(memory_space=pl.ANY)], out_specs=pl.BlockSpec(memory_space=pl.ANY),
        out_shape=jax.ShapeDtypeStruct(buf.shape, BF16),
        scratch_shapes=[pltpu.SemaphoreType.DMA((6,)), pltpu.SemaphoreType.DMA((6,))],
        input_output_aliases={0: 0}, name=name)(buf)
    return full.reshape(4, k, ns)


_RS_SLOTS = 7


def _sum_halves(g8, l1, c_idx, name):
    _, _, r, cc = g8.shape
    tr = _row_tile(r, r, cc)

    def body(c_ref, a_ref, b_ref, o_ref):
        o_ref[...] = (a_ref[...].astype(F32) + b_ref[...].astype(F32)).astype(o_ref.dtype)

    gs = pltpu.PrefetchScalarGridSpec(
        num_scalar_prefetch=1, grid=(4, r // tr),
        in_specs=[pl.BlockSpec((None, None, tr, cc), lambda s, i, c: (s, c[0], i, 0)),
                  pl.BlockSpec((None, tr, cc), lambda s, i, c: (s, i, 0))],
        out_specs=pl.BlockSpec((None, tr, cc), lambda s, i, c: (s, i, 0)))
    return pl.pallas_call(body, grid_spec=gs, out_shape=jax.ShapeDtypeStruct((_RS_SLOTS, r, cc), BF16), name=name,
                          compiler_params=_cparams("parallel", "parallel"))(c_idx, g8, l1)


def _sum_chips(buf, sc_idx, name):
    _, r, cc = buf.shape
    tr = _row_tile(r, r, cc)

    def body(s_ref, a_ref, b0_ref, b1_ref, b2_ref, o_ref):
        o_ref[...] = ((a_ref[...].astype(F32) + b0_ref[...].astype(F32)) + b1_ref[...].astype(F32)) + b2_ref[...].astype(F32)

    gs = pltpu.PrefetchScalarGridSpec(
        num_scalar_prefetch=1, grid=(r // tr,),
        in_specs=[pl.BlockSpec((None, tr, cc), lambda i, s: (s[0], i, 0))]
        + [pl.BlockSpec((None, tr, cc), lambda i, s, j=j: (4 + j, i, 0)) for j in range(3)],
        out_specs=pl.BlockSpec((None, tr, cc), lambda i, s: (s[1], i, 0)))
    return pl.pallas_call(body, grid_spec=gs, out_shape=jax.ShapeDtypeStruct((2, r, cc), F32), name=name,
                          compiler_params=_cparams("parallel"))(sc_idx, buf, buf, buf, buf)


def _rs_begin(g4, name):
    _, k, ns = g4.shape
    c_idx = jnp.reshape(_my_pos()[2], (1,)).astype(jnp.int32)
    plan1 = [((0, 0, 1), lambda me, peer, s=s: 2 * s + peer[2], lambda me, peer, s=s: s) for s in range(4)]
    l1 = xchg(g4.reshape(8, k // 2, ns), 4, plan1, name + "_rs_d2d")
    buf = _sum_halves(g4.reshape(4, 2, k // 2, ns), l1, c_idx, name + "_rs_sum2")
    plan2 = [(f, lambda me, peer: 2 * peer[0] + peer[1], lambda me, peer, j=j: 4 + j) for j, f in enumerate(_CHIP_FLIPS)]
    return buf, plan2


def _rs_finish(buf, name):
    _, kh, ns = buf.shape
    x, y, c = _my_pos()
    sc_idx = jnp.stack([2 * x + y, c]).astype(jnp.int32)
    halves = _sum_chips(buf, sc_idx, name + "_rs_sum4")
    plan3 = [((0, 0, 1), lambda me, peer: me[2], lambda me, peer: me[2])]
    return xchg(halves, 2, plan3, name + "_rs_swap", inplace=True).reshape(2 * kh, ns)


def reduce_scatter_weight(g4, name):
    buf, plan = _rs_begin(g4, name)
    return _rs_finish(xchg(buf, _RS_SLOTS, plan, name + "_rs_ici", inplace=True), name)


def _sum8(a8, name):
    _, r, cc = a8.shape
    tr = _row_tile(r, r, cc)

    def body(a_ref, o_ref):
        acc = a_ref[0]
        for j in range(1, N_DEV):
            acc = acc + a_ref[j]
        o_ref[...] = acc

    return pl.pallas_call(body, grid=(r // tr,), in_specs=[pl.BlockSpec((N_DEV, tr, cc), lambda i: (0, i, 0))],
                          out_specs=pl.BlockSpec((tr, cc), lambda i: (i, 0)), out_shape=jax.ShapeDtypeStruct((r, cc), F32),
                          name=name, compiler_params=_cparams("parallel"))(a8)


def all_reduce8(flat, name):
    n = flat.shape[0]
    unit = N_DEV * 256 * 128
    npad = -(-n // unit) * unit
    a = jnp.pad(flat, (0, npad - n)).reshape(N_DEV, npad // (N_DEV * 128), 128)
    mine = _sum8(all_to_all8(a, name + "_rs"), name + "_sum")
    return all_gather8(mine, name + "_ag").reshape(npad)[:n]


def _make_split(t, cuts):
    def pieces(qkv):
        out = []
        for rows in (slice(None, t), slice(t, None)):
            out += [qkv[rows, a:b] for a, b in zip(cuts[:-1], cuts[1:])]
        return tuple(out)

    @jax.custom_vjp
    def split(qkv):
        return pieces(qkv)

    def fwd(qkv):
        return pieces(qkv), None

    def bwd(_, g):
        n = len(cuts) - 1
        return (jnp.concatenate([jnp.concatenate(g[:n], axis=1), jnp.concatenate(g[n:], axis=1)], axis=0),)

    split.defvjp(fwd, bwd)
    return split


def _local_loss(x, ctx, target, mods, small, big, pending, shards, dims):
    t, m, d = dims["t"], dims["m"], dims["d"]
    a_w, bq_w, bkv_w = dims["a_w"], dims["bq_w"], dims["bkv_w"]
    z = jnp.concatenate([x, ctx], axis=0)
    big = dict(big)

    def grp(layer, j, n_groups=2):
        return mods[layer, :n_groups, j][:, None, :]

    def lin(kind, out_dtype, name, a, wname, gather=()):
        y, gathered = make_linear(kind, out_dtype, name)(a, big[wname], shards[wname], tuple(pending[n] for n in gather))
        big.update(zip(gather, gathered))
        return y

    def ffn(layer, a, gather_in, gather_out):
        n1, n3, n2 = f"ffn_w1_{layer}", f"ffn_w3_{layer}", f"ffn_w2_{layer}"
        gathering = (tuple(pending[n] for n in gather_in), tuple(pending[n] for n in gather_out))
        w2 = big[n2] if n2 in big else pending[n2]
        y, (got_in, got_out) = make_ffn(f"ffn{layer}", n2 not in big)(a, big[n1], big[n3], w2, shards[n1], shards[n3],
                                                                    shards[n2], gathering)
        big.update(zip(gather_in, got_in))
        big.update(zip(gather_out, got_out))
        return y

    h = make_norm_mod(t, "norm_mix0")(z, small["norm_mix"][0][None], grp(0, 1), grp(0, 0))
    qkv = lin("col", BF16, "attn_in", h, "attn_w_in", ("attn_w_out", "ffn_w1_0"))
    o3, o5 = 3 * a_w, 3 * a_w + bq_w + bkv_w
    cuts = (0, a_w, 2 * a_w, o3, o3 + bq_w, o5, o5 + bkv_w)
    qa, ka, va, qb_u, kb_u, vb, qa_c, ka_c, va_c, qb_c, kb_c, vb_c = _make_split(t, cuts)(qkv)
    qb, kb = make_rope(t, "rope_q")(qb_u), make_rope(t, "rope_k")(kb_u)
    sink = small["attn_sink"][0]
    no_sink = jnp.zeros((a_w // HEAD_DIM,), F32)
    na_meta, na_span, na_cases = _na_plan(t)
    oa = make_attention(na_meta, 1, na_span, True, False, True, "na")(
        qa, ka, va, ka_c, va_c, _na_bias(small["attn_rpb"][0], na_cases), no_sink)
    sw_meta, sw_span, sw_bias = _sw_plan(t)
    grp_b = bq_w // bkv_w
    ob = make_attention(sw_meta, grp_b, sw_span, True, True, False, "swa")(qb, kb, vb, kb_c, vb_c, jnp.asarray(sw_bias), sink)
    c_meta = np.array([[0] * (m // ATTN_BLOCK), [0] * (m // ATTN_BLOCK), [1] + [0] * (m // ATTN_BLOCK - 1)], np.int32)
    zero_bias = jnp.zeros((1, 1, ATTN_BLOCK, m), F32)
    oa_c = make_attention(c_meta, 1, m, False, False, False, "ctx_na")(qa_c, ka_c, va_c, ka_c, va_c, zero_bias, no_sink)
    ob_c = make_attention(c_meta, grp_b, m, False, True, False, "ctx_swa")(qb_c, kb_c, vb_c, kb_c, vb_c, zero_bias, sink)
    o = jnp.concatenate([jnp.concatenate([oa, ob], axis=1), jnp.concatenate([oa_c, ob_c], axis=1)], axis=0)
    y = lin("row", F32, "attn_out", o, "attn_w_out", ("ffn_w3_0",))
    z, h = make_residual_norm(t, "mix0_to_ffn0")(z, y, grp(0, 2), small["norm_ffn"][0][None], grp(0, 4), grp(0, 3))
    y = ffn(0, h, ("ssm_w_glu", "ffn_w1_1"), ("ffn_w3_1",))

    z, h = make_residual_norm(t, "ffn0_to_mix1")(z, y, grp(0, 5), small["norm_mix"][1][None], grp(1, 1), grp(1, 0))
    hx, hc = h[:t], h[t:]
    lam, wd, wr = _s5_params(small["ssm_a_re"][0], small["ssm_a_im"][0], small["ssm_log_dt"][0], small["ssm_b_re"][0],
                             small["ssm_b_im"][0], small["ssm_c_re"][0], small["ssm_c_im"][0])
    u2 = jnp.stack([_to_scan_order(jnp.concatenate([hc, hx], axis=0)), _to_scan_order(h)])
    y2 = make_s5_core("s5")(u2, lam, wd, wr)
    ys = _from_scan_order(y2[0])[m:] + _from_scan_order(y2[1])[:t]
    gl = make_gelu_in("gelu")(hx, ys, small["ssm_d_full"][None])
    zz = lin("col", F32, "glu_w", gl, "ssm_w_glu", ("ffn_w2_1",))
    yx = make_glu("glu")(zz, small["ssm_b_glu_full"][None])
    xs, h = make_residual_norm(t, "mix1_to_ffn1")(z[:t], yx, grp(1, 2, 1), small["norm_ffn"][1][None], grp(1, 4, 1), grp(1, 3, 1))
    xs = make_gated_residual(t, "res_ffn1")(xs, ffn(1, h, (), ()), grp(1, 5, 1))
    return make_final_loss("loss_head")(xs, small["norm_final"][None], target)[0, 0]


_WEIGHTS = ['c_ctx', 'ada_w', 'ada_b', 'norm_mix', 'norm_ffn', 'ffn_w1', 'ffn_w3', 'ffn_w2', 'attn_w_in', 'attn_w_out',
            'attn_rpb', 'attn_sink', 'ssm_a_re', 'ssm_a_im', 'ssm_log_dt', 'ssm_b_re', 'ssm_b_im', 'ssm_c_re', 'ssm_c_im',
            'ssm_d', 'ssm_w_glu', 'ssm_b_glu', 'norm_final']
_LOCAL_SMALL = ['norm_mix', 'norm_ffn', 'attn_rpb', 'attn_sink', 'ssm_a_re', 'ssm_a_im', 'ssm_log_dt', 'ssm_b_re',
                'ssm_b_im', 'ssm_c_re', 'ssm_c_im', 'norm_final']
_MOD_ROWS = 16


def _gather_chip_vector(v, name):
    g = all_gather8(v[None], name)
    return g[0::2, 0, :].reshape(-1)


def kernel(x, c, ctx, c_ctx, ada_w, ada_b, norm_mix, norm_ffn, ffn_w1, ffn_w3, ffn_w2, attn_w_in, attn_w_out, attn_rpb, attn_sink, ssm_a_re, ssm_a_im, ssm_log_dt, ssm_b_re, ssm_b_im, ssm_c_re, ssm_c_im, ssm_d, ssm_w_glu, ssm_b_glu, norm_final, loss_target, m_c_ctx, m_ada_w, m_ada_b, m_norm_mix, m_norm_ffn, m_ffn_w1, m_ffn_w3, m_ffn_w2, m_attn_w_in, m_attn_w_out, m_attn_rpb, m_attn_sink, m_ssm_a_re, m_ssm_a_im, m_ssm_log_dt, m_ssm_b_re, m_ssm_b_im, m_ssm_c_re, m_ssm_c_im, m_ssm_d, m_ssm_w_glu, m_ssm_b_glu, m_norm_final, v_c_ctx, v_ada_w, v_ada_b, v_norm_mix, v_norm_ffn, v_ffn_w1, v_ffn_w3, v_ffn_w2, v_attn_w_in, v_attn_w_out, v_attn_rpb, v_attn_sink, v_ssm_a_re, v_ssm_a_im, v_ssm_log_dt, v_ssm_b_re, v_ssm_b_im, v_ssm_c_re, v_ssm_c_im, v_ssm_d, v_ssm_w_glu, v_ssm_b_glu, v_norm_final):
    env = dict(locals())
    w = {n: env[n] for n in _WEIGHTS}
    mom = {n: env["m_" + n] for n in _WEIGHTS}
    var = {n: env["v_" + n] for n in _WEIGHTS}
    _, t, d = x.shape
    m = ctx.shape[1]
    px, py, pc = _my_pos()
    s_me = 2 * px + py
    a_w =attn_rpb.shape[1] * HEAD_DIM
    bq_w = attn_sink.shape[1] * HEAD_DIM
    bkv_w = (4 * attn_w_in.shape[2] - 3 * a_w - bq_w) // 2
    dims = dict(t=t, m=m, d=d, a_w=a_w, bq_w=bq_w, bkv_w=bkv_w)
    n_layers = ada_w.shape[0]
    ada_cols = ada_w.shape[2]

    big = {"attn_w_in": gather_weight(attn_w_in[0], "ag_attn_in")}
    pending = {"attn_w_out": _ag_prepare(attn_w_out[0]), "ssm_w_glu": _ag_prepare(ssm_w_glu[0])}
    for l in range(n_layers):
        pending.update({f"ffn_w1_{l}": _ag_prepare(ffn_w1[l]), f"ffn_w3_{l}": _ag_prepare(ffn_w3[l]),
                        f"ffn_w2_{l}": _ag_prepare(ffn_w2[l])})
    small ={n: w[n] for n in _LOCAL_SMALL}
    small["ssm_d_full"] = _gather_chip_vector(ssm_d[0], "ag_ssm_d")
    small["ssm_b_glu_full"] = _gather_chip_vector(ssm_b_glu[0], "ag_b_glu")

    c_all = all_gather8(c, "ag_c")[:, 0, :]
    cond = jnp.concatenate([c_all, c_ctx[None], jnp.zeros((_MOD_ROWS - N_DEV - 1, d), F32)], axis=0)
    sig = jax.nn.sigmoid(cond)
    silu_c = (cond * sig).astype(BF16)
    mods_shard = mm_nn(silu_c, ada_w, "col", F32, "ada_fwd").reshape(_MOD_ROWS, n_layers, ada_cols).transpose(1, 0, 2)
    send = jnp.stack([jnp.stack([mods_shard[:, tgt], mods_shard[:, N_DEV]], axis=1).reshape(2 * n_layers, ada_cols)
                      for tgt in range(N_DEV)])
    plan = [((0, 0, 0), lambda me, peer: _lin(me), lambda me, peer: 2 * me[0] + me[1])]
    plan += [(f, lambda me, peer: _lin(peer), lambda me, peer: 2 * me[0] + me[1]) for f in _CHIP_FLIPS]
    got = xchg(send, 4, plan, "mods_xchg")
    mods = got.reshape(4, n_layers, 2, ada_cols).transpose(1, 2, 0, 3).reshape(n_layers, 2, 4 * ada_cols)
    mods = (mods + ada_b[:, None, :]).reshape(n_layers, 2, 6, d)

    shards = {"attn_w_in": attn_w_in[0], "attn_w_out": attn_w_out[0], "ssm_w_glu": ssm_w_glu[0]}
    for l in range(n_layers):
        shards.update({f"ffn_w1_{l}": ffn_w1[l], f"ffn_w3_{l}": ffn_w3[l], f"ffn_w2_{l}": ffn_w2[l]})

    def local(xx, mods_, small_, shards_):
        return _local_loss(xx, ctx[0], loss_target[0], mods_, small_, big, pending, shards_, dims)

    loss_local, vjp = jax.vjp(local, x[0], mods, small, shards)
    g_x, g_mods, g_small, g_shards = vjp(jnp.ones((), F32))
    loss = lax.psum(loss_local, ("x", "y", "c"))
    grads = {"attn_w_in": g_shards["attn_w_in"][None], "attn_w_out": g_shards["attn_w_out"][None],
             "ssm_w_glu": g_shards["ssm_w_glu"][None]}
    for n in ("ffn_w1", "ffn_w3", "ffn_w2"):
        grads[n] = jnp.stack([g_shards[f"{n}_{l}"] for l in range(n_layers)])

    gm = all_gather8(g_mods.reshape(2 * n_layers, 6 * d), "ag_dmods").reshape(N_DEV, n_layers, 2, 6 * d)
    ctx_row = gm[0, :, 1]
    for j in range(1, N_DEV):
        ctx_row = ctx_row + gm[j, :, 1]
    dm16 = jnp.concatenate([gm[:, :, 0].transpose(1, 0, 2), ctx_row[:, None], jnp.zeros((n_layers, _MOD_ROWS - N_DEV - 1, 6 * d), F32)], axis=1)
    grads["ada_b"] = jnp.sum(dm16, axis=1)
    dm_mine = lax.dynamic_slice_in_dim(dm16, s_me * ada_cols, ada_cols, axis=2).astype(BF16)
    grads["ada_w"] = jnp.stack([mm_tn(silu_c, dm_mine[l], (1, d, ada_cols), "col", F32, f"ada_dw{l}")[0] for l in range(n_layers)])
    dsilu = mm_nt(dm_mine.transpose(1, 0, 2).reshape(_MOD_ROWS, n_layers * ada_cols), ada_w, "col", F32, "ada_dc")
    dsilu_ctx = 0.5 * dsilu[N_DEV]

    packed = [(n, g_small[n]) for n in _LOCAL_SMALL] + [("ssm_d", g_small["ssm_d_full"]), ("ssm_b_glu", g_small["ssm_b_glu_full"]),
                                                        ("c_ctx", dsilu_ctx)]
    flat = all_reduce8(jnp.concatenate([a.reshape(-1) for _, a in packed]), "ar_small")
    off = 0
    for n, a in packed:
        grads[n] = flat[off:off + a.size].reshape(a.shape)
        off += a.size
    sig_ctx = jax.nn.sigmoid(c_ctx)
    grads["c_ctx"] = grads["c_ctx"] * (sig_ctx * (1.0 + c_ctx * (1.0 - sig_ctx)))
    grads["ssm_d"] = lax.dynamic_slice_in_dim(grads["ssm_d"], s_me * ssm_d.shape[1], ssm_d.shape[1])[None]
    grads["ssm_b_glu"] = lax.dynamic_slice_in_dim(grads["ssm_b_glu"], s_me * ssm_b_glu.shape[1], ssm_b_glu.shape[1])[None]

    delta, new_m, new_v = {}, {}, {}
    for n in _WEIGHTS:
        delta[n], new_m[n], new_v[n] = adamw(w[n], grads[n], mom[n], var[n], "adamw_" + n)
    return (loss, g_x[None], *[grads[n] for n in _WEIGHTS], *[delta[n] for n in _WEIGHTS],
            *[new_m[n] for n in _WEIGHTS], *[new_v[n] for n in _WEIGHTS])
```

```python
import functools
import math

import numpy as np
import jax
import jax.numpy as jnp
from jax import lax
from jax.experimental import pallas as pl
from jax.experimental.pallas import tpu as pltpu

F32 = jnp.float32
BF16 = jnp.bfloat16
MESH = pl.DeviceIdType.MESH

HEAD_DIM = 128
GRID_W = 64
NA_ROWS = 8
NA_COLS = 16
SW_RADIUS = 128
ATTN_BLOCK = 128
ROPE_BASE = 10000.0
SSM_GROUP = 16
SSM_STATE = 64
SSM_TILE_GROUPS = 8
SCAN_BLOCKS = 8
EPS = 1e-6
NEG_INF = -1e30
ADAM_LR, ADAM_B1, ADAM_B2, ADAM_EPS, ADAM_WD, ADAM_STEP = 0.001, 0.9, 0.999, 1e-08, 0.01, 10
VMEM_LIMIT_BYTES = 56 * 1024 * 1024
N_DEV = 8


def _cparams(*sem):
    return pltpu.CompilerParams(dimension_semantics=tuple(sem) if sem else None, vmem_limit_bytes=VMEM_LIMIT_BYTES)


def _pick(n, cands):
    for c in cands:
        if n % c == 0:
            return c
    return n


def _dot_nn(a, b):
    return jnp.dot(a, b, preferred_element_type=F32)


def _dot_nt(a, b):
    return lax.dot_general(a, b, (((1,), (1,)), ((), ())), preferred_element_type=F32)


def _dot_tn(a, b):
    return lax.dot_general(a, b, (((0,), (0,)), ((), ())), preferred_element_type=F32)


def _mm_call(name, grid, ins, in_specs, o_specs, out_sds, acc_shape, step, carry=None):
    nk = grid[2]
    nb = 0 if carry is None else len(carry)
    ni, no = len(ins), len(out_sds)
    assert nk == 1 or no == 1

    def body(*refs):
        in_refs = refs[:ni]
        o_refs = refs[ni + nb:ni + nb + no]
        rest = refs[ni + nb + no + nb:]
        acc_ref = rest[0] if nk > 1 else None
        o_ref = o_refs[0]
        if carry is not None:
            bufs = refs[ni + nb + no:ni + nb + no + nb]
            sems = rest[-2 * nb:]
            ids = [pl.program_id(ax) for ax in range(3)]
            first = (ids[0] == 0) & (ids[1] == 0) & (ids[2] == 0)
            last = (ids[0] == grid[0] - 1) & (ids[1] == grid[1] - 1) & (ids[2] == grid[2] - 1)
            copies = []
            for q, (_, plan) in enumerate(carry):
                copies += _remote_copies(bufs[q], bufs[q], sems[2 * q], sems[2 * q + 1], plan)

            @pl.when(first)
            def _():
                for cp in copies:
                    cp.start()

        kk = pl.program_id(2)
        if nk == 1:
            for ref, val in zip(o_refs, step(*in_refs)):
                ref[...] = val.astype(ref.dtype)
        else:
            @pl.when(kk == 0)
            def _():
                acc_ref[...] = step(*in_refs)[0]

            @pl.when(kk > 0)
            def _():
                acc_ref[...] = step(*in_refs)[0] + acc_ref[...]

            @pl.when(kk == nk - 1)
            def _():
                o_ref[...] = acc_ref[...].astype(o_ref.dtype)

        if carry is not None:
            @pl.when(last)
            def _():
                for cp in copies:
                    cp.wait_recv()
                for cp in copies:
                    cp.wait_send()

    scratch = [pltpu.VMEM(acc_shape, F32)] if nk > 1 else []
    if carry is None:
        return pl.pallas_call(
            body, grid=grid, in_specs=list(in_specs), out_specs=list(o_specs), out_shape=list(out_sds),
            scratch_shapes=scratch, name=name, compiler_params=_cparams("parallel", "parallel", "arbitrary"))(*ins)
    any_spec = pl.BlockSpec(memory_space=pl.ANY)
    for _, plan in carry:
        scratch += [pltpu.SemaphoreType.DMA((len(plan),)), pltpu.SemaphoreType.DMA((len(plan),))]
    return pl.pallas_call(
        body, grid=grid, in_specs=list(in_specs) + [any_spec] * nb, out_specs=list(o_specs) + [any_spec] * nb,
        out_shape=list(out_sds) + [jax.ShapeDtypeStruct(buf.shape, buf.dtype) for buf, _ in carry],
        scratch_shapes=scratch, input_output_aliases={ni + q: no + q for q in range(nb)}, name=name,
        compiler_params=_cparams("arbitrary", "arbitrary", "arbitrary"))(*ins, *[buf for buf, _ in carry])


_ROW_TILES = (768, 512, 256, 128, 64, 32, 16, 8)
_K_TILES = (2048, 1408, 1024, 512, 256, 128)
_CONTRACT_TILES = (1408, 1024, 768, 512, 256, 128)
_WEIGHT_BLOCK_BYTES = 12 * 1024 * 1024
_FUSED_ROW_TILES =(528, 512, 384, 256, 128, 64, 32, 16, 8)


def mm_nn(a, w3, kind, out_dtype, name, carry=None):
    r = a.shape[0]
    s, d1, d2 = w3.shape
    tm = _pick(r, _ROW_TILES)
    if kind == "col":
        tn = d2 if d1 * d2 * w3.dtype.itemsize <= _WEIGHT_BLOCK_BYTES else _pick(d2, (1024, 512, 256, 128))
        nn = d2 // tn
        grid = (s * nn, r // tm, 1)
        a_spec = pl.BlockSpec((tm, d1), lambda j, i, k: (i, 0))
        b_spec = pl.BlockSpec((None, d1, tn), lambda j, i, k: (j // nn, 0, j % nn))
        o_spec = pl.BlockSpec((tm, tn), lambda j, i, k: (i, j))
        n = s * d2

        def step(a_ref, b_ref):
            return (_dot_nn(a_ref[...], b_ref[...].astype(a_ref.dtype)),)
    else:
        tn = d2 if d2 <= 1024 else _pick(d2, (1024, 512, 256, 128))
        grid = (d2 // tn, r // tm, 1)
        a_spec = pl.BlockSpec((tm, s * d1), lambda j, i, k: (i, 0))
        b_spec = pl.BlockSpec((s, d1, tn), lambda j, i, k: (0, 0, j))
        o_spec = pl.BlockSpec((tm, tn), lambda j, i, k: (i, j))
        n = d2

        def step(a_ref, b_ref):
            p = _dot_nn(a_ref[:, :d1], b_ref[0])
            for q in range(1, s):
                p = _dot_nn(a_ref[:, q * d1:(q + 1) * d1], b_ref[q]) + p
            return (p,)
    out = _mm_call(name, grid, [a, w3], [a_spec, b_spec], [o_spec], [jax.ShapeDtypeStruct((r, n), out_dtype)], None, step, carry)
    return out if carry else out[0]


def mm_nt(dy, w3, kind, out_dtype, name, carry=None, addend=None):
    r = dy.shape[0]
    s, d1, d2 = w3.shape
    tm = _pick(r, _ROW_TILES)
    if kind == "col":
        fits = [c for c in (1024, 512, 256, 128) if d1 % c == 0 and s * c * d2 * w3.dtype.itemsize <= _WEIGHT_BLOCK_BYTES]
        tko = d1 if d1 <= 1024 and s * d1 * d2 * w3.dtype.itemsize <= _WEIGHT_BLOCK_BYTES else fits[0]
        grid = (d1 // tko, r // tm, 1)
        a_spec = pl.BlockSpec((tm, s * d2), lambda j, i, k: (i, 0))
        b_spec = pl.BlockSpec((s, tko, d2), lambda j, i, k: (0, j, 0))
        o_spec = pl.BlockSpec((tm, tko), lambda j, i, k: (i, j))
        kdim = d1

        def step(a_ref, b_ref, *more):
            p = _dot_nt(a_ref[:, :d2], b_ref[0].astype(a_ref.dtype))
            for q in range(1, s):
                p = _dot_nt(a_ref[:, q * d2:(q + 1) * d2], b_ref[q].astype(a_ref.dtype)) + p
            return (p + more[0][...].astype(F32),) if more else (p,)
    else:
        grid = (s, r // tm, 1)
        a_spec = pl.BlockSpec((tm, d2), lambda j, i, k: (i, 0))
        b_spec = pl.BlockSpec((None, d1, d2), lambda j, i, k: (j, 0, 0))
        o_spec = pl.BlockSpec((tm, d1), lambda j, i, k: (i, j))
        kdim = s * d1

        def step(a_ref, b_ref, *more):
            p = _dot_nt(a_ref[...], b_ref[...])
            return (p + more[0][...].astype(F32),) if more else (p,)
    ins, specs = [dy, w3], [a_spec, b_spec]
    if addend is not None:
        ins, specs = ins + [addend], specs + [o_spec]
    out = _mm_call(name, grid, ins, specs, [o_spec], [jax.ShapeDtypeStruct((r, kdim), out_dtype)], None, step, carry)
    return out if carry else out[0]


def mm_tn(a, dy, w_shape, kind, out_dtype, name):
    s, d1, d2 = w_shape
    r = dy.shape[0]
    tr = _pick(r, _CONTRACT_TILES)
    if kind == "col":
        tkk = d1 if d1 <= 1024 else _pick(d1, (1024, 512, 256, 128))
        grid = (s * (d1 // tkk), 1, r // tr)
        nkk = d1 // tkk
        a_spec = pl.BlockSpec((tr, tkk), lambda j, i, k: (k, j % nkk))
        b_spec = pl.BlockSpec((tr, d2), lambda j, i, k: (k, j // nkk))
        o_spec = pl.BlockSpec((None, tkk, d2), lambda j, i, k: (j // nkk, j % nkk, 0))
        acc = (tkk, d2)
    else:
        tn = d2 if d2 <= 1024 else _pick(d2, (1024, 512, 256, 128))
        nn = d2 // tn
        grid = (s * nn, 1, r // tr)
        a_spec = pl.BlockSpec((tr, d1), lambda j, i, k: (k, j // nn))
        b_spec = pl.BlockSpec((tr, tn), lambda j, i, k: (k, j % nn))
        o_spec = pl.BlockSpec((None, d1, tn), lambda j, i, k: (j // nn, 0, j % nn))
        acc = (d1, tn)

    def step(a_ref, b_ref):
        return (_dot_tn(a_ref[...], b_ref[...]),)
    return _mm_call(name, grid, [a, dy], [a_spec, b_spec], [o_spec], [jax.ShapeDtypeStruct(w_shape, out_dtype)], acc, step)[0]


def mm_ffn_in(h, w1, w3, name, carry=None):
    r = h.shape[0]
    s, d1, d2 = w1.shape
    tm = _pick(r, _FUSED_ROW_TILES)
    a_spec = pl.BlockSpec((tm, d1), lambda j, i, k: (i, 0))
    b_spec = pl.BlockSpec((None, d1, d2), lambda j, i, k: (j, 0, 0))
    o_spec = pl.BlockSpec((tm, d2), lambda j, i, k: (i, j))
    sds = jax.ShapeDtypeStruct((r, s * d2), BF16)

    def step(a_ref, b1_ref, b3_ref):
        p1 = _dot_nn(a_ref[...], b1_ref[...])
        p3 = _dot_nn(a_ref[...], b3_ref[...])
        return p1, p3, _silu(p1) * p3

    return _mm_call(name, (s, r // tm, 1), [h, w1, w3], [a_spec, b_spec, b_spec], [o_spec] * 3, [sds] * 3, None, step, carry)


def mm_ffn_back(dy, w2, h1, h3, name, carry=None):
    r = dy.shape[0]
    s, d1, d2 = w2.shape
    tm = _pick(r, _FUSED_ROW_TILES)
    a_spec = pl.BlockSpec((tm, d2), lambda j, i, k: (i, 0))
    b_spec = pl.BlockSpec((None, d1, d2), lambda j, i, k: (j, 0, 0))
    o_spec = pl.BlockSpec((tm, d1), lambda j, i, k: (i, j))
    sds = jax.ShapeDtypeStruct((r, s * d1), BF16)

    def step(a_ref, b_ref, h1_ref, h3_ref):
        g = _dot_nt(a_ref[...], b_ref[...])
        a1 = h1_ref[...].astype(F32)
        sg = jax.nn.sigmoid(a1)
        return g * h3_ref[...].astype(F32) * (sg * (1.0 + a1 * (1.0 - sg))), g * a1 * sg

    return _mm_call(name, (s, r // tm, 1), [dy, w2, h1, h3], [a_spec, b_spec, o_spec, o_spec], [o_spec] * 2, [sds] * 2,
                    None, step, carry)


def make_linear(kind, out_dtype, name):
    def run(a, w3, gathering):
        carry = [(buf, _ag_plan()) for buf in gathering] or None
        out = mm_nn(a, w3, kind, out_dtype, name + "_fwd", carry)
        if not carry:
            return out, ()
        return out[0], _ag_finish_all(out[1:], name)

    @jax.custom_vjp
    def linear(a, w3, w_shard, gathering):
        return run(a, w3, gathering)

    def fwd(a, w3, w_shard, gathering):
        return run(a, w3, gathering), (a, w3, w_shard.shape, len(gathering))

    def bwd(res, cts):
        a, w3, shard_shape, n_gathering = res
        dyb = cts[0].astype(BF16)
        dw = mm_tn(a, dyb, w3.shape, kind, BF16, name + "_dw")
        buf, plan = _rs_begin(dw, name)
        da, buf = mm_nt(dyb, w3, kind, a.dtype, name + "_dx", carry=[(buf, plan)])
        return da, None, _rs_finish(buf, name).reshape(shard_shape), (None,) * n_gathering

    linear.defvjp(fwd, bwd)
    return linear


def _ag_finish_all(bufs, name):
    return tuple(_ag_finish(buf, f"{name}_gathered{q}") for q, buf in enumerate(bufs))


def make_ffn(name, w2_pending):
    def run(h, w1, w3, w2, gathering):
        g_in, g_out = gathering
        if w2_pending:
            g_in = (w2,) + tuple(g_in)
        carry = [(buf, _ag_plan()) for buf in g_in] or None
        out = mm_ffn_in(h, w1, w3, name + "_in", carry)
        h1, h3, act = out[:3]
        done_in = _ag_finish_all(out[3:], name + "_in")
        if w2_pending:
            w2, done_in = done_in[0], done_in[1:]
        carry2 = [(buf, _ag_plan()) for buf in g_out] or None
        y = mm_nn(act, w2, "row", F32, name + "_out", carry2)
        done = (done_in, _ag_finish_all(y[1:], name + "_out") if carry2 else ())
        return (y[0] if carry2 else y), done, (h1, h3, act, w2)

    @jax.custom_vjp
    def ffn(h, w1, w3, w2, s1, s3, s2, gathering):
        return run(h, w1, w3, w2, gathering)[:2]

    def fwd(h, w1, w3, w2, s1, s3, s2, gathering):
        y, done, (h1, h3, act, w2) = run(h, w1, w3, w2, gathering)
        return (y, done), (h, w1, w3, w2, h1, h3, act, s1.shape, s3.shape, s2.shape, tuple(len(g) for g in gathering))

    def bwd(res, cts):
        h, w1, w3, w2, h1, h3, act, shape1, shape3, shape2, n_gathering = res
        dyb = cts[0].astype(BF16)
        buf2, plan2 = _rs_begin(mm_tn(act, dyb, w2.shape, "row", BF16, name + "_dw2"), name + "_w2")
        dh1, dh3, buf2 = mm_ffn_back(dyb, w2, h1, h3, name + "_back", carry=[(buf2, plan2)])
        buf1, plan1 = _rs_begin(mm_tn(h, dh1, w1.shape, "col", BF16, name + "_dw1"), name + "_w1")
        buf3, plan3 = _rs_begin(mm_tn(h, dh3, w3.shape, "col", BF16, name + "_dw3"), name + "_w3")
        dh, buf1 = mm_nt(dh1, w1, "col", h.dtype, name + "_dx1", carry=[(buf1, plan1)])
        dh, buf3 = mm_nt(dh3, w3, "col", h.dtype, name + "_dx3", carry=[(buf3, plan3)], addend=dh)
        grads = [_rs_finish(b, name + n).reshape(sh) for b, n, sh in
                 ((buf1, "_w1", shape1), (buf3, "_w3", shape3), (buf2, "_w2", shape2))]
        return (dh, None, None, None, *grads, tuple((None,) * n for n in n_gathering))

    ffn.defvjp(fwd, bwd)
    return ffn


def _row_tile(r, t0, d):
    cap = max(8, (2 * 1024 * 1024) // (4 * d))
    cands = [t for t in (1024, 512, 256, 128, 64, 32, 16, 8) if t <= cap]
    for t in cands:
        if r % t == 0 and t0 % t == 0:
            return t
    raise ValueError("no row tile")


def _grp_spec(d, nb0):
    return pl.BlockSpec((None, 1, d), lambda i: (i // nb0, 0, 0))


def _norm_mod_fwd(z, g, scale, shift, t0, name):
    r, d = z.shape
    tr = _row_tile(r, t0, d)
    nb0 = t0 // tr

    def body(z_ref, g_ref, sc_ref, sh_ref, o_ref):
        zz = z_ref[...]
        rstd = lax.rsqrt(jnp.mean(zz * zz, axis=-1, keepdims=True) + EPS)
        y = zz * rstd * g_ref[...]
        o_ref[...] = (y * (1.0 + sc_ref[...]) + sh_ref[...]).astype(o_ref.dtype)

    return pl.pallas_call(
        body, grid=(r // tr,),
        in_specs=[pl.BlockSpec((tr, d), lambda i: (i, 0)), pl.BlockSpec((1, d), lambda i: (0, 0)),
                  _grp_spec(d, nb0), _grp_spec(d, nb0)],
        out_specs=pl.BlockSpec((tr, d), lambda i: (i, 0)),
        out_shape=jax.ShapeDtypeStruct((r, d), BF16), name=name, compiler_params=_cparams("parallel"))(z, g, scale, shift)


def _norm_mod_bwd(z, g, scale, dh, t0, name):
    r, d = z.shape
    ng = scale.shape[0]
    tr = _row_tile(r, t0, d)
    nb0 = t0 // tr

    def body(z_ref, g_ref, sc_ref, dh_ref, dz_ref, dg_ref, dsc_ref, dsh_ref):
        i = pl.program_id(0)
        zz = z_ref[...]
        gg = g_ref[...]
        rstd = lax.rsqrt(jnp.mean(zz * zz, axis=-1, keepdims=True) + EPS)
        zhat = zz * rstd
        dhh = dh_ref[...].astype(F32)
        dy = dhh * (1.0 + sc_ref[...])
        dyg = dy * gg
        dz_ref[...] = rstd * (dyg - zhat * jnp.mean(dyg * zhat, axis=-1, keepdims=True))

        @pl.when(i == 0)
        def _():
            dg_ref[...] = jnp.zeros_like(dg_ref)

        @pl.when((i == 0) | (i == nb0))
        def _():
            dsc_ref[...] = jnp.zeros_like(dsc_ref)
            dsh_ref[...] = jnp.zeros_like(dsh_ref)

        dg_ref[...] += jnp.sum(dy * zhat, axis=0, keepdims=True)
        dsc_ref[...] += jnp.sum(dhh * (zhat * gg), axis=0, keepdims=True)
        dsh_ref[...] += jnp.sum(dhh, axis=0, keepdims=True)

    return pl.pallas_call(
        body, grid=(r // tr,),
        in_specs=[pl.BlockSpec((tr, d), lambda i: (i, 0)), pl.BlockSpec((1, d), lambda i: (0, 0)),
                  _grp_spec(d, nb0), pl.BlockSpec((tr, d), lambda i: (i, 0))],
        out_specs=[pl.BlockSpec((tr, d), lambda i: (i, 0)), pl.BlockSpec((1, d), lambda i: (0, 0)),
                   _grp_spec(d, nb0), _grp_spec(d, nb0)],
        out_shape=[jax.ShapeDtypeStruct((r, d), F32), jax.ShapeDtypeStruct((1, d), F32),
                   jax.ShapeDtypeStruct((ng, 1, d), F32), jax.ShapeDtypeStruct((ng, 1, d), F32)],
        name=name, compiler_params=_cparams("arbitrary"))(z, g, scale, dh)


def make_norm_mod(t0, name):
    @jax.custom_vjp
    def f(z, g, scale, shift):
        return _norm_mod_fwd(z, g, scale, shift, t0, name + "_fwd")

    def fwd(z, g, scale, shift):
        return _norm_mod_fwd(z, g, scale, shift, t0, name + "_fwd"), (z, g, scale)

    def bwd(res, dh):
        z, g, scale = res
        dz, dg, dsc, dsh = _norm_mod_bwd(z, g, scale, dh, t0, name + "_bwd")
        return dz, dg, dsc, dsh

    f.defvjp(fwd, bwd)
    return f


def _gated_fwd(z, y, gate, t0, name):
    r, d = z.shape
    tr = _row_tile(r, t0, d)
    nb0 = t0 // tr

    def body(z_ref, y_ref, g_ref, o_ref):
        o_ref[...] = z_ref[...] + g_ref[...] * y_ref[...].astype(F32)

    return pl.pallas_call(
        body, grid=(r // tr,),
        in_specs=[pl.BlockSpec((tr, d), lambda i: (i, 0)), pl.BlockSpec((tr, d), lambda i: (i, 0)), _grp_spec(d, nb0)],
        out_specs=pl.BlockSpec((tr, d), lambda i: (i, 0)),
        out_shape=jax.ShapeDtypeStruct((r, d), F32), name=name, compiler_params=_cparams("parallel"))(z, y, gate)


def _gated_bwd(y, gate, dzn, t0, name):
    r, d = y.shape
    ng = gate.shape[0]
    tr = _row_tile(r, t0, d)
    nb0 = t0 // tr

    def body(y_ref, g_ref, dz_ref, dy_ref, dg_ref):
        i = pl.program_id(0)
        dzz = dz_ref[...]
        dy_ref[...] = (g_ref[...] * dzz).astype(dy_ref.dtype)

        @pl.when((i == 0) | (i == nb0))
        def _():
            dg_ref[...] = jnp.zeros_like(dg_ref)

        dg_ref[...] += jnp.sum(dzz * y_ref[...].astype(F32), axis=0, keepdims=True)

    return pl.pallas_call(
        body, grid=(r // tr,),
        in_specs=[pl.BlockSpec((tr, d), lambda i: (i, 0)), _grp_spec(d, nb0), pl.BlockSpec((tr, d), lambda i: (i, 0))],
        out_specs=[pl.BlockSpec((tr, d), lambda i: (i, 0)), _grp_spec(d, nb0)],
        out_shape=[jax.ShapeDtypeStruct((r, d), y.dtype), jax.ShapeDtypeStruct((ng, 1, d), F32)],
        name=name, compiler_params=_cparams("arbitrary"))(y, gate, dzn)


def make_gated_residual(t0, name):
    @jax.custom_vjp
    def f(z, y, gate):
        return _gated_fwd(z, y, gate, t0, name + "_fwd")

    def fwd(z, y, gate):
        return _gated_fwd(z, y, gate, t0, name + "_fwd"), (y, gate)

    def bwd(res, dzn):
        y, gate = res
        dy, dgate = _gated_bwd(y, gate, dzn, t0, name + "_bwd")
        return dzn, dy, dgate

    f.defvjp(fwd, bwd)
    return f


def make_residual_norm(t0, name):
    def row_spec(tr, d):
        return pl.BlockSpec((tr, d), lambda i: (i, 0))

    def fwd_call(z, y, gate, g, scale, shift):
        r, d = z.shape
        tr = _row_tile(r, t0, d)
        nb0 = t0 // tr

        def body(z_ref, y_ref, gt_ref, g_ref, sc_ref, sh_ref, zn_ref, h_ref):
            zz = z_ref[...] + gt_ref[...] * y_ref[...].astype(F32)
            zn_ref[...] = zz
            rstd = lax.rsqrt(jnp.mean(zz * zz, axis=-1, keepdims=True) + EPS)
            h_ref[...] = (zz * rstd * g_ref[...] * (1.0 + sc_ref[...]) + sh_ref[...]).astype(h_ref.dtype)

        grp = _grp_spec(d, nb0)
        return pl.pallas_call(
            body, grid=(r // tr,),
            in_specs=[row_spec(tr, d), row_spec(tr, d), grp, pl.BlockSpec((1, d), lambda i: (0, 0)), grp, grp],
            out_specs=[row_spec(tr, d), row_spec(tr, d)],
            out_shape=[jax.ShapeDtypeStruct((r, d), F32), jax.ShapeDtypeStruct((r, d), BF16)],
            name=name + "_fwd", compiler_params=_cparams("parallel"))(z, y, gate, g, scale, shift)

    def bwd_call(zn, y, gate, g, scale, dzn, dh):
        r, d = zn.shape
        ng = scale.shape[0]
        tr = _row_tile(r, t0, d)
        nb0 = t0 // tr

        def body(zn_ref, y_ref, gt_ref, g_ref, sc_ref, dzn_ref, dh_ref, dz_ref, dy_ref, dgt_ref, dg_ref, dsc_ref, dsh_ref):
            i = pl.program_id(0)
            zz = zn_ref[...]
            gg = g_ref[...]
            rstd = lax.rsqrt(jnp.mean(zz * zz, axis=-1, keepdims=True) + EPS)
            zhat = zz * rstd
            dhh = dh_ref[...].astype(F32)
            dyn = dhh * (1.0 + sc_ref[...])
            dyg = dyn * gg
            dz = dzn_ref[...] + rstd * (dyg - zhat * jnp.mean(dyg * zhat, axis=-1, keepdims=True))
            dz_ref[...] = dz
            dy_ref[...] = (gt_ref[...] * dz).astype(dy_ref.dtype)

            @pl.when(i == 0)
            def _():
                dg_ref[...] = jnp.zeros_like(dg_ref)

            @pl.when((i == 0) | (i == nb0))
            def _():
                dgt_ref[...] = jnp.zeros_like(dgt_ref)
                dsc_ref[...] = jnp.zeros_like(dsc_ref)
                dsh_ref[...] = jnp.zeros_like(dsh_ref)

            dgt_ref[...] += jnp.sum(dz * y_ref[...].astype(F32), axis=0, keepdims=True)
            dg_ref[...] += jnp.sum(dyn * zhat, axis=0, keepdims=True)
            dsc_ref[...] += jnp.sum(dhh * (zhat * gg), axis=0, keepdims=True)
            dsh_ref[...] += jnp.sum(dhh, axis=0, keepdims=True)

        grp = _grp_spec(d, nb0)
        vec = pl.BlockSpec((1, d), lambda i: (0, 0))
        gsds = jax.ShapeDtypeStruct((ng, 1, d), F32)
        return pl.pallas_call(
            body, grid=(r // tr,),
            in_specs=[row_spec(tr, d), row_spec(tr, d), grp, vec, grp, row_spec(tr, d), row_spec(tr, d)],
            out_specs=[row_spec(tr, d), row_spec(tr, d), grp, vec, grp, grp],
            out_shape=[jax.ShapeDtypeStruct((r, d), F32), jax.ShapeDtypeStruct((r, d), y.dtype), gsds,
                       jax.ShapeDtypeStruct((1, d), F32), gsds, gsds],
            name=name + "_bwd", compiler_params=_cparams("arbitrary"))(zn, y, gate, g, scale, dzn, dh)

    @jax.custom_vjp
    def f(z, y, gate, g, scale, shift):
        return tuple(fwd_call(z, y, gate, g, scale, shift))

    def fwd(z, y, gate, g, scale, shift):
        zn, h = fwd_call(z, y, gate, g, scale, shift)
        return (zn, h), (zn, y, gate, g, scale)

    def bwd(res, cts):
        zn, y, gate, g, scale = res
        dz, dy, dgate, dg, dsc, dsh = bwd_call(zn, y, gate, g, scale, cts[0], cts[1])
        return dz, dy, dgate, dg, dsc, dsh

    f.defvjp(fwd, bwd)
    return f


def _ew_call(name, body, ins, outs_sds, r, widths_in, widths_out, tr, extra_in=(), extra_specs=(), sem="parallel"):
    in_specs = [pl.BlockSpec((tr, w), lambda i: (i, 0)) for w in widths_in] + list(extra_specs)
    out_specs = [pl.BlockSpec((tr, w), lambda i: (i, 0)) if w is not None else pl.BlockSpec(s.shape, lambda i: (0,) * len(s.shape))
                 for w, s in zip(widths_out, outs_sds)]
    return pl.pallas_call(body, grid=(r // tr,), in_specs=in_specs, out_specs=out_specs, out_shape=outs_sds,
                          name=name, compiler_params=_cparams(sem))(*ins, *extra_in)


def _silu(x):
    return x * jax.nn.sigmoid(x)


_GELU_C = math.sqrt(2.0 / math.pi)


def _gelu_and_grad(y):
    inner = _GELU_C * (y + 0.044715 * y * y * y)
    t = jnp.tanh(inner)
    val = 0.5 * y * (1.0 + t)
    grad = 0.5 * (1.0 + t) + 0.5 * y * (1.0 - t * t) * _GELU_C * (1.0 + 3 * 0.044715 * y * y)
    return val, grad


def make_gelu_in(name):
    def fwd_call(u, ys, dsk):
        r, d = u.shape
        tr = _row_tile(r, r, d)

        def body(u_ref, y_ref, d_ref, o_ref):
            y = d_ref[...] * u_ref[...].astype(F32) + y_ref[...]
            o_ref[...] = _gelu_and_grad(y)[0].astype(o_ref.dtype)

        return _ew_call(name + "_fwd", body, (u, ys), [jax.ShapeDtypeStruct((r, d), BF16)], r, (d, d), (d,), tr,
                        extra_in=(dsk,), extra_specs=(pl.BlockSpec((1, d), lambda i: (0, 0)),))[0]

    @jax.custom_vjp
    def f(u, ys, dsk):
        return fwd_call(u, ys, dsk)

    def fwd(u, ys, dsk):
        return fwd_call(u, ys, dsk), (u, ys, dsk)

    def bwd(res, dg):
        u, ys, dsk = res
        r, d = u.shape
        tr = _row_tile(r, r, d)

        def body(u_ref, y_ref, dg_ref, d_ref, du_ref, dy_ref, dd_ref):
            i = pl.program_id(0)
            uu = u_ref[...].astype(F32)
            y = d_ref[...] * uu + y_ref[...]
            dy = dg_ref[...].astype(F32) * _gelu_and_grad(y)[1]
            dy_ref[...] = dy
            du_ref[...] = (d_ref[...] * dy).astype(du_ref.dtype)

            @pl.when(i == 0)
            def _():
                dd_ref[...] = jnp.zeros_like(dd_ref)

            dd_ref[...] += jnp.sum(dy * uu, axis=0, keepdims=True)

        outs = [jax.ShapeDtypeStruct((r, d), u.dtype), jax.ShapeDtypeStruct((r, d), F32), jax.ShapeDtypeStruct((1, d), F32)]
        du, dy, dd = _ew_call(name + "_bwd", body, (u, ys, dg), outs, r, (d, d, d), (d, d, None), tr,
                              extra_in=(dsk,), extra_specs=(pl.BlockSpec((1, d), lambda i: (0, 0)),), sem="arbitrary")
        return du, dy, dd

    f.defvjp(fwd, bwd)
    return f


def make_glu(name):
    def fwd_call(z, b):
        r, d2 = z.shape
        d = d2 // 2
        tr = _row_tile(r, r, d2)

        def body(z_ref, b_ref, o_ref):
            zz = z_ref[...].astype(F32) + b_ref[...]
            o_ref[...] = zz[:, :d] * jax.nn.sigmoid(zz[:, d:])

        return _ew_call(name + "_fwd", body, (z,), [jax.ShapeDtypeStruct((r, d), F32)], r, (d2,), (d,), tr,
                        extra_in=(b,), extra_specs=(pl.BlockSpec((1, d2), lambda i: (0, 0)),))[0]

    @jax.custom_vjp
    def f(z, b):
        return fwd_call(z, b)

    def fwd(z, b):
        return fwd_call(z, b), (z, b)

    def bwd(res, do):
        z, b = res
        r, d2 = z.shape
        d = d2 // 2
        tr = _row_tile(r, r, d2)

        def body(z_ref, do_ref, b_ref, dz_ref, db_ref):
            i = pl.program_id(0)
            zz = z_ref[...].astype(F32) + b_ref[...]
            sg = jax.nn.sigmoid(zz[:, d:])
            g = do_ref[...]
            dza = g * sg
            dzb = g * zz[:, :d] * sg * (1.0 - sg)
            dz_ref[:, :d] = dza.astype(dz_ref.dtype)
            dz_ref[:, d:] = dzb.astype(dz_ref.dtype)

            @pl.when(i == 0)
            def _():
                db_ref[...] = jnp.zeros_like(db_ref)

            db_ref[:, :d] += jnp.sum(dza, axis=0, keepdims=True)
            db_ref[:, d:] += jnp.sum(dzb, axis=0, keepdims=True)

        outs = [jax.ShapeDtypeStruct((r, d2), z.dtype), jax.ShapeDtypeStruct((1, d2), F32)]
        dz, db = _ew_call(name + "_bwd", body, (z, do), outs, r, (d2, d), (d2, None), tr,
                          extra_in=(b,), extra_specs=(pl.BlockSpec((1, d2), lambda i: (0, 0)),), sem="arbitrary")
        return dz, db

    f.defvjp(fwd, bwd)
    return f


def make_final_loss(name):
    def call(z, g, target):
        r, d = z.shape
        tr = _row_tile(r, r, d)

        def body(z_ref, t_ref, g_ref, dz_ref, dg_ref, l_ref):
            i = pl.program_id(0)
            zz = z_ref[...]
            gg = g_ref[...]
            rstd = lax.rsqrt(jnp.mean(zz * zz, axis=-1, keepdims=True) + EPS)
            zhat = zz * rstd
            e = zhat * gg - t_ref[...]
            dy = e * (1.0 / d)
            dyg = dy * gg
            dz_ref[...] = rstd * (dyg - zhat * jnp.mean(dyg * zhat, axis=-1, keepdims=True))

            @pl.when(i == 0)
            def _():
                dg_ref[...] = jnp.zeros_like(dg_ref)
                l_ref[...] = jnp.zeros_like(l_ref)

            dg_ref[...] += jnp.sum(dy * zhat, axis=0, keepdims=True)
            l_ref[...] += jnp.sum(jnp.sum(e * e, axis=1, keepdims=True), axis=0, keepdims=True) * (0.5 / d)

        outs = [jax.ShapeDtypeStruct((r, d), F32), jax.ShapeDtypeStruct((1, d), F32), jax.ShapeDtypeStruct((1, 1), F32)]
        return _ew_call(name, body, (z, target), outs, r, (d, d), (d, None, None), tr,
                        extra_in=(g,), extra_specs=(pl.BlockSpec((1, d), lambda i: (0, 0)),), sem="arbitrary")

    @jax.custom_vjp
    def f(z, g, target):
        return call(z, g, target)[2]

    def fwd(z, g, target):
        dz, dg, loss = call(z, g, target)
        return loss, (dz, dg)

    def bwd(res, dl):
        dz, dg = res
        s = dl[0, 0]
        return dz * s, dg * s, None

    f.defvjp(fwd, bwd)
    return f


def _rope_tables(t):
    quarter = HEAD_DIM // 4
    inv_freq = ROPE_BASE ** (-np.arange(quarter, dtype=np.float64) / quarter)
    pos = np.arange(t)
    ang_r = (pos // GRID_W)[:, None] * inv_freq[None, :]
    ang_c = (pos % GRID_W)[:, None] * inv_freq[None, :]
    cos = np.concatenate([np.cos(ang_r), np.cos(ang_r), np.cos(ang_c), np.cos(ang_c)], axis=1)
    sin = np.concatenate([-np.sin(ang_r), np.sin(ang_r), -np.sin(ang_c), np.sin(ang_c)], axis=1)
    return jnp.asarray(cos, F32), jnp.asarray(sin, F32)


def _rope_call(x, cos, sin, name):
    t, w = x.shape
    tr = _pick(t, (512, 256, 128, 64))
    quarter = HEAD_DIM // 4

    def body(x_ref, c_ref, s_ref, o_ref):
        xx = x_ref[...].astype(F32)
        lane = lax.broadcasted_iota(jnp.int32, xx.shape, 1)
        first = (lane % (2 * quarter)) < quarter
        partner = jnp.where(first, pltpu.roll(xx, HEAD_DIM - quarter, 1), pltpu.roll(xx, quarter, 1))
        o_ref[...] = (xx * c_ref[...] + partner * s_ref[...]).astype(o_ref.dtype)

    return pl.pallas_call(
        body, grid=(t // tr, w // HEAD_DIM),
        in_specs=[pl.BlockSpec((tr, HEAD_DIM), lambda i, j: (i, j)), pl.BlockSpec((tr, HEAD_DIM), lambda i, j: (i, 0)),
                  pl.BlockSpec((tr, HEAD_DIM), lambda i, j: (i, 0))],
        out_specs=pl.BlockSpec((tr, HEAD_DIM), lambda i, j: (i, j)),
        out_shape=jax.ShapeDtypeStruct((t, w), x.dtype), name=name, compiler_params=_cparams("parallel", "parallel"))(x, cos, sin)


def make_rope(t, name):
    cos, sin = _rope_tables(t)

    @jax.custom_vjp
    def f(x):
        return _rope_call(x, cos, sin, name + "_fwd")

    def fwd(x):
        return _rope_call(x, cos, sin, name + "_fwd"), None

    def bwd(_, dy):
        return (_rope_call(dy, cos, -sin, name + "_bwd"),)

    f.defvjp(fwd, bwd)
    return f


def _attn_specs(g, span, tk, m, nbh, has_ctx, hb=1):
    hd = HEAD_DIM
    q_spec = pl.BlockSpec((ATTN_BLOCK, hb * g * hd), lambda h, i, meta: (i, h))
    kv_spec = pl.BlockSpec((tk, hb * hd), lambda h, i, meta: (0, h))
    c_spec = pl.BlockSpec((m, hb * hd), lambda h, i, meta: (0, h))
    if nbh > 1 and hb > 1:
        b_spec = pl.BlockSpec((None, hb, ATTN_BLOCK, span), lambda h, i, meta: (meta[1, i], h, 0, 0))
    elif nbh > 1:
        b_spec = pl.BlockSpec((None, None, ATTN_BLOCK, span), lambda h, i, meta: (meta[1, i], h, 0, 0))
    else:
        b_spec = pl.BlockSpec((None, None, ATTN_BLOCK, span), lambda h, i, meta: (meta[1, i], 0, 0, 0))
    sink_spec = pl.BlockSpec(memory_space=pltpu.SMEM)
    return q_spec, kv_spec, c_spec, b_spec, sink_spec


def _attn_probs(qh, ks, kc, bias, sink_val, scale, has_ctx, has_sink):
    s = _dot_nt(qh, ks) * scale + bias
    mx = jnp.max(s, axis=-1, keepdims=True)
    sc = None
    if has_ctx:
        sc = _dot_nt(qh, kc) * scale
        mx = jnp.maximum(mx, jnp.max(sc, axis=-1, keepdims=True))
    if has_sink:
        mx = jnp.maximum(mx, sink_val)
    p = jnp.exp(s - mx)
    l = jnp.sum(p, axis=-1, keepdims=True)
    pc = None
    if has_ctx:
        pc = jnp.exp(sc - mx)
        l = l + jnp.sum(pc, axis=-1, keepdims=True)
    ps = None
    if has_sink:
        ps = jnp.exp(sink_val - mx)
        l = l + ps
    return p, pc, ps, l


def _attn_fwd(q, k, v, kc, vc, bias, sink, meta, g, span, has_ctx, has_sink, name):
    rq, wq = q.shape
    tk, wk = k.shape
    hkv = wk // HEAD_DIM
    m = kc.shape[0]
    nbh = bias.shape[1]
    scale = HEAD_DIM ** -0.5
    nqb = rq // ATTN_BLOCK
    hb = _pick(hkv, (4, 2, 1)) if g == 1 else 1
    q_spec, kv_spec, c_spec, b_spec, sink_spec = _attn_specs(g, span, tk, m, nbh, has_ctx, hb)
    hd = HEAD_DIM

    def body(meta_ref, sink_ref, q_ref, k_ref, v_ref, kc_ref, vc_ref, b_ref, o_ref):
        h = pl.program_id(0)
        i = pl.program_id(1)
        ks0 = pl.multiple_of(meta_ref[0, i], 64)
        for kl in range(hb):
            kcols = slice(kl * hd, (kl + 1) * hd)
            ks = k_ref[pl.ds(ks0, span), kcols]
            vs = v_ref[pl.ds(ks0, span), kcols]
            bias_t = b_ref[kl] if (hb > 1 and nbh > 1) else b_ref[...]
            for hh in range(g):
                cols = slice((kl * g + hh) * hd, (kl * g + hh + 1) * hd)
                sink_val = sink_ref[(h * hb + kl) * g + hh] if has_sink else None
                p, pc, _, l = _attn_probs(q_ref[:, cols], ks, kc_ref[:, kcols], bias_t, sink_val, scale, has_ctx, has_sink)
                acc = jnp.dot(p.astype(BF16), vs, preferred_element_type=F32)
                if has_ctx:
                    acc = acc + jnp.dot(pc.astype(BF16), vc_ref[:, kcols], preferred_element_type=F32)
                o_ref[:, cols] = (acc / l).astype(o_ref.dtype)

    gs = pltpu.PrefetchScalarGridSpec(
        num_scalar_prefetch=1, grid=(hkv // hb, nqb),
        in_specs=[sink_spec, q_spec, kv_spec, kv_spec, c_spec, c_spec, b_spec], out_specs=q_spec)
    return pl.pallas_call(body, grid_spec=gs, out_shape=jax.ShapeDtypeStruct((rq, wq), BF16), name=name,
                          compiler_params=_cparams("parallel", "arbitrary"))(meta, sink, q, k, v, kc, vc, bias)


def _attn_bwd(q, k, v, kc, vc, bias, sink, meta, o, do, g, span, has_ctx, has_sink, want_dbias, name):
    rq, wq = q.shape
    tk, wk = k.shape
    hkv = wk // HEAD_DIM
    m = kc.shape[0]
    ncase, nbh = bias.shape[:2]
    scale = HEAD_DIM ** -0.5
    nqb = rq // ATTN_BLOCK
    hb = _pick(hkv, (2, 1)) if g == 1 else 1
    q_spec, kv_spec, c_spec, b_spec, sink_spec = _attn_specs(g, span, tk, m, nbh, has_ctx, hb)
    dsink_spec = pl.BlockSpec((None, 8, HEAD_DIM), lambda h, i, meta: (h, 0, 0))
    hd = HEAD_DIM

    def body(meta_ref, sink_ref, q_ref, k_ref, v_ref, kc_ref, vc_ref, b_ref, o_ref, do_ref,
             dq_ref, dk_ref, dv_ref, dkc_ref, dvc_ref, db_ref, dsk_ref):
        h = pl.program_id(0)
        i = pl.program_id(1)

        @pl.when(i == 0)
        def _():
            dk_ref[...] = jnp.zeros_like(dk_ref)
            dv_ref[...] = jnp.zeros_like(dv_ref)
            dkc_ref[...] = jnp.zeros_like(dkc_ref)
            dvc_ref[...] = jnp.zeros_like(dvc_ref)
            dsk_ref[...] = jnp.zeros_like(dsk_ref)

        if want_dbias:
            @pl.when(meta_ref[2, i] == 1)
            def _():
                db_ref[...] = jnp.zeros_like(db_ref)
        else:
            @pl.when(i == 0)
            def _():
                db_ref[...] = jnp.zeros_like(db_ref)

        ks0 = pl.multiple_of(meta_ref[0, i], 64)
        for kl in range(hb):
            kcols = slice(kl * hd, (kl + 1) * hd)
            ks = k_ref[pl.ds(ks0, span), kcols]
            vs = v_ref[pl.ds(ks0, span), kcols]
            per_head_bias = hb > 1 and nbh > 1
            bias_t = b_ref[kl] if per_head_bias else b_ref[...]
            dk_acc = jnp.zeros((span, hd), F32)
            dv_acc = jnp.zeros((span, hd), F32)
            for hh in range(g):
                cols = slice((kl * g + hh) * hd, (kl * g + hh + 1) * hd)
                qh = q_ref[:, cols]
                doh = do_ref[:, cols]
                sink_val = sink_ref[(h * hb + kl) * g + hh] if has_sink else None
                p, pc, ps, l = _attn_probs(qh, ks, kc_ref[:, kcols], bias_t, sink_val, scale, has_ctx, has_sink)
                inv_l = 1.0 / l
                delta = jnp.sum(doh.astype(F32) * o_ref[:, cols].astype(F32), axis=-1, keepdims=True)
                pn = p * inv_l
                ds = pn * (_dot_nt(doh, vs) - delta)
                dsb = ds.astype(BF16)
                dq = jnp.dot(dsb, ks, preferred_element_type=F32)
                dk_acc = dk_acc + _dot_tn(dsb, qh)
                dv_acc = dv_acc + _dot_tn(pn.astype(BF16), doh)
                if want_dbias and per_head_bias:
                    db_ref[kl] += ds
                elif want_dbias:
                    db_ref[...] += ds
                if has_ctx:
                    pcn = pc * inv_l
                    dsc = (pcn * (_dot_nt(doh, vc_ref[:, kcols]) - delta)).astype(BF16)
                    dq = dq + jnp.dot(dsc, kc_ref[:, kcols], preferred_element_type=F32)
                    dkc_ref[:, kcols] += _dot_tn(dsc, qh) * scale
                    dvc_ref[:, kcols] += _dot_tn(pcn.astype(BF16), doh)
                if has_sink:
                    dsv = -jnp.sum(ps * inv_l * delta, axis=0, keepdims=True)
                    dsk_ref[kl * g + hh:kl * g + hh + 1, :] += jnp.broadcast_to(dsv, (1, hd))
                dq_ref[:, cols] = (dq * scale).astype(dq_ref.dtype)
            dk_ref[pl.ds(ks0, span), kcols] += dk_acc * scale
            dv_ref[pl.ds(ks0, span), kcols] += dv_acc

    gs = pltpu.PrefetchScalarGridSpec(
        num_scalar_prefetch=1, grid=(hkv // hb, nqb),
        in_specs=[sink_spec, q_spec, kv_spec, kv_spec, c_spec, c_spec, b_spec, q_spec, q_spec],
        out_specs=[q_spec, kv_spec, kv_spec, c_spec, c_spec, b_spec if want_dbias else dsink_spec, dsink_spec])
    db_sds = jax.ShapeDtypeStruct((ncase, nbh, ATTN_BLOCK, span) if want_dbias else (hkv, 8, HEAD_DIM), F32)
    out_shape = [jax.ShapeDtypeStruct((rq, wq), BF16), jax.ShapeDtypeStruct((tk, wk), F32), jax.ShapeDtypeStruct((tk, wk), F32),
                 jax.ShapeDtypeStruct((m, wk), F32), jax.ShapeDtypeStruct((m, wk), F32), db_sds,
                 jax.ShapeDtypeStruct((hkv, 8, HEAD_DIM), F32)]
    return pl.pallas_call(body, grid_spec=gs, out_shape=out_shape, name=name,
                          compiler_params=_cparams("parallel", "arbitrary"))(meta, sink, q, k, v, kc, vc, bias, o, do)


def make_attention(meta_np, g, span, has_ctx, has_sink, want_dbias, name):
    meta = jnp.asarray(meta_np, jnp.int32)

    @jax.custom_vjp
    def f(q, k, v, kc, vc, bias, sink):
        return _attn_fwd(q, k, v, kc, vc, bias, sink, meta, g, span, has_ctx, has_sink, name + "_fwd")

    def fwd(q, k, v, kc, vc, bias, sink):
        o = _attn_fwd(q, k, v, kc, vc, bias, sink, meta, g, span, has_ctx, has_sink, name + "_fwd")
        return o, (q, k, v, kc, vc, bias, sink, o)

    def bwd(res, do):
        q, k, v, kc, vc, bias, sink, o = res
        dq, dk, dv, dkc, dvc, db, dsk = _attn_bwd(q, k, v, kc, vc, bias, sink, meta, o, do.astype(BF16), g, span,
                                                   has_ctx, has_sink, want_dbias, name + "_bwd")
        dsink = dsk[:, :g, 0].reshape(sink.shape) if has_sink else jnp.zeros_like(sink)
        if not want_dbias:
            db = jnp.zeros_like(bias)
        return dq, dk.astype(k.dtype), dv.astype(v.dtype), dkc.astype(kc.dtype), dvc.astype(vc.dtype), db, dsink

    f.defvjp(fwd, bwd)
    return f


def _dedupe_cases(tables):
    cases, idx, first = [], [], []
    for tbl in tables:
        if cases and np.array_equal(cases[-1], tbl):
            idx.append(len(cases) - 1)
            first.append(0)
        else:
            cases.append(tbl)
            idx.append(len(cases) - 1)
            first.append(1)
    return cases, idx, first


def _na_plan(t):
    rows = t // GRID_W
    qr = ATTN_BLOCK // GRID_W
    kr = qr + NA_ROWS - 1
    assert rows >= kr and rows % qr == 0
    span = kr * GRID_W
    kstart, tables = [], []
    qcol = np.tile(np.arange(GRID_W), qr)
    kcol = np.tile(np.arange(GRID_W), kr)
    win_c = np.clip(qcol - NA_COLS // 2, 0, GRID_W - NA_COLS)
    col_ok = (kcol[None, :] >= win_c[:, None]) & (kcol[None, :] < win_c[:, None] + NA_COLS)
    dcol = np.clip(kcol[None, :] - qcol[:, None] + NA_COLS - 1, 0, 2 * NA_COLS - 2)
    for r0 in range(0, rows, qr):
        kb = int(np.clip(r0 - NA_ROWS // 2, 0, rows - kr))
        qrow = r0 + np.repeat(np.arange(qr), GRID_W)
        krow = kb + np.repeat(np.arange(kr), GRID_W)
        win_r = np.clip(qrow - NA_ROWS // 2, 0, rows - NA_ROWS)
        row_ok = (krow[None, :] >= win_r[:, None]) & (krow[None, :] < win_r[:, None] + NA_ROWS)
        drow = np.clip(krow[None, :] - qrow[:, None] + NA_ROWS - 1, 0, 2 * NA_ROWS - 2)
        tables.append(np.stack([row_ok & col_ok, drow, dcol]).astype(np.int32))
        kstart.append(kb * GRID_W)
    cases, idx, first = _dedupe_cases(tables)
    meta = np.array([kstart, idx, first], np.int32)
    return meta, span, np.stack(cases)


def _na_bias(rpb, cases):
    valid, drow, dcol = cases[:, 0], cases[:, 1], cases[:, 2]
    ncase, qn, span = valid.shape
    qr, kr = qn // GRID_W, span // GRID_W
    drow_s = drow.reshape(ncase, qr, GRID_W, kr, GRID_W)[:, :, 0, :, 0]
    dcol_s = dcol[0].reshape(qr, GRID_W, kr, GRID_W)[0, :, 0, :]
    oh_r = jnp.asarray(np.eye(2 * NA_ROWS - 1, dtype=np.float32)[drow_s])
    oh_c = jnp.asarray(np.eye(2 * NA_COLS - 1, dtype=np.float32)[dcol_s])
    tmp = jnp.einsum("hrc,xyc->hrxy", rpb, oh_c, precision=lax.Precision.HIGHEST)
    b = jnp.einsum("nakr,hrxy->nhaxky", oh_r, tmp, precision=lax.Precision.HIGHEST).reshape(ncase, -1, qn, span)
    return jnp.where(jnp.asarray(valid[:, None] > 0), b, NEG_INF)


def _sw_plan(t):
    span = 3 * ATTN_BLOCK
    assert t >= span
    kstart, tables = [], []
    for b in range(t // ATTN_BLOCK):
        ks = int(np.clip((b - 1) * ATTN_BLOCK, 0, t - span))
        qpos = b * ATTN_BLOCK + np.arange(ATTN_BLOCK)
        kpos = ks + np.arange(span)
        ok = np.abs(kpos[None, :] - qpos[:, None]) <= SW_RADIUS
        tables.append(np.where(ok, 0.0, NEG_INF).astype(np.float32))
        kstart.append(ks)
    cases, idx, first = _dedupe_cases(tables)
    return np.array([kstart, idx, first], np.int32), span, np.stack(cases)[:, None]


def _cmul(ar, ai, br, bi):
    return ar * br - ai * bi, ar * bi + ai * br


def _s5_scan_call(x2, win, lam, cin, wout, reverse, n_chunks, name):
    _, ll, d = x2.shape
    nt = d // HEAD_DIM
    sw = 2 * SSM_TILE_GROUPS * SSM_STATE
    hw = sw // 2
    rows = ll // n_chunks
    full = cin is not None

    down_dir = 0 if reverse else 1

    def chunk_idx(k, dd):
        return jnp.where(dd == down_dir, n_chunks - 1 - k, k)

    n_sub = 3 if (not full and rows % (3 * 16) == 0) else 1
    sub = rows // n_sub
    ics = sub // SCAN_BLOCKS

    def body(*refs):
        if full:
            x_ref, win_ref, lam_ref, cin_ref, wout_ref, s_out, y_out = refs[:7]
        else:
            x_ref, win_ref, lam_ref, f_out = refs[:4]
        ubs, st_ref = refs[-1 - n_sub:-1], refs[-1]
        k = pl.program_id(2)
        down = pl.program_id(0) == down_dir

        @pl.when(k == 0)
        def _():
            st_ref[...] = cin_ref[...] if full else jnp.zeros_like(st_ref)

        def sub_rows(p):
            return pl.ds(pl.multiple_of(jnp.where(down, n_sub - 1 - p, p) * sub, 16), sub)

        def drive(p):
            ubs[p][...] = jnp.dot(x_ref[sub_rows(p), :].astype(BF16), win_ref[...], preferred_element_type=F32)

        lr = lam_ref[:, :hw]
        li = lam_ref[:, hw:]
        if full:
            ub = ubs[0]
            drive(0)

            def step(ii, carry):
                sr, si = carry
                i = jnp.where(down, ics - 1 - ii, ii)
                r0 = pl.multiple_of(i * SCAN_BLOCKS, SCAN_BLOCKS)
                nr = lr * sr - li * si + ub[pl.ds(r0, SCAN_BLOCKS), :hw]
                ni = lr * si + li * sr + ub[pl.ds(r0, SCAN_BLOCKS), hw:]
                ub[pl.ds(r0, SCAN_BLOCKS), :hw] = nr
                ub[pl.ds(r0, SCAN_BLOCKS), hw:] = ni
                return nr, ni

            sr, si = lax.fori_loop(0, ics, step, (st_ref[:, :hw], st_ref[:, hw:]), unroll=4 if ics % 4 == 0 else 1)
            sb = ub[...].astype(BF16)
            s_out[...] = sb
            y_out[...] = jnp.dot(sb, wout_ref[...], preferred_element_type=F32)
        else:
            sr, si = st_ref[:, :hw], st_ref[:, hw:]
            drive(0)
            for p in range(n_sub):
                if p + 1 < n_sub:
                    drive(p + 1)
                for ii in range(ics):
                    r0 = pl.multiple_of(jnp.where(down, (ics - 1 - ii) * SCAN_BLOCKS, ii * SCAN_BLOCKS), SCAN_BLOCKS)
                    ur = ubs[p][pl.ds(r0, SCAN_BLOCKS), :hw]
                    ui = ubs[p][pl.ds(r0, SCAN_BLOCKS), hw:]
                    sr, si = lr * sr - li * si + ur, lr * si + li * sr + ui
        st_ref[:, :hw] = sr
        st_ref[:, hw:] = si
        if not full:
            @pl.when(k == n_chunks - 1)
            def _():
                f_out[...] = st_ref[...]

    x_spec = pl.BlockSpec((None, rows, HEAD_DIM), lambda dd, t, k: (dd, chunk_idx(k, dd), t))
    win_spec = pl.BlockSpec((None, None, HEAD_DIM, sw), lambda dd, t, k: (dd, t, 0, 0))
    vec_spec = pl.BlockSpec((None, None, SCAN_BLOCKS, sw), lambda dd, t, k: (dd, t, 0, 0))
    scratch = [pltpu.VMEM((sub, sw), F32) for _ in range(n_sub)] + [pltpu.VMEM((SCAN_BLOCKS, sw), F32)]
    if full:
        in_specs = [x_spec, win_spec, vec_spec, vec_spec, pl.BlockSpec((None, None, sw, HEAD_DIM), lambda dd, t, k: (dd, t, 0, 0))]
        out_specs = [pl.BlockSpec((None, None, rows, sw), lambda dd, t, k: (dd, t, chunk_idx(k, dd), 0)), x_spec]
        out_shape = [jax.ShapeDtypeStruct((2, nt, ll, sw), BF16), jax.ShapeDtypeStruct((2, ll, d), F32)]
        args = (x2, win, lam, cin, wout)
    else:
        in_specs = [x_spec, win_spec, vec_spec]
        out_specs = vec_spec
        out_shape = jax.ShapeDtypeStruct((2, nt, SCAN_BLOCKS, sw), F32)
        args = (x2, win, lam)
    return pl.pallas_call(body, grid=(2, nt, n_chunks), in_specs=in_specs, out_specs=out_specs, out_shape=out_shape,
                          scratch_shapes=scratch, name=name,
                          compiler_params=_cparams("parallel", "parallel", "arbitrary"))(*args)


def _s5_bwd_call(dy2, wrt, lamc, cin, st, u2, wdt, n_chunks, name):
    _, ll, d = dy2.shape
    nt = d // HEAD_DIM
    sw = 2 * SSM_TILE_GROUPS * SSM_STATE
    hw = sw // 2
    rows = ll // n_chunks

    def chunk_idx(k, dd):
        return jnp.where(dd == 0, n_chunks - 1 - k, k)

    n_sub = 3 if rows % (3 * 16) == 0 else 1
    sub = rows // n_sub
    ics = sub // SCAN_BLOCKS

    def body(dy_ref, wrt_ref, lam_ref, cin_ref, stb_ref, u_ref, wdt_ref, du_out, dwd_out, dwr_out, dlam_out, *scratch):
        dss, sts, a_ref = scratch[:n_sub], scratch[n_sub:2 * n_sub], scratch[-1]
        k = pl.program_id(2)
        down = pl.program_id(0) == 0

        @pl.when(k == 0)
        def _():
            a_ref[...] = cin_ref[...]
            dwd_out[...] = jnp.zeros_like(dwd_out)
            dwr_out[...] = jnp.zeros_like(dwr_out)
            dlam_out[...] = jnp.zeros_like(dlam_out)

        def sub_rows(p):
            return pl.ds(pl.multiple_of(jnp.where(down, n_sub - 1 - p, p) * sub, 16), sub)

        def prepare(p):
            dss[p][...] = jnp.dot(dy_ref[sub_rows(p), :].astype(BF16), wrt_ref[...], preferred_element_type=F32)
            sts[p][...] = stb_ref[sub_rows(p), :].astype(F32)

        def finish(p):
            ab = dss[p][...].astype(BF16)
            du_out[sub_rows(p), :] = jnp.dot(ab, wdt_ref[...], preferred_element_type=F32).astype(du_out.dtype)
            dwd_out[...] += _dot_tn(u_ref[sub_rows(p), :].astype(BF16), ab)
            dwr_out[...] += _dot_tn(stb_ref[sub_rows(p), :], dy_ref[sub_rows(p), :].astype(BF16))

        lr = lam_ref[:, :hw]
        li = lam_ref[:, hw:]
        ar, ai, gr, gi = a_ref[:, :hw], a_ref[:, hw:], dlam_out[:, :hw], dlam_out[:, hw:]
        prepare(0)
        for p in range(n_sub):
            if p + 1 < n_sub:
                prepare(p + 1)
            for ii in range(ics):
                r0 = pl.multiple_of(jnp.where(down, (ics - 1 - ii) * SCAN_BLOCKS, ii * SCAN_BLOCKS), SCAN_BLOCKS)
                sr = sts[p][pl.ds(r0, SCAN_BLOCKS), :hw]
                si = sts[p][pl.ds(r0, SCAN_BLOCKS), hw:]
                gr = gr + ar * sr + ai * si
                gi = gi + ai * sr - ar * si
                ar, ai = (lr * ar - li * ai + dss[p][pl.ds(r0, SCAN_BLOCKS), :hw],
                          lr * ai + li * ar + dss[p][pl.ds(r0, SCAN_BLOCKS), hw:])
                dss[p][pl.ds(r0, SCAN_BLOCKS), :hw] = ar
                dss[p][pl.ds(r0, SCAN_BLOCKS), hw:] = ai
            finish(p)
        a_ref[:, :hw] = ar
        a_ref[:, hw:] = ai
        dlam_out[:, :hw] = gr
        dlam_out[:, hw:] = gi

    x_spec = pl.BlockSpec((None, rows, HEAD_DIM), lambda dd, t, k: (dd, chunk_idx(k, dd), t))
    w_in = pl.BlockSpec((None, None, HEAD_DIM, sw), lambda dd, t, k: (dd, t, 0, 0))
    w_out = pl.BlockSpec((None, None, sw, HEAD_DIM), lambda dd, t, k: (dd, t, 0, 0))
    vec_spec = pl.BlockSpec((None, None, SCAN_BLOCKS, sw), lambda dd, t, k: (dd, t, 0, 0))
    st_spec = pl.BlockSpec((None, None, rows, sw), lambda dd, t, k: (dd, t, chunk_idx(k, dd), 0))
    out_shape = [jax.ShapeDtypeStruct((2, ll, d), u2.dtype), jax.ShapeDtypeStruct((2, nt, HEAD_DIM, sw), F32),
                 jax.ShapeDtypeStruct((2, nt, sw, HEAD_DIM), F32), jax.ShapeDtypeStruct((2, nt, SCAN_BLOCKS, sw), F32)]
    return pl.pallas_call(
        body, grid=(2, nt, n_chunks),
        in_specs=[x_spec, w_in, vec_spec, vec_spec, st_spec, x_spec, w_out],
        out_specs=[x_spec, w_in, w_out, vec_spec], out_shape=out_shape,
        scratch_shapes=[pltpu.VMEM((sub, sw), F32) for _ in range(2 * n_sub)] + [pltpu.VMEM((SCAN_BLOCKS, sw), F32)],
        name=name, compiler_params=_cparams("parallel", "parallel", "arbitrary"))(dy2, wrt, lamc, cin, st, u2, wdt)


def _cpow(lr, li, n):
    rr, ri = jnp.ones_like(lr), jnp.zeros_like(li)
    br, bi = lr, li
    while n:
        if n & 1:
            rr, ri = _cmul(rr, ri, br, bi)
        br, bi = _cmul(br, bi, br, bi)
        n >>= 1
    return rr, ri


def _resolve_carries(finals, lam, block_len, down_dir):
    hw = finals.shape[-1] // 2
    pr, pi = _cpow(lam[:, :, 0, :hw], lam[:, :, 0, hw:], block_len)
    fr, fi = finals[..., :hw], finals[..., hw:]

    def walk(order):
        cr, ci = jnp.zeros_like(pr), jnp.zeros_like(pi)
        out = [None] * SCAN_BLOCKS
        for j in order:
            out[j] = jnp.concatenate([cr, ci], axis=-1)
            mr, mi = _cmul(pr, pi, cr, ci)
            cr, ci = mr + fr[:, :, j], mi + fi[:, :, j]
        return jnp.stack(out, axis=2)

    up, down = walk(range(SCAN_BLOCKS)), walk(range(SCAN_BLOCKS - 1, -1, -1))
    return jnp.stack([down[0], up[1]] if down_dir == 0 else [up[0], down[1]])


def _scan_chunks(ll):
    block_len = ll // SCAN_BLOCKS
    for ic in (132, 128, 96, 64, 48, 36, 32, 24, 16, 8):
        if block_len % ic == 0:
            return block_len // ic
    return 1


def make_s5_core(name):
    def run_fwd(u2, lam, wd, wr):
        ll = u2.shape[1]
        nc = _scan_chunks(ll)
        lam8 = jnp.broadcast_to(lam[:, :, None, :], lam.shape[:2] + (SCAN_BLOCKS, lam.shape[-1]))
        wdb = wd.astype(BF16)
        finals = _s5_scan_call(u2, wdb, lam8, None, None, False, nc, name + "_carry")
        cin = _resolve_carries(finals, lam8, ll // SCAN_BLOCKS, 1)
        st, y2 = _s5_scan_call(u2, wdb, lam8, cin, wr.astype(BF16), False, nc, name + "_scan")
        return y2, st, lam8

    @jax.custom_vjp
    def f(u2, lam, wd, wr):
        return run_fwd(u2, lam, wd, wr)[0]

    def fwd(u2, lam, wd, wr):
        y2, st, lam8 = run_fwd(u2, lam, wd, wr)
        return y2, (u2, lam8, wd, wr, st)

    def bwd(res, dy2):
        u2, lam8, wd, wr, st = res
        ll = u2.shape[1]
        nc = _scan_chunks(ll)
        hw = lam8.shape[-1] // 2
        lamc = jnp.concatenate([lam8[..., :hw], -lam8[..., hw:]], axis=-1)
        wrt = jnp.swapaxes(wr, 2, 3).astype(BF16)
        wdt = jnp.swapaxes(wd, 2, 3).astype(BF16)
        finals = _s5_scan_call(dy2, wrt, lamc, None, None, True, nc, name + "_bcarry")
        cin = _resolve_carries(finals, lamc, ll // SCAN_BLOCKS, 0)
        du2, dwd, dwr, dlam8 = _s5_bwd_call(dy2, wrt, lamc, cin, st, u2, wdt, nc, name + "_bscan")
        return du2, jnp.sum(dlam8, axis=2), dwd, dwr

    f.defvjp(fwd, bwd)
    return f


def _s5_params(a_re, a_im, log_dt, b_re, b_im, c_re, c_im):
    dt = jnp.exp(log_dt)[..., None]
    mag = jnp.exp(a_re * dt)
    lam_r, lam_i = mag * jnp.cos(a_im * dt), mag * jnp.sin(a_im * dt)
    den = a_re * a_re + a_im * a_im
    nr = lam_r - 1.0
    coef_r = (nr * a_re + lam_i * a_im) / den
    coef_i = (lam_i * a_re - nr * a_im) / den
    bbar_r = coef_r[..., None] * b_re - coef_i[..., None] * b_im
    bbar_i = coef_r[..., None] * b_im + coef_i[..., None] * b_re
    ndir, g, p = lam_r.shape
    tg = SSM_TILE_GROUPS
    nt = g // tg
    eye = jnp.eye(tg, dtype=F32)

    def tile_vec(v):
        return v.reshape(ndir, nt, tg * p)

    lam = jnp.concatenate([tile_vec(lam_r), tile_vec(lam_i)], axis=-1)

    def drive(b):
        bt = b.reshape(ndir, nt, tg, p, SSM_GROUP)
        return (jnp.swapaxes(bt, 3, 4)[:, :, :, :, None, :] * eye[None, None, :, None, :, None]).reshape(ndir, nt, tg * SSM_GROUP, tg * p)

    wd = jnp.concatenate([drive(bbar_r), drive(bbar_i)], axis=-1)

    def readout(c):
        ct = c.reshape(ndir, nt, tg, SSM_GROUP, p)
        return (jnp.swapaxes(ct, 3, 4)[:, :, :, :, None, :] * eye[None, None, :, None, :, None]).reshape(ndir, nt, tg * p, tg * SSM_GROUP)

    wr = jnp.concatenate([readout(c_re), -readout(c_im)], axis=2)
    return lam, wd, wr


def _to_scan_order(seq):
    ll, d = seq.shape
    return seq.reshape(SCAN_BLOCKS, ll // SCAN_BLOCKS, d).swapaxes(0, 1).reshape(ll, d)


def _from_scan_order(y2):
    ll, d = y2.shape
    return y2.reshape(ll // SCAN_BLOCKS, SCAN_BLOCKS, d).swapaxes(0, 1).reshape(ll, d)


def adamw(w, g, m, v, name):
    shape = w.shape
    cols = shape[-1] if len(shape) > 1 else shape[0]
    w2, g2, m2, v2 = (a.reshape(-1, cols) for a in (w, g, m, v))
    r = w2.shape[0]
    cap = max(1, (1024 * 1024) // (4 * cols))
    tr = r
    for t in (512, 256, 128, 64, 32, 16, 8):
        if t <= cap and r % t == 0:
            tr = t
            break
    c1 = 1.0 / (1.0 - ADAM_B1 ** ADAM_STEP)
    c2 = 1.0 / (1.0 - ADAM_B2 ** ADAM_STEP)

    def body(w_ref, g_ref, m_ref, v_ref, d_ref, mo_ref, vo_ref):
        gg = g_ref[...]
        mn = ADAM_B1 * m_ref[...] + (1.0 - ADAM_B1) * gg
        vn = ADAM_B2 * v_ref[...] + (1.0 - ADAM_B2) * (gg * gg)
        d_ref[...] = -ADAM_LR * ((mn * c1) / (jnp.sqrt(vn * c2) + ADAM_EPS) + ADAM_WD * w_ref[...])
        mo_ref[...] = mn
        vo_ref[...] = vn

    spec = pl.BlockSpec((tr, cols), lambda i: (i, 0))
    sds = jax.ShapeDtypeStruct((r, cols), F32)
    d, mn, vn = pl.pallas_call(body, grid=(r // tr,), in_specs=[spec] * 4, out_specs=[spec] * 3, out_shape=[sds] * 3,
                               name=name, compiler_params=_cparams("parallel"))(w2, g2, m2, v2)
    return d.reshape(shape), mn.reshape(shape), vn.reshape(shape)


def _my_pos():
    return lax.axis_index("x"), lax.axis_index("y"), lax.axis_index("c")


def _flip(pos, f):
    return tuple((1 - p) if b else p for p, b in zip(pos, f))


def _lin(pos):
    return 4 * pos[0] + 2 * pos[1] + pos[2]


def _remote_copies(src_ref, out_ref, send_sems, recv_sems, plan):
    me = _my_pos()
    copies = []
    for k, (f, sfn, dfn) in enumerate(plan):
        peer = _flip(me, f)
        copies.append(pltpu.make_async_remote_copy(
            src_ref=src_ref.at[sfn(me, peer)], dst_ref=out_ref.at[dfn(me, peer)], send_sem=send_sems.at[k],
            recv_sem=recv_sems.at[k], device_id=peer, device_id_type=MESH))
    return copies


def xchg(src, n_out, plan, name, inplace=False):
    piece = src.shape[1:]

    def body(src_ref, out_ref, send_sems, recv_sems):
        me = _my_pos()
        copies = []
        for k, (f, sfn, dfn) in enumerate(plan):
            peer = _flip(me, f)
            s_ref = (out_ref if inplace else src_ref).at[sfn(me, peer)]
            d_ref = out_ref.at[dfn(me, peer)]
            if any(f):
                cp = pltpu.make_async_remote_copy(src_ref=s_ref, dst_ref=d_ref, send_sem=send_sems.at[k],
                                                  recv_sem=recv_sems.at[k], device_id=peer, device_id_type=MESH)
            else:
                cp = pltpu.make_async_copy(s_ref, d_ref, recv_sems.at[k])
            cp.start()
            copies.append((cp, any(f)))
        for cp, remote in copies:
            if remote:
                cp.wait_recv()
            else:
                cp.wait()
        for cp, remote in copies:
            if remote:
                cp.wait_send()

    return pl.pallas_call(
        body, in_specs=[pl.BlockSpec(memory_space=pl.ANY)], out_specs=pl.BlockSpec(memory_space=pl.ANY),
        out_shape=jax.ShapeDtypeStruct((n_out,) + piece, src.dtype),
        scratch_shapes=[pltpu.SemaphoreType.DMA((len(plan),)), pltpu.SemaphoreType.DMA((len(plan),))],
        input_output_aliases={0: 0} if inplace else {}, name=name)(src)


_CHIP_FLIPS = ((1, 0, 0), (0, 1, 0), (1, 1, 0))
_ALL_FLIPS = tuple((a, b, c) for a in (0, 1) for b in (0, 1) for c in (0, 1))[1:]


def all_to_all8(src, name):
    plan = [((0, 0, 0), lambda me, peer: _lin(me), lambda me, peer: _lin(me))]
    plan += [(f, lambda me, peer: _lin(peer), lambda me, peer: _lin(me)) for f in _ALL_FLIPS]
    return xchg(src, N_DEV, plan, name)


def all_gather8(piece, name):
    plan = [((0, 0, 0), lambda me, peer: 0, lambda me, peer: _lin(me))]
    plan += [(f, lambda me, peer: 0, lambda me, peer: _lin(me)) for f in _ALL_FLIPS]
    return xchg(piece[None], N_DEV, plan, name)


def _ag_prepare(shard):
    k, ns = shard.shape
    px, py, _ = _my_pos()
    own = shard.astype(BF16)[None]
    return lax.dynamic_update_slice(jnp.zeros((4, k, ns), BF16), own, (2 * px + py, 0, 0)).reshape(8, k // 2, ns)


def _ag_plan():
    return [(f, lambda me, peer: _lin(me), lambda me, peer: _lin(me)) for f in _CHIP_FLIPS]


def _ag_finish(buf, name):
    plan = [((0, 0, 1), lambda me, peer, f=f: _lin(_flip(me, f)), lambda me, peer, f=f: _lin(_flip(me, f)))
            for f in _CHIP_FLIPS]
    _, kh, ns = buf.shape
    return xchg(buf, 8, plan, name, inplace=True).reshape(4, 2 * kh, ns)


def gather_weight(shard, name):
    k, ns = shard.shape
    buf = _ag_prepare(shard)

    def body(in_ref, out_ref, send_sems, recv_sems):
        me = _my_pos()
        sibling = _flip(me, (0, 0, 1))
        chips = [_flip(me, f) for f in _CHIP_FLIPS]

        def copy(sem, holder, to):
            rows = out_ref.at[4 * holder[0] + 2 * holder[1] + me[2]]
            return pltpu.make_async_remote_copy(src_ref=rows, dst_ref=rows, send_sem=send_sems.at[sem],
                                                recv_sem=recv_sems.at[sem], device_id=to, device_id_type=MESH)

        first = [copy(j, me, chip) for j, chip in enumerate(chips)]
        for cp in first:
            cp.start()
        passed = [copy(3 + j, chip, sibling) for j, chip in enumerate(chips)]
        for j, chip in enumerate(chips):
            copy(j, chip, me).wait_recv()
            passed[j].start()
        for j in range(3):
            passed[j].wait_recv()
        for cp in first + passed:
            cp.wait_send()

    full = pl.pallas_call(
        body, in_specs=[pl.BlockSpec(memory_space=pl.ANY)], out_specs=pl.BlockSpec(memory_space=pl.ANY),
        out_shape=jax.ShapeDtypeStruct(buf.shape, BF16),
        scratch_shapes=[pltpu.SemaphoreType.DMA((6,)), pltpu.SemaphoreType.DMA((6,))],
        input_output_aliases={0: 0}, name=name)(buf)
    return full.reshape(4, k, ns)


_RS_SLOTS = 7


def _sum_halves(g8, l1, c_idx, name):
    _, _, r, cc = g8.shape
    tr = _row_tile(r, r, cc)

    def body(c_ref, a_ref, b_ref, o_ref):
        o_ref[...] = (a_ref[...].astype(F32) + b_ref[...].astype(F32)).astype(o_ref.dtype)

    gs = pltpu.PrefetchScalarGridSpec(
        num_scalar_prefetch=1, grid=(4, r // tr),
        in_specs=[pl.BlockSpec((None, None, tr, cc), lambda s, i, c: (s, c[0], i, 0)),
                  pl.BlockSpec((None, tr, cc), lambda s, i, c: (s, i, 0))],
        out_specs=pl.BlockSpec((None, tr, cc), lambda s, i, c: (s, i, 0)))
    return pl.pallas_call(body, grid_spec=gs, out_shape=jax.ShapeDtypeStruct((_RS_SLOTS, r, cc), BF16), name=name,
                          compiler_params=_cparams("parallel", "parallel"))(c_idx, g8, l1)


def _sum_chips(buf, sc_idx, name):
    _, r, cc = buf.shape
    tr = _row_tile(r, r, cc)

    def body(s_ref, a_ref, b0_ref, b1_ref, b2_ref, o_ref):
        o_ref[...] = ((a_ref[...].astype(F32) + b0_ref[...].astype(F32)) + b1_ref[...].astype(F32)) + b2_ref[...].astype(F32)

    gs = pltpu.PrefetchScalarGridSpec(
        num_scalar_prefetch=1, grid=(r // tr,),
        in_specs=[pl.BlockSpec((None, tr, cc), lambda i, s: (s[0], i, 0))]
        + [pl.BlockSpec((None, tr, cc), lambda i, s, j=j: (4 + j, i, 0)) for j in range(3)],
        out_specs=pl.BlockSpec((None, tr, cc), lambda i, s: (s[1], i, 0)))
    return pl.pallas_call(body, grid_spec=gs, out_shape=jax.ShapeDtypeStruct((2, r, cc), F32), name=name,
                          compiler_params=_cparams("parallel"))(sc_idx, buf, buf, buf, buf)


def _rs_begin(g4, name):
    _, k, ns = g4.shape
    c_idx = jnp.reshape(_my_pos()[2], (1,)).astype(jnp.int32)
    plan1 = [((0, 0, 1), lambda me, peer, s=s: 2 * s + peer[2], lambda me, peer, s=s: s) for s in range(4)]
    l1 = xchg(g4.reshape(8, k // 2, ns), 4, plan1, name + "_rs_d2d")
    buf = _sum_halves(g4.reshape(4, 2, k // 2, ns), l1, c_idx, name + "_rs_sum2")
    plan2 = [(f, lambda me, peer: 2 * peer[0] + peer[1], lambda me, peer, j=j: 4 + j) for j, f in enumerate(_CHIP_FLIPS)]
    return buf, plan2


def _rs_finish(buf, name):
    _, kh, ns = buf.shape
    x, y, c = _my_pos()
    sc_idx = jnp.stack([2 * x + y, c]).astype(jnp.int32)
    halves = _sum_chips(buf, sc_idx, name + "_rs_sum4")
    plan3 = [((0, 0, 1), lambda me, peer: me[2], lambda me, peer: me[2])]
    return xchg(halves, 2, plan3, name + "_rs_swap", inplace=True).reshape(2 * kh, ns)


def reduce_scatter_weight(g4, name):
    buf, plan = _rs_begin(g4, name)
    return _rs_finish(xchg(buf, _RS_SLOTS, plan, name + "_rs_ici", inplace=True), name)


def _sum8(a8, name):
    _, r, cc = a8.shape
    tr = _row_tile(r, r, cc)

    def body(a_ref, o_ref):
        acc = a_ref[0]
        for j in range(1, N_DEV):
            acc = acc + a_ref[j]
        o_ref[...] = acc

    return pl.pallas_call(body, grid=(r // tr,), in_specs=[pl.BlockSpec((N_DEV, tr, cc), lambda i: (0, i, 0))],
                          out_specs=pl.BlockSpec((tr, cc), lambda i: (i, 0)), out_shape=jax.ShapeDtypeStruct((r, cc), F32),
                          name=name, compiler_params=_cparams("parallel"))(a8)


def all_reduce8(flat, name):
    n = flat.shape[0]
    unit = N_DEV * 256 * 128
    npad = -(-n // unit) * unit
    a = jnp.pad(flat, (0, npad - n)).reshape(N_DEV, npad // (N_DEV * 128), 128)
    mine = _sum8(all_to_all8(a, name + "_rs"), name + "_sum")
    return all_gather8(mine, name + "_ag").reshape(npad)[:n]


def _make_split(t, cuts):
    def pieces(qkv):
        out = []
        for rows in (slice(None, t), slice(t, None)):
            out += [qkv[rows, a:b] for a, b in zip(cuts[:-1], cuts[1:])]
        return tuple(out)

    @jax.custom_vjp
    def split(qkv):
        return pieces(qkv)

    def fwd(qkv):
        return pieces(qkv), None

    def bwd(_, g):
        n = len(cuts) - 1
        return (jnp.concatenate([jnp.concatenate(g[:n], axis=1), jnp.concatenate(g[n:], axis=1)], axis=0),)

    split.defvjp(fwd, bwd)
    return split


def _local_loss(x, ctx, target, mods, small, big, pending, shards, dims):
    t, m, d = dims["t"], dims["m"], dims["d"]
    a_w, bq_w, bkv_w = dims["a_w"], dims["bq_w"], dims["bkv_w"]
    z = jnp.concatenate([x, ctx], axis=0)
    big = dict(big)

    def grp(layer, j, n_groups=2):
        return mods[layer, :n_groups, j][:, None, :]

    def lin(kind, out_dtype, name, a, wname, gather=()):
        y, gathered = make_linear(kind, out_dtype, name)(a, big[wname], shards[wname], tuple(pending[n] for n in gather))
        big.update(zip(gather, gathered))
        return y

    def ffn(layer, a, gather_in, gather_out):
        n1, n3, n2 = f"ffn_w1_{layer}", f"ffn_w3_{layer}", f"ffn_w2_{layer}"
        gathering = (tuple(pending[n] for n in gather_in), tuple(pending[n] for n in gather_out))
        w2 = big[n2] if n2 in big else pending[n2]
        y, (got_in, got_out) = make_ffn(f"ffn{layer}", n2 not in big)(a, big[n1], big[n3], w2, shards[n1], shards[n3],
                                                                    shards[n2], gathering)
        big.update(zip(gather_in, got_in))
        big.update(zip(gather_out, got_out))
        return y

    h = make_norm_mod(t, "norm_mix0")(z, small["norm_mix"][0][None], grp(0, 1), grp(0, 0))
    qkv = lin("col", BF16, "attn_in", h, "attn_w_in", ("attn_w_out", "ffn_w1_0"))
    o3, o5 = 3 * a_w, 3 * a_w + bq_w + bkv_w
    cuts = (0, a_w, 2 * a_w, o3, o3 + bq_w, o5, o5 + bkv_w)
    qa, ka, va, qb_u, kb_u, vb, qa_c, ka_c, va_c, qb_c, kb_c, vb_c = _make_split(t, cuts)(qkv)
    qb, kb = make_rope(t, "rope_q")(qb_u), make_rope(t, "rope_k")(kb_u)
    sink = small["attn_sink"][0]
    no_sink = jnp.zeros((a_w // HEAD_DIM,), F32)
    na_meta, na_span, na_cases = _na_plan(t)
    oa = make_attention(na_meta, 1, na_span, True, False, True, "na")(
        qa, ka, va, ka_c, va_c, _na_bias(small["attn_rpb"][0], na_cases), no_sink)
    sw_meta, sw_span, sw_bias = _sw_plan(t)
    grp_b = bq_w // bkv_w
    ob = make_attention(sw_meta, grp_b, sw_span, True, True, False, "swa")(qb, kb, vb, kb_c, vb_c, jnp.asarray(sw_bias), sink)
    c_meta = np.array([[0] * (m // ATTN_BLOCK), [0] * (m // ATTN_BLOCK), [1] + [0] * (m // ATTN_BLOCK - 1)], np.int32)
    zero_bias = jnp.zeros((1, 1, ATTN_BLOCK, m), F32)
    oa_c = make_attention(c_meta, 1, m, False, False, False, "ctx_na")(qa_c, ka_c, va_c, ka_c, va_c, zero_bias, no_sink)
    ob_c = make_attention(c_meta, grp_b, m, False, True, False, "ctx_swa")(qb_c, kb_c, vb_c, kb_c, vb_c, zero_bias, sink)
    o = jnp.concatenate([jnp.concatenate([oa, ob], axis=1), jnp.concatenate([oa_c, ob_c], axis=1)], axis=0)
    y = lin("row", F32, "attn_out", o, "attn_w_out", ("ffn_w3_0",))
    z, h = make_residual_norm(t, "mix0_to_ffn0")(z, y, grp(0, 2), small["norm_ffn"][0][None], grp(0, 4), grp(0, 3))
    y = ffn(0, h, ("ssm_w_glu", "ffn_w1_1"), ("ffn_w3_1",))

    z, h = make_residual_norm(t, "ffn0_to_mix1")(z, y, grp(0, 5), small["norm_mix"][1][None], grp(1, 1), grp(1, 0))
    hx, hc = h[:t], h[t:]
    lam, wd, wr = _s5_params(small["ssm_a_re"][0], small["ssm_a_im"][0], small["ssm_log_dt"][0], small["ssm_b_re"][0],
                             small["ssm_b_im"][0], small["ssm_c_re"][0], small["ssm_c_im"][0])
    u2 = jnp.stack([_to_scan_order(jnp.concatenate([hc, hx], axis=0)), _to_scan_order(h)])
    y2 = make_s5_core("s5")(u2, lam, wd, wr)
    ys = _from_scan_order(y2[0])[m:] + _from_scan_order(y2[1])[:t]
    gl = make_gelu_in("gelu")(hx, ys, small["ssm_d_full"][None])
    zz = lin("col", F32, "glu_w", gl, "ssm_w_glu", ("ffn_w2_1",))
    yx = make_glu("glu")(zz, small["ssm_b_glu_full"][None])
    xs, h = make_residual_norm(t, "mix1_to_ffn1")(z[:t], yx, grp(1, 2, 1), small["norm_ffn"][1][None], grp(1, 4, 1), grp(1, 3, 1))
    xs = make_gated_residual(t, "res_ffn1")(xs, ffn(1, h, (), ()), grp(1, 5, 1))
    return make_final_loss("loss_head")(xs, small["norm_final"][None], target)[0, 0]


_WEIGHTS = ['c_ctx', 'ada_w', 'ada_b', 'norm_mix', 'norm_ffn', 'ffn_w1', 'ffn_w3', 'ffn_w2', 'attn_w_in', 'attn_w_out',
            'attn_rpb', 'attn_sink', 'ssm_a_re', 'ssm_a_im', 'ssm_log_dt', 'ssm_b_re', 'ssm_b_im', 'ssm_c_re', 'ssm_c_im',
            'ssm_d', 'ssm_w_glu', 'ssm_b_glu', 'norm_final']
_LOCAL_SMALL = ['norm_mix', 'norm_ffn', 'attn_rpb', 'attn_sink', 'ssm_a_re', 'ssm_a_im', 'ssm_log_dt', 'ssm_b_re',
                'ssm_b_im', 'ssm_c_re', 'ssm_c_im', 'norm_final']
_MOD_ROWS = 16


def _gather_chip_vector(v, name):
    g = all_gather8(v[None], name)
    return g[0::2, 0, :].reshape(-1)


def kernel(x, c, ctx, c_ctx, ada_w, ada_b, norm_mix, norm_ffn, ffn_w1, ffn_w3, ffn_w2, attn_w_in, attn_w_out, attn_rpb, attn_sink, ssm_a_re, ssm_a_im, ssm_log_dt, ssm_b_re, ssm_b_im, ssm_c_re, ssm_c_im, ssm_d, ssm_w_glu, ssm_b_glu, norm_final, loss_target, m_c_ctx, m_ada_w, m_ada_b, m_norm_mix, m_norm_ffn, m_ffn_w1, m_ffn_w3, m_ffn_w2, m_attn_w_in, m_attn_w_out, m_attn_rpb, m_attn_sink, m_ssm_a_re, m_ssm_a_im, m_ssm_log_dt, m_ssm_b_re, m_ssm_b_im, m_ssm_c_re, m_ssm_c_im, m_ssm_d, m_ssm_w_glu, m_ssm_b_glu, m_norm_final, v_c_ctx, v_ada_w, v_ada_b, v_norm_mix, v_norm_ffn, v_ffn_w1, v_ffn_w3, v_ffn_w2, v_attn_w_in, v_attn_w_out, v_attn_rpb, v_attn_sink, v_ssm_a_re, v_ssm_a_im, v_ssm_log_dt, v_ssm_b_re, v_ssm_b_im, v_ssm_c_re, v_ssm_c_im, v_ssm_d, v_ssm_w_glu, v_ssm_b_glu, v_norm_final):
    env = dict(locals())
    w = {n: env[n] for n in _WEIGHTS}
    mom = {n: env["m_" + n] for n in _WEIGHTS}
    var = {n: env["v_" + n] for n in _WEIGHTS}
    _, t, d = x.shape
    m = ctx.shape[1]
    px, py, pc = _my_pos()
    s_me = 2 * px + py
    a_w =attn_rpb.shape[1] * HEAD_DIM
    bq_w = attn_sink.shape[1] * HEAD_DIM
    bkv_w = (4 * attn_w_in.shape[2] - 3 * a_w - bq_w) // 2
    dims = dict(t=t, m=m, d=d, a_w=a_w, bq_w=bq_w, bkv_w=bkv_w)
    n_layers = ada_w.shape[0]
    ada_cols = ada_w.shape[2]

    big = {"attn_w_in": gather_weight(attn_w_in[0], "ag_attn_in")}
    pending = {"attn_w_out": _ag_prepare(attn_w_out[0]), "ssm_w_glu": _ag_prepare(ssm_w_glu[0])}
    for l in range(n_layers):
        pending.update({f"ffn_w1_{l}": _ag_prepare(ffn_w1[l]), f"ffn_w3_{l}": _ag_prepare(ffn_w3[l]),
                        f"ffn_w2_{l}": _ag_prepare(ffn_w2[l])})
    small ={n: w[n] for n in _LOCAL_SMALL}
    small["ssm_d_full"] = _gather_chip_vector(ssm_d[0], "ag_ssm_d")
    small["ssm_b_glu_full"] = _gather_chip_vector(ssm_b_glu[0], "ag_b_glu")

    c_all = all_gather8(c, "ag_c")[:, 0, :]
    cond = jnp.concatenate([c_all, c_ctx[None], jnp.zeros((_MOD_ROWS - N_DEV - 1, d), F32)], axis=0)
    sig = jax.nn.sigmoid(cond)
    silu_c = (cond * sig).astype(BF16)
    mods_shard = mm_nn(silu_c, ada_w, "col", F32, "ada_fwd").reshape(_MOD_ROWS, n_layers, ada_cols).transpose(1, 0, 2)
    send = jnp.stack([jnp.stack([mods_shard[:, tgt], mods_shard[:, N_DEV]], axis=1).reshape(2 * n_layers, ada_cols)
                      for tgt in range(N_DEV)])
    plan = [((0, 0, 0), lambda me, peer: _lin(me), lambda me, peer: 2 * me[0] + me[1])]
    plan += [(f, lambda me, peer: _lin(peer), lambda me, peer: 2 * me[0] + me[1]) for f in _CHIP_FLIPS]
    got = xchg(send, 4, plan, "mods_xchg")
    mods = got.reshape(4, n_layers, 2, ada_cols).transpose(1, 2, 0, 3).reshape(n_layers, 2, 4 * ada_cols)
    mods = (mods + ada_b[:, None, :]).reshape(n_layers, 2, 6, d)

    shards = {"attn_w_in": attn_w_in[0], "attn_w_out": attn_w_out[0], "ssm_w_glu": ssm_w_glu[0]}
    for l in range(n_layers):
        shards.update({f"ffn_w1_{l}": ffn_w1[l], f"ffn_w3_{l}": ffn_w3[l], f"ffn_w2_{l}": ffn_w2[l]})

    def local(xx, mods_, small_, shards_):
        return _local_loss(xx, ctx[0], loss_target[0], mods_, small_, big, pending, shards_, dims)

    loss_local, vjp = jax.vjp(local, x[0], mods, small, shards)
    g_x, g_mods, g_small, g_shards = vjp(jnp.ones((), F32))
    loss = lax.psum(loss_local, ("x", "y", "c"))
    grads = {"attn_w_in": g_shards["attn_w_in"][None], "attn_w_out": g_shards["attn_w_out"][None],
             "ssm_w_glu": g_shards["ssm_w_glu"][None]}
    for n in ("ffn_w1", "ffn_w3", "ffn_w2"):
        grads[n] = jnp.stack([g_shards[f"{n}_{l}"] for l in range(n_layers)])

    gm = all_gather8(g_mods.reshape(2 * n_layers, 6 * d), "ag_dmods").reshape(N_DEV, n_layers, 2, 6 * d)
    ctx_row = gm[0, :, 1]
    for j in range(1, N_DEV):
        ctx_row = ctx_row + gm[j, :, 1]
    dm16 = jnp.concatenate([gm[:, :, 0].transpose(1, 0, 2), ctx_row[:, None], jnp.zeros((n_layers, _MOD_ROWS - N_DEV - 1, 6 * d), F32)], axis=1)
    grads["ada_b"] = jnp.sum(dm16, axis=1)
    dm_mine = lax.dynamic_slice_in_dim(dm16, s_me * ada_cols, ada_cols, axis=2).astype(BF16)
    grads["ada_w"] = jnp.stack([mm_tn(silu_c, dm_mine[l], (1, d, ada_cols), "col", F32, f"ada_dw{l}")[0] for l in range(n_layers)])
    dsilu = mm_nt(dm_mine.transpose(1, 0, 2).reshape(_MOD_ROWS, n_layers * ada_cols), ada_w, "col", F32, "ada_dc")
    dsilu_ctx = 0.5 * dsilu[N_DEV]

    packed = [(n, g_small[n]) for n in _LOCAL_SMALL] + [("ssm_d", g_small["ssm_d_full"]), ("ssm_b_glu", g_small["ssm_b_glu_full"]),
                                                        ("c_ctx", dsilu_ctx)]
    flat = all_reduce8(jnp.concatenate([a.reshape(-1) for _, a in packed]), "ar_small")
    off = 0
    for n, a in packed:
        grads[n] = flat[off:off + a.size].reshape(a.shape)
        off += a.size
    sig_ctx = jax.nn.sigmoid(c_ctx)
    grads["c_ctx"] = grads["c_ctx"] * (sig_ctx * (1.0 + c_ctx * (1.0 - sig_ctx)))
    grads["ssm_d"] = lax.dynamic_slice_in_dim(grads["ssm_d"], s_me * ssm_d.shape[1], ssm_d.shape[1])[None]
    grads["ssm_b_glu"] = lax.dynamic_slice_in_dim(grads["ssm_b_glu"], s_me * ssm_b_glu.shape[1], ssm_b_glu.shape[1])[None]

    delta, new_m, new_v = {}, {}, {}
    for n in _WEIGHTS:
        delta[n], new_m[n], new_v[n] = adamw(w[n], grads[n], mom[n], var[n], "adamw_" + n)
    return (loss, g_x[None], *[grads[n] for n in _WEIGHTS], *[delta[n] for n in _WEIGHTS],
            *[new_m[n] for n in _WEIGHTS], *[new_v[n] for n in _WEIGHTS])
```

```python
import functools
import math

import numpy as np
import jax
import jax.numpy as jnp
from jax import lax
from jax.experimental import pallas as pl
from jax.experimental.pallas import tpu as pltpu

F32 = jnp.float32
BF16 = jnp.bfloat16
MESH = pl.DeviceIdType.MESH

HEAD_DIM = 128
GRID_W = 64
NA_ROWS = 8
NA_COLS = 16
SW_RADIUS = 128
ATTN_BLOCK = 128
ROPE_BASE = 10000.0
SSM_GROUP = 16
SSM_STATE = 64
SSM_TILE_GROUPS = 8
SCAN_BLOCKS = 8
EPS = 1e-6
NEG_INF = -1e30
ADAM_LR, ADAM_B1, ADAM_B2, ADAM_EPS, ADAM_WD, ADAM_STEP = 0.001, 0.9, 0.999, 1e-08, 0.01, 10
VMEM_LIMIT_BYTES = 56 * 1024 * 1024
N_DEV = 8


def _cparams(*sem):
    return pltpu.CompilerParams(dimension_semantics=tuple(sem) if sem else None, vmem_limit_bytes=VMEM_LIMIT_BYTES)


def _pick(n, cands):
    for c in cands:
        if n % c == 0:
            return c
    return n


def _dot_nn(a, b):
    return jnp.dot(a, b, preferred_element_type=F32)


def _dot_nt(a, b):
    return lax.dot_general(a, b, (((1,), (1,)), ((), ())), preferred_element_type=F32)


def _dot_tn(a, b):
    return lax.dot_general(a, b, (((0,), (0,)), ((), ())), preferred_element_type=F32)


def _mm_call(name, grid, ins, in_specs, o_specs, out_sds, acc_shape, step, carry=None):
    nk = grid[2]
    nb = 0 if carry is None else len(carry)
    ni, no = len(ins), len(out_sds)
    assert nk == 1 or no == 1

    def body(*refs):
        in_refs = refs[:ni]
        o_refs = refs[ni + nb:ni + nb + no]
        rest = refs[ni + nb + no + nb:]
        acc_ref = rest[0] if nk > 1 else None
        o_ref = o_refs[0]
        if carry is not None:
            bufs = refs[ni + nb + no:ni + nb + no + nb]
            sems = rest[-2 * nb:]
            ids = [pl.program_id(ax) for ax in range(3)]
            first = (ids[0] == 0) & (ids[1] == 0) & (ids[2] == 0)
            last = (ids[0] == grid[0] - 1) & (ids[1] == grid[1] - 1) & (ids[2] == grid[2] - 1)
            copies = []
            for q, (_, plan) in enumerate(carry):
                copies += _remote_copies(bufs[q], bufs[q], sems[2 * q], sems[2 * q + 1], plan)

            @pl.when(first)
            def _():
                for cp in copies:
                    cp.start()

        kk = pl.program_id(2)
        if nk == 1:
            for ref, val in zip(o_refs, step(*in_refs)):
                ref[...] = val.astype(ref.dtype)
        else:
            @pl.when(kk == 0)
            def _():
                acc_ref[...] = step(*in_refs)[0]

            @pl.when(kk > 0)
            def _():
                acc_ref[...] = step(*in_refs)[0] + acc_ref[...]

            @pl.when(kk == nk - 1)
            def _():
                o_ref[...] = acc_ref[...].astype(o_ref.dtype)

        if carry is not None:
            @pl.when(last)
            def _():
                for cp in copies:
                    cp.wait_recv()
                for cp in copies:
                    cp.wait_send()

    scratch = [pltpu.VMEM(acc_shape, F32)] if nk > 1 else []
    if carry is None:
        return pl.pallas_call(
            body, grid=grid, in_specs=list(in_specs), out_specs=list(o_specs), out_shape=list(out_sds),
            scratch_shapes=scratch, name=name, compiler_params=_cparams("parallel", "parallel", "arbitrary"))(*ins)
    any_spec = pl.BlockSpec(memory_space=pl.ANY)
    for _, plan in carry:
        scratch += [pltpu.SemaphoreType.DMA((len(plan),)), pltpu.SemaphoreType.DMA((len(plan),))]
    return pl.pallas_call(
        body, grid=grid, in_specs=list(in_specs) + [any_spec] * nb, out_specs=list(o_specs) + [any_spec] * nb,
        out_shape=list(out_sds) + [jax.ShapeDtypeStruct(buf.shape, buf.dtype) for buf, _ in carry],
        scratch_shapes=scratch, input_output_aliases={ni + q: no + q for q in range(nb)}, name=name,
        compiler_params=_cparams("arbitrary", "arbitrary", "arbitrary"))(*ins, *[buf for buf, _ in carry])


_ROW_TILES = (768, 512, 256, 128, 64, 32, 16, 8)
_K_TILES = (2048, 1408, 1024, 512, 256, 128)
_CONTRACT_TILES = (1408, 1024, 768, 512, 256, 128)
_WEIGHT_BLOCK_BYTES = 12 * 1024 * 1024
_FUSED_ROW_TILES =(528, 512, 384, 256, 128, 64, 32, 16, 8)


def mm_nn(a, w3, kind, out_dtype, name, carry=None):
    r = a.shape[0]
    s, d1, d2 = w3.shape
    tm = _pick(r, _ROW_TILES)
    if kind == "col":
        tn = d2 if d1 * d2 * w3.dtype.itemsize <= _WEIGHT_BLOCK_BYTES else _pick(d2, (1024, 512, 256, 128))
        nn = d2 // tn
        grid = (s * nn, r // tm, 1)
        a_spec = pl.BlockSpec((tm, d1), lambda j, i, k: (i, 0))
        b_spec = pl.BlockSpec((None, d1, tn), lambda j, i, k: (j // nn, 0, j % nn))
        o_spec = pl.BlockSpec((tm, tn), lambda j, i, k: (i, j))
        n = s * d2

        def step(a_ref, b_ref):
            return (_dot_nn(a_ref[...], b_ref[...].astype(a_ref.dtype)),)
    else:
        tn = d2 if d2 <= 1024 else _pick(d2, (1024, 512, 256, 128))
        grid = (d2 // tn, r // tm, 1)
        a_spec = pl.BlockSpec((tm, s * d1), lambda j, i, k: (i, 0))
        b_spec = pl.BlockSpec((s, d1, tn), lambda j, i, k: (0, 0, j))
        o_spec = pl.BlockSpec((tm, tn), lambda j, i, k: (i, j))
        n = d2

        def step(a_ref, b_ref):
            p = _dot_nn(a_ref[:, :d1], b_ref[0])
            for q in range(1, s):
                p = _dot_nn(a_ref[:, q * d1:(q + 1) * d1], b_ref[q]) + p
            return (p,)
    out = _mm_call(name, grid, [a, w3], [a_spec, b_spec], [o_spec], [jax.ShapeDtypeStruct((r, n), out_dtype)], None, step, carry)
    return out if carry else out[0]


def mm_nt(dy, w3, kind, out_dtype, name, carry=None, addend=None):
    r = dy.shape[0]
    s, d1, d2 = w3.shape
    tm = _pick(r, _ROW_TILES)
    if kind == "col":
        fits = [c for c in (1024, 512, 256, 128) if d1 % c == 0 and s * c * d2 * w3.dtype.itemsize <= _WEIGHT_BLOCK_BYTES]
        tko = d1 if d1 <= 1024 and s * d1 * d2 * w3.dtype.itemsize <= _WEIGHT_BLOCK_BYTES else fits[0]
        grid = (d1 // tko, r // tm, 1)
        a_spec = pl.BlockSpec((tm, s * d2), lambda j, i, k: (i, 0))
        b_spec = pl.BlockSpec((s, tko, d2), lambda j, i, k: (0, j, 0))
        o_spec = pl.BlockSpec((tm, tko), lambda j, i, k: (i, j))
        kdim = d1

        def step(a_ref, b_ref, *more):
            p = _dot_nt(a_ref[:, :d2], b_ref[0].astype(a_ref.dtype))
            for q in range(1, s):
                p = _dot_nt(a_ref[:, q * d2:(q + 1) * d2], b_ref[q].astype(a_ref.dtype)) + p
            return (p + more[0][...].astype(F32),) if more else (p,)
    else:
        grid = (s, r // tm, 1)
        a_spec = pl.BlockSpec((tm, d2), lambda j, i, k: (i, 0))
        b_spec = pl.BlockSpec((None, d1, d2), lambda j, i, k: (j, 0, 0))
        o_spec = pl.BlockSpec((tm, d1), lambda j, i, k: (i, j))
        kdim = s * d1

        def step(a_ref, b_ref, *more):
            p = _dot_nt(a_ref[...], b_ref[...])
            return (p + more[0][...].astype(F32),) if more else (p,)
    ins, specs = [dy, w3], [a_spec, b_spec]
    if addend is not None:
        ins, specs = ins + [addend], specs + [o_spec]
    out = _mm_call(name, grid, ins, specs, [o_spec], [jax.ShapeDtypeStruct((r, kdim), out_dtype)], None, step, carry)
    return out if carry else out[0]


def mm_tn(a, dy, w_shape, kind, out_dtype, name):
    s, d1, d2 = w_shape
    r = dy.shape[0]
    tr = _pick(r, _CONTRACT_TILES)
    if kind == "col":
        tkk = d1 if d1 <= 1024 else _pick(d1, (1024, 512, 256, 128))
        grid = (s * (d1 // tkk), 1, r // tr)
        nkk = d1 // tkk
        a_spec = pl.BlockSpec((tr, tkk), lambda j, i, k: (k, j % nkk))
        b_spec = pl.BlockSpec((tr, d2), lambda j, i, k: (k, j // nkk))
        o_spec = pl.BlockSpec((None, tkk, d2), lambda j, i, k: (j // nkk, j % nkk, 0))
        acc = (tkk, d2)
    else:
        tn = d2 if d2 <= 1024 else _pick(d2, (1024, 512, 256, 128))
        nn = d2 // tn
        grid = (s * nn, 1, r // tr)
        a_spec = pl.BlockSpec((tr, d1), lambda j, i, k: (k, j // nn))
        b_spec = pl.BlockSpec((tr, tn), lambda j, i, k: (k, j % nn))
        o_spec = pl.BlockSpec((None, d1, tn), lambda j, i, k: (j // nn, 0, j % nn))
        acc = (d1, tn)

    def step(a_ref, b_ref):
        return (_dot_tn(a_ref[...], b_ref[...]),)
    return _mm_call(name, grid, [a, dy], [a_spec, b_spec], [o_spec], [jax.ShapeDtypeStruct(w_shape, out_dtype)], acc, step)[0]


def mm_ffn_in(h, w1, w3, name, carry=None):
    r = h.shape[0]
    s, d1, d2 = w1.shape
    tm = _pick(r, _FUSED_ROW_TILES)
    a_spec = pl.BlockSpec((tm, d1), lambda j, i, k: (i, 0))
    b_spec = pl.BlockSpec((None, d1, d2), lambda j, i, k: (j, 0, 0))
    o_spec = pl.BlockSpec((tm, d2), lambda j, i, k: (i, j))
    sds = jax.ShapeDtypeStruct((r, s * d2), BF16)

    def step(a_ref, b1_ref, b3_ref):
        p1 = _dot_nn(a_ref[...], b1_ref[...])
        p3 = _dot_nn(a_ref[...], b3_ref[...])
        return p1, p3, _silu(p1) * p3

    return _mm_call(name, (s, r // tm, 1), [h, w1, w3], [a_spec, b_spec, b_spec], [o_spec] * 3, [sds] * 3, None, step, carry)


def mm_ffn_back(dy, w2, h1, h3, name, carry=None):
    r = dy.shape[0]
    s, d1, d2 = w2.shape
    tm = _pick(r, _FUSED_ROW_TILES)
    a_spec = pl.BlockSpec((tm, d2), lambda j, i, k: (i, 0))
    b_spec = pl.BlockSpec((None, d1, d2), lambda j, i, k: (j, 0, 0))
    o_spec = pl.BlockSpec((tm, d1), lambda j, i, k: (i, j))
    sds = jax.ShapeDtypeStruct((r, s * d1), BF16)

    def step(a_ref, b_ref, h1_ref, h3_ref):
        g = _dot_nt(a_ref[...], b_ref[...])
        a1 = h1_ref[...].astype(F32)
        sg = jax.nn.sigmoid(a1)
        return g * h3_ref[...].astype(F32) * (sg * (1.0 + a1 * (1.0 - sg))), g * a1 * sg

    return _mm_call(name, (s, r // tm, 1), [dy, w2, h1, h3], [a_spec, b_spec, o_spec, o_spec], [o_spec] * 2, [sds] * 2,
                    None, step, carry)


def make_linear(kind, out_dtype, name):
    def run(a, w3, gathering):
        carry = [(buf, _ag_plan()) for buf in gathering] or None
        out = mm_nn(a, w3, kind, out_dtype, name + "_fwd", carry)
        if not carry:
            return out, ()
        return out[0], _ag_finish_all(out[1:], name)

    @jax.custom_vjp
    def linear(a, w3, w_shard, gathering):
        return run(a, w3, gathering)

    def fwd(a, w3, w_shard, gathering):
        return run(a, w3, gathering), (a, w3, w_shard.shape, len(gathering))

    def bwd(res, cts):
        a, w3, shard_shape, n_gathering = res
        dyb = cts[0].astype(BF16)
        dw = mm_tn(a, dyb, w3.shape, kind, BF16, name + "_dw")
        buf, plan = _rs_begin(dw, name)
        da, buf = mm_nt(dyb, w3, kind, a.dtype, name + "_dx", carry=[(buf, plan)])
        return da, None, _rs_finish(buf, name).reshape(shard_shape), (None,) * n_gathering

    linear.defvjp(fwd, bwd)
    return linear


def _ag_finish_all(bufs, name):
    return tuple(_ag_finish(buf, f"{name}_gathered{q}") for q, buf in enumerate(bufs))


def make_ffn(name, w2_pending):
    def run(h, w1, w3, w2, gathering):
        g_in, g_out = gathering
        if w2_pending:
            g_in = (w2,) + tuple(g_in)
        carry = [(buf, _ag_plan()) for buf in g_in] or None
        out = mm_ffn_in(h, w1, w3, name + "_in", carry)
        h1, h3, act = out[:3]
        done_in = _ag_finish_all(out[3:], name + "_in")
        if w2_pending:
            w2, done_in = done_in[0], done_in[1:]
        carry2 = [(buf, _ag_plan()) for buf in g_out] or None
        y = mm_nn(act, w2, "row", F32, name + "_out", carry2)
        done = (done_in, _ag_finish_all(y[1:], name + "_out") if carry2 else ())
        return (y[0] if carry2 else y), done, (h1, h3, act, w2)

    @jax.custom_vjp
    def ffn(h, w1, w3, w2, s1, s3, s2, gathering):
        return run(h, w1, w3, w2, gathering)[:2]

    def fwd(h, w1, w3, w2, s1, s3, s2, gathering):
        y, done, (h1, h3, act, w2) = run(h, w1, w3, w2, gathering)
        return (y, done), (h, w1, w3, w2, h1, h3, act, s1.shape, s3.shape, s2.shape, tuple(len(g) for g in gathering))

    def bwd(res, cts):
        h, w1, w3, w2, h1, h3, act, shape1, shape3, shape2, n_gathering = res
        dyb = cts[0].astype(BF16)
        buf2, plan2 = _rs_begin(mm_tn(act, dyb, w2.shape, "row", BF16, name + "_dw2"), name + "_w2")
        dh1, dh3, buf2 = mm_ffn_back(dyb, w2, h1, h3, name + "_back", carry=[(buf2, plan2)])
        buf1, plan1 = _rs_begin(mm_tn(h, dh1, w1.shape, "col", BF16, name + "_dw1"), name + "_w1")
        buf3, plan3 = _rs_begin(mm_tn(h, dh3, w3.shape, "col", BF16, name + "_dw3"), name + "_w3")
        dh, buf1 = mm_nt(dh1, w1, "col", h.dtype, name + "_dx1", carry=[(buf1, plan1)])
        dh, buf3 = mm_nt(dh3, w3, "col", h.dtype, name + "_dx3", carry=[(buf3, plan3)], addend=dh)
        grads = [_rs_finish(b, name + n).reshape(sh) for b, n, sh in
                 ((buf1, "_w1", shape1), (buf3, "_w3", shape3), (buf2, "_w2", shape2))]
        return (dh, None, None, None, *grads, tuple((None,) * n for n in n_gathering))

    ffn.defvjp(fwd, bwd)
    return ffn


def _row_tile(r, t0, d):
    cap = max(8, (2 * 1024 * 1024) // (4 * d))
    cands = [t for t in (1024, 512, 256, 128, 64, 32, 16, 8) if t <= cap]
    for t in cands:
        if r % t == 0 and t0 % t == 0:
            return t
    raise ValueError("no row tile")


def _grp_spec(d, nb0):
    return pl.BlockSpec((None, 1, d), lambda i: (i // nb0, 0, 0))


def _norm_mod_fwd(z, g, scale, shift, t0, name):
    r, d = z.shape
    tr = _row_tile(r, t0, d)
    nb0 = t0 // tr

    def body(z_ref, g_ref, sc_ref, sh_ref, o_ref):
        zz = z_ref[...]
        rstd = lax.rsqrt(jnp.mean(zz * zz, axis=-1, keepdims=True) + EPS)
        y = zz * rstd * g_ref[...]
        o_ref[...] = (y * (1.0 + sc_ref[...]) + sh_ref[...]).astype(o_ref.dtype)

    return pl.pallas_call(
        body, grid=(r // tr,),
        in_specs=[pl.BlockSpec((tr, d), lambda i: (i, 0)), pl.BlockSpec((1, d), lambda i: (0, 0)),
                  _grp_spec(d, nb0), _grp_spec(d, nb0)],
        out_specs=pl.BlockSpec((tr, d), lambda i: (i, 0)),
        out_shape=jax.ShapeDtypeStruct((r, d), BF16), name=name, compiler_params=_cparams("parallel"))(z, g, scale, shift)


def _norm_mod_bwd(z, g, scale, dh, t0, name):
    r, d = z.shape
    ng = scale.shape[0]
    tr = _row_tile(r, t0, d)
    nb0 = t0 // tr

    def body(z_ref, g_ref, sc_ref, dh_ref, dz_ref, dg_ref, dsc_ref, dsh_ref):
        i = pl.program_id(0)
        zz = z_ref[...]
        gg = g_ref[...]
        rstd = lax.rsqrt(jnp.mean(zz * zz, axis=-1, keepdims=True) + EPS)
        zhat = zz * rstd
        dhh = dh_ref[...].astype(F32)
        dy = dhh * (1.0 + sc_ref[...])
        dyg = dy * gg
        dz_ref[...] = rstd * (dyg - zhat * jnp.mean(dyg * zhat, axis=-1, keepdims=True))

        @pl.when(i == 0)
        def _():
            dg_ref[...] = jnp.zeros_like(dg_ref)

        @pl.when((i == 0) | (i == nb0))
        def _():
            dsc_ref[...] = jnp.zeros_like(dsc_ref)
            dsh_ref[...] = jnp.zeros_like(dsh_ref)

        dg_ref[...] += jnp.sum(dy * zhat, axis=0, keepdims=True)
        dsc_ref[...] += jnp.sum(dhh * (zhat * gg), axis=0, keepdims=True)
        dsh_ref[...] += jnp.sum(dhh, axis=0, keepdims=True)

    return pl.pallas_call(
        body, grid=(r // tr,),
        in_specs=[pl.BlockSpec((tr, d), lambda i: (i, 0)), pl.BlockSpec((1, d), lambda i: (0, 0)),
                  _grp_spec(d, nb0), pl.BlockSpec((tr, d), lambda i: (i, 0))],
        out_specs=[pl.BlockSpec((tr, d), lambda i: (i, 0)), pl.BlockSpec((1, d), lambda i: (0, 0)),
                   _grp_spec(d, nb0), _grp_spec(d, nb0)],
        out_shape=[jax.ShapeDtypeStruct((r, d), F32), jax.ShapeDtypeStruct((1, d), F32),
                   jax.ShapeDtypeStruct((ng, 1, d), F32), jax.ShapeDtypeStruct((ng, 1, d), F32)],
        name=name, compiler_params=_cparams("arbitrary"))(z, g, scale, dh)


def make_norm_mod(t0, name):
    @jax.custom_vjp
    def f(z, g, scale, shift):
        return _norm_mod_fwd(z, g, scale, shift, t0, name + "_fwd")

    def fwd(z, g, scale, shift):
        return _norm_mod_fwd(z, g, scale, shift, t0, name + "_fwd"), (z, g, scale)

    def bwd(res, dh):
        z, g, scale = res
        dz, dg, dsc, dsh = _norm_mod_bwd(z, g, scale, dh, t0, name + "_bwd")
        return dz, dg, dsc, dsh

    f.defvjp(fwd, bwd)
    return f


def _gated_fwd(z, y, gate, t0, name):
    r, d = z.shape
    tr = _row_tile(r, t0, d)
    nb0 = t0 // tr

    def body(z_ref, y_ref, g_ref, o_ref):
        o_ref[...] = z_ref[...] + g_ref[...] * y_ref[...].astype(F32)

    return pl.pallas_call(
        body, grid=(r // tr,),
        in_specs=[pl.BlockSpec((tr, d), lambda i: (i, 0)), pl.BlockSpec((tr, d), lambda i: (i, 0)), _grp_spec(d, nb0)],
        out_specs=pl.BlockSpec((tr, d), lambda i: (i, 0)),
        out_shape=jax.ShapeDtypeStruct((r, d), F32), name=name, compiler_params=_cparams("parallel"))(z, y, gate)


def _gated_bwd(y, gate, dzn, t0, name):
    r, d = y.shape
    ng = gate.shape[0]
    tr = _row_tile(r, t0, d)
    nb0 = t0 // tr

    def body(y_ref, g_ref, dz_ref, dy_ref, dg_ref):
        i = pl.program_id(0)
        dzz = dz_ref[...]
        dy_ref[...] = (g_ref[...] * dzz).astype(dy_ref.dtype)

        @pl.when((i == 0) | (i == nb0))
        def _():
            dg_ref[...] = jnp.zeros_like(dg_ref)

        dg_ref[...] += jnp.sum(dzz * y_ref[...].astype(F32), axis=0, keepdims=True)

    return pl.pallas_call(
        body, grid=(r // tr,),
        in_specs=[pl.BlockSpec((tr, d), lambda i: (i, 0)), _grp_spec(d, nb0), pl.BlockSpec((tr, d), lambda i: (i, 0))],
        out_specs=[pl.BlockSpec((tr, d), lambda i: (i, 0)), _grp_spec(d, nb0)],
        out_shape=[jax.ShapeDtypeStruct((r, d), y.dtype), jax.ShapeDtypeStruct((ng, 1, d), F32)],
        name=name, compiler_params=_cparams("arbitrary"))(y, gate, dzn)


def make_gated_residual(t0, name):
    @jax.custom_vjp
    def f(z, y, gate):
        return _gated_fwd(z, y, gate, t0, name + "_fwd")

    def fwd(z, y, gate):
        return _gated_fwd(z, y, gate, t0, name + "_fwd"), (y, gate)

    def bwd(res, dzn):
        y, gate = res
        dy, dgate = _gated_bwd(y, gate, dzn, t0, name + "_bwd")
        return dzn, dy, dgate

    f.defvjp(fwd, bwd)
    return f


def make_residual_norm(t0, name):
    def row_spec(tr, d):
        return pl.BlockSpec((tr, d), lambda i: (i, 0))

    def fwd_call(z, y, gate, g, scale, shift):
        r, d = z.shape
        tr = _row_tile(r, t0, d)
        nb0 = t0 // tr

        def body(z_ref, y_ref, gt_ref, g_ref, sc_ref, sh_ref, zn_ref, h_ref):
            zz = z_ref[...] + gt_ref[...] * y_ref[...].astype(F32)
            zn_ref[...] = zz
            rstd = lax.rsqrt(jnp.mean(zz * zz, axis=-1, keepdims=True) + EPS)
            h_ref[...] = (zz * rstd * g_ref[...] * (1.0 + sc_ref[...]) + sh_ref[...]).astype(h_ref.dtype)

        grp = _grp_spec(d, nb0)
        return pl.pallas_call(
            body, grid=(r // tr,),
            in_specs=[row_spec(tr, d), row_spec(tr, d), grp, pl.BlockSpec((1, d), lambda i: (0, 0)), grp, grp],
            out_specs=[row_spec(tr, d), row_spec(tr, d)],
            out_shape=[jax.ShapeDtypeStruct((r, d), F32), jax.ShapeDtypeStruct((r, d), BF16)],
            name=name + "_fwd", compiler_params=_cparams("parallel"))(z, y, gate, g, scale, shift)

    def bwd_call(zn, y, gate, g, scale, dzn, dh):
        r, d = zn.shape
        ng = scale.shape[0]
        tr = _row_tile(r, t0, d)
        nb0 = t0 // tr

        def body(zn_ref, y_ref, gt_ref, g_ref, sc_ref, dzn_ref, dh_ref, dz_ref, dy_ref, dgt_ref, dg_ref, dsc_ref, dsh_ref):
            i = pl.program_id(0)
            zz = zn_ref[...]
            gg = g_ref[...]
            rstd = lax.rsqrt(jnp.mean(zz * zz, axis=-1, keepdims=True) + EPS)
            zhat = zz * rstd
            dhh = dh_ref[...].astype(F32)
            dyn = dhh * (1.0 + sc_ref[...])
            dyg = dyn * gg
            dz = dzn_ref[...] + rstd * (dyg - zhat * jnp.mean(dyg * zhat, axis=-1, keepdims=True))
            dz_ref[...] = dz
            dy_ref[...] = (gt_ref[...] * dz).astype(dy_ref.dtype)

            @pl.when(i == 0)
            def _():
                dg_ref[...] = jnp.zeros_like(dg_ref)

            @pl.when((i == 0) | (i == nb0))
            def _():
                dgt_ref[...] = jnp.zeros_like(dgt_ref)
                dsc_ref[...] = jnp.zeros_like(dsc_ref)
                dsh_ref[...] = jnp.zeros_like(dsh_ref)

            dgt_ref[...] += jnp.sum(dz * y_ref[...].astype(F32), axis=0, keepdims=True)
            dg_ref[...] += jnp.sum(dyn * zhat, axis=0, keepdims=True)
            dsc_ref[...] += jnp.sum(dhh * (zhat * gg), axis=0, keepdims=True)
            dsh_ref[...] += jnp.sum(dhh, axis=0, keepdims=True)

        grp = _grp_spec(d, nb0)
        vec = pl.BlockSpec((1, d), lambda i: (0, 0))
        gsds = jax.ShapeDtypeStruct((ng, 1, d), F32)
        return pl.pallas_call(
            body, grid=(r // tr,),
            in_specs=[row_spec(tr, d), row_spec(tr, d), grp, vec, grp, row_spec(tr, d), row_spec(tr, d)],
            out_specs=[row_spec(tr, d), row_spec(tr, d), grp, vec, grp, grp],
            out_shape=[jax.ShapeDtypeStruct((r, d), F32), jax.ShapeDtypeStruct((r, d), y.dtype), gsds,
                       jax.ShapeDtypeStruct((1, d), F32), gsds, gsds],
            name=name + "_bwd", compiler_params=_cparams("arbitrary"))(zn, y, gate, g, scale, dzn, dh)

    @jax.custom_vjp
    def f(z, y, gate, g, scale, shift):
        return tuple(fwd_call(z, y, gate, g, scale, shift))

    def fwd(z, y, gate, g, scale, shift):
        zn, h = fwd_call(z, y, gate, g, scale, shift)
        return (zn, h), (zn, y, gate, g, scale)

    def bwd(res, cts):
        zn, y, gate, g, scale = res
        dz, dy, dgate, dg, dsc, dsh = bwd_call(zn, y, gate, g, scale, cts[0], cts[1])
        return dz, dy, dgate, dg, dsc, dsh

    f.defvjp(fwd, bwd)
    return f


def _ew_call(name, body, ins, outs_sds, r, widths_in, widths_out, tr, extra_in=(), extra_specs=(), sem="parallel"):
    in_specs = [pl.BlockSpec((tr, w), lambda i: (i, 0)) for w in widths_in] + list(extra_specs)
    out_specs = [pl.BlockSpec((tr, w), lambda i: (i, 0)) if w is not None else pl.BlockSpec(s.shape, lambda i: (0,) * len(s.shape))
                 for w, s in zip(widths_out, outs_sds)]
    return pl.pallas_call(body, grid=(r // tr,), in_specs=in_specs, out_specs=out_specs, out_shape=outs_sds,
                          name=name, compiler_params=_cparams(sem))(*ins, *extra_in)


def _silu(x):
    return x * jax.nn.sigmoid(x)


_GELU_C = math.sqrt(2.0 / math.pi)


def _gelu_and_grad(y):
    inner = _GELU_C * (y + 0.044715 * y * y * y)
    t = jnp.tanh(inner)
    val = 0.5 * y * (1.0 + t)
    grad = 0.5 * (1.0 + t) + 0.5 * y * (1.0 - t * t) * _GELU_C * (1.0 + 3 * 0.044715 * y * y)
    return val, grad


def make_gelu_in(name):
    def fwd_call(u, ys, dsk):
        r, d = u.shape
        tr = _row_tile(r, r, d)

        def body(u_ref, y_ref, d_ref, o_ref):
            y = d_ref[...] * u_ref[...].astype(F32) + y_ref[...]
            o_ref[...] = _gelu_and_grad(y)[0].astype(o_ref.dtype)

        return _ew_call(name + "_fwd", body, (u, ys), [jax.ShapeDtypeStruct((r, d), BF16)], r, (d, d), (d,), tr,
                        extra_in=(dsk,), extra_specs=(pl.BlockSpec((1, d), lambda i: (0, 0)),))[0]

    @jax.custom_vjp
    def f(u, ys, dsk):
        return fwd_call(u, ys, dsk)

    def fwd(u, ys, dsk):
        return fwd_call(u, ys, dsk), (u, ys, dsk)

    def bwd(res, dg):
        u, ys, dsk = res
        r, d = u.shape
        tr = _row_tile(r, r, d)

        def body(u_ref, y_ref, dg_ref, d_ref, du_ref, dy_ref, dd_ref):
            i = pl.program_id(0)
            uu = u_ref[...].astype(F32)
            y = d_ref[...] * uu + y_ref[...]
            dy = dg_ref[...].astype(F32) * _gelu_and_grad(y)[1]
            dy_ref[...] = dy
            du_ref[...] = (d_ref[...] * dy).astype(du_ref.dtype)

            @pl.when(i == 0)
            def _():
                dd_ref[...] = jnp.zeros_like(dd_ref)

            dd_ref[...] += jnp.sum(dy * uu, axis=0, keepdims=True)

        outs = [jax.ShapeDtypeStruct((r, d), u.dtype), jax.ShapeDtypeStruct((r, d), F32), jax.ShapeDtypeStruct((1, d), F32)]
        du, dy, dd = _ew_call(name + "_bwd", body, (u, ys, dg), outs, r, (d, d, d), (d, d, None), tr,
                              extra_in=(dsk,), extra_specs=(pl.BlockSpec((1, d), lambda i: (0, 0)),), sem="arbitrary")
        return du, dy, dd

    f.defvjp(fwd, bwd)
    return f


def make_glu(name):
    def fwd_call(z, b):
        r, d2 = z.shape
        d = d2 // 2
        tr = _row_tile(r, r, d2)

        def body(z_ref, b_ref, o_ref):
            zz = z_ref[...].astype(F32) + b_ref[...]
            o_ref[...] = zz[:, :d] * jax.nn.sigmoid(zz[:, d:])

        return _ew_call(name + "_fwd", body, (z,), [jax.ShapeDtypeStruct((r, d), F32)], r, (d2,), (d,), tr,
                        extra_in=(b,), extra_specs=(pl.BlockSpec((1, d2), lambda i: (0, 0)),))[0]

    @jax.custom_vjp
    def f(z, b):
        return fwd_call(z, b)

    def fwd(z, b):
        return fwd_call(z, b), (z, b)

    def bwd(res, do):
        z, b = res
        r, d2 = z.shape
        d = d2 // 2
        tr = _row_tile(r, r, d2)

        def body(z_ref, do_ref, b_ref, dz_ref, db_ref):
            i = pl.program_id(0)
            zz = z_ref[...].astype(F32) + b_ref[...]
            sg = jax.nn.sigmoid(zz[:, d:])
            g = do_ref[...]
            dza = g * sg
            dzb = g * zz[:, :d] * sg * (1.0 - sg)
            dz_ref[:, :d] = dza.astype(dz_ref.dtype)
            dz_ref[:, d:] = dzb.astype(dz_ref.dtype)

            @pl.when(i == 0)
            def _():
                db_ref[...] = jnp.zeros_like(db_ref)

            db_ref[:, :d] += jnp.sum(dza, axis=0, keepdims=True)
            db_ref[:, d:] += jnp.sum(dzb, axis=0, keepdims=True)

        outs = [jax.ShapeDtypeStruct((r, d2), z.dtype), jax.ShapeDtypeStruct((1, d2), F32)]
        dz, db = _ew_call(name + "_bwd", body, (z, do), outs, r, (d2, d), (d2, None), tr,
                          extra_in=(b,), extra_specs=(pl.BlockSpec((1, d2), lambda i: (0, 0)),), sem="arbitrary")
        return dz, db

    f.defvjp(fwd, bwd)
    return f


def make_final_loss(name):
    def call(z, g, target):
        r, d = z.shape
        tr = _row_tile(r, r, d)

        def body(z_ref, t_ref, g_ref, dz_ref, dg_ref, l_ref):
            i = pl.program_id(0)
            zz = z_ref[...]
            gg = g_ref[...]
            rstd = lax.rsqrt(jnp.mean(zz * zz, axis=-1, keepdims=True) + EPS)
            zhat = zz * rstd
            e = zhat * gg - t_ref[...]
            dy = e * (1.0 / d)
            dyg = dy * gg
            dz_ref[...] = rstd * (dyg - zhat * jnp.mean(dyg * zhat, axis=-1, keepdims=True))

            @pl.when(i == 0)
            def _():
                dg_ref[...] = jnp.zeros_like(dg_ref)
                l_ref[...] = jnp.zeros_like(l_ref)

            dg_ref[...] += jnp.sum(dy * zhat, axis=0, keepdims=True)
            l_ref[...] += jnp.sum(jnp.sum(e * e, axis=1, keepdims=True), axis=0, keepdims=True) * (0.5 / d)

        outs = [jax.ShapeDtypeStruct((r, d), F32), jax.ShapeDtypeStruct((1, d), F32), jax.ShapeDtypeStruct((1, 1), F32)]
        return _ew_call(name, body, (z, target), outs, r, (d, d), (d, None, None), tr,
                        extra_in=(g,), extra_specs=(pl.BlockSpec((1, d), lambda i: (0, 0)),), sem="arbitrary")

    @jax.custom_vjp
    def f(z, g, target):
        return call(z, g, target)[2]

    def fwd(z, g, target):
        dz, dg, loss = call(z, g, target)
        return loss, (dz, dg)

    def bwd(res, dl):
        dz, dg = res
        s = dl[0, 0]
        return dz * s, dg * s, None

    f.defvjp(fwd, bwd)
    return f


def _rope_tables(t):
    quarter = HEAD_DIM // 4
    inv_freq = ROPE_BASE ** (-np.arange(quarter, dtype=np.float64) / quarter)
    pos = np.arange(t)
    ang_r = (pos // GRID_W)[:, None] * inv_freq[None, :]
    ang_c = (pos % GRID_W)[:, None] * inv_freq[None, :]
    cos = np.concatenate([np.cos(ang_r), np.cos(ang_r), np.cos(ang_c), np.cos(ang_c)], axis=1)
    sin = np.concatenate([-np.sin(ang_r), np.sin(ang_r), -np.sin(ang_c), np.sin(ang_c)], axis=1)
    return jnp.asarray(cos, F32), jnp.asarray(sin, F32)


def _rope_call(x, cos, sin, name):
    t, w = x.shape
    tr = _pick(t, (512, 256, 128, 64))
    quarter = HEAD_DIM // 4

    def body(x_ref, c_ref, s_ref, o_ref):
        xx = x_ref[...].astype(F32)
        lane = lax.broadcasted_iota(jnp.int32, xx.shape, 1)
        first = (lane % (2 * quarter)) < quarter
        partner = jnp.where(first, pltpu.roll(xx, HEAD_DIM - quarter, 1), pltpu.roll(xx, quarter, 1))
        o_ref[...] = (xx * c_ref[...] + partner * s_ref[...]).astype(o_ref.dtype)

    return pl.pallas_call(
        body, grid=(t // tr, w // HEAD_DIM),
        in_specs=[pl.BlockSpec((tr, HEAD_DIM), lambda i, j: (i, j)), pl.BlockSpec((tr, HEAD_DIM), lambda i, j: (i, 0)),
                  pl.BlockSpec((tr, HEAD_DIM), lambda i, j: (i, 0))],
        out_specs=pl.BlockSpec((tr, HEAD_DIM), lambda i, j: (i, j)),
        out_shape=jax.ShapeDtypeStruct((t, w), x.dtype), name=name, compiler_params=_cparams("parallel", "parallel"))(x, cos, sin)


def make_rope(t, name):
    cos, sin = _rope_tables(t)

    @jax.custom_vjp
    def f(x):
        return _rope_call(x, cos, sin, name + "_fwd")

    def fwd(x):
        return _rope_call(x, cos, sin, name + "_fwd"), None

    def bwd(_, dy):
        return (_rope_call(dy, cos, -sin, name + "_bwd"),)

    f.defvjp(fwd, bwd)
    return f


def _attn_specs(g, span, tk, m, nbh, has_ctx, hb=1):
    hd = HEAD_DIM
    q_spec = pl.BlockSpec((ATTN_BLOCK, hb * g * hd), lambda h, i, meta: (i, h))
    kv_spec = pl.BlockSpec((tk, hb * hd), lambda h, i, meta: (0, h))
    c_spec = pl.BlockSpec((m, hb * hd), lambda h, i, meta: (0, h))
    if nbh > 1 and hb > 1:
        b_spec = pl.BlockSpec((None, hb, ATTN_BLOCK, span), lambda h, i, meta: (meta[1, i], h, 0, 0))
    elif nbh > 1:
        b_spec = pl.BlockSpec((None, None, ATTN_BLOCK, span), lambda h, i, meta: (meta[1, i], h, 0, 0))
    else:
        b_spec = pl.BlockSpec((None, None, ATTN_BLOCK, span), lambda h, i, meta: (meta[1, i], 0, 0, 0))
    sink_spec = pl.BlockSpec(memory_space=pltpu.SMEM)
    return q_spec, kv_spec, c_spec, b_spec, sink_spec


def _attn_probs(qh, ks, kc, bias, sink_val, scale, has_ctx, has_sink):
    s = _dot_nt(qh, ks) * scale + bias
    mx = jnp.max(s, axis=-1, keepdims=True)
    sc = None
    if has_ctx:
        sc = _dot_nt(qh, kc) * scale
        mx = jnp.maximum(mx, jnp.max(sc, axis=-1, keepdims=True))
    if has_sink:
        mx = jnp.maximum(mx, sink_val)
    p = jnp.exp(s - mx)
    l = jnp.sum(p, axis=-1, keepdims=True)
    pc = None
    if has_ctx:
        pc = jnp.exp(sc - mx)
        l = l + jnp.sum(pc, axis=-1, keepdims=True)
    ps = None
    if has_sink:
        ps = jnp.exp(sink_val - mx)
        l = l + ps
    return p, pc, ps, l


def _attn_fwd(q, k, v, kc, vc, bias, sink, meta, g, span, has_ctx, has_sink, name):
    rq, wq = q.shape
    tk, wk = k.shape
    hkv = wk // HEAD_DIM
    m = kc.shape[0]
    nbh = bias.shape[1]
    scale = HEAD_DIM ** -0.5
    nqb = rq // ATTN_BLOCK
    hb = _pick(hkv, (4, 2, 1)) if g == 1 else 1
    q_spec, kv_spec, c_spec, b_spec, sink_spec = _attn_specs(g, span, tk, m, nbh, has_ctx, hb)
    hd = HEAD_DIM

    def body(meta_ref, sink_ref, q_ref, k_ref, v_ref, kc_ref, vc_ref, b_ref, o_ref):
        h = pl.program_id(0)
        i = pl.program_id(1)
        ks0 = pl.multiple_of(meta_ref[0, i], 64)
        for kl in range(hb):
            kcols = slice(kl * hd, (kl + 1) * hd)
            ks = k_ref[pl.ds(ks0, span), kcols]
            vs = v_ref[pl.ds(ks0, span), kcols]
            bias_t = b_ref[kl] if (hb > 1 and nbh > 1) else b_ref[...]
            for hh in range(g):
                cols = slice((kl * g + hh) * hd, (kl * g + hh + 1) * hd)
                sink_val = sink_ref[(h * hb + kl) * g + hh] if has_sink else None
                p, pc, _, l = _attn_probs(q_ref[:, cols], ks, kc_ref[:, kcols], bias_t, sink_val, scale, has_ctx, has_sink)
                acc = jnp.dot(p.astype(BF16), vs, preferred_element_type=F32)
                if has_ctx:
                    acc = acc + jnp.dot(pc.astype(BF16), vc_ref[:, kcols], preferred_element_type=F32)
                o_ref[:, cols] = (acc / l).astype(o_ref.dtype)

    gs = pltpu.PrefetchScalarGridSpec(
        num_scalar_prefetch=1, grid=(hkv // hb, nqb),
        in_specs=[sink_spec, q_spec, kv_spec, kv_spec, c_spec, c_spec, b_spec], out_specs=q_spec)
    return pl.pallas_call(body, grid_spec=gs, out_shape=jax.ShapeDtypeStruct((rq, wq), BF16), name=name,
                          compiler_params=_cparams("parallel", "arbitrary"))(meta, sink, q, k, v, kc, vc, bias)


def _attn_bwd(q, k, v, kc, vc, bias, sink, meta, o, do, g, span, has_ctx, has_sink, want_dbias, name):
    rq, wq = q.shape
    tk, wk = k.shape
    hkv = wk // HEAD_DIM
    m = kc.shape[0]
    ncase, nbh = bias.shape[:2]
    scale = HEAD_DIM ** -0.5
    nqb = rq // ATTN_BLOCK
    hb = _pick(hkv, (2, 1)) if g == 1 else 1
    q_spec, kv_spec, c_spec, b_spec, sink_spec = _attn_specs(g, span, tk, m, nbh, has_ctx, hb)
    dsink_spec = pl.BlockSpec((None, 8, HEAD_DIM), lambda h, i, meta: (h, 0, 0))
    hd = HEAD_DIM

    def body(meta_ref, sink_ref, q_ref, k_ref, v_ref, kc_ref, vc_ref, b_ref, o_ref, do_ref,
             dq_ref, dk_ref, dv_ref, dkc_ref, dvc_ref, db_ref, dsk_ref):
        h = pl.program_id(0)
        i = pl.program_id(1)

        @pl.when(i == 0)
        def _():
            dk_ref[...] = jnp.zeros_like(dk_ref)
            dv_ref[...] = jnp.zeros_like(dv_ref)
            dkc_ref[...] = jnp.zeros_like(dkc_ref)
            dvc_ref[...] = jnp.zeros_like(dvc_ref)
            dsk_ref[...] = jnp.zeros_like(dsk_ref)

        if want_dbias:
            @pl.when(meta_ref[2, i] == 1)
            def _():
                db_ref[...] = jnp.zeros_like(db_ref)
        else:
            @pl.when(i == 0)
            def _():
                db_ref[...] = jnp.zeros_like(db_ref)

        ks0 = pl.multiple_of(meta_ref[0, i], 64)
        for kl in range(hb):
            kcols = slice(kl * hd, (kl + 1) * hd)
            ks = k_ref[pl.ds(ks0, span), kcols]
            vs = v_ref[pl.ds(ks0, span), kcols]
            per_head_bias = hb > 1 and nbh > 1
            bias_t = b_ref[kl] if per_head_bias else b_ref[...]
            dk_acc = jnp.zeros((span, hd), F32)
            dv_acc = jnp.zeros((span, hd), F32)
            for hh in range(g):
                cols = slice((kl * g + hh) * hd, (kl * g + hh + 1) * hd)
                qh = q_ref[:, cols]
                doh = do_ref[:, cols]
                sink_val = sink_ref[(h * hb + kl) * g + hh] if has_sink else None
                p, pc, ps, l = _attn_probs(qh, ks, kc_ref[:, kcols], bias_t, sink_val, scale, has_ctx, has_sink)
                inv_l = 1.0 / l
                delta = jnp.sum(doh.astype(F32) * o_ref[:, cols].astype(F32), axis=-1, keepdims=True)
                pn = p * inv_l
                ds = pn * (_dot_nt(doh, vs) - delta)
                dsb = ds.astype(BF16)
                dq = jnp.dot(dsb, ks, preferred_element_type=F32)
                dk_acc = dk_acc + _dot_tn(dsb, qh)
                dv_acc = dv_acc + _dot_tn(pn.astype(BF16), doh)
                if want_dbias and per_head_bias:
                    db_ref[kl] += ds
                elif want_dbias:
                    db_ref[...] += ds
                if has_ctx:
                    pcn = pc * inv_l
                    dsc = (pcn * (_dot_nt(doh, vc_ref[:, kcols]) - delta)).astype(BF16)
                    dq = dq + jnp.dot(dsc, kc_ref[:, kcols], preferred_element_type=F32)
                    dkc_ref[:, kcols] += _dot_tn(dsc, qh) * scale
                    dvc_ref[:, kcols] += _dot_tn(pcn.astype(BF16), doh)
                if has_sink:
                    dsv = -jnp.sum(ps * inv_l * delta, axis=0, keepdims=True)
                    dsk_ref[kl * g + hh:kl * g + hh + 1, :] += jnp.broadcast_to(dsv, (1, hd))
                dq_ref[:, cols] = (dq * scale).astype(dq_ref.dtype)
            dk_ref[pl.ds(ks0, span), kcols] += dk_acc * scale
            dv_ref[pl.ds(ks0, span), kcols] += dv_acc

    gs = pltpu.PrefetchScalarGridSpec(
        num_scalar_prefetch=1, grid=(hkv // hb, nqb),
        in_specs=[sink_spec, q_spec, kv_spec, kv_spec, c_spec, c_spec, b_spec, q_spec, q_spec],
        out_specs=[q_spec, kv_spec, kv_spec, c_spec, c_spec, b_spec if want_dbias else dsink_spec, dsink_spec])
    db_sds = jax.ShapeDtypeStruct((ncase, nbh, ATTN_BLOCK, span) if want_dbias else (hkv, 8, HEAD_DIM), F32)
    out_shape = [jax.ShapeDtypeStruct((rq, wq), BF16), jax.ShapeDtypeStruct((tk, wk), F32), jax.ShapeDtypeStruct((tk, wk), F32),
                 jax.ShapeDtypeStruct((m, wk), F32), jax.ShapeDtypeStruct((m, wk), F32), db_sds,
                 jax.ShapeDtypeStruct((hkv, 8, HEAD_DIM), F32)]
    return pl.pallas_call(body, grid_spec=gs, out_shape=out_shape, name=name,
                          compiler_params=_cparams("parallel", "arbitrary"))(meta, sink, q, k, v, kc, vc, bias, o, do)


def make_attention(meta_np, g, span, has_ctx, has_sink, want_dbias, name):
    meta = jnp.asarray(meta_np, jnp.int32)

    @jax.custom_vjp
    def f(q, k, v, kc, vc, bias, sink):
        return _attn_fwd(q, k, v, kc, vc, bias, sink, meta, g, span, has_ctx, has_sink, name + "_fwd")

    def fwd(q, k, v, kc, vc, bias, sink):
        o = _attn_fwd(q, k, v, kc, vc, bias, sink, meta, g, span, has_ctx, has_sink, name + "_fwd")
        return o, (q, k, v, kc, vc, bias, sink, o)

    def bwd(res, do):
        q, k, v, kc, vc, bias, sink, o = res
        dq, dk, dv, dkc, dvc, db, dsk = _attn_bwd(q, k, v, kc, vc, bias, sink, meta, o, do.astype(BF16), g, span,
                                                   has_ctx, has_sink, want_dbias, name + "_bwd")
        dsink = dsk[:, :g, 0].reshape(sink.shape) if has_sink else jnp.zeros_like(sink)
        if not want_dbias:
            db = jnp.zeros_like(bias)
        return dq, dk.astype(k.dtype), dv.astype(v.dtype), dkc.astype(kc.dtype), dvc.astype(vc.dtype), db, dsink

    f.defvjp(fwd, bwd)
    return f


def _dedupe_cases(tables):
    cases, idx, first = [], [], []
    for tbl in tables:
        if cases and np.array_equal(cases[-1], tbl):
            idx.append(len(cases) - 1)
            first.append(0)
        else:
            cases.append(tbl)
            idx.append(len(cases) - 1)
            first.append(1)
    return cases, idx, first


def _na_plan(t):
    rows = t // GRID_W
    qr = ATTN_BLOCK // GRID_W
    kr = qr + NA_ROWS - 1
    assert rows >= kr and rows % qr == 0
    span = kr * GRID_W
    kstart, tables = [], []
    qcol = np.tile(np.arange(GRID_W), qr)
    kcol = np.tile(np.arange(GRID_W), kr)
    win_c = np.clip(qcol - NA_COLS // 2, 0, GRID_W - NA_COLS)
    col_ok = (kcol[None, :] >= win_c[:, None]) & (kcol[None, :] < win_c[:, None] + NA_COLS)
    dcol = np.clip(kcol[None, :] - qcol[:, None] + NA_COLS - 1, 0, 2 * NA_COLS - 2)
    for r0 in range(0, rows, qr):
        kb = int(np.clip(r0 - NA_ROWS // 2, 0, rows - kr))
        qrow = r0 + np.repeat(np.arange(qr), GRID_W)
        krow = kb + np.repeat(np.arange(kr), GRID_W)
        win_r = np.clip(qrow - NA_ROWS // 2, 0, rows - NA_ROWS)
        row_ok = (krow[None, :] >= win_r[:, None]) & (krow[None, :] < win_r[:, None] + NA_ROWS)
        drow = np.clip(krow[None, :] - qrow[:, None] + NA_ROWS - 1, 0, 2 * NA_ROWS - 2)
        tables.append(np.stack([row_ok & col_ok, drow, dcol]).astype(np.int32))
        kstart.append(kb * GRID_W)
    cases, idx, first = _dedupe_cases(tables)
    meta = np.array([kstart, idx, first], np.int32)
    return meta, span, np.stack(cases)


def _na_bias(rpb, cases):
    valid, drow, dcol = cases[:, 0], cases[:, 1], cases[:, 2]
    ncase, qn, span = valid.shape
    qr, kr = qn // GRID_W, span // GRID_W
    drow_s = drow.reshape(ncase, qr, GRID_W, kr, GRID_W)[:, :, 0, :, 0]
    dcol_s = dcol[0].reshape(qr, GRID_W, kr, GRID_W)[0, :, 0, :]
    oh_r = jnp.asarray(np.eye(2 * NA_ROWS - 1, dtype=np.float32)[drow_s])
    oh_c = jnp.asarray(np.eye(2 * NA_COLS - 1, dtype=np.float32)[dcol_s])
    tmp = jnp.einsum("hrc,xyc->hrxy", rpb, oh_c, precision=lax.Precision.HIGHEST)
    b = jnp.einsum("nakr,hrxy->nhaxky", oh_r, tmp, precision=lax.Precision.HIGHEST).reshape(ncase, -1, qn, span)
    return jnp.where(jnp.asarray(valid[:, None] > 0), b, NEG_INF)


def _sw_plan(t):
    span = 3 * ATTN_BLOCK
    assert t >= span
    kstart, tables = [], []
    for b in range(t // ATTN_BLOCK):
        ks = int(np.clip((b - 1) * ATTN_BLOCK, 0, t - span))
        qpos = b * ATTN_BLOCK + np.arange(ATTN_BLOCK)
        kpos = ks + np.arange(span)
        ok = np.abs(kpos[None, :] - qpos[:, None]) <= SW_RADIUS
        tables.append(np.where(ok, 0.0, NEG_INF).astype(np.float32))
        kstart.append(ks)
    cases, idx, first = _dedupe_cases(tables)
    return np.array([kstart, idx, first], np.int32), span, np.stack(cases)[:, None]


def _cmul(ar, ai, br, bi):
    return ar * br - ai * bi, ar * bi + ai * br


def _s5_scan_call(x2, win, lam, cin, wout, reverse, n_chunks, name):
    _, ll, d = x2.shape
    nt = d // HEAD_DIM
    sw = 2 * SSM_TILE_GROUPS * SSM_STATE
    hw = sw // 2
    rows = ll // n_chunks
    full = cin is not None

    down_dir = 0 if reverse else 1

    def chunk_idx(k, dd):
        return jnp.where(dd == down_dir, n_chunks - 1 - k, k)

    n_sub = 1 if full else _pick(rows // 16, (6, 3, 1))
    sub = rows // n_sub
    ics = sub // SCAN_BLOCKS

    def body(*refs):
        if full:
            x_ref, win_ref, lam_ref, cin_ref, wout_ref, s_out, y_out = refs[:7]
        else:
            x_ref, win_ref, lam_ref, f_out = refs[:4]
        ubs, st_ref = refs[-1 - n_sub:-1], refs[-1]
        k = pl.program_id(2)
        down = pl.program_id(0) == down_dir

        @pl.when(k == 0)
        def _():
            st_ref[...] = cin_ref[...] if full else jnp.zeros_like(st_ref)

        def sub_rows(p):
            return pl.ds(pl.multiple_of(jnp.where(down, n_sub - 1 - p, p) * sub, 16), sub)

        def drive(p):
            ubs[p][...] = jnp.dot(x_ref[sub_rows(p), :].astype(BF16), win_ref[...], preferred_element_type=F32)

        lr = lam_ref[:, :hw]
        li = lam_ref[:, hw:]
        if full:
            ub = ubs[0]
            drive(0)

            def step(ii, carry):
                sr, si = carry
                i = jnp.where(down, ics - 1 - ii, ii)
                r0 = pl.multiple_of(i * SCAN_BLOCKS, SCAN_BLOCKS)
                nr = lr * sr - li * si + ub[pl.ds(r0, SCAN_BLOCKS), :hw]
                ni = lr * si + li * sr + ub[pl.ds(r0, SCAN_BLOCKS), hw:]
                ub[pl.ds(r0, SCAN_BLOCKS), :hw] = nr
                ub[pl.ds(r0, SCAN_BLOCKS), hw:] = ni
                return nr, ni

            sr, si = lax.fori_loop(0, ics, step, (st_ref[:, :hw], st_ref[:, hw:]), unroll=4 if ics % 4 == 0 else 1)
            sb = ub[...].astype(BF16)
            s_out[...] = sb
            y_out[...] = jnp.dot(sb, wout_ref[...], preferred_element_type=F32)
        else:
            sr, si = st_ref[:, :hw], st_ref[:, hw:]
            drive(0)
            for p in range(n_sub):
                if p + 1 < n_sub:
                    drive(p + 1)
                for ii in range(ics):
                    r0 = pl.multiple_of(jnp.where(down, (ics - 1 - ii) * SCAN_BLOCKS, ii * SCAN_BLOCKS), SCAN_BLOCKS)
                    ur = ubs[p][pl.ds(r0, SCAN_BLOCKS), :hw]
                    ui = ubs[p][pl.ds(r0, SCAN_BLOCKS), hw:]
                    sr, si = lr * sr - li * si + ur, lr * si + li * sr + ui
        st_ref[:, :hw] = sr
        st_ref[:, hw:] = si
        if not full:
            @pl.when(k == n_chunks - 1)
            def _():
                f_out[...] = st_ref[...]

    x_spec = pl.BlockSpec((None, rows, HEAD_DIM), lambda dd, t, k: (dd, chunk_idx(k, dd), t))
    win_spec = pl.BlockSpec((None, None, HEAD_DIM, sw), lambda dd, t, k: (dd, t, 0, 0))
    vec_spec = pl.BlockSpec((None, None, SCAN_BLOCKS, sw), lambda dd, t, k: (dd, t, 0, 0))
    scratch = [pltpu.VMEM((sub, sw), F32) for _ in range(n_sub)] + [pltpu.VMEM((SCAN_BLOCKS, sw), F32)]
    if full:
        in_specs = [x_spec, win_spec, vec_spec, vec_spec, pl.BlockSpec((None, None, sw, HEAD_DIM), lambda dd, t, k: (dd, t, 0, 0))]
        out_specs = [pl.BlockSpec((None, None, rows, sw), lambda dd, t, k: (dd, t, chunk_idx(k, dd), 0)), x_spec]
        out_shape = [jax.ShapeDtypeStruct((2, nt, ll, sw), BF16), jax.ShapeDtypeStruct((2, ll, d), F32)]
        args = (x2, win, lam, cin, wout)
    else:
        in_specs = [x_spec, win_spec, vec_spec]
        out_specs = vec_spec
        out_shape = jax.ShapeDtypeStruct((2, nt, SCAN_BLOCKS, sw), F32)
        args = (x2, win, lam)
    return pl.pallas_call(body, grid=(2, nt, n_chunks), in_specs=in_specs, out_specs=out_specs, out_shape=out_shape,
                          scratch_shapes=scratch, name=name,
                          compiler_params=_cparams("parallel", "parallel", "arbitrary"))(*args)


def _s5_bwd_call(dy2, wrt, lamc, cin, st, u2, wdt, n_chunks, name):
    _, ll, d = dy2.shape
    nt = d // HEAD_DIM
    sw = 2 * SSM_TILE_GROUPS * SSM_STATE
    hw = sw // 2
    rows = ll // n_chunks

    def chunk_idx(k, dd):
        return jnp.where(dd == 0, n_chunks - 1 - k, k)

    n_sub = _pick(rows // 16, (6, 3, 1))
    sub = rows // n_sub
    ics = sub // SCAN_BLOCKS

    def body(dy_ref, wrt_ref, lam_ref, cin_ref, stb_ref, u_ref, wdt_ref, du_out, dwd_out, dwr_out, dlam_out, *scratch):
        dss, sts, a_ref = scratch[:n_sub], scratch[n_sub:2 * n_sub], scratch[-1]
        k = pl.program_id(2)
        down = pl.program_id(0) == 0

        @pl.when(k == 0)
        def _():
            a_ref[...] = cin_ref[...]
            dwd_out[...] = jnp.zeros_like(dwd_out)
            dwr_out[...] = jnp.zeros_like(dwr_out)
            dlam_out[...] = jnp.zeros_like(dlam_out)

        def sub_rows(p):
            return pl.ds(pl.multiple_of(jnp.where(down, n_sub - 1 - p, p) * sub, 16), sub)

        def prepare(p):
            dss[p][...] = jnp.dot(dy_ref[sub_rows(p), :].astype(BF16), wrt_ref[...], preferred_element_type=F32)
            sts[p][...] = stb_ref[sub_rows(p), :].astype(F32)

        def finish(p):
            ab = dss[p][...].astype(BF16)
            du_out[sub_rows(p), :] = jnp.dot(ab, wdt_ref[...], preferred_element_type=F32).astype(du_out.dtype)
            dwd_out[...] += _dot_tn(u_ref[sub_rows(p), :].astype(BF16), ab)
            dwr_out[...] += _dot_tn(stb_ref[sub_rows(p), :], dy_ref[sub_rows(p), :].astype(BF16))

        lr = lam_ref[:, :hw]
        li = lam_ref[:, hw:]
        ar, ai, gr, gi = a_ref[:, :hw], a_ref[:, hw:], dlam_out[:, :hw], dlam_out[:, hw:]
        prepare(0)
        for p in range(n_sub):
            if p + 1 < n_sub:
                prepare(p + 1)
            for ii in range(ics):
                r0 = pl.multiple_of(jnp.where(down, (ics - 1 - ii) * SCAN_BLOCKS, ii * SCAN_BLOCKS), SCAN_BLOCKS)
                sr = sts[p][pl.ds(r0, SCAN_BLOCKS), :hw]
                si = sts[p][pl.ds(r0, SCAN_BLOCKS), hw:]
                gr = gr + ar * sr + ai * si
                gi = gi + ai * sr - ar * si
                ar, ai = (lr * ar - li * ai + dss[p][pl.ds(r0, SCAN_BLOCKS), :hw],
                          lr * ai + li * ar + dss[p][pl.ds(r0, SCAN_BLOCKS), hw:])
                dss[p][pl.ds(r0, SCAN_BLOCKS), :hw] = ar
                dss[p][pl.ds(r0, SCAN_BLOCKS), hw:] = ai
            finish(p)
        a_ref[:, :hw] = ar
        a_ref[:, hw:] = ai
        dlam_out[:, :hw] = gr
        dlam_out[:, hw:] = gi

    x_spec = pl.BlockSpec((None, rows, HEAD_DIM), lambda dd, t, k: (dd, chunk_idx(k, dd), t))
    w_in = pl.BlockSpec((None, None, HEAD_DIM, sw), lambda dd, t, k: (dd, t, 0, 0))
    w_out = pl.BlockSpec((None, None, sw, HEAD_DIM), lambda dd, t, k: (dd, t, 0, 0))
    vec_spec = pl.BlockSpec((None, None, SCAN_BLOCKS, sw), lambda dd, t, k: (dd, t, 0, 0))
    st_spec = pl.BlockSpec((None, None, rows, sw), lambda dd, t, k: (dd, t, chunk_idx(k, dd), 0))
    out_shape = [jax.ShapeDtypeStruct((2, ll, d), u2.dtype), jax.ShapeDtypeStruct((2, nt, HEAD_DIM, sw), F32),
                 jax.ShapeDtypeStruct((2, nt, sw, HEAD_DIM), F32), jax.ShapeDtypeStruct((2, nt, SCAN_BLOCKS, sw), F32)]
    return pl.pallas_call(
        body, grid=(2, nt, n_chunks),
        in_specs=[x_spec, w_in, vec_spec, vec_spec, st_spec, x_spec, w_out],
        out_specs=[x_spec, w_in, w_out, vec_spec], out_shape=out_shape,
        scratch_shapes=[pltpu.VMEM((sub, sw), F32) for _ in range(2 * n_sub)] + [pltpu.VMEM((SCAN_BLOCKS, sw), F32)],
        name=name, compiler_params=_cparams("parallel", "parallel", "arbitrary"))(dy2, wrt, lamc, cin, st, u2, wdt)


def _cpow(lr, li, n):
    rr, ri = jnp.ones_like(lr), jnp.zeros_like(li)
    br, bi = lr, li
    while n:
        if n & 1:
            rr, ri = _cmul(rr, ri, br, bi)
        br, bi = _cmul(br, bi, br, bi)
        n >>= 1
    return rr, ri


def _resolve_carries(finals, lam, block_len, down_dir):
    hw = finals.shape[-1] // 2
    pr, pi = _cpow(lam[:, :, 0, :hw], lam[:, :, 0, hw:], block_len)
    fr, fi = finals[..., :hw], finals[..., hw:]

    def walk(order):
        cr, ci = jnp.zeros_like(pr), jnp.zeros_like(pi)
        out = [None] * SCAN_BLOCKS
        for j in order:
            out[j] = jnp.concatenate([cr, ci], axis=-1)
            mr, mi = _cmul(pr, pi, cr, ci)
            cr, ci = mr + fr[:, :, j], mi + fi[:, :, j]
        return jnp.stack(out, axis=2)

    up, down = walk(range(SCAN_BLOCKS)), walk(range(SCAN_BLOCKS - 1, -1, -1))
    return jnp.stack([down[0], up[1]] if down_dir == 0 else [up[0], down[1]])


def _scan_chunks(ll):
    block_len = ll // SCAN_BLOCKS
    for ic in (132, 128, 96, 64, 48, 36, 32, 24, 16, 8):
        if block_len % ic == 0:
            return block_len // ic
    return 1


def make_s5_core(name):
    def run_fwd(u2, lam, wd, wr):
        ll = u2.shape[1]
        nc = _scan_chunks(ll)
        lam8 = jnp.broadcast_to(lam[:, :, None, :], lam.shape[:2] + (SCAN_BLOCKS, lam.shape[-1]))
        wdb = wd.astype(BF16)
        finals = _s5_scan_call(u2, wdb, lam8, None, None, False, nc, name + "_carry")
        cin = _resolve_carries(finals, lam8, ll // SCAN_BLOCKS, 1)
        st, y2 = _s5_scan_call(u2, wdb, lam8, cin, wr.astype(BF16), False, nc, name + "_scan")
        return y2, st, lam8

    @jax.custom_vjp
    def f(u2, lam, wd, wr):
        return run_fwd(u2, lam, wd, wr)[0]

    def fwd(u2, lam, wd, wr):
        y2, st, lam8 = run_fwd(u2, lam, wd, wr)
        return y2, (u2, lam8, wd, wr, st)

    def bwd(res, dy2):
        u2, lam8, wd, wr, st = res
        ll = u2.shape[1]
        nc = _scan_chunks(ll)
        hw = lam8.shape[-1] // 2
        lamc = jnp.concatenate([lam8[..., :hw], -lam8[..., hw:]], axis=-1)
        wrt = jnp.swapaxes(wr, 2, 3).astype(BF16)
        wdt = jnp.swapaxes(wd, 2, 3).astype(BF16)
        finals = _s5_scan_call(dy2, wrt, lamc, None, None, True, nc, name + "_bcarry")
        cin = _resolve_carries(finals, lamc, ll // SCAN_BLOCKS, 0)
        du2, dwd, dwr, dlam8 = _s5_bwd_call(dy2, wrt, lamc, cin, st, u2, wdt, nc, name + "_bscan")
        return du2, jnp.sum(dlam8, axis=2), dwd, dwr

    f.defvjp(fwd, bwd)
    return f


def _s5_params(a_re, a_im, log_dt, b_re, b_im, c_re, c_im):
    dt = jnp.exp(log_dt)[..., None]
    mag = jnp.exp(a_re * dt)
    lam_r, lam_i = mag * jnp.cos(a_im * dt), mag * jnp.sin(a_im * dt)
    den = a_re * a_re + a_im * a_im
    nr = lam_r - 1.0
    coef_r = (nr * a_re + lam_i * a_im) / den
    coef_i = (lam_i * a_re - nr * a_im) / den
    bbar_r = coef_r[..., None] * b_re - coef_i[..., None] * b_im
    bbar_i = coef_r[..., None] * b_im + coef_i[..., None] * b_re
    ndir, g, p = lam_r.shape
    tg = SSM_TILE_GROUPS
    nt = g // tg
    eye = jnp.eye(tg, dtype=F32)

    def tile_vec(v):
        return v.reshape(ndir, nt, tg * p)

    lam = jnp.concatenate([tile_vec(lam_r), tile_vec(lam_i)], axis=-1)

    def drive(b):
        bt = b.reshape(ndir, nt, tg, p, SSM_GROUP)
        return (jnp.swapaxes(bt, 3, 4)[:, :, :, :, None, :] * eye[None, None, :, None, :, None]).reshape(ndir, nt, tg * SSM_GROUP, tg * p)

    wd = jnp.concatenate([drive(bbar_r), drive(bbar_i)], axis=-1)

    def readout(c):
        ct = c.reshape(ndir, nt, tg, SSM_GROUP, p)
        return (jnp.swapaxes(ct, 3, 4)[:, :, :, :, None, :] * eye[None, None, :, None, :, None]).reshape(ndir, nt, tg * p, tg * SSM_GROUP)

    wr = jnp.concatenate([readout(c_re), -readout(c_im)], axis=2)
    return lam, wd, wr


def _to_scan_order(seq):
    ll, d = seq.shape
    return seq.reshape(SCAN_BLOCKS, ll // SCAN_BLOCKS, d).swapaxes(0, 1).reshape(ll, d)


def _from_scan_order(y2):
    ll, d = y2.shape
    return y2.reshape(ll // SCAN_BLOCKS, SCAN_BLOCKS, d).swapaxes(0, 1).reshape(ll, d)


def adamw(w, g, m, v, name):
    shape = w.shape
    cols = shape[-1] if len(shape) > 1 else shape[0]
    w2, g2, m2, v2 = (a.reshape(-1, cols) for a in (w, g, m, v))
    r = w2.shape[0]
    cap = max(1, (1024 * 1024) // (4 * cols))
    tr = r
    for t in (512, 256, 128, 64, 32, 16, 8):
        if t <= cap and r % t == 0:
            tr = t
            break
    c1 = 1.0 / (1.0 - ADAM_B1 ** ADAM_STEP)
    c2 = 1.0 / (1.0 - ADAM_B2 ** ADAM_STEP)

    def body(w_ref, g_ref, m_ref, v_ref, d_ref, mo_ref, vo_ref):
        gg = g_ref[...]
        mn = ADAM_B1 * m_ref[...] + (1.0 - ADAM_B1) * gg
        vn = ADAM_B2 * v_ref[...] + (1.0 - ADAM_B2) * (gg * gg)
        d_ref[...] = -ADAM_LR * ((mn * c1) / (jnp.sqrt(vn * c2) + ADAM_EPS) + ADAM_WD * w_ref[...])
        mo_ref[...] = mn
        vo_ref[...] = vn

    spec = pl.BlockSpec((tr, cols), lambda i: (i, 0))
    sds = jax.ShapeDtypeStruct((r, cols), F32)
    d, mn, vn = pl.pallas_call(body, grid=(r // tr,), in_specs=[spec] * 4, out_specs=[spec] * 3, out_shape=[sds] * 3,
                               name=name, compiler_params=_cparams("parallel"))(w2, g2, m2, v2)
    return d.reshape(shape), mn.reshape(shape), vn.reshape(shape)


def _my_pos():
    return lax.axis_index("x"), lax.axis_index("y"), lax.axis_index("c")


def _flip(pos, f):
    return tuple((1 - p) if b else p for p, b in zip(pos, f))


def _lin(pos):
    return 4 * pos[0] + 2 * pos[1] + pos[2]


def _remote_copies(src_ref, out_ref, send_sems, recv_sems, plan):
    me = _my_pos()
    copies = []
    for k, (f, sfn, dfn) in enumerate(plan):
        peer = _flip(me, f)
        copies.append(pltpu.make_async_remote_copy(
            src_ref=src_ref.at[sfn(me, peer)], dst_ref=out_ref.at[dfn(me, peer)], send_sem=send_sems.at[k],
            recv_sem=recv_sems.at[k], device_id=peer, device_id_type=MESH))
    return copies


def xchg(src, n_out, plan, name, inplace=False):
    piece = src.shape[1:]

    def body(src_ref, out_ref, send_sems, recv_sems):
        me = _my_pos()
        copies = []
        for k, (f, sfn, dfn) in enumerate(plan):
            peer = _flip(me, f)
            s_ref = (out_ref if inplace else src_ref).at[sfn(me, peer)]
            d_ref = out_ref.at[dfn(me, peer)]
            if any(f):
                cp = pltpu.make_async_remote_copy(src_ref=s_ref, dst_ref=d_ref, send_sem=send_sems.at[k],
                                                  recv_sem=recv_sems.at[k], device_id=peer, device_id_type=MESH)
            else:
                cp = pltpu.make_async_copy(s_ref, d_ref, recv_sems.at[k])
            cp.start()
            copies.append((cp, any(f)))
        for cp, remote in copies:
            if remote:
                cp.wait_recv()
            else:
                cp.wait()
        for cp, remote in copies:
            if remote:
                cp.wait_send()

    return pl.pallas_call(
        body, in_specs=[pl.BlockSpec(memory_space=pl.ANY)], out_specs=pl.BlockSpec(memory_space=pl.ANY),
        out_shape=jax.ShapeDtypeStruct((n_out,) + piece, src.dtype),
        scratch_shapes=[pltpu.SemaphoreType.DMA((len(plan),)), pltpu.SemaphoreType.DMA((len(plan),))],
        input_output_aliases={0: 0} if inplace else {}, name=name)(src)


_CHIP_FLIPS = ((1, 0, 0), (0, 1, 0), (1, 1, 0))
_ALL_FLIPS = tuple((a, b, c) for a in (0, 1) for b in (0, 1) for c in (0, 1))[1:]


def all_to_all8(src, name):
    plan = [((0, 0, 0), lambda me, peer: _lin(me), lambda me, peer: _lin(me))]
    plan += [(f, lambda me, peer: _lin(peer), lambda me, peer: _lin(me)) for f in _ALL_FLIPS]
    return xchg(src, N_DEV, plan, name)


def all_gather8(piece, name):
    plan = [((0, 0, 0), lambda me, peer: 0, lambda me, peer: _lin(me))]
    plan += [(f, lambda me, peer: 0, lambda me, peer: _lin(me)) for f in _ALL_FLIPS]
    return xchg(piece[None], N_DEV, plan, name)


def _ag_prepare(shard):
    k, ns = shard.shape
    px, py, _ = _my_pos()
    own = shard.astype(BF16)[None]
    return lax.dynamic_update_slice(jnp.zeros((4, k, ns), BF16), own, (2 * px + py, 0, 0)).reshape(8, k // 2, ns)


def _ag_plan():
    return [(f, lambda me, peer: _lin(me), lambda me, peer: _lin(me)) for f in _CHIP_FLIPS]


def _ag_finish(buf, name):
    plan = [((0, 0, 1), lambda me, peer, f=f: _lin(_flip(me, f)), lambda me, peer, f=f: _lin(_flip(me, f)))
            for f in _CHIP_FLIPS]
    _, kh, ns = buf.shape
    return xchg(buf, 8, plan, name, inplace=True).reshape(4, 2 * kh, ns)


def gather_weight(shard, name):
    k, ns = shard.shape
    buf = _ag_prepare(shard)

    def body(in_ref, out_ref, send_sems, recv_sems):
        me = _my_pos()
        sibling = _flip(me, (0, 0, 1))
        chips = [_flip(me, f) for f in _CHIP_FLIPS]

        def copy(sem, holder, to):
            rows = out_ref.at[4 * holder[0] + 2 * holder[1] + me[2]]
            return pltpu.make_async_remote_copy(src_ref=rows, dst_ref=rows, send_sem=send_sems.at[sem],
                                                recv_sem=recv_sems.at[sem], device_id=to, device_id_type=MESH)

        first = [copy(j, me, chip) for j, chip in enumerate(chips)]
        for cp in first:
            cp.start()
        passed = [copy(3 + j, chip, sibling) for j, chip in enumerate(chips)]
        for j, chip in enumerate(chips):
            copy(j, chip, me).wait_recv()
            passed[j].start()
        for j in range(3):
            passed[j].wait_recv()
        for cp in first + passed:
            cp.wait_send()

    full = pl.pallas_call(
        body, in_specs=[pl.BlockSpec(memory_space=pl.ANY)], out_specs=pl.BlockSpec(memory_space=pl.ANY),
        out_shape=jax.ShapeDtypeStruct(buf.shape, BF16),
        scratch_shapes=[pltpu.SemaphoreType.DMA((6,)), pltpu.SemaphoreType.DMA((6,))],
        input_output_aliases={0: 0}, name=name)(buf)
    return full.reshape(4, k, ns)


_RS_SLOTS = 7


def _sum_halves(g8, l1, c_idx, name):
    _, _, r, cc = g8.shape
    tr = _row_tile(r, r, cc)

    def body(c_ref, a_ref, b_ref, o_ref):
        o_ref[...] = (a_ref[...].astype(F32) + b_ref[...].astype(F32)).astype(o_ref.dtype)

    gs = pltpu.PrefetchScalarGridSpec(
        num_scalar_prefetch=1, grid=(4, r // tr),
        in_specs=[pl.BlockSpec((None, None, tr, cc), lambda s, i, c: (s, c[0], i, 0)),
                  pl.BlockSpec((None, tr, cc), lambda s, i, c: (s, i, 0))],
        out_specs=pl.BlockSpec((None, tr, cc), lambda s, i, c: (s, i, 0)))
    return pl.pallas_call(body, grid_spec=gs, out_shape=jax.ShapeDtypeStruct((_RS_SLOTS, r, cc), BF16), name=name,
                          compiler_params=_cparams("parallel", "parallel"))(c_idx, g8, l1)


def _sum_chips(buf, sc_idx, name):
    _, r, cc = buf.shape
    tr = _row_tile(r, r, cc)

    def body(s_ref, a_ref, b0_ref, b1_ref, b2_ref, o_ref):
        o_ref[...] = ((a_ref[...].astype(F32) + b0_ref[...].astype(F32)) + b1_ref[...].astype(F32)) + b2_ref[...].astype(F32)

    gs = pltpu.PrefetchScalarGridSpec(
        num_scalar_prefetch=1, grid=(r // tr,),
        in_specs=[pl.BlockSpec((None, tr, cc), lambda i, s: (s[0], i, 0))]
        + [pl.BlockSpec((None, tr, cc), lambda i, s, j=j: (4 + j, i, 0)) for j in range(3)],
        out_specs=pl.BlockSpec((None, tr, cc), lambda i, s: (s[1], i, 0)))
    return pl.pallas_call(body, grid_spec=gs, out_shape=jax.ShapeDtypeStruct((2, r, cc), F32), name=name,
                          compiler_params=_cparams("parallel"))(sc_idx, buf, buf, buf, buf)


def _rs_begin(g4, name):
    _, k, ns = g4.shape
    c_idx = jnp.reshape(_my_pos()[2], (1,)).astype(jnp.int32)
    plan1 = [((0, 0, 1), lambda me, peer, s=s: 2 * s + peer[2], lambda me, peer, s=s: s) for s in range(4)]
    l1 = xchg(g4.reshape(8, k // 2, ns), 4, plan1, name + "_rs_d2d")
    buf = _sum_halves(g4.reshape(4, 2, k // 2, ns), l1, c_idx, name + "_rs_sum2")
    plan2 = [(f, lambda me, peer: 2 * peer[0] + peer[1], lambda me, peer, j=j: 4 + j) for j, f in enumerate(_CHIP_FLIPS)]
    return buf, plan2


def _rs_finish(buf, name):
    _, kh, ns = buf.shape
    x, y, c = _my_pos()
    sc_idx = jnp.stack([2 * x + y, c]).astype(jnp.int32)
    halves = _sum_chips(buf, sc_idx, name + "_rs_sum4")
    plan3 = [((0, 0, 1), lambda me, peer: me[2], lambda me, peer: me[2])]
    return xchg(halves, 2, plan3, name + "_rs_swap", inplace=True).reshape(2 * kh, ns)


def reduce_scatter_weight(g4, name):
    buf, plan = _rs_begin(g4, name)
    return _rs_finish(xchg(buf, _RS_SLOTS, plan, name + "_rs_ici", inplace=True), name)


def _sum8(a8, name):
    _, r, cc = a8.shape
    tr = _row_tile(r, r, cc)

    def body(a_ref, o_ref):
        acc = a_ref[0]
        for j in range(1, N_DEV):
            acc = acc + a_ref[j]
        o_ref[...] = acc

    return pl.pallas_call(body, grid=(r // tr,), in_specs=[pl.BlockSpec((N_DEV, tr, cc), lambda i: (0, i, 0))],
                          out_specs=pl.BlockSpec((tr, cc), lambda i: (i, 0)), out_shape=jax.ShapeDtypeStruct((r, cc), F32),
                          name=name, compiler_params=_cparams("parallel"))(a8)


def all_reduce8(flat, name):
    n = flat.shape[0]
    unit = N_DEV * 256 * 128
    npad = -(-n // unit) * unit
    a = jnp.pad(flat, (0, npad - n)).reshape(N_DEV, npad // (N_DEV * 128), 128)
    mine = _sum8(all_to_all8(a, name + "_rs"), name + "_sum")
    return all_gather8(mine, name + "_ag").reshape(npad)[:n]


def _make_split(t, cuts):
    def pieces(qkv):
        out = []
        for rows in (slice(None, t), slice(t, None)):
            out += [qkv[rows, a:b] for a, b in zip(cuts[:-1], cuts[1:])]
        return tuple(out)

    @jax.custom_vjp
    def split(qkv):
        return pieces(qkv)

    def fwd(qkv):
        return pieces(qkv), None

    def bwd(_, g):
        n = len(cuts) - 1
        return (jnp.concatenate([jnp.concatenate(g[:n], axis=1), jnp.concatenate(g[n:], axis=1)], axis=0),)

    split.defvjp(fwd, bwd)
    return split


def _local_loss(x, ctx, target, mods, small, big, pending, shards, dims):
    t, m, d = dims["t"], dims["m"], dims["d"]
    a_w, bq_w, bkv_w = dims["a_w"], dims["bq_w"], dims["bkv_w"]
    z = jnp.concatenate([x, ctx], axis=0)
    big = dict(big)

    def grp(layer, j, n_groups=2):
        return mods[layer, :n_groups, j][:, None, :]

    def lin(kind, out_dtype, name, a, wname, gather=()):
        y, gathered = make_linear(kind, out_dtype, name)(a, big[wname], shards[wname], tuple(pending[n] for n in gather))
        big.update(zip(gather, gathered))
        return y

    def ffn(layer, a, gather_in, gather_out):
        n1, n3, n2 = f"ffn_w1_{layer}", f"ffn_w3_{layer}", f"ffn_w2_{layer}"
        gathering = (tuple(pending[n] for n in gather_in), tuple(pending[n] for n in gather_out))
        w2 = big[n2] if n2 in big else pending[n2]
        y, (got_in, got_out) = make_ffn(f"ffn{layer}", n2 not in big)(a, big[n1], big[n3], w2, shards[n1], shards[n3],
                                                                    shards[n2], gathering)
        big.update(zip(gather_in, got_in))
        big.update(zip(gather_out, got_out))
        return y

    h = make_norm_mod(t, "norm_mix0")(z, small["norm_mix"][0][None], grp(0, 1), grp(0, 0))
    qkv = lin("col", BF16, "attn_in", h, "attn_w_in", ("attn_w_out", "ffn_w1_0"))
    o3, o5 = 3 * a_w, 3 * a_w + bq_w + bkv_w
    cuts = (0, a_w, 2 * a_w, o3, o3 + bq_w, o5, o5 + bkv_w)
    qa, ka, va, qb_u, kb_u, vb, qa_c, ka_c, va_c, qb_c, kb_c, vb_c = _make_split(t, cuts)(qkv)
    qb, kb = make_rope(t, "rope_q")(qb_u), make_rope(t, "rope_k")(kb_u)
    sink = small["attn_sink"][0]
    no_sink = jnp.zeros((a_w // HEAD_DIM,), F32)
    na_meta, na_span, na_cases = _na_plan(t)
    oa = make_attention(na_meta, 1, na_span, True, False, True, "na")(
        qa, ka, va, ka_c, va_c, _na_bias(small["attn_rpb"][0], na_cases), no_sink)
    sw_meta, sw_span, sw_bias = _sw_plan(t)
    grp_b = bq_w // bkv_w
    ob = make_attention(sw_meta, grp_b, sw_span, True, True, False, "swa")(qb, kb, vb, kb_c, vb_c, jnp.asarray(sw_bias), sink)
    c_meta = np.array([[0] * (m // ATTN_BLOCK), [0] * (m // ATTN_BLOCK), [1] + [0] * (m // ATTN_BLOCK - 1)], np.int32)
    zero_bias = jnp.zeros((1, 1, ATTN_BLOCK, m), F32)
    oa_c = make_attention(c_meta, 1, m, False, False, False, "ctx_na")(qa_c, ka_c, va_c, ka_c, va_c, zero_bias, no_sink)
    ob_c = make_attention(c_meta, grp_b, m, False, True, False, "ctx_swa")(qb_c, kb_c, vb_c, kb_c, vb_c, zero_bias, sink)
    o = jnp.concatenate([jnp.concatenate([oa, ob], axis=1), jnp.concatenate([oa_c, ob_c], axis=1)], axis=0)
    y = lin("row", F32, "attn_out", o, "attn_w_out", ("ffn_w3_0",))
    z, h = make_residual_norm(t, "mix0_to_ffn0")(z, y, grp(0, 2), small["norm_ffn"][0][None], grp(0, 4), grp(0, 3))
    y = ffn(0, h, ("ssm_w_glu", "ffn_w1_1"), ("ffn_w3_1",))

    z, h = make_residual_norm(t, "ffn0_to_mix1")(z, y, grp(0, 5), small["norm_mix"][1][None], grp(1, 1), grp(1, 0))
    hx, hc = h[:t], h[t:]
    lam, wd, wr = _s5_params(small["ssm_a_re"][0], small["ssm_a_im"][0], small["ssm_log_dt"][0], small["ssm_b_re"][0],
                             small["ssm_b_im"][0], small["ssm_c_re"][0], small["ssm_c_im"][0])
    u2 = jnp.stack([_to_scan_order(jnp.concatenate([hc, hx], axis=0)), _to_scan_order(h)])
    y2 = make_s5_core("s5")(u2, lam, wd, wr)
    ys = _from_scan_order(y2[0])[m:] + _from_scan_order(y2[1])[:t]
    gl = make_gelu_in("gelu")(hx, ys, small["ssm_d_full"][None])
    zz = lin("col", F32, "glu_w", gl, "ssm_w_glu", ("ffn_w2_1",))
    yx = make_glu("glu")(zz, small["ssm_b_glu_full"][None])
    xs, h = make_residual_norm(t, "mix1_to_ffn1")(z[:t], yx, grp(1, 2, 1), small["norm_ffn"][1][None], grp(1, 4, 1), grp(1, 3, 1))
    xs = make_gated_residual(t, "res_ffn1")(xs, ffn(1, h, (), ()), grp(1, 5, 1))
    return make_final_loss("loss_head")(xs, small["norm_final"][None], target)[0, 0]


_WEIGHTS = ['c_ctx', 'ada_w', 'ada_b', 'norm_mix', 'norm_ffn', 'ffn_w1', 'ffn_w3', 'ffn_w2', 'attn_w_in', 'attn_w_out',
            'attn_rpb', 'attn_sink', 'ssm_a_re', 'ssm_a_im', 'ssm_log_dt', 'ssm_b_re', 'ssm_b_im', 'ssm_c_re', 'ssm_c_im',
            'ssm_d', 'ssm_w_glu', 'ssm_b_glu', 'norm_final']
_LOCAL_SMALL = ['norm_mix', 'norm_ffn', 'attn_rpb', 'attn_sink', 'ssm_a_re', 'ssm_a_im', 'ssm_log_dt', 'ssm_b_re',
                'ssm_b_im', 'ssm_c_re', 'ssm_c_im', 'norm_final']
_MOD_ROWS = 16


def _gather_chip_vector(v, name):
    g = all_gather8(v[None], name)
    return g[0::2, 0, :].reshape(-1)


def kernel(x, c, ctx, c_ctx, ada_w, ada_b, norm_mix, norm_ffn, ffn_w1, ffn_w3, ffn_w2, attn_w_in, attn_w_out, attn_rpb, attn_sink, ssm_a_re, ssm_a_im, ssm_log_dt, ssm_b_re, ssm_b_im, ssm_c_re, ssm_c_im, ssm_d, ssm_w_glu, ssm_b_glu, norm_final, loss_target, m_c_ctx, m_ada_w, m_ada_b, m_norm_mix, m_norm_ffn, m_ffn_w1, m_ffn_w3, m_ffn_w2, m_attn_w_in, m_attn_w_out, m_attn_rpb, m_attn_sink, m_ssm_a_re, m_ssm_a_im, m_ssm_log_dt, m_ssm_b_re, m_ssm_b_im, m_ssm_c_re, m_ssm_c_im, m_ssm_d, m_ssm_w_glu, m_ssm_b_glu, m_norm_final, v_c_ctx, v_ada_w, v_ada_b, v_norm_mix, v_norm_ffn, v_ffn_w1, v_ffn_w3, v_ffn_w2, v_attn_w_in, v_attn_w_out, v_attn_rpb, v_attn_sink, v_ssm_a_re, v_ssm_a_im, v_ssm_log_dt, v_ssm_b_re, v_ssm_b_im, v_ssm_c_re, v_ssm_c_im, v_ssm_d, v_ssm_w_glu, v_ssm_b_glu, v_norm_final):
    env = dict(locals())
    w = {n: env[n] for n in _WEIGHTS}
    mom = {n: env["m_" + n] for n in _WEIGHTS}
    var = {n: env["v_" + n] for n in _WEIGHTS}
    _, t, d = x.shape
    m = ctx.shape[1]
    px, py, pc = _my_pos()
    s_me = 2 * px + py
    a_w =attn_rpb.shape[1] * HEAD_DIM
    bq_w = attn_sink.shape[1] * HEAD_DIM
    bkv_w = (4 * attn_w_in.shape[2] - 3 * a_w - bq_w) // 2
    dims = dict(t=t, m=m, d=d, a_w=a_w, bq_w=bq_w, bkv_w=bkv_w)
    n_layers = ada_w.shape[0]
    ada_cols = ada_w.shape[2]

    big = {"attn_w_in": gather_weight(attn_w_in[0], "ag_attn_in")}
    pending = {"attn_w_out": _ag_prepare(attn_w_out[0]), "ssm_w_glu": _ag_prepare(ssm_w_glu[0])}
    for l in range(n_layers):
        pending.update({f"ffn_w1_{l}": _ag_prepare(ffn_w1[l]), f"ffn_w3_{l}": _ag_prepare(ffn_w3[l]),
                        f"ffn_w2_{l}": _ag_prepare(ffn_w2[l])})
    small ={n: w[n] for n in _LOCAL_SMALL}
    small["ssm_d_full"] = _gather_chip_vector(ssm_d[0], "ag_ssm_d")
    small["ssm_b_glu_full"] = _gather_chip_vector(ssm_b_glu[0], "ag_b_glu")

    c_all = all_gather8(c, "ag_c")[:, 0, :]
    cond = jnp.concatenate([c_all, c_ctx[None], jnp.zeros((_MOD_ROWS - N_DEV - 1, d), F32)], axis=0)
    sig = jax.nn.sigmoid(cond)
    silu_c = (cond * sig).astype(BF16)
    mods_shard = mm_nn(silu_c, ada_w, "col", F32, "ada_fwd").reshape(_MOD_ROWS, n_layers, ada_cols).transpose(1, 0, 2)
    send = jnp.stack([jnp.stack([mods_shard[:, tgt], mods_shard[:, N_DEV]], axis=1).reshape(2 * n_layers, ada_cols)
                      for tgt in range(N_DEV)])
    plan = [((0, 0, 0), lambda me, peer: _lin(me), lambda me, peer: 2 * me[0] + me[1])]
    plan += [(f, lambda me, peer: _lin(peer), lambda me, peer: 2 * me[0] + me[1]) for f in _CHIP_FLIPS]
    got = xchg(send, 4, plan, "mods_xchg")
    mods = got.reshape(4, n_layers, 2, ada_cols).transpose(1, 2, 0, 3).reshape(n_layers, 2, 4 * ada_cols)
    mods = (mods + ada_b[:, None, :]).reshape(n_layers, 2, 6, d)

    shards = {"attn_w_in": attn_w_in[0], "attn_w_out": attn_w_out[0], "ssm_w_glu": ssm_w_glu[0]}
    for l in range(n_layers):
        shards.update({f"ffn_w1_{l}": ffn_w1[l], f"ffn_w3_{l}": ffn_w3[l], f"ffn_w2_{l}": ffn_w2[l]})

    def local(xx, mods_, small_, shards_):
        return _local_loss(xx, ctx[0], loss_target[0], mods_, small_, big, pending, shards_, dims)

    loss_local, vjp = jax.vjp(local, x[0], mods, small, shards)
    g_x, g_mods, g_small, g_shards = vjp(jnp.ones((), F32))
    loss = lax.psum(loss_local, ("x", "y", "c"))
    grads = {"attn_w_in": g_shards["attn_w_in"][None], "attn_w_out": g_shards["attn_w_out"][None],
             "ssm_w_glu": g_shards["ssm_w_glu"][None]}
    for n in ("ffn_w1", "ffn_w3", "ffn_w2"):
        grads[n] = jnp.stack([g_shards[f"{n}_{l}"] for l in range(n_layers)])

    gm = all_gather8(g_mods.reshape(2 * n_layers, 6 * d), "ag_dmods").reshape(N_DEV, n_layers, 2, 6 * d)
    ctx_row = gm[0, :, 1]
    for j in range(1, N_DEV):
        ctx_row = ctx_row + gm[j, :, 1]
    dm16 = jnp.concatenate([gm[:, :, 0].transpose(1, 0, 2), ctx_row[:, None], jnp.zeros((n_layers, _MOD_ROWS - N_DEV - 1, 6 * d), F32)], axis=1)
    grads["ada_b"] = jnp.sum(dm16, axis=1)
    dm_mine = lax.dynamic_slice_in_dim(dm16, s_me * ada_cols, ada_cols, axis=2).astype(BF16)
    grads["ada_w"] = jnp.stack([mm_tn(silu_c, dm_mine[l], (1, d, ada_cols), "col", F32, f"ada_dw{l}")[0] for l in range(n_layers)])
    dsilu = mm_nt(dm_mine.transpose(1, 0, 2).reshape(_MOD_ROWS, n_layers * ada_cols), ada_w, "col", F32, "ada_dc")
    dsilu_ctx = 0.5 * dsilu[N_DEV]

    packed = [(n, g_small[n]) for n in _LOCAL_SMALL] + [("ssm_d", g_small["ssm_d_full"]), ("ssm_b_glu", g_small["ssm_b_glu_full"]),
                                                        ("c_ctx", dsilu_ctx)]
    flat = all_reduce8(jnp.concatenate([a.reshape(-1) for _, a in packed]), "ar_small")
    off = 0
    for n, a in packed:
        grads[n] = flat[off:off + a.size].reshape(a.shape)
        off += a.size
    sig_ctx = jax.nn.sigmoid(c_ctx)
    grads["c_ctx"] = grads["c_ctx"] * (sig_ctx * (1.0 + c_ctx * (1.0 - sig_ctx)))
    grads["ssm_d"] = lax.dynamic_slice_in_dim(grads["ssm_d"], s_me * ssm_d.shape[1], ssm_d.shape[1])[None]
    grads["ssm_b_glu"] = lax.dynamic_slice_in_dim(grads["ssm_b_glu"], s_me * ssm_b_glu.shape[1], ssm_b_glu.shape[1])[None]

    delta, new_m, new_v = {}, {}, {}
    for n in _WEIGHTS:
        delta[n], new_m[n], new_v[n] = adamw(w[n], grads[n], mom[n], var[n], "adamw_" + n)
    return (loss, g_x[None], *[grads[n] for n in _WEIGHTS], *[delta[n] for n in _WEIGHTS],
            *[new_m[n] for n in _WEIGHTS], *[new_v[n] for n in _WEIGHTS])
```

```python
import functools
import math

import numpy as np
import jax
import jax.numpy as jnp
from jax import lax
from jax.experimental import pallas as pl
from jax.experimental.pallas import tpu as pltpu

F32 = jnp.float32
BF16 = jnp.bfloat16
MESH = pl.DeviceIdType.MESH

HEAD_DIM = 128
GRID_W = 64
NA_ROWS = 8
NA_COLS = 16
SW_RADIUS = 128
ATTN_BLOCK = 128
ROPE_BASE = 10000.0
SSM_GROUP = 16
SSM_STATE = 64
SSM_TILE_GROUPS = 8
SCAN_BLOCKS = 8
EPS = 1e-6
NEG_INF = -1e30
ADAM_LR, ADAM_B1, ADAM_B2, ADAM_EPS, ADAM_WD, ADAM_STEP = 0.001, 0.9, 0.999, 1e-08, 0.01, 10
VMEM_LIMIT_BYTES = 56 * 1024 * 1024
N_DEV = 8


def _cparams(*sem):
    return pltpu.CompilerParams(dimension_semantics=tuple(sem) if sem else None, vmem_limit_bytes=VMEM_LIMIT_BYTES)


def _pick(n, cands):
    for c in cands:
        if n % c == 0:
            return c
    return n


def _dot_nn(a, b):
    return jnp.dot(a, b, preferred_element_type=F32)


def _dot_nt(a, b):
    return lax.dot_general(a, b, (((1,), (1,)), ((), ())), preferred_element_type=F32)


def _dot_tn(a, b):
    return lax.dot_general(a, b, (((0,), (0,)), ((), ())), preferred_element_type=F32)


def _mm_call(name, grid, ins, in_specs, o_specs, out_sds, acc_shape, step, carry=None):
    nk = grid[2]
    nb = 0 if carry is None else len(carry)
    ni, no = len(ins), len(out_sds)
    assert nk == 1 or no == 1

    def body(*refs):
        in_refs = refs[:ni]
        o_refs = refs[ni + nb:ni + nb + no]
        rest = refs[ni + nb + no + nb:]
        acc_ref = rest[0] if nk > 1 else None
        o_ref = o_refs[0]
        if carry is not None:
            bufs = refs[ni + nb + no:ni + nb + no + nb]
            sems = rest[-2 * nb:]
            ids = [pl.program_id(ax) for ax in range(3)]
            first = (ids[0] == 0) & (ids[1] == 0) & (ids[2] == 0)
            last = (ids[0] == grid[0] - 1) & (ids[1] == grid[1] - 1) & (ids[2] == grid[2] - 1)
            copies = []
            for q, (_, plan) in enumerate(carry):
                copies += _remote_copies(bufs[q], bufs[q], sems[2 * q], sems[2 * q + 1], plan)

            @pl.when(first)
            def _():
                for cp in copies:
                    cp.start()

        kk = pl.program_id(2)
        if nk == 1:
            for ref, val in zip(o_refs, step(*in_refs)):
                ref[...] = val.astype(ref.dtype)
        else:
            @pl.when(kk == 0)
            def _():
                acc_ref[...] = step(*in_refs)[0]

            @pl.when(kk > 0)
            def _():
                acc_ref[...] = step(*in_refs)[0] + acc_ref[...]

            @pl.when(kk == nk - 1)
            def _():
                o_ref[...] = acc_ref[...].astype(o_ref.dtype)

        if carry is not None:
            @pl.when(last)
            def _():
                for cp in copies:
                    cp.wait_recv()
                for cp in copies:
                    cp.wait_send()

    scratch = [pltpu.VMEM(acc_shape, F32)] if nk > 1 else []
    if carry is None:
        return pl.pallas_call(
            body, grid=grid, in_specs=list(in_specs), out_specs=list(o_specs), out_shape=list(out_sds),
            scratch_shapes=scratch, name=name, compiler_params=_cparams("parallel", "parallel", "arbitrary"))(*ins)
    any_spec = pl.BlockSpec(memory_space=pl.ANY)
    for _, plan in carry:
        scratch += [pltpu.SemaphoreType.DMA((len(plan),)), pltpu.SemaphoreType.DMA((len(plan),))]
    return pl.pallas_call(
        body, grid=grid, in_specs=list(in_specs) + [any_spec] * nb, out_specs=list(o_specs) + [any_spec] * nb,
        out_shape=list(out_sds) + [jax.ShapeDtypeStruct(buf.shape, buf.dtype) for buf, _ in carry],
        scratch_shapes=scratch, input_output_aliases={ni + q: no + q for q in range(nb)}, name=name,
        compiler_params=_cparams("arbitrary", "arbitrary", "arbitrary"))(*ins, *[buf for buf, _ in carry])


_ROW_TILES = (768, 512, 256, 128, 64, 32, 16, 8)
_K_TILES = (2048, 1408, 1024, 512, 256, 128)
_CONTRACT_TILES = (1408, 1024, 768, 512, 256, 128)
_WEIGHT_BLOCK_BYTES = 12 * 1024 * 1024
_FUSED_ROW_TILES =(528, 512, 384, 256, 128, 64, 32, 16, 8)


def mm_nn(a, w3, kind, out_dtype, name, carry=None):
    r = a.shape[0]
    s, d1, d2 = w3.shape
    tm = _pick(r, _ROW_TILES)
    if kind == "col":
        tn = d2 if d1 * d2 * w3.dtype.itemsize <= _WEIGHT_BLOCK_BYTES else _pick(d2, (1024, 512, 256, 128))
        nn = d2 // tn
        grid = (s * nn, r // tm, 1)
        a_spec = pl.BlockSpec((tm, d1), lambda j, i, k: (i, 0))
        b_spec = pl.BlockSpec((None, d1, tn), lambda j, i, k: (j // nn, 0, j % nn))
        o_spec = pl.BlockSpec((tm, tn), lambda j, i, k: (i, j))
        n = s * d2

        def step(a_ref, b_ref):
            return (_dot_nn(a_ref[...], b_ref[...].astype(a_ref.dtype)),)
    else:
        tn = d2 if d2 <= 1024 else _pick(d2, (1024, 512, 256, 128))
        grid = (d2 // tn, r // tm, 1)
        a_spec = pl.BlockSpec((tm, s * d1), lambda j, i, k: (i, 0))
        b_spec = pl.BlockSpec((s, d1, tn), lambda j, i, k: (0, 0, j))
        o_spec = pl.BlockSpec((tm, tn), lambda j, i, k: (i, j))
        n = d2

        def step(a_ref, b_ref):
            p = _dot_nn(a_ref[:, :d1], b_ref[0])
            for q in range(1, s):
                p = _dot_nn(a_ref[:, q * d1:(q + 1) * d1], b_ref[q]) + p
            return (p,)
    out = _mm_call(name, grid, [a, w3], [a_spec, b_spec], [o_spec], [jax.ShapeDtypeStruct((r, n), out_dtype)], None, step, carry)
    return out if carry else out[0]


def mm_nt(dy, w3, kind, out_dtype, name, carry=None, addend=None):
    r = dy.shape[0]
    s, d1, d2 = w3.shape
    tm = _pick(r, _ROW_TILES)
    if kind == "col":
        fits = [c for c in (1024, 512, 256, 128) if d1 % c == 0 and s * c * d2 * w3.dtype.itemsize <= _WEIGHT_BLOCK_BYTES]
        tko = d1 if d1 <= 1024 and s * d1 * d2 * w3.dtype.itemsize <= _WEIGHT_BLOCK_BYTES else fits[0]
        grid = (d1 // tko, r // tm, 1)
        a_spec = pl.BlockSpec((tm, s * d2), lambda j, i, k: (i, 0))
        b_spec = pl.BlockSpec((s, tko, d2), lambda j, i, k: (0, j, 0))
        o_spec = pl.BlockSpec((tm, tko), lambda j, i, k: (i, j))
        kdim = d1

        def step(a_ref, b_ref, *more):
            p = _dot_nt(a_ref[:, :d2], b_ref[0].astype(a_ref.dtype))
            for q in range(1, s):
                p = _dot_nt(a_ref[:, q * d2:(q + 1) * d2], b_ref[q].astype(a_ref.dtype)) + p
            return (p + more[0][...].astype(F32),) if more else (p,)
    else:
        grid = (s, r // tm, 1)
        a_spec = pl.BlockSpec((tm, d2), lambda j, i, k: (i, 0))
        b_spec = pl.BlockSpec((None, d1, d2), lambda j, i, k: (j, 0, 0))
        o_spec = pl.BlockSpec((tm, d1), lambda j, i, k: (i, j))
        kdim = s * d1

        def step(a_ref, b_ref, *more):
            p = _dot_nt(a_ref[...], b_ref[...])
            return (p + more[0][...].astype(F32),) if more else (p,)
    ins, specs = [dy, w3], [a_spec, b_spec]
    if addend is not None:
        ins, specs = ins + [addend], specs + [o_spec]
    out = _mm_call(name, grid, ins, specs, [o_spec], [jax.ShapeDtypeStruct((r, kdim), out_dtype)], None, step, carry)
    return out if carry else out[0]


def mm_tn(a, dy, w_shape, kind, out_dtype, name):
    s, d1, d2 = w_shape
    r = dy.shape[0]
    tr = _pick(r, _CONTRACT_TILES)
    if kind == "col":
        tkk = d1 if d1 <= 1024 else _pick(d1, (1024, 512, 256, 128))
        grid = (s * (d1 // tkk), 1, r // tr)
        nkk = d1 // tkk
        a_spec = pl.BlockSpec((tr, tkk), lambda j, i, k: (k, j % nkk))
        b_spec = pl.BlockSpec((tr, d2), lambda j, i, k: (k, j // nkk))
        o_spec = pl.BlockSpec((None, tkk, d2), lambda j, i, k: (j // nkk, j % nkk, 0))
        acc = (tkk, d2)
    else:
        tn = d2 if d2 <= 1024 else _pick(d2, (1024, 512, 256, 128))
        nn = d2 // tn
        grid = (s * nn, 1, r // tr)
        a_spec = pl.BlockSpec((tr, d1), lambda j, i, k: (k, j // nn))
        b_spec = pl.BlockSpec((tr, tn), lambda j, i, k: (k, j % nn))
        o_spec = pl.BlockSpec((None, d1, tn), lambda j, i, k: (j // nn, 0, j % nn))
        acc = (d1, tn)

    def step(a_ref, b_ref):
        return (_dot_tn(a_ref[...], b_ref[...]),)
    return _mm_call(name, grid, [a, dy], [a_spec, b_spec], [o_spec], [jax.ShapeDtypeStruct(w_shape, out_dtype)], acc, step)[0]


def mm_ffn_in(h, w1, w3, name, carry=None):
    r = h.shape[0]
    s, d1, d2 = w1.shape
    tm = _pick(r, _FUSED_ROW_TILES)
    a_spec = pl.BlockSpec((tm, d1), lambda j, i, k: (i, 0))
    b_spec = pl.BlockSpec((None, d1, d2), lambda j, i, k: (j, 0, 0))
    o_spec = pl.BlockSpec((tm, d2), lambda j, i, k: (i, j))
    sds = jax.ShapeDtypeStruct((r, s * d2), BF16)

    def step(a_ref, b1_ref, b3_ref):
        p1 = _dot_nn(a_ref[...], b1_ref[...])
        p3 = _dot_nn(a_ref[...], b3_ref[...])
        return p1, p3, _silu(p1) * p3

    return _mm_call(name, (s, r // tm, 1), [h, w1, w3], [a_spec, b_spec, b_spec], [o_spec] * 3, [sds] * 3, None, step, carry)


def mm_ffn_back(dy, w2, h1, h3, name, carry=None):
    r = dy.shape[0]
    s, d1, d2 = w2.shape
    tm = _pick(r, _FUSED_ROW_TILES)
    a_spec = pl.BlockSpec((tm, d2), lambda j, i, k: (i, 0))
    b_spec = pl.BlockSpec((None, d1, d2), lambda j, i, k: (j, 0, 0))
    o_spec = pl.BlockSpec((tm, d1), lambda j, i, k: (i, j))
    sds = jax.ShapeDtypeStruct((r, s * d1), BF16)

    def step(a_ref, b_ref, h1_ref, h3_ref):
        g = _dot_nt(a_ref[...], b_ref[...])
        a1 = h1_ref[...].astype(F32)
        sg = jax.nn.sigmoid(a1)
        return g * h3_ref[...].astype(F32) * (sg * (1.0 + a1 * (1.0 - sg))), g * a1 * sg

    return _mm_call(name, (s, r // tm, 1), [dy, w2, h1, h3], [a_spec, b_spec, o_spec, o_spec], [o_spec] * 2, [sds] * 2,
                    None, step, carry)


def make_linear(kind, out_dtype, name):
    def run(a, w3, gathering):
        carry = [(buf, _ag_plan()) for buf in gathering] or None
        out = mm_nn(a, w3, kind, out_dtype, name + "_fwd", carry)
        if not carry:
            return out, ()
        return out[0], _ag_finish_all(out[1:], name)

    @jax.custom_vjp
    def linear(a, w3, w_shard, gathering):
        return run(a, w3, gathering)

    def fwd(a, w3, w_shard, gathering):
        return run(a, w3, gathering), (a, w3, w_shard.shape, len(gathering))

    def bwd(res, cts):
        a, w3, shard_shape, n_gathering = res
        dyb = cts[0].astype(BF16)
        dw = mm_tn(a, dyb, w3.shape, kind, BF16, name + "_dw")
        buf, plan = _rs_begin(dw, name)
        da, buf = mm_nt(dyb, w3, kind, a.dtype, name + "_dx", carry=[(buf, plan)])
        return da, None, _rs_finish(buf, name).reshape(shard_shape), (None,) * n_gathering

    linear.defvjp(fwd, bwd)
    return linear


def _ag_finish_all(bufs, name):
    return tuple(_ag_finish(buf, f"{name}_gathered{q}") for q, buf in enumerate(bufs))


def make_ffn(name, w2_pending):
    def run(h, w1, w3, w2, gathering):
        g_in, g_out = gathering
        if w2_pending:
            g_in = (w2,) + tuple(g_in)
        carry = [(buf, _ag_plan()) for buf in g_in] or None
        out = mm_ffn_in(h, w1, w3, name + "_in", carry)
        h1, h3, act = out[:3]
        done_in = _ag_finish_all(out[3:], name + "_in")
        if w2_pending:
            w2, done_in = done_in[0], done_in[1:]
        carry2 = [(buf, _ag_plan()) for buf in g_out] or None
        y = mm_nn(act, w2, "row", F32, name + "_out", carry2)
        done = (done_in, _ag_finish_all(y[1:], name + "_out") if carry2 else ())
        return (y[0] if carry2 else y), done, (h1, h3, act, w2)

    @jax.custom_vjp
    def ffn(h, w1, w3, w2, s1, s3, s2, gathering):
        return run(h, w1, w3, w2, gathering)[:2]

    def fwd(h, w1, w3, w2, s1, s3, s2, gathering):
        y, done, (h1, h3, act, w2) = run(h, w1, w3, w2, gathering)
        return (y, done), (h, w1, w3, w2, h1, h3, act, s1.shape, s3.shape, s2.shape, tuple(len(g) for g in gathering))

    def bwd(res, cts):
        h, w1, w3, w2, h1, h3, act, shape1, shape3, shape2, n_gathering = res
        dyb = cts[0].astype(BF16)
        buf2, plan2 = _rs_begin(mm_tn(act, dyb, w2.shape, "row", BF16, name + "_dw2"), name + "_w2")
        dh1, dh3, buf2 = mm_ffn_back(dyb, w2, h1, h3, name + "_back", carry=[(buf2, plan2)])
        buf1, plan1 = _rs_begin(mm_tn(h, dh1, w1.shape, "col", BF16, name + "_dw1"), name + "_w1")
        buf3, plan3 = _rs_begin(mm_tn(h, dh3, w3.shape, "col", BF16, name + "_dw3"), name + "_w3")
        dh, buf1 = mm_nt(dh1, w1, "col", h.dtype, name + "_dx1", carry=[(buf1, plan1)])
        dh, buf3 = mm_nt(dh3, w3, "col", h.dtype, name + "_dx3", carry=[(buf3, plan3)], addend=dh)
        grads = [_rs_finish(b, name + n).reshape(sh) for b, n, sh in
                 ((buf1, "_w1", shape1), (buf3, "_w3", shape3), (buf2, "_w2", shape2))]
        return (dh, None, None, None, *grads, tuple((None,) * n for n in n_gathering))

    ffn.defvjp(fwd, bwd)
    return ffn


def _row_tile(r, t0, d):
    cap = max(8, (2 * 1024 * 1024) // (4 * d))
    cands = [t for t in (1024, 512, 256, 128, 64, 32, 16, 8) if t <= cap]
    for t in cands:
        if r % t == 0 and t0 % t == 0:
            return t
    raise ValueError("no row tile")


def _grp_spec(d, nb0):
    return pl.BlockSpec((None, 1, d), lambda i: (i // nb0, 0, 0))


def _norm_mod_fwd(z, g, scale, shift, t0, name):
    r, d = z.shape
    tr = _row_tile(r, t0, d)
    nb0 = t0 // tr

    def body(z_ref, g_ref, sc_ref, sh_ref, o_ref):
        zz = z_ref[...]
        rstd = lax.rsqrt(jnp.mean(zz * zz, axis=-1, keepdims=True) + EPS)
        y = zz * rstd * g_ref[...]
        o_ref[...] = (y * (1.0 + sc_ref[...]) + sh_ref[...]).astype(o_ref.dtype)

    return pl.pallas_call(
        body, grid=(r // tr,),
        in_specs=[pl.BlockSpec((tr, d), lambda i: (i, 0)), pl.BlockSpec((1, d), lambda i: (0, 0)),
                  _grp_spec(d, nb0), _grp_spec(d, nb0)],
        out_specs=pl.BlockSpec((tr, d), lambda i: (i, 0)),
        out_shape=jax.ShapeDtypeStruct((r, d), BF16), name=name, compiler_params=_cparams("parallel"))(z, g, scale, shift)


def _norm_mod_bwd(z, g, scale, dh, t0, name):
    r, d = z.shape
    ng = scale.shape[0]
    tr = _row_tile(r, t0, d)
    nb0 = t0 // tr

    def body(z_ref, g_ref, sc_ref, dh_ref, dz_ref, dg_ref, dsc_ref, dsh_ref):
        i = pl.program_id(0)
        zz = z_ref[...]
        gg = g_ref[...]
        rstd = lax.rsqrt(jnp.mean(zz * zz, axis=-1, keepdims=True) + EPS)
        zhat = zz * rstd
        dhh = dh_ref[...].astype(F32)
        dy = dhh * (1.0 + sc_ref[...])
        dyg = dy * gg
        dz_ref[...] = rstd * (dyg - zhat * jnp.mean(dyg * zhat, axis=-1, keepdims=True))

        @pl.when(i == 0)
        def _():
            dg_ref[...] = jnp.zeros_like(dg_ref)

        @pl.when((i == 0) | (i == nb0))
        def _():
            dsc_ref[...] = jnp.zeros_like(dsc_ref)
            dsh_ref[...] = jnp.zeros_like(dsh_ref)

        dg_ref[...] += jnp.sum(dy * zhat, axis=0, keepdims=True)
        dsc_ref[...] += jnp.sum(dhh * (zhat * gg), axis=0, keepdims=True)
        dsh_ref[...] += jnp.sum(dhh, axis=0, keepdims=True)

    return pl.pallas_call(
        body, grid=(r // tr,),
        in_specs=[pl.BlockSpec((tr, d), lambda i: (i, 0)), pl.BlockSpec((1, d), lambda i: (0, 0)),
                  _grp_spec(d, nb0), pl.BlockSpec((tr, d), lambda i: (i, 0))],
        out_specs=[pl.BlockSpec((tr, d), lambda i: (i, 0)), pl.BlockSpec((1, d), lambda i: (0, 0)),
                   _grp_spec(d, nb0), _grp_spec(d, nb0)],
        out_shape=[jax.ShapeDtypeStruct((r, d), F32), jax.ShapeDtypeStruct((1, d), F32),
                   jax.ShapeDtypeStruct((ng, 1, d), F32), jax.ShapeDtypeStruct((ng, 1, d), F32)],
        name=name, compiler_params=_cparams("arbitrary"))(z, g, scale, dh)


def make_norm_mod(t0, name):
    @jax.custom_vjp
    def f(z, g, scale, shift):
        return _norm_mod_fwd(z, g, scale, shift, t0, name + "_fwd")

    def fwd(z, g, scale, shift):
        return _norm_mod_fwd(z, g, scale, shift, t0, name + "_fwd"), (z, g, scale)

    def bwd(res, dh):
        z, g, scale = res
        dz, dg, dsc, dsh = _norm_mod_bwd(z, g, scale, dh, t0, name + "_bwd")
        return dz, dg, dsc, dsh

    f.defvjp(fwd, bwd)
    return f


def _gated_fwd(z, y, gate, t0, name):
    r, d = z.shape
    tr = _row_tile(r, t0, d)
    nb0 = t0 // tr

    def body(z_ref, y_ref, g_ref, o_ref):
        o_ref[...] = z_ref[...] + g_ref[...] * y_ref[...].astype(F32)

    return pl.pallas_call(
        body, grid=(r // tr,),
        in_specs=[pl.BlockSpec((tr, d), lambda i: (i, 0)), pl.BlockSpec((tr, d), lambda i: (i, 0)), _grp_spec(d, nb0)],
        out_specs=pl.BlockSpec((tr, d), lambda i: (i, 0)),
        out_shape=jax.ShapeDtypeStruct((r, d), F32), name=name, compiler_params=_cparams("parallel"))(z, y, gate)


def _gated_bwd(y, gate, dzn, t0, name):
    r, d = y.shape
    ng = gate.shape[0]
    tr = _row_tile(r, t0, d)
    nb0 = t0 // tr

    def body(y_ref, g_ref, dz_ref, dy_ref, dg_ref):
        i = pl.program_id(0)
        dzz = dz_ref[...]
        dy_ref[...] = (g_ref[...] * dzz).astype(dy_ref.dtype)

        @pl.when((i == 0) | (i == nb0))
        def _():
            dg_ref[...] = jnp.zeros_like(dg_ref)

        dg_ref[...] += jnp.sum(dzz * y_ref[...].astype(F32), axis=0, keepdims=True)

    return pl.pallas_call(
        body, grid=(r // tr,),
        in_specs=[pl.BlockSpec((tr, d), lambda i: (i, 0)), _grp_spec(d, nb0), pl.BlockSpec((tr, d), lambda i: (i, 0))],
        out_specs=[pl.BlockSpec((tr, d), lambda i: (i, 0)), _grp_spec(d, nb0)],
        out_shape=[jax.ShapeDtypeStruct((r, d), y.dtype), jax.ShapeDtypeStruct((ng, 1, d), F32)],
        name=name, compiler_params=_cparams("arbitrary"))(y, gate, dzn)


def make_gated_residual(t0, name):
    @jax.custom_vjp
    def f(z, y, gate):
        return _gated_fwd(z, y, gate, t0, name + "_fwd")

    def fwd(z, y, gate):
        return _gated_fwd(z, y, gate, t0, name + "_fwd"), (y, gate)

    def bwd(res, dzn):
        y, gate = res
        dy, dgate = _gated_bwd(y, gate, dzn, t0, name + "_bwd")
        return dzn, dy, dgate

    f.defvjp(fwd, bwd)
    return f


def make_residual_norm(t0, name):
    def row_spec(tr, d):
        return pl.BlockSpec((tr, d), lambda i: (i, 0))

    def fwd_call(z, y, gate, g, scale, shift):
        r, d = z.shape
        tr = _row_tile(r, t0, d)
        nb0 = t0 // tr

        def body(z_ref, y_ref, gt_ref, g_ref, sc_ref, sh_ref, zn_ref, h_ref):
            zz = z_ref[...] + gt_ref[...] * y_ref[...].astype(F32)
            zn_ref[...] = zz
            rstd = lax.rsqrt(jnp.mean(zz * zz, axis=-1, keepdims=True) + EPS)
            h_ref[...] = (zz * rstd * g_ref[...] * (1.0 + sc_ref[...]) + sh_ref[...]).astype(h_ref.dtype)

        grp = _grp_spec(d, nb0)
        return pl.pallas_call(
            body, grid=(r // tr,),
            in_specs=[row_spec(tr, d), row_spec(tr, d), grp, pl.BlockSpec((1, d), lambda i: (0, 0)), grp, grp],
            out_specs=[row_spec(tr, d), row_spec(tr, d)],
            out_shape=[jax.ShapeDtypeStruct((r, d), F32), jax.ShapeDtypeStruct((r, d), BF16)],
            name=name + "_fwd", compiler_params=_cparams("parallel"))(z, y, gate, g, scale, shift)

    def bwd_call(zn, y, gate, g, scale, dzn, dh):
        r, d = zn.shape
        ng = scale.shape[0]
        tr = _row_tile(r, t0, d)
        nb0 = t0 // tr

        def body(zn_ref, y_ref, gt_ref, g_ref, sc_ref, dzn_ref, dh_ref, dz_ref, dy_ref, dgt_ref, dg_ref, dsc_ref, dsh_ref):
            i = pl.program_id(0)
            zz = zn_ref[...]
            gg = g_ref[...]
            rstd = lax.rsqrt(jnp.mean(zz * zz, axis=-1, keepdims=True) + EPS)
            zhat = zz * rstd
            dhh = dh_ref[...].astype(F32)
            dyn = dhh * (1.0 + sc_ref[...])
            dyg = dyn * gg
            dz = dzn_ref[...] + rstd * (dyg - zhat * jnp.mean(dyg * zhat, axis=-1, keepdims=True))
            dz_ref[...] = dz
            dy_ref[...] = (gt_ref[...] * dz).astype(dy_ref.dtype)

            @pl.when(i == 0)
            def _():
                dg_ref[...] = jnp.zeros_like(dg_ref)

            @pl.when((i == 0) | (i == nb0))
            def _():
                dgt_ref[...] = jnp.zeros_like(dgt_ref)
                dsc_ref[...] = jnp.zeros_like(dsc_ref)
                dsh_ref[...] = jnp.zeros_like(dsh_ref)

            dgt_ref[...] += jnp.sum(dz * y_ref[...].astype(F32), axis=0, keepdims=True)
            dg_ref[...] += jnp.sum(dyn * zhat, axis=0, keepdims=True)
            dsc_ref[...] += jnp.sum(dhh * (zhat * gg), axis=0, keepdims=True)
            dsh_ref[...] += jnp.sum(dhh, axis=0, keepdims=True)

        grp = _grp_spec(d, nb0)
        vec = pl.BlockSpec((1, d), lambda i: (0, 0))
        gsds = jax.ShapeDtypeStruct((ng, 1, d), F32)
        return pl.pallas_call(
            body, grid=(r // tr,),
            in_specs=[row_spec(tr, d), row_spec(tr, d), grp, vec, grp, row_spec(tr, d), row_spec(tr, d)],
            out_specs=[row_spec(tr, d), row_spec(tr, d), grp, vec, grp, grp],
            out_shape=[jax.ShapeDtypeStruct((r, d), F32), jax.ShapeDtypeStruct((r, d), y.dtype), gsds,
                       jax.ShapeDtypeStruct((1, d), F32), gsds, gsds],
            name=name + "_bwd", compiler_params=_cparams("arbitrary"))(zn, y, gate, g, scale, dzn, dh)

    @jax.custom_vjp
    def f(z, y, gate, g, scale, shift):
        return tuple(fwd_call(z, y, gate, g, scale, shift))

    def fwd(z, y, gate, g, scale, shift):
        zn, h = fwd_call(z, y, gate, g, scale, shift)
        return (zn, h), (zn, y, gate, g, scale)

    def bwd(res, cts):
        zn, y, gate, g, scale = res
        dz, dy, dgate, dg, dsc, dsh = bwd_call(zn, y, gate, g, scale, cts[0], cts[1])
        return dz, dy, dgate, dg, dsc, dsh

    f.defvjp(fwd, bwd)
    return f


def _ew_call(name, body, ins, outs_sds, r, widths_in, widths_out, tr, extra_in=(), extra_specs=(), sem="parallel"):
    in_specs = [pl.BlockSpec((tr, w), lambda i: (i, 0)) for w in widths_in] + list(extra_specs)
    out_specs = [pl.BlockSpec((tr, w), lambda i: (i, 0)) if w is not None else pl.BlockSpec(s.shape, lambda i: (0,) * len(s.shape))
                 for w, s in zip(widths_out, outs_sds)]
    return pl.pallas_call(body, grid=(r // tr,), in_specs=in_specs, out_specs=out_specs, out_shape=outs_sds,
                          name=name, compiler_params=_cparams(sem))(*ins, *extra_in)


def _silu(x):
    return x * jax.nn.sigmoid(x)


_GELU_C = math.sqrt(2.0 / math.pi)


def _gelu_and_grad(y):
    inner = _GELU_C * (y + 0.044715 * y * y * y)
    t = jnp.tanh(inner)
    val = 0.5 * y * (1.0 + t)
    grad = 0.5 * (1.0 + t) + 0.5 * y * (1.0 - t * t) * _GELU_C * (1.0 + 3 * 0.044715 * y * y)
    return val, grad


def make_gelu_in(name):
    def fwd_call(u, ys, dsk):
        r, d = u.shape
        tr = _row_tile(r, r, d)

        def body(u_ref, y_ref, d_ref, o_ref):
            y = d_ref[...] * u_ref[...].astype(F32) + y_ref[...]
            o_ref[...] = _gelu_and_grad(y)[0].astype(o_ref.dtype)

        return _ew_call(name + "_fwd", body, (u, ys), [jax.ShapeDtypeStruct((r, d), BF16)], r, (d, d), (d,), tr,
                        extra_in=(dsk,), extra_specs=(pl.BlockSpec((1, d), lambda i: (0, 0)),))[0]

    @jax.custom_vjp
    def f(u, ys, dsk):
        return fwd_call(u, ys, dsk)

    def fwd(u, ys, dsk):
        return fwd_call(u, ys, dsk), (u, ys, dsk)

    def bwd(res, dg):
        u, ys, dsk = res
        r, d = u.shape
        tr = _row_tile(r, r, d)

        def body(u_ref, y_ref, dg_ref, d_ref, du_ref, dy_ref, dd_ref):
            i = pl.program_id(0)
            uu = u_ref[...].astype(F32)
            y = d_ref[...] * uu + y_ref[...]
            dy = dg_ref[...].astype(F32) * _gelu_and_grad(y)[1]
            dy_ref[...] = dy
            du_ref[...] = (d_ref[...] * dy).astype(du_ref.dtype)

            @pl.when(i == 0)
            def _():
                dd_ref[...] = jnp.zeros_like(dd_ref)

            dd_ref[...] += jnp.sum(dy * uu, axis=0, keepdims=True)

        outs = [jax.ShapeDtypeStruct((r, d), u.dtype), jax.ShapeDtypeStruct((r, d), F32), jax.ShapeDtypeStruct((1, d), F32)]
        du, dy, dd = _ew_call(name + "_bwd", body, (u, ys, dg), outs, r, (d, d, d), (d, d, None), tr,
                              extra_in=(dsk,), extra_specs=(pl.BlockSpec((1, d), lambda i: (0, 0)),), sem="arbitrary")
        return du, dy, dd

    f.defvjp(fwd, bwd)
    return f


def make_glu(name):
    def fwd_call(z, b):
        r, d2 = z.shape
        d = d2 // 2
        tr = _row_tile(r, r, d2)

        def body(z_ref, b_ref, o_ref):
            zz = z_ref[...].astype(F32) + b_ref[...]
            o_ref[...] = zz[:, :d] * jax.nn.sigmoid(zz[:, d:])

        return _ew_call(name + "_fwd", body, (z,), [jax.ShapeDtypeStruct((r, d), F32)], r, (d2,), (d,), tr,
                        extra_in=(b,), extra_specs=(pl.BlockSpec((1, d2), lambda i: (0, 0)),))[0]

    @jax.custom_vjp
    def f(z, b):
        return fwd_call(z, b)

    def fwd(z, b):
        return fwd_call(z, b), (z, b)

    def bwd(res, do):
        z, b = res
        r, d2 = z.shape
        d = d2 // 2
        tr = _row_tile(r, r, d2)

        def body(z_ref, do_ref, b_ref, dz_ref, db_ref):
            i = pl.program_id(0)
            zz = z_ref[...].astype(F32) + b_ref[...]
            sg = jax.nn.sigmoid(zz[:, d:])
            g = do_ref[...]
            dza = g * sg
            dzb = g * zz[:, :d] * sg * (1.0 - sg)
            dz_ref[:, :d] = dza.astype(dz_ref.dtype)
            dz_ref[:, d:] = dzb.astype(dz_ref.dtype)

            @pl.when(i == 0)
            def _():
                db_ref[...] = jnp.zeros_like(db_ref)

            db_ref[:, :d] += jnp.sum(dza, axis=0, keepdims=True)
            db_ref[:, d:] += jnp.sum(dzb, axis=0, keepdims=True)

        outs = [jax.ShapeDtypeStruct((r, d2), z.dtype), jax.ShapeDtypeStruct((1, d2), F32)]
        dz, db = _ew_call(name + "_bwd", body, (z, do), outs, r, (d2, d), (d2, None), tr,
                          extra_in=(b,), extra_specs=(pl.BlockSpec((1, d2), lambda i: (0, 0)),), sem="arbitrary")
        return dz, db

    f.defvjp(fwd, bwd)
    return f


def make_final_loss(name):
    def call(z, g, target):
        r, d = z.shape
        tr = _row_tile(r, r, d)

        def body(z_ref, t_ref, g_ref, dz_ref, dg_ref, l_ref):
            i = pl.program_id(0)
            zz = z_ref[...]
            gg = g_ref[...]
            rstd = lax.rsqrt(jnp.mean(zz * zz, axis=-1, keepdims=True) + EPS)
            zhat = zz * rstd
            e = zhat * gg - t_ref[...]
            dy = e * (1.0 / d)
            dyg = dy * gg
            dz_ref[...] = rstd * (dyg - zhat * jnp.mean(dyg * zhat, axis=-1, keepdims=True))

            @pl.when(i == 0)
            def _():
                dg_ref[...] = jnp.zeros_like(dg_ref)
                l_ref[...] = jnp.zeros_like(l_ref)

            dg_ref[...] += jnp.sum(dy * zhat, axis=0, keepdims=True)
            l_ref[...] += jnp.sum(jnp.sum(e * e, axis=1, keepdims=True), axis=0, keepdims=True) * (0.5 / d)

        outs = [jax.ShapeDtypeStruct((r, d), F32), jax.ShapeDtypeStruct((1, d), F32), jax.ShapeDtypeStruct((1, 1), F32)]
        return _ew_call(name, body, (z, target), outs, r, (d, d), (d, None, None), tr,
                        extra_in=(g,), extra_specs=(pl.BlockSpec((1, d), lambda i: (0, 0)),), sem="arbitrary")

    @jax.custom_vjp
    def f(z, g, target):
        return call(z, g, target)[2]

    def fwd(z, g, target):
        dz, dg, loss = call(z, g, target)
        return loss, (dz, dg)

    def bwd(res, dl):
        dz, dg = res
        s = dl[0, 0]
        return dz * s, dg * s, None

    f.defvjp(fwd, bwd)
    return f


def _rope_tables(t):
    quarter = HEAD_DIM // 4
    inv_freq = ROPE_BASE ** (-np.arange(quarter, dtype=np.float64) / quarter)
    pos = np.arange(t)
    ang_r = (pos // GRID_W)[:, None] * inv_freq[None, :]
    ang_c = (pos % GRID_W)[:, None] * inv_freq[None, :]
    cos = np.concatenate([np.cos(ang_r), np.cos(ang_r), np.cos(ang_c), np.cos(ang_c)], axis=1)
    sin = np.concatenate([-np.sin(ang_r), np.sin(ang_r), -np.sin(ang_c), np.sin(ang_c)], axis=1)
    return jnp.asarray(cos, F32), jnp.asarray(sin, F32)


def _rope_call(x, cos, sin, name):
    t, w = x.shape
    tr = _pick(t, (512, 256, 128, 64))
    quarter = HEAD_DIM // 4

    def body(x_ref, c_ref, s_ref, o_ref):
        xx = x_ref[...].astype(F32)
        lane = lax.broadcasted_iota(jnp.int32, xx.shape, 1)
        first = (lane % (2 * quarter)) < quarter
        partner = jnp.where(first, pltpu.roll(xx, HEAD_DIM - quarter, 1), pltpu.roll(xx, quarter, 1))
        o_ref[...] = (xx * c_ref[...] + partner * s_ref[...]).astype(o_ref.dtype)

    return pl.pallas_call(
        body, grid=(t // tr, w // HEAD_DIM),
        in_specs=[pl.BlockSpec((tr, HEAD_DIM), lambda i, j: (i, j)), pl.BlockSpec((tr, HEAD_DIM), lambda i, j: (i, 0)),
                  pl.BlockSpec((tr, HEAD_DIM), lambda i, j: (i, 0))],
        out_specs=pl.BlockSpec((tr, HEAD_DIM), lambda i, j: (i, j)),
        out_shape=jax.ShapeDtypeStruct((t, w), x.dtype), name=name, compiler_params=_cparams("parallel", "parallel"))(x, cos, sin)


def make_rope(t, name):
    cos, sin = _rope_tables(t)

    @jax.custom_vjp
    def f(x):
        return _rope_call(x, cos, sin, name + "_fwd")

    def fwd(x):
        return _rope_call(x, cos, sin, name + "_fwd"), None

    def bwd(_, dy):
        return (_rope_call(dy, cos, -sin, name + "_bwd"),)

    f.defvjp(fwd, bwd)
    return f


def _attn_specs(g, span, tk, m, nbh, has_ctx, hb=1):
    hd = HEAD_DIM
    q_spec = pl.BlockSpec((ATTN_BLOCK, hb * g * hd), lambda h, i, meta: (i, h))
    kv_spec = pl.BlockSpec((tk, hb * hd), lambda h, i, meta: (0, h))
    c_spec = pl.BlockSpec((m, hb * hd), lambda h, i, meta: (0, h))
    if nbh > 1 and hb > 1:
        b_spec = pl.BlockSpec((None, hb, ATTN_BLOCK, span), lambda h, i, meta: (meta[1, i], h, 0, 0))
    elif nbh > 1:
        b_spec = pl.BlockSpec((None, None, ATTN_BLOCK, span), lambda h, i, meta: (meta[1, i], h, 0, 0))
    else:
        b_spec = pl.BlockSpec((None, None, ATTN_BLOCK, span), lambda h, i, meta: (meta[1, i], 0, 0, 0))
    sink_spec = pl.BlockSpec(memory_space=pltpu.SMEM)
    return q_spec, kv_spec, c_spec, b_spec, sink_spec


def _attn_probs(qh, ks, kc, bias, sink_val, scale, has_ctx, has_sink):
    s = _dot_nt(qh, ks) * scale + bias
    mx = jnp.max(s, axis=-1, keepdims=True)
    sc = None
    if has_ctx:
        sc = _dot_nt(qh, kc) * scale
        mx = jnp.maximum(mx, jnp.max(sc, axis=-1, keepdims=True))
    if has_sink:
        mx = jnp.maximum(mx, sink_val)
    p = jnp.exp(s - mx)
    l = jnp.sum(p, axis=-1, keepdims=True)
    pc = None
    if has_ctx:
        pc = jnp.exp(sc - mx)
        l = l + jnp.sum(pc, axis=-1, keepdims=True)
    ps = None
    if has_sink:
        ps = jnp.exp(sink_val - mx)
        l = l + ps
    return p, pc, ps, l


def _attn_fwd(q, k, v, kc, vc, bias, sink, meta, g, span, has_ctx, has_sink, name):
    rq, wq = q.shape
    tk, wk = k.shape
    hkv = wk // HEAD_DIM
    m = kc.shape[0]
    nbh = bias.shape[1]
    scale = HEAD_DIM ** -0.5
    nqb = rq // ATTN_BLOCK
    hb = _pick(hkv, (4, 2, 1)) if g == 1 else 1
    q_spec, kv_spec, c_spec, b_spec, sink_spec = _attn_specs(g, span, tk, m, nbh, has_ctx, hb)
    hd = HEAD_DIM

    def body(meta_ref, sink_ref, q_ref, k_ref, v_ref, kc_ref, vc_ref, b_ref, o_ref):
        h = pl.program_id(0)
        i = pl.program_id(1)
        ks0 = pl.multiple_of(meta_ref[0, i], 64)
        for kl in range(hb):
            kcols = slice(kl * hd, (kl + 1) * hd)
            ks = k_ref[pl.ds(ks0, span), kcols]
            vs = v_ref[pl.ds(ks0, span), kcols]
            bias_t = b_ref[kl] if (hb > 1 and nbh > 1) else b_ref[...]
            for hh in range(g):
                cols = slice((kl * g + hh) * hd, (kl * g + hh + 1) * hd)
                sink_val = sink_ref[(h * hb + kl) * g + hh] if has_sink else None
                p, pc, _, l = _attn_probs(q_ref[:, cols], ks, kc_ref[:, kcols], bias_t, sink_val, scale, has_ctx, has_sink)
                acc = jnp.dot(p.astype(BF16), vs, preferred_element_type=F32)
                if has_ctx:
                    acc = acc + jnp.dot(pc.astype(BF16), vc_ref[:, kcols], preferred_element_type=F32)
                o_ref[:, cols] = (acc / l).astype(o_ref.dtype)

    gs = pltpu.PrefetchScalarGridSpec(
        num_scalar_prefetch=1, grid=(hkv // hb, nqb),
        in_specs=[sink_spec, q_spec, kv_spec, kv_spec, c_spec, c_spec, b_spec], out_specs=q_spec)
    return pl.pallas_call(body, grid_spec=gs, out_shape=jax.ShapeDtypeStruct((rq, wq), BF16), name=name,
                          compiler_params=_cparams("parallel", "arbitrary"))(meta, sink, q, k, v, kc, vc, bias)


def _attn_bwd(q, k, v, kc, vc, bias, sink, meta, o, do, g, span, has_ctx, has_sink, want_dbias, name):
    rq, wq = q.shape
    tk, wk = k.shape
    hkv = wk // HEAD_DIM
    m = kc.shape[0]
    ncase, nbh = bias.shape[:2]
    scale = HEAD_DIM ** -0.5
    nqb = rq // ATTN_BLOCK
    hb = _pick(hkv, (2, 1)) if g == 1 else 1
    q_spec, kv_spec, c_spec, b_spec, sink_spec = _attn_specs(g, span, tk, m, nbh, has_ctx, hb)
    dsink_spec = pl.BlockSpec((None, 8, HEAD_DIM), lambda h, i, meta: (h, 0, 0))
    hd = HEAD_DIM

    def body(meta_ref, sink_ref, q_ref, k_ref, v_ref, kc_ref, vc_ref, b_ref, o_ref, do_ref,
             dq_ref, dk_ref, dv_ref, dkc_ref, dvc_ref, db_ref, dsk_ref):
        h = pl.program_id(0)
        i = pl.program_id(1)

        @pl.when(i == 0)
        def _():
            dk_ref[...] = jnp.zeros_like(dk_ref)
            dv_ref[...] = jnp.zeros_like(dv_ref)
            dkc_ref[...] = jnp.zeros_like(dkc_ref)
            dvc_ref[...] = jnp.zeros_like(dvc_ref)
            dsk_ref[...] = jnp.zeros_like(dsk_ref)

        if want_dbias:
            @pl.when(meta_ref[2, i] == 1)
            def _():
                db_ref[...] = jnp.zeros_like(db_ref)
        else:
            @pl.when(i == 0)
            def _():
                db_ref[...] = jnp.zeros_like(db_ref)

        ks0 = pl.multiple_of(meta_ref[0, i], 64)
        for kl in range(hb):
            kcols = slice(kl * hd, (kl + 1) * hd)
            ks = k_ref[pl.ds(ks0, span), kcols]
            vs = v_ref[pl.ds(ks0, span), kcols]
            per_head_bias = hb > 1 and nbh > 1
            bias_t = b_ref[kl] if per_head_bias else b_ref[...]
            dk_acc = jnp.zeros((span, hd), F32)
            dv_acc = jnp.zeros((span, hd), F32)
            for hh in range(g):
                cols = slice((kl * g + hh) * hd, (kl * g + hh + 1) * hd)
                qh = q_ref[:, cols]
                doh = do_ref[:, cols]
                sink_val = sink_ref[(h * hb + kl) * g + hh] if has_sink else None
                p, pc, ps, l = _attn_probs(qh, ks, kc_ref[:, kcols], bias_t, sink_val, scale, has_ctx, has_sink)
                inv_l = 1.0 / l
                delta = jnp.sum(doh.astype(F32) * o_ref[:, cols].astype(F32), axis=-1, keepdims=True)
                pn = p * inv_l
                ds = pn * (_dot_nt(doh, vs) - delta)
                dsb = ds.astype(BF16)
                dq = jnp.dot(dsb, ks, preferred_element_type=F32)
                dk_acc = dk_acc + _dot_tn(dsb, qh)
                dv_acc = dv_acc + _dot_tn(pn.astype(BF16), doh)
                if want_dbias and per_head_bias:
                    db_ref[kl] += ds
                elif want_dbias:
                    db_ref[...] += ds
                if has_ctx:
                    pcn = pc * inv_l
                    dsc = (pcn * (_dot_nt(doh, vc_ref[:, kcols]) - delta)).astype(BF16)
                    dq = dq + jnp.dot(dsc, kc_ref[:, kcols], preferred_element_type=F32)
                    dkc_ref[:, kcols] += _dot_tn(dsc, qh) * scale
                    dvc_ref[:, kcols] += _dot_tn(pcn.astype(BF16), doh)
                if has_sink:
                    dsv = -jnp.sum(ps * inv_l * delta, axis=0, keepdims=True)
                    dsk_ref[kl * g + hh:kl * g + hh + 1, :] += jnp.broadcast_to(dsv, (1, hd))
                dq_ref[:, cols] = (dq * scale).astype(dq_ref.dtype)
            dk_ref[pl.ds(ks0, span), kcols] += dk_acc * scale
            dv_ref[pl.ds(ks0, span), kcols] += dv_acc

    gs = pltpu.PrefetchScalarGridSpec(
        num_scalar_prefetch=1, grid=(hkv // hb, nqb),
        in_specs=[sink_spec, q_spec, kv_spec, kv_spec, c_spec, c_spec, b_spec, q_spec, q_spec],
        out_specs=[q_spec, kv_spec, kv_spec, c_spec, c_spec, b_spec if want_dbias else dsink_spec, dsink_spec])
    db_sds = jax.ShapeDtypeStruct((ncase, nbh, ATTN_BLOCK, span) if want_dbias else (hkv, 8, HEAD_DIM), F32)
    out_shape = [jax.ShapeDtypeStruct((rq, wq), BF16), jax.ShapeDtypeStruct((tk, wk), F32), jax.ShapeDtypeStruct((tk, wk), F32),
                 jax.ShapeDtypeStruct((m, wk), F32), jax.ShapeDtypeStruct((m, wk), F32), db_sds,
                 jax.ShapeDtypeStruct((hkv, 8, HEAD_DIM), F32)]
    return pl.pallas_call(body, grid_spec=gs, out_shape=out_shape, name=name,
                          compiler_params=_cparams("parallel", "arbitrary"))(meta, sink, q, k, v, kc, vc, bias, o, do)


def make_attention(meta_np, g, span, has_ctx, has_sink, want_dbias, name):
    meta = jnp.asarray(meta_np, jnp.int32)

    @jax.custom_vjp
    def f(q, k, v, kc, vc, bias, sink):
        return _attn_fwd(q, k, v, kc, vc, bias, sink, meta, g, span, has_ctx, has_sink, name + "_fwd")

    def fwd(q, k, v, kc, vc, bias, sink):
        o = _attn_fwd(q, k, v, kc, vc, bias, sink, meta, g, span, has_ctx, has_sink, name + "_fwd")
        return o, (q, k, v, kc, vc, bias, sink, o)

    def bwd(res, do):
        q, k, v, kc, vc, bias, sink, o = res
        dq, dk, dv, dkc, dvc, db, dsk = _attn_bwd(q, k, v, kc, vc, bias, sink, meta, o, do.astype(BF16), g, span,
                                                   has_ctx, has_sink, want_dbias, name + "_bwd")
        dsink = dsk[:, :g, 0].reshape(sink.shape) if has_sink else jnp.zeros_like(sink)
        if not want_dbias:
            db = jnp.zeros_like(bias)
        return dq, dk.astype(k.dtype), dv.astype(v.dtype), dkc.astype(kc.dtype), dvc.astype(vc.dtype), db, dsink

    f.defvjp(fwd, bwd)
    return f


def _dedupe_cases(tables):
    cases, idx, first = [], [], []
    for tbl in tables:
        if cases and np.array_equal(cases[-1], tbl):
            idx.append(len(cases) - 1)
            first.append(0)
        else:
            cases.append(tbl)
            idx.append(len(cases) - 1)
            first.append(1)
    return cases, idx, first


def _na_plan(t):
    rows = t // GRID_W
    qr = ATTN_BLOCK // GRID_W
    kr = qr + NA_ROWS - 1
    assert rows >= kr and rows % qr == 0
    span = kr * GRID_W
    kstart, tables = [], []
    qcol = np.tile(np.arange(GRID_W), qr)
    kcol = np.tile(np.arange(GRID_W), kr)
    win_c = np.clip(qcol - NA_COLS // 2, 0, GRID_W - NA_COLS)
    col_ok = (kcol[None, :] >= win_c[:, None]) & (kcol[None, :] < win_c[:, None] + NA_COLS)
    dcol = np.clip(kcol[None, :] - qcol[:, None] + NA_COLS - 1, 0, 2 * NA_COLS - 2)
    for r0 in range(0, rows, qr):
        kb = int(np.clip(r0 - NA_ROWS // 2, 0, rows - kr))
        qrow = r0 + np.repeat(np.arange(qr), GRID_W)
        krow = kb + np.repeat(np.arange(kr), GRID_W)
        win_r = np.clip(qrow - NA_ROWS // 2, 0, rows - NA_ROWS)
        row_ok = (krow[None, :] >= win_r[:, None]) & (krow[None, :] < win_r[:, None] + NA_ROWS)
        drow = np.clip(krow[None, :] - qrow[:, None] + NA_ROWS - 1, 0, 2 * NA_ROWS - 2)
        tables.append(np.stack([row_ok & col_ok, drow, dcol]).astype(np.int32))
        kstart.append(kb * GRID_W)
    cases, idx, first = _dedupe_cases(tables)
    meta = np.array([kstart, idx, first], np.int32)
    return meta, span, np.stack(cases)


def _na_bias(rpb, cases):
    valid, drow, dcol = cases[:, 0], cases[:, 1], cases[:, 2]
    ncase, qn, span = valid.shape
    qr, kr = qn // GRID_W, span // GRID_W
    drow_s = drow.reshape(ncase, qr, GRID_W, kr, GRID_W)[:, :, 0, :, 0]
    dcol_s = dcol[0].reshape(qr, GRID_W, kr, GRID_W)[0, :, 0, :]
    oh_r = jnp.asarray(np.eye(2 * NA_ROWS - 1, dtype=np.float32)[drow_s])
    oh_c = jnp.asarray(np.eye(2 * NA_COLS - 1, dtype=np.float32)[dcol_s])
    tmp = jnp.einsum("hrc,xyc->hrxy", rpb, oh_c, precision=lax.Precision.HIGHEST)
    b = jnp.einsum("nakr,hrxy->nhaxky", oh_r, tmp, precision=lax.Precision.HIGHEST).reshape(ncase, -1, qn, span)
    return jnp.where(jnp.asarray(valid[:, None] > 0), b, NEG_INF)


def _sw_plan(t):
    span = 3 * ATTN_BLOCK
    assert t >= span
    kstart, tables = [], []
    for b in range(t // ATTN_BLOCK):
        ks = int(np.clip((b - 1) * ATTN_BLOCK, 0, t - span))
        qpos = b * ATTN_BLOCK + np.arange(ATTN_BLOCK)
        kpos = ks + np.arange(span)
        ok = np.abs(kpos[None, :] - qpos[:, None]) <= SW_RADIUS
        tables.append(np.where(ok, 0.0, NEG_INF).astype(np.float32))
        kstart.append(ks)
    cases, idx, first = _dedupe_cases(tables)
    return np.array([kstart, idx, first], np.int32), span, np.stack(cases)[:, None]


def _cmul(ar, ai, br, bi):
    return ar * br - ai * bi, ar * bi + ai * br


def _s5_scan_call(x2, win, lam, cin, wout, reverse, n_chunks, name):
    _, ll, d = x2.shape
    nt = d // HEAD_DIM
    sw = 2 * SSM_TILE_GROUPS * SSM_STATE
    hw = sw // 2
    rows = ll // n_chunks
    full = cin is not None

    down_dir = 0 if reverse else 1

    def chunk_idx(k, dd):
        return jnp.where(dd == down_dir, n_chunks - 1 - k, k)

    n_sub = _pick(rows // 16, (6, 3, 1))
    sub = rows // n_sub
    ics = sub // SCAN_BLOCKS

    def body(*refs):
        if full:
            x_ref, win_ref, lam_ref, cin_ref, wout_ref, s_out, y_out = refs[:7]
        else:
            x_ref, win_ref, lam_ref, f_out = refs[:4]
        ubs, st_ref = refs[-1 - n_sub:-1], refs[-1]
        k = pl.program_id(2)
        down = pl.program_id(0) == down_dir

        @pl.when(k == 0)
        def _():
            st_ref[...] = cin_ref[...] if full else jnp.zeros_like(st_ref)

        def sub_rows(p):
            return pl.ds(pl.multiple_of(jnp.where(down, n_sub - 1 - p, p) * sub, 16), sub)

        def drive(p):
            ubs[p][...] = jnp.dot(x_ref[sub_rows(p), :].astype(BF16), win_ref[...], preferred_element_type=F32)

        lr = lam_ref[:, :hw]
        li = lam_ref[:, hw:]
        if full:
            def walk(downwards):
                def rows_of(p):
                    q = n_sub - 1 - p if downwards else p
                    return slice(q * sub, (q + 1) * sub)

                def drive_static(p):
                    ubs[p][...] = jnp.dot(x_ref[rows_of(p), :].astype(BF16), win_ref[...], preferred_element_type=F32)

                sr, si = st_ref[:, :hw], st_ref[:, hw:]
                drive_static(0)
                for p in range(n_sub):
                    if p + 1 < n_sub:
                        drive_static(p + 1)
                    for ii in range(ics):
                        r0 = (ics - 1 - ii if downwards else ii) * SCAN_BLOCKS
                        sr, si = (lr * sr - li * si + ubs[p][r0:r0 + SCAN_BLOCKS, :hw],
                                  lr * si + li * sr + ubs[p][r0:r0 + SCAN_BLOCKS, hw:])
                        ubs[p][r0:r0 + SCAN_BLOCKS, :hw] = sr
                        ubs[p][r0:r0 + SCAN_BLOCKS, hw:] = si
                    sb = ubs[p][...].astype(BF16)
                    s_out[rows_of(p), :] = sb
                    y_out[rows_of(p), :] = jnp.dot(sb, wout_ref[...], preferred_element_type=F32)
                st_ref[:, :hw] = sr
                st_ref[:, hw:] = si

            @pl.when(down)
            def _():
                walk(True)

            @pl.when(jnp.logical_not(down))
            def _():
                walk(False)
        else:
            sr, si = st_ref[:, :hw], st_ref[:, hw:]
            drive(0)
            for p in range(n_sub):
                if p + 1 < n_sub:
                    drive(p + 1)
                for ii in range(ics):
                    r0 = pl.multiple_of(jnp.where(down, (ics - 1 - ii) * SCAN_BLOCKS, ii * SCAN_BLOCKS), SCAN_BLOCKS)
                    ur = ubs[p][pl.ds(r0, SCAN_BLOCKS), :hw]
                    ui = ubs[p][pl.ds(r0, SCAN_BLOCKS), hw:]
                    sr, si = lr * sr - li * si + ur, lr * si + li * sr + ui
            st_ref[:, :hw] = sr
            st_ref[:, hw:] = si
        if not full:
            @pl.when(k == n_chunks - 1)
            def _():
                f_out[...] = st_ref[...]

    x_spec = pl.BlockSpec((None, rows, HEAD_DIM), lambda dd, t, k: (dd, chunk_idx(k, dd), t))
    win_spec = pl.BlockSpec((None, None, HEAD_DIM, sw), lambda dd, t, k: (dd, t, 0, 0))
    vec_spec = pl.BlockSpec((None, None, SCAN_BLOCKS, sw), lambda dd, t, k: (dd, t, 0, 0))
    scratch = [pltpu.VMEM((sub, sw), F32) for _ in range(n_sub)] + [pltpu.VMEM((SCAN_BLOCKS, sw), F32)]
    if full:
        in_specs = [x_spec, win_spec, vec_spec, vec_spec, pl.BlockSpec((None, None, sw, HEAD_DIM), lambda dd, t, k: (dd, t, 0, 0))]
        out_specs = [pl.BlockSpec((None, None, rows, sw), lambda dd, t, k: (dd, t, chunk_idx(k, dd), 0)), x_spec]
        out_shape = [jax.ShapeDtypeStruct((2, nt, ll, sw), BF16), jax.ShapeDtypeStruct((2, ll, d), F32)]
        args = (x2, win, lam, cin, wout)
    else:
        in_specs = [x_spec, win_spec, vec_spec]
        out_specs = vec_spec
        out_shape = jax.ShapeDtypeStruct((2, nt, SCAN_BLOCKS, sw), F32)
        args = (x2, win, lam)
    return pl.pallas_call(body, grid=(2, nt, n_chunks), in_specs=in_specs, out_specs=out_specs, out_shape=out_shape,
                          scratch_shapes=scratch, name=name,
                          compiler_params=_cparams("parallel", "parallel", "arbitrary"))(*args)


def _s5_bwd_call(dy2, wrt, lamc, cin, st, u2, wdt, n_chunks, name):
    _, ll, d = dy2.shape
    nt = d // HEAD_DIM
    sw = 2 * SSM_TILE_GROUPS * SSM_STATE
    hw = sw // 2
    rows = ll // n_chunks

    def chunk_idx(k, dd):
        return jnp.where(dd == 0, n_chunks - 1 - k, k)

    n_sub = _pick(rows // 16, (6, 3, 1))
    sub = rows // n_sub
    ics = sub // SCAN_BLOCKS

    def body(dy_ref, wrt_ref, lam_ref, cin_ref, stb_ref, u_ref, wdt_ref, du_out, dwd_out, dwr_out, dlam_out, *scratch):
        dss, sts, a_ref = scratch[:n_sub], scratch[n_sub:2 * n_sub], scratch[-1]
        k = pl.program_id(2)
        down = pl.program_id(0) == 0

        @pl.when(k == 0)
        def _():
            a_ref[...] = cin_ref[...]
            dwd_out[...] = jnp.zeros_like(dwd_out)
            dwr_out[...] = jnp.zeros_like(dwr_out)
            dlam_out[...] = jnp.zeros_like(dlam_out)

        def sub_rows(p):
            return pl.ds(pl.multiple_of(jnp.where(down, n_sub - 1 - p, p) * sub, 16), sub)

        def prepare(p):
            dss[p][...] = jnp.dot(dy_ref[sub_rows(p), :].astype(BF16), wrt_ref[...], preferred_element_type=F32)
            sts[p][...] = stb_ref[sub_rows(p), :].astype(F32)

        def finish(p):
            ab = dss[p][...].astype(BF16)
            du_out[sub_rows(p), :] = jnp.dot(ab, wdt_ref[...], preferred_element_type=F32).astype(du_out.dtype)
            dwd_out[...] += _dot_tn(u_ref[sub_rows(p), :].astype(BF16), ab)
            dwr_out[...] += _dot_tn(stb_ref[sub_rows(p), :], dy_ref[sub_rows(p), :].astype(BF16))

        lr = lam_ref[:, :hw]
        li = lam_ref[:, hw:]
        ar, ai, gr, gi = a_ref[:, :hw], a_ref[:, hw:], dlam_out[:, :hw], dlam_out[:, hw:]
        prepare(0)
        for p in range(n_sub):
            if p + 1 < n_sub:
                prepare(p + 1)
            for ii in range(ics):
                r0 = pl.multiple_of(jnp.where(down, (ics - 1 - ii) * SCAN_BLOCKS, ii * SCAN_BLOCKS), SCAN_BLOCKS)
                sr = sts[p][pl.ds(r0, SCAN_BLOCKS), :hw]
                si = sts[p][pl.ds(r0, SCAN_BLOCKS), hw:]
                gr = gr + ar * sr + ai * si
                gi = gi + ai * sr - ar * si
                ar, ai = (lr * ar - li * ai + dss[p][pl.ds(r0, SCAN_BLOCKS), :hw],
                          lr * ai + li * ar + dss[p][pl.ds(r0, SCAN_BLOCKS), hw:])
                dss[p][pl.ds(r0, SCAN_BLOCKS), :hw] = ar
                dss[p][pl.ds(r0, SCAN_BLOCKS), hw:] = ai
            finish(p)
        a_ref[:, :hw] = ar
        a_ref[:, hw:] = ai
        dlam_out[:, :hw] = gr
        dlam_out[:, hw:] = gi

    x_spec = pl.BlockSpec((None, rows, HEAD_DIM), lambda dd, t, k: (dd, chunk_idx(k, dd), t))
    w_in = pl.BlockSpec((None, None, HEAD_DIM, sw), lambda dd, t, k: (dd, t, 0, 0))
    w_out = pl.BlockSpec((None, None, sw, HEAD_DIM), lambda dd, t, k: (dd, t, 0, 0))
    vec_spec = pl.BlockSpec((None, None, SCAN_BLOCKS, sw), lambda dd, t, k: (dd, t, 0, 0))
    st_spec = pl.BlockSpec((None, None, rows, sw), lambda dd, t, k: (dd, t, chunk_idx(k, dd), 0))
    out_shape = [jax.ShapeDtypeStruct((2, ll, d), u2.dtype), jax.ShapeDtypeStruct((2, nt, HEAD_DIM, sw), F32),
                 jax.ShapeDtypeStruct((2, nt, sw, HEAD_DIM), F32), jax.ShapeDtypeStruct((2, nt, SCAN_BLOCKS, sw), F32)]
    return pl.pallas_call(
        body, grid=(2, nt, n_chunks),
        in_specs=[x_spec, w_in, vec_spec, vec_spec, st_spec, x_spec, w_out],
        out_specs=[x_spec, w_in, w_out, vec_spec], out_shape=out_shape,
        scratch_shapes=[pltpu.VMEM((sub, sw), F32) for _ in range(2 * n_sub)] + [pltpu.VMEM((SCAN_BLOCKS, sw), F32)],
        name=name, compiler_params=_cparams("parallel", "parallel", "arbitrary"))(dy2, wrt, lamc, cin, st, u2, wdt)


def _cpow(lr, li, n):
    rr, ri = jnp.ones_like(lr), jnp.zeros_like(li)
    br, bi = lr, li
    while n:
        if n & 1:
            rr, ri = _cmul(rr, ri, br, bi)
        br, bi = _cmul(br, bi, br, bi)
        n >>= 1
    return rr, ri


def _resolve_carries(finals, lam, block_len, down_dir):
    hw = finals.shape[-1] // 2
    pr, pi = _cpow(lam[:, :, 0, :hw], lam[:, :, 0, hw:], block_len)
    fr, fi = finals[..., :hw], finals[..., hw:]

    def walk(order):
        cr, ci = jnp.zeros_like(pr), jnp.zeros_like(pi)
        out = [None] * SCAN_BLOCKS
        for j in order:
            out[j] = jnp.concatenate([cr, ci], axis=-1)
            mr, mi = _cmul(pr, pi, cr, ci)
            cr, ci = mr + fr[:, :, j], mi + fi[:, :, j]
        return jnp.stack(out, axis=2)

    up, down = walk(range(SCAN_BLOCKS)), walk(range(SCAN_BLOCKS - 1, -1, -1))
    return jnp.stack([down[0], up[1]] if down_dir == 0 else [up[0], down[1]])


def _scan_chunks(ll):
    block_len = ll // SCAN_BLOCKS
    for ic in (132, 128, 96, 64, 48, 36, 32, 24, 16, 8):
        if block_len % ic == 0:
            return block_len // ic
    return 1


def make_s5_core(name):
    def run_fwd(u2, lam, wd, wr):
        ll = u2.shape[1]
        nc = _scan_chunks(ll)
        lam8 = jnp.broadcast_to(lam[:, :, None, :], lam.shape[:2] + (SCAN_BLOCKS, lam.shape[-1]))
        wdb = wd.astype(BF16)
        finals = _s5_scan_call(u2, wdb, lam8, None, None, False, nc, name + "_carry")
        cin = _resolve_carries(finals, lam8, ll // SCAN_BLOCKS, 1)
        st, y2 = _s5_scan_call(u2, wdb, lam8, cin, wr.astype(BF16), False, nc, name + "_scan")
        return y2, st, lam8

    @jax.custom_vjp
    def f(u2, lam, wd, wr):
        return run_fwd(u2, lam, wd, wr)[0]

    def fwd(u2, lam, wd, wr):
        y2, st, lam8 = run_fwd(u2, lam, wd, wr)
        return y2, (u2, lam8, wd, wr, st)

    def bwd(res, dy2):
        u2, lam8, wd, wr, st = res
        ll = u2.shape[1]
        nc = _scan_chunks(ll)
        hw = lam8.shape[-1] // 2
        lamc = jnp.concatenate([lam8[..., :hw], -lam8[..., hw:]], axis=-1)
        wrt = jnp.swapaxes(wr, 2, 3).astype(BF16)
        wdt = jnp.swapaxes(wd, 2, 3).astype(BF16)
        finals = _s5_scan_call(dy2, wrt, lamc, None, None, True, nc, name + "_bcarry")
        cin = _resolve_carries(finals, lamc, ll // SCAN_BLOCKS, 0)
        du2, dwd, dwr, dlam8 = _s5_bwd_call(dy2, wrt, lamc, cin, st, u2, wdt, nc, name + "_bscan")
        return du2, jnp.sum(dlam8, axis=2), dwd, dwr

    f.defvjp(fwd, bwd)
    return f


def _s5_params(a_re, a_im, log_dt, b_re, b_im, c_re, c_im):
    dt = jnp.exp(log_dt)[..., None]
    mag = jnp.exp(a_re * dt)
    lam_r, lam_i = mag * jnp.cos(a_im * dt), mag * jnp.sin(a_im * dt)
    den = a_re * a_re + a_im * a_im
    nr = lam_r - 1.0
    coef_r = (nr * a_re + lam_i * a_im) / den
    coef_i = (lam_i * a_re - nr * a_im) / den
    bbar_r = coef_r[..., None] * b_re - coef_i[..., None] * b_im
    bbar_i = coef_r[..., None] * b_im + coef_i[..., None] * b_re
    ndir, g, p = lam_r.shape
    tg = SSM_TILE_GROUPS
    nt = g // tg
    eye = jnp.eye(tg, dtype=F32)

    def tile_vec(v):
        return v.reshape(ndir, nt, tg * p)

    lam = jnp.concatenate([tile_vec(lam_r), tile_vec(lam_i)], axis=-1)

    def drive(b):
        bt = b.reshape(ndir, nt, tg, p, SSM_GROUP)
        return (jnp.swapaxes(bt, 3, 4)[:, :, :, :, None, :] * eye[None, None, :, None, :, None]).reshape(ndir, nt, tg * SSM_GROUP, tg * p)

    wd = jnp.concatenate([drive(bbar_r), drive(bbar_i)], axis=-1)

    def readout(c):
        ct = c.reshape(ndir, nt, tg, SSM_GROUP, p)
        return (jnp.swapaxes(ct, 3, 4)[:, :, :, :, None, :] * eye[None, None, :, None, :, None]).reshape(ndir, nt, tg * p, tg * SSM_GROUP)

    wr = jnp.concatenate([readout(c_re), -readout(c_im)], axis=2)
    return lam, wd, wr


def _to_scan_order(seq):
    ll, d = seq.shape
    return seq.reshape(SCAN_BLOCKS, ll // SCAN_BLOCKS, d).swapaxes(0, 1).reshape(ll, d)


def _from_scan_order(y2):
    ll, d = y2.shape
    return y2.reshape(ll // SCAN_BLOCKS, SCAN_BLOCKS, d).swapaxes(0, 1).reshape(ll, d)


def adamw(w, g, m, v, name):
    shape = w.shape
    cols = shape[-1] if len(shape) > 1 else shape[0]
    w2, g2, m2, v2 = (a.reshape(-1, cols) for a in (w, g, m, v))
    r = w2.shape[0]
    cap = max(1, (1024 * 1024) // (4 * cols))
    tr = r
    for t in (512, 256, 128, 64, 32, 16, 8):
        if t <= cap and r % t == 0:
            tr = t
            break
    c1 = 1.0 / (1.0 - ADAM_B1 ** ADAM_STEP)
    c2 = 1.0 / (1.0 - ADAM_B2 ** ADAM_STEP)

    def body(w_ref, g_ref, m_ref, v_ref, d_ref, mo_ref, vo_ref):
        gg = g_ref[...]
        mn = ADAM_B1 * m_ref[...] + (1.0 - ADAM_B1) * gg
        vn = ADAM_B2 * v_ref[...] + (1.0 - ADAM_B2) * (gg * gg)
        d_ref[...] = -ADAM_LR * ((mn * c1) / (jnp.sqrt(vn * c2) + ADAM_EPS) + ADAM_WD * w_ref[...])
        mo_ref[...] = mn
        vo_ref[...] = vn

    spec = pl.BlockSpec((tr, cols), lambda i: (i, 0))
    sds = jax.ShapeDtypeStruct((r, cols), F32)
    d, mn, vn = pl.pallas_call(body, grid=(r // tr,), in_specs=[spec] * 4, out_specs=[spec] * 3, out_shape=[sds] * 3,
                               name=name, compiler_params=_cparams("parallel"))(w2, g2, m2, v2)
    return d.reshape(shape), mn.reshape(shape), vn.reshape(shape)


def _my_pos():
    return lax.axis_index("x"), lax.axis_index("y"), lax.axis_index("c")


def _flip(pos, f):
    return tuple((1 - p) if b else p for p, b in zip(pos, f))


def _lin(pos):
    return 4 * pos[0] + 2 * pos[1] + pos[2]


def _remote_copies(src_ref, out_ref, send_sems, recv_sems, plan):
    me = _my_pos()
    copies = []
    for k, (f, sfn, dfn) in enumerate(plan):
        peer = _flip(me, f)
        copies.append(pltpu.make_async_remote_copy(
            src_ref=src_ref.at[sfn(me, peer)], dst_ref=out_ref.at[dfn(me, peer)], send_sem=send_sems.at[k],
            recv_sem=recv_sems.at[k], device_id=peer, device_id_type=MESH))
    return copies


def xchg(src, n_out, plan, name, inplace=False):
    piece = src.shape[1:]

    def body(src_ref, out_ref, send_sems, recv_sems):
        me = _my_pos()
        copies = []
        for k, (f, sfn, dfn) in enumerate(plan):
            peer = _flip(me, f)
            s_ref = (out_ref if inplace else src_ref).at[sfn(me, peer)]
            d_ref = out_ref.at[dfn(me, peer)]
            if any(f):
                cp = pltpu.make_async_remote_copy(src_ref=s_ref, dst_ref=d_ref, send_sem=send_sems.at[k],
                                                  recv_sem=recv_sems.at[k], device_id=peer, device_id_type=MESH)
            else:
                cp = pltpu.make_async_copy(s_ref, d_ref, recv_sems.at[k])
            cp.start()
            copies.append((cp, any(f)))
        for cp, remote in copies:
            if remote:
                cp.wait_recv()
            else:
                cp.wait()
        for cp, remote in copies:
            if remote:
                cp.wait_send()

    return pl.pallas_call(
        body, in_specs=[pl.BlockSpec(memory_space=pl.ANY)], out_specs=pl.BlockSpec(memory_space=pl.ANY),
        out_shape=jax.ShapeDtypeStruct((n_out,) + piece, src.dtype),
        scratch_shapes=[pltpu.SemaphoreType.DMA((len(plan),)), pltpu.SemaphoreType.DMA((len(plan),))],
        input_output_aliases={0: 0} if inplace else {}, name=name)(src)


_CHIP_FLIPS = ((1, 0, 0), (0, 1, 0), (1, 1, 0))
_ALL_FLIPS = tuple((a, b, c) for a in (0, 1) for b in (0, 1) for c in (0, 1))[1:]


def all_to_all8(src, name):
    plan = [((0, 0, 0), lambda me, peer: _lin(me), lambda me, peer: _lin(me))]
    plan += [(f, lambda me, peer: _lin(peer), lambda me, peer: _lin(me)) for f in _ALL_FLIPS]
    return xchg(src, N_DEV, plan, name)


def all_gather8(piece, name):
    plan = [((0, 0, 0), lambda me, peer: 0, lambda me, peer: _lin(me))]
    plan += [(f, lambda me, peer: 0, lambda me, peer: _lin(me)) for f in _ALL_FLIPS]
    return xchg(piece[None], N_DEV, plan, name)


def _ag_prepare(shard):
    k, ns = shard.shape
    px, py, _ = _my_pos()
    own = shard.astype(BF16)[None]
    return lax.dynamic_update_slice(jnp.zeros((4, k, ns), BF16), own, (2 * px + py, 0, 0)).reshape(8, k // 2, ns)


def _ag_plan():
    return [(f, lambda me, peer: _lin(me), lambda me, peer: _lin(me)) for f in _CHIP_FLIPS]


def _ag_finish(buf, name):
    plan = [((0, 0, 1), lambda me, peer, f=f: _lin(_flip(me, f)), lambda me, peer, f=f: _lin(_flip(me, f)))
            for f in _CHIP_FLIPS]
    _, kh, ns = buf.shape
    return xchg(buf, 8, plan, name, inplace=True).reshape(4, 2 * kh, ns)


def gather_weight(shard, name):
    k, ns = shard.shape
    buf = _ag_prepare(shard)

    def body(in_ref, out_ref, send_sems, recv_sems):
        me = _my_pos()
        sibling = _flip(me, (0, 0, 1))
        chips = [_flip(me, f) for f in _CHIP_FLIPS]

        def copy(sem, holder, to):
            rows = out_ref.at[4 * holder[0] + 2 * holder[1] + me[2]]
            return pltpu.make_async_remote_copy(src_ref=rows, dst_ref=rows, send_sem=send_sems.at[sem],
                                                recv_sem=recv_sems.at[sem], device_id=to, device_id_type=MESH)

        first = [copy(j, me, chip) for j, chip in enumerate(chips)]
        for cp in first:
            cp.start()
        passed = [copy(3 + j, chip, sibling) for j, chip in enumerate(chips)]
        for j, chip in enumerate(chips):
            copy(j, chip, me).wait_recv()
            passed[j].start()
        for j in range(3):
            passed[j].wait_recv()
        for cp in first + passed:
            cp.wait_send()

    full = pl.pallas_call(
        body, in_specs=[pl.BlockSpec(memory_space=pl.ANY)], out_specs=pl.BlockSpec(memory_space=pl.ANY),
        out_shape=jax.ShapeDtypeStruct(buf.shape, BF16),
        scratch_shapes=[pltpu.SemaphoreType.DMA((6,)), pltpu.SemaphoreType.DMA((6,))],
        input_output_aliases={0: 0}, name=name)(buf)
    return full.reshape(4, k, ns)


_RS_SLOTS = 7


def _sum_halves(g8, l1, c_idx, name):
    _, _, r, cc = g8.shape
    tr = _row_tile(r, r, cc)

    def body(c_ref, a_ref, b_ref, o_ref):
        o_ref[...] = (a_ref[...].astype(F32) + b_ref[...].astype(F32)).astype(o_ref.dtype)

    gs = pltpu.PrefetchScalarGridSpec(
        num_scalar_prefetch=1, grid=(4, r // tr),
        in_specs=[pl.BlockSpec((None, None, tr, cc), lambda s, i, c: (s, c[0], i, 0)),
                  pl.BlockSpec((None, tr, cc), lambda s, i, c: (s, i, 0))],
        out_specs=pl.BlockSpec((None, tr, cc), lambda s, i, c: (s, i, 0)))
    return pl.pallas_call(body, grid_spec=gs, out_shape=jax.ShapeDtypeStruct((_RS_SLOTS, r, cc), BF16), name=name,
                          compiler_params=_cparams("parallel", "parallel"))(c_idx, g8, l1)


def _sum_chips(buf, sc_idx, name):
    _, r, cc = buf.shape
    tr = _row_tile(r, r, cc)

    def body(s_ref, a_ref, b0_ref, b1_ref, b2_ref, o_ref):
        o_ref[...] = ((a_ref[...].astype(F32) + b0_ref[...].astype(F32)) + b1_ref[...].astype(F32)) + b2_ref[...].astype(F32)

    gs = pltpu.PrefetchScalarGridSpec(
        num_scalar_prefetch=1, grid=(r // tr,),
        in_specs=[pl.BlockSpec((None, tr, cc), lambda i, s: (s[0], i, 0))]
        + [pl.BlockSpec((None, tr, cc), lambda i, s, j=j: (4 + j, i, 0)) for j in range(3)],
        out_specs=pl.BlockSpec((None, tr, cc), lambda i, s: (s[1], i, 0)))
    return pl.pallas_call(body, grid_spec=gs, out_shape=jax.ShapeDtypeStruct((2, r, cc), F32), name=name,
                          compiler_params=_cparams("parallel"))(sc_idx, buf, buf, buf, buf)


def _rs_begin(g4, name):
    _, k, ns = g4.shape
    c_idx = jnp.reshape(_my_pos()[2], (1,)).astype(jnp.int32)
    plan1 = [((0, 0, 1), lambda me, peer, s=s: 2 * s + peer[2], lambda me, peer, s=s: s) for s in range(4)]
    l1 = xchg(g4.reshape(8, k // 2, ns), 4, plan1, name + "_rs_d2d")
    buf = _sum_halves(g4.reshape(4, 2, k // 2, ns), l1, c_idx, name + "_rs_sum2")
    plan2 = [(f, lambda me, peer: 2 * peer[0] + peer[1], lambda me, peer, j=j: 4 + j) for j, f in enumerate(_CHIP_FLIPS)]
    return buf, plan2


def _rs_finish(buf, name):
    _, kh, ns = buf.shape
    x, y, c = _my_pos()
    sc_idx = jnp.stack([2 * x + y, c]).astype(jnp.int32)
    halves = _sum_chips(buf, sc_idx, name + "_rs_sum4")
    plan3 = [((0, 0, 1), lambda me, peer: me[2], lambda me, peer: me[2])]
    return xchg(halves, 2, plan3, name + "_rs_swap", inplace=True).reshape(2 * kh, ns)


def reduce_scatter_weight(g4, name):
    buf, plan = _rs_begin(g4, name)
    return _rs_finish(xchg(buf, _RS_SLOTS, plan, name + "_rs_ici", inplace=True), name)


def _sum8(a8, name):
    _, r, cc = a8.shape
    tr = _row_tile(r, r, cc)

    def body(a_ref, o_ref):
        acc = a_ref[0]
        for j in range(1, N_DEV):
            acc = acc + a_ref[j]
        o_ref[...] = acc

    return pl.pallas_call(body, grid=(r // tr,), in_specs=[pl.BlockSpec((N_DEV, tr, cc), lambda i: (0, i, 0))],
                          out_specs=pl.BlockSpec((tr, cc), lambda i: (i, 0)), out_shape=jax.ShapeDtypeStruct((r, cc), F32),
                          name=name, compiler_params=_cparams("parallel"))(a8)


def all_reduce8(flat, name):
    n = flat.shape[0]
    unit = N_DEV * 256 * 128
    npad = -(-n // unit) * unit
    a = jnp.pad(flat, (0, npad - n)).reshape(N_DEV, npad // (N_DEV * 128), 128)
    mine = _sum8(all_to_all8(a, name + "_rs"), name + "_sum")
    return all_gather8(mine, name + "_ag").reshape(npad)[:n]


def _make_split(t, cuts):
    def pieces(qkv):
        out = []
        for rows in (slice(None, t), slice(t, None)):
            out += [qkv[rows, a:b] for a, b in zip(cuts[:-1], cuts[1:])]
        return tuple(out)

    @jax.custom_vjp
    def split(qkv):
        return pieces(qkv)

    def fwd(qkv):
        return pieces(qkv), None

    def bwd(_, g):
        n = len(cuts) - 1
        return (jnp.concatenate([jnp.concatenate(g[:n], axis=1), jnp.concatenate(g[n:], axis=1)], axis=0),)

    split.defvjp(fwd, bwd)
    return split


def _local_loss(x, ctx, target, mods, small, big, pending, shards, dims):
    t, m, d = dims["t"], dims["m"], dims["d"]
    a_w, bq_w, bkv_w = dims["a_w"], dims["bq_w"], dims["bkv_w"]
    z = jnp.concatenate([x, ctx], axis=0)
    big = dict(big)

    def grp(layer, j, n_groups=2):
        return mods[layer, :n_groups, j][:, None, :]

    def lin(kind, out_dtype, name, a, wname, gather=()):
        y, gathered = make_linear(kind, out_dtype, name)(a, big[wname], shards[wname], tuple(pending[n] for n in gather))
        big.update(zip(gather, gathered))
        return y

    def ffn(layer, a, gather_in, gather_out):
        n1, n3, n2 = f"ffn_w1_{layer}", f"ffn_w3_{layer}", f"ffn_w2_{layer}"
        gathering = (tuple(pending[n] for n in gather_in), tuple(pending[n] for n in gather_out))
        w2 = big[n2] if n2 in big else pending[n2]
        y, (got_in, got_out) = make_ffn(f"ffn{layer}", n2 not in big)(a, big[n1], big[n3], w2, shards[n1], shards[n3],
                                                                    shards[n2], gathering)
        big.update(zip(gather_in, got_in))
        big.update(zip(gather_out, got_out))
        return y

    h = make_norm_mod(t, "norm_mix0")(z, small["norm_mix"][0][None], grp(0, 1), grp(0, 0))
    qkv = lin("col", BF16, "attn_in", h, "attn_w_in", ("attn_w_out", "ffn_w1_0"))
    o3, o5 = 3 * a_w, 3 * a_w + bq_w + bkv_w
    cuts = (0, a_w, 2 * a_w, o3, o3 + bq_w, o5, o5 + bkv_w)
    qa, ka, va, qb_u, kb_u, vb, qa_c, ka_c, va_c, qb_c, kb_c, vb_c = _make_split(t, cuts)(qkv)
    qb, kb = make_rope(t, "rope_q")(qb_u), make_rope(t, "rope_k")(kb_u)
    sink = small["attn_sink"][0]
    no_sink = jnp.zeros((a_w // HEAD_DIM,), F32)
    na_meta, na_span, na_cases = _na_plan(t)
    oa = make_attention(na_meta, 1, na_span, True, False, True, "na")(
        qa, ka, va, ka_c, va_c, _na_bias(small["attn_rpb"][0], na_cases), no_sink)
    sw_meta, sw_span, sw_bias = _sw_plan(t)
    grp_b = bq_w // bkv_w
    ob = make_attention(sw_meta, grp_b, sw_span, True, True, False, "swa")(qb, kb, vb, kb_c, vb_c, jnp.asarray(sw_bias), sink)
    c_meta = np.array([[0] * (m // ATTN_BLOCK), [0] * (m // ATTN_BLOCK), [1] + [0] * (m // ATTN_BLOCK - 1)], np.int32)
    zero_bias = jnp.zeros((1, 1, ATTN_BLOCK, m), F32)
    oa_c = make_attention(c_meta, 1, m, False, False, False, "ctx_na")(qa_c, ka_c, va_c, ka_c, va_c, zero_bias, no_sink)
    ob_c = make_attention(c_meta, grp_b, m, False, True, False, "ctx_swa")(qb_c, kb_c, vb_c, kb_c, vb_c, zero_bias, sink)
    o = jnp.concatenate([jnp.concatenate([oa, ob], axis=1), jnp.concatenate([oa_c, ob_c], axis=1)], axis=0)
    y = lin("row", F32, "attn_out", o, "attn_w_out", ("ffn_w3_0",))
    z, h = make_residual_norm(t, "mix0_to_ffn0")(z, y, grp(0, 2), small["norm_ffn"][0][None], grp(0, 4), grp(0, 3))
    y = ffn(0, h, ("ssm_w_glu", "ffn_w1_1"), ("ffn_w3_1",))

    z, h = make_residual_norm(t, "ffn0_to_mix1")(z, y, grp(0, 5), small["norm_mix"][1][None], grp(1, 1), grp(1, 0))
    hx, hc = h[:t], h[t:]
    lam, wd, wr = _s5_params(small["ssm_a_re"][0], small["ssm_a_im"][0], small["ssm_log_dt"][0], small["ssm_b_re"][0],
                             small["ssm_b_im"][0], small["ssm_c_re"][0], small["ssm_c_im"][0])
    u2 = jnp.stack([_to_scan_order(jnp.concatenate([hc, hx], axis=0)), _to_scan_order(h)])
    y2 = make_s5_core("s5")(u2, lam, wd, wr)
    ys = _from_scan_order(y2[0])[m:] + _from_scan_order(y2[1])[:t]
    gl = make_gelu_in("gelu")(hx, ys, small["ssm_d_full"][None])
    zz = lin("col", F32, "glu_w", gl, "ssm_w_glu", ("ffn_w2_1",))
    yx = make_glu("glu")(zz, small["ssm_b_glu_full"][None])
    xs, h = make_residual_norm(t, "mix1_to_ffn1")(z[:t], yx, grp(1, 2, 1), small["norm_ffn"][1][None], grp(1, 4, 1), grp(1, 3, 1))
    xs = make_gated_residual(t, "res_ffn1")(xs, ffn(1, h, (), ()), grp(1, 5, 1))
    return make_final_loss("loss_head")(xs, small["norm_final"][None], target)[0, 0]


_WEIGHTS = ['c_ctx', 'ada_w', 'ada_b', 'norm_mix', 'norm_ffn', 'ffn_w1', 'ffn_w3', 'ffn_w2', 'attn_w_in', 'attn_w_out',
            'attn_rpb', 'attn_sink', 'ssm_a_re', 'ssm_a_im', 'ssm_log_dt', 'ssm_b_re', 'ssm_b_im', 'ssm_c_re', 'ssm_c_im',
            'ssm_d', 'ssm_w_glu', 'ssm_b_glu', 'norm_final']
_LOCAL_SMALL = ['norm_mix', 'norm_ffn', 'attn_rpb', 'attn_sink', 'ssm_a_re', 'ssm_a_im', 'ssm_log_dt', 'ssm_b_re',
                'ssm_b_im', 'ssm_c_re', 'ssm_c_im', 'norm_final']
_MOD_ROWS = 16


def _gather_chip_vector(v, name):
    g = all_gather8(v[None], name)
    return g[0::2, 0, :].reshape(-1)


def kernel(x, c, ctx, c_ctx, ada_w, ada_b, norm_mix, norm_ffn, ffn_w1, ffn_w3, ffn_w2, attn_w_in, attn_w_out, attn_rpb, attn_sink, ssm_a_re, ssm_a_im, ssm_log_dt, ssm_b_re, ssm_b_im, ssm_c_re, ssm_c_im, ssm_d, ssm_w_glu, ssm_b_glu, norm_final, loss_target, m_c_ctx, m_ada_w, m_ada_b, m_norm_mix, m_norm_ffn, m_ffn_w1, m_ffn_w3, m_ffn_w2, m_attn_w_in, m_attn_w_out, m_attn_rpb, m_attn_sink, m_ssm_a_re, m_ssm_a_im, m_ssm_log_dt, m_ssm_b_re, m_ssm_b_im, m_ssm_c_re, m_ssm_c_im, m_ssm_d, m_ssm_w_glu, m_ssm_b_glu, m_norm_final, v_c_ctx, v_ada_w, v_ada_b, v_norm_mix, v_norm_ffn, v_ffn_w1, v_ffn_w3, v_ffn_w2, v_attn_w_in, v_attn_w_out, v_attn_rpb, v_attn_sink, v_ssm_a_re, v_ssm_a_im, v_ssm_log_dt, v_ssm_b_re, v_ssm_b_im, v_ssm_c_re, v_ssm_c_im, v_ssm_d, v_ssm_w_glu, v_ssm_b_glu, v_norm_final):
    env = dict(locals())
    w = {n: env[n] for n in _WEIGHTS}
    mom = {n: env["m_" + n] for n in _WEIGHTS}
    var = {n: env["v_" + n] for n in _WEIGHTS}
    _, t, d = x.shape
    m = ctx.shape[1]
    px, py, pc = _my_pos()
    s_me = 2 * px + py
    a_w =attn_rpb.shape[1] * HEAD_DIM
    bq_w = attn_sink.shape[1] * HEAD_DIM
    bkv_w = (4 * attn_w_in.shape[2] - 3 * a_w - bq_w) // 2
    dims = dict(t=t, m=m, d=d, a_w=a_w, bq_w=bq_w, bkv_w=bkv_w)
    n_layers = ada_w.shape[0]
    ada_cols = ada_w.shape[2]

    big = {"attn_w_in": gather_weight(attn_w_in[0], "ag_attn_in")}
    pending = {"attn_w_out": _ag_prepare(attn_w_out[0]), "ssm_w_glu": _ag_prepare(ssm_w_glu[0])}
    for l in range(n_layers):
        pending.update({f"ffn_w1_{l}": _ag_prepare(ffn_w1[l]), f"ffn_w3_{l}": _ag_prepare(ffn_w3[l]),
                        f"ffn_w2_{l}": _ag_prepare(ffn_w2[l])})
    small ={n: w[n] for n in _LOCAL_SMALL}
    small["ssm_d_full"] = _gather_chip_vector(ssm_d[0], "ag_ssm_d")
    small["ssm_b_glu_full"] = _gather_chip_vector(ssm_b_glu[0], "ag_b_glu")

    c_all = all_gather8(c, "ag_c")[:, 0, :]
    cond = jnp.concatenate([c_all, c_ctx[None], jnp.zeros((_MOD_ROWS - N_DEV - 1, d), F32)], axis=0)
    sig = jax.nn.sigmoid(cond)
    silu_c = (cond * sig).astype(BF16)
    mods_shard = mm_nn(silu_c, ada_w, "col", F32, "ada_fwd").reshape(_MOD_ROWS, n_layers, ada_cols).transpose(1, 0, 2)
    send = jnp.stack([jnp.stack([mods_shard[:, tgt], mods_shard[:, N_DEV]], axis=1).reshape(2 * n_layers, ada_cols)
                      for tgt in range(N_DEV)])
    plan = [((0, 0, 0), lambda me, peer: _lin(me), lambda me, peer: 2 * me[0] + me[1])]
    plan += [(f, lambda me, peer: _lin(peer), lambda me, peer: 2 * me[0] + me[1]) for f in _CHIP_FLIPS]
    got = xchg(send, 4, plan, "mods_xchg")
    mods = got.reshape(4, n_layers, 2, ada_cols).transpose(1, 2, 0, 3).reshape(n_layers, 2, 4 * ada_cols)
    mods = (mods + ada_b[:, None, :]).reshape(n_layers, 2, 6, d)

    shards = {"attn_w_in": attn_w_in[0], "attn_w_out": attn_w_out[0], "ssm_w_glu": ssm_w_glu[0]}
    for l in range(n_layers):
        shards.update({f"ffn_w1_{l}": ffn_w1[l], f"ffn_w3_{l}": ffn_w3[l], f"ffn_w2_{l}": ffn_w2[l]})

    def local(xx, mods_, small_, shards_):
        return _local_loss(xx, ctx[0], loss_target[0], mods_, small_, big, pending, shards_, dims)

    loss_local, vjp = jax.vjp(local, x[0], mods, small, shards)
    g_x, g_mods, g_small, g_shards = vjp(jnp.ones((), F32))
    loss = lax.psum(loss_local, ("x", "y", "c"))
    grads = {"attn_w_in": g_shards["attn_w_in"][None], "attn_w_out": g_shards["attn_w_out"][None],
             "ssm_w_glu": g_shards["ssm_w_glu"][None]}
    for n in ("ffn_w1", "ffn_w3", "ffn_w2"):
        grads[n] = jnp.stack([g_shards[f"{n}_{l}"] for l in range(n_layers)])

    gm = all_gather8(g_mods.reshape(2 * n_layers, 6 * d), "ag_dmods").reshape(N_DEV, n_layers, 2, 6 * d)
    ctx_row = gm[0, :, 1]
    for j in range(1, N_DEV):
        ctx_row = ctx_row + gm[j, :, 1]
    dm16 = jnp.concatenate([gm[:, :, 0].transpose(1, 0, 2), ctx_row[:, None], jnp.zeros((n_layers, _MOD_ROWS - N_DEV - 1, 6 * d), F32)], axis=1)
    grads["ada_b"] = jnp.sum(dm16, axis=1)
    dm_mine = lax.dynamic_slice_in_dim(dm16, s_me * ada_cols, ada_cols, axis=2).astype(BF16)
    grads["ada_w"] = jnp.stack([mm_tn(silu_c, dm_mine[l], (1, d, ada_cols), "col", F32, f"ada_dw{l}")[0] for l in range(n_layers)])
    dsilu = mm_nt(dm_mine.transpose(1, 0, 2).reshape(_MOD_ROWS, n_layers * ada_cols), ada_w, "col", F32, "ada_dc")
    dsilu_ctx = 0.5 * dsilu[N_DEV]

    packed = [(n, g_small[n]) for n in _LOCAL_SMALL] + [("ssm_d", g_small["ssm_d_full"]), ("ssm_b_glu", g_small["ssm_b_glu_full"]),
                                                        ("c_ctx", dsilu_ctx)]
    flat = all_reduce8(jnp.concatenate([a.reshape(-1) for _, a in packed]), "ar_small")
    off = 0
    for n, a in packed:
        grads[n] = flat[off:off + a.size].reshape(a.shape)
        off += a.size
    sig_ctx = jax.nn.sigmoid(c_ctx)
    grads["c_ctx"] = grads["c_ctx"] * (sig_ctx * (1.0 + c_ctx * (1.0 - sig_ctx)))
    grads["ssm_d"] = lax.dynamic_slice_in_dim(grads["ssm_d"], s_me * ssm_d.shape[1], ssm_d.shape[1])[None]
    grads["ssm_b_glu"] = lax.dynamic_slice_in_dim(grads["ssm_b_glu"], s_me * ssm_b_glu.shape[1], ssm_b_glu.shape[1])[None]

    delta, new_m, new_v = {}, {}, {}
    for n in _WEIGHTS:
        delta[n], new_m[n], new_v[n] = adamw(w[n], grads[n], mom[n], var[n], "adamw_" + n)
    return (loss, g_x[None], *[grads[n] for n in _WEIGHTS], *[delta[n] for n in _WEIGHTS],
            *[new_m[n] for n in _WEIGHTS], *[new_v[n] for n in _WEIGHTS])
```
